```python
import math
import jax, jax.numpy as jnp
from jax import lax
import numpy as np

D_MODEL = 2048
BATCH = 8
SEQ = 2048
DEPTH = 1

N_MEM = 256
HEAD_DIM = 128
FOX_HEADS = 8
FOX_W = FOX_HEADS * HEAD_DIM
LRU_W = D_MODEL - FOX_W
LRU_BLOCKS = 8
LRU_BLOCK = LRU_W // LRU_BLOCKS
LRU_C = 8.0
CONV_W = 4
MIX_W = FOX_W + LRU_W
IN_W = 3 * FOX_W + FOX_HEADS + 2 * LRU_W
XATT_HEADS = 4
XATT_W = XATT_HEADS * HEAD_DIM
FFN_HIDDEN = int(math.ceil((8 * D_MODEL / 3) / 256) * 256)
Q_BLOCK = 128
RMS_EPS = 1e-6

SPLITS = (FOX_W, 2 * FOX_W, 3 * FOX_W, 3 * FOX_W + FOX_HEADS, 3 * FOX_W + FOX_HEADS + LRU_W)

kernel_name = "hymba_fox_rglru_memxattn_block"


def rmsnorm(x, g):
    xf = x.astype(jnp.float32)
    y = xf * lax.rsqrt(jnp.mean(xf * xf, axis=-1, keepdims=True) + RMS_EPS)
    return (y * g.astype(jnp.float32)).astype(x.dtype)


def forgetting_attention(q, k, v, c):
    B, H, S, dh = q.shape
    nb = S // Q_BLOCK
    scale = 1.0 / math.sqrt(dh)
    qb = q.reshape(B, H, nb, Q_BLOCK, dh).transpose(2, 0, 1, 3, 4)
    cb = c.reshape(B, H, nb, Q_BLOCK).transpose(2, 0, 1, 3)
    k_pos = jnp.arange(S)

    def one_block(args):
        q_i, c_i, i = args
        s = jnp.einsum('bhqd,bhkd->bhqk', q_i, k, preferred_element_type=jnp.float32) * scale
        s = s + c_i[..., None] - c[:, :, None, :]
        q_pos = i * Q_BLOCK + jnp.arange(Q_BLOCK)
        causal = k_pos[None, :] <= q_pos[:, None]
        s = jnp.where(causal, s, -jnp.inf)
        p = jax.nn.softmax(s, axis=-1)
        return jnp.einsum('bhqk,bhkd->bhqd', p.astype(v.dtype), v)

    o = lax.map(one_block, (qb, cb, jnp.arange(nb)))
    return o.transpose(1, 2, 0, 3, 4).reshape(B, H, S, dh)


def causal_depthwise_conv(u, w, b):
    S = u.shape[1]
    up = jnp.pad(u, ((0, 0), (CONV_W - 1, 0), (0, 0)))
    return b + sum(w[j] * up[:, j:j + S] for j in range(CONV_W))


def rg_lru(u, w_ra, b_ra, w_ri, b_ri, lam):
    B, S, W = u.shape
    ub = u.reshape(B, S, LRU_BLOCKS, LRU_BLOCK)
    r = jax.nn.sigmoid(jnp.einsum('bsnc,ncd->bsnd', ub, w_ra).reshape(B, S, W) + b_ra)
    i = jax.nn.sigmoid(jnp.einsum('bsnc,ncd->bsnd', ub, w_ri).reshape(B, S, W) + b_ri)
    log_a = -LRU_C * r.astype(jnp.float32) * jax.nn.softplus(-lam.astype(jnp.float32))
    a = jnp.exp(log_a)
    b_in = jnp.sqrt(-jnp.expm1(2.0 * log_a)) * (i * u).astype(jnp.float32)

    def combine(left, right):
        a1, b1 = left
        a2, b2 = right
        return a1 * a2, a2 * b1 + b2

    _, h = lax.associative_scan(combine, (a, b_in), axis=1)
    return h.astype(u.dtype)


def _fwd_setup_inputs(seed: int = 0) -> dict:
    key = jax.random.key(seed)
    ks = jax.random.split(key, 32)
    f32 = jnp.float32

    def nrm(k, shape, scale):
        return jax.random.normal(k, shape, f32) * scale

    def gain(k, shape):
        return 1.0 + 0.02 * jax.random.normal(k, shape, f32)

    L = DEPTH
    a_c = jax.random.uniform(ks[13], (L, LRU_W), f32, 0.9, 0.999)
    s_lam = a_c ** (1.0 / LRU_C)
    lam = jnp.log(s_lam) - jnp.log1p(-s_lam)
    return {
        "x": nrm(ks[0], (BATCH, SEQ, D_MODEL), 1.0),
        "mem": nrm(ks[1], (BATCH, N_MEM, D_MODEL), 1.0),
        "g_mix": gain(ks[2], (L, D_MODEL)),
        "w_in": nrm(ks[3], (L, D_MODEL, IN_W), D_MODEL ** -0.5),
        "b_f": jax.random.uniform(ks[4], (L, FOX_HEADS), f32, 3.0, 5.0),
        "g_q": gain(ks[5], (L, HEAD_DIM)),
        "g_k": gain(ks[6], (L, HEAD_DIM)),
        "conv_w": nrm(ks[7], (L, CONV_W, LRU_W), CONV_W ** -0.5),
        "conv_b": nrm(ks[8], (L, LRU_W), 0.02),
        "w_ra": nrm(ks[9], (L, LRU_BLOCKS, LRU_BLOCK, LRU_BLOCK), LRU_BLOCK ** -0.5),
        "b_ra": nrm(ks[10], (L, LRU_W), 0.02),
        "w_ri": nrm(ks[11], (L, LRU_BLOCKS, LRU_BLOCK, LRU_BLOCK), LRU_BLOCK ** -0.5),
        "b_ri": nrm(ks[12], (L, LRU_W), 0.02),
        "lam": lam,
        "g_fox_out": gain(ks[14], (L, FOX_W)),
        "g_lru_out": gain(ks[15], (L, LRU_W)),
        "w_out": nrm(ks[16], (L, MIX_W, D_MODEL), MIX_W ** -0.5),
        "g_xattn": gain(ks[17], (L, D_MODEL)),
        "g_mem": gain(ks[18], (L, D_MODEL)),
        "w_cq": nrm(ks[19], (L, D_MODEL, XATT_W), D_MODEL ** -0.5),
        "w_ckv": nrm(ks[20], (L, D_MODEL, 2 * XATT_W), D_MODEL ** -0.5),
        "g_cq": gain(ks[21], (L, HEAD_DIM)),
        "g_ck": gain(ks[22], (L, HEAD_DIM)),
        "w_co": nrm(ks[23], (L, XATT_W, D_MODEL), XATT_W ** -0.5),
        "g_ffn": gain(ks[24], (L, D_MODEL)),
        "w_gate_up": nrm(ks[25], (L, D_MODEL, 2 * FFN_HIDDEN), D_MODEL ** -0.5),
        "w_down": nrm(ks[26], (L, FFN_HIDDEN, D_MODEL), FFN_HIDDEN ** -0.5),
    }


def _fwd_reference(x, mem, g_mix, w_in, b_f, g_q, g_k, conv_w, conv_b, w_ra, b_ra, w_ri, b_ri,
              lam, g_fox_out, g_lru_out, w_out, g_xattn, g_mem, w_cq, w_ckv, g_cq, g_ck,
              w_co, g_ffn, w_gate_up, w_down):
    B, S, _ = x.shape
    M = mem.shape[1]
    for l in range(DEPTH):
        h = rmsnorm(x, g_mix[l])
        proj = h @ w_in[l]
        q, k, v, f_logit, u, gate = jnp.split(proj, SPLITS, axis=-1)
        q = rmsnorm(q.reshape(B, S, FOX_HEADS, HEAD_DIM), g_q[l]).transpose(0, 2, 1, 3)
        k = rmsnorm(k.reshape(B, S, FOX_HEADS, HEAD_DIM), g_k[l]).transpose(0, 2, 1, 3)
        v = v.reshape(B, S, FOX_HEADS, HEAD_DIM).transpose(0, 2, 1, 3)
        log_f = jax.nn.log_sigmoid((f_logit + b_f[l]).astype(jnp.float32))
        c = lax.cumsum(log_f, axis=1).transpose(0, 2, 1)
        o_fox = forgetting_attention(q, k, v, c)
        o_fox = o_fox.transpose(0, 2, 1, 3).reshape(B, S, FOX_W)

        u = causal_depthwise_conv(u, conv_w[l], conv_b[l])
        y_lru = rg_lru(u, w_ra[l], b_ra[l], w_ri[l], b_ri[l], lam[l]) * jax.nn.gelu(gate)

        mix = jnp.concatenate([rmsnorm(o_fox, g_fox_out[l]), rmsnorm(y_lru, g_lru_out[l])], axis=-1)
        x = x + mix @ w_out[l]

        hq = rmsnorm(x, g_xattn[l])
        mn = rmsnorm(mem, g_mem[l])
        cq = rmsnorm((hq @ w_cq[l]).reshape(B, S, XATT_HEADS, HEAD_DIM), g_cq[l])
        ck, cv = jnp.split(mn @ w_ckv[l], 2, axis=-1)
        ck = rmsnorm(ck.reshape(B, M, XATT_HEADS, HEAD_DIM), g_ck[l])
        cv = cv.reshape(B, M, XATT_HEADS, HEAD_DIM)
        s = jnp.einsum('bshd,bmhd->bhsm', cq, ck, preferred_element_type=jnp.float32) / math.sqrt(HEAD_DIM)
        p = jax.nn.softmax(s, axis=-1)
        o_x = jnp.einsum('bhsm,bmhd->bshd', p.astype(cv.dtype), cv).reshape(B, S, XATT_W)
        x = x + o_x @ w_co[l]

        hf = rmsnorm(x, g_ffn[l])
        f_gate, f_up = jnp.split(hf @ w_gate_up[l], 2, axis=-1)
        x = x + (jax.nn.silu(f_gate) * f_up) @ w_down[l]
    return x


import jax as _jax
import jax.numpy as _jnp

TWIN_FORMAT = 'train_step'
FWD_PARAMS = ['x', 'mem', 'g_mix', 'w_in', 'b_f', 'g_q', 'g_k', 'conv_w', 'conv_b', 'w_ra', 'b_ra', 'w_ri', 'b_ri', 'lam', 'g_fox_out', 'g_lru_out', 'w_out', 'g_xattn', 'g_mem', 'w_cq', 'w_ckv', 'g_cq', 'g_ck', 'w_co', 'g_ffn', 'w_gate_up', 'w_down']
TWIN_WEIGHTS = ['g_mix', 'w_in', 'b_f', 'g_q', 'g_k', 'conv_w', 'conv_b', 'w_ra', 'b_ra', 'w_ri', 'b_ri', 'lam', 'g_fox_out', 'g_lru_out', 'w_out', 'g_xattn', 'g_mem', 'w_cq', 'w_ckv', 'g_cq', 'g_ck', 'w_co', 'g_ffn', 'w_gate_up', 'w_down']
TWIN_DIFF_INPUT = 'x'
TWIN_INPUTS = ['x', 'mem', 'g_mix', 'w_in', 'b_f', 'g_q', 'g_k', 'conv_w', 'conv_b', 'w_ra', 'b_ra', 'w_ri', 'b_ri', 'lam', 'g_fox_out', 'g_lru_out', 'w_out', 'g_xattn', 'g_mem', 'w_cq', 'w_ckv', 'g_cq', 'g_ck', 'w_co', 'g_ffn', 'w_gate_up', 'w_down', 'loss_target', 'm_g_mix', 'm_w_in', 'm_b_f', 'm_g_q', 'm_g_k', 'm_conv_w', 'm_conv_b', 'm_w_ra', 'm_b_ra', 'm_w_ri', 'm_b_ri', 'm_lam', 'm_g_fox_out', 'm_g_lru_out', 'm_w_out', 'm_g_xattn', 'm_g_mem', 'm_w_cq', 'm_w_ckv', 'm_g_cq', 'm_g_ck', 'm_w_co', 'm_g_ffn', 'm_w_gate_up', 'm_w_down', 'v_g_mix', 'v_w_in', 'v_b_f', 'v_g_q', 'v_g_k', 'v_conv_w', 'v_conv_b', 'v_w_ra', 'v_b_ra', 'v_w_ri', 'v_b_ri', 'v_lam', 'v_g_fox_out', 'v_g_lru_out', 'v_w_out', 'v_g_xattn', 'v_g_mem', 'v_w_cq', 'v_w_ckv', 'v_g_cq', 'v_g_ck', 'v_w_co', 'v_g_ffn', 'v_w_gate_up', 'v_w_down']
TWIN_OUTPUTS = ['loss', 'grad_x', 'grad_g_mix', 'grad_w_in', 'grad_b_f', 'grad_g_q', 'grad_g_k', 'grad_conv_w', 'grad_conv_b', 'grad_w_ra', 'grad_b_ra', 'grad_w_ri', 'grad_b_ri', 'grad_lam', 'grad_g_fox_out', 'grad_g_lru_out', 'grad_w_out', 'grad_g_xattn', 'grad_g_mem', 'grad_w_cq', 'grad_w_ckv', 'grad_g_cq', 'grad_g_ck', 'grad_w_co', 'grad_g_ffn', 'grad_w_gate_up', 'grad_w_down', 'delta_g_mix', 'delta_w_in', 'delta_b_f', 'delta_g_q', 'delta_g_k', 'delta_conv_w', 'delta_conv_b', 'delta_w_ra', 'delta_b_ra', 'delta_w_ri', 'delta_b_ri', 'delta_lam', 'delta_g_fox_out', 'delta_g_lru_out', 'delta_w_out', 'delta_g_xattn', 'delta_g_mem', 'delta_w_cq', 'delta_w_ckv', 'delta_g_cq', 'delta_g_ck', 'delta_w_co', 'delta_g_ffn', 'delta_w_gate_up', 'delta_w_down', 'new_m_g_mix', 'new_m_w_in', 'new_m_b_f', 'new_m_g_q', 'new_m_g_k', 'new_m_conv_w', 'new_m_conv_b', 'new_m_w_ra', 'new_m_b_ra', 'new_m_w_ri', 'new_m_b_ri', 'new_m_lam', 'new_m_g_fox_out', 'new_m_g_lru_out', 'new_m_w_out', 'new_m_g_xattn', 'new_m_g_mem', 'new_m_w_cq', 'new_m_w_ckv', 'new_m_g_cq', 'new_m_g_ck', 'new_m_w_co', 'new_m_g_ffn', 'new_m_w_gate_up', 'new_m_w_down', 'new_v_g_mix', 'new_v_w_in', 'new_v_b_f', 'new_v_g_q', 'new_v_g_k', 'new_v_conv_w', 'new_v_conv_b', 'new_v_w_ra', 'new_v_b_ra', 'new_v_w_ri', 'new_v_b_ri', 'new_v_lam', 'new_v_g_fox_out', 'new_v_g_lru_out', 'new_v_w_out', 'new_v_g_xattn', 'new_v_g_mem', 'new_v_w_cq', 'new_v_w_ckv', 'new_v_g_cq', 'new_v_g_ck', 'new_v_w_co', 'new_v_g_ffn', 'new_v_w_gate_up', 'new_v_w_down']
TWIN_LEAF_KINDS = {'loss': 'loss', 'grad_x': 'grad_x', 'grad_g_mix': 'grad_w', 'grad_w_in': 'grad_w', 'grad_b_f': 'grad_w', 'grad_g_q': 'grad_w', 'grad_g_k': 'grad_w', 'grad_conv_w': 'grad_w', 'grad_conv_b': 'grad_w', 'grad_w_ra': 'grad_w', 'grad_b_ra': 'grad_w', 'grad_w_ri': 'grad_w', 'grad_b_ri': 'grad_w', 'grad_lam': 'grad_w', 'grad_g_fox_out': 'grad_w', 'grad_g_lru_out': 'grad_w', 'grad_w_out': 'grad_w', 'grad_g_xattn': 'grad_w', 'grad_g_mem': 'grad_w', 'grad_w_cq': 'grad_w', 'grad_w_ckv': 'grad_w', 'grad_g_cq': 'grad_w', 'grad_g_ck': 'grad_w', 'grad_w_co': 'grad_w', 'grad_g_ffn': 'grad_w', 'grad_w_gate_up': 'grad_w', 'grad_w_down': 'grad_w', 'delta_g_mix': 'delta_w', 'delta_w_in': 'delta_w', 'delta_b_f': 'delta_w', 'delta_g_q': 'delta_w', 'delta_g_k': 'delta_w', 'delta_conv_w': 'delta_w', 'delta_conv_b': 'delta_w', 'delta_w_ra': 'delta_w', 'delta_b_ra': 'delta_w', 'delta_w_ri': 'delta_w', 'delta_b_ri': 'delta_w', 'delta_lam': 'delta_w', 'delta_g_fox_out': 'delta_w', 'delta_g_lru_out': 'delta_w', 'delta_w_out': 'delta_w', 'delta_g_xattn': 'delta_w', 'delta_g_mem': 'delta_w', 'delta_w_cq': 'delta_w', 'delta_w_ckv': 'delta_w', 'delta_g_cq': 'delta_w', 'delta_g_ck': 'delta_w', 'delta_w_co': 'delta_w', 'delta_g_ffn': 'delta_w', 'delta_w_gate_up': 'delta_w', 'delta_w_down': 'delta_w', 'new_m_g_mix': 'new_m', 'new_m_w_in': 'new_m', 'new_m_b_f': 'new_m', 'new_m_g_q': 'new_m', 'new_m_g_k': 'new_m', 'new_m_conv_w': 'new_m', 'new_m_conv_b': 'new_m', 'new_m_w_ra': 'new_m', 'new_m_b_ra': 'new_m', 'new_m_w_ri': 'new_m', 'new_m_b_ri': 'new_m', 'new_m_lam': 'new_m', 'new_m_g_fox_out': 'new_m', 'new_m_g_lru_out': 'new_m', 'new_m_w_out': 'new_m', 'new_m_g_xattn': 'new_m', 'new_m_g_mem': 'new_m', 'new_m_w_cq': 'new_m', 'new_m_w_ckv': 'new_m', 'new_m_g_cq': 'new_m', 'new_m_g_ck': 'new_m', 'new_m_w_co': 'new_m', 'new_m_g_ffn': 'new_m', 'new_m_w_gate_up': 'new_m', 'new_m_w_down': 'new_m', 'new_v_g_mix': 'new_v', 'new_v_w_in': 'new_v', 'new_v_b_f': 'new_v', 'new_v_g_q': 'new_v', 'new_v_g_k': 'new_v', 'new_v_conv_w': 'new_v', 'new_v_conv_b': 'new_v', 'new_v_w_ra': 'new_v', 'new_v_b_ra': 'new_v', 'new_v_w_ri': 'new_v', 'new_v_b_ri': 'new_v', 'new_v_lam': 'new_v', 'new_v_g_fox_out': 'new_v', 'new_v_g_lru_out': 'new_v', 'new_v_w_out': 'new_v', 'new_v_g_xattn': 'new_v', 'new_v_g_mem': 'new_v', 'new_v_w_cq': 'new_v', 'new_v_w_ckv': 'new_v', 'new_v_g_cq': 'new_v', 'new_v_g_ck': 'new_v', 'new_v_w_co': 'new_v', 'new_v_g_ffn': 'new_v', 'new_v_w_gate_up': 'new_v', 'new_v_w_down': 'new_v'}


def _forward(args):
    return _fwd_reference(*[args[k] for k in FWD_PARAMS])


def _output_shape():
    out = _jax.eval_shape(lambda: _forward(_fwd_setup_inputs(0)))
    return out.shape, out.dtype

N_MICROBATCH = 1
ADAM_LR = 0.001
ADAM_B1 = 0.9
ADAM_B2 = 0.999
ADAM_EPS = 1e-08
ADAM_WD = 0.01
ADAM_STEP = 10
PER_EXAMPLE_BATCH_AXIS = {'x': 0, 'mem': 0, 'loss_target': 0}
SHARED_INPUTS = []
_WEIGHT_DTYPES = {'g_mix': _jnp.float32, 'w_in': _jnp.float32, 'b_f': _jnp.float32, 'g_q': _jnp.float32, 'g_k': _jnp.float32, 'conv_w': _jnp.float32, 'conv_b': _jnp.float32, 'w_ra': _jnp.float32, 'b_ra': _jnp.float32, 'w_ri': _jnp.float32, 'b_ri': _jnp.float32, 'lam': _jnp.float32, 'g_fox_out': _jnp.float32, 'g_lru_out': _jnp.float32, 'w_out': _jnp.float32, 'g_xattn': _jnp.float32, 'g_mem': _jnp.float32, 'w_cq': _jnp.float32, 'w_ckv': _jnp.float32, 'g_cq': _jnp.float32, 'g_ck': _jnp.float32, 'w_co': _jnp.float32, 'g_ffn': _jnp.float32, 'w_gate_up': _jnp.float32, 'w_down': _jnp.float32}
MOMENT_SCALE = {'g_mix': 2.733760e-01, 'w_in': 1.631972e-01, 'b_f': 1.494960e+00, 'g_q': 3.775431e-01, 'g_k': 3.803392e-01, 'conv_w': 3.861906e-01, 'conv_b': 6.601162e+00, 'w_ra': 1.565011e-01, 'b_ra': 1.096262e-01, 'w_ri': 2.935087e-01, 'b_ri': 1.777942e-01, 'lam': 1.745134e-01, 'g_fox_out': 8.462323e+00, 'g_lru_out': 1.233065e+01, 'w_out': 3.291713e-01, 'g_xattn': 1.924906e-02, 'g_mem': 1.024869e-01, 'w_cq': 3.786335e-02, 'w_ckv': 8.538319e-02, 'g_cq': 1.181904e+00, 'g_ck': 1.182923e+00, 'w_co': 4.920185e-02, 'g_ffn': 6.204979e+00, 'w_gate_up': 7.774325e-02, 'w_down': 9.450099e-02}


def _to_microbatches(a, axis):
    t = _jnp.moveaxis(a, axis, 0)
    t = t.reshape((N_MICROBATCH, t.shape[0] // N_MICROBATCH) + t.shape[1:])
    return _jnp.moveaxis(t, 1, axis + 1)


def setup_inputs(seed: int = 0) -> dict:
    inp = _fwd_setup_inputs(seed)
    key = _jax.random.fold_in(_jax.random.key(seed), 7919)
    shape, _ = _output_shape()
    out = dict(inp)
    out["loss_target"] = _jax.random.normal(_jax.random.fold_in(key, 0), shape, _jnp.float32)
    for i, name in enumerate(TWIN_WEIGHTS):
        w = inp[name].astype(_jnp.float32)
        if MOMENT_SCALE is None:
            s = _jnp.sqrt(_jnp.mean(_jnp.square(w)) + 1e-30)
        else:
            s = MOMENT_SCALE[name]
        km, kv = _jax.random.split(_jax.random.fold_in(key, i + 1))
        out[name] = w
        out["m_" + name] = s * _jax.random.normal(km, w.shape, _jnp.float32)
        out["v_" + name] = (s * s) * _jax.random.uniform(kv, w.shape, _jnp.float32, 0.5, 1.5)
    if N_MICROBATCH > 1:
        for name, axis in PER_EXAMPLE_BATCH_AXIS.items():
            out[name] = _to_microbatches(out[name], axis)
    return {'x': out['x'], 'mem': out['mem'], 'g_mix': out['g_mix'], 'w_in': out['w_in'], 'b_f': out['b_f'], 'g_q': out['g_q'], 'g_k': out['g_k'], 'conv_w': out['conv_w'], 'conv_b': out['conv_b'], 'w_ra': out['w_ra'], 'b_ra': out['b_ra'], 'w_ri': out['w_ri'], 'b_ri': out['b_ri'], 'lam': out['lam'], 'g_fox_out': out['g_fox_out'], 'g_lru_out': out['g_lru_out'], 'w_out': out['w_out'], 'g_xattn': out['g_xattn'], 'g_mem': out['g_mem'], 'w_cq': out['w_cq'], 'w_ckv': out['w_ckv'], 'g_cq': out['g_cq'], 'g_ck': out['g_ck'], 'w_co': out['w_co'], 'g_ffn': out['g_ffn'], 'w_gate_up': out['w_gate_up'], 'w_down': out['w_down'], 'loss_target': out['loss_target'], 'm_g_mix': out['m_g_mix'], 'm_w_in': out['m_w_in'], 'm_b_f': out['m_b_f'], 'm_g_q': out['m_g_q'], 'm_g_k': out['m_g_k'], 'm_conv_w': out['m_conv_w'], 'm_conv_b': out['m_conv_b'], 'm_w_ra': out['m_w_ra'], 'm_b_ra': out['m_b_ra'], 'm_w_ri': out['m_w_ri'], 'm_b_ri': out['m_b_ri'], 'm_lam': out['m_lam'], 'm_g_fox_out': out['m_g_fox_out'], 'm_g_lru_out': out['m_g_lru_out'], 'm_w_out': out['m_w_out'], 'm_g_xattn': out['m_g_xattn'], 'm_g_mem': out['m_g_mem'], 'm_w_cq': out['m_w_cq'], 'm_w_ckv': out['m_w_ckv'], 'm_g_cq': out['m_g_cq'], 'm_g_ck': out['m_g_ck'], 'm_w_co': out['m_w_co'], 'm_g_ffn': out['m_g_ffn'], 'm_w_gate_up': out['m_w_gate_up'], 'm_w_down': out['m_w_down'], 'v_g_mix': out['v_g_mix'], 'v_w_in': out['v_w_in'], 'v_b_f': out['v_b_f'], 'v_g_q': out['v_g_q'], 'v_g_k': out['v_g_k'], 'v_conv_w': out['v_conv_w'], 'v_conv_b': out['v_conv_b'], 'v_w_ra': out['v_w_ra'], 'v_b_ra': out['v_b_ra'], 'v_w_ri': out['v_w_ri'], 'v_b_ri': out['v_b_ri'], 'v_lam': out['v_lam'], 'v_g_fox_out': out['v_g_fox_out'], 'v_g_lru_out': out['v_g_lru_out'], 'v_w_out': out['v_w_out'], 'v_g_xattn': out['v_g_xattn'], 'v_g_mem': out['v_g_mem'], 'v_w_cq': out['v_w_cq'], 'v_w_ckv': out['v_w_ckv'], 'v_g_cq': out['v_g_cq'], 'v_g_ck': out['v_g_ck'], 'v_w_co': out['v_w_co'], 'v_g_ffn': out['v_g_ffn'], 'v_w_gate_up': out['v_w_gate_up'], 'v_w_down': out['v_w_down']}


def _loss(weights, diff, rest, loss_target):
    with _jax.named_scope("forward"):
        args = {**rest, TWIN_DIFF_INPUT: diff, **{k: w.astype(_WEIGHT_DTYPES[k]) for k, w in weights.items()}}
        y = _forward(args)
    with _jax.named_scope("loss_head"):
        err = _jnp.square(y.astype(_jnp.float32) - loss_target)
        return 0.5 * _jnp.sum(_jnp.mean(err, axis=-1)) if err.ndim else 0.5 * err


def _adamw(w, g, m, v):
    m = ADAM_B1 * m + (1.0 - ADAM_B1) * g
    v = ADAM_B2 * v + (1.0 - ADAM_B2) * _jnp.square(g)
    m_hat = m / (1.0 - ADAM_B1 ** ADAM_STEP)
    v_hat = v / (1.0 - ADAM_B2 ** ADAM_STEP)
    delta = -ADAM_LR * (m_hat / (_jnp.sqrt(v_hat) + ADAM_EPS) + ADAM_WD * w)
    return delta, m, v


def reference(x, mem, g_mix, w_in, b_f, g_q, g_k, conv_w, conv_b, w_ra, b_ra, w_ri, b_ri, lam, g_fox_out, g_lru_out, w_out, g_xattn, g_mem, w_cq, w_ckv, g_cq, g_ck, w_co, g_ffn, w_gate_up, w_down, loss_target, m_g_mix, m_w_in, m_b_f, m_g_q, m_g_k, m_conv_w, m_conv_b, m_w_ra, m_b_ra, m_w_ri, m_b_ri, m_lam, m_g_fox_out, m_g_lru_out, m_w_out, m_g_xattn, m_g_mem, m_w_cq, m_w_ckv, m_g_cq, m_g_ck, m_w_co, m_g_ffn, m_w_gate_up, m_w_down, v_g_mix, v_w_in, v_b_f, v_g_q, v_g_k, v_conv_w, v_conv_b, v_w_ra, v_b_ra, v_w_ri, v_b_ri, v_lam, v_g_fox_out, v_g_lru_out, v_w_out, v_g_xattn, v_g_mem, v_w_cq, v_w_ckv, v_g_cq, v_g_ck, v_w_co, v_g_ffn, v_w_gate_up, v_w_down):
    given = dict(x=x, mem=mem, g_mix=g_mix, w_in=w_in, b_f=b_f, g_q=g_q, g_k=g_k, conv_w=conv_w, conv_b=conv_b, w_ra=w_ra, b_ra=b_ra, w_ri=w_ri, b_ri=b_ri, lam=lam, g_fox_out=g_fox_out, g_lru_out=g_lru_out, w_out=w_out, g_xattn=g_xattn, g_mem=g_mem, w_cq=w_cq, w_ckv=w_ckv, g_cq=g_cq, g_ck=g_ck, w_co=w_co, g_ffn=g_ffn, w_gate_up=w_gate_up, w_down=w_down, loss_target=loss_target, m_g_mix=m_g_mix, m_w_in=m_w_in, m_b_f=m_b_f, m_g_q=m_g_q, m_g_k=m_g_k, m_conv_w=m_conv_w, m_conv_b=m_conv_b, m_w_ra=m_w_ra, m_b_ra=m_b_ra, m_w_ri=m_w_ri, m_b_ri=m_b_ri, m_lam=m_lam, m_g_fox_out=m_g_fox_out, m_g_lru_out=m_g_lru_out, m_w_out=m_w_out, m_g_xattn=m_g_xattn, m_g_mem=m_g_mem, m_w_cq=m_w_cq, m_w_ckv=m_w_ckv, m_g_cq=m_g_cq, m_g_ck=m_g_ck, m_w_co=m_w_co, m_g_ffn=m_g_ffn, m_w_gate_up=m_w_gate_up, m_w_down=m_w_down, v_g_mix=v_g_mix, v_w_in=v_w_in, v_b_f=v_b_f, v_g_q=v_g_q, v_g_k=v_g_k, v_conv_w=v_conv_w, v_conv_b=v_conv_b, v_w_ra=v_w_ra, v_b_ra=v_b_ra, v_w_ri=v_w_ri, v_b_ri=v_b_ri, v_lam=v_lam, v_g_fox_out=v_g_fox_out, v_g_lru_out=v_g_lru_out, v_w_out=v_w_out, v_g_xattn=v_g_xattn, v_g_mem=v_g_mem, v_w_cq=v_w_cq, v_w_ckv=v_w_ckv, v_g_cq=v_g_cq, v_g_ck=v_g_ck, v_w_co=v_w_co, v_g_ffn=v_g_ffn, v_w_gate_up=v_w_gate_up, v_w_down=v_w_down)
    weights = {n: given[n] for n in TWIN_WEIGHTS}
    shared = {n: given[n] for n in SHARED_INPUTS}
    per_example = {n: given[n] for n in ['x', 'mem']}
    grad_fn = _jax.value_and_grad(_loss, argnums=(0, 1))

    def one_microbatch(ex, loss_target):
        ex = dict(ex)
        diff = ex.pop(TWIN_DIFF_INPUT)
        return grad_fn(weights, diff, {**shared, **ex}, loss_target)

    if N_MICROBATCH == 1:
        loss, (grad_w, grad_x) = one_microbatch(per_example, given["loss_target"])
    else:
        def body(carry, xs):
            loss_sum, grad_sum = carry
            l_k, (gw_k, gx_k) = one_microbatch(xs[0], xs[1])
            with _jax.named_scope("update"):
                return (loss_sum + l_k, _jax.tree.map(_jnp.add, grad_sum, gw_k)), gx_k

        init = (_jnp.zeros((), _jnp.float32), _jax.tree.map(_jnp.zeros_like, weights))
        (loss, grad_w), grad_x = _jax.lax.scan(body, init, (per_example, given["loss_target"]))
    with _jax.named_scope("update"):
        delta_w, new_m, new_v = {}, {}, {}
        for n in TWIN_WEIGHTS:
            delta_w[n], new_m[n], new_v[n] = _adamw(weights[n], grad_w[n], given["m_" + n], given["v_" + n])
    return (loss, grad_x, *[grad_w[n] for n in TWIN_WEIGHTS], *[delta_w[n] for n in TWIN_WEIGHTS],
            *[new_m[n] for n in TWIN_WEIGHTS], *[new_v[n] for n in TWIN_WEIGHTS])
```

```python
import functools
import math

import jax
import jax.numpy as jnp
from jax import lax
from jax.experimental import pallas as pl
from jax.experimental.pallas import tpu as pltpu

F32 = jnp.float32
BF16 = jnp.bfloat16
HEAD_DIM = 128
LANES = 128
LRU_C = 8.0
RMS_EPS = 1e-6
CONV_W = 4
ADAM_LR = 0.001
ADAM_B1 = 0.9
ADAM_B2 = 0.999
ADAM_EPS = 1e-08
ADAM_WD = 0.01
ADAM_STEP = 10
VMEM_LIMIT = 56 * 1024 * 1024
N_CHIPS = 4
MESH = pl.DeviceIdType.MESH
ANY = pl.BlockSpec(memory_space=pl.ANY)

WEIGHTS = ['g_mix', 'w_in', 'b_f', 'g_q', 'g_k', 'conv_w', 'conv_b', 'w_ra', 'b_ra', 'w_ri', 'b_ri', 'lam',
           'g_fox_out', 'g_lru_out', 'w_out', 'g_xattn', 'g_mem', 'w_cq', 'w_ckv', 'g_cq', 'g_ck', 'w_co', 'g_ffn',
           'w_gate_up', 'w_down']
BIG = ['w_in', 'w_out', 'w_cq', 'w_ckv', 'w_co', 'w_gate_up', 'w_down']
SMALL = [n for n in WEIGHTS if n not in BIG]


def _params(sem=None):
    if sem is None:
        return pltpu.CompilerParams(vmem_limit_bytes=VMEM_LIMIT)
    return pltpu.CompilerParams(dimension_semantics=sem, vmem_limit_bytes=VMEM_LIMIT)


def _tile(n, cands):
    for t in cands:
        if n % t == 0:
            return t
    return n


def _sigmoid(z):
    return 1.0 / (1.0 + jnp.exp(-z))


def _softplus(z):
    return jnp.maximum(z, 0.0) + jnp.log(1.0 + jnp.exp(-jnp.abs(z)))


def _neg_expm1(z):
    series = -z * (1.0 + z * (0.5 + z * (1.0 / 6.0 + z * (1.0 / 24.0 + z * (1.0 / 120.0)))))
    return jnp.where(z > -0.25, series, 1.0 - jnp.exp(z))


_GELU_K = math.sqrt(2.0 / math.pi)


def _gelu_and_grad(z):
    inner = _GELU_K * (z + 0.044715 * z * z * z)
    t = jnp.tanh(inner)
    g = 0.5 * z * (1.0 + t)
    dg = 0.5 * (1.0 + t) + 0.5 * z * (1.0 - t * t) * _GELU_K * (1.0 + 3.0 * 0.044715 * z * z)
    return g, dg


def _rms(xv, g):
    r = lax.rsqrt(jnp.mean(xv * xv, axis=-1, keepdims=True) + RMS_EPS)
    return xv * r * g


def _rms_bwd(xv, g, dy):
    r = lax.rsqrt(jnp.mean(xv * xv, axis=-1, keepdims=True) + RMS_EPS)
    xh = xv * r
    dyg = dy * g
    dx = r * (dyg - xh * jnp.mean(dyg * xh, axis=-1, keepdims=True))
    return dx, jnp.sum(dy * xh, axis=0, keepdims=True)


def _heads(fn, n_heads, *arrs):
    outs = [fn(*[a[:, h * HEAD_DIM:(h + 1) * HEAD_DIM] for a in arrs]) for h in range(n_heads)]
    first = jnp.concatenate([o[0] for o in outs], axis=1) if n_heads > 1 else outs[0][0]
    rest = [functools.reduce(lambda p, q: p + q, [o[i] for o in outs]) for i in range(1, len(outs[0]))]
    return (first, *rest)


def _split3(v):
    hi = v.astype(BF16)
    r1 = v - hi.astype(F32)
    mid = r1.astype(BF16)
    lo = (r1 - mid.astype(F32)).astype(BF16)
    return hi, mid, lo


def _acc_out(ref, first, val):
    @pl.when(first)
    def _():
        ref[...] = val

    @pl.when(jnp.logical_not(first))
    def _():
        ref[...] += val


_DIMS = {'nn': (((1,), (0,)), ((), ())), 'nt': (((1,), (1,)), ((), ())), 'tn': (((0,), (0,)), ((), ()))}


def _mm_call(name, a, b, mode, grid, a_spec, b_spec, o_spec, o_shape, o_dtype, acc_shape, res=None):
    nk = grid[2]
    dn = _DIMS[mode]

    def body(*refs):
        if res is None:
            a_ref, b_ref, o_ref, acc = refs
            r_ref = None
        else:
            a_ref, b_ref, r_ref, o_ref, acc = refs
        k = pl.program_id(2)

        @pl.when(k == 0)
        def _():
            acc[...] = jnp.zeros_like(acc)

        acc[...] += lax.dot_general(a_ref[...].astype(BF16), b_ref[...].astype(BF16), dn,
                                    preferred_element_type=F32)

        @pl.when(k == nk - 1)
        def _():
            r = acc[...]
            if r_ref is not None:
                r = r + r_ref[...]
            o_ref[...] = r.astype(o_dtype)

    ins = [a, b] + ([] if res is None else [res])
    specs = [a_spec, b_spec] + ([] if res is None else [o_spec])
    return pl.pallas_call(
        body, name=name, grid=grid, in_specs=specs, out_specs=o_spec,
        out_shape=jax.ShapeDtypeStruct(o_shape, o_dtype), scratch_shapes=[pltpu.VMEM(acc_shape, F32)],
        compiler_params=_params(('parallel', 'parallel', 'arbitrary')))(*ins)


def _mm(name, a, b, mode, o_dtype, res=None):
    if mode == 'tn':
        K, M = a.shape
    else:
        M, K = a.shape
    N = b.shape[0] if mode == 'nt' else b.shape[1]
    tm = _tile(M, (1024, 512, 256, 128))
    tn = _tile(N, (1024, 512, 256, 128))
    tk = _tile(K, (512, 256, 128))
    a_spec = (pl.BlockSpec((tk, tm), lambda m, n, k: (k, m)) if mode == 'tn'
              else pl.BlockSpec((tm, tk), lambda m, n, k: (m, k)))
    b_spec = (pl.BlockSpec((tn, tk), lambda m, n, k: (n, k)) if mode == 'nt'
              else pl.BlockSpec((tk, tn), lambda m, n, k: (k, n)))
    o_spec = pl.BlockSpec((tm, tn), lambda m, n, k: (m, n))
    return _mm_call(name, a, b, mode, (M // tm, N // tn, K // tk), a_spec, b_spec, o_spec, (M, N), o_dtype,
                    (tm, tn), res)


def _mm_colsharded(name, a, w, o_dtype, res=None):
    M, K = a.shape
    J, _, Nj = w.shape
    tm = _tile(M, (1024, 512, 256, 128))
    tn = _tile(Nj, (1408, 1024, 512, 256, 128))
    tk = _tile(K, (512, 256, 128))
    per = Nj // tn
    return _mm_call(name, a, w, 'nn', (M // tm, J * per, K // tk),
                    pl.BlockSpec((tm, tk), lambda m, n, k: (m, k)),
                    pl.BlockSpec((None, tk, tn), lambda m, n, k: (n // per, k, n % per)),
                    pl.BlockSpec((tm, tn), lambda m, n, k: (m, n)), (M, J * Nj), o_dtype, (tm, tn), res)


def _mm_colsharded_t(name, a, w, o_dtype):
    M = a.shape[0]
    J, K, Nj = w.shape
    tm = _tile(M, (1024, 512, 256, 128))
    tn = _tile(K, (1024, 512, 256, 128))
    tk = _tile(Nj, (1408, 1024, 512, 256, 128))
    per = Nj // tk
    return _mm_call(name, a, w, 'nt', (M // tm, K // tn, J * per),
                    pl.BlockSpec((tm, tk), lambda m, n, k: (m, k)),
                    pl.BlockSpec((None, tn, tk), lambda m, n, k: (k // per, n, k % per)),
                    pl.BlockSpec((tm, tn), lambda m, n, k: (m, n)), (M, K), o_dtype, (tm, tn))


def _mm_grad_colsharded(name, a, dy, J):
    S, M = a.shape
    Nj = dy.shape[1] // J
    tm = _tile(M, (1024, 512, 256, 128))
    tn = _tile(Nj, (1408, 1024, 512, 256, 128))
    tk = _tile(S, (512, 256, 128))
    per = Nj // tn
    return _mm_call(name, a, dy, 'tn', (M // tm, J * per, S // tk),
                    pl.BlockSpec((tk, tm), lambda m, n, k: (k, m)),
                    pl.BlockSpec((tk, tn), lambda m, n, k: (k, n)),
                    pl.BlockSpec((None, tm, tn), lambda m, n, k: (n // per, m, n % per)), (J, M, Nj), F32, (tm, tn))


def _rows_call(name, body, n_rows, tr, ins, outs):
    return pl.pallas_call(
        body, name=name, grid=(n_rows // tr,), in_specs=[s for _, s in ins], out_specs=[s for _, _, s in outs],
        out_shape=[jax.ShapeDtypeStruct(sh, dt) for sh, dt, _ in outs],
        compiler_params=_params(('arbitrary',)))(*[a for a, _ in ins])


def _rb(tr, w, cb=0):
    return pl.BlockSpec((tr, w), lambda i: (i, cb))


def _fb(shape):
    nd = len(shape)
    return pl.BlockSpec(shape, lambda i: (0,) * nd)


def norm_fwd(name, xv, g):
    S, D = xv.shape
    tr = _tile(S, (256, 128))

    def body(x_ref, g_ref, o_ref):
        o_ref[...] = _rms(x_ref[...], g_ref[...]).astype(BF16)

    return _rows_call(name, body, S, tr, [(xv, _rb(tr, D)), (g, _fb((1, D)))], [((S, D), BF16, _rb(tr, D))])[0]


def norm_bwd(name, xv, g, dy, res=None, want_dx=True):
    S, D = xv.shape
    tr = _tile(S, (256, 128))

    def body(*refs):
        if res is None:
            x_ref, g_ref, dy_ref = refs[:3]
            outs = refs[3:]
            r_ref = None
        else:
            x_ref, g_ref, dy_ref, r_ref = refs[:4]
            outs = refs[4:]
        dx, dg = _rms_bwd(x_ref[...], g_ref[...], dy_ref[...])
        if r_ref is not None:
            dx = dx + r_ref[...]
        if want_dx:
            outs[0][...] = dx
            outs[1][...] = dx.astype(BF16)
        _acc_out(outs[-1], pl.program_id(0) == 0, dg)

    ins = [(xv, _rb(tr, D)), (g, _fb((1, D))), (dy, _rb(tr, D))] + ([] if res is None else [(res, _rb(tr, D))])
    outs = ([((S, D), F32, _rb(tr, D)), ((S, D), BF16, _rb(tr, D))] if want_dx else []) + [((1, D), F32, _fb((1, D)))]
    return _rows_call(name, body, S, tr, ins, outs)


def qkv_fwd(proj, g_q, g_k, FW):
    S = proj.shape[0]
    H = FW // HEAD_DIM
    tr = _tile(S, (256, 128))

    def body(q_ref, k_ref, v_ref, gq_ref, gk_ref, qo, ko, vo):
        qo[...] = _heads(lambda t: (_rms(t, gq_ref[...]),), H, q_ref[...])[0].astype(BF16)
        ko[...] = _heads(lambda t: (_rms(t, gk_ref[...]),), H, k_ref[...])[0].astype(BF16)
        vo[...] = v_ref[...].astype(BF16)

    o = ((S, FW), BF16, _rb(tr, FW))
    return _rows_call('qkv_fwd', body, S, tr,
                      [(proj, _rb(tr, FW, 0)), (proj, _rb(tr, FW, 1)), (proj, _rb(tr, FW, 2)),
                       (g_q, _fb((1, HEAD_DIM))), (g_k, _fb((1, HEAD_DIM)))], [o, o, o])


def qkv_bwd(proj, g_q, g_k, dqn, dkn, FW):
    S = proj.shape[0]
    H = FW // HEAD_DIM
    tr = _tile(S, (256, 128))

    def body(q_ref, k_ref, gq_ref, gk_ref, dq_ref, dk_ref, dqo, dko, dgq, dgk):
        dq, gq = _heads(lambda t, d: _rms_bwd(t, gq_ref[...], d), H, q_ref[...], dq_ref[...])
        dk, gk = _heads(lambda t, d: _rms_bwd(t, gk_ref[...], d), H, k_ref[...], dk_ref[...])
        dqo[...] = dq.astype(BF16)
        dko[...] = dk.astype(BF16)
        first = pl.program_id(0) == 0
        _acc_out(dgq, first, gq)
        _acc_out(dgk, first, gk)

    o = ((S, FW), BF16, _rb(tr, FW))
    og = ((1, HEAD_DIM), F32, _fb((1, HEAD_DIM)))
    return _rows_call('qkv_bwd', body, S, tr,
                      [(proj, _rb(tr, FW, 0)), (proj, _rb(tr, FW, 1)), (g_q, _fb((1, HEAD_DIM))),
                       (g_k, _fb((1, HEAD_DIM))), (dqn, _rb(tr, FW)), (dkn, _rb(tr, FW))], [o, o, og, og])


def _tri(n, upper):
    r = lax.broadcasted_iota(jnp.int32, (n, n), 0)
    c = lax.broadcasted_iota(jnp.int32, (n, n), 1)
    return jnp.where((c >= r) if upper else (c <= r), 1.0, 0.0).astype(BF16)


def _blocked_cumsum(val, S, blk, reverse):
    tri = _tri(blk, reverse)
    order = range(S // blk - 1, -1, -1) if reverse else range(S // blk)
    carry = jnp.zeros((1, LANES), F32)
    outs = {}
    for bi in order:
        part = val[bi * blk:(bi + 1) * blk]
        acc = carry
        for piece in _split3(part):
            acc = acc + jnp.dot(tri, piece, preferred_element_type=F32)
        outs[bi] = acc
        carry = carry + jnp.sum(part, axis=0, keepdims=True)
    return jnp.concatenate([outs[bi] for bi in range(S // blk)], axis=0)


def fgate_fwd(f_raw, b_f_pad):
    S = f_raw.shape[0]
    blk = _tile(S, (256, 128))

    def body(f_ref, b_ref, c_ref):
        z = f_ref[...] + b_ref[...]
        c_ref[...] = _blocked_cumsum(-_softplus(-z), S, blk, False)

    return pl.pallas_call(body, name='fgate_fwd', grid=(1,), in_specs=[_fb((S, LANES)), _fb((1, LANES))],
                          out_specs=_fb((S, LANES)), out_shape=jax.ShapeDtypeStruct((S, LANES), F32),
                          compiler_params=_params(('arbitrary',)))(f_raw, b_f_pad)


def fgate_bwd(f_raw, b_f_pad, dc, H):
    S = f_raw.shape[0]
    blk = _tile(S, (256, 128))

    def body(f_ref, b_ref, dc_ref, df_ref, db_ref):
        z = f_ref[...] + b_ref[...]
        dlogf = _blocked_cumsum(dc_ref[...], S, blk, True)
        lane = lax.broadcasted_iota(jnp.int32, (S, LANES), 1)
        df = jnp.where(lane < H, dlogf * _sigmoid(-z), 0.0)
        df_ref[...] = df.astype(BF16)
        db_ref[...] = jnp.sum(df, axis=0, keepdims=True)

    return pl.pallas_call(body, name='fgate_bwd', grid=(1,),
                          in_specs=[_fb((S, LANES)), _fb((1, LANES)), _fb((S, LANES))],
                          out_specs=[_fb((S, LANES)), _fb((1, LANES))],
                          out_shape=[jax.ShapeDtypeStruct((S, LANES), BF16), jax.ShapeDtypeStruct((1, LANES), F32)],
                          compiler_params=_params(('arbitrary',)))(f_raw, b_f_pad, dc)


def _fox_logits(q, k, c_blk, ct_blk, h, i, j, T):
    s = lax.dot_general(q, k, _DIMS['nt'], preferred_element_type=F32) * (1.0 / math.sqrt(HEAD_DIM))
    lane = lax.broadcasted_iota(jnp.int32, c_blk.shape, 1)
    cq = jnp.sum(jnp.where(lane == h, c_blk, 0.0), axis=1, keepdims=True)
    sub = lax.broadcasted_iota(jnp.int32, ct_blk.shape, 0)
    ck = jnp.sum(jnp.where(sub == h, ct_blk, 0.0), axis=0, keepdims=True)
    rows = i * T + lax.broadcasted_iota(jnp.int32, (T, T), 0)
    cols = j * T + lax.broadcasted_iota(jnp.int32, (T, T), 1)
    return jnp.where(cols <= rows, s + cq - ck, -jnp.inf)


def fox_fwd(qn, kn, vb, c, ct, T):
    S, FW = qn.shape
    H = FW // HEAD_DIM
    Hp = ct.shape[0]
    n = S // T

    def body(q_ref, k_ref, v_ref, c_ref, ct_ref, o_ref, lse_ref, m_s, l_s, acc_s):
        h, i, j = pl.program_id(0), pl.program_id(1), pl.program_id(2)

        @pl.when(j == 0)
        def _():
            m_s[...] = jnp.full_like(m_s, -jnp.inf)
            l_s[...] = jnp.zeros_like(l_s)
            acc_s[...] = jnp.zeros_like(acc_s)

        @pl.when(j <= i)
        def _():
            s = _fox_logits(q_ref[...], k_ref[...], c_ref[...], ct_ref[...], h, i, j, T)
            m_new = jnp.maximum(m_s[...], jnp.max(s, axis=1, keepdims=True))
            alpha = jnp.exp(m_s[...] - m_new)
            p = jnp.exp(s - m_new)
            l_s[...] = alpha * l_s[...] + jnp.sum(p, axis=1, keepdims=True)
            acc_s[...] = alpha * acc_s[...] + jnp.dot(p.astype(BF16), v_ref[...], preferred_element_type=F32)
            m_s[...] = m_new

        @pl.when(j == i)
        def _():
            o_ref[...] = acc_s[...] / l_s[...]
            lse_ref[...] = jnp.broadcast_to(m_s[...] + jnp.log(l_s[...]), (T, LANES))

    qs = pl.BlockSpec((T, HEAD_DIM), lambda h, i, j: (i, h))
    ks = pl.BlockSpec((T, HEAD_DIM), lambda h, i, j: (jnp.minimum(j, i), h))
    return pl.pallas_call(
        body, name='fox_fwd', grid=(H, n, n),
        in_specs=[qs, ks, ks, pl.BlockSpec((T, LANES), lambda h, i, j: (i, 0)),
                  pl.BlockSpec((Hp, T), lambda h, i, j: (0, jnp.minimum(j, i)))],
        out_specs=[qs, pl.BlockSpec((None, T, LANES), lambda h, i, j: (h, i, 0))],
        out_shape=[jax.ShapeDtypeStruct((S, FW), F32), jax.ShapeDtypeStruct((H, S, LANES), F32)],
        scratch_shapes=[pltpu.VMEM((T, 1), F32), pltpu.VMEM((T, 1), F32), pltpu.VMEM((T, HEAD_DIM), F32)],
        compiler_params=_params(('parallel', 'parallel', 'arbitrary')))(qn, kn, vb, c, ct)


def _fox_p_ds(q_ref, k_ref, v_ref, do_ref, c_ref, ct_ref, lse_ref, dl_ref, h, i, j, T):
    s = _fox_logits(q_ref[...], k_ref[...], c_ref[...], ct_ref[...], h, i, j, T)
    p = jnp.exp(s - jnp.tile(lse_ref[...], (1, T // LANES)))
    dp = lax.dot_general(do_ref[...], v_ref[...], _DIMS['nt'], preferred_element_type=F32)
    ds = p * (dp - jnp.tile(dl_ref[...], (1, T // LANES)))
    return p, dp, ds


def fox_bwd_q(qn, kn, vb, do, c, ct, lse, dl, T):
    S, FW = qn.shape
    H = FW // HEAD_DIM
    Hp = ct.shape[0]
    n = S // T

    def body(q_ref, k_ref, v_ref, do_ref, c_ref, ct_ref, lse_ref, dl_ref, dq_ref, dl2_ref, acc_s, rs_s):
        h, i, j = pl.program_id(0), pl.program_id(1), pl.program_id(2)

        @pl.when(j == 0)
        def _():
            acc_s[...] = jnp.zeros_like(acc_s)
            rs_s[...] = jnp.zeros_like(rs_s)

        @pl.when(j <= i)
        def _():
            p, dp, ds = _fox_p_ds(q_ref, k_ref, v_ref, do_ref, c_ref, ct_ref, lse_ref, dl_ref, h, i, j, T)
            acc_s[...] += jnp.dot(ds.astype(BF16), k_ref[...], preferred_element_type=F32)
            rs_s[...] += jnp.sum(p * dp, axis=1, keepdims=True)

        @pl.when(j == i)
        def _():
            dq_ref[...] = acc_s[...] * (1.0 / math.sqrt(HEAD_DIM))
            dl2_ref[...] = jnp.broadcast_to(rs_s[...], (T, LANES))

    qs = pl.BlockSpec((T, HEAD_DIM), lambda h, i, j: (i, h))
    ks = pl.BlockSpec((T, HEAD_DIM), lambda h, i, j: (jnp.minimum(j, i), h))
    st = pl.BlockSpec((None, T, LANES), lambda h, i, j: (h, i, 0))
    return pl.pallas_call(
        body, name='fox_bwd_q', grid=(H, n, n),
        in_specs=[qs, ks, ks, qs, pl.BlockSpec((T, LANES), lambda h, i, j: (i, 0)),
                  pl.BlockSpec((Hp, T), lambda h, i, j: (0, jnp.minimum(j, i))), st, st],
        out_specs=[qs, st], out_shape=[jax.ShapeDtypeStruct((S, FW), F32), jax.ShapeDtypeStruct((H, S, LANES), F32)],
        scratch_shapes=[pltpu.VMEM((T, HEAD_DIM), F32), pltpu.VMEM((T, 1), F32)],
        compiler_params=_params(('parallel', 'parallel', 'arbitrary')))(qn, kn, vb, do, c, ct, lse, dl)


def fox_bwd_kv(qn, kn, vb, do, c, ct, lse, dl, T):
    S, FW = qn.shape
    H = FW // HEAD_DIM
    Hp = ct.shape[0]
    n = S // T

    def body(q_ref, k_ref, v_ref, do_ref, c_ref, ct_ref, lse_ref, dl_ref, dk_ref, dv_ref, dc_ref, dk_s, dv_s, dc_s):
        h, j, i = pl.program_id(0), pl.program_id(1), pl.program_id(2)

        @pl.when(i == 0)
        def _():
            dk_s[...] = jnp.zeros_like(dk_s)
            dv_s[...] = jnp.zeros_like(dv_s)
            dc_s[...] = jnp.zeros_like(dc_s)

        @pl.when(i >= j)
        def _():
            p, _, ds = _fox_p_ds(q_ref, k_ref, v_ref, do_ref, c_ref, ct_ref, lse_ref, dl_ref, h, i, j, T)
            dv_s[...] += lax.dot_general(p.astype(BF16), do_ref[...], _DIMS['tn'], preferred_element_type=F32)
            dk_s[...] += lax.dot_general(ds.astype(BF16), q_ref[...], _DIMS['tn'], preferred_element_type=F32)
            dc_s[...] += jnp.sum(ds, axis=0, keepdims=True)

        @pl.when(i == n - 1)
        def _():
            dk_ref[...] = dk_s[...] * (1.0 / math.sqrt(HEAD_DIM))
            dv_ref[...] = dv_s[...].astype(BF16)
            dc_ref[...] = -dc_s[...]

    qs = pl.BlockSpec((T, HEAD_DIM), lambda h, j, i: (jnp.maximum(i, j), h))
    ks = pl.BlockSpec((T, HEAD_DIM), lambda h, j, i: (j, h))
    st = pl.BlockSpec((None, T, LANES), lambda h, j, i: (h, jnp.maximum(i, j), 0))
    return pl.pallas_call(
        body, name='fox_bwd_kv', grid=(H, n, n),
        in_specs=[qs, ks, ks, qs, pl.BlockSpec((T, LANES), lambda h, j, i: (jnp.maximum(i, j), 0)),
                  pl.BlockSpec((Hp, T), lambda h, j, i: (0, j)), st, st],
        out_specs=[ks, ks, pl.BlockSpec((None, 1, T), lambda h, j, i: (h, 0, j))],
        out_shape=[jax.ShapeDtypeStruct((S, FW), F32), jax.ShapeDtypeStruct((S, FW), BF16),
                   jax.ShapeDtypeStruct((H, 1, S), F32)],
        scratch_shapes=[pltpu.VMEM((T, HEAD_DIM), F32), pltpu.VMEM((T, HEAD_DIM), F32), pltpu.VMEM((1, T), F32)],
        compiler_params=_params(('parallel', 'parallel', 'arbitrary')))(qn, kn, vb, do, c, ct, lse, dl)


def _shift_down(v, d, rows, fill):
    return jnp.where(rows >= d, pltpu.roll(v, d, 0), fill)


def _shift_up(v, d, rows, S, fill):
    return jnp.where(rows < S - d, pltpu.roll(v, S - d, 0), fill)


def _scan(a, b, rows, S, reverse):
    d = 1
    while d < S:
        if reverse:
            a_s, b_s = _shift_up(a, d, rows, S, 1.0), _shift_up(b, d, rows, S, 0.0)
        else:
            a_s, b_s = _shift_down(a, d, rows, 1.0), _shift_down(b, d, rows, 0.0)
        b = a * b_s + b
        a = a * a_s
        d *= 2
    return b


def _lru_forward(u, cw, cb, wra, bra, wri, bri, lam, rows):
    uc = cb + cw[CONV_W - 1] * u
    for d in range(1, CONV_W):
        uc = uc + cw[CONV_W - 1 - d] * _shift_down(u, d, rows, 0.0)
    ucb = uc.astype(BF16)
    r = _sigmoid(jnp.dot(ucb, wra.astype(BF16), preferred_element_type=F32) + bra)
    ig = _sigmoid(jnp.dot(ucb, wri.astype(BF16), preferred_element_type=F32) + bri)
    sp = _softplus(-lam)
    log_a = -LRU_C * r * sp
    a = jnp.exp(log_a)
    sq = jnp.sqrt(_neg_expm1(2.0 * log_a))
    iu = ig * uc
    hseq = _scan(a, sq * iu, rows, u.shape[0], False)
    return uc, ucb, r, ig, sp, a, sq, iu, hseq


def _lru_specs(S, n_u, n_g):
    col = lambda off: pl.BlockSpec((S, LANES), lambda cbk: (0, off + cbk))
    vec = pl.BlockSpec((1, LANES), lambda cbk: (0, cbk))
    mat = pl.BlockSpec((None, LANES, LANES), lambda cbk: (cbk, 0, 0))
    cw = pl.BlockSpec((CONV_W, LANES), lambda cbk: (0, cbk))
    return col, vec, mat, cw


def lru_fwd(proj, conv_w, conv_b, w_ra, b_ra, w_ri, b_ri, lam, u_off, g_off):
    S = proj.shape[0]
    nb = w_ra.shape[0]
    col, vec, mat, cws = _lru_specs(S, u_off, g_off)

    def body(u_ref, g_ref, cw_ref, cb_ref, wra_ref, bra_ref, wri_ref, bri_ref, lam_ref, y_ref):
        rows = lax.broadcasted_iota(jnp.int32, (S, LANES), 0)
        cw = [cw_ref[t:t + 1, :] for t in range(CONV_W)]
        hseq = _lru_forward(u_ref[...], cw, cb_ref[...], wra_ref[...], bra_ref[...], wri_ref[...],
                            bri_ref[...], lam_ref[...], rows)[-1]
        y_ref[...] = hseq * _gelu_and_grad(g_ref[...])[0]

    return pl.pallas_call(
        body, name='lru_fwd', grid=(nb,),
        in_specs=[col(u_off), col(g_off), cws, vec, mat, vec, mat, vec, vec], out_specs=col(0),
        out_shape=jax.ShapeDtypeStruct((S, nb * LANES), F32),
        compiler_params=_params(('parallel',)))(proj, proj, conv_w, conv_b, w_ra, b_ra, w_ri, b_ri, lam)


def lru_bwd(proj, dy, conv_w, conv_b, w_ra, b_ra, w_ri, b_ri, lam, u_off, g_off):
    S = proj.shape[0]
    nb = w_ra.shape[0]
    LW = nb * LANES
    col, vec, mat, cws = _lru_specs(S, u_off, g_off)

    def body(u_ref, g_ref, dy_ref, cw_ref, cb_ref, wra_ref, bra_ref, wri_ref, bri_ref, lam_ref,
             du_ref, dg_ref, dcw_ref, dcb_ref, dwra_ref, dbra_ref, dwri_ref, dbri_ref, dlam_ref):
        rows = lax.broadcasted_iota(jnp.int32, (S, LANES), 0)
        u, lam_v = u_ref[...], lam_ref[...]
        cw = [cw_ref[t:t + 1, :] for t in range(CONV_W)]
        wra, wri = wra_ref[...].astype(BF16), wri_ref[...].astype(BF16)
        uc, ucb, r, ig, sp, a, sq, iu, hseq = _lru_forward(u, cw, cb_ref[...], wra, bra_ref[...], wri, bri_ref[...],
                                                           lam_v, rows)
        gl, dgl = _gelu_and_grad(g_ref[...])
        dy_v = dy_ref[...]
        dg_ref[...] = (dy_v * hseq * dgl).astype(BF16)
        G = _scan(_shift_up(a, 1, rows, S, 0.0), dy_v * gl, rows, S, True)
        da = G * _shift_down(hseq, 1, rows, 0.0)
        diu = G * sq
        dsq = G * iu
        dlog_a = da * a - dsq * a * a / jnp.maximum(sq, 1e-30)
        dr = dlog_a * (-LRU_C * sp)
        dsp = jnp.sum(dlog_a * (-LRU_C * r), axis=0, keepdims=True)
        dlam_ref[...] = -dsp * _sigmoid(-lam_v)
        dzr = dr * r * (1.0 - r)
        dzi = diu * uc * ig * (1.0 - ig)
        dzrb, dzib = dzr.astype(BF16), dzi.astype(BF16)
        duc = (diu * ig + lax.dot_general(dzrb, wra, _DIMS['nt'], preferred_element_type=F32)
               + lax.dot_general(dzib, wri, _DIMS['nt'], preferred_element_type=F32))
        dwra_ref[...] = lax.dot_general(ucb, dzrb, _DIMS['tn'], preferred_element_type=F32)
        dwri_ref[...] = lax.dot_general(ucb, dzib, _DIMS['tn'], preferred_element_type=F32)
        dbra_ref[...] = jnp.sum(dzr, axis=0, keepdims=True)
        dbri_ref[...] = jnp.sum(dzi, axis=0, keepdims=True)
        dcb_ref[...] = jnp.sum(duc, axis=0, keepdims=True)
        du = cw[CONV_W - 1] * duc
        dcw_ref[CONV_W - 1:CONV_W, :] = jnp.sum(duc * u, axis=0, keepdims=True)
        for d in range(1, CONV_W):
            du = du + cw[CONV_W - 1 - d] * _shift_up(duc, d, rows, S, 0.0)
            dcw_ref[CONV_W - 1 - d:CONV_W - d, :] = jnp.sum(duc * _shift_down(u, d, rows, 0.0), axis=0, keepdims=True)
        du_ref[...] = du.astype(BF16)

    sd = jax.ShapeDtypeStruct
    return pl.pallas_call(
        body, name='lru_bwd', grid=(nb,),
        in_specs=[col(u_off), col(g_off), col(0), cws, vec, mat, vec, mat, vec, vec],
        out_specs=[col(0), col(0), cws, vec, mat, vec, mat, vec, vec],
        out_shape=[sd((S, LW), BF16), sd((S, LW), BF16), sd((CONV_W, LW), F32), sd((1, LW), F32),
                   sd((nb, LANES, LANES), F32), sd((1, LW), F32), sd((nb, LANES, LANES), F32), sd((1, LW), F32),
                   sd((1, LW), F32)],
        compiler_params=_params(('parallel',)))(proj, proj, dy, conv_w, conv_b, w_ra, b_ra, w_ri, b_ri, lam)


def mix_fwd(o_fox, y_lru, g_fox, g_lru):
    S, FW = o_fox.shape
    tr = _tile(S, (256, 128))

    def body(o_ref, y_ref, gf_ref, gl_ref, m_ref):
        m_ref[...] = jnp.concatenate([_rms(o_ref[...], gf_ref[...]), _rms(y_ref[...], gl_ref[...])],
                                     axis=1).astype(BF16)

    return _rows_call('mix_fwd', body, S, tr,
                      [(o_fox, _rb(tr, FW)), (y_lru, _rb(tr, FW)), (g_fox, _fb((1, FW))), (g_lru, _fb((1, FW)))],
                      [((S, 2 * FW), BF16, _rb(tr, 2 * FW))])[0]


def mix_bwd(o_fox, y_lru, g_fox, g_lru, dmix):
    S, FW = o_fox.shape
    H = FW // HEAD_DIM
    tr = _tile(S, (256, 128))

    def body(o_ref, y_ref, gf_ref, gl_ref, df_ref, dl_ref, do_ref, dlt_ref, dy_ref, dgf_ref, dgl_ref):
        o = o_ref[...]
        do, dgf = _rms_bwd(o, gf_ref[...], df_ref[...])
        dyl, dgl = _rms_bwd(y_ref[...], gl_ref[...], dl_ref[...])
        do_ref[...] = do.astype(BF16)
        dy_ref[...] = dyl
        prod = do * o
        for h in range(H):
            dlt_ref[h] = jnp.broadcast_to(
                jnp.sum(prod[:, h * HEAD_DIM:(h + 1) * HEAD_DIM], axis=1, keepdims=True), (tr, LANES))
        first = pl.program_id(0) == 0
        _acc_out(dgf_ref, first, dgf)
        _acc_out(dgl_ref, first, dgl)

    g = _fb((1, FW))
    return _rows_call('mix_bwd', body, S, tr,
                      [(o_fox, _rb(tr, FW)), (y_lru, _rb(tr, FW)), (g_fox, g), (g_lru, g), (dmix, _rb(tr, FW, 0)),
                       (dmix, _rb(tr, FW, 1))],
                      [((S, FW), BF16, _rb(tr, FW)), ((H, S, LANES), F32, pl.BlockSpec((H, tr, LANES), lambda i: (0, i, 0))),
                       ((S, FW), F32, _rb(tr, FW)), ((1, FW), F32, g), ((1, FW), F32, g)])


def _xattn_heads(cq_raw, ckv, g_cq, g_ck, XW):
    out = []
    for h in range(XW // HEAD_DIM):
        sl = slice(h * HEAD_DIM, (h + 1) * HEAD_DIM)
        out.append((cq_raw[:, sl], _rms(cq_raw[:, sl], g_cq), ckv[:, sl], _rms(ckv[:, sl], g_ck),
                    ckv[:, XW + h * HEAD_DIM:XW + (h + 1) * HEAD_DIM].astype(BF16)))
    return out


def xattn_fwd(cq_raw, ckv, g_cq, g_ck):
    S, XW = cq_raw.shape
    M = ckv.shape[0]
    tr = _tile(S, (512, 256, 128))

    def body(q_ref, kv_ref, gq_ref, gk_ref, o_ref):
        outs = []
        for _, qn, _, kn, v in _xattn_heads(q_ref[...], kv_ref[...], gq_ref[...], gk_ref[...], XW):
            s = lax.dot_general(qn.astype(BF16), kn.astype(BF16), _DIMS['nt'], preferred_element_type=F32)
            s = s / math.sqrt(HEAD_DIM)
            p = jnp.exp(s - jnp.max(s, axis=1, keepdims=True))
            p = p / jnp.sum(p, axis=1, keepdims=True)
            outs.append(jnp.dot(p.astype(BF16), v, preferred_element_type=F32))
        o_ref[...] = jnp.concatenate(outs, axis=1).astype(BF16)

    g = _fb((1, HEAD_DIM))
    return _rows_call('xattn_fwd', body, S, tr,
                      [(cq_raw, _rb(tr, XW)), (ckv, _fb((M, 2 * XW))), (g_cq, g), (g_ck, g)],
                      [((S, XW), BF16, _rb(tr, XW))])[0]


def xattn_bwd(cq_raw, ckv, g_cq, g_ck, do):
    S, XW = cq_raw.shape
    M = ckv.shape[0]
    tr = _tile(S, (512, 256, 128))
    n = S // tr

    def body(q_ref, kv_ref, gq_ref, gk_ref, do_ref, dq_ref, dkv_ref, dgq_ref, dgk_ref):
        i = pl.program_id(0)
        do_v = do_ref[...]
        dqs, dkn, dvs = [], [], []
        dgq = jnp.zeros((1, HEAD_DIM), F32)
        for h, (q_raw, qn, _, kn, v) in enumerate(_xattn_heads(q_ref[...], kv_ref[...], gq_ref[...], gk_ref[...], XW)):
            qb, kb = qn.astype(BF16), kn.astype(BF16)
            doh = do_v[:, h * HEAD_DIM:(h + 1) * HEAD_DIM]
            s = lax.dot_general(qb, kb, _DIMS['nt'], preferred_element_type=F32) / math.sqrt(HEAD_DIM)
            p = jnp.exp(s - jnp.max(s, axis=1, keepdims=True))
            p = p / jnp.sum(p, axis=1, keepdims=True)
            dp = lax.dot_general(doh, v, _DIMS['nt'], preferred_element_type=F32)
            ds = (p * (dp - jnp.sum(p * dp, axis=1, keepdims=True)) / math.sqrt(HEAD_DIM)).astype(BF16)
            dvs.append(lax.dot_general(p.astype(BF16), doh, _DIMS['tn'], preferred_element_type=F32))
            dkn.append(lax.dot_general(ds, qb, _DIMS['tn'], preferred_element_type=F32))
            dq, g1 = _rms_bwd(q_raw, gq_ref[...], jnp.dot(ds, kb, preferred_element_type=F32))
            dqs.append(dq)
            dgq = dgq + g1
        dq_ref[...] = jnp.concatenate(dqs, axis=1).astype(BF16)
        first = i == 0
        _acc_out(dgq_ref, first, dgq)
        _acc_out(dkv_ref, first, jnp.concatenate(dkn + dvs, axis=1))

        @pl.when(i == n - 1)
        def _():
            kv = kv_ref[...]
            acc = dkv_ref[...]
            dk, gk = _heads(lambda t, d: _rms_bwd(t, gk_ref[...], d), XW // HEAD_DIM, kv[:, :XW], acc[:, :XW])
            dkv_ref[:, :XW] = dk
            dgk_ref[...] = gk

    g = _fb((1, HEAD_DIM))
    return _rows_call('xattn_bwd', body, S, tr,
                      [(cq_raw, _rb(tr, XW)), (ckv, _fb((M, 2 * XW))), (g_cq, g), (g_ck, g), (do, _rb(tr, XW))],
                      [((S, XW), BF16, _rb(tr, XW)), ((M, 2 * XW), F32, _fb((M, 2 * XW))), ((1, HEAD_DIM), F32, g),
                       ((1, HEAD_DIM), F32, g)])


def swiglu_fwd(gu, F):
    S = gu.shape[0]
    tr = _tile(S, (256, 128))
    tf = _tile(F, (1408, 1024, 512, 256, 128))
    nf = F // tf

    def body(g_ref, u_ref, a_ref):
        g = g_ref[...]
        a_ref[...] = (g * _sigmoid(g) * u_ref[...]).astype(BF16)

    return pl.pallas_call(
        body, name='swiglu_fwd', grid=(S // tr, nf),
        in_specs=[pl.BlockSpec((tr, tf), lambda i, n: (i, n)), pl.BlockSpec((tr, tf), lambda i, n: (i, n + nf))],
        out_specs=pl.BlockSpec((tr, tf), lambda i, n: (i, n)), out_shape=jax.ShapeDtypeStruct((S, F), BF16),
        compiler_params=_params(('parallel', 'parallel')))(gu, gu)


def swiglu_bwd(gu, dact, F):
    S = gu.shape[0]
    tr = _tile(S, (256, 128))
    tf = _tile(F, (1408, 1024, 512, 256, 128))
    nf = F // tf

    def body(g_ref, u_ref, da_ref, o_ref):
        n = pl.program_id(1)
        g, da = g_ref[...], da_ref[...]
        sg = _sigmoid(g)

        @pl.when(n < nf)
        def _():
            o_ref[...] = (da * u_ref[...] * sg * (1.0 + g * (1.0 - sg))).astype(BF16)

        @pl.when(n >= nf)
        def _():
            o_ref[...] = (da * g * sg).astype(BF16)

    return pl.pallas_call(
        body, name='swiglu_bwd', grid=(S // tr, 2 * nf),
        in_specs=[pl.BlockSpec((tr, tf), lambda i, n: (i, n % nf)), pl.BlockSpec((tr, tf), lambda i, n: (i, n % nf + nf)),
                  pl.BlockSpec((tr, tf), lambda i, n: (i, n % nf))],
        out_specs=pl.BlockSpec((tr, tf), lambda i, n: (i, n)), out_shape=jax.ShapeDtypeStruct((S, 2 * F), BF16),
        compiler_params=_params(('parallel', 'arbitrary')))(gu, gu, dact)


def loss_head(y, target):
    S, D = y.shape
    tr = _tile(S, (256, 128))

    def body(y_ref, t_ref, d_ref, db_ref, l_ref):
        err = y_ref[...] - t_ref[...]
        d = err * (1.0 / D)
        d_ref[...] = d
        db_ref[...] = d.astype(BF16)
        part = jnp.sum(jnp.sum(err * err, axis=1, keepdims=True), axis=0, keepdims=True) * (0.5 / D)
        _acc_out(l_ref, pl.program_id(0) == 0, jnp.broadcast_to(part, (1, LANES)))

    return _rows_call('loss_head', body, S, tr, [(y, _rb(tr, D)), (target, _rb(tr, D))],
                      [((S, D), F32, _rb(tr, D)), ((S, D), BF16, _rb(tr, D)), ((1, LANES), F32, _fb((1, LANES)))])


def adamw(name, w, g, m, v):
    R, C = w.shape
    tr = _tile(R, (256, 128, 64, 32, 16, 8))

    def body(w_ref, g_ref, m_ref, v_ref, d_ref, mo_ref, vo_ref):
        gv = g_ref[...]
        mn = ADAM_B1 * m_ref[...] + (1.0 - ADAM_B1) * gv
        vn = ADAM_B2 * v_ref[...] + (1.0 - ADAM_B2) * (gv * gv)
        m_hat = mn / (1.0 - ADAM_B1 ** ADAM_STEP)
        v_hat = vn / (1.0 - ADAM_B2 ** ADAM_STEP)
        d_ref[...] = -ADAM_LR * (m_hat / (jnp.sqrt(v_hat) + ADAM_EPS) + ADAM_WD * w_ref[...])
        mo_ref[...] = mn
        vo_ref[...] = vn

    spec = _rb(tr, C)
    return _rows_call(name, body, R, tr, [(w, spec), (g, spec), (m, spec), (v, spec)], [((R, C), F32, spec)] * 3)


def _place():
    x, y, c = lax.axis_index('x'), lax.axis_index('y'), lax.axis_index('c')
    return x, y, c, [(1 - x, y), (x, 1 - y), (1 - x, 1 - y)]


def _rcopy(src, dst, ssem, rsem, dev):
    return pltpu.make_async_remote_copy(src_ref=src, dst_ref=dst, send_sem=ssem, recv_sem=rsem, device_id=dev,
                                        device_id_type=MESH)


def gather_weights(shards, whole):
    nT = len(shards)

    def body(*refs):
        ins, outs = refs[:nT], refs[nT:2 * nT]
        ssem, rsem, lsem = refs[2 * nT:]
        x, y, c, chips = _place()
        me = 2 * x + y
        local = [pltpu.make_async_copy(ins[t], outs[t].at[me], lsem.at[t]) for t in range(nT)]
        for cp in local:
            cp.start()

        def part(t, half):
            hr = shards[t].shape[0] // 2
            return pl.ds(0, shards[t].shape[0]) if whole[t] else pl.ds(half * hr, hr)

        sends = []
        for t in range(nT):
            for k, (px, py) in enumerate(chips):
                cp = _rcopy(ins[t].at[part(t, c)], outs[t].at[me, part(t, c)], ssem.at[6 * t + k], rsem.at[6 * t + k],
                            (px, py, c))
                cp.start()
                sends.append(cp)
        for t in range(nT):
            for k, (px, py) in enumerate(chips):
                blk = outs[t].at[2 * px + py, part(t, c)]
                _rcopy(blk, blk, ssem.at[6 * t + k], rsem.at[6 * t + k], (px, py, c)).wait_recv()
                if not whole[t]:
                    cp = _rcopy(blk, blk, ssem.at[6 * t + 3 + k], rsem.at[6 * t + 3 + k], (x, y, 1 - c))
                    cp.start()
                    sends.append(cp)
        for t in range(nT):
            if not whole[t]:
                for k, (px, py) in enumerate(chips):
                    blk = outs[t].at[2 * px + py, part(t, 1 - c)]
                    _rcopy(blk, blk, ssem.at[6 * t + 3 + k], rsem.at[6 * t + 3 + k], (x, y, 1 - c)).wait_recv()
        for cp in sends:
            cp.wait_send()
        for cp in local:
            cp.wait()

    return pl.pallas_call(
        body, name='gather_weights', in_specs=[ANY] * nT, out_specs=[ANY] * nT,
        out_shape=[jax.ShapeDtypeStruct((N_CHIPS,) + s.shape, s.dtype) for s in shards],
        scratch_shapes=[pltpu.SemaphoreType.DMA((6 * nT,)), pltpu.SemaphoreType.DMA((6 * nT,)),
                        pltpu.SemaphoreType.DMA((nT,))],
        compiler_params=_params())(*shards)


def pair_exchange(grads):
    nT = len(grads)

    def body(*refs):
        ins, outs = refs[:nT], refs[nT:2 * nT]
        ssem, rsem = refs[2 * nT:]
        x, y, c, _ = _place()
        cps = [_rcopy(ins[t].at[:, 1 - c], outs[t], ssem.at[t], rsem.at[t], (x, y, 1 - c)) for t in range(nT)]
        for cp in cps:
            cp.start()
        for cp in cps:
            cp.wait()

    return pl.pallas_call(
        body, name='pair_exchange', in_specs=[ANY] * nT, out_specs=[ANY] * nT,
        out_shape=[jax.ShapeDtypeStruct((g.shape[0],) + g.shape[2:], g.dtype) for g in grads],
        scratch_shapes=[pltpu.SemaphoreType.DMA((nT,)), pltpu.SemaphoreType.DMA((nT,))],
        compiler_params=_params())(*grads)


def pair_add(name, g, got, c_idx):
    J, _, hr, C = g.shape
    tr = _tile(hr, (256, 128, 64, 32, 16))

    def body(c_ref, g_ref, r_ref, o_ref):
        o_ref[...] = (g_ref[...] + r_ref[...]).astype(BF16)

    return pl.pallas_call(
        body, name=name,
        grid_spec=pltpu.PrefetchScalarGridSpec(
            num_scalar_prefetch=1, grid=(J, hr // tr),
            in_specs=[pl.BlockSpec((None, None, tr, C), lambda j, i, c_ref: (j, c_ref[0], i, 0)),
                      pl.BlockSpec((None, tr, C), lambda j, i, c_ref: (j, i, 0))],
            out_specs=pl.BlockSpec((None, tr, C), lambda j, i, c_ref: (j, i, 0))),
        out_shape=jax.ShapeDtypeStruct((J, hr, C), BF16),
        compiler_params=_params(('parallel', 'parallel')))(c_idx, g, got)


def chip_exchange(parts):
    nT = len(parts)

    def body(*refs):
        ins, outs = refs[:nT], refs[nT:2 * nT]
        ssem, rsem, lsem = refs[2 * nT:]
        x, y, c, chips = _place()
        me = 2 * x + y
        local = [pltpu.make_async_copy(ins[t].at[me], outs[t].at[me], lsem.at[t]) for t in range(nT)]
        for cp in local:
            cp.start()
        sends = []
        for t in range(nT):
            for k, (px, py) in enumerate(chips):
                cp = _rcopy(ins[t].at[2 * px + py], outs[t].at[me], ssem.at[3 * t + k], rsem.at[3 * t + k], (px, py, c))
                cp.start()
                sends.append(cp)
        for t in range(nT):
            for k, (px, py) in enumerate(chips):
                blk = outs[t].at[2 * px + py]
                _rcopy(blk, blk, ssem.at[3 * t + k], rsem.at[3 * t + k], (px, py, c)).wait_recv()
        for cp in sends:
            cp.wait_send()
        for cp in local:
            cp.wait()

    return pl.pallas_call(
        body, name='chip_exchange', in_specs=[ANY] * nT, out_specs=[ANY] * nT,
        out_shape=[jax.ShapeDtypeStruct(p.shape, p.dtype) for p in parts],
        scratch_shapes=[pltpu.SemaphoreType.DMA((3 * nT,)), pltpu.SemaphoreType.DMA((3 * nT,)),
                        pltpu.SemaphoreType.DMA((nT,))],
        compiler_params=_params())(*parts)


def sum_chips(name, parts):
    J, hr, C = parts.shape
    tr = _tile(hr, (256, 128, 64, 32, 16))

    def body(p_ref, o_ref):
        acc = p_ref[0].astype(F32)
        for j in range(1, J):
            acc = acc + p_ref[j].astype(F32)
        o_ref[...] = acc

    return pl.pallas_call(
        body, name=name, grid=(hr // tr,), in_specs=[pl.BlockSpec((J, tr, C), lambda i: (0, i, 0))],
        out_specs=pl.BlockSpec((tr, C), lambda i: (i, 0)), out_shape=jax.ShapeDtypeStruct((hr, C), F32),
        compiler_params=_params(('parallel',)))(parts)


def pair_join(halves):
    nT = len(halves)

    def body(*refs):
        ins, outs = refs[:nT], refs[nT:2 * nT]
        ssem, rsem, lsem = refs[2 * nT:]
        x, y, c, _ = _place()
        local = [pltpu.make_async_copy(ins[t], outs[t].at[c], lsem.at[t]) for t in range(nT)]
        cps = [_rcopy(ins[t], outs[t].at[c], ssem.at[t], rsem.at[t], (x, y, 1 - c)) for t in range(nT)]
        for cp in local + cps:
            cp.start()
        for t in range(nT):
            blk = outs[t].at[1 - c]
            _rcopy(blk, blk, ssem.at[t], rsem.at[t], (x, y, 1 - c)).wait_recv()
        for cp in cps:
            cp.wait_send()
        for cp in local:
            cp.wait()

    return pl.pallas_call(
        body, name='pair_join', in_specs=[ANY] * nT, out_specs=[ANY] * nT,
        out_shape=[jax.ShapeDtypeStruct((2,) + h.shape, h.dtype) for h in halves],
        scratch_shapes=[pltpu.SemaphoreType.DMA((nT,)), pltpu.SemaphoreType.DMA((nT,)), pltpu.SemaphoreType.DMA((nT,))],
        compiler_params=_params())(*halves)


def allreduce_small(buf):
    R = buf.shape[0]
    VM = pl.BlockSpec(memory_space=pltpu.VMEM)

    def body(x_ref, o_ref, all_ref, ssem, rsem, lsem):
        x, y, c, chips = _place()
        me, sibling = (x, y, c), (x, y, 1 - c)

        def rows(px, py, pc):
            return all_ref.at[pl.ds((4 * px + 2 * py + pc) * R, R), :]

        def copy(k, block, to, src=None):
            return _rcopy(rows(*block) if src is None else src, rows(*block), ssem.at[k], rsem.at[k], to)

        mine = pltpu.make_async_copy(x_ref, rows(*me), lsem)
        mine.start()
        first = [copy(0, me, sibling, src=x_ref)]
        first += [copy(1 + k, me, (*chip, c), src=x_ref) for k, chip in enumerate(chips)]
        for cp in first:
            cp.start()
        passed = [copy(4 + k, (*chip, c), sibling) for k, chip in enumerate(chips)]
        for k, chip in enumerate(chips):
            copy(1 + k, (*chip, c), me).wait_recv()
            passed[k].start()
        copy(0, sibling, me).wait_recv()
        for k, chip in enumerate(chips):
            copy(4 + k, (*chip, 1 - c), me).wait_recv()
        for cp in first + passed:
            cp.wait_send()
        mine.wait()
        acc = all_ref[0:R, :]
        for d in range(1, 8):
            acc = acc + all_ref[d * R:(d + 1) * R, :]
        o_ref[...] = acc

    return pl.pallas_call(
        body, name='allreduce_small', in_specs=[VM], out_specs=VM, out_shape=jax.ShapeDtypeStruct((R, LANES), F32),
        scratch_shapes=[pltpu.VMEM((8 * R, LANES), F32), pltpu.SemaphoreType.DMA((7,)), pltpu.SemaphoreType.DMA((7,)),
                        pltpu.SemaphoreType.DMA],
        compiler_params=_params())(buf)


_PACK = 8 * LANES


def _pack(arrs):
    flat = []
    for a in arrs:
        v = a.reshape(-1).astype(F32)
        flat.append(jnp.pad(v, (0, (-v.shape[0]) % _PACK)))
    return jnp.concatenate(flat).reshape(-1, LANES)


def _unpack(buf, shapes):
    out, off = [], 0
    flat = buf.reshape(-1)
    for sh in shapes:
        n = math.prod(sh)
        out.append(flat[off:off + n].reshape(sh))
        off += n + (-n) % _PACK
    return out


def kernel(x, mem, g_mix, w_in, b_f, g_q, g_k, conv_w, conv_b, w_ra, b_ra, w_ri, b_ri, lam, g_fox_out, g_lru_out, w_out, g_xattn, g_mem, w_cq, w_ckv, g_cq, g_ck, w_co, g_ffn, w_gate_up, w_down, loss_target, m_g_mix, m_w_in, m_b_f, m_g_q, m_g_k, m_conv_w, m_conv_b, m_w_ra, m_b_ra, m_w_ri, m_b_ri, m_lam, m_g_fox_out, m_g_lru_out, m_w_out, m_g_xattn, m_g_mem, m_w_cq, m_w_ckv, m_g_cq, m_g_ck, m_w_co, m_g_ffn, m_w_gate_up, m_w_down, v_g_mix, v_w_in, v_b_f, v_g_q, v_g_k, v_conv_w, v_conv_b, v_w_ra, v_b_ra, v_w_ri, v_b_ri, v_lam, v_g_fox_out, v_g_lru_out, v_w_out, v_g_xattn, v_g_mem, v_w_cq, v_w_ckv, v_g_cq, v_g_ck, v_w_co, v_g_ffn, v_w_gate_up, v_w_down):
    given = dict(locals())
    W = {n: given[n][0] for n in WEIGHTS}
    M1 = {n: given['m_' + n][0] for n in WEIGHTS}
    V1 = {n: given['v_' + n][0] for n in WEIGHTS}
    xs, ms, tgt = x[0], mem[0], loss_target[0]
    S, D = xs.shape
    H = W['b_f'].shape[0]
    FW = H * HEAD_DIM
    LW = W['lam'].shape[0]
    nb = W['w_ra'].shape[0]
    XW = W['w_cq'].shape[1]
    F = W['w_down'].shape[0] * N_CHIPS
    IN_W = W['w_in'].shape[1] * N_CHIPS
    assert FW == LW and LW == nb * LANES and IN_W == 3 * FW + H + 2 * LW and H <= 8
    T = _tile(S, (512, 256, 128))
    c_idx = lax.axis_index('c').astype(jnp.int32).reshape(1)
    chip = 2 * lax.axis_index('x') + lax.axis_index('y')
    vec = lambda n: W[n].reshape(1, -1)

    conv_blk = W['conv_w'].reshape(-1, LANES)
    gathered = gather_weights([W[n].astype(BF16) for n in BIG] + [conv_blk], [False] * len(BIG) + [True])
    G = dict(zip(BIG, gathered[:-1]))
    conv_full = gathered[-1].reshape(N_CHIPS, CONV_W, LW // N_CHIPS).transpose(1, 0, 2).reshape(CONV_W, LW)
    w_in_full = G['w_in'].transpose(1, 0, 2).reshape(D, IN_W)
    w5 = jnp.concatenate([w_in_full[:, :3 * FW], w_in_full[:, 3 * FW + H:]], axis=1)
    wf = jnp.pad(w_in_full[:, 3 * FW:3 * FW + H], ((0, 0), (0, LANES - H)))
    w_out_f = G['w_out'].reshape(2 * FW, D)
    w_cq_f = G['w_cq'].reshape(D, XW)
    w_ckv_f = G['w_ckv'].reshape(D, 2 * XW)
    w_down_f = G['w_down'].reshape(F, D)
    b_f_pad = jnp.pad(vec('b_f'), ((0, 0), (0, LANES - H)))
    u_off, g_off = 3 * FW // LANES, (3 * FW + LW) // LANES

    h1 = norm_fwd('norm_mix', xs, vec('g_mix'))
    proj = _mm('proj_in', h1, w5, 'nn', F32)
    f_raw = _mm('proj_f', h1, wf, 'nn', F32)
    qn, kn, vb = qkv_fwd(proj, vec('g_q'), vec('g_k'), FW)
    cc = fgate_fwd(f_raw, b_f_pad)
    ct = cc[:, :8].T
    o_fox, lse = fox_fwd(qn, kn, vb, cc, ct, T)
    lru_w = (conv_full, vec('conv_b'), W['w_ra'], vec('b_ra'), W['w_ri'], vec('b_ri'), vec('lam'))
    y_lru = lru_fwd(proj, *lru_w, u_off, g_off)
    mixn = mix_fwd(o_fox, y_lru, vec('g_fox_out'), vec('g_lru_out'))
    x1 = _mm('proj_out', mixn, w_out_f, 'nn', F32, res=xs)

    hq = norm_fwd('norm_xq', x1, vec('g_xattn'))
    mn = norm_fwd('norm_mem', ms, vec('g_mem'))
    cq_raw = _mm('proj_cq', hq, w_cq_f, 'nn', F32)
    ckv = _mm('proj_ckv', mn, w_ckv_f, 'nn', F32)
    o_x = xattn_fwd(cq_raw, ckv, vec('g_cq'), vec('g_ck'))
    x2 = _mm_colsharded('proj_co', o_x, G['w_co'], F32, res=x1)

    hf = norm_fwd('norm_ffn', x2, vec('g_ffn'))
    gu = _mm_colsharded('proj_gate_up', hf, G['w_gate_up'], F32)
    act = swiglu_fwd(gu, F)
    yv = _mm('proj_down', act, w_down_f, 'nn', F32, res=x2)
    dy, dyb, loss_blk = loss_head(yv, tgt)
    loss = lax.psum(loss_blk[0, 0], ('x', 'y', 'c'))

    gw = {}
    dact = _mm('bwd_down_x', dyb, w_down_f, 'nt', F32)
    gw['w_down'] = _mm('bwd_down_w', act, dyb, 'tn', F32).reshape(N_CHIPS, F // N_CHIPS, D)
    dgu = swiglu_bwd(gu, dact, F)
    dhf = _mm_colsharded_t('bwd_gate_up_x', dgu, G['w_gate_up'], F32)
    gw['w_gate_up'] = _mm_grad_colsharded('bwd_gate_up_w', hf, dgu, N_CHIPS)
    dx2, dx2b, gw['g_ffn'] = norm_bwd('norm_ffn_bwd', x2, vec('g_ffn'), dhf, res=dy)

    do_x = _mm_colsharded_t('bwd_co_x', dx2b, G['w_co'], BF16)
    gw['w_co'] = _mm_grad_colsharded('bwd_co_w', o_x, dx2b, N_CHIPS)
    dcq_raw, dckv, gw['g_cq'], gw['g_ck'] = xattn_bwd(cq_raw, ckv, vec('g_cq'), vec('g_ck'), do_x)
    dhq = _mm('bwd_cq_x', dcq_raw, w_cq_f, 'nt', F32)
    gw['w_cq'] = _mm('bwd_cq_w', hq, dcq_raw, 'tn', F32).reshape(N_CHIPS, D // N_CHIPS, XW)
    dmn = _mm('bwd_ckv_x', dckv, w_ckv_f, 'nt', F32)
    gw['w_ckv'] = _mm('bwd_ckv_w', mn, dckv, 'tn', F32).reshape(N_CHIPS, D // N_CHIPS, 2 * XW)
    (gw['g_mem'],) = norm_bwd('norm_mem_bwd', ms, vec('g_mem'), dmn, want_dx=False)
    dx1, dx1b, gw['g_xattn'] = norm_bwd('norm_xq_bwd', x1, vec('g_xattn'), dhq, res=dx2)

    dmix = _mm('bwd_out_x', dx1b, w_out_f, 'nt', F32)
    gw['w_out'] = _mm('bwd_out_w', mixn, dx1b, 'tn', F32).reshape(N_CHIPS, 2 * FW // N_CHIPS, D)
    do_fox, delta, dy_lru, gw['g_fox_out'], gw['g_lru_out'] = mix_bwd(o_fox, y_lru, vec('g_fox_out'), vec('g_lru_out'),
                                                                     dmix)
    (du, dgate, gw['conv_w'], gw['conv_b'], gw['w_ra'], gw['b_ra'], gw['w_ri'], gw['b_ri'],
     gw['lam']) = lru_bwd(proj, dy_lru, *lru_w, u_off, g_off)
    dqn, delta2 = fox_bwd_q(qn, kn, vb, do_fox, cc, ct, lse, delta, T)
    dkn, dv, dct = fox_bwd_kv(qn, kn, vb, do_fox, cc, ct, lse, delta2, T)
    dq, dk, gw['g_q'], gw['g_k'] = qkv_bwd(proj, vec('g_q'), vec('g_k'), dqn, dkn, FW)
    dc = jnp.pad(dct.reshape(H, S).T, ((0, 0), (0, LANES - H)))
    df, db_f = fgate_bwd(f_raw, b_f_pad, dc, H)
    gw['b_f'] = db_f[:, :H]
    dproj = jnp.concatenate([dq, dk, dv, du, dgate], axis=1)
    dh_a = _mm('bwd_f_x', df, wf, 'nt', F32)
    dh1 = _mm('bwd_in_x', dproj, w5, 'nt', F32, res=dh_a)
    dw5 = _mm('bwd_in_w', h1, dproj, 'tn', F32)
    dwf = _mm('bwd_f_w', h1, df, 'tn', F32)
    dw_in = jnp.concatenate([dw5[:, :3 * FW], dwf[:, :H], dw5[:, 3 * FW:]], axis=1)
    gw['w_in'] = dw_in.reshape(D, N_CHIPS, IN_W // N_CHIPS).transpose(1, 0, 2)
    grad_x, _, gw['g_mix'] = norm_bwd('norm_mix_bwd', xs, vec('g_mix'), dh1, res=dx1)

    split = [gw[n].reshape(N_CHIPS, 2, gw[n].shape[1] // 2, gw[n].shape[2]) for n in BIG]
    got = pair_exchange(split)
    parts = [pair_add('pair_add_' + n, g, r, c_idx) for n, g, r in zip(BIG, split, got)]
    landed = chip_exchange(parts)
    halves = [sum_chips('sum_chips_' + n, p) for n, p in zip(BIG, landed)]
    joined = pair_join(halves)
    grads = {n: j.reshape(W[n].shape) for n, j in zip(BIG, joined)}

    small_shapes = [gw[n].shape for n in SMALL]
    summed = _unpack(allreduce_small(_pack([gw[n] for n in SMALL])), small_shapes)
    for n, g in zip(SMALL, summed):
        grads[n] = g.reshape(W[n].shape) if n != 'conv_w' else lax.dynamic_slice_in_dim(
            g, chip * (LW // N_CHIPS), LW // N_CHIPS, axis=1)

    delta_w, new_m, new_v = {}, {}, {}
    for n in BIG:
        delta_w[n], new_m[n], new_v[n] = adamw('adamw_' + n, W[n], grads[n], M1[n], V1[n])
    packs = [_pack([d[n] for n in SMALL]) for d in (W, grads, M1, V1)]
    shapes = [W[n].shape for n in SMALL]
    for d, res in zip((delta_w, new_m, new_v), adamw('adamw_small', *packs)):
        d.update(zip(SMALL, _unpack(res, shapes)))

    lead = lambda d: [d[n][None] for n in WEIGHTS]
    return (loss, grad_x[None], *lead(grads), *lead(delta_w), *lead(new_m), *lead(new_v))
```

```python
import functools
import math

import jax
import jax.numpy as jnp
from jax import lax
from jax.experimental import pallas as pl
from jax.experimental.pallas import tpu as pltpu

F32 = jnp.float32
BF16 = jnp.bfloat16
HEAD_DIM = 128
LANES = 128
LRU_C = 8.0
RMS_EPS = 1e-6
CONV_W = 4
ADAM_LR = 0.001
ADAM_B1 = 0.9
ADAM_B2 = 0.999
ADAM_EPS = 1e-08
ADAM_WD = 0.01
ADAM_STEP = 10
VMEM_LIMIT = 56 * 1024 * 1024
N_CHIPS = 4
MESH = pl.DeviceIdType.MESH
ANY = pl.BlockSpec(memory_space=pl.ANY)

WEIGHTS = ['g_mix', 'w_in', 'b_f', 'g_q', 'g_k', 'conv_w', 'conv_b', 'w_ra', 'b_ra', 'w_ri', 'b_ri', 'lam',
           'g_fox_out', 'g_lru_out', 'w_out', 'g_xattn', 'g_mem', 'w_cq', 'w_ckv', 'g_cq', 'g_ck', 'w_co', 'g_ffn',
           'w_gate_up', 'w_down']
BIG = ['w_in', 'w_out', 'w_cq', 'w_ckv', 'w_co', 'w_gate_up', 'w_down']
SMALL = [n for n in WEIGHTS if n not in BIG]


def _params(sem=None):
    if sem is None:
        return pltpu.CompilerParams(vmem_limit_bytes=VMEM_LIMIT)
    return pltpu.CompilerParams(dimension_semantics=sem, vmem_limit_bytes=VMEM_LIMIT)


def _tile(n, cands):
    for t in cands:
        if n % t == 0:
            return t
    return n


ROW_BLOCK_BYTES = 1 << 20


def _row_tile(n_rows, n_cols, min_rows=8):
    cands = [t for t in (512, 256, 128, 64, 32, 16, 8) if t >= min_rows and t * n_cols * 4 <= ROW_BLOCK_BYTES]
    return _tile(n_rows, cands or [min_rows])


def _sigmoid(z):
    return 1.0 / (1.0 + jnp.exp(-z))


def _softplus(z):
    return jnp.maximum(z, 0.0) + jnp.log(1.0 + jnp.exp(-jnp.abs(z)))


def _neg_expm1(z):
    series = -z * (1.0 + z * (0.5 + z * (1.0 / 6.0 + z * (1.0 / 24.0 + z * (1.0 / 120.0)))))
    return jnp.where(z > -0.25, series, 1.0 - jnp.exp(z))


_GELU_K = math.sqrt(2.0 / math.pi)


def _gelu_and_grad(z):
    inner = _GELU_K * (z + 0.044715 * z * z * z)
    t = jnp.tanh(inner)
    g = 0.5 * z * (1.0 + t)
    dg = 0.5 * (1.0 + t) + 0.5 * z * (1.0 - t * t) * _GELU_K * (1.0 + 3.0 * 0.044715 * z * z)
    return g, dg


def _rms(xv, g):
    r = lax.rsqrt(jnp.mean(xv * xv, axis=-1, keepdims=True) + RMS_EPS)
    return xv * r * g


def _rms_bwd(xv, g, dy):
    r = lax.rsqrt(jnp.mean(xv * xv, axis=-1, keepdims=True) + RMS_EPS)
    xh = xv * r
    dyg = dy * g
    dx = r * (dyg - xh * jnp.mean(dyg * xh, axis=-1, keepdims=True))
    return dx, jnp.sum(dy * xh, axis=0, keepdims=True)


def _heads(fn, n_heads, *arrs):
    outs = [fn(*[a[:, h * HEAD_DIM:(h + 1) * HEAD_DIM] for a in arrs]) for h in range(n_heads)]
    first = jnp.concatenate([o[0] for o in outs], axis=1) if n_heads > 1 else outs[0][0]
    rest = [functools.reduce(lambda p, q: p + q, [o[i] for o in outs]) for i in range(1, len(outs[0]))]
    return (first, *rest)


def _split3(v):
    hi = v.astype(BF16)
    r1 = v - hi.astype(F32)
    mid = r1.astype(BF16)
    lo = (r1 - mid.astype(F32)).astype(BF16)
    return hi, mid, lo


def _acc_out(ref, first, val):
    @pl.when(first)
    def _():
        ref[...] = val

    @pl.when(jnp.logical_not(first))
    def _():
        ref[...] += val


_DIMS = {'nn': (((1,), (0,)), ((), ())), 'nt': (((1,), (1,)), ((), ())), 'tn': (((0,), (0,)), ((), ()))}


def _mm_call(name, a, b, mode, grid, a_spec, b_spec, o_spec, o_shape, o_dtype, acc_shape, res=None):
    nk = grid[2]
    dn = _DIMS[mode]

    def body(*refs):
        if res is None:
            a_ref, b_ref, o_ref, acc = refs
            r_ref = None
        else:
            a_ref, b_ref, r_ref, o_ref, acc = refs
        k = pl.program_id(2)

        @pl.when(k == 0)
        def _():
            acc[...] = jnp.zeros_like(acc)

        acc[...] += lax.dot_general(a_ref[...].astype(BF16), b_ref[...].astype(BF16), dn,
                                    preferred_element_type=F32)

        @pl.when(k == nk - 1)
        def _():
            r = acc[...]
            if r_ref is not None:
                r = r + r_ref[...]
            o_ref[...] = r.astype(o_dtype)

    ins = [a, b] + ([] if res is None else [res])
    specs = [a_spec, b_spec] + ([] if res is None else [o_spec])
    return pl.pallas_call(
        body, name=name, grid=grid, in_specs=specs, out_specs=o_spec,
        out_shape=jax.ShapeDtypeStruct(o_shape, o_dtype), scratch_shapes=[pltpu.VMEM(acc_shape, F32)],
        compiler_params=_params(('parallel', 'parallel', 'arbitrary')))(*ins)


def _mm(name, a, b, mode, o_dtype, res=None):
    if mode == 'tn':
        K, M = a.shape
    else:
        M, K = a.shape
    N = b.shape[0] if mode == 'nt' else b.shape[1]
    tm = _tile(M, (1024, 512, 256, 128))
    tn = _tile(N, (1024, 512, 256, 128))
    tk = _tile(K, (512, 256, 128))
    a_spec = (pl.BlockSpec((tk, tm), lambda m, n, k: (k, m)) if mode == 'tn'
              else pl.BlockSpec((tm, tk), lambda m, n, k: (m, k)))
    b_spec = (pl.BlockSpec((tn, tk), lambda m, n, k: (n, k)) if mode == 'nt'
              else pl.BlockSpec((tk, tn), lambda m, n, k: (k, n)))
    o_spec = pl.BlockSpec((tm, tn), lambda m, n, k: (m, n))
    return _mm_call(name, a, b, mode, (M // tm, N // tn, K // tk), a_spec, b_spec, o_spec, (M, N), o_dtype,
                    (tm, tn), res)


def _mm_colsharded(name, a, w, o_dtype, res=None):
    M, K = a.shape
    J, _, Nj = w.shape
    tm = _tile(M, (1024, 512, 256, 128))
    tn = _tile(Nj, (1408, 1024, 512, 256, 128))
    tk = _tile(K, (512, 256, 128))
    per = Nj // tn
    return _mm_call(name, a, w, 'nn', (M // tm, J * per, K // tk),
                    pl.BlockSpec((tm, tk), lambda m, n, k: (m, k)),
                    pl.BlockSpec((None, tk, tn), lambda m, n, k: (n // per, k, n % per)),
                    pl.BlockSpec((tm, tn), lambda m, n, k: (m, n)), (M, J * Nj), o_dtype, (tm, tn), res)


def _mm_colsharded_t(name, a, w, o_dtype):
    M = a.shape[0]
    J, K, Nj = w.shape
    tm = _tile(M, (1024, 512, 256, 128))
    tn = _tile(K, (1024, 512, 256, 128))
    tk = _tile(Nj, (1408, 1024, 512, 256, 128))
    per = Nj // tk
    return _mm_call(name, a, w, 'nt', (M // tm, K // tn, J * per),
                    pl.BlockSpec((tm, tk), lambda m, n, k: (m, k)),
                    pl.BlockSpec((None, tn, tk), lambda m, n, k: (k // per, n, k % per)),
                    pl.BlockSpec((tm, tn), lambda m, n, k: (m, n)), (M, K), o_dtype, (tm, tn))


def _mm_grad_colsharded(name, a, dy, J):
    S, M = a.shape
    Nj = dy.shape[1] // J
    tm = _tile(M, (1024, 512, 256, 128))
    tn = _tile(Nj, (1408, 1024, 512, 256, 128))
    tk = _tile(S, (512, 256, 128))
    per = Nj // tn
    return _mm_call(name, a, dy, 'tn', (M // tm, J * per, S // tk),
                    pl.BlockSpec((tk, tm), lambda m, n, k: (k, m)),
                    pl.BlockSpec((tk, tn), lambda m, n, k: (k, n)),
                    pl.BlockSpec((None, tm, tn), lambda m, n, k: (n // per, m, n % per)), (J, M, Nj), F32, (tm, tn))


def _rows_call(name, body, n_rows, tr, ins, outs):
    return pl.pallas_call(
        body, name=name, grid=(n_rows // tr,), in_specs=[s for _, s in ins], out_specs=[s for _, _, s in outs],
        out_shape=[jax.ShapeDtypeStruct(sh, dt) for sh, dt, _ in outs],
        compiler_params=_params(('arbitrary',)))(*[a for a, _ in ins])


def _rb(tr, w, cb=0):
    return pl.BlockSpec((tr, w), lambda i: (i, cb))


def _fb(shape):
    nd = len(shape)
    return pl.BlockSpec(shape, lambda i: (0,) * nd)


def norm_fwd(name, xv, g):
    S, D = xv.shape
    tr = _tile(S, (256, 128))

    def body(x_ref, g_ref, o_ref):
        o_ref[...] = _rms(x_ref[...], g_ref[...]).astype(BF16)

    return _rows_call(name, body, S, tr, [(xv, _rb(tr, D)), (g, _fb((1, D)))], [((S, D), BF16, _rb(tr, D))])[0]


def norm_bwd(name, xv, g, dy, res=None, want_dx=True):
    S, D = xv.shape
    tr = _tile(S, (256, 128))

    def body(*refs):
        if res is None:
            x_ref, g_ref, dy_ref = refs[:3]
            outs = refs[3:]
            r_ref = None
        else:
            x_ref, g_ref, dy_ref, r_ref = refs[:4]
            outs = refs[4:]
        dx, dg = _rms_bwd(x_ref[...], g_ref[...], dy_ref[...])
        if r_ref is not None:
            dx = dx + r_ref[...]
        if want_dx:
            outs[0][...] = dx
            outs[1][...] = dx.astype(BF16)
        _acc_out(outs[-1], pl.program_id(0) == 0, dg)

    ins = [(xv, _rb(tr, D)), (g, _fb((1, D))), (dy, _rb(tr, D))] + ([] if res is None else [(res, _rb(tr, D))])
    outs = ([((S, D), F32, _rb(tr, D)), ((S, D), BF16, _rb(tr, D))] if want_dx else []) + [((1, D), F32, _fb((1, D)))]
    return _rows_call(name, body, S, tr, ins, outs)


def qkv_fwd(proj, g_q, g_k, FW):
    S = proj.shape[0]
    H = FW // HEAD_DIM
    tr = _tile(S, (256, 128))

    def body(q_ref, k_ref, v_ref, gq_ref, gk_ref, qo, ko, vo):
        qo[...] = _heads(lambda t: (_rms(t, gq_ref[...]),), H, q_ref[...])[0].astype(BF16)
        ko[...] = _heads(lambda t: (_rms(t, gk_ref[...]),), H, k_ref[...])[0].astype(BF16)
        vo[...] = v_ref[...].astype(BF16)

    o = ((S, FW), BF16, _rb(tr, FW))
    return _rows_call('qkv_fwd', body, S, tr,
                      [(proj, _rb(tr, FW, 0)), (proj, _rb(tr, FW, 1)), (proj, _rb(tr, FW, 2)),
                       (g_q, _fb((1, HEAD_DIM))), (g_k, _fb((1, HEAD_DIM)))], [o, o, o])


def qkv_bwd(proj, g_q, g_k, dqn, dkn, FW):
    S = proj.shape[0]
    H = FW // HEAD_DIM
    tr = _tile(S, (256, 128))

    def body(q_ref, k_ref, gq_ref, gk_ref, dq_ref, dk_ref, dqo, dko, dgq, dgk):
        dq, gq = _heads(lambda t, d: _rms_bwd(t, gq_ref[...], d), H, q_ref[...], dq_ref[...])
        dk, gk = _heads(lambda t, d: _rms_bwd(t, gk_ref[...], d), H, k_ref[...], dk_ref[...])
        dqo[...] = dq.astype(BF16)
        dko[...] = dk.astype(BF16)
        first = pl.program_id(0) == 0
        _acc_out(dgq, first, gq)
        _acc_out(dgk, first, gk)

    o = ((S, FW), BF16, _rb(tr, FW))
    og = ((1, HEAD_DIM), F32, _fb((1, HEAD_DIM)))
    return _rows_call('qkv_bwd', body, S, tr,
                      [(proj, _rb(tr, FW, 0)), (proj, _rb(tr, FW, 1)), (g_q, _fb((1, HEAD_DIM))),
                       (g_k, _fb((1, HEAD_DIM))), (dqn, _rb(tr, FW)), (dkn, _rb(tr, FW))], [o, o, og, og])


def _tri(n, upper):
    r = lax.broadcasted_iota(jnp.int32, (n, n), 0)
    c = lax.broadcasted_iota(jnp.int32, (n, n), 1)
    return jnp.where((c >= r) if upper else (c <= r), 1.0, 0.0).astype(BF16)


def _blocked_cumsum(val, S, blk, reverse):
    tri = _tri(blk, reverse)
    order = range(S // blk - 1, -1, -1) if reverse else range(S // blk)
    carry = jnp.zeros((1, LANES), F32)
    outs = {}
    for bi in order:
        part = val[bi * blk:(bi + 1) * blk]
        acc = carry
        for piece in _split3(part):
            acc = acc + jnp.dot(tri, piece, preferred_element_type=F32)
        outs[bi] = acc
        carry = carry + jnp.sum(part, axis=0, keepdims=True)
    return jnp.concatenate([outs[bi] for bi in range(S // blk)], axis=0)


def fgate_fwd(f_raw, b_f_pad):
    S = f_raw.shape[0]
    blk = _tile(S, (256, 128))

    def body(f_ref, b_ref, c_ref):
        z = f_ref[...] + b_ref[...]
        c_ref[...] = _blocked_cumsum(-_softplus(-z), S, blk, False)

    return pl.pallas_call(body, name='fgate_fwd', grid=(1,), in_specs=[_fb((S, LANES)), _fb((1, LANES))],
                          out_specs=_fb((S, LANES)), out_shape=jax.ShapeDtypeStruct((S, LANES), F32),
                          compiler_params=_params(('arbitrary',)))(f_raw, b_f_pad)


def fgate_bwd(f_raw, b_f_pad, dc, H):
    S = f_raw.shape[0]
    blk = _tile(S, (256, 128))

    def body(f_ref, b_ref, dc_ref, df_ref, db_ref):
        z = f_ref[...] + b_ref[...]
        dlogf = _blocked_cumsum(dc_ref[...], S, blk, True)
        lane = lax.broadcasted_iota(jnp.int32, (S, LANES), 1)
        df = jnp.where(lane < H, dlogf * _sigmoid(-z), 0.0)
        df_ref[...] = df.astype(BF16)
        db_ref[...] = jnp.sum(df, axis=0, keepdims=True)

    return pl.pallas_call(body, name='fgate_bwd', grid=(1,),
                          in_specs=[_fb((S, LANES)), _fb((1, LANES)), _fb((S, LANES))],
                          out_specs=[_fb((S, LANES)), _fb((1, LANES))],
                          out_shape=[jax.ShapeDtypeStruct((S, LANES), BF16), jax.ShapeDtypeStruct((1, LANES), F32)],
                          compiler_params=_params(('arbitrary',)))(f_raw, b_f_pad, dc)


def _fox_logits(q, k, c_blk, ct_blk, h, i, j, T):
    s = lax.dot_general(q, k, _DIMS['nt'], preferred_element_type=F32) * (1.0 / math.sqrt(HEAD_DIM))
    lane = lax.broadcasted_iota(jnp.int32, c_blk.shape, 1)
    cq = jnp.sum(jnp.where(lane == h, c_blk, 0.0), axis=1, keepdims=True)
    sub = lax.broadcasted_iota(jnp.int32, ct_blk.shape, 0)
    ck = jnp.sum(jnp.where(sub == h, ct_blk, 0.0), axis=0, keepdims=True)
    rows = i * T + lax.broadcasted_iota(jnp.int32, (T, T), 0)
    cols = j * T + lax.broadcasted_iota(jnp.int32, (T, T), 1)
    return jnp.where(cols <= rows, s + cq - ck, -jnp.inf)


def fox_fwd(qn, kn, vb, c, ct, T):
    S, FW = qn.shape
    H = FW // HEAD_DIM
    Hp = ct.shape[0]
    n = S // T

    def body(q_ref, k_ref, v_ref, c_ref, ct_ref, o_ref, lse_ref, m_s, l_s, acc_s):
        h, i, j = pl.program_id(0), pl.program_id(1), pl.program_id(2)

        @pl.when(j == 0)
        def _():
            m_s[...] = jnp.full_like(m_s, -jnp.inf)
            l_s[...] = jnp.zeros_like(l_s)
            acc_s[...] = jnp.zeros_like(acc_s)

        @pl.when(j <= i)
        def _():
            s = _fox_logits(q_ref[...], k_ref[...], c_ref[...], ct_ref[...], h, i, j, T)
            m_new = jnp.maximum(m_s[...], jnp.max(s, axis=1, keepdims=True))
            alpha = jnp.exp(m_s[...] - m_new)
            p = jnp.exp(s - m_new)
            l_s[...] = alpha * l_s[...] + jnp.sum(p, axis=1, keepdims=True)
            acc_s[...] = alpha * acc_s[...] + jnp.dot(p.astype(BF16), v_ref[...], preferred_element_type=F32)
            m_s[...] = m_new

        @pl.when(j == i)
        def _():
            o_ref[...] = acc_s[...] / l_s[...]
            lse_ref[...] = jnp.broadcast_to(m_s[...] + jnp.log(l_s[...]), (T, LANES))

    qs = pl.BlockSpec((T, HEAD_DIM), lambda h, i, j: (i, h))
    ks = pl.BlockSpec((T, HEAD_DIM), lambda h, i, j: (jnp.minimum(j, i), h))
    return pl.pallas_call(
        body, name='fox_fwd', grid=(H, n, n),
        in_specs=[qs, ks, ks, pl.BlockSpec((T, LANES), lambda h, i, j: (i, 0)),
                  pl.BlockSpec((Hp, T), lambda h, i, j: (0, jnp.minimum(j, i)))],
        out_specs=[qs, pl.BlockSpec((None, T, LANES), lambda h, i, j: (h, i, 0))],
        out_shape=[jax.ShapeDtypeStruct((S, FW), F32), jax.ShapeDtypeStruct((H, S, LANES), F32)],
        scratch_shapes=[pltpu.VMEM((T, 1), F32), pltpu.VMEM((T, 1), F32), pltpu.VMEM((T, HEAD_DIM), F32)],
        compiler_params=_params(('parallel', 'parallel', 'arbitrary')))(qn, kn, vb, c, ct)


def _fox_p_ds(q_ref, k_ref, v_ref, do_ref, c_ref, ct_ref, lse_ref, dl_ref, h, i, j, T):
    s = _fox_logits(q_ref[...], k_ref[...], c_ref[...], ct_ref[...], h, i, j, T)
    p = jnp.exp(s - jnp.tile(lse_ref[...], (1, T // LANES)))
    dp = lax.dot_general(do_ref[...], v_ref[...], _DIMS['nt'], preferred_element_type=F32)
    ds = p * (dp - jnp.tile(dl_ref[...], (1, T // LANES)))
    return p, dp, ds


def fox_bwd_q(qn, kn, vb, do, c, ct, lse, dl, T):
    S, FW = qn.shape
    H = FW // HEAD_DIM
    Hp = ct.shape[0]
    n = S // T

    def body(q_ref, k_ref, v_ref, do_ref, c_ref, ct_ref, lse_ref, dl_ref, dq_ref, dl2_ref, acc_s, rs_s):
        h, i, j = pl.program_id(0), pl.program_id(1), pl.program_id(2)

        @pl.when(j == 0)
        def _():
            acc_s[...] = jnp.zeros_like(acc_s)
            rs_s[...] = jnp.zeros_like(rs_s)

        @pl.when(j <= i)
        def _():
            p, dp, ds = _fox_p_ds(q_ref, k_ref, v_ref, do_ref, c_ref, ct_ref, lse_ref, dl_ref, h, i, j, T)
            acc_s[...] += jnp.dot(ds.astype(BF16), k_ref[...], preferred_element_type=F32)
            rs_s[...] += jnp.sum(p * dp, axis=1, keepdims=True)

        @pl.when(j == i)
        def _():
            dq_ref[...] = acc_s[...] * (1.0 / math.sqrt(HEAD_DIM))
            dl2_ref[...] = jnp.broadcast_to(rs_s[...], (T, LANES))

    qs = pl.BlockSpec((T, HEAD_DIM), lambda h, i, j: (i, h))
    ks = pl.BlockSpec((T, HEAD_DIM), lambda h, i, j: (jnp.minimum(j, i), h))
    st = pl.BlockSpec((None, T, LANES), lambda h, i, j: (h, i, 0))
    return pl.pallas_call(
        body, name='fox_bwd_q', grid=(H, n, n),
        in_specs=[qs, ks, ks, qs, pl.BlockSpec((T, LANES), lambda h, i, j: (i, 0)),
                  pl.BlockSpec((Hp, T), lambda h, i, j: (0, jnp.minimum(j, i))), st, st],
        out_specs=[qs, st], out_shape=[jax.ShapeDtypeStruct((S, FW), F32), jax.ShapeDtypeStruct((H, S, LANES), F32)],
        scratch_shapes=[pltpu.VMEM((T, HEAD_DIM), F32), pltpu.VMEM((T, 1), F32)],
        compiler_params=_params(('parallel', 'parallel', 'arbitrary')))(qn, kn, vb, do, c, ct, lse, dl)


def fox_bwd_kv(qn, kn, vb, do, c, ct, lse, dl, T):
    S, FW = qn.shape
    H = FW // HEAD_DIM
    Hp = ct.shape[0]
    n = S // T

    def body(q_ref, k_ref, v_ref, do_ref, c_ref, ct_ref, lse_ref, dl_ref, dk_ref, dv_ref, dc_ref, dk_s, dv_s, dc_s):
        h, j, i = pl.program_id(0), pl.program_id(1), pl.program_id(2)

        @pl.when(i == 0)
        def _():
            dk_s[...] = jnp.zeros_like(dk_s)
            dv_s[...] = jnp.zeros_like(dv_s)
            dc_s[...] = jnp.zeros_like(dc_s)

        @pl.when(i >= j)
        def _():
            p, _, ds = _fox_p_ds(q_ref, k_ref, v_ref, do_ref, c_ref, ct_ref, lse_ref, dl_ref, h, i, j, T)
            dv_s[...] += lax.dot_general(p.astype(BF16), do_ref[...], _DIMS['tn'], preferred_element_type=F32)
            dk_s[...] += lax.dot_general(ds.astype(BF16), q_ref[...], _DIMS['tn'], preferred_element_type=F32)
            dc_s[...] += jnp.sum(ds, axis=0, keepdims=True)

        @pl.when(i == n - 1)
        def _():
            dk_ref[...] = dk_s[...] * (1.0 / math.sqrt(HEAD_DIM))
            dv_ref[...] = dv_s[...].astype(BF16)
            dc_ref[...] = -dc_s[...]

    qs = pl.BlockSpec((T, HEAD_DIM), lambda h, j, i: (jnp.maximum(i, j), h))
    ks = pl.BlockSpec((T, HEAD_DIM), lambda h, j, i: (j, h))
    st = pl.BlockSpec((None, T, LANES), lambda h, j, i: (h, jnp.maximum(i, j), 0))
    return pl.pallas_call(
        body, name='fox_bwd_kv', grid=(H, n, n),
        in_specs=[qs, ks, ks, qs, pl.BlockSpec((T, LANES), lambda h, j, i: (jnp.maximum(i, j), 0)),
                  pl.BlockSpec((Hp, T), lambda h, j, i: (0, j)), st, st],
        out_specs=[ks, ks, pl.BlockSpec((None, 1, T), lambda h, j, i: (h, 0, j))],
        out_shape=[jax.ShapeDtypeStruct((S, FW), F32), jax.ShapeDtypeStruct((S, FW), BF16),
                   jax.ShapeDtypeStruct((H, 1, S), F32)],
        scratch_shapes=[pltpu.VMEM((T, HEAD_DIM), F32), pltpu.VMEM((T, HEAD_DIM), F32), pltpu.VMEM((1, T), F32)],
        compiler_params=_params(('parallel', 'parallel', 'arbitrary')))(qn, kn, vb, do, c, ct, lse, dl)


def _shift_down(v, d, rows, fill):
    return jnp.where(rows >= d, pltpu.roll(v, d, 0), fill)


def _shift_up(v, d, rows, S, fill):
    return jnp.where(rows < S - d, pltpu.roll(v, S - d, 0), fill)


def _scan(a, b, rows, S, reverse):
    d = 1
    while d < S:
        if reverse:
            a_s, b_s = _shift_up(a, d, rows, S, 1.0), _shift_up(b, d, rows, S, 0.0)
        else:
            a_s, b_s = _shift_down(a, d, rows, 1.0), _shift_down(b, d, rows, 0.0)
        b = a * b_s + b
        a = a * a_s
        d *= 2
    return b


def _lru_forward(u, cw, cb, wra, bra, wri, bri, lam, rows):
    uc = cb + cw[CONV_W - 1] * u
    for d in range(1, CONV_W):
        uc = uc + cw[CONV_W - 1 - d] * _shift_down(u, d, rows, 0.0)
    ucb = uc.astype(BF16)
    r = _sigmoid(jnp.dot(ucb, wra.astype(BF16), preferred_element_type=F32) + bra)
    ig = _sigmoid(jnp.dot(ucb, wri.astype(BF16), preferred_element_type=F32) + bri)
    sp = _softplus(-lam)
    log_a = -LRU_C * r * sp
    a = jnp.exp(log_a)
    sq = jnp.sqrt(_neg_expm1(2.0 * log_a))
    iu = ig * uc
    hseq = _scan(a, sq * iu, rows, u.shape[0], False)
    return uc, ucb, r, ig, sp, a, sq, iu, hseq


def _lru_specs(S, n_u, n_g):
    col = lambda off: pl.BlockSpec((S, LANES), lambda cbk: (0, off + cbk))
    vec = pl.BlockSpec((1, LANES), lambda cbk: (0, cbk))
    mat = pl.BlockSpec((None, LANES, LANES), lambda cbk: (cbk, 0, 0))
    cw = pl.BlockSpec((CONV_W, LANES), lambda cbk: (0, cbk))
    return col, vec, mat, cw


def lru_fwd(proj, conv_w, conv_b, w_ra, b_ra, w_ri, b_ri, lam, u_off, g_off):
    S = proj.shape[0]
    nb = w_ra.shape[0]
    col, vec, mat, cws = _lru_specs(S, u_off, g_off)

    def body(u_ref, g_ref, cw_ref, cb_ref, wra_ref, bra_ref, wri_ref, bri_ref, lam_ref, y_ref):
        rows = lax.broadcasted_iota(jnp.int32, (S, LANES), 0)
        cw = [cw_ref[t:t + 1, :] for t in range(CONV_W)]
        hseq = _lru_forward(u_ref[...], cw, cb_ref[...], wra_ref[...], bra_ref[...], wri_ref[...],
                            bri_ref[...], lam_ref[...], rows)[-1]
        y_ref[...] = hseq * _gelu_and_grad(g_ref[...])[0]

    return pl.pallas_call(
        body, name='lru_fwd', grid=(nb,),
        in_specs=[col(u_off), col(g_off), cws, vec, mat, vec, mat, vec, vec], out_specs=col(0),
        out_shape=jax.ShapeDtypeStruct((S, nb * LANES), F32),
        compiler_params=_params(('parallel',)))(proj, proj, conv_w, conv_b, w_ra, b_ra, w_ri, b_ri, lam)


def lru_bwd(proj, dy, conv_w, conv_b, w_ra, b_ra, w_ri, b_ri, lam, u_off, g_off):
    S = proj.shape[0]
    nb = w_ra.shape[0]
    LW = nb * LANES
    col, vec, mat, cws = _lru_specs(S, u_off, g_off)

    def body(u_ref, g_ref, dy_ref, cw_ref, cb_ref, wra_ref, bra_ref, wri_ref, bri_ref, lam_ref,
             du_ref, dg_ref, dcw_ref, dcb_ref, dwra_ref, dbra_ref, dwri_ref, dbri_ref, dlam_ref):
        rows = lax.broadcasted_iota(jnp.int32, (S, LANES), 0)
        u, lam_v = u_ref[...], lam_ref[...]
        cw = [cw_ref[t:t + 1, :] for t in range(CONV_W)]
        wra, wri = wra_ref[...].astype(BF16), wri_ref[...].astype(BF16)
        uc, ucb, r, ig, sp, a, sq, iu, hseq = _lru_forward(u, cw, cb_ref[...], wra, bra_ref[...], wri, bri_ref[...],
                                                           lam_v, rows)
        gl, dgl = _gelu_and_grad(g_ref[...])
        dy_v = dy_ref[...]
        dg_ref[...] = (dy_v * hseq * dgl).astype(BF16)
        G = _scan(_shift_up(a, 1, rows, S, 0.0), dy_v * gl, rows, S, True)
        da = G * _shift_down(hseq, 1, rows, 0.0)
        diu = G * sq
        dsq = G * iu
        dlog_a = da * a - dsq * a * a / jnp.maximum(sq, 1e-30)
        dr = dlog_a * (-LRU_C * sp)
        dsp = jnp.sum(dlog_a * (-LRU_C * r), axis=0, keepdims=True)
        dlam_ref[...] = -dsp * _sigmoid(-lam_v)
        dzr = dr * r * (1.0 - r)
        dzi = diu * uc * ig * (1.0 - ig)
        dzrb, dzib = dzr.astype(BF16), dzi.astype(BF16)
        duc = (diu * ig + lax.dot_general(dzrb, wra, _DIMS['nt'], preferred_element_type=F32)
               + lax.dot_general(dzib, wri, _DIMS['nt'], preferred_element_type=F32))
        dwra_ref[...] = lax.dot_general(ucb, dzrb, _DIMS['tn'], preferred_element_type=F32)
        dwri_ref[...] = lax.dot_general(ucb, dzib, _DIMS['tn'], preferred_element_type=F32)
        dbra_ref[...] = jnp.sum(dzr, axis=0, keepdims=True)
        dbri_ref[...] = jnp.sum(dzi, axis=0, keepdims=True)
        dcb_ref[...] = jnp.sum(duc, axis=0, keepdims=True)
        du = cw[CONV_W - 1] * duc
        dcw_ref[CONV_W - 1:CONV_W, :] = jnp.sum(duc * u, axis=0, keepdims=True)
        for d in range(1, CONV_W):
            du = du + cw[CONV_W - 1 - d] * _shift_up(duc, d, rows, S, 0.0)
            dcw_ref[CONV_W - 1 - d:CONV_W - d, :] = jnp.sum(duc * _shift_down(u, d, rows, 0.0), axis=0, keepdims=True)
        du_ref[...] = du.astype(BF16)

    sd = jax.ShapeDtypeStruct
    return pl.pallas_call(
        body, name='lru_bwd', grid=(nb,),
        in_specs=[col(u_off), col(g_off), col(0), cws, vec, mat, vec, mat, vec, vec],
        out_specs=[col(0), col(0), cws, vec, mat, vec, mat, vec, vec],
        out_shape=[sd((S, LW), BF16), sd((S, LW), BF16), sd((CONV_W, LW), F32), sd((1, LW), F32),
                   sd((nb, LANES, LANES), F32), sd((1, LW), F32), sd((nb, LANES, LANES), F32), sd((1, LW), F32),
                   sd((1, LW), F32)],
        compiler_params=_params(('parallel',)))(proj, proj, dy, conv_w, conv_b, w_ra, b_ra, w_ri, b_ri, lam)


def mix_fwd(o_fox, y_lru, g_fox, g_lru):
    S, FW = o_fox.shape
    tr = _tile(S, (256, 128))

    def body(o_ref, y_ref, gf_ref, gl_ref, m_ref):
        m_ref[...] = jnp.concatenate([_rms(o_ref[...], gf_ref[...]), _rms(y_ref[...], gl_ref[...])],
                                     axis=1).astype(BF16)

    return _rows_call('mix_fwd', body, S, tr,
                      [(o_fox, _rb(tr, FW)), (y_lru, _rb(tr, FW)), (g_fox, _fb((1, FW))), (g_lru, _fb((1, FW)))],
                      [((S, 2 * FW), BF16, _rb(tr, 2 * FW))])[0]


def mix_bwd(o_fox, y_lru, g_fox, g_lru, dmix):
    S, FW = o_fox.shape
    H = FW // HEAD_DIM
    tr = _tile(S, (256, 128))

    def body(o_ref, y_ref, gf_ref, gl_ref, df_ref, dl_ref, do_ref, dlt_ref, dy_ref, dgf_ref, dgl_ref):
        o = o_ref[...]
        do, dgf = _rms_bwd(o, gf_ref[...], df_ref[...])
        dyl, dgl = _rms_bwd(y_ref[...], gl_ref[...], dl_ref[...])
        do_ref[...] = do.astype(BF16)
        dy_ref[...] = dyl
        prod = do * o
        for h in range(H):
            dlt_ref[h] = jnp.broadcast_to(
                jnp.sum(prod[:, h * HEAD_DIM:(h + 1) * HEAD_DIM], axis=1, keepdims=True), (tr, LANES))
        first = pl.program_id(0) == 0
        _acc_out(dgf_ref, first, dgf)
        _acc_out(dgl_ref, first, dgl)

    g = _fb((1, FW))
    return _rows_call('mix_bwd', body, S, tr,
                      [(o_fox, _rb(tr, FW)), (y_lru, _rb(tr, FW)), (g_fox, g), (g_lru, g), (dmix, _rb(tr, FW, 0)),
                       (dmix, _rb(tr, FW, 1))],
                      [((S, FW), BF16, _rb(tr, FW)), ((H, S, LANES), F32, pl.BlockSpec((H, tr, LANES), lambda i: (0, i, 0))),
                       ((S, FW), F32, _rb(tr, FW)), ((1, FW), F32, g), ((1, FW), F32, g)])


def _xattn_heads(cq_raw, ckv, g_cq, g_ck, XW):
    out = []
    for h in range(XW // HEAD_DIM):
        sl = slice(h * HEAD_DIM, (h + 1) * HEAD_DIM)
        out.append((cq_raw[:, sl], _rms(cq_raw[:, sl], g_cq), ckv[:, sl], _rms(ckv[:, sl], g_ck),
                    ckv[:, XW + h * HEAD_DIM:XW + (h + 1) * HEAD_DIM].astype(BF16)))
    return out


def xattn_fwd(cq_raw, ckv, g_cq, g_ck):
    S, XW = cq_raw.shape
    M = ckv.shape[0]
    tr = _tile(S, (512, 256, 128))

    def body(q_ref, kv_ref, gq_ref, gk_ref, o_ref):
        outs = []
        for _, qn, _, kn, v in _xattn_heads(q_ref[...], kv_ref[...], gq_ref[...], gk_ref[...], XW):
            s = lax.dot_general(qn.astype(BF16), kn.astype(BF16), _DIMS['nt'], preferred_element_type=F32)
            s = s / math.sqrt(HEAD_DIM)
            p = jnp.exp(s - jnp.max(s, axis=1, keepdims=True))
            p = p / jnp.sum(p, axis=1, keepdims=True)
            outs.append(jnp.dot(p.astype(BF16), v, preferred_element_type=F32))
        o_ref[...] = jnp.concatenate(outs, axis=1).astype(BF16)

    g = _fb((1, HEAD_DIM))
    return _rows_call('xattn_fwd', body, S, tr,
                      [(cq_raw, _rb(tr, XW)), (ckv, _fb((M, 2 * XW))), (g_cq, g), (g_ck, g)],
                      [((S, XW), BF16, _rb(tr, XW))])[0]


def xattn_bwd(cq_raw, ckv, g_cq, g_ck, do):
    S, XW = cq_raw.shape
    M = ckv.shape[0]
    tr = _tile(S, (512, 256, 128))
    n = S // tr

    def body(q_ref, kv_ref, gq_ref, gk_ref, do_ref, dq_ref, dkv_ref, dgq_ref, dgk_ref):
        i = pl.program_id(0)
        do_v = do_ref[...]
        dqs, dkn, dvs = [], [], []
        dgq = jnp.zeros((1, HEAD_DIM), F32)
        for h, (q_raw, qn, _, kn, v) in enumerate(_xattn_heads(q_ref[...], kv_ref[...], gq_ref[...], gk_ref[...], XW)):
            qb, kb = qn.astype(BF16), kn.astype(BF16)
            doh = do_v[:, h * HEAD_DIM:(h + 1) * HEAD_DIM]
            s = lax.dot_general(qb, kb, _DIMS['nt'], preferred_element_type=F32) / math.sqrt(HEAD_DIM)
            p = jnp.exp(s - jnp.max(s, axis=1, keepdims=True))
            p = p / jnp.sum(p, axis=1, keepdims=True)
            dp = lax.dot_general(doh, v, _DIMS['nt'], preferred_element_type=F32)
            ds = (p * (dp - jnp.sum(p * dp, axis=1, keepdims=True)) / math.sqrt(HEAD_DIM)).astype(BF16)
            dvs.append(lax.dot_general(p.astype(BF16), doh, _DIMS['tn'], preferred_element_type=F32))
            dkn.append(lax.dot_general(ds, qb, _DIMS['tn'], preferred_element_type=F32))
            dq, g1 = _rms_bwd(q_raw, gq_ref[...], jnp.dot(ds, kb, preferred_element_type=F32))
            dqs.append(dq)
            dgq = dgq + g1
        dq_ref[...] = jnp.concatenate(dqs, axis=1).astype(BF16)
        first = i == 0
        _acc_out(dgq_ref, first, dgq)
        _acc_out(dkv_ref, first, jnp.concatenate(dkn + dvs, axis=1))

        @pl.when(i == n - 1)
        def _():
            kv = kv_ref[...]
            acc = dkv_ref[...]
            dk, gk = _heads(lambda t, d: _rms_bwd(t, gk_ref[...], d), XW // HEAD_DIM, kv[:, :XW], acc[:, :XW])
            dkv_ref[:, :XW] = dk
            dgk_ref[...] = gk

    g = _fb((1, HEAD_DIM))
    return _rows_call('xattn_bwd', body, S, tr,
                      [(cq_raw, _rb(tr, XW)), (ckv, _fb((M, 2 * XW))), (g_cq, g), (g_ck, g), (do, _rb(tr, XW))],
                      [((S, XW), BF16, _rb(tr, XW)), ((M, 2 * XW), F32, _fb((M, 2 * XW))), ((1, HEAD_DIM), F32, g),
                       ((1, HEAD_DIM), F32, g)])


def swiglu_fwd(gu, F):
    S = gu.shape[0]
    tr = _tile(S, (256, 128))
    tf = _tile(F, (1408, 1024, 512, 256, 128))
    nf = F // tf

    def body(g_ref, u_ref, a_ref):
        g = g_ref[...]
        a_ref[...] = (g * _sigmoid(g) * u_ref[...]).astype(BF16)

    return pl.pallas_call(
        body, name='swiglu_fwd', grid=(S // tr, nf),
        in_specs=[pl.BlockSpec((tr, tf), lambda i, n: (i, n)), pl.BlockSpec((tr, tf), lambda i, n: (i, n + nf))],
        out_specs=pl.BlockSpec((tr, tf), lambda i, n: (i, n)), out_shape=jax.ShapeDtypeStruct((S, F), BF16),
        compiler_params=_params(('parallel', 'parallel')))(gu, gu)


def swiglu_bwd(gu, dact, F):
    S = gu.shape[0]
    tr = _tile(S, (256, 128))
    tf = _tile(F, (1408, 1024, 512, 256, 128))
    nf = F // tf

    def body(g_ref, u_ref, da_ref, o_ref):
        n = pl.program_id(1)
        g, da = g_ref[...], da_ref[...]
        sg = _sigmoid(g)

        @pl.when(n < nf)
        def _():
            o_ref[...] = (da * u_ref[...] * sg * (1.0 + g * (1.0 - sg))).astype(BF16)

        @pl.when(n >= nf)
        def _():
            o_ref[...] = (da * g * sg).astype(BF16)

    return pl.pallas_call(
        body, name='swiglu_bwd', grid=(S // tr, 2 * nf),
        in_specs=[pl.BlockSpec((tr, tf), lambda i, n: (i, n % nf)), pl.BlockSpec((tr, tf), lambda i, n: (i, n % nf + nf)),
                  pl.BlockSpec((tr, tf), lambda i, n: (i, n % nf))],
        out_specs=pl.BlockSpec((tr, tf), lambda i, n: (i, n)), out_shape=jax.ShapeDtypeStruct((S, 2 * F), BF16),
        compiler_params=_params(('parallel', 'arbitrary')))(gu, gu, dact)


def loss_head(y, target):
    S, D = y.shape
    tr = _tile(S, (256, 128))

    def body(y_ref, t_ref, d_ref, db_ref, l_ref):
        err = y_ref[...] - t_ref[...]
        d = err * (1.0 / D)
        d_ref[...] = d
        db_ref[...] = d.astype(BF16)
        part = jnp.sum(jnp.sum(err * err, axis=1, keepdims=True), axis=0, keepdims=True) * (0.5 / D)
        _acc_out(l_ref, pl.program_id(0) == 0, jnp.broadcast_to(part, (1, LANES)))

    return _rows_call('loss_head', body, S, tr, [(y, _rb(tr, D)), (target, _rb(tr, D))],
                      [((S, D), F32, _rb(tr, D)), ((S, D), BF16, _rb(tr, D)), ((1, LANES), F32, _fb((1, LANES)))])


def _adamw_math(w, gv, m, v):
    mn = ADAM_B1 * m + (1.0 - ADAM_B1) * gv
    vn = ADAM_B2 * v + (1.0 - ADAM_B2) * (gv * gv)
    m_hat = mn / (1.0 - ADAM_B1 ** ADAM_STEP)
    v_hat = vn / (1.0 - ADAM_B2 ** ADAM_STEP)
    return -ADAM_LR * (m_hat / (jnp.sqrt(v_hat) + ADAM_EPS) + ADAM_WD * w), mn, vn


def adamw(name, w, g, m, v):
    R, C = w.shape
    tr = _row_tile(R, C)

    def body(w_ref, g_ref, m_ref, v_ref, d_ref, mo_ref, vo_ref):
        d_ref[...], mo_ref[...], vo_ref[...] = _adamw_math(w_ref[...], g_ref[...], m_ref[...], v_ref[...])

    spec = _rb(tr, C)
    return _rows_call(name, body, R, tr, [(w, spec), (g, spec), (m, spec), (v, spec)], [((R, C), F32, spec)] * 3)


def adamw_halves(name, w, mine, other, m, v, c_idx):
    R, C = w.shape
    hr = R // 2
    tr = _row_tile(hr, C)

    def body(c_ref, w_ref, a_ref, b_ref, m_ref, v_ref, g_ref, d_ref, mo_ref, vo_ref):
        gv = jnp.where(pl.program_id(0) == c_ref[0], a_ref[...], b_ref[...])
        g_ref[...] = gv
        d_ref[...], mo_ref[...], vo_ref[...] = _adamw_math(w_ref[...], gv, m_ref[...], v_ref[...])

    full = pl.BlockSpec((None, tr, C), lambda hh, i, c_ref: (hh, i, 0))
    half = pl.BlockSpec((tr, C), lambda hh, i, c_ref: (i, 0))
    outs = pl.pallas_call(
        body, name=name,
        grid_spec=pltpu.PrefetchScalarGridSpec(num_scalar_prefetch=1, grid=(2, hr // tr),
                                               in_specs=[full, half, half, full, full], out_specs=[full] * 4),
        out_shape=[jax.ShapeDtypeStruct((2, hr, C), F32)] * 4,
        compiler_params=_params(('parallel', 'parallel')))(
            c_idx, w.reshape(2, hr, C), mine, other, m.reshape(2, hr, C), v.reshape(2, hr, C))
    return [o.reshape(R, C) for o in outs]


def _place():
    x, y, c = lax.axis_index('x'), lax.axis_index('y'), lax.axis_index('c')
    return x, y, c, [(1 - x, y), (x, 1 - y), (1 - x, 1 - y)]


def _rcopy(src, dst, ssem, rsem, dev):
    return pltpu.make_async_remote_copy(src_ref=src, dst_ref=dst, send_sem=ssem, recv_sem=rsem, device_id=dev,
                                        device_id_type=MESH)


def gather_weights(shards, whole):
    nT = len(shards)

    def body(*refs):
        ins, outs = refs[:nT], refs[nT:2 * nT]
        ssem, rsem = refs[2 * nT:]
        x, y, c, chips = _place()
        me = 2 * x + y

        def part(t, half):
            hr = shards[t].shape[0] // 2
            return pl.ds(0, shards[t].shape[0]) if whole[t] else pl.ds(half * hr, hr)

        sends = []
        for t in range(nT):
            for k, (px, py) in enumerate(chips):
                cp = _rcopy(ins[t].at[part(t, c)], outs[t].at[me, part(t, c)], ssem.at[6 * t + k], rsem.at[6 * t + k],
                            (px, py, c))
                cp.start()
                sends.append(cp)
        for t in range(nT):
            for k, (px, py) in enumerate(chips):
                blk = outs[t].at[2 * px + py, part(t, c)]
                _rcopy(blk, blk, ssem.at[6 * t + k], rsem.at[6 * t + k], (px, py, c)).wait_recv()
                if not whole[t]:
                    cp = _rcopy(blk, blk, ssem.at[6 * t + 3 + k], rsem.at[6 * t + 3 + k], (x, y, 1 - c))
                    cp.start()
                    sends.append(cp)
        for t in range(nT):
            if not whole[t]:
                for k, (px, py) in enumerate(chips):
                    blk = outs[t].at[2 * px + py, part(t, 1 - c)]
                    _rcopy(blk, blk, ssem.at[6 * t + 3 + k], rsem.at[6 * t + 3 + k], (x, y, 1 - c)).wait_recv()
        for cp in sends:
            cp.wait_send()

    return pl.pallas_call(
        body, name='gather_weights', in_specs=[ANY] * nT, out_specs=[ANY] * nT,
        out_shape=[jax.ShapeDtypeStruct((N_CHIPS,) + s.shape, s.dtype) for s in shards],
        scratch_shapes=[pltpu.SemaphoreType.DMA((6 * nT,)), pltpu.SemaphoreType.DMA((6 * nT,))],
        compiler_params=_params())(*shards)


def pair_exchange(grads):
    nT = len(grads)

    def body(*refs):
        ins, outs = refs[:nT], refs[nT:2 * nT]
        ssem, rsem = refs[2 * nT:]
        x, y, c, _ = _place()
        cps = [_rcopy(ins[t].at[:, 1 - c], outs[t], ssem.at[t], rsem.at[t], (x, y, 1 - c)) for t in range(nT)]
        for cp in cps:
            cp.start()
        for cp in cps:
            cp.wait()

    return pl.pallas_call(
        body, name='pair_exchange', in_specs=[ANY] * nT, out_specs=[ANY] * nT,
        out_shape=[jax.ShapeDtypeStruct((g.shape[0],) + g.shape[2:], g.dtype) for g in grads],
        scratch_shapes=[pltpu.SemaphoreType.DMA((nT,)), pltpu.SemaphoreType.DMA((nT,))],
        compiler_params=_params())(*grads)


def pair_add(name, g, got, c_idx):
    J, _, hr, C = g.shape
    tr = _row_tile(hr, C, min_rows=16)

    def body(c_ref, g_ref, r_ref, o_ref):
        o_ref[...] = (g_ref[...] + r_ref[...]).astype(BF16)

    return pl.pallas_call(
        body, name=name,
        grid_spec=pltpu.PrefetchScalarGridSpec(
            num_scalar_prefetch=1, grid=(J, hr // tr),
            in_specs=[pl.BlockSpec((None, None, tr, C), lambda j, i, c_ref: (j, c_ref[0], i, 0)),
                      pl.BlockSpec((None, tr, C), lambda j, i, c_ref: (j, i, 0))],
            out_specs=pl.BlockSpec((None, tr, C), lambda j, i, c_ref: (j, i, 0))),
        out_shape=jax.ShapeDtypeStruct((J, hr, C), BF16),
        compiler_params=_params(('parallel', 'parallel')))(c_idx, g, got)


def chip_exchange(parts):
    nT = len(parts)

    def body(*refs):
        ins, outs = refs[:nT], refs[nT:2 * nT]
        ssem, rsem = refs[2 * nT:]
        x, y, c, chips = _place()
        sends = []
        for t in range(nT):
            for k, (px, py) in enumerate(chips):
                cp = _rcopy(ins[t].at[2 * px + py], outs[t].at[k], ssem.at[3 * t + k], rsem.at[3 * t + k], (px, py, c))
                cp.start()
                sends.append(cp)
        for t in range(nT):
            for k, (px, py) in enumerate(chips):
                blk = outs[t].at[k]
                _rcopy(blk, blk, ssem.at[3 * t + k], rsem.at[3 * t + k], (px, py, c)).wait_recv()
        for cp in sends:
            cp.wait_send()

    return pl.pallas_call(
        body, name='chip_exchange', in_specs=[ANY] * nT, out_specs=[ANY] * nT,
        out_shape=[jax.ShapeDtypeStruct((3,) + p.shape[1:], p.dtype) for p in parts],
        scratch_shapes=[pltpu.SemaphoreType.DMA((3 * nT,)), pltpu.SemaphoreType.DMA((3 * nT,))],
        compiler_params=_params())(*parts)


def sum_chips(name, parts, landed, chip_idx):
    _, hr, C = parts.shape
    tr = _row_tile(hr, C, min_rows=16)

    def body(me_ref, p_ref, l_ref, o_ref):
        acc = p_ref[...].astype(F32)
        for k in range(3):
            acc = acc + l_ref[k].astype(F32)
        o_ref[...] = acc

    return pl.pallas_call(
        body, name=name,
        grid_spec=pltpu.PrefetchScalarGridSpec(
            num_scalar_prefetch=1, grid=(hr // tr,),
            in_specs=[pl.BlockSpec((None, tr, C), lambda i, me_ref: (me_ref[0], i, 0)),
                      pl.BlockSpec((3, tr, C), lambda i, me_ref: (0, i, 0))],
            out_specs=pl.BlockSpec((tr, C), lambda i, me_ref: (i, 0))),
        out_shape=jax.ShapeDtypeStruct((hr, C), F32),
        compiler_params=_params(('parallel',)))(chip_idx, parts, landed)


def pair_join(halves):
    nT = len(halves)

    def body(*refs):
        ins, outs = refs[:nT], refs[nT:2 * nT]
        ssem, rsem = refs[2 * nT:]
        x, y, c, _ = _place()
        cps = [_rcopy(ins[t], outs[t], ssem.at[t], rsem.at[t], (x, y, 1 - c)) for t in range(nT)]
        for cp in cps:
            cp.start()
        for cp in cps:
            cp.wait()

    return pl.pallas_call(
        body, name='pair_join', in_specs=[ANY] * nT, out_specs=[ANY] * nT,
        out_shape=[jax.ShapeDtypeStruct(h.shape, h.dtype) for h in halves],
        scratch_shapes=[pltpu.SemaphoreType.DMA((nT,)), pltpu.SemaphoreType.DMA((nT,))],
        compiler_params=_params())(*halves)


def allreduce_small(buf):
    R = buf.shape[0]
    VM = pl.BlockSpec(memory_space=pltpu.VMEM)

    def body(x_ref, o_ref, all_ref, ssem, rsem, lsem):
        x, y, c, chips = _place()
        me, sibling = (x, y, c), (x, y, 1 - c)

        def rows(px, py, pc):
            return all_ref.at[pl.ds((4 * px + 2 * py + pc) * R, R), :]

        def copy(k, block, to, src=None):
            return _rcopy(rows(*block) if src is None else src, rows(*block), ssem.at[k], rsem.at[k], to)

        mine = pltpu.make_async_copy(x_ref, rows(*me), lsem)
        mine.start()
        first = [copy(0, me, sibling, src=x_ref)]
        first += [copy(1 + k, me, (*chip, c), src=x_ref) for k, chip in enumerate(chips)]
        for cp in first:
            cp.start()
        passed = [copy(4 + k, (*chip, c), sibling) for k, chip in enumerate(chips)]
        for k, chip in enumerate(chips):
            copy(1 + k, (*chip, c), me).wait_recv()
            passed[k].start()
        copy(0, sibling, me).wait_recv()
        for k, chip in enumerate(chips):
            copy(4 + k, (*chip, 1 - c), me).wait_recv()
        for cp in first + passed:
            cp.wait_send()
        mine.wait()
        acc = all_ref[0:R, :]
        for d in range(1, 8):
            acc = acc + all_ref[d * R:(d + 1) * R, :]
        o_ref[...] = acc

    return pl.pallas_call(
        body, name='allreduce_small', in_specs=[VM], out_specs=VM, out_shape=jax.ShapeDtypeStruct((R, LANES), F32),
        scratch_shapes=[pltpu.VMEM((8 * R, LANES), F32), pltpu.SemaphoreType.DMA((7,)), pltpu.SemaphoreType.DMA((7,)),
                        pltpu.SemaphoreType.DMA],
        compiler_params=_params())(buf)


_PACK = 8 * LANES


def _pack(arrs):
    flat = []
    for a in arrs:
        v = a.reshape(-1).astype(F32)
        flat.append(jnp.pad(v, (0, (-v.shape[0]) % _PACK)))
    return jnp.concatenate(flat).reshape(-1, LANES)


def _unpack(buf, shapes):
    out, off = [], 0
    flat = buf.reshape(-1)
    for sh in shapes:
        n = math.prod(sh)
        out.append(flat[off:off + n].reshape(sh))
        off += n + (-n) % _PACK
    return out


def kernel(x, mem, g_mix, w_in, b_f, g_q, g_k, conv_w, conv_b, w_ra, b_ra, w_ri, b_ri, lam, g_fox_out, g_lru_out, w_out, g_xattn, g_mem, w_cq, w_ckv, g_cq, g_ck, w_co, g_ffn, w_gate_up, w_down, loss_target, m_g_mix, m_w_in, m_b_f, m_g_q, m_g_k, m_conv_w, m_conv_b, m_w_ra, m_b_ra, m_w_ri, m_b_ri, m_lam, m_g_fox_out, m_g_lru_out, m_w_out, m_g_xattn, m_g_mem, m_w_cq, m_w_ckv, m_g_cq, m_g_ck, m_w_co, m_g_ffn, m_w_gate_up, m_w_down, v_g_mix, v_w_in, v_b_f, v_g_q, v_g_k, v_conv_w, v_conv_b, v_w_ra, v_b_ra, v_w_ri, v_b_ri, v_lam, v_g_fox_out, v_g_lru_out, v_w_out, v_g_xattn, v_g_mem, v_w_cq, v_w_ckv, v_g_cq, v_g_ck, v_w_co, v_g_ffn, v_w_gate_up, v_w_down):
    given = dict(locals())
    W = {n: given[n][0] for n in WEIGHTS}
    M1 = {n: given['m_' + n][0] for n in WEIGHTS}
    V1 = {n: given['v_' + n][0] for n in WEIGHTS}
    xs, ms, tgt = x[0], mem[0], loss_target[0]
    S, D = xs.shape
    H = W['b_f'].shape[0]
    FW = H * HEAD_DIM
    LW = W['lam'].shape[0]
    nb = W['w_ra'].shape[0]
    XW = W['w_cq'].shape[1]
    F = W['w_down'].shape[0] * N_CHIPS
    IN_W = W['w_in'].shape[1] * N_CHIPS
    assert FW == LW and LW == nb * LANES and IN_W == 3 * FW + H + 2 * LW and H <= 8
    T = _tile(S, (512, 256, 128))
    c_idx = lax.axis_index('c').astype(jnp.int32).reshape(1)
    chip = 2 * lax.axis_index('x') + lax.axis_index('y')
    chip_idx = chip.astype(jnp.int32).reshape(1)
    vec = lambda n: W[n].reshape(1, -1)

    own = [W[n].astype(BF16) for n in BIG] + [W['conv_w'].reshape(-1, LANES)]
    gathered = gather_weights(own, [False] * len(BIG) + [True])
    gathered = [lax.dynamic_update_index_in_dim(g, o, chip, 0) for g, o in zip(gathered, own)]
    G = dict(zip(BIG, gathered[:-1]))
    conv_full = gathered[-1].reshape(N_CHIPS, CONV_W, LW // N_CHIPS).transpose(1, 0, 2).reshape(CONV_W, LW)
    w_in_full = G['w_in'].transpose(1, 0, 2).reshape(D, IN_W)
    w5 = jnp.concatenate([w_in_full[:, :3 * FW], w_in_full[:, 3 * FW + H:]], axis=1)
    wf = jnp.pad(w_in_full[:, 3 * FW:3 * FW + H], ((0, 0), (0, LANES - H)))
    w_out_f = G['w_out'].reshape(2 * FW, D)
    w_cq_f = G['w_cq'].reshape(D, XW)
    w_ckv_f = G['w_ckv'].reshape(D, 2 * XW)
    w_down_f = G['w_down'].reshape(F, D)
    b_f_pad = jnp.pad(vec('b_f'), ((0, 0), (0, LANES - H)))
    u_off, g_off = 3 * FW // LANES, (3 * FW + LW) // LANES

    h1 = norm_fwd('norm_mix', xs, vec('g_mix'))
    proj = _mm('proj_in', h1, w5, 'nn', F32)
    f_raw = _mm('proj_f', h1, wf, 'nn', F32)
    qn, kn, vb = qkv_fwd(proj, vec('g_q'), vec('g_k'), FW)
    cc = fgate_fwd(f_raw, b_f_pad)
    ct = cc[:, :8].T
    o_fox, lse = fox_fwd(qn, kn, vb, cc, ct, T)
    lru_w = (conv_full, vec('conv_b'), W['w_ra'], vec('b_ra'), W['w_ri'], vec('b_ri'), vec('lam'))
    y_lru = lru_fwd(proj, *lru_w, u_off, g_off)
    mixn = mix_fwd(o_fox, y_lru, vec('g_fox_out'), vec('g_lru_out'))
    x1 = _mm('proj_out', mixn, w_out_f, 'nn', F32, res=xs)

    hq = norm_fwd('norm_xq', x1, vec('g_xattn'))
    mn = norm_fwd('norm_mem', ms, vec('g_mem'))
    cq_raw = _mm('proj_cq', hq, w_cq_f, 'nn', F32)
    ckv = _mm('proj_ckv', mn, w_ckv_f, 'nn', F32)
    o_x = xattn_fwd(cq_raw, ckv, vec('g_cq'), vec('g_ck'))
    x2 = _mm_colsharded('proj_co', o_x, G['w_co'], F32, res=x1)

    hf = norm_fwd('norm_ffn', x2, vec('g_ffn'))
    gu = _mm_colsharded('proj_gate_up', hf, G['w_gate_up'], F32)
    act = swiglu_fwd(gu, F)
    yv = _mm('proj_down', act, w_down_f, 'nn', F32, res=x2)
    dy, dyb, loss_blk = loss_head(yv, tgt)
    loss = lax.psum(loss_blk[0, 0], ('x', 'y', 'c'))

    gw = {}
    dact = _mm('bwd_down_x', dyb, w_down_f, 'nt', F32)
    gw['w_down'] = _mm('bwd_down_w', act, dyb, 'tn', F32).reshape(N_CHIPS, F // N_CHIPS, D)
    dgu = swiglu_bwd(gu, dact, F)
    dhf = _mm_colsharded_t('bwd_gate_up_x', dgu, G['w_gate_up'], F32)
    gw['w_gate_up'] = _mm_grad_colsharded('bwd_gate_up_w', hf, dgu, N_CHIPS)
    dx2, dx2b, gw['g_ffn'] = norm_bwd('norm_ffn_bwd', x2, vec('g_ffn'), dhf, res=dy)

    do_x = _mm_colsharded_t('bwd_co_x', dx2b, G['w_co'], BF16)
    gw['w_co'] = _mm_grad_colsharded('bwd_co_w', o_x, dx2b, N_CHIPS)
    dcq_raw, dckv, gw['g_cq'], gw['g_ck'] = xattn_bwd(cq_raw, ckv, vec('g_cq'), vec('g_ck'), do_x)
    dhq = _mm('bwd_cq_x', dcq_raw, w_cq_f, 'nt', F32)
    gw['w_cq'] = _mm('bwd_cq_w', hq, dcq_raw, 'tn', F32).reshape(N_CHIPS, D // N_CHIPS, XW)
    dmn = _mm('bwd_ckv_x', dckv, w_ckv_f, 'nt', F32)
    gw['w_ckv'] = _mm('bwd_ckv_w', mn, dckv, 'tn', F32).reshape(N_CHIPS, D // N_CHIPS, 2 * XW)
    (gw['g_mem'],) = norm_bwd('norm_mem_bwd', ms, vec('g_mem'), dmn, want_dx=False)
    dx1, dx1b, gw['g_xattn'] = norm_bwd('norm_xq_bwd', x1, vec('g_xattn'), dhq, res=dx2)

    dmix = _mm('bwd_out_x', dx1b, w_out_f, 'nt', F32)
    gw['w_out'] = _mm('bwd_out_w', mixn, dx1b, 'tn', F32).reshape(N_CHIPS, 2 * FW // N_CHIPS, D)
    do_fox, delta, dy_lru, gw['g_fox_out'], gw['g_lru_out'] = mix_bwd(o_fox, y_lru, vec('g_fox_out'), vec('g_lru_out'),
                                                                     dmix)
    (du, dgate, gw['conv_w'], gw['conv_b'], gw['w_ra'], gw['b_ra'], gw['w_ri'], gw['b_ri'],
     gw['lam']) = lru_bwd(proj, dy_lru, *lru_w, u_off, g_off)
    dqn, delta2 = fox_bwd_q(qn, kn, vb, do_fox, cc, ct, lse, delta, T)
    dkn, dv, dct = fox_bwd_kv(qn, kn, vb, do_fox, cc, ct, lse, delta2, T)
    dq, dk, gw['g_q'], gw['g_k'] = qkv_bwd(proj, vec('g_q'), vec('g_k'), dqn, dkn, FW)
    dc = jnp.pad(dct.reshape(H, S).T, ((0, 0), (0, LANES - H)))
    df, db_f = fgate_bwd(f_raw, b_f_pad, dc, H)
    gw['b_f'] = db_f[:, :H]
    dproj = jnp.concatenate([dq, dk, dv, du, dgate], axis=1)
    dh_a = _mm('bwd_f_x', df, wf, 'nt', F32)
    dh1 = _mm('bwd_in_x', dproj, w5, 'nt', F32, res=dh_a)
    dw5 = _mm('bwd_in_w', h1, dproj, 'tn', F32)
    dwf = _mm('bwd_f_w', h1, df, 'tn', F32)
    dw_in = jnp.concatenate([dw5[:, :3 * FW], dwf[:, :H], dw5[:, 3 * FW:]], axis=1)
    gw['w_in'] = dw_in.reshape(D, N_CHIPS, IN_W // N_CHIPS).transpose(1, 0, 2)
    grad_x, _, gw['g_mix'] = norm_bwd('norm_mix_bwd', xs, vec('g_mix'), dh1, res=dx1)

    split = [gw[n].reshape(N_CHIPS, 2, gw[n].shape[1] // 2, gw[n].shape[2]) for n in BIG]
    got = pair_exchange(split)
    parts = [pair_add('pair_add_' + n, g, r, c_idx) for n, g, r in zip(BIG, split, got)]
    landed = chip_exchange(parts)
    halves = [sum_chips('sum_chips_' + n, p, l, chip_idx) for n, p, l in zip(BIG, parts, landed)]
    others = pair_join(halves)
    grads = {}

    small_shapes = [gw[n].shape for n in SMALL]
    summed = _unpack(allreduce_small(_pack([gw[n] for n in SMALL])), small_shapes)
    for n, g in zip(SMALL, summed):
        grads[n] = g.reshape(W[n].shape) if n != 'conv_w' else lax.dynamic_slice_in_dim(
            g, chip * (LW // N_CHIPS), LW // N_CHIPS, axis=1)

    delta_w, new_m, new_v = {}, {}, {}
    for n, mine, other in zip(BIG, halves, others):
        grads[n], delta_w[n], new_m[n], new_v[n] = adamw_halves('adamw_' + n, W[n], mine, other, M1[n], V1[n], c_idx)
    packs = [_pack([d[n] for n in SMALL]) for d in (W, grads, M1, V1)]
    shapes = [W[n].shape for n in SMALL]
    for d, res in zip((delta_w, new_m, new_v), adamw('adamw_small', *packs)):
        d.update(zip(SMALL, _unpack(res, shapes)))

    lead = lambda d: [d[n][None] for n in WEIGHTS]
    return (loss, grad_x[None], *lead(grads), *lead(delta_w), *lead(new_m), *lead(new_v))
```

```python
import functools
import math

import jax
import jax.numpy as jnp
from jax import lax
from jax.experimental import pallas as pl
from jax.experimental.pallas import tpu as pltpu

F32 = jnp.float32
BF16 = jnp.bfloat16
HEAD_DIM = 128
LANES = 128
LRU_C = 8.0
RMS_EPS = 1e-6
CONV_W = 4
ADAM_LR = 0.001
ADAM_B1 = 0.9
ADAM_B2 = 0.999
ADAM_EPS = 1e-08
ADAM_WD = 0.01
ADAM_STEP = 10
VMEM_LIMIT = 56 * 1024 * 1024
N_CHIPS = 4
MESH = pl.DeviceIdType.MESH
ANY = pl.BlockSpec(memory_space=pl.ANY)

WEIGHTS = ['g_mix', 'w_in', 'b_f', 'g_q', 'g_k', 'conv_w', 'conv_b', 'w_ra', 'b_ra', 'w_ri', 'b_ri', 'lam',
           'g_fox_out', 'g_lru_out', 'w_out', 'g_xattn', 'g_mem', 'w_cq', 'w_ckv', 'g_cq', 'g_ck', 'w_co', 'g_ffn',
           'w_gate_up', 'w_down']
BIG = ['w_in', 'w_out', 'w_cq', 'w_ckv', 'w_co', 'w_gate_up', 'w_down']
SMALL = [n for n in WEIGHTS if n not in BIG]


def _params(sem=None):
    if sem is None:
        return pltpu.CompilerParams(vmem_limit_bytes=VMEM_LIMIT)
    return pltpu.CompilerParams(dimension_semantics=sem, vmem_limit_bytes=VMEM_LIMIT)


def _tile(n, cands):
    for t in cands:
        if n % t == 0:
            return t
    return n


ROW_BLOCK_BYTES = 1 << 20


def _row_tile(n_rows, n_cols, min_rows=8):
    cands = [t for t in (512, 256, 128, 64, 32, 16, 8) if t >= min_rows and t * n_cols * 4 <= ROW_BLOCK_BYTES]
    return _tile(n_rows, cands or [min_rows])


def _sigmoid(z):
    return 1.0 / (1.0 + jnp.exp(-z))


def _softplus(z):
    return jnp.maximum(z, 0.0) + jnp.log(1.0 + jnp.exp(-jnp.abs(z)))


def _neg_expm1(z):
    series = -z * (1.0 + z * (0.5 + z * (1.0 / 6.0 + z * (1.0 / 24.0 + z * (1.0 / 120.0)))))
    return jnp.where(z > -0.25, series, 1.0 - jnp.exp(z))


_GELU_K = math.sqrt(2.0 / math.pi)


def _gelu_and_grad(z):
    inner = _GELU_K * (z + 0.044715 * z * z * z)
    t = jnp.tanh(inner)
    g = 0.5 * z * (1.0 + t)
    dg = 0.5 * (1.0 + t) + 0.5 * z * (1.0 - t * t) * _GELU_K * (1.0 + 3.0 * 0.044715 * z * z)
    return g, dg


def _rms(xv, g):
    r = lax.rsqrt(jnp.mean(xv * xv, axis=-1, keepdims=True) + RMS_EPS)
    return xv * r * g


def _rms_bwd(xv, g, dy):
    r = lax.rsqrt(jnp.mean(xv * xv, axis=-1, keepdims=True) + RMS_EPS)
    xh = xv * r
    dyg = dy * g
    dx = r * (dyg - xh * jnp.mean(dyg * xh, axis=-1, keepdims=True))
    return dx, jnp.sum(dy * xh, axis=0, keepdims=True)


def _heads(fn, n_heads, *arrs):
    outs = [fn(*[a[:, h * HEAD_DIM:(h + 1) * HEAD_DIM] for a in arrs]) for h in range(n_heads)]
    first = jnp.concatenate([o[0] for o in outs], axis=1) if n_heads > 1 else outs[0][0]
    rest = [functools.reduce(lambda p, q: p + q, [o[i] for o in outs]) for i in range(1, len(outs[0]))]
    return (first, *rest)


def _split3(v):
    hi = v.astype(BF16)
    r1 = v - hi.astype(F32)
    mid = r1.astype(BF16)
    lo = (r1 - mid.astype(F32)).astype(BF16)
    return hi, mid, lo


def _acc_out(ref, first, val):
    @pl.when(first)
    def _():
        ref[...] = val

    @pl.when(jnp.logical_not(first))
    def _():
        ref[...] += val


_DIMS = {'nn': (((1,), (0,)), ((), ())), 'nt': (((1,), (1,)), ((), ())), 'tn': (((0,), (0,)), ((), ()))}


def _mm_call(name, a, b, mode, grid, a_spec, b_spec, o_spec, o_shape, o_dtype, acc_shape, res=None):
    nk = grid[2]
    dn = _DIMS[mode]

    def body(*refs):
        if res is None:
            a_ref, b_ref, o_ref, acc = refs
            r_ref = None
        else:
            a_ref, b_ref, r_ref, o_ref, acc = refs
        k = pl.program_id(2)

        @pl.when(k == 0)
        def _():
            acc[...] = jnp.zeros_like(acc)

        acc[...] += lax.dot_general(a_ref[...].astype(BF16), b_ref[...].astype(BF16), dn,
                                    preferred_element_type=F32)

        @pl.when(k == nk - 1)
        def _():
            r = acc[...]
            if r_ref is not None:
                r = r + r_ref[...]
            o_ref[...] = r.astype(o_dtype)

    ins = [a, b] + ([] if res is None else [res])
    specs = [a_spec, b_spec] + ([] if res is None else [o_spec])
    return pl.pallas_call(
        body, name=name, grid=grid, in_specs=specs, out_specs=o_spec,
        out_shape=jax.ShapeDtypeStruct(o_shape, o_dtype), scratch_shapes=[pltpu.VMEM(acc_shape, F32)],
        compiler_params=_params(('parallel', 'parallel', 'arbitrary')))(*ins)


def _mm(name, a, b, mode, o_dtype, res=None):
    if mode == 'tn':
        K, M = a.shape
    else:
        M, K = a.shape
    N = b.shape[0] if mode == 'nt' else b.shape[1]
    tm = _tile(M, (1024, 512, 256, 128))
    tn = _tile(N, (1024, 512, 256, 128))
    tk = _tile(K, (512, 256, 128))
    a_spec = (pl.BlockSpec((tk, tm), lambda m, n, k: (k, m)) if mode == 'tn'
              else pl.BlockSpec((tm, tk), lambda m, n, k: (m, k)))
    b_spec = (pl.BlockSpec((tn, tk), lambda m, n, k: (n, k)) if mode == 'nt'
              else pl.BlockSpec((tk, tn), lambda m, n, k: (k, n)))
    o_spec = pl.BlockSpec((tm, tn), lambda m, n, k: (m, n))
    return _mm_call(name, a, b, mode, (M // tm, N // tn, K // tk), a_spec, b_spec, o_spec, (M, N), o_dtype,
                    (tm, tn), res)


def _mm_colsharded(name, a, w, o_dtype, res=None):
    M, K = a.shape
    J, _, Nj = w.shape
    tm = _tile(M, (1024, 512, 256, 128))
    tn = _tile(Nj, (1408, 1024, 512, 256, 128))
    tk = _tile(K, (512, 256, 128))
    per = Nj // tn
    return _mm_call(name, a, w, 'nn', (M // tm, J * per, K // tk),
                    pl.BlockSpec((tm, tk), lambda m, n, k: (m, k)),
                    pl.BlockSpec((None, tk, tn), lambda m, n, k: (n // per, k, n % per)),
                    pl.BlockSpec((tm, tn), lambda m, n, k: (m, n)), (M, J * Nj), o_dtype, (tm, tn), res)


def _mm_colsharded_t(name, a, w, o_dtype):
    M = a.shape[0]
    J, K, Nj = w.shape
    tm = _tile(M, (1024, 512, 256, 128))
    tn = _tile(K, (1024, 512, 256, 128))
    tk = _tile(Nj, (1408, 1024, 512, 256, 128))
    per = Nj // tk
    return _mm_call(name, a, w, 'nt', (M // tm, K // tn, J * per),
                    pl.BlockSpec((tm, tk), lambda m, n, k: (m, k)),
                    pl.BlockSpec((None, tn, tk), lambda m, n, k: (k // per, n, k % per)),
                    pl.BlockSpec((tm, tn), lambda m, n, k: (m, n)), (M, K), o_dtype, (tm, tn))


def _mm_grad_colsharded(name, a, dy, J):
    S, M = a.shape
    Nj = dy.shape[1] // J
    tm = _tile(M, (1024, 512, 256, 128))
    tn = _tile(Nj, (1408, 1024, 512, 256, 128))
    tk = _tile(S, (512, 256, 128))
    per = Nj // tn
    return _mm_call(name, a, dy, 'tn', (M // tm, J * per, S // tk),
                    pl.BlockSpec((tk, tm), lambda m, n, k: (k, m)),
                    pl.BlockSpec((tk, tn), lambda m, n, k: (k, n)),
                    pl.BlockSpec((None, tm, tn), lambda m, n, k: (n // per, m, n % per)), (J, M, Nj), F32, (tm, tn))


def _rows_call(name, body, n_rows, tr, ins, outs):
    return pl.pallas_call(
        body, name=name, grid=(n_rows // tr,), in_specs=[s for _, s in ins], out_specs=[s for _, _, s in outs],
        out_shape=[jax.ShapeDtypeStruct(sh, dt) for sh, dt, _ in outs],
        compiler_params=_params(('arbitrary',)))(*[a for a, _ in ins])


def _rb(tr, w, cb=0):
    return pl.BlockSpec((tr, w), lambda i: (i, cb))


def _fb(shape):
    nd = len(shape)
    return pl.BlockSpec(shape, lambda i: (0,) * nd)


def norm_fwd(name, xv, g):
    S, D = xv.shape
    tr = _tile(S, (256, 128))

    def body(x_ref, g_ref, o_ref):
        o_ref[...] = _rms(x_ref[...], g_ref[...]).astype(BF16)

    return _rows_call(name, body, S, tr, [(xv, _rb(tr, D)), (g, _fb((1, D)))], [((S, D), BF16, _rb(tr, D))])[0]


def norm_bwd(name, xv, g, dy, res=None, want_dx=True):
    S, D = xv.shape
    tr = _tile(S, (256, 128))

    def body(*refs):
        if res is None:
            x_ref, g_ref, dy_ref = refs[:3]
            outs = refs[3:]
            r_ref = None
        else:
            x_ref, g_ref, dy_ref, r_ref = refs[:4]
            outs = refs[4:]
        dx, dg = _rms_bwd(x_ref[...], g_ref[...], dy_ref[...])
        if r_ref is not None:
            dx = dx + r_ref[...]
        if want_dx:
            outs[0][...] = dx
            outs[1][...] = dx.astype(BF16)
        _acc_out(outs[-1], pl.program_id(0) == 0, dg)

    ins = [(xv, _rb(tr, D)), (g, _fb((1, D))), (dy, _rb(tr, D))] + ([] if res is None else [(res, _rb(tr, D))])
    outs = ([((S, D), F32, _rb(tr, D)), ((S, D), BF16, _rb(tr, D))] if want_dx else []) + [((1, D), F32, _fb((1, D)))]
    return _rows_call(name, body, S, tr, ins, outs)


def qkv_fwd(proj, g_q, g_k, FW):
    S = proj.shape[0]
    H = FW // HEAD_DIM
    tr = _tile(S, (256, 128))

    def body(q_ref, k_ref, v_ref, gq_ref, gk_ref, qo, ko, vo):
        qo[...] = _heads(lambda t: (_rms(t, gq_ref[...]),), H, q_ref[...])[0].astype(BF16)
        ko[...] = _heads(lambda t: (_rms(t, gk_ref[...]),), H, k_ref[...])[0].astype(BF16)
        vo[...] = v_ref[...].astype(BF16)

    o = ((S, FW), BF16, _rb(tr, FW))
    return _rows_call('qkv_fwd', body, S, tr,
                      [(proj, _rb(tr, FW, 0)), (proj, _rb(tr, FW, 1)), (proj, _rb(tr, FW, 2)),
                       (g_q, _fb((1, HEAD_DIM))), (g_k, _fb((1, HEAD_DIM)))], [o, o, o])


def qkv_bwd(proj, g_q, g_k, dqn, dkn, FW):
    S = proj.shape[0]
    H = FW // HEAD_DIM
    tr = _tile(S, (256, 128))

    def body(q_ref, k_ref, gq_ref, gk_ref, dq_ref, dk_ref, dqo, dko, dgq, dgk):
        dq, gq = _heads(lambda t, d: _rms_bwd(t, gq_ref[...], d), H, q_ref[...], dq_ref[...])
        dk, gk = _heads(lambda t, d: _rms_bwd(t, gk_ref[...], d), H, k_ref[...], dk_ref[...])
        dqo[...] = dq.astype(BF16)
        dko[...] = dk.astype(BF16)
        first = pl.program_id(0) == 0
        _acc_out(dgq, first, gq)
        _acc_out(dgk, first, gk)

    o = ((S, FW), BF16, _rb(tr, FW))
    og = ((1, HEAD_DIM), F32, _fb((1, HEAD_DIM)))
    return _rows_call('qkv_bwd', body, S, tr,
                      [(proj, _rb(tr, FW, 0)), (proj, _rb(tr, FW, 1)), (g_q, _fb((1, HEAD_DIM))),
                       (g_k, _fb((1, HEAD_DIM))), (dqn, _rb(tr, FW)), (dkn, _rb(tr, FW))], [o, o, og, og])


def _tri(n, upper):
    r = lax.broadcasted_iota(jnp.int32, (n, n), 0)
    c = lax.broadcasted_iota(jnp.int32, (n, n), 1)
    return jnp.where((c >= r) if upper else (c <= r), 1.0, 0.0).astype(BF16)


def _blocked_cumsum(val, S, blk, reverse):
    tri = _tri(blk, reverse)
    order = range(S // blk - 1, -1, -1) if reverse else range(S // blk)
    carry = jnp.zeros((1, LANES), F32)
    outs = {}
    for bi in order:
        part = val[bi * blk:(bi + 1) * blk]
        acc = carry
        for piece in _split3(part):
            acc = acc + jnp.dot(tri, piece, preferred_element_type=F32)
        outs[bi] = acc
        carry = carry + jnp.sum(part, axis=0, keepdims=True)
    return jnp.concatenate([outs[bi] for bi in range(S // blk)], axis=0)


def fgate_fwd(f_raw, b_f_pad):
    S = f_raw.shape[0]
    blk = _tile(S, (256, 128))

    def body(f_ref, b_ref, c_ref):
        z = f_ref[...] + b_ref[...]
        c_ref[...] = _blocked_cumsum(-_softplus(-z), S, blk, False)

    return pl.pallas_call(body, name='fgate_fwd', grid=(1,), in_specs=[_fb((S, LANES)), _fb((1, LANES))],
                          out_specs=_fb((S, LANES)), out_shape=jax.ShapeDtypeStruct((S, LANES), F32),
                          compiler_params=_params(('arbitrary',)))(f_raw, b_f_pad)


def fgate_bwd(f_raw, b_f_pad, dc, H):
    S = f_raw.shape[0]
    blk = _tile(S, (256, 128))

    def body(f_ref, b_ref, dc_ref, df_ref, db_ref):
        z = f_ref[...] + b_ref[...]
        dlogf = _blocked_cumsum(dc_ref[...], S, blk, True)
        lane = lax.broadcasted_iota(jnp.int32, (S, LANES), 1)
        df = jnp.where(lane < H, dlogf * _sigmoid(-z), 0.0)
        df_ref[...] = df.astype(BF16)
        db_ref[...] = jnp.sum(df, axis=0, keepdims=True)

    return pl.pallas_call(body, name='fgate_bwd', grid=(1,),
                          in_specs=[_fb((S, LANES)), _fb((1, LANES)), _fb((S, LANES))],
                          out_specs=[_fb((S, LANES)), _fb((1, LANES))],
                          out_shape=[jax.ShapeDtypeStruct((S, LANES), BF16), jax.ShapeDtypeStruct((1, LANES), F32)],
                          compiler_params=_params(('arbitrary',)))(f_raw, b_f_pad, dc)


def _fox_logits(q, k, c_blk, ct_blk, h, i, j, T):
    s = lax.dot_general(q, k, _DIMS['nt'], preferred_element_type=F32) * (1.0 / math.sqrt(HEAD_DIM))
    lane = lax.broadcasted_iota(jnp.int32, c_blk.shape, 1)
    cq = jnp.sum(jnp.where(lane == h, c_blk, 0.0), axis=1, keepdims=True)
    sub = lax.broadcasted_iota(jnp.int32, ct_blk.shape, 0)
    ck = jnp.sum(jnp.where(sub == h, ct_blk, 0.0), axis=0, keepdims=True)
    rows = i * T + lax.broadcasted_iota(jnp.int32, (T, T), 0)
    cols = j * T + lax.broadcasted_iota(jnp.int32, (T, T), 1)
    return jnp.where(cols <= rows, s + cq - ck, -jnp.inf)


def fox_fwd(qn, kn, vb, c, ct, T):
    S, FW = qn.shape
    H = FW // HEAD_DIM
    Hp = ct.shape[0]
    n = S // T

    def body(q_ref, k_ref, v_ref, c_ref, ct_ref, o_ref, lse_ref, m_s, l_s, acc_s):
        h, i, j = pl.program_id(0), pl.program_id(1), pl.program_id(2)

        @pl.when(j == 0)
        def _():
            m_s[...] = jnp.full_like(m_s, -jnp.inf)
            l_s[...] = jnp.zeros_like(l_s)
            acc_s[...] = jnp.zeros_like(acc_s)

        @pl.when(j <= i)
        def _():
            s = _fox_logits(q_ref[...], k_ref[...], c_ref[...], ct_ref[...], h, i, j, T)
            m_new = jnp.maximum(m_s[...], jnp.max(s, axis=1, keepdims=True))
            alpha = jnp.exp(m_s[...] - m_new)
            p = jnp.exp(s - m_new)
            l_s[...] = alpha * l_s[...] + jnp.sum(p, axis=1, keepdims=True)
            acc_s[...] = alpha * acc_s[...] + jnp.dot(p.astype(BF16), v_ref[...], preferred_element_type=F32)
            m_s[...] = m_new

        @pl.when(j == i)
        def _():
            o_ref[...] = acc_s[...] / l_s[...]
            lse_ref[...] = jnp.broadcast_to(m_s[...] + jnp.log(l_s[...]), (T, LANES))

    qs = pl.BlockSpec((T, HEAD_DIM), lambda h, i, j: (i, h))
    ks = pl.BlockSpec((T, HEAD_DIM), lambda h, i, j: (jnp.minimum(j, i), h))
    return pl.pallas_call(
        body, name='fox_fwd', grid=(H, n, n),
        in_specs=[qs, ks, ks, pl.BlockSpec((T, LANES), lambda h, i, j: (i, 0)),
                  pl.BlockSpec((Hp, T), lambda h, i, j: (0, jnp.minimum(j, i)))],
        out_specs=[qs, pl.BlockSpec((None, T, LANES), lambda h, i, j: (h, i, 0))],
        out_shape=[jax.ShapeDtypeStruct((S, FW), F32), jax.ShapeDtypeStruct((H, S, LANES), F32)],
        scratch_shapes=[pltpu.VMEM((T, 1), F32), pltpu.VMEM((T, 1), F32), pltpu.VMEM((T, HEAD_DIM), F32)],
        compiler_params=_params(('parallel', 'parallel', 'arbitrary')))(qn, kn, vb, c, ct)


def _fox_p_ds(q_ref, k_ref, v_ref, do_ref, c_ref, ct_ref, lse_ref, dl_ref, h, i, j, T):
    s = _fox_logits(q_ref[...], k_ref[...], c_ref[...], ct_ref[...], h, i, j, T)
    p = jnp.exp(s - jnp.tile(lse_ref[...], (1, T // LANES)))
    dp = lax.dot_general(do_ref[...], v_ref[...], _DIMS['nt'], preferred_element_type=F32)
    ds = p * (dp - jnp.tile(dl_ref[...], (1, T // LANES)))
    return p, dp, ds


def fox_bwd_q(qn, kn, vb, do, c, ct, lse, dl, T):
    S, FW = qn.shape
    H = FW // HEAD_DIM
    Hp = ct.shape[0]
    n = S // T

    def body(q_ref, k_ref, v_ref, do_ref, c_ref, ct_ref, lse_ref, dl_ref, dq_ref, dl2_ref, acc_s, rs_s):
        h, i, j = pl.program_id(0), pl.program_id(1), pl.program_id(2)

        @pl.when(j == 0)
        def _():
            acc_s[...] = jnp.zeros_like(acc_s)
            rs_s[...] = jnp.zeros_like(rs_s)

        @pl.when(j <= i)
        def _():
            p, dp, ds = _fox_p_ds(q_ref, k_ref, v_ref, do_ref, c_ref, ct_ref, lse_ref, dl_ref, h, i, j, T)
            acc_s[...] += jnp.dot(ds.astype(BF16), k_ref[...], preferred_element_type=F32)
            rs_s[...] += jnp.sum(p * dp, axis=1, keepdims=True)

        @pl.when(j == i)
        def _():
            dq_ref[...] = acc_s[...] * (1.0 / math.sqrt(HEAD_DIM))
            dl2_ref[...] = jnp.broadcast_to(rs_s[...], (T, LANES))

    qs = pl.BlockSpec((T, HEAD_DIM), lambda h, i, j: (i, h))
    ks = pl.BlockSpec((T, HEAD_DIM), lambda h, i, j: (jnp.minimum(j, i), h))
    st = pl.BlockSpec((None, T, LANES), lambda h, i, j: (h, i, 0))
    return pl.pallas_call(
        body, name='fox_bwd_q', grid=(H, n, n),
        in_specs=[qs, ks, ks, qs, pl.BlockSpec((T, LANES), lambda h, i, j: (i, 0)),
                  pl.BlockSpec((Hp, T), lambda h, i, j: (0, jnp.minimum(j, i))), st, st],
        out_specs=[qs, st], out_shape=[jax.ShapeDtypeStruct((S, FW), F32), jax.ShapeDtypeStruct((H, S, LANES), F32)],
        scratch_shapes=[pltpu.VMEM((T, HEAD_DIM), F32), pltpu.VMEM((T, 1), F32)],
        compiler_params=_params(('parallel', 'parallel', 'arbitrary')))(qn, kn, vb, do, c, ct, lse, dl)


def fox_bwd_kv(qn, kn, vb, do, c, ct, lse, dl, T):
    S, FW = qn.shape
    H = FW // HEAD_DIM
    Hp = ct.shape[0]
    n = S // T

    def body(q_ref, k_ref, v_ref, do_ref, c_ref, ct_ref, lse_ref, dl_ref, dk_ref, dv_ref, dc_ref, dk_s, dv_s, dc_s):
        h, j, i = pl.program_id(0), pl.program_id(1), pl.program_id(2)

        @pl.when(i == 0)
        def _():
            dk_s[...] = jnp.zeros_like(dk_s)
            dv_s[...] = jnp.zeros_like(dv_s)
            dc_s[...] = jnp.zeros_like(dc_s)

        @pl.when(i >= j)
        def _():
            p, _, ds = _fox_p_ds(q_ref, k_ref, v_ref, do_ref, c_ref, ct_ref, lse_ref, dl_ref, h, i, j, T)
            dv_s[...] += lax.dot_general(p.astype(BF16), do_ref[...], _DIMS['tn'], preferred_element_type=F32)
            dk_s[...] += lax.dot_general(ds.astype(BF16), q_ref[...], _DIMS['tn'], preferred_element_type=F32)
            dc_s[...] += jnp.sum(ds, axis=0, keepdims=True)

        @pl.when(i == n - 1)
        def _():
            dk_ref[...] = dk_s[...] * (1.0 / math.sqrt(HEAD_DIM))
            dv_ref[...] = dv_s[...].astype(BF16)
            dc_ref[...] = -dc_s[...]

    qs = pl.BlockSpec((T, HEAD_DIM), lambda h, j, i: (jnp.maximum(i, j), h))
    ks = pl.BlockSpec((T, HEAD_DIM), lambda h, j, i: (j, h))
    st = pl.BlockSpec((None, T, LANES), lambda h, j, i: (h, jnp.maximum(i, j), 0))
    return pl.pallas_call(
        body, name='fox_bwd_kv', grid=(H, n, n),
        in_specs=[qs, ks, ks, qs, pl.BlockSpec((T, LANES), lambda h, j, i: (jnp.maximum(i, j), 0)),
                  pl.BlockSpec((Hp, T), lambda h, j, i: (0, j)), st, st],
        out_specs=[ks, ks, pl.BlockSpec((None, 1, T), lambda h, j, i: (h, 0, j))],
        out_shape=[jax.ShapeDtypeStruct((S, FW), F32), jax.ShapeDtypeStruct((S, FW), BF16),
                   jax.ShapeDtypeStruct((H, 1, S), F32)],
        scratch_shapes=[pltpu.VMEM((T, HEAD_DIM), F32), pltpu.VMEM((T, HEAD_DIM), F32), pltpu.VMEM((1, T), F32)],
        compiler_params=_params(('parallel', 'parallel', 'arbitrary')))(qn, kn, vb, do, c, ct, lse, dl)


def _shift_down(v, d, rows, fill):
    return jnp.where(rows >= d, pltpu.roll(v, d, 0), fill)


def _shift_up(v, d, rows, S, fill):
    return jnp.where(rows < S - d, pltpu.roll(v, S - d, 0), fill)


def _scan(a, b, rows, S, reverse):
    d = 1
    while d < S:
        if reverse:
            a_s, b_s = _shift_up(a, d, rows, S, 1.0), _shift_up(b, d, rows, S, 0.0)
        else:
            a_s, b_s = _shift_down(a, d, rows, 1.0), _shift_down(b, d, rows, 0.0)
        b = a * b_s + b
        a = a * a_s
        d *= 2
    return b


def _lru_forward(u, cw, cb, wra, bra, wri, bri, lam, rows):
    uc = cb + cw[CONV_W - 1] * u
    for d in range(1, CONV_W):
        uc = uc + cw[CONV_W - 1 - d] * _shift_down(u, d, rows, 0.0)
    ucb = uc.astype(BF16)
    r = _sigmoid(jnp.dot(ucb, wra.astype(BF16), preferred_element_type=F32) + bra)
    ig = _sigmoid(jnp.dot(ucb, wri.astype(BF16), preferred_element_type=F32) + bri)
    sp = _softplus(-lam)
    log_a = -LRU_C * r * sp
    a = jnp.exp(log_a)
    sq = jnp.sqrt(_neg_expm1(2.0 * log_a))
    iu = ig * uc
    hseq = _scan(a, sq * iu, rows, u.shape[0], False)
    return uc, ucb, r, ig, sp, a, sq, iu, hseq


def _lru_specs(S, n_u, n_g):
    col = lambda off: pl.BlockSpec((S, LANES), lambda cbk: (0, off + cbk))
    vec = pl.BlockSpec((1, LANES), lambda cbk: (0, cbk))
    mat = pl.BlockSpec((None, LANES, LANES), lambda cbk: (cbk, 0, 0))
    cw = pl.BlockSpec((CONV_W, LANES), lambda cbk: (0, cbk))
    return col, vec, mat, cw


def lru_fwd(proj, conv_w, conv_b, w_ra, b_ra, w_ri, b_ri, lam, u_off, g_off):
    S = proj.shape[0]
    nb = w_ra.shape[0]
    col, vec, mat, cws = _lru_specs(S, u_off, g_off)

    def body(u_ref, g_ref, cw_ref, cb_ref, wra_ref, bra_ref, wri_ref, bri_ref, lam_ref, y_ref):
        rows = lax.broadcasted_iota(jnp.int32, (S, LANES), 0)
        cw = [cw_ref[t:t + 1, :] for t in range(CONV_W)]
        hseq = _lru_forward(u_ref[...], cw, cb_ref[...], wra_ref[...], bra_ref[...], wri_ref[...],
                            bri_ref[...], lam_ref[...], rows)[-1]
        y_ref[...] = hseq * _gelu_and_grad(g_ref[...])[0]

    return pl.pallas_call(
        body, name='lru_fwd', grid=(nb,),
        in_specs=[col(u_off), col(g_off), cws, vec, mat, vec, mat, vec, vec], out_specs=col(0),
        out_shape=jax.ShapeDtypeStruct((S, nb * LANES), F32),
        compiler_params=_params(('parallel',)))(proj, proj, conv_w, conv_b, w_ra, b_ra, w_ri, b_ri, lam)


def lru_bwd(proj, dy, conv_w, conv_b, w_ra, b_ra, w_ri, b_ri, lam, u_off, g_off):
    S = proj.shape[0]
    nb = w_ra.shape[0]
    LW = nb * LANES
    col, vec, mat, cws = _lru_specs(S, u_off, g_off)

    def body(u_ref, g_ref, dy_ref, cw_ref, cb_ref, wra_ref, bra_ref, wri_ref, bri_ref, lam_ref,
             du_ref, dg_ref, dcw_ref, dcb_ref, dwra_ref, dbra_ref, dwri_ref, dbri_ref, dlam_ref):
        rows = lax.broadcasted_iota(jnp.int32, (S, LANES), 0)
        u, lam_v = u_ref[...], lam_ref[...]
        cw = [cw_ref[t:t + 1, :] for t in range(CONV_W)]
        wra, wri = wra_ref[...].astype(BF16), wri_ref[...].astype(BF16)
        uc, ucb, r, ig, sp, a, sq, iu, hseq = _lru_forward(u, cw, cb_ref[...], wra, bra_ref[...], wri, bri_ref[...],
                                                           lam_v, rows)
        gl, dgl = _gelu_and_grad(g_ref[...])
        dy_v = dy_ref[...]
        dg_ref[...] = (dy_v * hseq * dgl).astype(BF16)
        G = _scan(_shift_up(a, 1, rows, S, 0.0), dy_v * gl, rows, S, True)
        da = G * _shift_down(hseq, 1, rows, 0.0)
        diu = G * sq
        dsq = G * iu
        dlog_a = da * a - dsq * a * a / jnp.maximum(sq, 1e-30)
        dr = dlog_a * (-LRU_C * sp)
        dsp = jnp.sum(dlog_a * (-LRU_C * r), axis=0, keepdims=True)
        dlam_ref[...] = -dsp * _sigmoid(-lam_v)
        dzr = dr * r * (1.0 - r)
        dzi = diu * uc * ig * (1.0 - ig)
        dzrb, dzib = dzr.astype(BF16), dzi.astype(BF16)
        duc = (diu * ig + lax.dot_general(dzrb, wra, _DIMS['nt'], preferred_element_type=F32)
               + lax.dot_general(dzib, wri, _DIMS['nt'], preferred_element_type=F32))
        dwra_ref[...] = lax.dot_general(ucb, dzrb, _DIMS['tn'], preferred_element_type=F32)
        dwri_ref[...] = lax.dot_general(ucb, dzib, _DIMS['tn'], preferred_element_type=F32)
        dbra_ref[...] = jnp.sum(dzr, axis=0, keepdims=True)
        dbri_ref[...] = jnp.sum(dzi, axis=0, keepdims=True)
        dcb_ref[...] = jnp.sum(duc, axis=0, keepdims=True)
        du = cw[CONV_W - 1] * duc
        dcw_ref[CONV_W - 1:CONV_W, :] = jnp.sum(duc * u, axis=0, keepdims=True)
        for d in range(1, CONV_W):
            du = du + cw[CONV_W - 1 - d] * _shift_up(duc, d, rows, S, 0.0)
            dcw_ref[CONV_W - 1 - d:CONV_W - d, :] = jnp.sum(duc * _shift_down(u, d, rows, 0.0), axis=0, keepdims=True)
        du_ref[...] = du.astype(BF16)

    sd = jax.ShapeDtypeStruct
    return pl.pallas_call(
        body, name='lru_bwd', grid=(nb,),
        in_specs=[col(u_off), col(g_off), col(0), cws, vec, mat, vec, mat, vec, vec],
        out_specs=[col(0), col(0), cws, vec, mat, vec, mat, vec, vec],
        out_shape=[sd((S, LW), BF16), sd((S, LW), BF16), sd((CONV_W, LW), F32), sd((1, LW), F32),
                   sd((nb, LANES, LANES), F32), sd((1, LW), F32), sd((nb, LANES, LANES), F32), sd((1, LW), F32),
                   sd((1, LW), F32)],
        compiler_params=_params(('parallel',)))(proj, proj, dy, conv_w, conv_b, w_ra, b_ra, w_ri, b_ri, lam)


def mix_fwd(o_fox, y_lru, g_fox, g_lru):
    S, FW = o_fox.shape
    tr = _tile(S, (256, 128))

    def body(o_ref, y_ref, gf_ref, gl_ref, m_ref):
        m_ref[...] = jnp.concatenate([_rms(o_ref[...], gf_ref[...]), _rms(y_ref[...], gl_ref[...])],
                                     axis=1).astype(BF16)

    return _rows_call('mix_fwd', body, S, tr,
                      [(o_fox, _rb(tr, FW)), (y_lru, _rb(tr, FW)), (g_fox, _fb((1, FW))), (g_lru, _fb((1, FW)))],
                      [((S, 2 * FW), BF16, _rb(tr, 2 * FW))])[0]


def mix_bwd(o_fox, y_lru, g_fox, g_lru, dmix):
    S, FW = o_fox.shape
    H = FW // HEAD_DIM
    tr = _tile(S, (256, 128))

    def body(o_ref, y_ref, gf_ref, gl_ref, df_ref, dl_ref, do_ref, dlt_ref, dy_ref, dgf_ref, dgl_ref):
        o = o_ref[...]
        do, dgf = _rms_bwd(o, gf_ref[...], df_ref[...])
        dyl, dgl = _rms_bwd(y_ref[...], gl_ref[...], dl_ref[...])
        do_ref[...] = do.astype(BF16)
        dy_ref[...] = dyl
        prod = do * o
        for h in range(H):
            dlt_ref[h] = jnp.broadcast_to(
                jnp.sum(prod[:, h * HEAD_DIM:(h + 1) * HEAD_DIM], axis=1, keepdims=True), (tr, LANES))
        first = pl.program_id(0) == 0
        _acc_out(dgf_ref, first, dgf)
        _acc_out(dgl_ref, first, dgl)

    g = _fb((1, FW))
    return _rows_call('mix_bwd', body, S, tr,
                      [(o_fox, _rb(tr, FW)), (y_lru, _rb(tr, FW)), (g_fox, g), (g_lru, g), (dmix, _rb(tr, FW, 0)),
                       (dmix, _rb(tr, FW, 1))],
                      [((S, FW), BF16, _rb(tr, FW)), ((H, S, LANES), F32, pl.BlockSpec((H, tr, LANES), lambda i: (0, i, 0))),
                       ((S, FW), F32, _rb(tr, FW)), ((1, FW), F32, g), ((1, FW), F32, g)])


def _xattn_heads(cq_raw, ckv, g_cq, g_ck, XW):
    out = []
    for h in range(XW // HEAD_DIM):
        sl = slice(h * HEAD_DIM, (h + 1) * HEAD_DIM)
        out.append((cq_raw[:, sl], _rms(cq_raw[:, sl], g_cq), ckv[:, sl], _rms(ckv[:, sl], g_ck),
                    ckv[:, XW + h * HEAD_DIM:XW + (h + 1) * HEAD_DIM].astype(BF16)))
    return out


def xattn_fwd(cq_raw, ckv, g_cq, g_ck):
    S, XW = cq_raw.shape
    M = ckv.shape[0]
    tr = _tile(S, (512, 256, 128))

    def body(q_ref, kv_ref, gq_ref, gk_ref, o_ref):
        outs = []
        for _, qn, _, kn, v in _xattn_heads(q_ref[...], kv_ref[...], gq_ref[...], gk_ref[...], XW):
            s = lax.dot_general(qn.astype(BF16), kn.astype(BF16), _DIMS['nt'], preferred_element_type=F32)
            s = s / math.sqrt(HEAD_DIM)
            p = jnp.exp(s - jnp.max(s, axis=1, keepdims=True))
            p = p / jnp.sum(p, axis=1, keepdims=True)
            outs.append(jnp.dot(p.astype(BF16), v, preferred_element_type=F32))
        o_ref[...] = jnp.concatenate(outs, axis=1).astype(BF16)

    g = _fb((1, HEAD_DIM))
    return _rows_call('xattn_fwd', body, S, tr,
                      [(cq_raw, _rb(tr, XW)), (ckv, _fb((M, 2 * XW))), (g_cq, g), (g_ck, g)],
                      [((S, XW), BF16, _rb(tr, XW))])[0]


def xattn_bwd(cq_raw, ckv, g_cq, g_ck, do):
    S, XW = cq_raw.shape
    M = ckv.shape[0]
    tr = _tile(S, (512, 256, 128))
    n = S // tr

    def body(q_ref, kv_ref, gq_ref, gk_ref, do_ref, dq_ref, dkv_ref, dgq_ref, dgk_ref):
        i = pl.program_id(0)
        do_v = do_ref[...]
        dqs, dkn, dvs = [], [], []
        dgq = jnp.zeros((1, HEAD_DIM), F32)
        for h, (q_raw, qn, _, kn, v) in enumerate(_xattn_heads(q_ref[...], kv_ref[...], gq_ref[...], gk_ref[...], XW)):
            qb, kb = qn.astype(BF16), kn.astype(BF16)
            doh = do_v[:, h * HEAD_DIM:(h + 1) * HEAD_DIM]
            s = lax.dot_general(qb, kb, _DIMS['nt'], preferred_element_type=F32) / math.sqrt(HEAD_DIM)
            p = jnp.exp(s - jnp.max(s, axis=1, keepdims=True))
            p = p / jnp.sum(p, axis=1, keepdims=True)
            dp = lax.dot_general(doh, v, _DIMS['nt'], preferred_element_type=F32)
            ds = (p * (dp - jnp.sum(p * dp, axis=1, keepdims=True)) / math.sqrt(HEAD_DIM)).astype(BF16)
            dvs.append(lax.dot_general(p.astype(BF16), doh, _DIMS['tn'], preferred_element_type=F32))
            dkn.append(lax.dot_general(ds, qb, _DIMS['tn'], preferred_element_type=F32))
            dq, g1 = _rms_bwd(q_raw, gq_ref[...], jnp.dot(ds, kb, preferred_element_type=F32))
            dqs.append(dq)
            dgq = dgq + g1
        dq_ref[...] = jnp.concatenate(dqs, axis=1).astype(BF16)
        first = i == 0
        _acc_out(dgq_ref, first, dgq)
        _acc_out(dkv_ref, first, jnp.concatenate(dkn + dvs, axis=1))

        @pl.when(i == n - 1)
        def _():
            kv = kv_ref[...]
            acc = dkv_ref[...]
            dk, gk = _heads(lambda t, d: _rms_bwd(t, gk_ref[...], d), XW // HEAD_DIM, kv[:, :XW], acc[:, :XW])
            dkv_ref[:, :XW] = dk
            dgk_ref[...] = gk

    g = _fb((1, HEAD_DIM))
    return _rows_call('xattn_bwd', body, S, tr,
                      [(cq_raw, _rb(tr, XW)), (ckv, _fb((M, 2 * XW))), (g_cq, g), (g_ck, g), (do, _rb(tr, XW))],
                      [((S, XW), BF16, _rb(tr, XW)), ((M, 2 * XW), F32, _fb((M, 2 * XW))), ((1, HEAD_DIM), F32, g),
                       ((1, HEAD_DIM), F32, g)])


def swiglu_fwd(gu, F):
    S = gu.shape[0]
    tr = _tile(S, (256, 128))
    tf = _tile(F, (1408, 1024, 512, 256, 128))
    nf = F // tf

    def body(g_ref, u_ref, a_ref):
        g = g_ref[...]
        a_ref[...] = (g * _sigmoid(g) * u_ref[...]).astype(BF16)

    return pl.pallas_call(
        body, name='swiglu_fwd', grid=(S // tr, nf),
        in_specs=[pl.BlockSpec((tr, tf), lambda i, n: (i, n)), pl.BlockSpec((tr, tf), lambda i, n: (i, n + nf))],
        out_specs=pl.BlockSpec((tr, tf), lambda i, n: (i, n)), out_shape=jax.ShapeDtypeStruct((S, F), BF16),
        compiler_params=_params(('parallel', 'parallel')))(gu, gu)


def swiglu_bwd(gu, dact, F):
    S = gu.shape[0]
    tr = _tile(S, (256, 128))
    tf = _tile(F, (1408, 1024, 512, 256, 128))
    nf = F // tf

    def body(g_ref, u_ref, da_ref, o_ref):
        n = pl.program_id(1)
        g, da = g_ref[...], da_ref[...]
        sg = _sigmoid(g)

        @pl.when(n < nf)
        def _():
            o_ref[...] = (da * u_ref[...] * sg * (1.0 + g * (1.0 - sg))).astype(BF16)

        @pl.when(n >= nf)
        def _():
            o_ref[...] = (da * g * sg).astype(BF16)

    return pl.pallas_call(
        body, name='swiglu_bwd', grid=(S // tr, 2 * nf),
        in_specs=[pl.BlockSpec((tr, tf), lambda i, n: (i, n % nf)), pl.BlockSpec((tr, tf), lambda i, n: (i, n % nf + nf)),
                  pl.BlockSpec((tr, tf), lambda i, n: (i, n % nf))],
        out_specs=pl.BlockSpec((tr, tf), lambda i, n: (i, n)), out_shape=jax.ShapeDtypeStruct((S, 2 * F), BF16),
        compiler_params=_params(('parallel', 'arbitrary')))(gu, gu, dact)


def loss_head(y, target):
    S, D = y.shape
    tr = _tile(S, (256, 128))

    def body(y_ref, t_ref, d_ref, db_ref, l_ref):
        err = y_ref[...] - t_ref[...]
        d = err * (1.0 / D)
        d_ref[...] = d
        db_ref[...] = d.astype(BF16)
        part = jnp.sum(jnp.sum(err * err, axis=1, keepdims=True), axis=0, keepdims=True) * (0.5 / D)
        _acc_out(l_ref, pl.program_id(0) == 0, jnp.broadcast_to(part, (1, LANES)))

    return _rows_call('loss_head', body, S, tr, [(y, _rb(tr, D)), (target, _rb(tr, D))],
                      [((S, D), F32, _rb(tr, D)), ((S, D), BF16, _rb(tr, D)), ((1, LANES), F32, _fb((1, LANES)))])


def _adamw_math(w, gv, m, v):
    mn = ADAM_B1 * m + (1.0 - ADAM_B1) * gv
    vn = ADAM_B2 * v + (1.0 - ADAM_B2) * (gv * gv)
    m_hat = mn / (1.0 - ADAM_B1 ** ADAM_STEP)
    v_hat = vn / (1.0 - ADAM_B2 ** ADAM_STEP)
    return -ADAM_LR * (m_hat / (jnp.sqrt(v_hat) + ADAM_EPS) + ADAM_WD * w), mn, vn


def adamw(name, w, g, m, v):
    R, C = w.shape
    tr = _row_tile(R, C)

    def body(w_ref, g_ref, m_ref, v_ref, d_ref, mo_ref, vo_ref):
        d_ref[...], mo_ref[...], vo_ref[...] = _adamw_math(w_ref[...], g_ref[...], m_ref[...], v_ref[...])

    spec = _rb(tr, C)
    return _rows_call(name, body, R, tr, [(w, spec), (g, spec), (m, spec), (v, spec)], [((R, C), F32, spec)] * 3)


def adamw_halves(name, w, mine, other, m, v, c_idx):
    R, C = w.shape
    hr = R // 2
    tr = _row_tile(hr, C)

    def body(c_ref, w_ref, a_ref, b_ref, m_ref, v_ref, g_ref, d_ref, mo_ref, vo_ref):
        gv = jnp.where(pl.program_id(0) == c_ref[0], a_ref[...], b_ref[...])
        g_ref[...] = gv
        d_ref[...], mo_ref[...], vo_ref[...] = _adamw_math(w_ref[...], gv, m_ref[...], v_ref[...])

    full = pl.BlockSpec((None, tr, C), lambda hh, i, c_ref: (hh, i, 0))
    half = pl.BlockSpec((tr, C), lambda hh, i, c_ref: (i, 0))
    outs = pl.pallas_call(
        body, name=name,
        grid_spec=pltpu.PrefetchScalarGridSpec(num_scalar_prefetch=1, grid=(2, hr // tr),
                                               in_specs=[full, half, half, full, full], out_specs=[full] * 4),
        out_shape=[jax.ShapeDtypeStruct((2, hr, C), F32)] * 4,
        compiler_params=_params(('parallel', 'parallel')))(
            c_idx, w.reshape(2, hr, C), mine, other, m.reshape(2, hr, C), v.reshape(2, hr, C))
    return [o.reshape(R, C) for o in outs]


def _place():
    x, y, c = lax.axis_index('x'), lax.axis_index('y'), lax.axis_index('c')
    return x, y, c, [(1 - x, y), (x, 1 - y), (1 - x, 1 - y)]


def _rcopy(src, dst, ssem, rsem, dev):
    return pltpu.make_async_remote_copy(src_ref=src, dst_ref=dst, send_sem=ssem, recv_sem=rsem, device_id=dev,
                                        device_id_type=MESH)


HBM = pl.BlockSpec(memory_space=pltpu.HBM)
SEM = pl.BlockSpec(memory_space=pltpu.SEMAPHORE)
EFFECT = pltpu.SideEffectType.DATAFLOW_SIDE_EFFECTING


def _in_hbm(a):
    return pltpu.with_memory_space_constraint(a, pltpu.HBM)


def _rows_part(shape, whole, half):
    return pl.ds(0, shape[0]) if whole else pl.ds(half * (shape[0] // 2), shape[0] // 2)


def gather_start(shards, whole):
    nT = len(shards)

    def body(*refs):
        srcs, lands = refs[:nT], refs[nT:2 * nT]
        ssem, rsem, token = refs[2 * nT], refs[2 * nT + 1], refs[-1]
        x, y, c, chips = _place()
        for t in range(nT):
            rows = _rows_part(shards[t].shape, whole[t], c)
            for k, (px, py) in enumerate(chips):
                _rcopy(srcs[t].at[rows], lands[t].at[2 * x + y, rows], ssem.at[3 * t + k], rsem.at[3 * t + k],
                       (px, py, c)).start()
        token[...] = jnp.zeros_like(token)

    zones = [lax.empty((N_CHIPS,) + s.shape, s.dtype) for s in shards]
    outs = pl.pallas_call(
        body, name='gather_start',
        out_shape=(pltpu.SemaphoreType.DMA((3 * nT,)), pltpu.SemaphoreType.DMA((3 * nT,)),
                   *[pltpu.HBM(s.shape, s.dtype) for s in shards], *[pltpu.HBM(z.shape, z.dtype) for z in zones],
                   jax.ShapeDtypeStruct((8, LANES), F32)),
        in_specs=[HBM] * (2 * nT), out_specs=(SEM, SEM, *[HBM] * (2 * nT), pl.BlockSpec(memory_space=pltpu.VMEM)),
        input_output_aliases={i: 2 + i for i in range(2 * nT)},
        compiler_params=pltpu.CompilerParams(has_side_effects=EFFECT))(*[_in_hbm(a) for a in list(shards) + zones])
    return outs[0], outs[1], outs[2:2 + nT], outs[2 + nT:2 + 2 * nT], outs[-1]


def gather_wait(name, t, shard, zone, ssem, rsem, after, whole):
    def body(src_ref, land_ref, ssem_ref, rsem_ref, after_ref, src_out, land_out):
        x, y, c, chips = _place()
        rows = _rows_part(shard.shape, whole, c)
        for k, (px, py) in enumerate(chips):
            cp = _rcopy(src_ref.at[rows], land_ref.at[2 * px + py, rows], ssem_ref.at[3 * t + k], rsem_ref.at[3 * t + k],
                        (px, py, c))
            cp.wait_send()
            cp.wait_recv()

    return pl.pallas_call(
        body, name=name, out_shape=(pltpu.HBM(shard.shape, shard.dtype), pltpu.HBM(zone.shape, zone.dtype)),
        in_specs=(HBM, HBM, SEM, SEM, ANY), out_specs=(HBM, HBM), input_output_aliases={0: 0, 1: 1},
        compiler_params=pltpu.CompilerParams(has_side_effects=EFFECT))(shard, zone, ssem, rsem, after)


def pair_swap(name, zone):
    hr = zone.shape[1] // 2

    def body(z_in, z_ref, ssem, rsem):
        x, y, c, chips = _place()
        cps = []
        for k, (px, py) in enumerate(chips):
            blk = z_ref.at[2 * px + py, pl.ds(c * hr, hr)]
            cps.append(_rcopy(blk, blk, ssem.at[k], rsem.at[k], (x, y, 1 - c)))
            cps[-1].start()
        for k, (px, py) in enumerate(chips):
            blk = z_ref.at[2 * px + py, pl.ds((1 - c) * hr, hr)]
            _rcopy(blk, blk, ssem.at[k], rsem.at[k], (x, y, 1 - c)).wait_recv()
        for cp in cps:
            cp.wait_send()

    return pl.pallas_call(
        body, name=name, in_specs=[ANY], out_specs=ANY, out_shape=jax.ShapeDtypeStruct(zone.shape, zone.dtype),
        input_output_aliases={0: 0},
        scratch_shapes=[pltpu.SemaphoreType.DMA((3,)), pltpu.SemaphoreType.DMA((3,))],
        compiler_params=_params())(zone)


def scatter_start(name, parts):
    def body(p_ref, l_ref, ssem, rsem, p_out, l_out, token):
        x, y, c, chips = _place()
        for k, (px, py) in enumerate(chips):
            _rcopy(p_ref.at[2 * px + py], l_ref.at[k], ssem.at[k], rsem.at[k], (px, py, c)).start()
        token[...] = jnp.zeros_like(token)

    zone = lax.empty((3,) + parts.shape[1:], parts.dtype)
    return pl.pallas_call(
        body, name=name,
        out_shape=(pltpu.SemaphoreType.DMA((3,)), pltpu.SemaphoreType.DMA((3,)), pltpu.HBM(parts.shape, parts.dtype),
                   pltpu.HBM(zone.shape, zone.dtype), jax.ShapeDtypeStruct((8, LANES), F32)),
        in_specs=[HBM, HBM], out_specs=(SEM, SEM, HBM, HBM, pl.BlockSpec(memory_space=pltpu.VMEM)),
        input_output_aliases={0: 2, 1: 3},
        compiler_params=pltpu.CompilerParams(has_side_effects=EFFECT))(_in_hbm(parts), _in_hbm(zone))


def scatter_wait(name, parts, zone, ssem, rsem, after):
    def body(p_ref, l_ref, ssem_ref, rsem_ref, after_ref, p_out, l_out):
        x, y, c, chips = _place()
        for k, (px, py) in enumerate(chips):
            cp = _rcopy(p_ref.at[2 * px + py], l_ref.at[k], ssem_ref.at[k], rsem_ref.at[k], (px, py, c))
            cp.wait_send()
            cp.wait_recv()

    return pl.pallas_call(
        body, name=name, out_shape=(pltpu.HBM(parts.shape, parts.dtype), pltpu.HBM(zone.shape, zone.dtype)),
        in_specs=(HBM, HBM, SEM, SEM, ANY), out_specs=(HBM, HBM), input_output_aliases={0: 0, 1: 1},
        compiler_params=pltpu.CompilerParams(has_side_effects=EFFECT))(parts, zone, ssem, rsem, after)


def pair_exchange(name, grads):
    nT = len(grads)

    def body(*refs):
        ins, outs = refs[:nT], refs[nT:2 * nT]
        ssem, rsem = refs[2 * nT:]
        x, y, c, _ = _place()
        cps = [_rcopy(ins[t].at[:, 1 - c], outs[t], ssem.at[t], rsem.at[t], (x, y, 1 - c)) for t in range(nT)]
        for cp in cps:
            cp.start()
        for cp in cps:
            cp.wait()

    return pl.pallas_call(
        body, name=name, in_specs=[ANY] * nT, out_specs=[ANY] * nT,
        out_shape=[jax.ShapeDtypeStruct((g.shape[0],) + g.shape[2:], g.dtype) for g in grads],
        scratch_shapes=[pltpu.SemaphoreType.DMA((nT,)), pltpu.SemaphoreType.DMA((nT,))],
        compiler_params=_params())(*grads)


def pair_add(name, g, got, c_idx):
    J, _, hr, C = g.shape
    tr = _row_tile(hr, C, min_rows=16)

    def body(c_ref, g_ref, r_ref, o_ref):
        o_ref[...] = (g_ref[...] + r_ref[...]).astype(BF16)

    return pl.pallas_call(
        body, name=name,
        grid_spec=pltpu.PrefetchScalarGridSpec(
            num_scalar_prefetch=1, grid=(J, hr // tr),
            in_specs=[pl.BlockSpec((None, None, tr, C), lambda j, i, c_ref: (j, c_ref[0], i, 0)),
                      pl.BlockSpec((None, tr, C), lambda j, i, c_ref: (j, i, 0))],
            out_specs=pl.BlockSpec((None, tr, C), lambda j, i, c_ref: (j, i, 0))),
        out_shape=jax.ShapeDtypeStruct((J, hr, C), BF16),
        compiler_params=_params(('parallel', 'parallel')))(c_idx, g, got)


def sum_chips(name, parts, landed, chip_idx):
    _, hr, C = parts.shape
    tr = _row_tile(hr, C, min_rows=16)

    def body(me_ref, p_ref, l_ref, o_ref):
        acc = p_ref[...].astype(F32)
        for k in range(3):
            acc = acc + l_ref[k].astype(F32)
        o_ref[...] = acc

    return pl.pallas_call(
        body, name=name,
        grid_spec=pltpu.PrefetchScalarGridSpec(
            num_scalar_prefetch=1, grid=(hr // tr,),
            in_specs=[pl.BlockSpec((None, tr, C), lambda i, me_ref: (me_ref[0], i, 0)),
                      pl.BlockSpec((3, tr, C), lambda i, me_ref: (0, i, 0))],
            out_specs=pl.BlockSpec((tr, C), lambda i, me_ref: (i, 0))),
        out_shape=jax.ShapeDtypeStruct((hr, C), F32),
        compiler_params=_params(('parallel',)))(chip_idx, parts, landed)


def pair_join(name, halves):
    nT = len(halves)

    def body(*refs):
        ins, outs = refs[:nT], refs[nT:2 * nT]
        ssem, rsem = refs[2 * nT:]
        x, y, c, _ = _place()
        cps = [_rcopy(ins[t], outs[t], ssem.at[t], rsem.at[t], (x, y, 1 - c)) for t in range(nT)]
        for cp in cps:
            cp.start()
        for cp in cps:
            cp.wait()

    return pl.pallas_call(
        body, name=name, in_specs=[ANY] * nT, out_specs=[ANY] * nT,
        out_shape=[jax.ShapeDtypeStruct(h.shape, h.dtype) for h in halves],
        scratch_shapes=[pltpu.SemaphoreType.DMA((nT,)), pltpu.SemaphoreType.DMA((nT,))],
        compiler_params=_params())(*halves)


def allreduce_small(buf):
    R = buf.shape[0]
    VM = pl.BlockSpec(memory_space=pltpu.VMEM)

    def body(x_ref, o_ref, all_ref, ssem, rsem, lsem):
        x, y, c, chips = _place()
        me, sibling = (x, y, c), (x, y, 1 - c)

        def rows(px, py, pc):
            return all_ref.at[pl.ds((4 * px + 2 * py + pc) * R, R), :]

        def copy(k, block, to, src=None):
            return _rcopy(rows(*block) if src is None else src, rows(*block), ssem.at[k], rsem.at[k], to)

        mine = pltpu.make_async_copy(x_ref, rows(*me), lsem)
        mine.start()
        first = [copy(0, me, sibling, src=x_ref)]
        first += [copy(1 + k, me, (*chip, c), src=x_ref) for k, chip in enumerate(chips)]
        for cp in first:
            cp.start()
        passed = [copy(4 + k, (*chip, c), sibling) for k, chip in enumerate(chips)]
        for k, chip in enumerate(chips):
            copy(1 + k, (*chip, c), me).wait_recv()
            passed[k].start()
        copy(0, sibling, me).wait_recv()
        for k, chip in enumerate(chips):
            copy(4 + k, (*chip, 1 - c), me).wait_recv()
        for cp in first + passed:
            cp.wait_send()
        mine.wait()
        acc = all_ref[0:R, :]
        for d in range(1, 8):
            acc = acc + all_ref[d * R:(d + 1) * R, :]
        o_ref[...] = acc

    return pl.pallas_call(
        body, name='allreduce_small', in_specs=[VM], out_specs=VM, out_shape=jax.ShapeDtypeStruct((R, LANES), F32),
        scratch_shapes=[pltpu.VMEM((8 * R, LANES), F32), pltpu.SemaphoreType.DMA((7,)), pltpu.SemaphoreType.DMA((7,)),
                        pltpu.SemaphoreType.DMA],
        compiler_params=_params())(buf)


_PACK = 8 * LANES


def _pack(arrs):
    flat = []
    for a in arrs:
        v = a.reshape(-1).astype(F32)
        flat.append(jnp.pad(v, (0, (-v.shape[0]) % _PACK)))
    return jnp.concatenate(flat).reshape(-1, LANES)


def _unpack(buf, shapes):
    out, off = [], 0
    flat = buf.reshape(-1)
    for sh in shapes:
        n = math.prod(sh)
        out.append(flat[off:off + n].reshape(sh))
        off += n + (-n) % _PACK
    return out


def kernel(x, mem, g_mix, w_in, b_f, g_q, g_k, conv_w, conv_b, w_ra, b_ra, w_ri, b_ri, lam, g_fox_out, g_lru_out, w_out, g_xattn, g_mem, w_cq, w_ckv, g_cq, g_ck, w_co, g_ffn, w_gate_up, w_down, loss_target, m_g_mix, m_w_in, m_b_f, m_g_q, m_g_k, m_conv_w, m_conv_b, m_w_ra, m_b_ra, m_w_ri, m_b_ri, m_lam, m_g_fox_out, m_g_lru_out, m_w_out, m_g_xattn, m_g_mem, m_w_cq, m_w_ckv, m_g_cq, m_g_ck, m_w_co, m_g_ffn, m_w_gate_up, m_w_down, v_g_mix, v_w_in, v_b_f, v_g_q, v_g_k, v_conv_w, v_conv_b, v_w_ra, v_b_ra, v_w_ri, v_b_ri, v_lam, v_g_fox_out, v_g_lru_out, v_w_out, v_g_xattn, v_g_mem, v_w_cq, v_w_ckv, v_g_cq, v_g_ck, v_w_co, v_g_ffn, v_w_gate_up, v_w_down):
    given = dict(locals())
    W = {n: given[n][0] for n in WEIGHTS}
    M1 = {n: given['m_' + n][0] for n in WEIGHTS}
    V1 = {n: given['v_' + n][0] for n in WEIGHTS}
    xs, ms, tgt = x[0], mem[0], loss_target[0]
    S, D = xs.shape
    H = W['b_f'].shape[0]
    FW = H * HEAD_DIM
    LW = W['lam'].shape[0]
    nb = W['w_ra'].shape[0]
    XW = W['w_cq'].shape[1]
    F = W['w_down'].shape[0] * N_CHIPS
    IN_W = W['w_in'].shape[1] * N_CHIPS
    assert FW == LW and LW == nb * LANES and IN_W == 3 * FW + H + 2 * LW and H <= 8
    T = _tile(S, (512, 256, 128))
    c_idx = lax.axis_index('c').astype(jnp.int32).reshape(1)
    chip = 2 * lax.axis_index('x') + lax.axis_index('y')
    chip_idx = chip.astype(jnp.int32).reshape(1)
    vec = lambda n: W[n].reshape(1, -1)

    order = ['conv_w'] + BIG
    own = {n: W[n].astype(BF16) for n in BIG}
    own['conv_w'] = W['conv_w'].reshape(-1, LANES)
    g_ssem, g_rsem, g_src, g_zone, g_tok = gather_start([own[n] for n in order], [n == 'conv_w' for n in order])

    def fetch(n, after):
        t = order.index(n)
        src, zone = gather_wait('gather_wait_' + n, t, g_src[t], g_zone[t], g_ssem, g_rsem, after, n == 'conv_w')
        if n != 'conv_w':
            zone = pair_swap('pair_swap_' + n, zone)
        return lax.dynamic_update_index_in_dim(zone, src, chip, 0)

    b_f_pad = jnp.pad(vec('b_f'), ((0, 0), (0, LANES - H)))
    u_off, g_off = 3 * FW // LANES, (3 * FW + LW) // LANES

    h1 = norm_fwd('norm_mix', xs, vec('g_mix') + g_tok[0:1, 0:1])
    conv_full = fetch('conv_w', h1).reshape(N_CHIPS, CONV_W, LW // N_CHIPS).transpose(1, 0, 2).reshape(CONV_W, LW)
    w_in_full = fetch('w_in', h1).transpose(1, 0, 2).reshape(D, IN_W)
    w5 = jnp.concatenate([w_in_full[:, :3 * FW], w_in_full[:, 3 * FW + H:]], axis=1)
    wf = jnp.pad(w_in_full[:, 3 * FW:3 * FW + H], ((0, 0), (0, LANES - H)))
    proj = _mm('proj_in', h1, w5, 'nn', F32)
    f_raw = _mm('proj_f', h1, wf, 'nn', F32)
    qn, kn, vb = qkv_fwd(proj, vec('g_q'), vec('g_k'), FW)
    cc = fgate_fwd(f_raw, b_f_pad)
    ct = cc[:, :8].T
    o_fox, lse = fox_fwd(qn, kn, vb, cc, ct, T)
    lru_w = (conv_full, vec('conv_b'), W['w_ra'], vec('b_ra'), W['w_ri'], vec('b_ri'), vec('lam'))
    y_lru = lru_fwd(proj, *lru_w, u_off, g_off)
    mixn = mix_fwd(o_fox, y_lru, vec('g_fox_out'), vec('g_lru_out'))
    w_out_f = fetch('w_out', mixn).reshape(2 * FW, D)
    x1 = _mm('proj_out', mixn, w_out_f, 'nn', F32, res=xs)

    hq = norm_fwd('norm_xq', x1, vec('g_xattn'))
    mn = norm_fwd('norm_mem', ms, vec('g_mem'))
    w_cq_f = fetch('w_cq', hq).reshape(D, XW)
    w_ckv_f = fetch('w_ckv', mn).reshape(D, 2 * XW)
    cq_raw = _mm('proj_cq', hq, w_cq_f, 'nn', F32)
    ckv = _mm('proj_ckv', mn, w_ckv_f, 'nn', F32)
    o_x = xattn_fwd(cq_raw, ckv, vec('g_cq'), vec('g_ck'))
    w_co_g = fetch('w_co', o_x)
    x2 = _mm_colsharded('proj_co', o_x, w_co_g, F32, res=x1)

    hf = norm_fwd('norm_ffn', x2, vec('g_ffn'))
    w_gu_g = fetch('w_gate_up', hf)
    gu = _mm_colsharded('proj_gate_up', hf, w_gu_g, F32)
    act = swiglu_fwd(gu, F)
    w_down_f = fetch('w_down', act).reshape(F, D)
    yv = _mm('proj_down', act, w_down_f, 'nn', F32, res=x2)
    dy, dyb, loss_blk = loss_head(yv, tgt)
    loss = lax.psum(loss_blk[0, 0], ('x', 'y', 'c'))

    gw, pending = {}, []

    def reduce_begin(n, g):
        sp = g.reshape(N_CHIPS, 2, g.shape[1] // 2, g.shape[2])
        (got,) = pair_exchange('pair_exchange_' + n, [sp])
        part = pair_add('pair_add_' + n, sp, got, c_idx)
        ssem, rsem, part, zone, tok = scatter_start('scatter_start_' + n, part)
        pending.append((n, part, zone, ssem, rsem))
        return tok[0:1, 0:1]

    dact = _mm('bwd_down_x', dyb, w_down_f, 'nt', F32)
    t_down = reduce_begin('w_down', _mm('bwd_down_w', act, dyb, 'tn', F32).reshape(N_CHIPS, F // N_CHIPS, D))
    dgu = swiglu_bwd(gu, dact, F)
    dhf = _mm_colsharded_t('bwd_gate_up_x', dgu, w_gu_g, F32)
    t_gu = reduce_begin('w_gate_up', _mm_grad_colsharded('bwd_gate_up_w', hf, dgu, N_CHIPS))
    dx2, dx2b, gw['g_ffn'] = norm_bwd('norm_ffn_bwd', x2, vec('g_ffn') + t_down + t_gu, dhf, res=dy)

    do_x = _mm_colsharded_t('bwd_co_x', dx2b, w_co_g, BF16)
    t_co = reduce_begin('w_co', _mm_grad_colsharded('bwd_co_w', o_x, dx2b, N_CHIPS))
    dcq_raw, dckv, gw['g_cq'], gw['g_ck'] = xattn_bwd(cq_raw, ckv, vec('g_cq') + t_co, vec('g_ck'), do_x)
    dhq = _mm('bwd_cq_x', dcq_raw, w_cq_f, 'nt', F32)
    t_cq = reduce_begin('w_cq', _mm('bwd_cq_w', hq, dcq_raw, 'tn', F32).reshape(N_CHIPS, D // N_CHIPS, XW))
    dmn = _mm('bwd_ckv_x', dckv, w_ckv_f, 'nt', F32)
    t_ckv = reduce_begin('w_ckv', _mm('bwd_ckv_w', mn, dckv, 'tn', F32).reshape(N_CHIPS, D // N_CHIPS, 2 * XW))
    (gw['g_mem'],) = norm_bwd('norm_mem_bwd', ms, vec('g_mem'), dmn, want_dx=False)
    dx1, dx1b, gw['g_xattn'] = norm_bwd('norm_xq_bwd', x1, vec('g_xattn') + t_cq + t_ckv, dhq, res=dx2)

    dmix = _mm('bwd_out_x', dx1b, w_out_f, 'nt', F32)
    t_out = reduce_begin('w_out', _mm('bwd_out_w', mixn, dx1b, 'tn', F32).reshape(N_CHIPS, 2 * FW // N_CHIPS, D))
    do_fox, delta, dy_lru, gw['g_fox_out'], gw['g_lru_out'] = mix_bwd(o_fox, y_lru, vec('g_fox_out') + t_out,
                                                                     vec('g_lru_out'), dmix)
    (du, dgate, gw['conv_w'], gw['conv_b'], gw['w_ra'], gw['b_ra'], gw['w_ri'], gw['b_ri'],
     gw['lam']) = lru_bwd(proj, dy_lru, *lru_w, u_off, g_off)
    dqn, delta2 = fox_bwd_q(qn, kn, vb, do_fox, cc, ct, lse, delta, T)
    dkn, dv, dct = fox_bwd_kv(qn, kn, vb, do_fox, cc, ct, lse, delta2, T)
    dq, dk, gw['g_q'], gw['g_k'] = qkv_bwd(proj, vec('g_q'), vec('g_k'), dqn, dkn, FW)
    dc = jnp.pad(dct.reshape(H, S).T, ((0, 0), (0, LANES - H)))
    df, db_f = fgate_bwd(f_raw, b_f_pad, dc, H)
    gw['b_f'] = db_f[:, :H]
    dproj = jnp.concatenate([dq, dk, dv, du, dgate], axis=1)
    dw5 = _mm('bwd_in_w', h1, dproj, 'tn', F32)
    dwf = _mm('bwd_f_w', h1, df, 'tn', F32)
    dw_in = jnp.concatenate([dw5[:, :3 * FW], dwf[:, :H], dw5[:, 3 * FW:]], axis=1)
    t_in = reduce_begin('w_in', dw_in.reshape(D, N_CHIPS, IN_W // N_CHIPS).transpose(1, 0, 2))
    dh_a = _mm('bwd_f_x', df, wf, 'nt', F32)
    dh1 = _mm('bwd_in_x', dproj, w5, 'nt', F32, res=dh_a)
    grad_x, _, gw['g_mix'] = norm_bwd('norm_mix_bwd', xs, vec('g_mix') + t_in, dh1, res=dx1)

    grads, delta_w, new_m, new_v = {}, {}, {}, {}
    small_shapes = [gw[n].shape for n in SMALL]
    summed = _unpack(allreduce_small(_pack([gw[n] for n in SMALL])), small_shapes)
    for n, g in zip(SMALL, summed):
        grads[n] = g.reshape(W[n].shape) if n != 'conv_w' else lax.dynamic_slice_in_dim(
            g, chip * (LW // N_CHIPS), LW // N_CHIPS, axis=1)

    for n, part, zone, ssem, rsem in pending:
        part, landed = scatter_wait('scatter_wait_' + n, part, zone, ssem, rsem, grad_x)
        mine = sum_chips('sum_chips_' + n, part, landed, chip_idx)
        (other,) = pair_join('pair_join_' + n, [mine])
        grads[n], delta_w[n], new_m[n], new_v[n] = adamw_halves('adamw_' + n, W[n], mine, other, M1[n], V1[n], c_idx)
    packs = [_pack([d[n] for n in SMALL]) for d in (W, grads, M1, V1)]
    shapes = [W[n].shape for n in SMALL]
    for d, res in zip((delta_w, new_m, new_v), adamw('adamw_small', *packs)):
        d.update(zip(SMALL, _unpack(res, shapes)))

    lead = lambda d: [d[n][None] for n in WEIGHTS]
    return (loss, grad_x[None], *lead(grads), *lead(delta_w), *lead(new_m), *lead(new_v))
```

```python
import functools
import math

import jax
import jax.numpy as jnp
from jax import lax
from jax.experimental import pallas as pl
from jax.experimental.pallas import tpu as pltpu

F32 = jnp.float32
BF16 = jnp.bfloat16
HEAD_DIM = 128
LANES = 128
LRU_C = 8.0
RMS_EPS = 1e-6
CONV_W = 4
ADAM_LR = 0.001
ADAM_B1 = 0.9
ADAM_B2 = 0.999
ADAM_EPS = 1e-08
ADAM_WD = 0.01
ADAM_STEP = 10
VMEM_LIMIT = 56 * 1024 * 1024
N_CHIPS = 4
MESH = pl.DeviceIdType.MESH
ANY = pl.BlockSpec(memory_space=pl.ANY)

WEIGHTS = ['g_mix', 'w_in', 'b_f', 'g_q', 'g_k', 'conv_w', 'conv_b', 'w_ra', 'b_ra', 'w_ri', 'b_ri', 'lam',
           'g_fox_out', 'g_lru_out', 'w_out', 'g_xattn', 'g_mem', 'w_cq', 'w_ckv', 'g_cq', 'g_ck', 'w_co', 'g_ffn',
           'w_gate_up', 'w_down']
BIG = ['w_in', 'w_out', 'w_cq', 'w_ckv', 'w_co', 'w_gate_up', 'w_down']
SMALL = [n for n in WEIGHTS if n not in BIG]


def _params(sem=None):
    if sem is None:
        return pltpu.CompilerParams(vmem_limit_bytes=VMEM_LIMIT)
    return pltpu.CompilerParams(dimension_semantics=sem, vmem_limit_bytes=VMEM_LIMIT)


def _tile(n, cands):
    for t in cands:
        if n % t == 0:
            return t
    return n


ROW_BLOCK_BYTES = 1 << 20


def _row_tile(n_rows, n_cols, min_rows=8):
    cands = [t for t in (512, 256, 128, 64, 32, 16, 8) if t >= min_rows and t * n_cols * 4 <= ROW_BLOCK_BYTES]
    return _tile(n_rows, cands or [min_rows])


def _sigmoid(z):
    return 1.0 / (1.0 + jnp.exp(-z))


def _softplus(z):
    return jnp.maximum(z, 0.0) + jnp.log(1.0 + jnp.exp(-jnp.abs(z)))


def _neg_expm1(z):
    series = -z * (1.0 + z * (0.5 + z * (1.0 / 6.0 + z * (1.0 / 24.0 + z * (1.0 / 120.0)))))
    return jnp.where(z > -0.25, series, 1.0 - jnp.exp(z))


_GELU_K = math.sqrt(2.0 / math.pi)


def _gelu_and_grad(z):
    inner = _GELU_K * (z + 0.044715 * z * z * z)
    t = jnp.tanh(inner)
    g = 0.5 * z * (1.0 + t)
    dg = 0.5 * (1.0 + t) + 0.5 * z * (1.0 - t * t) * _GELU_K * (1.0 + 3.0 * 0.044715 * z * z)
    return g, dg


def _rms(xv, g):
    r = lax.rsqrt(jnp.mean(xv * xv, axis=-1, keepdims=True) + RMS_EPS)
    return xv * r * g


def _rms_bwd(xv, g, dy):
    r = lax.rsqrt(jnp.mean(xv * xv, axis=-1, keepdims=True) + RMS_EPS)
    xh = xv * r
    dyg = dy * g
    dx = r * (dyg - xh * jnp.mean(dyg * xh, axis=-1, keepdims=True))
    return dx, jnp.sum(dy * xh, axis=0, keepdims=True)


def _heads(fn, n_heads, *arrs):
    outs = [fn(*[a[:, h * HEAD_DIM:(h + 1) * HEAD_DIM] for a in arrs]) for h in range(n_heads)]
    first = jnp.concatenate([o[0] for o in outs], axis=1) if n_heads > 1 else outs[0][0]
    rest = [functools.reduce(lambda p, q: p + q, [o[i] for o in outs]) for i in range(1, len(outs[0]))]
    return (first, *rest)


def _split3(v):
    hi = v.astype(BF16)
    r1 = v - hi.astype(F32)
    mid = r1.astype(BF16)
    lo = (r1 - mid.astype(F32)).astype(BF16)
    return hi, mid, lo


def _acc_out(ref, first, val):
    @pl.when(first)
    def _():
        ref[...] = val

    @pl.when(jnp.logical_not(first))
    def _():
        ref[...] += val


_DIMS = {'nn': (((1,), (0,)), ((), ())), 'nt': (((1,), (1,)), ((), ())), 'tn': (((0,), (0,)), ((), ()))}


MM_VMEM_BYTES = 36 * 1024 * 1024


def _k_tile(K, tm, tn, a, b, o_dtype, res):
    fixed = tm * tn * (2 * jnp.dtype(o_dtype).itemsize + 4 + (8 if res is not None else 0))
    per_k = 2 * (tm * a.dtype.itemsize + tn * b.dtype.itemsize)
    per_k += 2 * tm * (a.dtype.itemsize > 2) + 2 * tn * (b.dtype.itemsize > 2)
    units = K // LANES
    for d in sorted((d for d in range(1, units + 1) if units % d == 0), reverse=True):
        if fixed + d * LANES * per_k <= MM_VMEM_BYTES:
            return d * LANES
    return LANES


def _mm_call(name, a, b, mode, grid, a_spec, b_spec, o_spec, o_shape, o_dtype, acc_shape, res=None):
    nk = grid[2]
    dn = _DIMS[mode]

    def body(*refs):
        a_ref, b_ref = refs[:2]
        r_ref = refs[2] if res is not None else None
        o_ref = refs[3] if res is not None else refs[2]
        part = lax.dot_general(a_ref[...].astype(BF16), b_ref[...].astype(BF16), dn, preferred_element_type=F32)

        def finish(r):
            if r_ref is not None:
                r = r + r_ref[...]
            o_ref[...] = r.astype(o_dtype)

        if nk == 1:
            finish(part)
            return
        acc = refs[-1]
        k = pl.program_id(2)

        @pl.when(k == 0)
        def _():
            acc[...] = part

        @pl.when(k > 0)
        def _():
            acc[...] += part

        @pl.when(k == nk - 1)
        def _():
            finish(acc[...])

    ins = [a, b] + ([] if res is None else [res])
    specs = [a_spec, b_spec] + ([] if res is None else [o_spec])
    return pl.pallas_call(
        body, name=name, grid=grid, in_specs=specs, out_specs=o_spec,
        out_shape=jax.ShapeDtypeStruct(o_shape, o_dtype),
        scratch_shapes=[] if nk == 1 else [pltpu.VMEM(acc_shape, F32)],
        compiler_params=_params(('parallel', 'parallel', 'arbitrary')))(*ins)


def _mm(name, a, b, mode, o_dtype, res=None):
    if mode == 'tn':
        K, M = a.shape
    else:
        M, K = a.shape
    N = b.shape[0] if mode == 'nt' else b.shape[1]
    tm = _tile(M, (1024, 512, 256, 128))
    tn = _tile(N, (1024, 512, 256, 128))
    tk = _k_tile(K, tm, tn, a, b, o_dtype, res)
    a_spec = (pl.BlockSpec((tk, tm), lambda m, n, k: (k, m)) if mode == 'tn'
              else pl.BlockSpec((tm, tk), lambda m, n, k: (m, k)))
    b_spec = (pl.BlockSpec((tn, tk), lambda m, n, k: (n, k)) if mode == 'nt'
              else pl.BlockSpec((tk, tn), lambda m, n, k: (k, n)))
    o_spec = pl.BlockSpec((tm, tn), lambda m, n, k: (m, n))
    return _mm_call(name, a, b, mode, (M // tm, N // tn, K // tk), a_spec, b_spec, o_spec, (M, N), o_dtype,
                    (tm, tn), res)


def _mm_colsharded(name, a, w, o_dtype, res=None):
    M, K = a.shape
    J, _, Nj = w.shape
    tm = _tile(M, (1024, 512, 256, 128))
    tn = _tile(Nj, (1408, 1024, 512, 256, 128))
    tk = _k_tile(K, tm, tn, a, w, o_dtype, res)
    per = Nj // tn
    return _mm_call(name, a, w, 'nn', (M // tm, J * per, K // tk),
                    pl.BlockSpec((tm, tk), lambda m, n, k: (m, k)),
                    pl.BlockSpec((None, tk, tn), lambda m, n, k: (n // per, k, n % per)),
                    pl.BlockSpec((tm, tn), lambda m, n, k: (m, n)), (M, J * Nj), o_dtype, (tm, tn), res)


def _mm_colsharded_t(name, a, w, o_dtype):
    M = a.shape[0]
    J, K, Nj = w.shape
    tm = _tile(M, (1024, 512, 256, 128))
    tn = _tile(K, (1024, 512, 256, 128))
    tk = _k_tile(Nj, tm, tn, a, w, o_dtype, None)
    per = Nj // tk
    return _mm_call(name, a, w, 'nt', (M // tm, K // tn, J * per),
                    pl.BlockSpec((tm, tk), lambda m, n, k: (m, k)),
                    pl.BlockSpec((None, tn, tk), lambda m, n, k: (k // per, n, k % per)),
                    pl.BlockSpec((tm, tn), lambda m, n, k: (m, n)), (M, K), o_dtype, (tm, tn))


def _mm_grad_colsharded(name, a, dy, J, o_dtype):
    S, M = a.shape
    Nj = dy.shape[1] // J
    tm = _tile(M, (1024, 512, 256, 128))
    tn = _tile(Nj, (1408, 1024, 512, 256, 128))
    tk = _k_tile(S, tm, tn, a, dy, o_dtype, None)
    per = Nj // tn
    return _mm_call(name, a, dy, 'tn', (M // tm, J * per, S // tk),
                    pl.BlockSpec((tk, tm), lambda m, n, k: (k, m)),
                    pl.BlockSpec((tk, tn), lambda m, n, k: (k, n)),
                    pl.BlockSpec((None, tm, tn), lambda m, n, k: (n // per, m, n % per)), (J, M, Nj), o_dtype, (tm, tn))


def _rows_call(name, body, n_rows, tr, ins, outs):
    return pl.pallas_call(
        body, name=name, grid=(n_rows // tr,), in_specs=[s for _, s in ins], out_specs=[s for _, _, s in outs],
        out_shape=[jax.ShapeDtypeStruct(sh, dt) for sh, dt, _ in outs],
        compiler_params=_params(('arbitrary',)))(*[a for a, _ in ins])


def _rb(tr, w, cb=0):
    return pl.BlockSpec((tr, w), lambda i: (i, cb))


def _fb(shape):
    nd = len(shape)
    return pl.BlockSpec(shape, lambda i: (0,) * nd)


def norm_fwd(name, xv, g):
    S, D = xv.shape
    tr = _tile(S, (256, 128))

    def body(x_ref, g_ref, o_ref):
        o_ref[...] = _rms(x_ref[...], g_ref[...]).astype(BF16)

    return _rows_call(name, body, S, tr, [(xv, _rb(tr, D)), (g, _fb((1, D)))], [((S, D), BF16, _rb(tr, D))])[0]


def norm_bwd(name, xv, g, dy, res=None, want_dx=True):
    S, D = xv.shape
    tr = _tile(S, (256, 128))

    def body(*refs):
        if res is None:
            x_ref, g_ref, dy_ref = refs[:3]
            outs = refs[3:]
            r_ref = None
        else:
            x_ref, g_ref, dy_ref, r_ref = refs[:4]
            outs = refs[4:]
        dx, dg = _rms_bwd(x_ref[...], g_ref[...], dy_ref[...])
        if r_ref is not None:
            dx = dx + r_ref[...]
        if want_dx:
            outs[0][...] = dx
            outs[1][...] = dx.astype(BF16)
        _acc_out(outs[-1], pl.program_id(0) == 0, dg)

    ins = [(xv, _rb(tr, D)), (g, _fb((1, D))), (dy, _rb(tr, D))] + ([] if res is None else [(res, _rb(tr, D))])
    outs = ([((S, D), F32, _rb(tr, D)), ((S, D), BF16, _rb(tr, D))] if want_dx else []) + [((1, D), F32, _fb((1, D)))]
    return _rows_call(name, body, S, tr, ins, outs)


def qkv_fwd(proj, g_q, g_k, FW):
    S = proj.shape[0]
    H = FW // HEAD_DIM
    tr = _tile(S, (256, 128))

    def body(q_ref, k_ref, v_ref, gq_ref, gk_ref, qo, ko, vo):
        qo[...] = _heads(lambda t: (_rms(t, gq_ref[...]),), H, q_ref[...])[0].astype(BF16)
        ko[...] = _heads(lambda t: (_rms(t, gk_ref[...]),), H, k_ref[...])[0].astype(BF16)
        vo[...] = v_ref[...].astype(BF16)

    o = ((S, FW), BF16, _rb(tr, FW))
    return _rows_call('qkv_fwd', body, S, tr,
                      [(proj, _rb(tr, FW, 0)), (proj, _rb(tr, FW, 1)), (proj, _rb(tr, FW, 2)),
                       (g_q, _fb((1, HEAD_DIM))), (g_k, _fb((1, HEAD_DIM)))], [o, o, o])


def qkv_bwd(proj, g_q, g_k, dqn, dkn, FW):
    S = proj.shape[0]
    H = FW // HEAD_DIM
    tr = _tile(S, (256, 128))

    def body(q_ref, k_ref, gq_ref, gk_ref, dq_ref, dk_ref, dqo, dko, dgq, dgk):
        dq, gq = _heads(lambda t, d: _rms_bwd(t, gq_ref[...], d), H, q_ref[...], dq_ref[...])
        dk, gk = _heads(lambda t, d: _rms_bwd(t, gk_ref[...], d), H, k_ref[...], dk_ref[...])
        dqo[...] = dq.astype(BF16)
        dko[...] = dk.astype(BF16)
        first = pl.program_id(0) == 0
        _acc_out(dgq, first, gq)
        _acc_out(dgk, first, gk)

    o = ((S, FW), BF16, _rb(tr, FW))
    og = ((1, HEAD_DIM), F32, _fb((1, HEAD_DIM)))
    return _rows_call('qkv_bwd', body, S, tr,
                      [(proj, _rb(tr, FW, 0)), (proj, _rb(tr, FW, 1)), (g_q, _fb((1, HEAD_DIM))),
                       (g_k, _fb((1, HEAD_DIM))), (dqn, _rb(tr, FW)), (dkn, _rb(tr, FW))], [o, o, og, og])


def _tri(n, upper):
    r = lax.broadcasted_iota(jnp.int32, (n, n), 0)
    c = lax.broadcasted_iota(jnp.int32, (n, n), 1)
    return jnp.where((c >= r) if upper else (c <= r), 1.0, 0.0).astype(BF16)


def _blocked_cumsum(val, S, blk, reverse):
    tri = _tri(blk, reverse)
    order = range(S // blk - 1, -1, -1) if reverse else range(S // blk)
    carry = jnp.zeros((1, LANES), F32)
    outs = {}
    for bi in order:
        part = val[bi * blk:(bi + 1) * blk]
        acc = carry
        for piece in _split3(part):
            acc = acc + jnp.dot(tri, piece, preferred_element_type=F32)
        outs[bi] = acc
        carry = carry + jnp.sum(part, axis=0, keepdims=True)
    return jnp.concatenate([outs[bi] for bi in range(S // blk)], axis=0)


def fgate_fwd(f_raw, b_f_pad):
    S = f_raw.shape[0]
    blk = _tile(S, (256, 128))

    def body(f_ref, b_ref, c_ref):
        z = f_ref[...] + b_ref[...]
        c_ref[...] = _blocked_cumsum(-_softplus(-z), S, blk, False)

    return pl.pallas_call(body, name='fgate_fwd', grid=(1,), in_specs=[_fb((S, LANES)), _fb((1, LANES))],
                          out_specs=_fb((S, LANES)), out_shape=jax.ShapeDtypeStruct((S, LANES), F32),
                          compiler_params=_params(('arbitrary',)))(f_raw, b_f_pad)


def fgate_bwd(f_raw, b_f_pad, dc, H):
    S = f_raw.shape[0]
    blk = _tile(S, (256, 128))

    def body(f_ref, b_ref, dc_ref, df_ref, db_ref):
        z = f_ref[...] + b_ref[...]
        dlogf = _blocked_cumsum(dc_ref[...], S, blk, True)
        lane = lax.broadcasted_iota(jnp.int32, (S, LANES), 1)
        df = jnp.where(lane < H, dlogf * _sigmoid(-z), 0.0)
        df_ref[...] = df.astype(BF16)
        db_ref[...] = jnp.sum(df, axis=0, keepdims=True)

    return pl.pallas_call(body, name='fgate_bwd', grid=(1,),
                          in_specs=[_fb((S, LANES)), _fb((1, LANES)), _fb((S, LANES))],
                          out_specs=[_fb((S, LANES)), _fb((1, LANES))],
                          out_shape=[jax.ShapeDtypeStruct((S, LANES), BF16), jax.ShapeDtypeStruct((1, LANES), F32)],
                          compiler_params=_params(('arbitrary',)))(f_raw, b_f_pad, dc)


def _fox_logits(q, k, c_blk, ct_blk, h, i, j, T):
    s = lax.dot_general(q, k, _DIMS['nt'], preferred_element_type=F32) * (1.0 / math.sqrt(HEAD_DIM))
    lane = lax.broadcasted_iota(jnp.int32, c_blk.shape, 1)
    cq = jnp.sum(jnp.where(lane == h, c_blk, 0.0), axis=1, keepdims=True)
    sub = lax.broadcasted_iota(jnp.int32, ct_blk.shape, 0)
    ck = jnp.sum(jnp.where(sub == h, ct_blk, 0.0), axis=0, keepdims=True)
    rows = i * T + lax.broadcasted_iota(jnp.int32, (T, T), 0)
    cols = j * T + lax.broadcasted_iota(jnp.int32, (T, T), 1)
    return jnp.where(cols <= rows, s + cq - ck, -jnp.inf)


def fox_fwd(qn, kn, vb, c, ct, T):
    S, FW = qn.shape
    H = FW // HEAD_DIM
    Hp = ct.shape[0]
    n = S // T

    def body(q_ref, k_ref, v_ref, c_ref, ct_ref, o_ref, lse_ref, m_s, l_s, acc_s):
        h, i, j = pl.program_id(0), pl.program_id(1), pl.program_id(2)

        @pl.when(j == 0)
        def _():
            m_s[...] = jnp.full_like(m_s, -jnp.inf)
            l_s[...] = jnp.zeros_like(l_s)
            acc_s[...] = jnp.zeros_like(acc_s)

        @pl.when(j <= i)
        def _():
            s = _fox_logits(q_ref[...], k_ref[...], c_ref[...], ct_ref[...], h, i, j, T)
            m_new = jnp.maximum(m_s[...], jnp.max(s, axis=1, keepdims=True))
            alpha = jnp.exp(m_s[...] - m_new)
            p = jnp.exp(s - m_new)
            l_s[...] = alpha * l_s[...] + jnp.sum(p, axis=1, keepdims=True)
            acc_s[...] = alpha * acc_s[...] + jnp.dot(p.astype(BF16), v_ref[...], preferred_element_type=F32)
            m_s[...] = m_new

        @pl.when(j == i)
        def _():
            o_ref[...] = acc_s[...] / l_s[...]
            lse_ref[...] = jnp.broadcast_to(m_s[...] + jnp.log(l_s[...]), (T, LANES))

    qs = pl.BlockSpec((T, HEAD_DIM), lambda h, i, j: (i, h))
    ks = pl.BlockSpec((T, HEAD_DIM), lambda h, i, j: (jnp.minimum(j, i), h))
    return pl.pallas_call(
        body, name='fox_fwd', grid=(H, n, n),
        in_specs=[qs, ks, ks, pl.BlockSpec((T, LANES), lambda h, i, j: (i, 0)),
                  pl.BlockSpec((Hp, T), lambda h, i, j: (0, jnp.minimum(j, i)))],
        out_specs=[qs, pl.BlockSpec((None, T, LANES), lambda h, i, j: (h, i, 0))],
        out_shape=[jax.ShapeDtypeStruct((S, FW), F32), jax.ShapeDtypeStruct((H, S, LANES), F32)],
        scratch_shapes=[pltpu.VMEM((T, 1), F32), pltpu.VMEM((T, 1), F32), pltpu.VMEM((T, HEAD_DIM), F32)],
        compiler_params=_params(('parallel', 'parallel', 'arbitrary')))(qn, kn, vb, c, ct)


def _fox_p_ds(q_ref, k_ref, v_ref, do_ref, c_ref, ct_ref, lse_ref, dl_ref, h, i, j, T):
    s = _fox_logits(q_ref[...], k_ref[...], c_ref[...], ct_ref[...], h, i, j, T)
    p = jnp.exp(s - jnp.tile(lse_ref[...], (1, T // LANES)))
    dp = lax.dot_general(do_ref[...], v_ref[...], _DIMS['nt'], preferred_element_type=F32)
    ds = p * (dp - jnp.tile(dl_ref[...], (1, T // LANES)))
    return p, dp, ds


def fox_bwd_q(qn, kn, vb, do, c, ct, lse, dl, T):
    S, FW = qn.shape
    H = FW // HEAD_DIM
    Hp = ct.shape[0]
    n = S // T

    def body(q_ref, k_ref, v_ref, do_ref, c_ref, ct_ref, lse_ref, dl_ref, dq_ref, dl2_ref, acc_s, rs_s):
        h, i, j = pl.program_id(0), pl.program_id(1), pl.program_id(2)

        @pl.when(j == 0)
        def _():
            acc_s[...] = jnp.zeros_like(acc_s)
            rs_s[...] = jnp.zeros_like(rs_s)

        @pl.when(j <= i)
        def _():
            p, dp, ds = _fox_p_ds(q_ref, k_ref, v_ref, do_ref, c_ref, ct_ref, lse_ref, dl_ref, h, i, j, T)
            acc_s[...] += jnp.dot(ds.astype(BF16), k_ref[...], preferred_element_type=F32)
            rs_s[...] += jnp.sum(p * dp, axis=1, keepdims=True)

        @pl.when(j == i)
        def _():
            dq_ref[...] = acc_s[...] * (1.0 / math.sqrt(HEAD_DIM))
            dl2_ref[...] = jnp.broadcast_to(rs_s[...], (T, LANES))

    qs = pl.BlockSpec((T, HEAD_DIM), lambda h, i, j: (i, h))
    ks = pl.BlockSpec((T, HEAD_DIM), lambda h, i, j: (jnp.minimum(j, i), h))
    st = pl.BlockSpec((None, T, LANES), lambda h, i, j: (h, i, 0))
    return pl.pallas_call(
        body, name='fox_bwd_q', grid=(H, n, n),
        in_specs=[qs, ks, ks, qs, pl.BlockSpec((T, LANES), lambda h, i, j: (i, 0)),
                  pl.BlockSpec((Hp, T), lambda h, i, j: (0, jnp.minimum(j, i))), st, st],
        out_specs=[qs, st], out_shape=[jax.ShapeDtypeStruct((S, FW), F32), jax.ShapeDtypeStruct((H, S, LANES), F32)],
        scratch_shapes=[pltpu.VMEM((T, HEAD_DIM), F32), pltpu.VMEM((T, 1), F32)],
        compiler_params=_params(('parallel', 'parallel', 'arbitrary')))(qn, kn, vb, do, c, ct, lse, dl)


def fox_bwd_kv(qn, kn, vb, do, c, ct, lse, dl, T):
    S, FW = qn.shape
    H = FW // HEAD_DIM
    Hp = ct.shape[0]
    n = S // T

    def body(q_ref, k_ref, v_ref, do_ref, c_ref, ct_ref, lse_ref, dl_ref, dk_ref, dv_ref, dc_ref, dk_s, dv_s, dc_s):
        h, j, i = pl.program_id(0), pl.program_id(1), pl.program_id(2)

        @pl.when(i == 0)
        def _():
            dk_s[...] = jnp.zeros_like(dk_s)
            dv_s[...] = jnp.zeros_like(dv_s)
            dc_s[...] = jnp.zeros_like(dc_s)

        @pl.when(i >= j)
        def _():
            p, _, ds = _fox_p_ds(q_ref, k_ref, v_ref, do_ref, c_ref, ct_ref, lse_ref, dl_ref, h, i, j, T)
            dv_s[...] += lax.dot_general(p.astype(BF16), do_ref[...], _DIMS['tn'], preferred_element_type=F32)
            dk_s[...] += lax.dot_general(ds.astype(BF16), q_ref[...], _DIMS['tn'], preferred_element_type=F32)
            dc_s[...] += jnp.sum(ds, axis=0, keepdims=True)

        @pl.when(i == n - 1)
        def _():
            dk_ref[...] = dk_s[...] * (1.0 / math.sqrt(HEAD_DIM))
            dv_ref[...] = dv_s[...].astype(BF16)
            dc_ref[...] = -dc_s[...]

    qs = pl.BlockSpec((T, HEAD_DIM), lambda h, j, i: (jnp.maximum(i, j), h))
    ks = pl.BlockSpec((T, HEAD_DIM), lambda h, j, i: (j, h))
    st = pl.BlockSpec((None, T, LANES), lambda h, j, i: (h, jnp.maximum(i, j), 0))
    return pl.pallas_call(
        body, name='fox_bwd_kv', grid=(H, n, n),
        in_specs=[qs, ks, ks, qs, pl.BlockSpec((T, LANES), lambda h, j, i: (jnp.maximum(i, j), 0)),
                  pl.BlockSpec((Hp, T), lambda h, j, i: (0, j)), st, st],
        out_specs=[ks, ks, pl.BlockSpec((None, 1, T), lambda h, j, i: (h, 0, j))],
        out_shape=[jax.ShapeDtypeStruct((S, FW), F32), jax.ShapeDtypeStruct((S, FW), BF16),
                   jax.ShapeDtypeStruct((H, 1, S), F32)],
        scratch_shapes=[pltpu.VMEM((T, HEAD_DIM), F32), pltpu.VMEM((T, HEAD_DIM), F32), pltpu.VMEM((1, T), F32)],
        compiler_params=_params(('parallel', 'parallel', 'arbitrary')))(qn, kn, vb, do, c, ct, lse, dl)


def _shift_down(v, d, rows, fill):
    return jnp.where(rows >= d, pltpu.roll(v, d, 0), fill)


def _shift_up(v, d, rows, S, fill):
    return jnp.where(rows < S - d, pltpu.roll(v, S - d, 0), fill)


def _scan(a, b, rows, S, reverse):
    d = 1
    while d < S:
        if reverse:
            a_s, b_s = _shift_up(a, d, rows, S, 1.0), _shift_up(b, d, rows, S, 0.0)
        else:
            a_s, b_s = _shift_down(a, d, rows, 1.0), _shift_down(b, d, rows, 0.0)
        b = a * b_s + b
        a = a * a_s
        d *= 2
    return b


def _lru_forward(u, cw, cb, wra, bra, wri, bri, lam, rows):
    uc = cb + cw[CONV_W - 1] * u
    for d in range(1, CONV_W):
        uc = uc + cw[CONV_W - 1 - d] * _shift_down(u, d, rows, 0.0)
    ucb = uc.astype(BF16)
    r = _sigmoid(jnp.dot(ucb, wra.astype(BF16), preferred_element_type=F32) + bra)
    ig = _sigmoid(jnp.dot(ucb, wri.astype(BF16), preferred_element_type=F32) + bri)
    sp = _softplus(-lam)
    log_a = -LRU_C * r * sp
    a = jnp.exp(log_a)
    sq = jnp.sqrt(_neg_expm1(2.0 * log_a))
    iu = ig * uc
    hseq = _scan(a, sq * iu, rows, u.shape[0], False)
    return uc, ucb, r, ig, sp, a, sq, iu, hseq


def _lru_specs(S, n_u, n_g):
    col = lambda off: pl.BlockSpec((S, LANES), lambda cbk: (0, off + cbk))
    vec = pl.BlockSpec((1, LANES), lambda cbk: (0, cbk))
    mat = pl.BlockSpec((None, LANES, LANES), lambda cbk: (cbk, 0, 0))
    cw = pl.BlockSpec((CONV_W, LANES), lambda cbk: (0, cbk))
    return col, vec, mat, cw


def lru_fwd(proj, conv_w, conv_b, w_ra, b_ra, w_ri, b_ri, lam, u_off, g_off):
    S = proj.shape[0]
    nb = w_ra.shape[0]
    col, vec, mat, cws = _lru_specs(S, u_off, g_off)

    def body(u_ref, g_ref, cw_ref, cb_ref, wra_ref, bra_ref, wri_ref, bri_ref, lam_ref, y_ref):
        rows = lax.broadcasted_iota(jnp.int32, (S, LANES), 0)
        cw = [cw_ref[t:t + 1, :] for t in range(CONV_W)]
        hseq = _lru_forward(u_ref[...], cw, cb_ref[...], wra_ref[...], bra_ref[...], wri_ref[...],
                            bri_ref[...], lam_ref[...], rows)[-1]
        y_ref[...] = hseq * _gelu_and_grad(g_ref[...])[0]

    return pl.pallas_call(
        body, name='lru_fwd', grid=(nb,),
        in_specs=[col(u_off), col(g_off), cws, vec, mat, vec, mat, vec, vec], out_specs=col(0),
        out_shape=jax.ShapeDtypeStruct((S, nb * LANES), F32),
        compiler_params=_params(('parallel',)))(proj, proj, conv_w, conv_b, w_ra, b_ra, w_ri, b_ri, lam)


def lru_bwd(proj, dy, conv_w, conv_b, w_ra, b_ra, w_ri, b_ri, lam, u_off, g_off):
    S = proj.shape[0]
    nb = w_ra.shape[0]
    LW = nb * LANES
    col, vec, mat, cws = _lru_specs(S, u_off, g_off)

    def body(u_ref, g_ref, dy_ref, cw_ref, cb_ref, wra_ref, bra_ref, wri_ref, bri_ref, lam_ref,
             du_ref, dg_ref, dcw_ref, dcb_ref, dwra_ref, dbra_ref, dwri_ref, dbri_ref, dlam_ref):
        rows = lax.broadcasted_iota(jnp.int32, (S, LANES), 0)
        u, lam_v = u_ref[...], lam_ref[...]
        cw = [cw_ref[t:t + 1, :] for t in range(CONV_W)]
        wra, wri = wra_ref[...].astype(BF16), wri_ref[...].astype(BF16)
        uc, ucb, r, ig, sp, a, sq, iu, hseq = _lru_forward(u, cw, cb_ref[...], wra, bra_ref[...], wri, bri_ref[...],
                                                           lam_v, rows)
        gl, dgl = _gelu_and_grad(g_ref[...])
        dy_v = dy_ref[...]
        dg_ref[...] = (dy_v * hseq * dgl).astype(BF16)
        G = _scan(_shift_up(a, 1, rows, S, 0.0), dy_v * gl, rows, S, True)
        da = G * _shift_down(hseq, 1, rows, 0.0)
        diu = G * sq
        dsq = G * iu
        dlog_a = da * a - dsq * a * a / jnp.maximum(sq, 1e-30)
        dr = dlog_a * (-LRU_C * sp)
        dsp = jnp.sum(dlog_a * (-LRU_C * r), axis=0, keepdims=True)
        dlam_ref[...] = -dsp * _sigmoid(-lam_v)
        dzr = dr * r * (1.0 - r)
        dzi = diu * uc * ig * (1.0 - ig)
        dzrb, dzib = dzr.astype(BF16), dzi.astype(BF16)
        duc = (diu * ig + lax.dot_general(dzrb, wra, _DIMS['nt'], preferred_element_type=F32)
               + lax.dot_general(dzib, wri, _DIMS['nt'], preferred_element_type=F32))
        dwra_ref[...] = lax.dot_general(ucb, dzrb, _DIMS['tn'], preferred_element_type=F32)
        dwri_ref[...] = lax.dot_general(ucb, dzib, _DIMS['tn'], preferred_element_type=F32)
        dbra_ref[...] = jnp.sum(dzr, axis=0, keepdims=True)
        dbri_ref[...] = jnp.sum(dzi, axis=0, keepdims=True)
        dcb_ref[...] = jnp.sum(duc, axis=0, keepdims=True)
        du = cw[CONV_W - 1] * duc
        dcw_ref[CONV_W - 1:CONV_W, :] = jnp.sum(duc * u, axis=0, keepdims=True)
        for d in range(1, CONV_W):
            du = du + cw[CONV_W - 1 - d] * _shift_up(duc, d, rows, S, 0.0)
            dcw_ref[CONV_W - 1 - d:CONV_W - d, :] = jnp.sum(duc * _shift_down(u, d, rows, 0.0), axis=0, keepdims=True)
        du_ref[...] = du.astype(BF16)

    sd = jax.ShapeDtypeStruct
    return pl.pallas_call(
        body, name='lru_bwd', grid=(nb,),
        in_specs=[col(u_off), col(g_off), col(0), cws, vec, mat, vec, mat, vec, vec],
        out_specs=[col(0), col(0), cws, vec, mat, vec, mat, vec, vec],
        out_shape=[sd((S, LW), BF16), sd((S, LW), BF16), sd((CONV_W, LW), F32), sd((1, LW), F32),
                   sd((nb, LANES, LANES), F32), sd((1, LW), F32), sd((nb, LANES, LANES), F32), sd((1, LW), F32),
                   sd((1, LW), F32)],
        compiler_params=_params(('parallel',)))(proj, proj, dy, conv_w, conv_b, w_ra, b_ra, w_ri, b_ri, lam)


def mix_fwd(o_fox, y_lru, g_fox, g_lru):
    S, FW = o_fox.shape
    tr = _tile(S, (256, 128))

    def body(o_ref, y_ref, gf_ref, gl_ref, m_ref):
        m_ref[...] = jnp.concatenate([_rms(o_ref[...], gf_ref[...]), _rms(y_ref[...], gl_ref[...])],
                                     axis=1).astype(BF16)

    return _rows_call('mix_fwd', body, S, tr,
                      [(o_fox, _rb(tr, FW)), (y_lru, _rb(tr, FW)), (g_fox, _fb((1, FW))), (g_lru, _fb((1, FW)))],
                      [((S, 2 * FW), BF16, _rb(tr, 2 * FW))])[0]


def mix_bwd(o_fox, y_lru, g_fox, g_lru, dmix):
    S, FW = o_fox.shape
    H = FW // HEAD_DIM
    tr = _tile(S, (256, 128))

    def body(o_ref, y_ref, gf_ref, gl_ref, df_ref, dl_ref, do_ref, dlt_ref, dy_ref, dgf_ref, dgl_ref):
        o = o_ref[...]
        do, dgf = _rms_bwd(o, gf_ref[...], df_ref[...])
        dyl, dgl = _rms_bwd(y_ref[...], gl_ref[...], dl_ref[...])
        do_ref[...] = do.astype(BF16)
        dy_ref[...] = dyl
        prod = do * o
        for h in range(H):
            dlt_ref[h] = jnp.broadcast_to(
                jnp.sum(prod[:, h * HEAD_DIM:(h + 1) * HEAD_DIM], axis=1, keepdims=True), (tr, LANES))
        first = pl.program_id(0) == 0
        _acc_out(dgf_ref, first, dgf)
        _acc_out(dgl_ref, first, dgl)

    g = _fb((1, FW))
    return _rows_call('mix_bwd', body, S, tr,
                      [(o_fox, _rb(tr, FW)), (y_lru, _rb(tr, FW)), (g_fox, g), (g_lru, g), (dmix, _rb(tr, FW, 0)),
                       (dmix, _rb(tr, FW, 1))],
                      [((S, FW), BF16, _rb(tr, FW)), ((H, S, LANES), F32, pl.BlockSpec((H, tr, LANES), lambda i: (0, i, 0))),
                       ((S, FW), F32, _rb(tr, FW)), ((1, FW), F32, g), ((1, FW), F32, g)])


def _xattn_heads(cq_raw, ckv, g_cq, g_ck, XW):
    out = []
    for h in range(XW // HEAD_DIM):
        sl = slice(h * HEAD_DIM, (h + 1) * HEAD_DIM)
        out.append((cq_raw[:, sl], _rms(cq_raw[:, sl], g_cq), ckv[:, sl], _rms(ckv[:, sl], g_ck),
                    ckv[:, XW + h * HEAD_DIM:XW + (h + 1) * HEAD_DIM].astype(BF16)))
    return out


def xattn_fwd(cq_raw, ckv, g_cq, g_ck):
    S, XW = cq_raw.shape
    M = ckv.shape[0]
    tr = _tile(S, (512, 256, 128))

    def body(q_ref, kv_ref, gq_ref, gk_ref, o_ref):
        outs = []
        for _, qn, _, kn, v in _xattn_heads(q_ref[...], kv_ref[...], gq_ref[...], gk_ref[...], XW):
            s = lax.dot_general(qn.astype(BF16), kn.astype(BF16), _DIMS['nt'], preferred_element_type=F32)
            s = s / math.sqrt(HEAD_DIM)
            p = jnp.exp(s - jnp.max(s, axis=1, keepdims=True))
            p = p / jnp.sum(p, axis=1, keepdims=True)
            outs.append(jnp.dot(p.astype(BF16), v, preferred_element_type=F32))
        o_ref[...] = jnp.concatenate(outs, axis=1).astype(BF16)

    g = _fb((1, HEAD_DIM))
    return _rows_call('xattn_fwd', body, S, tr,
                      [(cq_raw, _rb(tr, XW)), (ckv, _fb((M, 2 * XW))), (g_cq, g), (g_ck, g)],
                      [((S, XW), BF16, _rb(tr, XW))])[0]


def xattn_bwd(cq_raw, ckv, g_cq, g_ck, do):
    S, XW = cq_raw.shape
    M = ckv.shape[0]
    tr = _tile(S, (512, 256, 128))
    n = S // tr

    def body(q_ref, kv_ref, gq_ref, gk_ref, do_ref, dq_ref, dkv_ref, dgq_ref, dgk_ref):
        i = pl.program_id(0)
        do_v = do_ref[...]
        dqs, dkn, dvs = [], [], []
        dgq = jnp.zeros((1, HEAD_DIM), F32)
        for h, (q_raw, qn, _, kn, v) in enumerate(_xattn_heads(q_ref[...], kv_ref[...], gq_ref[...], gk_ref[...], XW)):
            qb, kb = qn.astype(BF16), kn.astype(BF16)
            doh = do_v[:, h * HEAD_DIM:(h + 1) * HEAD_DIM]
            s = lax.dot_general(qb, kb, _DIMS['nt'], preferred_element_type=F32) / math.sqrt(HEAD_DIM)
            p = jnp.exp(s - jnp.max(s, axis=1, keepdims=True))
            p = p / jnp.sum(p, axis=1, keepdims=True)
            dp = lax.dot_general(doh, v, _DIMS['nt'], preferred_element_type=F32)
            ds = (p * (dp - jnp.sum(p * dp, axis=1, keepdims=True)) / math.sqrt(HEAD_DIM)).astype(BF16)
            dvs.append(lax.dot_general(p.astype(BF16), doh, _DIMS['tn'], preferred_element_type=F32))
            dkn.append(lax.dot_general(ds, qb, _DIMS['tn'], preferred_element_type=F32))
            dq, g1 = _rms_bwd(q_raw, gq_ref[...], jnp.dot(ds, kb, preferred_element_type=F32))
            dqs.append(dq)
            dgq = dgq + g1
        dq_ref[...] = jnp.concatenate(dqs, axis=1).astype(BF16)
        first = i == 0
        _acc_out(dgq_ref, first, dgq)
        _acc_out(dkv_ref, first, jnp.concatenate(dkn + dvs, axis=1))

        @pl.when(i == n - 1)
        def _():
            kv = kv_ref[...]
            acc = dkv_ref[...]
            dk, gk = _heads(lambda t, d: _rms_bwd(t, gk_ref[...], d), XW // HEAD_DIM, kv[:, :XW], acc[:, :XW])
            dkv_ref[:, :XW] = dk
            dgk_ref[...] = gk

    g = _fb((1, HEAD_DIM))
    return _rows_call('xattn_bwd', body, S, tr,
                      [(cq_raw, _rb(tr, XW)), (ckv, _fb((M, 2 * XW))), (g_cq, g), (g_ck, g), (do, _rb(tr, XW))],
                      [((S, XW), BF16, _rb(tr, XW)), ((M, 2 * XW), F32, _fb((M, 2 * XW))), ((1, HEAD_DIM), F32, g),
                       ((1, HEAD_DIM), F32, g)])


def swiglu_fwd(gu, F):
    S = gu.shape[0]
    tr = _tile(S, (256, 128))
    tf = _tile(F, (1408, 1024, 512, 256, 128))
    nf = F // tf

    def body(g_ref, u_ref, a_ref):
        g = g_ref[...]
        a_ref[...] = (g * _sigmoid(g) * u_ref[...]).astype(BF16)

    return pl.pallas_call(
        body, name='swiglu_fwd', grid=(S // tr, nf),
        in_specs=[pl.BlockSpec((tr, tf), lambda i, n: (i, n)), pl.BlockSpec((tr, tf), lambda i, n: (i, n + nf))],
        out_specs=pl.BlockSpec((tr, tf), lambda i, n: (i, n)), out_shape=jax.ShapeDtypeStruct((S, F), BF16),
        compiler_params=_params(('parallel', 'parallel')))(gu, gu)


def swiglu_bwd(gu, dact, F):
    S = gu.shape[0]
    tr = _tile(S, (256, 128))
    tf = _tile(F, (1408, 1024, 512, 256, 128))
    nf = F // tf

    def body(g_ref, u_ref, da_ref, o_ref):
        n = pl.program_id(1)
        g, da = g_ref[...], da_ref[...]
        sg = _sigmoid(g)

        @pl.when(n < nf)
        def _():
            o_ref[...] = (da * u_ref[...] * sg * (1.0 + g * (1.0 - sg))).astype(BF16)

        @pl.when(n >= nf)
        def _():
            o_ref[...] = (da * g * sg).astype(BF16)

    return pl.pallas_call(
        body, name='swiglu_bwd', grid=(S // tr, 2 * nf),
        in_specs=[pl.BlockSpec((tr, tf), lambda i, n: (i, n % nf)), pl.BlockSpec((tr, tf), lambda i, n: (i, n % nf + nf)),
                  pl.BlockSpec((tr, tf), lambda i, n: (i, n % nf))],
        out_specs=pl.BlockSpec((tr, tf), lambda i, n: (i, n)), out_shape=jax.ShapeDtypeStruct((S, 2 * F), BF16),
        compiler_params=_params(('parallel', 'arbitrary')))(gu, gu, dact)


def loss_head(y, target):
    S, D = y.shape
    tr = _tile(S, (256, 128))

    def body(y_ref, t_ref, d_ref, db_ref, l_ref):
        err = y_ref[...] - t_ref[...]
        d = err * (1.0 / D)
        d_ref[...] = d
        db_ref[...] = d.astype(BF16)
        part = jnp.sum(jnp.sum(err * err, axis=1, keepdims=True), axis=0, keepdims=True) * (0.5 / D)
        _acc_out(l_ref, pl.program_id(0) == 0, jnp.broadcast_to(part, (1, LANES)))

    return _rows_call('loss_head', body, S, tr, [(y, _rb(tr, D)), (target, _rb(tr, D))],
                      [((S, D), F32, _rb(tr, D)), ((S, D), BF16, _rb(tr, D)), ((1, LANES), F32, _fb((1, LANES)))])


def _adamw_math(w, gv, m, v):
    mn = ADAM_B1 * m + (1.0 - ADAM_B1) * gv
    vn = ADAM_B2 * v + (1.0 - ADAM_B2) * (gv * gv)
    m_hat = mn / (1.0 - ADAM_B1 ** ADAM_STEP)
    v_hat = vn / (1.0 - ADAM_B2 ** ADAM_STEP)
    return -ADAM_LR * (m_hat / (jnp.sqrt(v_hat) + ADAM_EPS) + ADAM_WD * w), mn, vn


def adamw(name, w, g, m, v):
    R, C = w.shape
    tr = _row_tile(R, C)

    def body(w_ref, g_ref, m_ref, v_ref, d_ref, mo_ref, vo_ref):
        d_ref[...], mo_ref[...], vo_ref[...] = _adamw_math(w_ref[...], g_ref[...], m_ref[...], v_ref[...])

    spec = _rb(tr, C)
    return _rows_call(name, body, R, tr, [(w, spec), (g, spec), (m, spec), (v, spec)], [((R, C), F32, spec)] * 3)


def adamw_halves(name, w, mine, other, m, v, c_idx):
    R, C = w.shape
    hr = R // 2
    tr = _row_tile(hr, C)

    def body(c_ref, w_ref, a_ref, b_ref, m_ref, v_ref, g_ref, d_ref, mo_ref, vo_ref):
        gv = jnp.where(pl.program_id(0) == c_ref[0], a_ref[...], b_ref[...])
        g_ref[...] = gv
        d_ref[...], mo_ref[...], vo_ref[...] = _adamw_math(w_ref[...], gv, m_ref[...], v_ref[...])

    full = pl.BlockSpec((None, tr, C), lambda hh, i, c_ref: (hh, i, 0))
    half = pl.BlockSpec((tr, C), lambda hh, i, c_ref: (i, 0))
    outs = pl.pallas_call(
        body, name=name,
        grid_spec=pltpu.PrefetchScalarGridSpec(num_scalar_prefetch=1, grid=(2, hr // tr),
                                               in_specs=[full, half, half, full, full], out_specs=[full] * 4),
        out_shape=[jax.ShapeDtypeStruct((2, hr, C), F32)] * 4,
        compiler_params=_params(('parallel', 'parallel')))(
            c_idx, w.reshape(2, hr, C), mine, other, m.reshape(2, hr, C), v.reshape(2, hr, C))
    return [o.reshape(R, C) for o in outs]


def _place():
    x, y, c = lax.axis_index('x'), lax.axis_index('y'), lax.axis_index('c')
    return x, y, c, [(1 - x, y), (x, 1 - y), (1 - x, 1 - y)]


def _rcopy(src, dst, ssem, rsem, dev):
    return pltpu.make_async_remote_copy(src_ref=src, dst_ref=dst, send_sem=ssem, recv_sem=rsem, device_id=dev,
                                        device_id_type=MESH)


HBM = pl.BlockSpec(memory_space=pltpu.HBM)
SEM = pl.BlockSpec(memory_space=pltpu.SEMAPHORE)
EFFECT = pltpu.SideEffectType.DATAFLOW_SIDE_EFFECTING


def _in_hbm(a):
    return pltpu.with_memory_space_constraint(a, pltpu.HBM)


def _rows_part(shape, whole, half):
    return pl.ds(0, shape[0]) if whole else pl.ds(half * (shape[0] // 2), shape[0] // 2)


def gather_start(shards, whole):
    nT = len(shards)

    def body(*refs):
        srcs, lands = refs[:nT], refs[nT:2 * nT]
        ssem, rsem, token = refs[2 * nT], refs[2 * nT + 1], refs[-1]
        x, y, c, chips = _place()
        for t in range(nT):
            rows = _rows_part(shards[t].shape, whole[t], c)
            for k, (px, py) in enumerate(chips):
                _rcopy(srcs[t].at[rows], lands[t].at[2 * x + y, rows], ssem.at[3 * t + k], rsem.at[3 * t + k],
                       (px, py, c)).start()
        token[...] = jnp.zeros_like(token)

    zones = [lax.empty((N_CHIPS,) + s.shape, s.dtype) for s in shards]
    outs = pl.pallas_call(
        body, name='gather_start',
        out_shape=(pltpu.SemaphoreType.DMA((3 * nT,)), pltpu.SemaphoreType.DMA((3 * nT,)),
                   *[pltpu.HBM(s.shape, s.dtype) for s in shards], *[pltpu.HBM(z.shape, z.dtype) for z in zones],
                   jax.ShapeDtypeStruct((8, LANES), F32)),
        in_specs=[HBM] * (2 * nT), out_specs=(SEM, SEM, *[HBM] * (2 * nT), pl.BlockSpec(memory_space=pltpu.VMEM)),
        input_output_aliases={i: 2 + i for i in range(2 * nT)},
        compiler_params=pltpu.CompilerParams(has_side_effects=EFFECT))(*[_in_hbm(a) for a in list(shards) + zones])
    return outs[0], outs[1], outs[2:2 + nT], outs[2 + nT:2 + 2 * nT], outs[-1]


def gather_wait(name, t, shard, zone, ssem, rsem, after, whole):
    def body(src_ref, land_ref, ssem_ref, rsem_ref, after_ref, src_out, land_out):
        x, y, c, chips = _place()
        rows = _rows_part(shard.shape, whole, c)
        for k, (px, py) in enumerate(chips):
            cp = _rcopy(src_ref.at[rows], land_ref.at[2 * px + py, rows], ssem_ref.at[3 * t + k], rsem_ref.at[3 * t + k],
                        (px, py, c))
            cp.wait_send()
            cp.wait_recv()

    return pl.pallas_call(
        body, name=name, out_shape=(pltpu.HBM(shard.shape, shard.dtype), pltpu.HBM(zone.shape, zone.dtype)),
        in_specs=(HBM, HBM, SEM, SEM, ANY), out_specs=(HBM, HBM), input_output_aliases={0: 0, 1: 1},
        compiler_params=pltpu.CompilerParams(has_side_effects=EFFECT))(shard, zone, ssem, rsem, after)


def pair_swap(name, zone):
    hr = zone.shape[1] // 2

    def body(z_in, z_ref, ssem, rsem):
        x, y, c, chips = _place()
        cps = []
        for k, (px, py) in enumerate(chips):
            blk = z_ref.at[2 * px + py, pl.ds(c * hr, hr)]
            cps.append(_rcopy(blk, blk, ssem.at[k], rsem.at[k], (x, y, 1 - c)))
            cps[-1].start()
        for k, (px, py) in enumerate(chips):
            blk = z_ref.at[2 * px + py, pl.ds((1 - c) * hr, hr)]
            _rcopy(blk, blk, ssem.at[k], rsem.at[k], (x, y, 1 - c)).wait_recv()
        for cp in cps:
            cp.wait_send()

    return pl.pallas_call(
        body, name=name, in_specs=[ANY], out_specs=ANY, out_shape=jax.ShapeDtypeStruct(zone.shape, zone.dtype),
        input_output_aliases={0: 0},
        scratch_shapes=[pltpu.SemaphoreType.DMA((3,)), pltpu.SemaphoreType.DMA((3,))],
        compiler_params=_params())(zone)


N_SENDERS = 7


def _scatter_copies(g_ref, l_ref, ssem, rsem):
    x, y, c, chips = _place()
    cps = []
    for k, (px, py) in enumerate(chips):
        for d in range(2):
            to = (c + d) % 2
            cps.append(_rcopy(g_ref.at[2 * px + py, to], l_ref.at[2 * k + d], ssem.at[2 * k + d], rsem.at[2 * k + d],
                              (px, py, to)))
    cps.append(_rcopy(g_ref.at[2 * x + y, 1 - c], l_ref.at[6], ssem.at[6], rsem.at[6], (x, y, 1 - c)))
    return cps


def scatter_start(name, g):
    def body(g_ref, l_ref, ssem, rsem, g_out, l_out, token):
        for cp in _scatter_copies(g_ref, l_ref, ssem, rsem):
            cp.start()
        token[...] = jnp.zeros_like(token)

    zone = lax.empty((N_SENDERS,) + g.shape[2:], g.dtype)
    return pl.pallas_call(
        body, name=name,
        out_shape=(pltpu.SemaphoreType.DMA((N_SENDERS,)), pltpu.SemaphoreType.DMA((N_SENDERS,)),
                   pltpu.HBM(g.shape, g.dtype), pltpu.HBM(zone.shape, zone.dtype), jax.ShapeDtypeStruct((8, LANES), F32)),
        in_specs=[HBM, HBM], out_specs=(SEM, SEM, HBM, HBM, pl.BlockSpec(memory_space=pltpu.VMEM)),
        input_output_aliases={0: 2, 1: 3},
        compiler_params=pltpu.CompilerParams(has_side_effects=EFFECT))(_in_hbm(g), _in_hbm(zone))


def scatter_wait(name, g, zone, ssem, rsem, after):
    def body(g_ref, l_ref, ssem_ref, rsem_ref, after_ref, g_out, l_out):
        for cp in _scatter_copies(g_ref, l_ref, ssem_ref, rsem_ref):
            cp.wait_send()
            cp.wait_recv()

    return pl.pallas_call(
        body, name=name, out_shape=(pltpu.HBM(g.shape, g.dtype), pltpu.HBM(zone.shape, zone.dtype)),
        in_specs=(HBM, HBM, SEM, SEM, ANY), out_specs=(HBM, HBM), input_output_aliases={0: 0, 1: 1},
        compiler_params=pltpu.CompilerParams(has_side_effects=EFFECT))(g, zone, ssem, rsem, after)


def sum_parts(name, g, landed, chip_idx, c_idx):
    hr, C = g.shape[2:]
    tr = _row_tile(hr, C, min_rows=16)

    def body(me_ref, c_ref, g_ref, l_ref, o_ref):
        acc = g_ref[...].astype(F32)
        for s in range(N_SENDERS):
            acc = acc + l_ref[s].astype(F32)
        o_ref[...] = acc

    return pl.pallas_call(
        body, name=name,
        grid_spec=pltpu.PrefetchScalarGridSpec(
            num_scalar_prefetch=2, grid=(hr // tr,),
            in_specs=[pl.BlockSpec((None, None, tr, C), lambda i, me_ref, c_ref: (me_ref[0], c_ref[0], i, 0)),
                      pl.BlockSpec((N_SENDERS, tr, C), lambda i, me_ref, c_ref: (0, i, 0))],
            out_specs=pl.BlockSpec((tr, C), lambda i, me_ref, c_ref: (i, 0))),
        out_shape=jax.ShapeDtypeStruct((hr, C), F32),
        compiler_params=_params(('parallel',)))(chip_idx, c_idx, g, landed)


def pair_join(name, halves):
    nT = len(halves)

    def body(*refs):
        ins, outs = refs[:nT], refs[nT:2 * nT]
        ssem, rsem = refs[2 * nT:]
        x, y, c, _ = _place()
        cps = [_rcopy(ins[t], outs[t], ssem.at[t], rsem.at[t], (x, y, 1 - c)) for t in range(nT)]
        for cp in cps:
            cp.start()
        for cp in cps:
            cp.wait()

    return pl.pallas_call(
        body, name=name, in_specs=[ANY] * nT, out_specs=[ANY] * nT,
        out_shape=[jax.ShapeDtypeStruct(h.shape, h.dtype) for h in halves],
        scratch_shapes=[pltpu.SemaphoreType.DMA((nT,)), pltpu.SemaphoreType.DMA((nT,))],
        compiler_params=_params())(*halves)


def allreduce_small(buf):
    R = buf.shape[0]
    VM = pl.BlockSpec(memory_space=pltpu.VMEM)

    def body(x_ref, o_ref, all_ref, ssem, rsem, lsem):
        x, y, c, chips = _place()
        me, sibling = (x, y, c), (x, y, 1 - c)

        def rows(px, py, pc):
            return all_ref.at[pl.ds((4 * px + 2 * py + pc) * R, R), :]

        def copy(k, block, to, src=None):
            return _rcopy(rows(*block) if src is None else src, rows(*block), ssem.at[k], rsem.at[k], to)

        mine = pltpu.make_async_copy(x_ref, rows(*me), lsem)
        mine.start()
        first = [copy(0, me, sibling, src=x_ref)]
        first += [copy(1 + k, me, (*chip, c), src=x_ref) for k, chip in enumerate(chips)]
        for cp in first:
            cp.start()
        passed = [copy(4 + k, (*chip, c), sibling) for k, chip in enumerate(chips)]
        for k, chip in enumerate(chips):
            copy(1 + k, (*chip, c), me).wait_recv()
            passed[k].start()
        copy(0, sibling, me).wait_recv()
        for k, chip in enumerate(chips):
            copy(4 + k, (*chip, 1 - c), me).wait_recv()
        for cp in first + passed:
            cp.wait_send()
        mine.wait()
        acc = all_ref[0:R, :]
        for d in range(1, 8):
            acc = acc + all_ref[d * R:(d + 1) * R, :]
        o_ref[...] = acc

    return pl.pallas_call(
        body, name='allreduce_small', in_specs=[VM], out_specs=VM, out_shape=jax.ShapeDtypeStruct((R, LANES), F32),
        scratch_shapes=[pltpu.VMEM((8 * R, LANES), F32), pltpu.SemaphoreType.DMA((7,)), pltpu.SemaphoreType.DMA((7,)),
                        pltpu.SemaphoreType.DMA],
        compiler_params=_params())(buf)


_PACK = 8 * LANES


def _pack(arrs):
    flat = []
    for a in arrs:
        v = a.reshape(-1).astype(F32)
        flat.append(jnp.pad(v, (0, (-v.shape[0]) % _PACK)))
    return jnp.concatenate(flat).reshape(-1, LANES)


def _unpack(buf, shapes):
    out, off = [], 0
    flat = buf.reshape(-1)
    for sh in shapes:
        n = math.prod(sh)
        out.append(flat[off:off + n].reshape(sh))
        off += n + (-n) % _PACK
    return out


def kernel(x, mem, g_mix, w_in, b_f, g_q, g_k, conv_w, conv_b, w_ra, b_ra, w_ri, b_ri, lam, g_fox_out, g_lru_out, w_out, g_xattn, g_mem, w_cq, w_ckv, g_cq, g_ck, w_co, g_ffn, w_gate_up, w_down, loss_target, m_g_mix, m_w_in, m_b_f, m_g_q, m_g_k, m_conv_w, m_conv_b, m_w_ra, m_b_ra, m_w_ri, m_b_ri, m_lam, m_g_fox_out, m_g_lru_out, m_w_out, m_g_xattn, m_g_mem, m_w_cq, m_w_ckv, m_g_cq, m_g_ck, m_w_co, m_g_ffn, m_w_gate_up, m_w_down, v_g_mix, v_w_in, v_b_f, v_g_q, v_g_k, v_conv_w, v_conv_b, v_w_ra, v_b_ra, v_w_ri, v_b_ri, v_lam, v_g_fox_out, v_g_lru_out, v_w_out, v_g_xattn, v_g_mem, v_w_cq, v_w_ckv, v_g_cq, v_g_ck, v_w_co, v_g_ffn, v_w_gate_up, v_w_down):
    given = dict(locals())
    W = {n: given[n][0] for n in WEIGHTS}
    M1 = {n: given['m_' + n][0] for n in WEIGHTS}
    V1 = {n: given['v_' + n][0] for n in WEIGHTS}
    xs, ms, tgt = x[0], mem[0], loss_target[0]
    S, D = xs.shape
    H = W['b_f'].shape[0]
    FW = H * HEAD_DIM
    LW = W['lam'].shape[0]
    nb = W['w_ra'].shape[0]
    XW = W['w_cq'].shape[1]
    F = W['w_down'].shape[0] * N_CHIPS
    IN_W = W['w_in'].shape[1] * N_CHIPS
    assert FW == LW and LW == nb * LANES and IN_W == 3 * FW + H + 2 * LW and H <= 8
    T = _tile(S, (512, 256, 128))
    c_idx = lax.axis_index('c').astype(jnp.int32).reshape(1)
    chip = 2 * lax.axis_index('x') + lax.axis_index('y')
    chip_idx = chip.astype(jnp.int32).reshape(1)
    vec = lambda n: W[n].reshape(1, -1)

    order = ['conv_w'] + BIG
    own = {n: W[n].astype(BF16) for n in BIG}
    own['conv_w'] = W['conv_w'].reshape(-1, LANES)
    g_ssem, g_rsem, g_src, g_zone, g_tok = gather_start([own[n] for n in order], [n == 'conv_w' for n in order])

    def fetch(n, after):
        t = order.index(n)
        src, zone = gather_wait('gather_wait_' + n, t, g_src[t], g_zone[t], g_ssem, g_rsem, after, n == 'conv_w')
        if n != 'conv_w':
            zone = pair_swap('pair_swap_' + n, zone)
        return lax.dynamic_update_index_in_dim(zone, src, chip, 0)

    b_f_pad = jnp.pad(vec('b_f'), ((0, 0), (0, LANES - H)))
    u_off, g_off = 3 * FW // LANES, (3 * FW + LW) // LANES

    h1 = norm_fwd('norm_mix', xs, vec('g_mix') + g_tok[0:1, 0:1])
    conv_full = fetch('conv_w', h1).reshape(N_CHIPS, CONV_W, LW // N_CHIPS).transpose(1, 0, 2).reshape(CONV_W, LW)
    w_in_full = fetch('w_in', h1).transpose(1, 0, 2).reshape(D, IN_W)
    w5 = jnp.concatenate([w_in_full[:, :3 * FW], w_in_full[:, 3 * FW + H:]], axis=1)
    wf = jnp.pad(w_in_full[:, 3 * FW:3 * FW + H], ((0, 0), (0, LANES - H)))
    proj = _mm('proj_in', h1, w5, 'nn', F32)
    f_raw = _mm('proj_f', h1, wf, 'nn', F32)
    qn, kn, vb = qkv_fwd(proj, vec('g_q'), vec('g_k'), FW)
    cc = fgate_fwd(f_raw, b_f_pad)
    ct = cc[:, :8].T
    o_fox, lse = fox_fwd(qn, kn, vb, cc, ct, T)
    lru_w = (conv_full, vec('conv_b'), W['w_ra'], vec('b_ra'), W['w_ri'], vec('b_ri'), vec('lam'))
    y_lru = lru_fwd(proj, *lru_w, u_off, g_off)
    mixn = mix_fwd(o_fox, y_lru, vec('g_fox_out'), vec('g_lru_out'))
    w_out_f = fetch('w_out', mixn).reshape(2 * FW, D)
    x1 = _mm('proj_out', mixn, w_out_f, 'nn', F32, res=xs)

    hq = norm_fwd('norm_xq', x1, vec('g_xattn'))
    mn = norm_fwd('norm_mem', ms, vec('g_mem'))
    w_cq_f = fetch('w_cq', hq).reshape(D, XW)
    w_ckv_f = fetch('w_ckv', mn).reshape(D, 2 * XW)
    cq_raw = _mm('proj_cq', hq, w_cq_f, 'nn', F32)
    ckv = _mm('proj_ckv', mn, w_ckv_f, 'nn', F32)
    o_x = xattn_fwd(cq_raw, ckv, vec('g_cq'), vec('g_ck'))
    w_co_g = fetch('w_co', o_x)
    x2 = _mm_colsharded('proj_co', o_x, w_co_g, F32, res=x1)

    hf = norm_fwd('norm_ffn', x2, vec('g_ffn'))
    w_gu_g = fetch('w_gate_up', hf)
    gu = _mm_colsharded('proj_gate_up', hf, w_gu_g, F32)
    act = swiglu_fwd(gu, F)
    w_down_f = fetch('w_down', act).reshape(F, D)
    yv = _mm('proj_down', act, w_down_f, 'nn', F32, res=x2)
    dy, dyb, loss_blk = loss_head(yv, tgt)
    loss = lax.psum(loss_blk[0, 0], ('x', 'y', 'c'))

    gw, pending = {}, []

    def reduce_begin(n, g):
        sp = g.reshape(N_CHIPS, 2, g.shape[1] // 2, g.shape[2])
        ssem, rsem, sp, zone, tok = scatter_start('scatter_start_' + n, sp)
        pending.append((n, sp, zone, ssem, rsem))
        return tok[0:1, 0:1]

    dact = _mm('bwd_down_x', dyb, w_down_f, 'nt', F32)
    t_down = reduce_begin('w_down', _mm('bwd_down_w', act, dyb, 'tn', BF16).reshape(N_CHIPS, F // N_CHIPS, D))
    dgu = swiglu_bwd(gu, dact, F)
    dhf = _mm_colsharded_t('bwd_gate_up_x', dgu, w_gu_g, F32)
    t_gu = reduce_begin('w_gate_up', _mm_grad_colsharded('bwd_gate_up_w', hf, dgu, N_CHIPS, BF16))
    dx2, dx2b, gw['g_ffn'] = norm_bwd('norm_ffn_bwd', x2, vec('g_ffn') + t_down + t_gu, dhf, res=dy)

    do_x = _mm_colsharded_t('bwd_co_x', dx2b, w_co_g, BF16)
    t_co = reduce_begin('w_co', _mm_grad_colsharded('bwd_co_w', o_x, dx2b, N_CHIPS, BF16))
    dcq_raw, dckv, gw['g_cq'], gw['g_ck'] = xattn_bwd(cq_raw, ckv, vec('g_cq') + t_co, vec('g_ck'), do_x)
    dhq = _mm('bwd_cq_x', dcq_raw, w_cq_f, 'nt', F32)
    t_cq = reduce_begin('w_cq', _mm('bwd_cq_w', hq, dcq_raw, 'tn', BF16).reshape(N_CHIPS, D // N_CHIPS, XW))
    dmn = _mm('bwd_ckv_x', dckv, w_ckv_f, 'nt', F32)
    t_ckv = reduce_begin('w_ckv', _mm('bwd_ckv_w', mn, dckv, 'tn', BF16).reshape(N_CHIPS, D // N_CHIPS, 2 * XW))
    (gw['g_mem'],) = norm_bwd('norm_mem_bwd', ms, vec('g_mem'), dmn, want_dx=False)
    dx1, dx1b, gw['g_xattn'] = norm_bwd('norm_xq_bwd', x1, vec('g_xattn') + t_cq + t_ckv, dhq, res=dx2)

    dmix = _mm('bwd_out_x', dx1b, w_out_f, 'nt', F32)
    t_out = reduce_begin('w_out', _mm('bwd_out_w', mixn, dx1b, 'tn', BF16).reshape(N_CHIPS, 2 * FW // N_CHIPS, D))
    do_fox, delta, dy_lru, gw['g_fox_out'], gw['g_lru_out'] = mix_bwd(o_fox, y_lru, vec('g_fox_out') + t_out,
                                                                     vec('g_lru_out'), dmix)
    (du, dgate, gw['conv_w'], gw['conv_b'], gw['w_ra'], gw['b_ra'], gw['w_ri'], gw['b_ri'],
     gw['lam']) = lru_bwd(proj, dy_lru, *lru_w, u_off, g_off)
    dqn, delta2 = fox_bwd_q(qn, kn, vb, do_fox, cc, ct, lse, delta, T)
    dkn, dv, dct = fox_bwd_kv(qn, kn, vb, do_fox, cc, ct, lse, delta2, T)
    dq, dk, gw['g_q'], gw['g_k'] = qkv_bwd(proj, vec('g_q'), vec('g_k'), dqn, dkn, FW)
    dc = jnp.pad(dct.reshape(H, S).T, ((0, 0), (0, LANES - H)))
    df, db_f = fgate_bwd(f_raw, b_f_pad, dc, H)
    gw['b_f'] = db_f[:, :H]
    dproj = jnp.concatenate([dq, dk, dv, du, dgate], axis=1)
    dw5 = _mm('bwd_in_w', h1, dproj, 'tn', BF16)
    dwf = _mm('bwd_f_w', h1, df, 'tn', BF16)
    dw_in = jnp.concatenate([dw5[:, :3 * FW], dwf[:, :H], dw5[:, 3 * FW:]], axis=1)
    t_in = reduce_begin('w_in', dw_in.reshape(D, N_CHIPS, IN_W // N_CHIPS).transpose(1, 0, 2))
    dh_a = _mm('bwd_f_x', df, wf, 'nt', F32)
    dh1 = _mm('bwd_in_x', dproj, w5, 'nt', F32, res=dh_a)
    grad_x, _, gw['g_mix'] = norm_bwd('norm_mix_bwd', xs, vec('g_mix') + t_in, dh1, res=dx1)

    grads, delta_w, new_m, new_v = {}, {}, {}, {}
    small_shapes = [gw[n].shape for n in SMALL]
    summed = _unpack(allreduce_small(_pack([gw[n] for n in SMALL])), small_shapes)
    for n, g in zip(SMALL, summed):
        grads[n] = g.reshape(W[n].shape) if n != 'conv_w' else lax.dynamic_slice_in_dim(
            g, chip * (LW // N_CHIPS), LW // N_CHIPS, axis=1)

    for n, part, zone, ssem, rsem in pending:
        part, landed = scatter_wait('scatter_wait_' + n, part, zone, ssem, rsem, grad_x)
        mine = sum_parts('sum_parts_' + n, part, landed, chip_idx, c_idx)
        (other,) = pair_join('pair_join_' + n, [mine])
        grads[n], delta_w[n], new_m[n], new_v[n] = adamw_halves('adamw_' + n, W[n], mine, other, M1[n], V1[n], c_idx)
    packs = [_pack([d[n] for n in SMALL]) for d in (W, grads, M1, V1)]
    shapes = [W[n].shape for n in SMALL]
    for d, res in zip((delta_w, new_m, new_v), adamw('adamw_small', *packs)):
        d.update(zip(SMALL, _unpack(res, shapes)))

    lead = lambda d: [d[n][None] for n in WEIGHTS]
    return (loss, grad_x[None], *lead(grads), *lead(delta_w), *lead(new_m), *lead(new_v))
```

```python
import functools
import math

import jax
import jax.numpy as jnp
from jax import lax
from jax.experimental import pallas as pl
from jax.experimental.pallas import tpu as pltpu

F32 = jnp.float32
BF16 = jnp.bfloat16
HEAD_DIM = 128
LANES = 128
LRU_C = 8.0
RMS_EPS = 1e-6
CONV_W = 4
ADAM_LR = 0.001
ADAM_B1 = 0.9
ADAM_B2 = 0.999
ADAM_EPS = 1e-08
ADAM_WD = 0.01
ADAM_STEP = 10
VMEM_LIMIT = 56 * 1024 * 1024
N_CHIPS = 4
MESH = pl.DeviceIdType.MESH
ANY = pl.BlockSpec(memory_space=pl.ANY)

WEIGHTS = ['g_mix', 'w_in', 'b_f', 'g_q', 'g_k', 'conv_w', 'conv_b', 'w_ra', 'b_ra', 'w_ri', 'b_ri', 'lam',
           'g_fox_out', 'g_lru_out', 'w_out', 'g_xattn', 'g_mem', 'w_cq', 'w_ckv', 'g_cq', 'g_ck', 'w_co', 'g_ffn',
           'w_gate_up', 'w_down']
BIG = ['w_in', 'w_out', 'w_cq', 'w_ckv', 'w_co', 'w_gate_up', 'w_down']
SMALL = [n for n in WEIGHTS if n not in BIG]


def _params(sem=None):
    if sem is None:
        return pltpu.CompilerParams(vmem_limit_bytes=VMEM_LIMIT)
    return pltpu.CompilerParams(dimension_semantics=sem, vmem_limit_bytes=VMEM_LIMIT)


def _tile(n, cands):
    for t in cands:
        if n % t == 0:
            return t
    return n


ROW_BLOCK_BYTES = 1 << 20


def _row_tile(n_rows, n_cols, min_rows=8):
    cands = [t for t in (512, 256, 128, 64, 32, 16, 8) if t >= min_rows and t * n_cols * 4 <= ROW_BLOCK_BYTES]
    return _tile(n_rows, cands or [min_rows])


def _sigmoid(z):
    return 1.0 / (1.0 + jnp.exp(-z))


def _softplus(z):
    return jnp.maximum(z, 0.0) + jnp.log(1.0 + jnp.exp(-jnp.abs(z)))


def _neg_expm1(z):
    series = -z * (1.0 + z * (0.5 + z * (1.0 / 6.0 + z * (1.0 / 24.0 + z * (1.0 / 120.0)))))
    return jnp.where(z > -0.25, series, 1.0 - jnp.exp(z))


_GELU_K = math.sqrt(2.0 / math.pi)


def _gelu_and_grad(z):
    inner = _GELU_K * (z + 0.044715 * z * z * z)
    t = jnp.tanh(inner)
    g = 0.5 * z * (1.0 + t)
    dg = 0.5 * (1.0 + t) + 0.5 * z * (1.0 - t * t) * _GELU_K * (1.0 + 3.0 * 0.044715 * z * z)
    return g, dg


def _rms(xv, g):
    r = lax.rsqrt(jnp.mean(xv * xv, axis=-1, keepdims=True) + RMS_EPS)
    return xv * r * g


def _rms_bwd(xv, g, dy):
    r = lax.rsqrt(jnp.mean(xv * xv, axis=-1, keepdims=True) + RMS_EPS)
    xh = xv * r
    dyg = dy * g
    dx = r * (dyg - xh * jnp.mean(dyg * xh, axis=-1, keepdims=True))
    return dx, jnp.sum(dy * xh, axis=0, keepdims=True)


def _heads(fn, n_heads, *arrs):
    outs = [fn(*[a[:, h * HEAD_DIM:(h + 1) * HEAD_DIM] for a in arrs]) for h in range(n_heads)]
    first = jnp.concatenate([o[0] for o in outs], axis=1) if n_heads > 1 else outs[0][0]
    rest = [functools.reduce(lambda p, q: p + q, [o[i] for o in outs]) for i in range(1, len(outs[0]))]
    return (first, *rest)


def _split3(v):
    hi = v.astype(BF16)
    r1 = v - hi.astype(F32)
    mid = r1.astype(BF16)
    lo = (r1 - mid.astype(F32)).astype(BF16)
    return hi, mid, lo


def _acc_out(ref, first, val):
    @pl.when(first)
    def _():
        ref[...] = val

    @pl.when(jnp.logical_not(first))
    def _():
        ref[...] += val


_DIMS = {'nn': (((1,), (0,)), ((), ())), 'nt': (((1,), (1,)), ((), ())), 'tn': (((0,), (0,)), ((), ()))}


MM_VMEM_BYTES = 36 * 1024 * 1024


def _k_tile(K, tm, tn, a, b, o_dtype, res):
    fixed = tm * tn * (2 * jnp.dtype(o_dtype).itemsize + 4 + (8 if res is not None else 0))
    per_k = 2 * (tm * a.dtype.itemsize + tn * b.dtype.itemsize)
    per_k += 2 * tm * (a.dtype.itemsize > 2) + 2 * tn * (b.dtype.itemsize > 2)
    units = K // LANES
    for d in sorted((d for d in range(1, units + 1) if units % d == 0), reverse=True):
        if fixed + d * LANES * per_k <= MM_VMEM_BYTES:
            return d * LANES
    return LANES


def _mm_call(name, a, b, mode, grid, a_spec, b_spec, o_spec, o_shape, o_dtype, acc_shape, res=None):
    nk = grid[2]
    dn = _DIMS[mode]

    def body(*refs):
        a_ref, b_ref = refs[:2]
        r_ref = refs[2] if res is not None else None
        o_ref = refs[3] if res is not None else refs[2]
        part = lax.dot_general(a_ref[...].astype(BF16), b_ref[...].astype(BF16), dn, preferred_element_type=F32)

        def finish(r):
            if r_ref is not None:
                r = r + r_ref[...]
            o_ref[...] = r.astype(o_dtype)

        if nk == 1:
            finish(part)
            return
        acc = refs[-1]
        k = pl.program_id(2)

        @pl.when(k == 0)
        def _():
            acc[...] = part

        @pl.when(k > 0)
        def _():
            acc[...] += part

        @pl.when(k == nk - 1)
        def _():
            finish(acc[...])

    ins = [a, b] + ([] if res is None else [res])
    specs = [a_spec, b_spec] + ([] if res is None else [o_spec])
    return pl.pallas_call(
        body, name=name, grid=grid, in_specs=specs, out_specs=o_spec,
        out_shape=jax.ShapeDtypeStruct(o_shape, o_dtype),
        scratch_shapes=[] if nk == 1 else [pltpu.VMEM(acc_shape, F32)],
        compiler_params=_params(('parallel', 'parallel', 'arbitrary')))(*ins)


def _mm(name, a, b, mode, o_dtype, res=None):
    if mode == 'tn':
        K, M = a.shape
    else:
        M, K = a.shape
    N = b.shape[0] if mode == 'nt' else b.shape[1]
    tm = _tile(M, (1024, 512, 256, 128))
    tn = _tile(N, (1024, 512, 256, 128))
    tk = _k_tile(K, tm, tn, a, b, o_dtype, res)
    a_spec = (pl.BlockSpec((tk, tm), lambda m, n, k: (k, m)) if mode == 'tn'
              else pl.BlockSpec((tm, tk), lambda m, n, k: (m, k)))
    b_spec = (pl.BlockSpec((tn, tk), lambda m, n, k: (n, k)) if mode == 'nt'
              else pl.BlockSpec((tk, tn), lambda m, n, k: (k, n)))
    o_spec = pl.BlockSpec((tm, tn), lambda m, n, k: (m, n))
    return _mm_call(name, a, b, mode, (M // tm, N // tn, K // tk), a_spec, b_spec, o_spec, (M, N), o_dtype,
                    (tm, tn), res)


def _mm_colsharded(name, a, w, o_dtype, res=None):
    M, K = a.shape
    J, _, Nj = w.shape
    tm = _tile(M, (1024, 512, 256, 128))
    tn = _tile(Nj, (1408, 1024, 512, 256, 128))
    tk = _k_tile(K, tm, tn, a, w, o_dtype, res)
    per = Nj // tn
    return _mm_call(name, a, w, 'nn', (M // tm, J * per, K // tk),
                    pl.BlockSpec((tm, tk), lambda m, n, k: (m, k)),
                    pl.BlockSpec((None, tk, tn), lambda m, n, k: (n // per, k, n % per)),
                    pl.BlockSpec((tm, tn), lambda m, n, k: (m, n)), (M, J * Nj), o_dtype, (tm, tn), res)


def _mm_colsharded_t(name, a, w, o_dtype):
    M = a.shape[0]
    J, K, Nj = w.shape
    tm = _tile(M, (1024, 512, 256, 128))
    tn = _tile(K, (1024, 512, 256, 128))
    tk = _k_tile(Nj, tm, tn, a, w, o_dtype, None)
    per = Nj // tk
    return _mm_call(name, a, w, 'nt', (M // tm, K // tn, J * per),
                    pl.BlockSpec((tm, tk), lambda m, n, k: (m, k)),
                    pl.BlockSpec((None, tn, tk), lambda m, n, k: (k // per, n, k % per)),
                    pl.BlockSpec((tm, tn), lambda m, n, k: (m, n)), (M, K), o_dtype, (tm, tn))


def _mm_grad_colsharded(name, a, dy, J, o_dtype):
    S, M = a.shape
    Nj = dy.shape[1] // J
    tm = _tile(M, (1024, 512, 256, 128))
    tn = _tile(Nj, (1408, 1024, 512, 256, 128))
    tk = _k_tile(S, tm, tn, a, dy, o_dtype, None)
    per = Nj // tn
    return _mm_call(name, a, dy, 'tn', (M // tm, J * per, S // tk),
                    pl.BlockSpec((tk, tm), lambda m, n, k: (k, m)),
                    pl.BlockSpec((tk, tn), lambda m, n, k: (k, n)),
                    pl.BlockSpec((None, tm, tn), lambda m, n, k: (n // per, m, n % per)), (J, M, Nj), o_dtype, (tm, tn))


def _rows_call(name, body, n_rows, tr, ins, outs):
    return pl.pallas_call(
        body, name=name, grid=(n_rows // tr,), in_specs=[s for _, s in ins], out_specs=[s for _, _, s in outs],
        out_shape=[jax.ShapeDtypeStruct(sh, dt) for sh, dt, _ in outs],
        compiler_params=_params(('arbitrary',)))(*[a for a, _ in ins])


def _rb(tr, w, cb=0):
    return pl.BlockSpec((tr, w), lambda i: (i, cb))


def _fb(shape):
    nd = len(shape)
    return pl.BlockSpec(shape, lambda i: (0,) * nd)


def norm_fwd(name, xv, g):
    S, D = xv.shape
    tr = _tile(S, (256, 128))

    def body(x_ref, g_ref, o_ref):
        o_ref[...] = _rms(x_ref[...], g_ref[...]).astype(BF16)

    return _rows_call(name, body, S, tr, [(xv, _rb(tr, D)), (g, _fb((1, D)))], [((S, D), BF16, _rb(tr, D))])[0]


def norm_bwd(name, xv, g, dy, res=None, want_dx=True):
    S, D = xv.shape
    tr = _tile(S, (256, 128))

    def body(*refs):
        if res is None:
            x_ref, g_ref, dy_ref = refs[:3]
            outs = refs[3:]
            r_ref = None
        else:
            x_ref, g_ref, dy_ref, r_ref = refs[:4]
            outs = refs[4:]
        dx, dg = _rms_bwd(x_ref[...], g_ref[...], dy_ref[...])
        if r_ref is not None:
            dx = dx + r_ref[...]
        if want_dx:
            outs[0][...] = dx
            outs[1][...] = dx.astype(BF16)
        _acc_out(outs[-1], pl.program_id(0) == 0, dg)

    ins = [(xv, _rb(tr, D)), (g, _fb((1, D))), (dy, _rb(tr, D))] + ([] if res is None else [(res, _rb(tr, D))])
    outs = ([((S, D), F32, _rb(tr, D)), ((S, D), BF16, _rb(tr, D))] if want_dx else []) + [((1, D), F32, _fb((1, D)))]
    return _rows_call(name, body, S, tr, ins, outs)


def qkv_fwd(proj, g_q, g_k, FW):
    S = proj.shape[0]
    H = FW // HEAD_DIM
    tr = _tile(S, (256, 128))

    def body(q_ref, k_ref, v_ref, gq_ref, gk_ref, qo, ko, vo):
        qo[...] = _heads(lambda t: (_rms(t, gq_ref[...]),), H, q_ref[...])[0].astype(BF16)
        ko[...] = _heads(lambda t: (_rms(t, gk_ref[...]),), H, k_ref[...])[0].astype(BF16)
        vo[...] = v_ref[...].astype(BF16)

    o = ((S, FW), BF16, _rb(tr, FW))
    return _rows_call('qkv_fwd', body, S, tr,
                      [(proj, _rb(tr, FW, 0)), (proj, _rb(tr, FW, 1)), (proj, _rb(tr, FW, 2)),
                       (g_q, _fb((1, HEAD_DIM))), (g_k, _fb((1, HEAD_DIM)))], [o, o, o])


def qkv_bwd(proj, g_q, g_k, dqn, dkn, FW):
    S = proj.shape[0]
    H = FW // HEAD_DIM
    tr = _tile(S, (256, 128))

    def body(q_ref, k_ref, gq_ref, gk_ref, dq_ref, dk_ref, dqo, dko, dgq, dgk):
        dq, gq = _heads(lambda t, d: _rms_bwd(t, gq_ref[...], d), H, q_ref[...], dq_ref[...])
        dk, gk = _heads(lambda t, d: _rms_bwd(t, gk_ref[...], d), H, k_ref[...], dk_ref[...])
        dqo[...] = dq.astype(BF16)
        dko[...] = dk.astype(BF16)
        first = pl.program_id(0) == 0
        _acc_out(dgq, first, gq)
        _acc_out(dgk, first, gk)

    o = ((S, FW), BF16, _rb(tr, FW))
    og = ((1, HEAD_DIM), F32, _fb((1, HEAD_DIM)))
    return _rows_call('qkv_bwd', body, S, tr,
                      [(proj, _rb(tr, FW, 0)), (proj, _rb(tr, FW, 1)), (g_q, _fb((1, HEAD_DIM))),
                       (g_k, _fb((1, HEAD_DIM))), (dqn, _rb(tr, FW)), (dkn, _rb(tr, FW))], [o, o, og, og])


def _tri(n, upper):
    r = lax.broadcasted_iota(jnp.int32, (n, n), 0)
    c = lax.broadcasted_iota(jnp.int32, (n, n), 1)
    return jnp.where((c >= r) if upper else (c <= r), 1.0, 0.0).astype(BF16)


def _blocked_cumsum(val, S, blk, reverse):
    tri = _tri(blk, reverse)
    order = range(S // blk - 1, -1, -1) if reverse else range(S // blk)
    carry = jnp.zeros((1, LANES), F32)
    outs = {}
    for bi in order:
        part = val[bi * blk:(bi + 1) * blk]
        acc = carry
        for piece in _split3(part):
            acc = acc + jnp.dot(tri, piece, preferred_element_type=F32)
        outs[bi] = acc
        carry = carry + jnp.sum(part, axis=0, keepdims=True)
    return jnp.concatenate([outs[bi] for bi in range(S // blk)], axis=0)


def fgate_fwd(f_raw, b_f_pad):
    S = f_raw.shape[0]
    blk = _tile(S, (256, 128))

    def body(f_ref, b_ref, c_ref):
        z = f_ref[...] + b_ref[...]
        c_ref[...] = _blocked_cumsum(-_softplus(-z), S, blk, False)

    return pl.pallas_call(body, name='fgate_fwd', grid=(1,), in_specs=[_fb((S, LANES)), _fb((1, LANES))],
                          out_specs=_fb((S, LANES)), out_shape=jax.ShapeDtypeStruct((S, LANES), F32),
                          compiler_params=_params(('arbitrary',)))(f_raw, b_f_pad)


def fgate_bwd(f_raw, b_f_pad, dc, H):
    S = f_raw.shape[0]
    blk = _tile(S, (256, 128))

    def body(f_ref, b_ref, dc_ref, df_ref, db_ref):
        z = f_ref[...] + b_ref[...]
        dlogf = _blocked_cumsum(dc_ref[...], S, blk, True)
        lane = lax.broadcasted_iota(jnp.int32, (S, LANES), 1)
        df = jnp.where(lane < H, dlogf * _sigmoid(-z), 0.0)
        df_ref[...] = df.astype(BF16)
        db_ref[...] = jnp.sum(df, axis=0, keepdims=True)

    return pl.pallas_call(body, name='fgate_bwd', grid=(1,),
                          in_specs=[_fb((S, LANES)), _fb((1, LANES)), _fb((S, LANES))],
                          out_specs=[_fb((S, LANES)), _fb((1, LANES))],
                          out_shape=[jax.ShapeDtypeStruct((S, LANES), BF16), jax.ShapeDtypeStruct((1, LANES), F32)],
                          compiler_params=_params(('arbitrary',)))(f_raw, b_f_pad, dc)


def _fox_logits(q, k, c_blk, ct_blk, h, i, j, T):
    s = lax.dot_general(q, k, _DIMS['nt'], preferred_element_type=F32) * (1.0 / math.sqrt(HEAD_DIM))
    lane = lax.broadcasted_iota(jnp.int32, c_blk.shape, 1)
    cq = jnp.sum(jnp.where(lane == h, c_blk, 0.0), axis=1, keepdims=True)
    sub = lax.broadcasted_iota(jnp.int32, ct_blk.shape, 0)
    ck = jnp.sum(jnp.where(sub == h, ct_blk, 0.0), axis=0, keepdims=True)
    rows = i * T + lax.broadcasted_iota(jnp.int32, (T, T), 0)
    cols = j * T + lax.broadcasted_iota(jnp.int32, (T, T), 1)
    return jnp.where(cols <= rows, s + cq - ck, -jnp.inf)


def fox_fwd(qn, kn, vb, c, ct, T):
    S, FW = qn.shape
    H = FW // HEAD_DIM
    Hp = ct.shape[0]
    n = S // T

    def body(q_ref, k_ref, v_ref, c_ref, ct_ref, o_ref, lse_ref, m_s, l_s, acc_s):
        h, i, j = pl.program_id(0), pl.program_id(1), pl.program_id(2)

        @pl.when(j == 0)
        def _():
            m_s[...] = jnp.full_like(m_s, -jnp.inf)
            l_s[...] = jnp.zeros_like(l_s)
            acc_s[...] = jnp.zeros_like(acc_s)

        @pl.when(j <= i)
        def _():
            s = _fox_logits(q_ref[...], k_ref[...], c_ref[...], ct_ref[...], h, i, j, T)
            m_new = jnp.maximum(m_s[...], jnp.max(s, axis=1, keepdims=True))
            alpha = jnp.exp(m_s[...] - m_new)
            p = jnp.exp(s - m_new)
            l_s[...] = alpha * l_s[...] + jnp.sum(p, axis=1, keepdims=True)
            acc_s[...] = alpha * acc_s[...] + jnp.dot(p.astype(BF16), v_ref[...], preferred_element_type=F32)
            m_s[...] = m_new

        @pl.when(j == i)
        def _():
            o_ref[...] = acc_s[...] / l_s[...]
            lse_ref[...] = jnp.broadcast_to(m_s[...] + jnp.log(l_s[...]), (T, LANES))

    qs = pl.BlockSpec((T, HEAD_DIM), lambda h, i, j: (i, h))
    ks = pl.BlockSpec((T, HEAD_DIM), lambda h, i, j: (jnp.minimum(j, i), h))
    return pl.pallas_call(
        body, name='fox_fwd', grid=(H, n, n),
        in_specs=[qs, ks, ks, pl.BlockSpec((T, LANES), lambda h, i, j: (i, 0)),
                  pl.BlockSpec((Hp, T), lambda h, i, j: (0, jnp.minimum(j, i)))],
        out_specs=[qs, pl.BlockSpec((None, T, LANES), lambda h, i, j: (h, i, 0))],
        out_shape=[jax.ShapeDtypeStruct((S, FW), F32), jax.ShapeDtypeStruct((H, S, LANES), F32)],
        scratch_shapes=[pltpu.VMEM((T, 1), F32), pltpu.VMEM((T, 1), F32), pltpu.VMEM((T, HEAD_DIM), F32)],
        compiler_params=_params(('parallel', 'parallel', 'arbitrary')))(qn, kn, vb, c, ct)


def _fox_p_ds(q_ref, k_ref, v_ref, do_ref, c_ref, ct_ref, lse_ref, dl_ref, h, i, j, T):
    s = _fox_logits(q_ref[...], k_ref[...], c_ref[...], ct_ref[...], h, i, j, T)
    p = jnp.exp(s - jnp.tile(lse_ref[...], (1, T // LANES)))
    dp = lax.dot_general(do_ref[...], v_ref[...], _DIMS['nt'], preferred_element_type=F32)
    ds = p * (dp - jnp.tile(dl_ref[...], (1, T // LANES)))
    return p, dp, ds


def fox_bwd_q(qn, kn, vb, do, c, ct, lse, dl, T):
    S, FW = qn.shape
    H = FW // HEAD_DIM
    Hp = ct.shape[0]
    n = S // T

    def body(q_ref, k_ref, v_ref, do_ref, c_ref, ct_ref, lse_ref, dl_ref, dq_ref, dl2_ref, acc_s, rs_s):
        h, i, j = pl.program_id(0), pl.program_id(1), pl.program_id(2)

        @pl.when(j == 0)
        def _():
            acc_s[...] = jnp.zeros_like(acc_s)
            rs_s[...] = jnp.zeros_like(rs_s)

        @pl.when(j <= i)
        def _():
            p, dp, ds = _fox_p_ds(q_ref, k_ref, v_ref, do_ref, c_ref, ct_ref, lse_ref, dl_ref, h, i, j, T)
            acc_s[...] += jnp.dot(ds.astype(BF16), k_ref[...], preferred_element_type=F32)
            rs_s[...] += jnp.sum(p * dp, axis=1, keepdims=True)

        @pl.when(j == i)
        def _():
            dq_ref[...] = acc_s[...] * (1.0 / math.sqrt(HEAD_DIM))
            dl2_ref[...] = jnp.broadcast_to(rs_s[...], (T, LANES))

    qs = pl.BlockSpec((T, HEAD_DIM), lambda h, i, j: (i, h))
    ks = pl.BlockSpec((T, HEAD_DIM), lambda h, i, j: (jnp.minimum(j, i), h))
    st = pl.BlockSpec((None, T, LANES), lambda h, i, j: (h, i, 0))
    return pl.pallas_call(
        body, name='fox_bwd_q', grid=(H, n, n),
        in_specs=[qs, ks, ks, qs, pl.BlockSpec((T, LANES), lambda h, i, j: (i, 0)),
                  pl.BlockSpec((Hp, T), lambda h, i, j: (0, jnp.minimum(j, i))), st, st],
        out_specs=[qs, st], out_shape=[jax.ShapeDtypeStruct((S, FW), F32), jax.ShapeDtypeStruct((H, S, LANES), F32)],
        scratch_shapes=[pltpu.VMEM((T, HEAD_DIM), F32), pltpu.VMEM((T, 1), F32)],
        compiler_params=_params(('parallel', 'parallel', 'arbitrary')))(qn, kn, vb, do, c, ct, lse, dl)


def fox_bwd_kv(qn, kn, vb, do, c, ct, lse, dl, T):
    S, FW = qn.shape
    H = FW // HEAD_DIM
    Hp = ct.shape[0]
    n = S // T

    def body(q_ref, k_ref, v_ref, do_ref, c_ref, ct_ref, lse_ref, dl_ref, dk_ref, dv_ref, dc_ref, dk_s, dv_s, dc_s):
        h, j, i = pl.program_id(0), pl.program_id(1), pl.program_id(2)

        @pl.when(i == 0)
        def _():
            dk_s[...] = jnp.zeros_like(dk_s)
            dv_s[...] = jnp.zeros_like(dv_s)
            dc_s[...] = jnp.zeros_like(dc_s)

        @pl.when(i >= j)
        def _():
            p, _, ds = _fox_p_ds(q_ref, k_ref, v_ref, do_ref, c_ref, ct_ref, lse_ref, dl_ref, h, i, j, T)
            dv_s[...] += lax.dot_general(p.astype(BF16), do_ref[...], _DIMS['tn'], preferred_element_type=F32)
            dk_s[...] += lax.dot_general(ds.astype(BF16), q_ref[...], _DIMS['tn'], preferred_element_type=F32)
            dc_s[...] += jnp.sum(ds, axis=0, keepdims=True)

        @pl.when(i == n - 1)
        def _():
            dk_ref[...] = dk_s[...] * (1.0 / math.sqrt(HEAD_DIM))
            dv_ref[...] = dv_s[...].astype(BF16)
            dc_ref[...] = -dc_s[...]

    qs = pl.BlockSpec((T, HEAD_DIM), lambda h, j, i: (jnp.maximum(i, j), h))
    ks = pl.BlockSpec((T, HEAD_DIM), lambda h, j, i: (j, h))
    st = pl.BlockSpec((None, T, LANES), lambda h, j, i: (h, jnp.maximum(i, j), 0))
    return pl.pallas_call(
        body, name='fox_bwd_kv', grid=(H, n, n),
        in_specs=[qs, ks, ks, qs, pl.BlockSpec((T, LANES), lambda h, j, i: (jnp.maximum(i, j), 0)),
                  pl.BlockSpec((Hp, T), lambda h, j, i: (0, j)), st, st],
        out_specs=[ks, ks, pl.BlockSpec((None, 1, T), lambda h, j, i: (h, 0, j))],
        out_shape=[jax.ShapeDtypeStruct((S, FW), F32), jax.ShapeDtypeStruct((S, FW), BF16),
                   jax.ShapeDtypeStruct((H, 1, S), F32)],
        scratch_shapes=[pltpu.VMEM((T, HEAD_DIM), F32), pltpu.VMEM((T, HEAD_DIM), F32), pltpu.VMEM((1, T), F32)],
        compiler_params=_params(('parallel', 'parallel', 'arbitrary')))(qn, kn, vb, do, c, ct, lse, dl)


def _shift_down(v, d, rows, fill):
    return jnp.where(rows >= d, pltpu.roll(v, d, 0), fill)


def _shift_up(v, d, rows, S, fill):
    return jnp.where(rows < S - d, pltpu.roll(v, S - d, 0), fill)


def _scan(a, b, rows, S, reverse):
    d = 1
    while d < S:
        if reverse:
            a_s, b_s = _shift_up(a, d, rows, S, 1.0), _shift_up(b, d, rows, S, 0.0)
        else:
            a_s, b_s = _shift_down(a, d, rows, 1.0), _shift_down(b, d, rows, 0.0)
        b = a * b_s + b
        a = a * a_s
        d *= 2
    return b


def _lru_forward(u, cw, cb, wra, bra, wri, bri, lam, rows):
    uc = cb + cw[CONV_W - 1] * u
    for d in range(1, CONV_W):
        uc = uc + cw[CONV_W - 1 - d] * _shift_down(u, d, rows, 0.0)
    ucb = uc.astype(BF16)
    r = _sigmoid(jnp.dot(ucb, wra.astype(BF16), preferred_element_type=F32) + bra)
    ig = _sigmoid(jnp.dot(ucb, wri.astype(BF16), preferred_element_type=F32) + bri)
    sp = _softplus(-lam)
    log_a = -LRU_C * r * sp
    a = jnp.exp(log_a)
    sq = jnp.sqrt(_neg_expm1(2.0 * log_a))
    iu = ig * uc
    hseq = _scan(a, sq * iu, rows, u.shape[0], False)
    return uc, ucb, r, ig, sp, a, sq, iu, hseq


def _lru_specs(S, n_u, n_g):
    col = lambda off: pl.BlockSpec((S, LANES), lambda cbk: (0, off + cbk))
    vec = pl.BlockSpec((1, LANES), lambda cbk: (0, cbk))
    mat = pl.BlockSpec((None, LANES, LANES), lambda cbk: (cbk, 0, 0))
    cw = pl.BlockSpec((CONV_W, LANES), lambda cbk: (0, cbk))
    return col, vec, mat, cw


def lru_fwd(proj, conv_w, conv_b, w_ra, b_ra, w_ri, b_ri, lam, u_off, g_off):
    S = proj.shape[0]
    nb = w_ra.shape[0]
    col, vec, mat, cws = _lru_specs(S, u_off, g_off)

    def body(u_ref, g_ref, cw_ref, cb_ref, wra_ref, bra_ref, wri_ref, bri_ref, lam_ref, y_ref):
        rows = lax.broadcasted_iota(jnp.int32, (S, LANES), 0)
        cw = [cw_ref[t:t + 1, :] for t in range(CONV_W)]
        hseq = _lru_forward(u_ref[...], cw, cb_ref[...], wra_ref[...], bra_ref[...], wri_ref[...],
                            bri_ref[...], lam_ref[...], rows)[-1]
        y_ref[...] = hseq * _gelu_and_grad(g_ref[...])[0]

    return pl.pallas_call(
        body, name='lru_fwd', grid=(nb,),
        in_specs=[col(u_off), col(g_off), cws, vec, mat, vec, mat, vec, vec], out_specs=col(0),
        out_shape=jax.ShapeDtypeStruct((S, nb * LANES), F32),
        compiler_params=_params(('parallel',)))(proj, proj, conv_w, conv_b, w_ra, b_ra, w_ri, b_ri, lam)


def lru_bwd(proj, dy, conv_w, conv_b, w_ra, b_ra, w_ri, b_ri, lam, u_off, g_off):
    S = proj.shape[0]
    nb = w_ra.shape[0]
    LW = nb * LANES
    col, vec, mat, cws = _lru_specs(S, u_off, g_off)

    def body(u_ref, g_ref, dy_ref, cw_ref, cb_ref, wra_ref, bra_ref, wri_ref, bri_ref, lam_ref,
             du_ref, dg_ref, dcw_ref, dcb_ref, dwra_ref, dbra_ref, dwri_ref, dbri_ref, dlam_ref):
        rows = lax.broadcasted_iota(jnp.int32, (S, LANES), 0)
        u, lam_v = u_ref[...], lam_ref[...]
        cw = [cw_ref[t:t + 1, :] for t in range(CONV_W)]
        wra, wri = wra_ref[...].astype(BF16), wri_ref[...].astype(BF16)
        uc, ucb, r, ig, sp, a, sq, iu, hseq = _lru_forward(u, cw, cb_ref[...], wra, bra_ref[...], wri, bri_ref[...],
                                                           lam_v, rows)
        gl, dgl = _gelu_and_grad(g_ref[...])
        dy_v = dy_ref[...]
        dg_ref[...] = (dy_v * hseq * dgl).astype(BF16)
        G = _scan(_shift_up(a, 1, rows, S, 0.0), dy_v * gl, rows, S, True)
        da = G * _shift_down(hseq, 1, rows, 0.0)
        diu = G * sq
        dsq = G * iu
        dlog_a = da * a - dsq * a * a / jnp.maximum(sq, 1e-30)
        dr = dlog_a * (-LRU_C * sp)
        dsp = jnp.sum(dlog_a * (-LRU_C * r), axis=0, keepdims=True)
        dlam_ref[...] = -dsp * _sigmoid(-lam_v)
        dzr = dr * r * (1.0 - r)
        dzi = diu * uc * ig * (1.0 - ig)
        dzrb, dzib = dzr.astype(BF16), dzi.astype(BF16)
        duc = (diu * ig + lax.dot_general(dzrb, wra, _DIMS['nt'], preferred_element_type=F32)
               + lax.dot_general(dzib, wri, _DIMS['nt'], preferred_element_type=F32))
        dwra_ref[...] = lax.dot_general(ucb, dzrb, _DIMS['tn'], preferred_element_type=F32)
        dwri_ref[...] = lax.dot_general(ucb, dzib, _DIMS['tn'], preferred_element_type=F32)
        dbra_ref[...] = jnp.sum(dzr, axis=0, keepdims=True)
        dbri_ref[...] = jnp.sum(dzi, axis=0, keepdims=True)
        dcb_ref[...] = jnp.sum(duc, axis=0, keepdims=True)
        du = cw[CONV_W - 1] * duc
        dcw_ref[CONV_W - 1:CONV_W, :] = jnp.sum(duc * u, axis=0, keepdims=True)
        for d in range(1, CONV_W):
            du = du + cw[CONV_W - 1 - d] * _shift_up(duc, d, rows, S, 0.0)
            dcw_ref[CONV_W - 1 - d:CONV_W - d, :] = jnp.sum(duc * _shift_down(u, d, rows, 0.0), axis=0, keepdims=True)
        du_ref[...] = du.astype(BF16)

    sd = jax.ShapeDtypeStruct
    return pl.pallas_call(
        body, name='lru_bwd', grid=(nb,),
        in_specs=[col(u_off), col(g_off), col(0), cws, vec, mat, vec, mat, vec, vec],
        out_specs=[col(0), col(0), cws, vec, mat, vec, mat, vec, vec],
        out_shape=[sd((S, LW), BF16), sd((S, LW), BF16), sd((CONV_W, LW), F32), sd((1, LW), F32),
                   sd((nb, LANES, LANES), F32), sd((1, LW), F32), sd((nb, LANES, LANES), F32), sd((1, LW), F32),
                   sd((1, LW), F32)],
        compiler_params=_params(('parallel',)))(proj, proj, dy, conv_w, conv_b, w_ra, b_ra, w_ri, b_ri, lam)


def mix_fwd(o_fox, y_lru, g_fox, g_lru):
    S, FW = o_fox.shape
    tr = _tile(S, (256, 128))

    def body(o_ref, y_ref, gf_ref, gl_ref, m_ref):
        m_ref[...] = jnp.concatenate([_rms(o_ref[...], gf_ref[...]), _rms(y_ref[...], gl_ref[...])],
                                     axis=1).astype(BF16)

    return _rows_call('mix_fwd', body, S, tr,
                      [(o_fox, _rb(tr, FW)), (y_lru, _rb(tr, FW)), (g_fox, _fb((1, FW))), (g_lru, _fb((1, FW)))],
                      [((S, 2 * FW), BF16, _rb(tr, 2 * FW))])[0]


def mix_bwd(o_fox, y_lru, g_fox, g_lru, dmix):
    S, FW = o_fox.shape
    H = FW // HEAD_DIM
    tr = _tile(S, (256, 128))

    def body(o_ref, y_ref, gf_ref, gl_ref, df_ref, dl_ref, do_ref, dlt_ref, dy_ref, dgf_ref, dgl_ref):
        o = o_ref[...]
        do, dgf = _rms_bwd(o, gf_ref[...], df_ref[...])
        dyl, dgl = _rms_bwd(y_ref[...], gl_ref[...], dl_ref[...])
        do_ref[...] = do.astype(BF16)
        dy_ref[...] = dyl
        prod = do * o
        for h in range(H):
            dlt_ref[h] = jnp.broadcast_to(
                jnp.sum(prod[:, h * HEAD_DIM:(h + 1) * HEAD_DIM], axis=1, keepdims=True), (tr, LANES))
        first = pl.program_id(0) == 0
        _acc_out(dgf_ref, first, dgf)
        _acc_out(dgl_ref, first, dgl)

    g = _fb((1, FW))
    return _rows_call('mix_bwd', body, S, tr,
                      [(o_fox, _rb(tr, FW)), (y_lru, _rb(tr, FW)), (g_fox, g), (g_lru, g), (dmix, _rb(tr, FW, 0)),
                       (dmix, _rb(tr, FW, 1))],
                      [((S, FW), BF16, _rb(tr, FW)), ((H, S, LANES), F32, pl.BlockSpec((H, tr, LANES), lambda i: (0, i, 0))),
                       ((S, FW), F32, _rb(tr, FW)), ((1, FW), F32, g), ((1, FW), F32, g)])


def _xattn_heads(cq_raw, ckv, g_cq, g_ck, XW):
    out = []
    for h in range(XW // HEAD_DIM):
        sl = slice(h * HEAD_DIM, (h + 1) * HEAD_DIM)
        out.append((cq_raw[:, sl], _rms(cq_raw[:, sl], g_cq), ckv[:, sl], _rms(ckv[:, sl], g_ck),
                    ckv[:, XW + h * HEAD_DIM:XW + (h + 1) * HEAD_DIM].astype(BF16)))
    return out


def xattn_fwd(cq_raw, ckv, g_cq, g_ck):
    S, XW = cq_raw.shape
    M = ckv.shape[0]
    tr = _tile(S, (512, 256, 128))

    def body(q_ref, kv_ref, gq_ref, gk_ref, o_ref):
        outs = []
        for _, qn, _, kn, v in _xattn_heads(q_ref[...], kv_ref[...], gq_ref[...], gk_ref[...], XW):
            s = lax.dot_general(qn.astype(BF16), kn.astype(BF16), _DIMS['nt'], preferred_element_type=F32)
            s = s / math.sqrt(HEAD_DIM)
            p = jnp.exp(s - jnp.max(s, axis=1, keepdims=True))
            p = p / jnp.sum(p, axis=1, keepdims=True)
            outs.append(jnp.dot(p.astype(BF16), v, preferred_element_type=F32))
        o_ref[...] = jnp.concatenate(outs, axis=1).astype(BF16)

    g = _fb((1, HEAD_DIM))
    return _rows_call('xattn_fwd', body, S, tr,
                      [(cq_raw, _rb(tr, XW)), (ckv, _fb((M, 2 * XW))), (g_cq, g), (g_ck, g)],
                      [((S, XW), BF16, _rb(tr, XW))])[0]


def xattn_bwd(cq_raw, ckv, g_cq, g_ck, do):
    S, XW = cq_raw.shape
    M = ckv.shape[0]
    tr = _tile(S, (512, 256, 128))
    n = S // tr

    def body(q_ref, kv_ref, gq_ref, gk_ref, do_ref, dq_ref, dkv_ref, dgq_ref, dgk_ref):
        i = pl.program_id(0)
        do_v = do_ref[...]
        dqs, dkn, dvs = [], [], []
        dgq = jnp.zeros((1, HEAD_DIM), F32)
        for h, (q_raw, qn, _, kn, v) in enumerate(_xattn_heads(q_ref[...], kv_ref[...], gq_ref[...], gk_ref[...], XW)):
            qb, kb = qn.astype(BF16), kn.astype(BF16)
            doh = do_v[:, h * HEAD_DIM:(h + 1) * HEAD_DIM]
            s = lax.dot_general(qb, kb, _DIMS['nt'], preferred_element_type=F32) / math.sqrt(HEAD_DIM)
            p = jnp.exp(s - jnp.max(s, axis=1, keepdims=True))
            p = p / jnp.sum(p, axis=1, keepdims=True)
            dp = lax.dot_general(doh, v, _DIMS['nt'], preferred_element_type=F32)
            ds = (p * (dp - jnp.sum(p * dp, axis=1, keepdims=True)) / math.sqrt(HEAD_DIM)).astype(BF16)
            dvs.append(lax.dot_general(p.astype(BF16), doh, _DIMS['tn'], preferred_element_type=F32))
            dkn.append(lax.dot_general(ds, qb, _DIMS['tn'], preferred_element_type=F32))
            dq, g1 = _rms_bwd(q_raw, gq_ref[...], jnp.dot(ds, kb, preferred_element_type=F32))
            dqs.append(dq)
            dgq = dgq + g1
        dq_ref[...] = jnp.concatenate(dqs, axis=1).astype(BF16)
        first = i == 0
        _acc_out(dgq_ref, first, dgq)
        _acc_out(dkv_ref, first, jnp.concatenate(dkn + dvs, axis=1))

        @pl.when(i == n - 1)
        def _():
            kv = kv_ref[...]
            acc = dkv_ref[...]
            dk, gk = _heads(lambda t, d: _rms_bwd(t, gk_ref[...], d), XW // HEAD_DIM, kv[:, :XW], acc[:, :XW])
            dkv_ref[:, :XW] = dk
            dgk_ref[...] = gk

    g = _fb((1, HEAD_DIM))
    return _rows_call('xattn_bwd', body, S, tr,
                      [(cq_raw, _rb(tr, XW)), (ckv, _fb((M, 2 * XW))), (g_cq, g), (g_ck, g), (do, _rb(tr, XW))],
                      [((S, XW), BF16, _rb(tr, XW)), ((M, 2 * XW), F32, _fb((M, 2 * XW))), ((1, HEAD_DIM), F32, g),
                       ((1, HEAD_DIM), F32, g)])


def swiglu_fwd(gu, F):
    S = gu.shape[0]
    tr = _tile(S, (256, 128))
    tf = _tile(F, (1408, 1024, 512, 256, 128))
    nf = F // tf

    def body(g_ref, u_ref, a_ref):
        g = g_ref[...]
        a_ref[...] = (g * _sigmoid(g) * u_ref[...]).astype(BF16)

    return pl.pallas_call(
        body, name='swiglu_fwd', grid=(S // tr, nf),
        in_specs=[pl.BlockSpec((tr, tf), lambda i, n: (i, n)), pl.BlockSpec((tr, tf), lambda i, n: (i, n + nf))],
        out_specs=pl.BlockSpec((tr, tf), lambda i, n: (i, n)), out_shape=jax.ShapeDtypeStruct((S, F), BF16),
        compiler_params=_params(('parallel', 'parallel')))(gu, gu)


def swiglu_bwd(gu, dact, F):
    S = gu.shape[0]
    tr = _tile(S, (256, 128))
    tf = _tile(F, (1408, 1024, 512, 256, 128))
    nf = F // tf

    def body(g_ref, u_ref, da_ref, o_ref):
        n = pl.program_id(1)
        g, da = g_ref[...], da_ref[...]
        sg = _sigmoid(g)

        @pl.when(n < nf)
        def _():
            o_ref[...] = (da * u_ref[...] * sg * (1.0 + g * (1.0 - sg))).astype(BF16)

        @pl.when(n >= nf)
        def _():
            o_ref[...] = (da * g * sg).astype(BF16)

    return pl.pallas_call(
        body, name='swiglu_bwd', grid=(S // tr, 2 * nf),
        in_specs=[pl.BlockSpec((tr, tf), lambda i, n: (i, n % nf)), pl.BlockSpec((tr, tf), lambda i, n: (i, n % nf + nf)),
                  pl.BlockSpec((tr, tf), lambda i, n: (i, n % nf))],
        out_specs=pl.BlockSpec((tr, tf), lambda i, n: (i, n)), out_shape=jax.ShapeDtypeStruct((S, 2 * F), BF16),
        compiler_params=_params(('parallel', 'arbitrary')))(gu, gu, dact)


def loss_head(y, target):
    S, D = y.shape
    tr = _tile(S, (256, 128))

    def body(y_ref, t_ref, d_ref, db_ref, l_ref):
        err = y_ref[...] - t_ref[...]
        d = err * (1.0 / D)
        d_ref[...] = d
        db_ref[...] = d.astype(BF16)
        part = jnp.sum(jnp.sum(err * err, axis=1, keepdims=True), axis=0, keepdims=True) * (0.5 / D)
        _acc_out(l_ref, pl.program_id(0) == 0, jnp.broadcast_to(part, (1, LANES)))

    return _rows_call('loss_head', body, S, tr, [(y, _rb(tr, D)), (target, _rb(tr, D))],
                      [((S, D), F32, _rb(tr, D)), ((S, D), BF16, _rb(tr, D)), ((1, LANES), F32, _fb((1, LANES)))])


def _adamw_math(w, gv, m, v):
    mn = ADAM_B1 * m + (1.0 - ADAM_B1) * gv
    vn = ADAM_B2 * v + (1.0 - ADAM_B2) * (gv * gv)
    m_hat = mn / (1.0 - ADAM_B1 ** ADAM_STEP)
    v_hat = vn / (1.0 - ADAM_B2 ** ADAM_STEP)
    return -ADAM_LR * (m_hat / (jnp.sqrt(v_hat) + ADAM_EPS) + ADAM_WD * w), mn, vn


def adamw(name, w, g, m, v):
    R, C = w.shape
    tr = _row_tile(R, C)

    def body(w_ref, g_ref, m_ref, v_ref, d_ref, mo_ref, vo_ref):
        d_ref[...], mo_ref[...], vo_ref[...] = _adamw_math(w_ref[...], g_ref[...], m_ref[...], v_ref[...])

    spec = _rb(tr, C)
    return _rows_call(name, body, R, tr, [(w, spec), (g, spec), (m, spec), (v, spec)], [((R, C), F32, spec)] * 3)


def adamw_halves(name, w, mine, other, m, v, c_idx):
    R, C = w.shape
    hr = R // 2
    tr = _row_tile(hr, C)

    def body(c_ref, w_ref, a_ref, b_ref, m_ref, v_ref, g_ref, d_ref, mo_ref, vo_ref):
        gv = jnp.where(pl.program_id(0) == c_ref[0], a_ref[...], b_ref[...])
        g_ref[...] = gv
        d_ref[...], mo_ref[...], vo_ref[...] = _adamw_math(w_ref[...], gv, m_ref[...], v_ref[...])

    full = pl.BlockSpec((None, tr, C), lambda hh, i, c_ref: (hh, i, 0))
    half = pl.BlockSpec((tr, C), lambda hh, i, c_ref: (i, 0))
    outs = pl.pallas_call(
        body, name=name,
        grid_spec=pltpu.PrefetchScalarGridSpec(num_scalar_prefetch=1, grid=(2, hr // tr),
                                               in_specs=[full, half, half, full, full], out_specs=[full] * 4),
        out_shape=[jax.ShapeDtypeStruct((2, hr, C), F32)] * 4,
        compiler_params=_params(('parallel', 'parallel')))(
            c_idx, w.reshape(2, hr, C), mine, other, m.reshape(2, hr, C), v.reshape(2, hr, C))
    return [o.reshape(R, C) for o in outs]


def _place():
    x, y, c = lax.axis_index('x'), lax.axis_index('y'), lax.axis_index('c')
    return x, y, c, [(1 - x, y), (x, 1 - y), (1 - x, 1 - y)]


def _rcopy(src, dst, ssem, rsem, dev):
    return pltpu.make_async_remote_copy(src_ref=src, dst_ref=dst, send_sem=ssem, recv_sem=rsem, device_id=dev,
                                        device_id_type=MESH)


HBM = pl.BlockSpec(memory_space=pltpu.HBM)
SEM = pl.BlockSpec(memory_space=pltpu.SEMAPHORE)
EFFECT = pltpu.SideEffectType.DATAFLOW_SIDE_EFFECTING


def _in_hbm(a):
    return pltpu.with_memory_space_constraint(a, pltpu.HBM)


def _rows_part(shape, whole, half):
    return pl.ds(0, shape[0]) if whole else pl.ds(half * (shape[0] // 2), shape[0] // 2)


def gather_start(shards, whole):
    nT = len(shards)

    def body(*refs):
        srcs, lands = refs[:nT], refs[nT:2 * nT]
        ssem, rsem, token = refs[2 * nT], refs[2 * nT + 1], refs[-1]
        x, y, c, chips = _place()
        for t in range(nT):
            rows = _rows_part(shards[t].shape, whole[t], c)
            for k, (px, py) in enumerate(chips):
                _rcopy(srcs[t].at[rows], lands[t].at[2 * x + y, rows], ssem.at[3 * t + k], rsem.at[3 * t + k],
                       (px, py, c)).start()
        token[...] = jnp.zeros_like(token)

    zones = [lax.empty((N_CHIPS,) + s.shape, s.dtype) for s in shards]
    outs = pl.pallas_call(
        body, name='gather_start',
        out_shape=(pltpu.SemaphoreType.DMA((3 * nT,)), pltpu.SemaphoreType.DMA((3 * nT,)),
                   *[pltpu.HBM(s.shape, s.dtype) for s in shards], *[pltpu.HBM(z.shape, z.dtype) for z in zones],
                   jax.ShapeDtypeStruct((8, LANES), F32)),
        in_specs=[HBM] * (2 * nT), out_specs=(SEM, SEM, *[HBM] * (2 * nT), pl.BlockSpec(memory_space=pltpu.VMEM)),
        input_output_aliases={i: 2 + i for i in range(2 * nT)},
        compiler_params=pltpu.CompilerParams(has_side_effects=EFFECT))(*[_in_hbm(a) for a in list(shards) + zones])
    return outs[0], outs[1], outs[2:2 + nT], outs[2 + nT:2 + 2 * nT], outs[-1]


def gather_wait(name, t, shard, zone, ssem, rsem, after, whole):
    def body(src_ref, land_ref, ssem_ref, rsem_ref, after_ref, src_out, land_out):
        x, y, c, chips = _place()
        rows = _rows_part(shard.shape, whole, c)
        for k, (px, py) in enumerate(chips):
            cp = _rcopy(src_ref.at[rows], land_ref.at[2 * px + py, rows], ssem_ref.at[3 * t + k], rsem_ref.at[3 * t + k],
                        (px, py, c))
            cp.wait_send()
            cp.wait_recv()

    return pl.pallas_call(
        body, name=name, out_shape=(pltpu.HBM(shard.shape, shard.dtype), pltpu.HBM(zone.shape, zone.dtype)),
        in_specs=(HBM, HBM, SEM, SEM, ANY), out_specs=(HBM, HBM), input_output_aliases={0: 0, 1: 1},
        compiler_params=pltpu.CompilerParams(has_side_effects=EFFECT))(shard, zone, ssem, rsem, after)


def pair_swap(name, zone):
    hr = zone.shape[1] // 2

    def body(z_in, z_ref, ssem, rsem):
        x, y, c, chips = _place()
        cps = []
        for k, (px, py) in enumerate(chips):
            blk = z_ref.at[2 * px + py, pl.ds(c * hr, hr)]
            cps.append(_rcopy(blk, blk, ssem.at[k], rsem.at[k], (x, y, 1 - c)))
            cps[-1].start()
        for k, (px, py) in enumerate(chips):
            blk = z_ref.at[2 * px + py, pl.ds((1 - c) * hr, hr)]
            _rcopy(blk, blk, ssem.at[k], rsem.at[k], (x, y, 1 - c)).wait_recv()
        for cp in cps:
            cp.wait_send()

    return pl.pallas_call(
        body, name=name, in_specs=[ANY], out_specs=ANY, out_shape=jax.ShapeDtypeStruct(zone.shape, zone.dtype),
        input_output_aliases={0: 0},
        scratch_shapes=[pltpu.SemaphoreType.DMA((3,)), pltpu.SemaphoreType.DMA((3,))],
        compiler_params=_params())(zone)


N_SENDERS = 7


def _scatter_copies(g_ref, l_ref, ssem, rsem):
    x, y, c, chips = _place()
    cps = []
    for k, (px, py) in enumerate(chips):
        for d in range(2):
            to = (c + d) % 2
            cps.append(_rcopy(g_ref.at[2 * px + py, to], l_ref.at[2 * k + d], ssem.at[2 * k + d], rsem.at[2 * k + d],
                              (px, py, to)))
    cps.append(_rcopy(g_ref.at[2 * x + y, 1 - c], l_ref.at[6], ssem.at[6], rsem.at[6], (x, y, 1 - c)))
    return cps


def scatter_start(name, g):
    def body(g_ref, l_ref, ssem, rsem, g_out, l_out, token):
        for cp in _scatter_copies(g_ref, l_ref, ssem, rsem):
            cp.start()
        token[...] = jnp.zeros_like(token)

    zone = lax.empty((N_SENDERS,) + g.shape[2:], g.dtype)
    return pl.pallas_call(
        body, name=name,
        out_shape=(pltpu.SemaphoreType.DMA((N_SENDERS,)), pltpu.SemaphoreType.DMA((N_SENDERS,)),
                   pltpu.HBM(g.shape, g.dtype), pltpu.HBM(zone.shape, zone.dtype), jax.ShapeDtypeStruct((8, LANES), F32)),
        in_specs=[HBM, HBM], out_specs=(SEM, SEM, HBM, HBM, pl.BlockSpec(memory_space=pltpu.VMEM)),
        input_output_aliases={0: 2, 1: 3},
        compiler_params=pltpu.CompilerParams(has_side_effects=EFFECT))(_in_hbm(g), _in_hbm(zone))


def scatter_wait(name, g, zone, ssem, rsem, after):
    def body(g_ref, l_ref, ssem_ref, rsem_ref, after_ref, g_out, l_out):
        for cp in _scatter_copies(g_ref, l_ref, ssem_ref, rsem_ref):
            cp.wait_send()
            cp.wait_recv()

    return pl.pallas_call(
        body, name=name, out_shape=(pltpu.HBM(g.shape, g.dtype), pltpu.HBM(zone.shape, zone.dtype)),
        in_specs=(HBM, HBM, SEM, SEM, ANY), out_specs=(HBM, HBM), input_output_aliases={0: 0, 1: 1},
        compiler_params=pltpu.CompilerParams(has_side_effects=EFFECT))(g, zone, ssem, rsem, after)


def sum_parts(name, g, landed, chip_idx, c_idx):
    hr, C = g.shape[2:]
    tr = _row_tile(hr, C, min_rows=16)

    def body(me_ref, c_ref, g_ref, l_ref, o_ref):
        acc = g_ref[...].astype(F32)
        for s in range(N_SENDERS):
            acc = acc + l_ref[s].astype(F32)
        o_ref[...] = acc

    return pl.pallas_call(
        body, name=name,
        grid_spec=pltpu.PrefetchScalarGridSpec(
            num_scalar_prefetch=2, grid=(hr // tr,),
            in_specs=[pl.BlockSpec((None, None, tr, C), lambda i, me_ref, c_ref: (me_ref[0], c_ref[0], i, 0)),
                      pl.BlockSpec((N_SENDERS, tr, C), lambda i, me_ref, c_ref: (0, i, 0))],
            out_specs=pl.BlockSpec((tr, C), lambda i, me_ref, c_ref: (i, 0))),
        out_shape=jax.ShapeDtypeStruct((hr, C), F32),
        compiler_params=_params(('parallel',)))(chip_idx, c_idx, g, landed)


def pair_join(name, halves):
    nT = len(halves)

    def body(*refs):
        ins, outs = refs[:nT], refs[nT:2 * nT]
        ssem, rsem = refs[2 * nT:]
        x, y, c, _ = _place()
        cps = [_rcopy(ins[t], outs[t], ssem.at[t], rsem.at[t], (x, y, 1 - c)) for t in range(nT)]
        for cp in cps:
            cp.start()
        for cp in cps:
            cp.wait()

    return pl.pallas_call(
        body, name=name, in_specs=[ANY] * nT, out_specs=[ANY] * nT,
        out_shape=[jax.ShapeDtypeStruct(h.shape, h.dtype) for h in halves],
        scratch_shapes=[pltpu.SemaphoreType.DMA((nT,)), pltpu.SemaphoreType.DMA((nT,))],
        compiler_params=_params())(*halves)


def allreduce_small(buf, after):
    R = buf.shape[0]
    VM = pl.BlockSpec(memory_space=pltpu.VMEM)

    def body(x_ref, after_ref, o_ref, all_ref, ssem, rsem, lsem):
        x, y, c, chips = _place()
        me, sibling = (x, y, c), (x, y, 1 - c)

        def rows(px, py, pc):
            return all_ref.at[pl.ds((4 * px + 2 * py + pc) * R, R), :]

        def copy(k, block, to, src=None):
            return _rcopy(rows(*block) if src is None else src, rows(*block), ssem.at[k], rsem.at[k], to)

        mine = pltpu.make_async_copy(x_ref, rows(*me), lsem)
        mine.start()
        first = [copy(0, me, sibling, src=x_ref)]
        first += [copy(1 + k, me, (*chip, c), src=x_ref) for k, chip in enumerate(chips)]
        for cp in first:
            cp.start()
        passed = [copy(4 + k, (*chip, c), sibling) for k, chip in enumerate(chips)]
        for k, chip in enumerate(chips):
            copy(1 + k, (*chip, c), me).wait_recv()
            passed[k].start()
        copy(0, sibling, me).wait_recv()
        for k, chip in enumerate(chips):
            copy(4 + k, (*chip, 1 - c), me).wait_recv()
        for cp in first + passed:
            cp.wait_send()
        mine.wait()
        acc = all_ref[0:R, :]
        for d in range(1, 8):
            acc = acc + all_ref[d * R:(d + 1) * R, :]
        o_ref[...] = acc

    return pl.pallas_call(
        body, name='allreduce_small', in_specs=[VM, ANY], out_specs=VM, out_shape=jax.ShapeDtypeStruct((R, LANES), F32),
        scratch_shapes=[pltpu.VMEM((8 * R, LANES), F32), pltpu.SemaphoreType.DMA((7,)), pltpu.SemaphoreType.DMA((7,)),
                        pltpu.SemaphoreType.DMA],
        compiler_params=_params())(buf, after)


_PACK = 8 * LANES


def _pack(arrs):
    flat = []
    for a in arrs:
        v = a.reshape(-1).astype(F32)
        flat.append(jnp.pad(v, (0, (-v.shape[0]) % _PACK)))
    return jnp.concatenate(flat).reshape(-1, LANES)


def _unpack(buf, shapes):
    out, off = [], 0
    flat = buf.reshape(-1)
    for sh in shapes:
        n = math.prod(sh)
        out.append(flat[off:off + n].reshape(sh))
        off += n + (-n) % _PACK
    return out


def kernel(x, mem, g_mix, w_in, b_f, g_q, g_k, conv_w, conv_b, w_ra, b_ra, w_ri, b_ri, lam, g_fox_out, g_lru_out, w_out, g_xattn, g_mem, w_cq, w_ckv, g_cq, g_ck, w_co, g_ffn, w_gate_up, w_down, loss_target, m_g_mix, m_w_in, m_b_f, m_g_q, m_g_k, m_conv_w, m_conv_b, m_w_ra, m_b_ra, m_w_ri, m_b_ri, m_lam, m_g_fox_out, m_g_lru_out, m_w_out, m_g_xattn, m_g_mem, m_w_cq, m_w_ckv, m_g_cq, m_g_ck, m_w_co, m_g_ffn, m_w_gate_up, m_w_down, v_g_mix, v_w_in, v_b_f, v_g_q, v_g_k, v_conv_w, v_conv_b, v_w_ra, v_b_ra, v_w_ri, v_b_ri, v_lam, v_g_fox_out, v_g_lru_out, v_w_out, v_g_xattn, v_g_mem, v_w_cq, v_w_ckv, v_g_cq, v_g_ck, v_w_co, v_g_ffn, v_w_gate_up, v_w_down):
    given = dict(locals())
    W = {n: given[n][0] for n in WEIGHTS}
    M1 = {n: given['m_' + n][0] for n in WEIGHTS}
    V1 = {n: given['v_' + n][0] for n in WEIGHTS}
    xs, ms, tgt = x[0], mem[0], loss_target[0]
    S, D = xs.shape
    H = W['b_f'].shape[0]
    FW = H * HEAD_DIM
    LW = W['lam'].shape[0]
    nb = W['w_ra'].shape[0]
    XW = W['w_cq'].shape[1]
    F = W['w_down'].shape[0] * N_CHIPS
    IN_W = W['w_in'].shape[1] * N_CHIPS
    assert FW == LW and LW == nb * LANES and IN_W == 3 * FW + H + 2 * LW and H <= 8
    T = _tile(S, (512, 256, 128))
    c_idx = lax.axis_index('c').astype(jnp.int32).reshape(1)
    chip = 2 * lax.axis_index('x') + lax.axis_index('y')
    chip_idx = chip.astype(jnp.int32).reshape(1)
    vec = lambda n: W[n].reshape(1, -1)

    order = ['conv_w'] + BIG
    own = {n: W[n].astype(BF16) for n in BIG}
    own['conv_w'] = W['conv_w'].reshape(-1, LANES)
    g_ssem, g_rsem, g_src, g_zone, g_tok = gather_start([own[n] for n in order], [n == 'conv_w' for n in order])

    def fetch(n, after):
        t = order.index(n)
        src, zone = gather_wait('gather_wait_' + n, t, g_src[t], g_zone[t], g_ssem, g_rsem, after, n == 'conv_w')
        if n != 'conv_w':
            zone = pair_swap('pair_swap_' + n, zone)
        return lax.dynamic_update_index_in_dim(zone, src, chip, 0)

    b_f_pad = jnp.pad(vec('b_f'), ((0, 0), (0, LANES - H)))
    u_off, g_off = 3 * FW // LANES, (3 * FW + LW) // LANES

    h1 = norm_fwd('norm_mix', xs, vec('g_mix') + g_tok[0:1, 0:1])
    conv_full = fetch('conv_w', h1).reshape(N_CHIPS, CONV_W, LW // N_CHIPS).transpose(1, 0, 2).reshape(CONV_W, LW)
    w_in_full = fetch('w_in', h1).transpose(1, 0, 2).reshape(D, IN_W)
    w5 = jnp.concatenate([w_in_full[:, :3 * FW], w_in_full[:, 3 * FW + H:]], axis=1)
    wf = jnp.pad(w_in_full[:, 3 * FW:3 * FW + H], ((0, 0), (0, LANES - H)))
    proj = _mm('proj_in', h1, w5, 'nn', F32)
    f_raw = _mm('proj_f', h1, wf, 'nn', F32)
    qn, kn, vb = qkv_fwd(proj, vec('g_q'), vec('g_k'), FW)
    cc = fgate_fwd(f_raw, b_f_pad)
    ct = cc[:, :8].T
    o_fox, lse = fox_fwd(qn, kn, vb, cc, ct, T)
    lru_w = (conv_full, vec('conv_b'), W['w_ra'], vec('b_ra'), W['w_ri'], vec('b_ri'), vec('lam'))
    y_lru = lru_fwd(proj, *lru_w, u_off, g_off)
    mixn = mix_fwd(o_fox, y_lru, vec('g_fox_out'), vec('g_lru_out'))
    w_out_f = fetch('w_out', mixn).reshape(2 * FW, D)
    x1 = _mm('proj_out', mixn, w_out_f, 'nn', F32, res=xs)

    hq = norm_fwd('norm_xq', x1, vec('g_xattn'))
    mn = norm_fwd('norm_mem', ms, vec('g_mem'))
    w_cq_f = fetch('w_cq', hq).reshape(D, XW)
    w_ckv_f = fetch('w_ckv', hq).reshape(D, 2 * XW)
    cq_raw = _mm('proj_cq', hq, w_cq_f, 'nn', F32)
    ckv = _mm('proj_ckv', mn, w_ckv_f, 'nn', F32)
    o_x = xattn_fwd(cq_raw, ckv, vec('g_cq'), vec('g_ck'))
    w_co_g = fetch('w_co', o_x)
    x2 = _mm_colsharded('proj_co', o_x, w_co_g, F32, res=x1)

    hf = norm_fwd('norm_ffn', x2, vec('g_ffn'))
    w_gu_g = fetch('w_gate_up', hf)
    gu = _mm_colsharded('proj_gate_up', hf, w_gu_g, F32)
    act = swiglu_fwd(gu, F)
    w_down_f = fetch('w_down', act).reshape(F, D)
    yv = _mm('proj_down', act, w_down_f, 'nn', F32, res=x2)
    dy, dyb, loss_blk = loss_head(yv, tgt)

    gw, pending = {}, []

    def reduce_begin(n, g):
        sp = g.reshape(N_CHIPS, 2, g.shape[1] // 2, g.shape[2])
        ssem, rsem, sp, zone, tok = scatter_start('scatter_start_' + n, sp)
        pending.append((n, sp, zone, ssem, rsem))
        return tok[0:1, 0:1]

    dact = _mm('bwd_down_x', dyb, w_down_f, 'nt', F32)
    t_down = reduce_begin('w_down', _mm('bwd_down_w', act, dyb, 'tn', BF16).reshape(N_CHIPS, F // N_CHIPS, D))
    dgu = swiglu_bwd(gu, dact, F)
    dhf = _mm_colsharded_t('bwd_gate_up_x', dgu, w_gu_g, F32)
    t_gu = reduce_begin('w_gate_up', _mm_grad_colsharded('bwd_gate_up_w', hf, dgu, N_CHIPS, BF16))
    dx2, dx2b, gw['g_ffn'] = norm_bwd('norm_ffn_bwd', x2, vec('g_ffn') + t_down + t_gu, dhf, res=dy)

    do_x = _mm_colsharded_t('bwd_co_x', dx2b, w_co_g, BF16)
    t_co = reduce_begin('w_co', _mm_grad_colsharded('bwd_co_w', o_x, dx2b, N_CHIPS, BF16))
    dcq_raw, dckv, gw['g_cq'], gw['g_ck'] = xattn_bwd(cq_raw, ckv, vec('g_cq') + t_co, vec('g_ck'), do_x)
    dhq = _mm('bwd_cq_x', dcq_raw, w_cq_f, 'nt', F32)
    t_cq = reduce_begin('w_cq', _mm('bwd_cq_w', hq, dcq_raw, 'tn', BF16).reshape(N_CHIPS, D // N_CHIPS, XW))
    dmn = _mm('bwd_ckv_x', dckv, w_ckv_f, 'nt', F32)
    t_ckv = reduce_begin('w_ckv', _mm('bwd_ckv_w', mn, dckv, 'tn', BF16).reshape(N_CHIPS, D // N_CHIPS, 2 * XW))
    (gw['g_mem'],) = norm_bwd('norm_mem_bwd', ms, vec('g_mem'), dmn, want_dx=False)
    dx1, dx1b, gw['g_xattn'] = norm_bwd('norm_xq_bwd', x1, vec('g_xattn') + t_cq + t_ckv, dhq, res=dx2)

    dmix = _mm('bwd_out_x', dx1b, w_out_f, 'nt', F32)
    t_out = reduce_begin('w_out', _mm('bwd_out_w', mixn, dx1b, 'tn', BF16).reshape(N_CHIPS, 2 * FW // N_CHIPS, D))
    do_fox, delta, dy_lru, gw['g_fox_out'], gw['g_lru_out'] = mix_bwd(o_fox, y_lru, vec('g_fox_out') + t_out,
                                                                     vec('g_lru_out'), dmix)
    (du, dgate, gw['conv_w'], gw['conv_b'], gw['w_ra'], gw['b_ra'], gw['w_ri'], gw['b_ri'],
     gw['lam']) = lru_bwd(proj, dy_lru, *lru_w, u_off, g_off)
    dqn, delta2 = fox_bwd_q(qn, kn, vb, do_fox, cc, ct, lse, delta, T)
    dkn, dv, dct = fox_bwd_kv(qn, kn, vb, do_fox, cc, ct, lse, delta2, T)
    dq, dk, gw['g_q'], gw['g_k'] = qkv_bwd(proj, vec('g_q'), vec('g_k'), dqn, dkn, FW)
    dc = jnp.pad(dct.reshape(H, S).T, ((0, 0), (0, LANES - H)))
    df, db_f = fgate_bwd(f_raw, b_f_pad, dc, H)
    gw['b_f'] = db_f[:, :H]
    dproj = jnp.concatenate([dq, dk, dv, du, dgate], axis=1)
    dw5 = _mm('bwd_in_w', h1, dproj, 'tn', BF16)
    dwf = _mm('bwd_f_w', h1, df, 'tn', BF16)
    dw_in = jnp.concatenate([dw5[:, :3 * FW], dwf[:, :H], dw5[:, 3 * FW:]], axis=1)
    t_in = reduce_begin('w_in', dw_in.reshape(D, N_CHIPS, IN_W // N_CHIPS).transpose(1, 0, 2))
    dh_a = _mm('bwd_f_x', df, wf, 'nt', F32)
    dh1 = _mm('bwd_in_x', dproj, w5, 'nt', F32, res=dh_a)
    grad_x, _, gw['g_mix'] = norm_bwd('norm_mix_bwd', xs, vec('g_mix') + t_in, dh1, res=dx1)

    grads, delta_w, new_m, new_v = {}, {}, {}, {}
    for n, part, zone, ssem, rsem in pending:
        part, landed = scatter_wait('scatter_wait_' + n, part, zone, ssem, rsem, grad_x)
        mine = sum_parts('sum_parts_' + n, part, landed, chip_idx, c_idx)
        (other,) = pair_join('pair_join_' + n, [mine])
        grads[n], delta_w[n], new_m[n], new_v[n] = adamw_halves('adamw_' + n, W[n], mine, other, M1[n], V1[n], c_idx)

    small_shapes = [gw[n].shape for n in SMALL] + [(1, 1)]
    summed = _unpack(allreduce_small(_pack([gw[n] for n in SMALL] + [loss_blk[0:1, 0:1]]), delta_w[BIG[0]]), small_shapes)
    loss = summed[-1].reshape(())
    for n, g in zip(SMALL, summed):
        grads[n] = g.reshape(W[n].shape) if n != 'conv_w' else lax.dynamic_slice_in_dim(
            g, chip * (LW // N_CHIPS), LW // N_CHIPS, axis=1)
    packs = [_pack([d[n] for n in SMALL]) for d in (W, grads, M1, V1)]
    shapes = [W[n].shape for n in SMALL]
    for d, res in zip((delta_w, new_m, new_v), adamw('adamw_small', *packs)):
        d.update(zip(SMALL, _unpack(res, shapes)))

    lead = lambda d: [d[n][None] for n in WEIGHTS]
    return (loss, grad_x[None], *lead(grads), *lead(delta_w), *lead(new_m), *lead(new_v))
```

```python
import functools
import math

import jax
import jax.numpy as jnp
from jax import lax
from jax.experimental import pallas as pl
from jax.experimental.pallas import tpu as pltpu

F32 = jnp.float32
BF16 = jnp.bfloat16
HEAD_DIM = 128
LANES = 128
LRU_C = 8.0
RMS_EPS = 1e-6
CONV_W = 4
ADAM_LR = 0.001
ADAM_B1 = 0.9
ADAM_B2 = 0.999
ADAM_EPS = 1e-08
ADAM_WD = 0.01
ADAM_STEP = 10
VMEM_LIMIT = 56 * 1024 * 1024
N_CHIPS = 4
MESH = pl.DeviceIdType.MESH
ANY = pl.BlockSpec(memory_space=pl.ANY)

WEIGHTS = ['g_mix', 'w_in', 'b_f', 'g_q', 'g_k', 'conv_w', 'conv_b', 'w_ra', 'b_ra', 'w_ri', 'b_ri', 'lam',
           'g_fox_out', 'g_lru_out', 'w_out', 'g_xattn', 'g_mem', 'w_cq', 'w_ckv', 'g_cq', 'g_ck', 'w_co', 'g_ffn',
           'w_gate_up', 'w_down']
BIG = ['w_in', 'w_out', 'w_cq', 'w_ckv', 'w_co', 'w_gate_up', 'w_down']
SMALL = [n for n in WEIGHTS if n not in BIG]


def _params(sem=None):
    if sem is None:
        return pltpu.CompilerParams(vmem_limit_bytes=VMEM_LIMIT)
    return pltpu.CompilerParams(dimension_semantics=sem, vmem_limit_bytes=VMEM_LIMIT)


def _tile(n, cands):
    for t in cands:
        if n % t == 0:
            return t
    return n


ROW_BLOCK_BYTES = 1 << 20


def _row_tile(n_rows, n_cols, min_rows=8):
    cands = [t for t in (512, 256, 128, 64, 32, 16, 8) if t >= min_rows and t * n_cols * 4 <= ROW_BLOCK_BYTES]
    return _tile(n_rows, cands or [min_rows])


def _sigmoid(z):
    return 1.0 / (1.0 + jnp.exp(-z))


def _softplus(z):
    return jnp.maximum(z, 0.0) + jnp.log(1.0 + jnp.exp(-jnp.abs(z)))


def _neg_expm1(z):
    series = -z * (1.0 + z * (0.5 + z * (1.0 / 6.0 + z * (1.0 / 24.0 + z * (1.0 / 120.0)))))
    return jnp.where(z > -0.25, series, 1.0 - jnp.exp(z))


_GELU_K = math.sqrt(2.0 / math.pi)


def _gelu_and_grad(z):
    inner = _GELU_K * (z + 0.044715 * z * z * z)
    t = jnp.tanh(inner)
    g = 0.5 * z * (1.0 + t)
    dg = 0.5 * (1.0 + t) + 0.5 * z * (1.0 - t * t) * _GELU_K * (1.0 + 3.0 * 0.044715 * z * z)
    return g, dg


def _rms(xv, g):
    r = lax.rsqrt(jnp.mean(xv * xv, axis=-1, keepdims=True) + RMS_EPS)
    return xv * r * g


def _rms_bwd(xv, g, dy):
    r = lax.rsqrt(jnp.mean(xv * xv, axis=-1, keepdims=True) + RMS_EPS)
    xh = xv * r
    dyg = dy * g
    dx = r * (dyg - xh * jnp.mean(dyg * xh, axis=-1, keepdims=True))
    return dx, jnp.sum(dy * xh, axis=0, keepdims=True)


def _heads(fn, n_heads, *arrs):
    outs = [fn(*[a[:, h * HEAD_DIM:(h + 1) * HEAD_DIM] for a in arrs]) for h in range(n_heads)]
    first = jnp.concatenate([o[0] for o in outs], axis=1) if n_heads > 1 else outs[0][0]
    rest = [functools.reduce(lambda p, q: p + q, [o[i] for o in outs]) for i in range(1, len(outs[0]))]
    return (first, *rest)


def _split3(v):
    hi = v.astype(BF16)
    r1 = v - hi.astype(F32)
    mid = r1.astype(BF16)
    lo = (r1 - mid.astype(F32)).astype(BF16)
    return hi, mid, lo


def _acc_out(ref, first, val):
    @pl.when(first)
    def _():
        ref[...] = val

    @pl.when(jnp.logical_not(first))
    def _():
        ref[...] += val


_DIMS = {'nn': (((1,), (0,)), ((), ())), 'nt': (((1,), (1,)), ((), ())), 'tn': (((0,), (0,)), ((), ()))}


MM_VMEM_BYTES = 36 * 1024 * 1024


def _k_tile(K, tm, tn, a, b, o_dtype, res):
    fixed = tm * tn * (2 * jnp.dtype(o_dtype).itemsize + 4 + (8 if res is not None else 0))
    per_k = 2 * (tm * a.dtype.itemsize + tn * b.dtype.itemsize)
    per_k += 2 * tm * (a.dtype.itemsize > 2) + 2 * tn * (b.dtype.itemsize > 2)
    units = K // LANES
    for d in sorted((d for d in range(1, units + 1) if units % d == 0), reverse=True):
        if fixed + d * LANES * per_k <= MM_VMEM_BYTES:
            return d * LANES
    return LANES


def _mm_call(name, a, b, mode, grid, a_spec, b_spec, o_spec, o_shape, o_dtype, acc_shape, res=None):
    nk = grid[2]
    dn = _DIMS[mode]

    def body(*refs):
        a_ref, b_ref = refs[:2]
        r_ref = refs[2] if res is not None else None
        o_ref = refs[3] if res is not None else refs[2]
        part = lax.dot_general(a_ref[...].astype(BF16), b_ref[...].astype(BF16), dn, preferred_element_type=F32)

        def finish(r):
            if r_ref is not None:
                r = r + r_ref[...]
            o_ref[...] = r.astype(o_dtype)

        if nk == 1:
            finish(part)
            return
        acc = refs[-1]
        k = pl.program_id(2)

        @pl.when(k == 0)
        def _():
            acc[...] = part

        @pl.when(k > 0)
        def _():
            acc[...] += part

        @pl.when(k == nk - 1)
        def _():
            finish(acc[...])

    ins = [a, b] + ([] if res is None else [res])
    specs = [a_spec, b_spec] + ([] if res is None else [o_spec])
    return pl.pallas_call(
        body, name=name, grid=grid, in_specs=specs, out_specs=o_spec,
        out_shape=jax.ShapeDtypeStruct(o_shape, o_dtype),
        scratch_shapes=[] if nk == 1 else [pltpu.VMEM(acc_shape, F32)],
        compiler_params=_params(('parallel', 'parallel', 'arbitrary')))(*ins)


def _mm(name, a, b, mode, o_dtype, res=None):
    if mode == 'tn':
        K, M = a.shape
    else:
        M, K = a.shape
    N = b.shape[0] if mode == 'nt' else b.shape[1]
    tm = _tile(M, (1024, 512, 256, 128))
    tn = _tile(N, (1024, 512, 256, 128))
    tk = _k_tile(K, tm, tn, a, b, o_dtype, res)
    a_spec = (pl.BlockSpec((tk, tm), lambda m, n, k: (k, m)) if mode == 'tn'
              else pl.BlockSpec((tm, tk), lambda m, n, k: (m, k)))
    b_spec = (pl.BlockSpec((tn, tk), lambda m, n, k: (n, k)) if mode == 'nt'
              else pl.BlockSpec((tk, tn), lambda m, n, k: (k, n)))
    o_spec = pl.BlockSpec((tm, tn), lambda m, n, k: (m, n))
    return _mm_call(name, a, b, mode, (M // tm, N // tn, K // tk), a_spec, b_spec, o_spec, (M, N), o_dtype,
                    (tm, tn), res)


def _mm_colsharded(name, a, w, o_dtype, res=None):
    M, K = a.shape
    J, _, Nj = w.shape
    tm = _tile(M, (1024, 512, 256, 128))
    tn = _tile(Nj, (1408, 1024, 512, 256, 128))
    tk = _k_tile(K, tm, tn, a, w, o_dtype, res)
    per = Nj // tn
    return _mm_call(name, a, w, 'nn', (M // tm, J * per, K // tk),
                    pl.BlockSpec((tm, tk), lambda m, n, k: (m, k)),
                    pl.BlockSpec((None, tk, tn), lambda m, n, k: (n // per, k, n % per)),
                    pl.BlockSpec((tm, tn), lambda m, n, k: (m, n)), (M, J * Nj), o_dtype, (tm, tn), res)


def _planes_spec(arr, rows, cols, row_of, col_of):
    if arr.ndim == 2:
        return pl.BlockSpec((rows, cols), lambda m, n, k: (row_of(m, n, k), col_of(m, n, k)))
    per_plane = arr.shape[2] // cols
    return pl.BlockSpec((None, rows, cols),
                        lambda m, n, k: (col_of(m, n, k) // per_plane, row_of(m, n, k), col_of(m, n, k) % per_plane))


def _mm_colsharded_t(name, a, w, o_dtype):
    M = a.shape[-2]
    J, K, Nj = w.shape
    tm = _tile(M, (1024, 512, 256, 128))
    tn = _tile(K, (1024, 512, 256, 128))
    tk = _k_tile(Nj, tm, tn, a, w, o_dtype, None)
    per = Nj // tk
    return _mm_call(name, a, w, 'nt', (M // tm, K // tn, J * per),
                    _planes_spec(a, tm, tk, lambda m, n, k: m, lambda m, n, k: k),
                    pl.BlockSpec((None, tn, tk), lambda m, n, k: (k // per, n, k % per)),
                    pl.BlockSpec((tm, tn), lambda m, n, k: (m, n)), (M, K), o_dtype, (tm, tn))


def _mm_grad_colsharded(name, a, dy, J, o_dtype):
    S, M = a.shape
    Nj = dy.shape[-1] * (dy.shape[0] if dy.ndim == 3 else 1) // J
    tm = _tile(M, (1024, 512, 256, 128))
    tn = _tile(Nj, (1408, 1024, 512, 256, 128))
    tk = _k_tile(S, tm, tn, a, dy, o_dtype, None)
    per = Nj // tn
    return _mm_call(name, a, dy, 'tn', (M // tm, J * per, S // tk),
                    pl.BlockSpec((tk, tm), lambda m, n, k: (k, m)),
                    _planes_spec(dy, tk, tn, lambda m, n, k: k, lambda m, n, k: n),
                    pl.BlockSpec((None, tm, tn), lambda m, n, k: (n // per, m, n % per)), (J, M, Nj), o_dtype, (tm, tn))


def _rows_call(name, body, n_rows, tr, ins, outs):
    return pl.pallas_call(
        body, name=name, grid=(n_rows // tr,), in_specs=[s for _, s in ins], out_specs=[s for _, _, s in outs],
        out_shape=[jax.ShapeDtypeStruct(sh, dt) for sh, dt, _ in outs],
        compiler_params=_params(('arbitrary',)))(*[a for a, _ in ins])


def _rb(tr, w, cb=0):
    return pl.BlockSpec((tr, w), lambda i: (i, cb))


def _fb(shape):
    nd = len(shape)
    return pl.BlockSpec(shape, lambda i: (0,) * nd)


def norm_fwd(name, xv, g):
    S, D = xv.shape
    tr = _tile(S, (256, 128))

    def body(x_ref, g_ref, o_ref):
        o_ref[...] = _rms(x_ref[...], g_ref[...]).astype(BF16)

    return _rows_call(name, body, S, tr, [(xv, _rb(tr, D)), (g, _fb((1, D)))], [((S, D), BF16, _rb(tr, D))])[0]


def norm_bwd(name, xv, g, dy, res=None, want_dx=True):
    S, D = xv.shape
    tr = _tile(S, (256, 128))

    def body(*refs):
        if res is None:
            x_ref, g_ref, dy_ref = refs[:3]
            outs = refs[3:]
            r_ref = None
        else:
            x_ref, g_ref, dy_ref, r_ref = refs[:4]
            outs = refs[4:]
        dx, dg = _rms_bwd(x_ref[...], g_ref[...], dy_ref[...])
        if r_ref is not None:
            dx = dx + r_ref[...]
        if want_dx:
            outs[0][...] = dx
            outs[1][...] = dx.astype(BF16)
        _acc_out(outs[-1], pl.program_id(0) == 0, dg)

    ins = [(xv, _rb(tr, D)), (g, _fb((1, D))), (dy, _rb(tr, D))] + ([] if res is None else [(res, _rb(tr, D))])
    outs = ([((S, D), F32, _rb(tr, D)), ((S, D), BF16, _rb(tr, D))] if want_dx else []) + [((1, D), F32, _fb((1, D)))]
    return _rows_call(name, body, S, tr, ins, outs)


def qkv_fwd(proj, g_q, g_k, FW):
    S = proj.shape[0]
    H = FW // HEAD_DIM
    tr = _tile(S, (256, 128))

    def body(q_ref, k_ref, v_ref, gq_ref, gk_ref, qo, ko, vo):
        qo[...] = _heads(lambda t: (_rms(t, gq_ref[...]),), H, q_ref[...])[0].astype(BF16)
        ko[...] = _heads(lambda t: (_rms(t, gk_ref[...]),), H, k_ref[...])[0].astype(BF16)
        vo[...] = v_ref[...].astype(BF16)

    o = ((S, FW), BF16, _rb(tr, FW))
    return _rows_call('qkv_fwd', body, S, tr,
                      [(proj, _rb(tr, FW, 0)), (proj, _rb(tr, FW, 1)), (proj, _rb(tr, FW, 2)),
                       (g_q, _fb((1, HEAD_DIM))), (g_k, _fb((1, HEAD_DIM)))], [o, o, o])


def qkv_bwd(proj, g_q, g_k, dqn, dkn, FW):
    S = proj.shape[0]
    H = FW // HEAD_DIM
    tr = _tile(S, (256, 128))

    def body(q_ref, k_ref, gq_ref, gk_ref, dq_ref, dk_ref, dqo, dko, dgq, dgk):
        dq, gq = _heads(lambda t, d: _rms_bwd(t, gq_ref[...], d), H, q_ref[...], dq_ref[...])
        dk, gk = _heads(lambda t, d: _rms_bwd(t, gk_ref[...], d), H, k_ref[...], dk_ref[...])
        dqo[...] = dq.astype(BF16)
        dko[...] = dk.astype(BF16)
        first = pl.program_id(0) == 0
        _acc_out(dgq, first, gq)
        _acc_out(dgk, first, gk)

    o = ((S, FW), BF16, _rb(tr, FW))
    og = ((1, HEAD_DIM), F32, _fb((1, HEAD_DIM)))
    return _rows_call('qkv_bwd', body, S, tr,
                      [(proj, _rb(tr, FW, 0)), (proj, _rb(tr, FW, 1)), (g_q, _fb((1, HEAD_DIM))),
                       (g_k, _fb((1, HEAD_DIM))), (dqn, _rb(tr, FW)), (dkn, _rb(tr, FW))], [o, o, og, og])


def _tri(n, upper):
    r = lax.broadcasted_iota(jnp.int32, (n, n), 0)
    c = lax.broadcasted_iota(jnp.int32, (n, n), 1)
    return jnp.where((c >= r) if upper else (c <= r), 1.0, 0.0).astype(BF16)


def _blocked_cumsum(val, S, blk, reverse):
    tri = _tri(blk, reverse)
    order = range(S // blk - 1, -1, -1) if reverse else range(S // blk)
    carry = jnp.zeros((1, LANES), F32)
    outs = {}
    for bi in order:
        part = val[bi * blk:(bi + 1) * blk]
        acc = carry
        for piece in _split3(part):
            acc = acc + jnp.dot(tri, piece, preferred_element_type=F32)
        outs[bi] = acc
        carry = carry + jnp.sum(part, axis=0, keepdims=True)
    return jnp.concatenate([outs[bi] for bi in range(S // blk)], axis=0)


def fgate_fwd(f_raw, b_f_pad):
    S = f_raw.shape[0]
    blk = _tile(S, (256, 128))

    def body(f_ref, b_ref, c_ref):
        z = f_ref[...] + b_ref[...]
        c_ref[...] = _blocked_cumsum(-_softplus(-z), S, blk, False)

    return pl.pallas_call(body, name='fgate_fwd', grid=(1,), in_specs=[_fb((S, LANES)), _fb((1, LANES))],
                          out_specs=_fb((S, LANES)), out_shape=jax.ShapeDtypeStruct((S, LANES), F32),
                          compiler_params=_params(('arbitrary',)))(f_raw, b_f_pad)


def fgate_bwd(f_raw, b_f_pad, dc, H):
    S = f_raw.shape[0]
    blk = _tile(S, (256, 128))

    def body(f_ref, b_ref, dc_ref, df_ref, db_ref):
        z = f_ref[...] + b_ref[...]
        dlogf = _blocked_cumsum(dc_ref[...], S, blk, True)
        lane = lax.broadcasted_iota(jnp.int32, (S, LANES), 1)
        df = jnp.where(lane < H, dlogf * _sigmoid(-z), 0.0)
        df_ref[...] = df.astype(BF16)
        db_ref[...] = jnp.sum(df, axis=0, keepdims=True)

    return pl.pallas_call(body, name='fgate_bwd', grid=(1,),
                          in_specs=[_fb((S, LANES)), _fb((1, LANES)), _fb((S, LANES))],
                          out_specs=[_fb((S, LANES)), _fb((1, LANES))],
                          out_shape=[jax.ShapeDtypeStruct((S, LANES), BF16), jax.ShapeDtypeStruct((1, LANES), F32)],
                          compiler_params=_params(('arbitrary',)))(f_raw, b_f_pad, dc)


def _fox_logits(q, k, c_blk, ct_blk, h, i, j, T):
    s = lax.dot_general(q, k, _DIMS['nt'], preferred_element_type=F32) * (1.0 / math.sqrt(HEAD_DIM))
    lane = lax.broadcasted_iota(jnp.int32, c_blk.shape, 1)
    cq = jnp.sum(jnp.where(lane == h, c_blk, 0.0), axis=1, keepdims=True)
    sub = lax.broadcasted_iota(jnp.int32, ct_blk.shape, 0)
    ck = jnp.sum(jnp.where(sub == h, ct_blk, 0.0), axis=0, keepdims=True)
    rows = i * T + lax.broadcasted_iota(jnp.int32, (T, T), 0)
    cols = j * T + lax.broadcasted_iota(jnp.int32, (T, T), 1)
    return jnp.where(cols <= rows, s + cq - ck, -jnp.inf)


def fox_fwd(qn, kn, vb, c, ct, T):
    S, FW = qn.shape
    H = FW // HEAD_DIM
    Hp = ct.shape[0]
    n = S // T

    def body(q_ref, k_ref, v_ref, c_ref, ct_ref, o_ref, lse_ref, m_s, l_s, acc_s):
        h, i, j = pl.program_id(0), pl.program_id(1), pl.program_id(2)

        @pl.when(j == 0)
        def _():
            m_s[...] = jnp.full_like(m_s, -jnp.inf)
            l_s[...] = jnp.zeros_like(l_s)
            acc_s[...] = jnp.zeros_like(acc_s)

        @pl.when(j <= i)
        def _():
            s = _fox_logits(q_ref[...], k_ref[...], c_ref[...], ct_ref[...], h, i, j, T)
            m_new = jnp.maximum(m_s[...], jnp.max(s, axis=1, keepdims=True))
            alpha = jnp.exp(m_s[...] - m_new)
            p = jnp.exp(s - m_new)
            l_s[...] = alpha * l_s[...] + jnp.sum(p, axis=1, keepdims=True)
            acc_s[...] = alpha * acc_s[...] + jnp.dot(p.astype(BF16), v_ref[...], preferred_element_type=F32)
            m_s[...] = m_new

        @pl.when(j == i)
        def _():
            o_ref[...] = acc_s[...] / l_s[...]
            lse_ref[...] = jnp.broadcast_to(m_s[...] + jnp.log(l_s[...]), (T, LANES))

    qs = pl.BlockSpec((T, HEAD_DIM), lambda h, i, j: (i, h))
    ks = pl.BlockSpec((T, HEAD_DIM), lambda h, i, j: (jnp.minimum(j, i), h))
    return pl.pallas_call(
        body, name='fox_fwd', grid=(H, n, n),
        in_specs=[qs, ks, ks, pl.BlockSpec((T, LANES), lambda h, i, j: (i, 0)),
                  pl.BlockSpec((Hp, T), lambda h, i, j: (0, jnp.minimum(j, i)))],
        out_specs=[qs, pl.BlockSpec((None, T, LANES), lambda h, i, j: (h, i, 0))],
        out_shape=[jax.ShapeDtypeStruct((S, FW), F32), jax.ShapeDtypeStruct((H, S, LANES), F32)],
        scratch_shapes=[pltpu.VMEM((T, 1), F32), pltpu.VMEM((T, 1), F32), pltpu.VMEM((T, HEAD_DIM), F32)],
        compiler_params=_params(('parallel', 'parallel', 'arbitrary')))(qn, kn, vb, c, ct)


def _fox_p_ds(q_ref, k_ref, v_ref, do_ref, c_ref, ct_ref, lse_ref, dl_ref, h, i, j, T):
    s = _fox_logits(q_ref[...], k_ref[...], c_ref[...], ct_ref[...], h, i, j, T)
    p = jnp.exp(s - jnp.tile(lse_ref[...], (1, T // LANES)))
    dp = lax.dot_general(do_ref[...], v_ref[...], _DIMS['nt'], preferred_element_type=F32)
    ds = p * (dp - jnp.tile(dl_ref[...], (1, T // LANES)))
    return p, dp, ds


def fox_bwd_q(qn, kn, vb, do, c, ct, lse, dl, T):
    S, FW = qn.shape
    H = FW // HEAD_DIM
    Hp = ct.shape[0]
    n = S // T

    def body(q_ref, k_ref, v_ref, do_ref, c_ref, ct_ref, lse_ref, dl_ref, dq_ref, dl2_ref, acc_s, rs_s):
        h, i, j = pl.program_id(0), pl.program_id(1), pl.program_id(2)

        @pl.when(j == 0)
        def _():
            acc_s[...] = jnp.zeros_like(acc_s)
            rs_s[...] = jnp.zeros_like(rs_s)

        @pl.when(j <= i)
        def _():
            p, dp, ds = _fox_p_ds(q_ref, k_ref, v_ref, do_ref, c_ref, ct_ref, lse_ref, dl_ref, h, i, j, T)
            acc_s[...] += jnp.dot(ds.astype(BF16), k_ref[...], preferred_element_type=F32)
            rs_s[...] += jnp.sum(p * dp, axis=1, keepdims=True)

        @pl.when(j == i)
        def _():
            dq_ref[...] = acc_s[...] * (1.0 / math.sqrt(HEAD_DIM))
            dl2_ref[...] = jnp.broadcast_to(rs_s[...], (T, LANES))

    qs = pl.BlockSpec((T, HEAD_DIM), lambda h, i, j: (i, h))
    ks = pl.BlockSpec((T, HEAD_DIM), lambda h, i, j: (jnp.minimum(j, i), h))
    st = pl.BlockSpec((None, T, LANES), lambda h, i, j: (h, i, 0))
    return pl.pallas_call(
        body, name='fox_bwd_q', grid=(H, n, n),
        in_specs=[qs, ks, ks, qs, pl.BlockSpec((T, LANES), lambda h, i, j: (i, 0)),
                  pl.BlockSpec((Hp, T), lambda h, i, j: (0, jnp.minimum(j, i))), st, st],
        out_specs=[qs, st], out_shape=[jax.ShapeDtypeStruct((S, FW), F32), jax.ShapeDtypeStruct((H, S, LANES), F32)],
        scratch_shapes=[pltpu.VMEM((T, HEAD_DIM), F32), pltpu.VMEM((T, 1), F32)],
        compiler_params=_params(('parallel', 'parallel', 'arbitrary')))(qn, kn, vb, do, c, ct, lse, dl)


def fox_bwd_kv(qn, kn, vb, do, c, ct, lse, dl, T):
    S, FW = qn.shape
    H = FW // HEAD_DIM
    Hp = ct.shape[0]
    n = S // T

    def body(q_ref, k_ref, v_ref, do_ref, c_ref, ct_ref, lse_ref, dl_ref, dk_ref, dv_ref, dc_ref, dk_s, dv_s, dc_s):
        h, j, i = pl.program_id(0), pl.program_id(1), pl.program_id(2)

        @pl.when(i == 0)
        def _():
            dk_s[...] = jnp.zeros_like(dk_s)
            dv_s[...] = jnp.zeros_like(dv_s)
            dc_s[...] = jnp.zeros_like(dc_s)

        @pl.when(i >= j)
        def _():
            p, _, ds = _fox_p_ds(q_ref, k_ref, v_ref, do_ref, c_ref, ct_ref, lse_ref, dl_ref, h, i, j, T)
            dv_s[...] += lax.dot_general(p.astype(BF16), do_ref[...], _DIMS['tn'], preferred_element_type=F32)
            dk_s[...] += lax.dot_general(ds.astype(BF16), q_ref[...], _DIMS['tn'], preferred_element_type=F32)
            dc_s[...] += jnp.sum(ds, axis=0, keepdims=True)

        @pl.when(i == n - 1)
        def _():
            dk_ref[...] = dk_s[...] * (1.0 / math.sqrt(HEAD_DIM))
            dv_ref[...] = dv_s[...].astype(BF16)
            dc_ref[...] = -dc_s[...]

    qs = pl.BlockSpec((T, HEAD_DIM), lambda h, j, i: (jnp.maximum(i, j), h))
    ks = pl.BlockSpec((T, HEAD_DIM), lambda h, j, i: (j, h))
    st = pl.BlockSpec((None, T, LANES), lambda h, j, i: (h, jnp.maximum(i, j), 0))
    return pl.pallas_call(
        body, name='fox_bwd_kv', grid=(H, n, n),
        in_specs=[qs, ks, ks, qs, pl.BlockSpec((T, LANES), lambda h, j, i: (jnp.maximum(i, j), 0)),
                  pl.BlockSpec((Hp, T), lambda h, j, i: (0, j)), st, st],
        out_specs=[ks, ks, pl.BlockSpec((None, 1, T), lambda h, j, i: (h, 0, j))],
        out_shape=[jax.ShapeDtypeStruct((S, FW), F32), jax.ShapeDtypeStruct((S, FW), BF16),
                   jax.ShapeDtypeStruct((H, 1, S), F32)],
        scratch_shapes=[pltpu.VMEM((T, HEAD_DIM), F32), pltpu.VMEM((T, HEAD_DIM), F32), pltpu.VMEM((1, T), F32)],
        compiler_params=_params(('parallel', 'parallel', 'arbitrary')))(qn, kn, vb, do, c, ct, lse, dl)


def _shift_down(v, d, rows, fill):
    return jnp.where(rows >= d, pltpu.roll(v, d, 0), fill)


def _shift_up(v, d, rows, S, fill):
    return jnp.where(rows < S - d, pltpu.roll(v, S - d, 0), fill)


def _scan(a, b, rows, S, reverse):
    d = 1
    while d < S:
        if reverse:
            a_s, b_s = _shift_up(a, d, rows, S, 1.0), _shift_up(b, d, rows, S, 0.0)
        else:
            a_s, b_s = _shift_down(a, d, rows, 1.0), _shift_down(b, d, rows, 0.0)
        b = a * b_s + b
        a = a * a_s
        d *= 2
    return b


def _lru_forward(u, cw, cb, wra, bra, wri, bri, lam, rows):
    uc = cb + cw[CONV_W - 1] * u
    for d in range(1, CONV_W):
        uc = uc + cw[CONV_W - 1 - d] * _shift_down(u, d, rows, 0.0)
    ucb = uc.astype(BF16)
    r = _sigmoid(jnp.dot(ucb, wra.astype(BF16), preferred_element_type=F32) + bra)
    ig = _sigmoid(jnp.dot(ucb, wri.astype(BF16), preferred_element_type=F32) + bri)
    sp = _softplus(-lam)
    log_a = -LRU_C * r * sp
    a = jnp.exp(log_a)
    sq = jnp.sqrt(_neg_expm1(2.0 * log_a))
    iu = ig * uc
    hseq = _scan(a, sq * iu, rows, u.shape[0], False)
    return uc, ucb, r, ig, sp, a, sq, iu, hseq


def _lru_specs(S, n_u, n_g):
    col = lambda off: pl.BlockSpec((S, LANES), lambda cbk: (0, off + cbk))
    vec = pl.BlockSpec((1, LANES), lambda cbk: (0, cbk))
    mat = pl.BlockSpec((None, LANES, LANES), lambda cbk: (cbk, 0, 0))
    cw = pl.BlockSpec((CONV_W, LANES), lambda cbk: (0, cbk))
    return col, vec, mat, cw


def lru_fwd(proj, conv_w, conv_b, w_ra, b_ra, w_ri, b_ri, lam, u_off, g_off):
    S = proj.shape[0]
    nb = w_ra.shape[0]
    col, vec, mat, cws = _lru_specs(S, u_off, g_off)

    def body(u_ref, g_ref, cw_ref, cb_ref, wra_ref, bra_ref, wri_ref, bri_ref, lam_ref, y_ref):
        rows = lax.broadcasted_iota(jnp.int32, (S, LANES), 0)
        cw = [cw_ref[t:t + 1, :] for t in range(CONV_W)]
        hseq = _lru_forward(u_ref[...], cw, cb_ref[...], wra_ref[...], bra_ref[...], wri_ref[...],
                            bri_ref[...], lam_ref[...], rows)[-1]
        y_ref[...] = hseq * _gelu_and_grad(g_ref[...])[0]

    return pl.pallas_call(
        body, name='lru_fwd', grid=(nb,),
        in_specs=[col(u_off), col(g_off), cws, vec, mat, vec, mat, vec, vec], out_specs=col(0),
        out_shape=jax.ShapeDtypeStruct((S, nb * LANES), F32),
        compiler_params=_params(('parallel',)))(proj, proj, conv_w, conv_b, w_ra, b_ra, w_ri, b_ri, lam)


def lru_bwd(proj, dy, conv_w, conv_b, w_ra, b_ra, w_ri, b_ri, lam, u_off, g_off):
    S = proj.shape[0]
    nb = w_ra.shape[0]
    LW = nb * LANES
    col, vec, mat, cws = _lru_specs(S, u_off, g_off)

    def body(u_ref, g_ref, dy_ref, cw_ref, cb_ref, wra_ref, bra_ref, wri_ref, bri_ref, lam_ref,
             du_ref, dg_ref, dcw_ref, dcb_ref, dwra_ref, dbra_ref, dwri_ref, dbri_ref, dlam_ref):
        rows = lax.broadcasted_iota(jnp.int32, (S, LANES), 0)
        u, lam_v = u_ref[...], lam_ref[...]
        cw = [cw_ref[t:t + 1, :] for t in range(CONV_W)]
        wra, wri = wra_ref[...].astype(BF16), wri_ref[...].astype(BF16)
        uc, ucb, r, ig, sp, a, sq, iu, hseq = _lru_forward(u, cw, cb_ref[...], wra, bra_ref[...], wri, bri_ref[...],
                                                           lam_v, rows)
        gl, dgl = _gelu_and_grad(g_ref[...])
        dy_v = dy_ref[...]
        dg_ref[...] = (dy_v * hseq * dgl).astype(BF16)
        G = _scan(_shift_up(a, 1, rows, S, 0.0), dy_v * gl, rows, S, True)
        da = G * _shift_down(hseq, 1, rows, 0.0)
        diu = G * sq
        dsq = G * iu
        dlog_a = da * a - dsq * a * a / jnp.maximum(sq, 1e-30)
        dr = dlog_a * (-LRU_C * sp)
        dsp = jnp.sum(dlog_a * (-LRU_C * r), axis=0, keepdims=True)
        dlam_ref[...] = -dsp * _sigmoid(-lam_v)
        dzr = dr * r * (1.0 - r)
        dzi = diu * uc * ig * (1.0 - ig)
        dzrb, dzib = dzr.astype(BF16), dzi.astype(BF16)
        duc = (diu * ig + lax.dot_general(dzrb, wra, _DIMS['nt'], preferred_element_type=F32)
               + lax.dot_general(dzib, wri, _DIMS['nt'], preferred_element_type=F32))
        dwra_ref[...] = lax.dot_general(ucb, dzrb, _DIMS['tn'], preferred_element_type=F32)
        dwri_ref[...] = lax.dot_general(ucb, dzib, _DIMS['tn'], preferred_element_type=F32)
        dbra_ref[...] = jnp.sum(dzr, axis=0, keepdims=True)
        dbri_ref[...] = jnp.sum(dzi, axis=0, keepdims=True)
        dcb_ref[...] = jnp.sum(duc, axis=0, keepdims=True)
        du = cw[CONV_W - 1] * duc
        dcw_ref[CONV_W - 1:CONV_W, :] = jnp.sum(duc * u, axis=0, keepdims=True)
        for d in range(1, CONV_W):
            du = du + cw[CONV_W - 1 - d] * _shift_up(duc, d, rows, S, 0.0)
            dcw_ref[CONV_W - 1 - d:CONV_W - d, :] = jnp.sum(duc * _shift_down(u, d, rows, 0.0), axis=0, keepdims=True)
        du_ref[...] = du.astype(BF16)

    sd = jax.ShapeDtypeStruct
    return pl.pallas_call(
        body, name='lru_bwd', grid=(nb,),
        in_specs=[col(u_off), col(g_off), col(0), cws, vec, mat, vec, mat, vec, vec],
        out_specs=[col(0), col(0), cws, vec, mat, vec, mat, vec, vec],
        out_shape=[sd((S, LW), BF16), sd((S, LW), BF16), sd((CONV_W, LW), F32), sd((1, LW), F32),
                   sd((nb, LANES, LANES), F32), sd((1, LW), F32), sd((nb, LANES, LANES), F32), sd((1, LW), F32),
                   sd((1, LW), F32)],
        compiler_params=_params(('parallel',)))(proj, proj, dy, conv_w, conv_b, w_ra, b_ra, w_ri, b_ri, lam)


def mix_fwd(o_fox, y_lru, g_fox, g_lru):
    S, FW = o_fox.shape
    tr = _tile(S, (256, 128))

    def body(o_ref, y_ref, gf_ref, gl_ref, m_ref):
        m_ref[...] = jnp.concatenate([_rms(o_ref[...], gf_ref[...]), _rms(y_ref[...], gl_ref[...])],
                                     axis=1).astype(BF16)

    return _rows_call('mix_fwd', body, S, tr,
                      [(o_fox, _rb(tr, FW)), (y_lru, _rb(tr, FW)), (g_fox, _fb((1, FW))), (g_lru, _fb((1, FW)))],
                      [((S, 2 * FW), BF16, _rb(tr, 2 * FW))])[0]


def mix_bwd(o_fox, y_lru, g_fox, g_lru, dmix):
    S, FW = o_fox.shape
    H = FW // HEAD_DIM
    tr = _tile(S, (256, 128))

    def body(o_ref, y_ref, gf_ref, gl_ref, df_ref, dl_ref, do_ref, dlt_ref, dy_ref, dgf_ref, dgl_ref):
        o = o_ref[...]
        do, dgf = _rms_bwd(o, gf_ref[...], df_ref[...])
        dyl, dgl = _rms_bwd(y_ref[...], gl_ref[...], dl_ref[...])
        do_ref[...] = do.astype(BF16)
        dy_ref[...] = dyl
        prod = do * o
        for h in range(H):
            dlt_ref[h] = jnp.broadcast_to(
                jnp.sum(prod[:, h * HEAD_DIM:(h + 1) * HEAD_DIM], axis=1, keepdims=True), (tr, LANES))
        first = pl.program_id(0) == 0
        _acc_out(dgf_ref, first, dgf)
        _acc_out(dgl_ref, first, dgl)

    g = _fb((1, FW))
    return _rows_call('mix_bwd', body, S, tr,
                      [(o_fox, _rb(tr, FW)), (y_lru, _rb(tr, FW)), (g_fox, g), (g_lru, g), (dmix, _rb(tr, FW, 0)),
                       (dmix, _rb(tr, FW, 1))],
                      [((S, FW), BF16, _rb(tr, FW)), ((H, S, LANES), F32, pl.BlockSpec((H, tr, LANES), lambda i: (0, i, 0))),
                       ((S, FW), F32, _rb(tr, FW)), ((1, FW), F32, g), ((1, FW), F32, g)])


def _xattn_heads(cq_raw, ckv, g_cq, g_ck, XW):
    out = []
    for h in range(XW // HEAD_DIM):
        sl = slice(h * HEAD_DIM, (h + 1) * HEAD_DIM)
        out.append((cq_raw[:, sl], _rms(cq_raw[:, sl], g_cq), ckv[:, sl], _rms(ckv[:, sl], g_ck),
                    ckv[:, XW + h * HEAD_DIM:XW + (h + 1) * HEAD_DIM].astype(BF16)))
    return out


def xattn_fwd(cq_raw, ckv, g_cq, g_ck):
    S, XW = cq_raw.shape
    M = ckv.shape[0]
    tr = _tile(S, (512, 256, 128))

    def body(q_ref, kv_ref, gq_ref, gk_ref, o_ref):
        outs = []
        for _, qn, _, kn, v in _xattn_heads(q_ref[...], kv_ref[...], gq_ref[...], gk_ref[...], XW):
            s = lax.dot_general(qn.astype(BF16), kn.astype(BF16), _DIMS['nt'], preferred_element_type=F32)
            s = s / math.sqrt(HEAD_DIM)
            p = jnp.exp(s - jnp.max(s, axis=1, keepdims=True))
            p = p / jnp.sum(p, axis=1, keepdims=True)
            outs.append(jnp.dot(p.astype(BF16), v, preferred_element_type=F32))
        o_ref[...] = jnp.concatenate(outs, axis=1).astype(BF16)

    g = _fb((1, HEAD_DIM))
    return _rows_call('xattn_fwd', body, S, tr,
                      [(cq_raw, _rb(tr, XW)), (ckv, _fb((M, 2 * XW))), (g_cq, g), (g_ck, g)],
                      [((S, XW), BF16, _rb(tr, XW))])[0]


def xattn_bwd(cq_raw, ckv, g_cq, g_ck, do):
    S, XW = cq_raw.shape
    M = ckv.shape[0]
    tr = _tile(S, (512, 256, 128))
    n = S // tr

    def body(q_ref, kv_ref, gq_ref, gk_ref, do_ref, dq_ref, dkv_ref, dgq_ref, dgk_ref):
        i = pl.program_id(0)
        do_v = do_ref[...]
        dqs, dkn, dvs = [], [], []
        dgq = jnp.zeros((1, HEAD_DIM), F32)
        for h, (q_raw, qn, _, kn, v) in enumerate(_xattn_heads(q_ref[...], kv_ref[...], gq_ref[...], gk_ref[...], XW)):
            qb, kb = qn.astype(BF16), kn.astype(BF16)
            doh = do_v[:, h * HEAD_DIM:(h + 1) * HEAD_DIM]
            s = lax.dot_general(qb, kb, _DIMS['nt'], preferred_element_type=F32) / math.sqrt(HEAD_DIM)
            p = jnp.exp(s - jnp.max(s, axis=1, keepdims=True))
            p = p / jnp.sum(p, axis=1, keepdims=True)
            dp = lax.dot_general(doh, v, _DIMS['nt'], preferred_element_type=F32)
            ds = (p * (dp - jnp.sum(p * dp, axis=1, keepdims=True)) / math.sqrt(HEAD_DIM)).astype(BF16)
            dvs.append(lax.dot_general(p.astype(BF16), doh, _DIMS['tn'], preferred_element_type=F32))
            dkn.append(lax.dot_general(ds, qb, _DIMS['tn'], preferred_element_type=F32))
            dq, g1 = _rms_bwd(q_raw, gq_ref[...], jnp.dot(ds, kb, preferred_element_type=F32))
            dqs.append(dq)
            dgq = dgq + g1
        dq_ref[...] = jnp.concatenate(dqs, axis=1).astype(BF16)
        first = i == 0
        _acc_out(dgq_ref, first, dgq)
        _acc_out(dkv_ref, first, jnp.concatenate(dkn + dvs, axis=1))

        @pl.when(i == n - 1)
        def _():
            kv = kv_ref[...]
            acc = dkv_ref[...]
            dk, gk = _heads(lambda t, d: _rms_bwd(t, gk_ref[...], d), XW // HEAD_DIM, kv[:, :XW], acc[:, :XW])
            dkv_ref[:, :XW] = dk
            dgk_ref[...] = gk

    g = _fb((1, HEAD_DIM))
    return _rows_call('xattn_bwd', body, S, tr,
                      [(cq_raw, _rb(tr, XW)), (ckv, _fb((M, 2 * XW))), (g_cq, g), (g_ck, g), (do, _rb(tr, XW))],
                      [((S, XW), BF16, _rb(tr, XW)), ((M, 2 * XW), F32, _fb((M, 2 * XW))), ((1, HEAD_DIM), F32, g),
                       ((1, HEAD_DIM), F32, g)])


def swiglu_fwd(gu, F):
    S = gu.shape[0]
    tr = _tile(S, (256, 128))
    tf = _tile(F, (1408, 1024, 512, 256, 128))
    nf = F // tf

    def body(g_ref, u_ref, a_ref):
        g = g_ref[...]
        a_ref[...] = (g * _sigmoid(g) * u_ref[...]).astype(BF16)

    return pl.pallas_call(
        body, name='swiglu_fwd', grid=(S // tr, nf),
        in_specs=[pl.BlockSpec((tr, tf), lambda i, n: (i, n)), pl.BlockSpec((tr, tf), lambda i, n: (i, n + nf))],
        out_specs=pl.BlockSpec((tr, tf), lambda i, n: (i, n)), out_shape=jax.ShapeDtypeStruct((S, F), BF16),
        compiler_params=_params(('parallel', 'parallel')))(gu, gu)


def swiglu_bwd(gu, dact, F, after):
    S = gu.shape[0]
    tr = _tile(S, (256, 128))
    tf = _tile(F, (1408, 1024, 512, 256, 128))
    nf = F // tf

    def body(g_ref, u_ref, da_ref, after_ref, o_ref):
        g, da = g_ref[...], da_ref[...]
        sg = _sigmoid(g)
        o_ref[0] = (da * u_ref[...] * sg * (1.0 + g * (1.0 - sg))).astype(BF16)
        o_ref[1] = (da * g * sg).astype(BF16)

    return pl.pallas_call(
        body, name='swiglu_bwd', grid=(S // tr, nf),
        in_specs=[pl.BlockSpec((tr, tf), lambda i, n: (i, n)), pl.BlockSpec((tr, tf), lambda i, n: (i, n + nf)),
                  pl.BlockSpec((tr, tf), lambda i, n: (i, n)), ANY],
        out_specs=pl.BlockSpec((2, tr, tf), lambda i, n: (0, i, n)), out_shape=jax.ShapeDtypeStruct((2, S, F), BF16),
        compiler_params=_params(('parallel', 'parallel')))(gu, gu, dact, after)


def loss_head(y, target):
    S, D = y.shape
    tr = _tile(S, (256, 128))

    def body(y_ref, t_ref, d_ref, db_ref, l_ref):
        err = y_ref[...] - t_ref[...]
        d = err * (1.0 / D)
        d_ref[...] = d
        db_ref[...] = d.astype(BF16)
        part = jnp.sum(jnp.sum(err * err, axis=1, keepdims=True), axis=0, keepdims=True) * (0.5 / D)
        _acc_out(l_ref, pl.program_id(0) == 0, jnp.broadcast_to(part, (1, LANES)))

    return _rows_call('loss_head', body, S, tr, [(y, _rb(tr, D)), (target, _rb(tr, D))],
                      [((S, D), F32, _rb(tr, D)), ((S, D), BF16, _rb(tr, D)), ((1, LANES), F32, _fb((1, LANES)))])


def _adamw_math(w, gv, m, v):
    mn = ADAM_B1 * m + (1.0 - ADAM_B1) * gv
    vn = ADAM_B2 * v + (1.0 - ADAM_B2) * (gv * gv)
    m_hat = mn / (1.0 - ADAM_B1 ** ADAM_STEP)
    v_hat = vn / (1.0 - ADAM_B2 ** ADAM_STEP)
    return -ADAM_LR * (m_hat / (jnp.sqrt(v_hat) + ADAM_EPS) + ADAM_WD * w), mn, vn


def adamw(name, w, g, m, v):
    R, C = w.shape
    tr = _row_tile(R, C)

    def body(w_ref, g_ref, m_ref, v_ref, d_ref, mo_ref, vo_ref):
        d_ref[...], mo_ref[...], vo_ref[...] = _adamw_math(w_ref[...], g_ref[...], m_ref[...], v_ref[...])

    spec = _rb(tr, C)
    return _rows_call(name, body, R, tr, [(w, spec), (g, spec), (m, spec), (v, spec)], [((R, C), F32, spec)] * 3)


def adamw_halves(name, w, mine, other, m, v, c_idx):
    R, C = w.shape
    hr = R // 2
    tr = _row_tile(hr, C)

    def body(c_ref, w_ref, a_ref, b_ref, m_ref, v_ref, g_ref, d_ref, mo_ref, vo_ref):
        gv = jnp.where(pl.program_id(0) == c_ref[0], a_ref[...], b_ref[...])
        g_ref[...] = gv
        d_ref[...], mo_ref[...], vo_ref[...] = _adamw_math(w_ref[...], gv, m_ref[...], v_ref[...])

    full = pl.BlockSpec((None, tr, C), lambda hh, i, c_ref: (hh, i, 0))
    half = pl.BlockSpec((tr, C), lambda hh, i, c_ref: (i, 0))
    outs = pl.pallas_call(
        body, name=name,
        grid_spec=pltpu.PrefetchScalarGridSpec(num_scalar_prefetch=1, grid=(2, hr // tr),
                                               in_specs=[full, half, half, full, full], out_specs=[full] * 4),
        out_shape=[jax.ShapeDtypeStruct((2, hr, C), F32)] * 4,
        compiler_params=_params(('parallel', 'parallel')))(
            c_idx, w.reshape(2, hr, C), mine, other, m.reshape(2, hr, C), v.reshape(2, hr, C))
    return [o.reshape(R, C) for o in outs]


def _place():
    x, y, c = lax.axis_index('x'), lax.axis_index('y'), lax.axis_index('c')
    return x, y, c, [(1 - x, y), (x, 1 - y), (1 - x, 1 - y)]


def _rcopy(src, dst, ssem, rsem, dev):
    return pltpu.make_async_remote_copy(src_ref=src, dst_ref=dst, send_sem=ssem, recv_sem=rsem, device_id=dev,
                                        device_id_type=MESH)


HBM = pl.BlockSpec(memory_space=pltpu.HBM)
SEM = pl.BlockSpec(memory_space=pltpu.SEMAPHORE)
EFFECT = pltpu.SideEffectType.DATAFLOW_SIDE_EFFECTING


def _in_hbm(a):
    return pltpu.with_memory_space_constraint(a, pltpu.HBM)


def _rows_part(shape, whole, half):
    return pl.ds(0, shape[0]) if whole else pl.ds(half * (shape[0] // 2), shape[0] // 2)


def gather_start(name, shards, whole):
    nT = len(shards)

    def body(*refs):
        srcs, lands = refs[:nT], refs[nT:2 * nT]
        ssem, rsem, token = refs[2 * nT], refs[2 * nT + 1], refs[-1]
        x, y, c, chips = _place()
        for t in range(nT):
            rows = _rows_part(shards[t].shape, whole[t], c)
            for k, (px, py) in enumerate(chips):
                _rcopy(srcs[t].at[rows], lands[t].at[2 * x + y, rows], ssem.at[3 * t + k], rsem.at[3 * t + k],
                       (px, py, c)).start()
        token[...] = jnp.zeros_like(token)

    zones = [lax.empty((N_CHIPS,) + s.shape, s.dtype) for s in shards]
    outs = pl.pallas_call(
        body, name=name,
        out_shape=(pltpu.SemaphoreType.DMA((3 * nT,)), pltpu.SemaphoreType.DMA((3 * nT,)),
                   *[pltpu.HBM(s.shape, s.dtype) for s in shards], *[pltpu.HBM(z.shape, z.dtype) for z in zones],
                   jax.ShapeDtypeStruct((8, LANES), F32)),
        in_specs=[HBM] * (2 * nT), out_specs=(SEM, SEM, *[HBM] * (2 * nT), pl.BlockSpec(memory_space=pltpu.VMEM)),
        input_output_aliases={i: 2 + i for i in range(2 * nT)},
        compiler_params=pltpu.CompilerParams(has_side_effects=EFFECT))(*[_in_hbm(a) for a in list(shards) + zones])
    return outs[0], outs[1], outs[2:2 + nT], outs[2 + nT:2 + 2 * nT], outs[-1]


def gather_wait(name, t, shard, zone, ssem, rsem, after, whole):
    def body(src_ref, land_ref, ssem_ref, rsem_ref, after_ref, src_out, land_out):
        x, y, c, chips = _place()
        rows = _rows_part(shard.shape, whole, c)
        for k, (px, py) in enumerate(chips):
            cp = _rcopy(src_ref.at[rows], land_ref.at[2 * px + py, rows], ssem_ref.at[3 * t + k], rsem_ref.at[3 * t + k],
                        (px, py, c))
            cp.wait_send()
            cp.wait_recv()

    return pl.pallas_call(
        body, name=name, out_shape=(pltpu.HBM(shard.shape, shard.dtype), pltpu.HBM(zone.shape, zone.dtype)),
        in_specs=(HBM, HBM, SEM, SEM, ANY), out_specs=(HBM, HBM), input_output_aliases={0: 0, 1: 1},
        compiler_params=pltpu.CompilerParams(has_side_effects=EFFECT))(shard, zone, ssem, rsem, after)


def pair_swap(name, zone):
    hr = zone.shape[1] // 2

    def body(z_in, z_ref, ssem, rsem):
        x, y, c, chips = _place()
        cps = []
        for k, (px, py) in enumerate(chips):
            blk = z_ref.at[2 * px + py, pl.ds(c * hr, hr)]
            cps.append(_rcopy(blk, blk, ssem.at[k], rsem.at[k], (x, y, 1 - c)))
            cps[-1].start()
        for k, (px, py) in enumerate(chips):
            blk = z_ref.at[2 * px + py, pl.ds((1 - c) * hr, hr)]
            _rcopy(blk, blk, ssem.at[k], rsem.at[k], (x, y, 1 - c)).wait_recv()
        for cp in cps:
            cp.wait_send()

    return pl.pallas_call(
        body, name=name, in_specs=[ANY], out_specs=ANY, out_shape=jax.ShapeDtypeStruct(zone.shape, zone.dtype),
        input_output_aliases={0: 0},
        scratch_shapes=[pltpu.SemaphoreType.DMA((3,)), pltpu.SemaphoreType.DMA((3,))],
        compiler_params=_params())(zone)


N_SENDERS = 7


def _scatter_copies(g_ref, l_ref, ssem, rsem):
    x, y, c, chips = _place()
    cps = []
    for k, (px, py) in enumerate(chips):
        for d in range(2):
            to = (c + d) % 2
            cps.append(_rcopy(g_ref.at[2 * px + py, to], l_ref.at[2 * k + d], ssem.at[2 * k + d], rsem.at[2 * k + d],
                              (px, py, to)))
    cps.append(_rcopy(g_ref.at[2 * x + y, 1 - c], l_ref.at[6], ssem.at[6], rsem.at[6], (x, y, 1 - c)))
    return cps


def scatter_start(name, g):
    def body(g_ref, l_ref, ssem, rsem, g_out, l_out, token):
        for cp in _scatter_copies(g_ref, l_ref, ssem, rsem):
            cp.start()
        token[...] = jnp.zeros_like(token)

    zone = lax.empty((N_SENDERS,) + g.shape[2:], g.dtype)
    return pl.pallas_call(
        body, name=name,
        out_shape=(pltpu.SemaphoreType.DMA((N_SENDERS,)), pltpu.SemaphoreType.DMA((N_SENDERS,)),
                   pltpu.HBM(g.shape, g.dtype), pltpu.HBM(zone.shape, zone.dtype), jax.ShapeDtypeStruct((8, LANES), F32)),
        in_specs=[HBM, HBM], out_specs=(SEM, SEM, HBM, HBM, pl.BlockSpec(memory_space=pltpu.VMEM)),
        input_output_aliases={0: 2, 1: 3},
        compiler_params=pltpu.CompilerParams(has_side_effects=EFFECT))(_in_hbm(g), _in_hbm(zone))


def scatter_wait(name, g, zone, ssem, rsem, after):
    def body(g_ref, l_ref, ssem_ref, rsem_ref, after_ref, g_out, l_out):
        for cp in _scatter_copies(g_ref, l_ref, ssem_ref, rsem_ref):
            cp.wait_send()
            cp.wait_recv()

    return pl.pallas_call(
        body, name=name, out_shape=(pltpu.HBM(g.shape, g.dtype), pltpu.HBM(zone.shape, zone.dtype)),
        in_specs=(HBM, HBM, SEM, SEM, ANY), out_specs=(HBM, HBM), input_output_aliases={0: 0, 1: 1},
        compiler_params=pltpu.CompilerParams(has_side_effects=EFFECT))(g, zone, ssem, rsem, after)


def sum_parts(name, g, landed, chip_idx, c_idx):
    hr, C = g.shape[2:]
    tr = _row_tile(hr, C, min_rows=16)

    def body(me_ref, c_ref, g_ref, l_ref, o_ref):
        acc = g_ref[...].astype(F32)
        for s in range(N_SENDERS):
            acc = acc + l_ref[s].astype(F32)
        o_ref[...] = acc

    return pl.pallas_call(
        body, name=name,
        grid_spec=pltpu.PrefetchScalarGridSpec(
            num_scalar_prefetch=2, grid=(hr // tr,),
            in_specs=[pl.BlockSpec((None, None, tr, C), lambda i, me_ref, c_ref: (me_ref[0], c_ref[0], i, 0)),
                      pl.BlockSpec((N_SENDERS, tr, C), lambda i, me_ref, c_ref: (0, i, 0))],
            out_specs=pl.BlockSpec((tr, C), lambda i, me_ref, c_ref: (i, 0))),
        out_shape=jax.ShapeDtypeStruct((hr, C), F32),
        compiler_params=_params(('parallel',)))(chip_idx, c_idx, g, landed)


def pair_join(name, halves):
    nT = len(halves)

    def body(*refs):
        ins, outs = refs[:nT], refs[nT:2 * nT]
        ssem, rsem = refs[2 * nT:]
        x, y, c, _ = _place()
        cps = [_rcopy(ins[t], outs[t], ssem.at[t], rsem.at[t], (x, y, 1 - c)) for t in range(nT)]
        for cp in cps:
            cp.start()
        for cp in cps:
            cp.wait()

    return pl.pallas_call(
        body, name=name, in_specs=[ANY] * nT, out_specs=[ANY] * nT,
        out_shape=[jax.ShapeDtypeStruct(h.shape, h.dtype) for h in halves],
        scratch_shapes=[pltpu.SemaphoreType.DMA((nT,)), pltpu.SemaphoreType.DMA((nT,))],
        compiler_params=_params())(*halves)


def allreduce_small(buf, after):
    R = buf.shape[0]
    VM = pl.BlockSpec(memory_space=pltpu.VMEM)

    def body(x_ref, after_ref, o_ref, all_ref, ssem, rsem, lsem):
        x, y, c, chips = _place()
        me, sibling = (x, y, c), (x, y, 1 - c)

        def rows(px, py, pc):
            return all_ref.at[pl.ds((4 * px + 2 * py + pc) * R, R), :]

        def copy(k, block, to, src=None):
            return _rcopy(rows(*block) if src is None else src, rows(*block), ssem.at[k], rsem.at[k], to)

        mine = pltpu.make_async_copy(x_ref, rows(*me), lsem)
        mine.start()
        first = [copy(0, me, sibling, src=x_ref)]
        first += [copy(1 + k, me, (*chip, c), src=x_ref) for k, chip in enumerate(chips)]
        for cp in first:
            cp.start()
        passed = [copy(4 + k, (*chip, c), sibling) for k, chip in enumerate(chips)]
        for k, chip in enumerate(chips):
            copy(1 + k, (*chip, c), me).wait_recv()
            passed[k].start()
        copy(0, sibling, me).wait_recv()
        for k, chip in enumerate(chips):
            copy(4 + k, (*chip, 1 - c), me).wait_recv()
        for cp in first + passed:
            cp.wait_send()
        mine.wait()
        acc = all_ref[0:R, :]
        for d in range(1, 8):
            acc = acc + all_ref[d * R:(d + 1) * R, :]
        o_ref[...] = acc

    return pl.pallas_call(
        body, name='allreduce_small', in_specs=[VM, ANY], out_specs=VM, out_shape=jax.ShapeDtypeStruct((R, LANES), F32),
        scratch_shapes=[pltpu.VMEM((8 * R, LANES), F32), pltpu.SemaphoreType.DMA((7,)), pltpu.SemaphoreType.DMA((7,)),
                        pltpu.SemaphoreType.DMA],
        compiler_params=_params())(buf, after)


_PACK = 8 * LANES


def _pack(arrs):
    flat = []
    for a in arrs:
        v = a.reshape(-1).astype(F32)
        flat.append(jnp.pad(v, (0, (-v.shape[0]) % _PACK)))
    return jnp.concatenate(flat).reshape(-1, LANES)


def _unpack(buf, shapes):
    out, off = [], 0
    flat = buf.reshape(-1)
    for sh in shapes:
        n = math.prod(sh)
        out.append(flat[off:off + n].reshape(sh))
        off += n + (-n) % _PACK
    return out


def kernel(x, mem, g_mix, w_in, b_f, g_q, g_k, conv_w, conv_b, w_ra, b_ra, w_ri, b_ri, lam, g_fox_out, g_lru_out, w_out, g_xattn, g_mem, w_cq, w_ckv, g_cq, g_ck, w_co, g_ffn, w_gate_up, w_down, loss_target, m_g_mix, m_w_in, m_b_f, m_g_q, m_g_k, m_conv_w, m_conv_b, m_w_ra, m_b_ra, m_w_ri, m_b_ri, m_lam, m_g_fox_out, m_g_lru_out, m_w_out, m_g_xattn, m_g_mem, m_w_cq, m_w_ckv, m_g_cq, m_g_ck, m_w_co, m_g_ffn, m_w_gate_up, m_w_down, v_g_mix, v_w_in, v_b_f, v_g_q, v_g_k, v_conv_w, v_conv_b, v_w_ra, v_b_ra, v_w_ri, v_b_ri, v_lam, v_g_fox_out, v_g_lru_out, v_w_out, v_g_xattn, v_g_mem, v_w_cq, v_w_ckv, v_g_cq, v_g_ck, v_w_co, v_g_ffn, v_w_gate_up, v_w_down):
    given = dict(locals())
    W = {n: given[n][0] for n in WEIGHTS}
    M1 = {n: given['m_' + n][0] for n in WEIGHTS}
    V1 = {n: given['v_' + n][0] for n in WEIGHTS}
    xs, ms, tgt = x[0], mem[0], loss_target[0]
    S, D = xs.shape
    H = W['b_f'].shape[0]
    FW = H * HEAD_DIM
    LW = W['lam'].shape[0]
    nb = W['w_ra'].shape[0]
    XW = W['w_cq'].shape[1]
    F = W['w_down'].shape[0] * N_CHIPS
    IN_W = W['w_in'].shape[1] * N_CHIPS
    assert FW == LW and LW == nb * LANES and IN_W == 3 * FW + H + 2 * LW and H <= 8
    T = _tile(S, (512, 256, 128))
    c_idx = lax.axis_index('c').astype(jnp.int32).reshape(1)
    chip = 2 * lax.axis_index('x') + lax.axis_index('y')
    chip_idx = chip.astype(jnp.int32).reshape(1)
    vec = lambda n: W[n].reshape(1, -1)

    started = {}
    g_tok = jnp.zeros((1, 1), F32)
    for call, names in (('gather_start_first', ['conv_w', 'w_in']), ('gather_start_rest', BIG[1:])):
        own = [W[n].reshape(-1, LANES) if n == 'conv_w' else W[n].astype(BF16) + g_tok.astype(BF16) for n in names]
        ssem, rsem, srcs, zones, tok = gather_start(call, own, [n == 'conv_w' for n in names])
        g_tok = tok[0:1, 0:1]
        started.update({n: (t, srcs[t], zones[t], ssem, rsem) for t, n in enumerate(names)})

    def fetch(n, after):
        t, g_src, g_zone, g_ssem, g_rsem = started[n]
        src, zone = gather_wait('gather_wait_' + n, t, g_src, g_zone, g_ssem, g_rsem, after, n == 'conv_w')
        if n != 'conv_w':
            zone = pair_swap('pair_swap_' + n, zone)
        return lax.dynamic_update_index_in_dim(zone, src, chip, 0)

    b_f_pad = jnp.pad(vec('b_f'), ((0, 0), (0, LANES - H)))
    u_off, g_off = 3 * FW // LANES, (3 * FW + LW) // LANES

    h1 = norm_fwd('norm_mix', xs, vec('g_mix') + g_tok[0:1, 0:1])
    conv_full = fetch('conv_w', h1).reshape(N_CHIPS, CONV_W, LW // N_CHIPS).transpose(1, 0, 2).reshape(CONV_W, LW)
    w_in_full = fetch('w_in', h1).transpose(1, 0, 2).reshape(D, IN_W)
    w5 = jnp.concatenate([w_in_full[:, :3 * FW], w_in_full[:, 3 * FW + H:]], axis=1)
    wf = jnp.pad(w_in_full[:, 3 * FW:3 * FW + H], ((0, 0), (0, LANES - H)))
    proj = _mm('proj_in', h1, w5, 'nn', F32)
    f_raw = _mm('proj_f', h1, wf, 'nn', F32)
    qn, kn, vb = qkv_fwd(proj, vec('g_q'), vec('g_k'), FW)
    cc = fgate_fwd(f_raw, b_f_pad)
    ct = cc[:, :8].T
    o_fox, lse = fox_fwd(qn, kn, vb, cc, ct, T)
    lru_w = (conv_full, vec('conv_b'), W['w_ra'], vec('b_ra'), W['w_ri'], vec('b_ri'), vec('lam'))
    y_lru = lru_fwd(proj, *lru_w, u_off, g_off)
    mixn = mix_fwd(o_fox, y_lru, vec('g_fox_out'), vec('g_lru_out'))
    w_out_f = fetch('w_out', mixn).reshape(2 * FW, D)
    x1 = _mm('proj_out', mixn, w_out_f, 'nn', F32, res=xs)

    hq = norm_fwd('norm_xq', x1, vec('g_xattn'))
    mn = norm_fwd('norm_mem', ms, vec('g_mem'))
    w_cq_f = fetch('w_cq', hq).reshape(D, XW)
    w_ckv_f = fetch('w_ckv', hq).reshape(D, 2 * XW)
    cq_raw = _mm('proj_cq', hq, w_cq_f, 'nn', F32)
    ckv = _mm('proj_ckv', mn, w_ckv_f, 'nn', F32)
    o_x = xattn_fwd(cq_raw, ckv, vec('g_cq'), vec('g_ck'))
    w_co_g = fetch('w_co', o_x)
    x2 = _mm_colsharded('proj_co', o_x, w_co_g, F32, res=x1)

    hf = norm_fwd('norm_ffn', x2, vec('g_ffn'))
    w_gu_g = fetch('w_gate_up', hf)
    gu = _mm_colsharded('proj_gate_up', hf, w_gu_g, F32)
    act = swiglu_fwd(gu, F)
    w_down_f = fetch('w_down', act).reshape(F, D)
    yv = _mm('proj_down', act, w_down_f, 'nn', F32, res=x2)
    dy, dyb, loss_blk = loss_head(yv, tgt)

    gw, pending = {}, []

    def reduce_begin(n, g):
        sp = g.reshape(N_CHIPS, 2, g.shape[1] // 2, g.shape[2])
        ssem, rsem, sp, zone, tok = scatter_start('scatter_start_' + n, sp)
        pending.append((n, sp, zone, ssem, rsem))
        return tok[0:1, 0:1]

    dact = _mm('bwd_down_x', dyb, w_down_f, 'nt', F32)
    t_down = reduce_begin('w_down', _mm('bwd_down_w', act, dyb, 'tn', BF16).reshape(N_CHIPS, F // N_CHIPS, D))
    dgu = swiglu_bwd(gu, dact, F, t_down)
    dhf = _mm_colsharded_t('bwd_gate_up_x', dgu, w_gu_g, F32)
    t_gu = reduce_begin('w_gate_up', _mm_grad_colsharded('bwd_gate_up_w', hf, dgu, N_CHIPS, BF16))
    dx2, dx2b, gw['g_ffn'] = norm_bwd('norm_ffn_bwd', x2, vec('g_ffn') + t_down + t_gu, dhf, res=dy)

    do_x = _mm_colsharded_t('bwd_co_x', dx2b, w_co_g, BF16)
    t_co = reduce_begin('w_co', _mm_grad_colsharded('bwd_co_w', o_x, dx2b, N_CHIPS, BF16))
    dcq_raw, dckv, gw['g_cq'], gw['g_ck'] = xattn_bwd(cq_raw, ckv, vec('g_cq') + t_co, vec('g_ck'), do_x)
    dhq = _mm('bwd_cq_x', dcq_raw, w_cq_f, 'nt', F32)
    t_cq = reduce_begin('w_cq', _mm('bwd_cq_w', hq, dcq_raw, 'tn', BF16).reshape(N_CHIPS, D // N_CHIPS, XW))
    dmn = _mm('bwd_ckv_x', dckv, w_ckv_f, 'nt', F32)
    t_ckv = reduce_begin('w_ckv', _mm('bwd_ckv_w', mn, dckv, 'tn', BF16).reshape(N_CHIPS, D // N_CHIPS, 2 * XW))
    (gw['g_mem'],) = norm_bwd('norm_mem_bwd', ms, vec('g_mem'), dmn, want_dx=False)
    dx1, dx1b, gw['g_xattn'] = norm_bwd('norm_xq_bwd', x1, vec('g_xattn') + t_cq + t_ckv, dhq, res=dx2)

    dmix = _mm('bwd_out_x', dx1b, w_out_f, 'nt', F32)
    t_out = reduce_begin('w_out', _mm('bwd_out_w', mixn, dx1b, 'tn', BF16).reshape(N_CHIPS, 2 * FW // N_CHIPS, D))
    do_fox, delta, dy_lru, gw['g_fox_out'], gw['g_lru_out'] = mix_bwd(o_fox, y_lru, vec('g_fox_out') + t_out,
                                                                     vec('g_lru_out'), dmix)
    (du, dgate, gw['conv_w'], gw['conv_b'], gw['w_ra'], gw['b_ra'], gw['w_ri'], gw['b_ri'],
     gw['lam']) = lru_bwd(proj, dy_lru, *lru_w, u_off, g_off)
    dqn, delta2 = fox_bwd_q(qn, kn, vb, do_fox, cc, ct, lse, delta, T)
    dkn, dv, dct = fox_bwd_kv(qn, kn, vb, do_fox, cc, ct, lse, delta2, T)
    dq, dk, gw['g_q'], gw['g_k'] = qkv_bwd(proj, vec('g_q'), vec('g_k'), dqn, dkn, FW)
    dc = jnp.pad(dct.reshape(H, S).T, ((0, 0), (0, LANES - H)))
    df, db_f = fgate_bwd(f_raw, b_f_pad, dc, H)
    gw['b_f'] = db_f[:, :H]
    dproj = jnp.concatenate([dq, dk, dv, du, dgate], axis=1)
    dw5 = _mm('bwd_in_w', h1, dproj, 'tn', BF16)
    dwf = _mm('bwd_f_w', h1, df, 'tn', BF16)
    dw_in = jnp.concatenate([dw5[:, :3 * FW], dwf[:, :H], dw5[:, 3 * FW:]], axis=1)
    t_in = reduce_begin('w_in', dw_in.reshape(D, N_CHIPS, IN_W // N_CHIPS).transpose(1, 0, 2))
    dh_a = _mm('bwd_f_x', df, wf, 'nt', F32)
    dh1 = _mm('bwd_in_x', dproj, w5, 'nt', F32, res=dh_a)
    grad_x, _, gw['g_mix'] = norm_bwd('norm_mix_bwd', xs, vec('g_mix') + t_in, dh1, res=dx1)

    grads, delta_w, new_m, new_v = {}, {}, {}, {}
    done = grad_x
    for n, part, zone, ssem, rsem in pending:
        part, landed = scatter_wait('scatter_wait_' + n, part, zone, ssem, rsem, done)
        mine = sum_parts('sum_parts_' + n, part, landed, chip_idx, c_idx)
        (other,) = pair_join('pair_join_' + n, [mine])
        grads[n], delta_w[n], new_m[n], new_v[n] = adamw_halves('adamw_' + n, W[n], mine, other, M1[n], V1[n], c_idx)
        done = delta_w[n]

    small_shapes = [gw[n].shape for n in SMALL] + [(1, 1)]
    summed = _unpack(allreduce_small(_pack([gw[n] for n in SMALL] + [loss_blk[0:1, 0:1]]), delta_w[BIG[0]]), small_shapes)
    loss = summed[-1].reshape(())
    for n, g in zip(SMALL, summed):
        grads[n] = g.reshape(W[n].shape) if n != 'conv_w' else lax.dynamic_slice_in_dim(
            g, chip * (LW // N_CHIPS), LW // N_CHIPS, axis=1)
    packs = [_pack([d[n] for n in SMALL]) for d in (W, grads, M1, V1)]
    shapes = [W[n].shape for n in SMALL]
    for d, res in zip((delta_w, new_m, new_v), adamw('adamw_small', *packs)):
        d.update(zip(SMALL, _unpack(res, shapes)))

    lead = lambda d: [d[n][None] for n in WEIGHTS]
    return (loss, grad_x[None], *lead(grads), *lead(delta_w), *lead(new_m), *lead(new_v))
```

```python
import functools
import math

import jax
import jax.numpy as jnp
from jax import lax
from jax.experimental import pallas as pl
from jax.experimental.pallas import tpu as pltpu

F32 = jnp.float32
BF16 = jnp.bfloat16
HEAD_DIM = 128
LANES = 128
LRU_C = 8.0
RMS_EPS = 1e-6
CONV_W = 4
ADAM_LR = 0.001
ADAM_B1 = 0.9
ADAM_B2 = 0.999
ADAM_EPS = 1e-08
ADAM_WD = 0.01
ADAM_STEP = 10
VMEM_LIMIT = 56 * 1024 * 1024
N_CHIPS = 4
MESH = pl.DeviceIdType.MESH
ANY = pl.BlockSpec(memory_space=pl.ANY)

WEIGHTS = ['g_mix', 'w_in', 'b_f', 'g_q', 'g_k', 'conv_w', 'conv_b', 'w_ra', 'b_ra', 'w_ri', 'b_ri', 'lam',
           'g_fox_out', 'g_lru_out', 'w_out', 'g_xattn', 'g_mem', 'w_cq', 'w_ckv', 'g_cq', 'g_ck', 'w_co', 'g_ffn',
           'w_gate_up', 'w_down']
BIG = ['w_in', 'w_out', 'w_cq', 'w_ckv', 'w_co', 'w_gate_up', 'w_down']
SMALL = [n for n in WEIGHTS if n not in BIG]


def _params(sem=None):
    if sem is None:
        return pltpu.CompilerParams(vmem_limit_bytes=VMEM_LIMIT)
    return pltpu.CompilerParams(dimension_semantics=sem, vmem_limit_bytes=VMEM_LIMIT)


def _tile(n, cands):
    for t in cands:
        if n % t == 0:
            return t
    return n


ROW_BLOCK_BYTES = 1 << 20


def _row_tile(n_rows, n_cols, min_rows=8):
    cands = [t for t in (512, 256, 128, 64, 32, 16, 8) if t >= min_rows and t * n_cols * 4 <= ROW_BLOCK_BYTES]
    return _tile(n_rows, cands or [min_rows])


def _sigmoid(z):
    return 1.0 / (1.0 + jnp.exp(-z))


def _softplus(z):
    return jnp.maximum(z, 0.0) + jnp.log(1.0 + jnp.exp(-jnp.abs(z)))


def _neg_expm1(z):
    series = -z * (1.0 + z * (0.5 + z * (1.0 / 6.0 + z * (1.0 / 24.0 + z * (1.0 / 120.0)))))
    return jnp.where(z > -0.25, series, 1.0 - jnp.exp(z))


_GELU_K = math.sqrt(2.0 / math.pi)


def _gelu_and_grad(z):
    inner = _GELU_K * (z + 0.044715 * z * z * z)
    t = jnp.tanh(inner)
    g = 0.5 * z * (1.0 + t)
    dg = 0.5 * (1.0 + t) + 0.5 * z * (1.0 - t * t) * _GELU_K * (1.0 + 3.0 * 0.044715 * z * z)
    return g, dg


def _rms(xv, g):
    r = lax.rsqrt(jnp.mean(xv * xv, axis=-1, keepdims=True) + RMS_EPS)
    return xv * r * g


def _rms_bwd(xv, g, dy):
    r = lax.rsqrt(jnp.mean(xv * xv, axis=-1, keepdims=True) + RMS_EPS)
    xh = xv * r
    dyg = dy * g
    dx = r * (dyg - xh * jnp.mean(dyg * xh, axis=-1, keepdims=True))
    return dx, jnp.sum(dy * xh, axis=0, keepdims=True)


def _heads(fn, n_heads, *arrs):
    outs = [fn(*[a[:, h * HEAD_DIM:(h + 1) * HEAD_DIM] for a in arrs]) for h in range(n_heads)]
    first = jnp.concatenate([o[0] for o in outs], axis=1) if n_heads > 1 else outs[0][0]
    rest = [functools.reduce(lambda p, q: p + q, [o[i] for o in outs]) for i in range(1, len(outs[0]))]
    return (first, *rest)


def _split3(v):
    hi = v.astype(BF16)
    r1 = v - hi.astype(F32)
    mid = r1.astype(BF16)
    lo = (r1 - mid.astype(F32)).astype(BF16)
    return hi, mid, lo


def _acc_out(ref, first, val):
    @pl.when(first)
    def _():
        ref[...] = val

    @pl.when(jnp.logical_not(first))
    def _():
        ref[...] += val


_DIMS = {'nn': (((1,), (0,)), ((), ())), 'nt': (((1,), (1,)), ((), ())), 'tn': (((0,), (0,)), ((), ()))}


MM_VMEM_BYTES = 36 * 1024 * 1024


MXU_FLOPS = 800e12
HBM_BYTES_S = 3.2e12
VMEM_ADD_BYTES_S = 8e12
STEP_S = 0.35e-6


def _k_tile(K, tm, tn, a, b, o_dtype, res):
    fixed = tm * tn * (2 * jnp.dtype(o_dtype).itemsize + 4 + (8 if res is not None else 0))
    per_k = 2 * (tm * a.dtype.itemsize + tn * b.dtype.itemsize)
    per_k += 2 * tm * (a.dtype.itemsize > 2) + 2 * tn * (b.dtype.itemsize > 2)
    units = K // LANES
    for d in sorted((d for d in range(1, units + 1) if units % d == 0), reverse=True):
        if fixed + d * LANES * per_k <= MM_VMEM_BYTES:
            return d * LANES
    return None


def _mm_tiles(M, N, K, k_span, a, b, o_dtype, res, tn_cands=(2048, 1024, 512, 256, 128)):
    best = None
    for tm in (2048, 1024, 512, 256, 128):
        for tn in tn_cands:
            if M % tm or N % tn:
                continue
            tk = _k_tile(k_span, tm, tn, a, b, o_dtype, res)
            if tk is None:
                continue
            nk = K // tk
            traffic = (M * K * a.dtype.itemsize * (N // tn) + K * N * b.dtype.itemsize * (M // tm)
                       + M * N * (jnp.dtype(o_dtype).itemsize + (4 if res is not None else 0)))
            work = 2.0 * M * N * K / MXU_FLOPS + (M * N * 4 * nk / VMEM_ADD_BYTES_S if nk > 1 else 0.0)
            t = max(work, traffic / HBM_BYTES_S) + (M // tm) * (N // tn) * nk * STEP_S
            if best is None or t < best[0]:
                best = (t, tm, tn, tk)
    assert best is not None, (M, N, K)
    return best[1:]


def _mm_call(name, a, b, mode, grid, a_spec, b_spec, o_spec, o_shape, o_dtype, acc_shape, res=None):
    nk = grid[2]
    dn = _DIMS[mode]

    def body(*refs):
        a_ref, b_ref = refs[:2]
        r_ref = refs[2] if res is not None else None
        o_ref = refs[3] if res is not None else refs[2]
        part = lax.dot_general(a_ref[...].astype(BF16), b_ref[...].astype(BF16), dn, preferred_element_type=F32)

        def finish(r):
            if r_ref is not None:
                r = r + r_ref[...]
            o_ref[...] = r.astype(o_dtype)

        if nk == 1:
            finish(part)
            return
        acc = refs[-1]
        k = pl.program_id(2)

        @pl.when(k == 0)
        def _():
            acc[...] = part

        @pl.when(k > 0)
        def _():
            acc[...] += part

        @pl.when(k == nk - 1)
        def _():
            finish(acc[...])

    ins = [a, b] + ([] if res is None else [res])
    specs = [a_spec, b_spec] + ([] if res is None else [o_spec])
    return pl.pallas_call(
        body, name=name, grid=grid, in_specs=specs, out_specs=o_spec,
        out_shape=jax.ShapeDtypeStruct(o_shape, o_dtype),
        scratch_shapes=[] if nk == 1 else [pltpu.VMEM(acc_shape, F32)],
        compiler_params=_params(('parallel', 'parallel', 'arbitrary')))(*ins)


def _mm(name, a, b, mode, o_dtype, res=None):
    if mode == 'tn':
        K, M = a.shape
    else:
        M, K = a.shape
    N = b.shape[0] if mode == 'nt' else b.shape[1]
    tm, tn, tk = _mm_tiles(M, N, K, K, a, b, o_dtype, res)
    a_spec = (pl.BlockSpec((tk, tm), lambda m, n, k: (k, m)) if mode == 'tn'
              else pl.BlockSpec((tm, tk), lambda m, n, k: (m, k)))
    b_spec = (pl.BlockSpec((tn, tk), lambda m, n, k: (n, k)) if mode == 'nt'
              else pl.BlockSpec((tk, tn), lambda m, n, k: (k, n)))
    o_spec = pl.BlockSpec((tm, tn), lambda m, n, k: (m, n))
    return _mm_call(name, a, b, mode, (M // tm, N // tn, K // tk), a_spec, b_spec, o_spec, (M, N), o_dtype,
                    (tm, tn), res)


def _mm_colsharded(name, a, w, o_dtype, res=None):
    M, K = a.shape
    J, _, Nj = w.shape
    tm, tn, tk = _mm_tiles(M, J * Nj, K, K, a, w, o_dtype, res,
                           tn_cands=[t for t in (2816, 1408, 1024, 512, 256, 128) if Nj % t == 0])
    per = Nj // tn
    return _mm_call(name, a, w, 'nn', (M // tm, J * per, K // tk),
                    pl.BlockSpec((tm, tk), lambda m, n, k: (m, k)),
                    pl.BlockSpec((None, tk, tn), lambda m, n, k: (n // per, k, n % per)),
                    pl.BlockSpec((tm, tn), lambda m, n, k: (m, n)), (M, J * Nj), o_dtype, (tm, tn), res)


def _planes_spec(arr, rows, cols, row_of, col_of):
    if arr.ndim == 2:
        return pl.BlockSpec((rows, cols), lambda m, n, k: (row_of(m, n, k), col_of(m, n, k)))
    per_plane = arr.shape[2] // cols
    return pl.BlockSpec((None, rows, cols),
                        lambda m, n, k: (col_of(m, n, k) // per_plane, row_of(m, n, k), col_of(m, n, k) % per_plane))


def _mm_colsharded_t(name, a, w, o_dtype):
    M = a.shape[-2]
    J, K, Nj = w.shape
    tm, tn, tk = _mm_tiles(M, K, J * Nj, Nj, a, w, o_dtype, None)
    per = Nj // tk
    return _mm_call(name, a, w, 'nt', (M // tm, K // tn, J * per),
                    _planes_spec(a, tm, tk, lambda m, n, k: m, lambda m, n, k: k),
                    pl.BlockSpec((None, tn, tk), lambda m, n, k: (k // per, n, k % per)),
                    pl.BlockSpec((tm, tn), lambda m, n, k: (m, n)), (M, K), o_dtype, (tm, tn))


def _mm_grad_colsharded(name, a, dy, J, o_dtype):
    S, M = a.shape
    Nj = dy.shape[-1] * (dy.shape[0] if dy.ndim == 3 else 1) // J
    tm, tn, tk = _mm_tiles(M, J * Nj, S, S, a, dy, o_dtype, None,
                           tn_cands=[t for t in (2816, 1408, 1024, 512, 256, 128) if Nj % t == 0])
    per = Nj // tn
    return _mm_call(name, a, dy, 'tn', (M // tm, J * per, S // tk),
                    pl.BlockSpec((tk, tm), lambda m, n, k: (k, m)),
                    _planes_spec(dy, tk, tn, lambda m, n, k: k, lambda m, n, k: n),
                    pl.BlockSpec((None, tm, tn), lambda m, n, k: (n // per, m, n % per)), (J, M, Nj), o_dtype, (tm, tn))


def _rows_call(name, body, n_rows, tr, ins, outs):
    return pl.pallas_call(
        body, name=name, grid=(n_rows // tr,), in_specs=[s for _, s in ins], out_specs=[s for _, _, s in outs],
        out_shape=[jax.ShapeDtypeStruct(sh, dt) for sh, dt, _ in outs],
        compiler_params=_params(('arbitrary',)))(*[a for a, _ in ins])


def _rb(tr, w, cb=0):
    return pl.BlockSpec((tr, w), lambda i: (i, cb))


def _fb(shape):
    nd = len(shape)
    return pl.BlockSpec(shape, lambda i: (0,) * nd)


def norm_fwd(name, xv, g):
    S, D = xv.shape
    tr = _tile(S, (256, 128))

    def body(x_ref, g_ref, o_ref):
        o_ref[...] = _rms(x_ref[...], g_ref[...]).astype(BF16)

    return _rows_call(name, body, S, tr, [(xv, _rb(tr, D)), (g, _fb((1, D)))], [((S, D), BF16, _rb(tr, D))])[0]


def norm_bwd(name, xv, g, dy, res=None, want_dx=True):
    S, D = xv.shape
    tr = _tile(S, (256, 128))

    def body(*refs):
        if res is None:
            x_ref, g_ref, dy_ref = refs[:3]
            outs = refs[3:]
            r_ref = None
        else:
            x_ref, g_ref, dy_ref, r_ref = refs[:4]
            outs = refs[4:]
        dx, dg = _rms_bwd(x_ref[...], g_ref[...], dy_ref[...])
        if r_ref is not None:
            dx = dx + r_ref[...]
        if want_dx:
            outs[0][...] = dx
            outs[1][...] = dx.astype(BF16)
        _acc_out(outs[-1], pl.program_id(0) == 0, dg)

    ins = [(xv, _rb(tr, D)), (g, _fb((1, D))), (dy, _rb(tr, D))] + ([] if res is None else [(res, _rb(tr, D))])
    outs = ([((S, D), F32, _rb(tr, D)), ((S, D), BF16, _rb(tr, D))] if want_dx else []) + [((1, D), F32, _fb((1, D)))]
    return _rows_call(name, body, S, tr, ins, outs)


def qkv_fwd(proj, g_q, g_k, FW):
    S = proj.shape[0]
    H = FW // HEAD_DIM
    tr = _tile(S, (256, 128))

    def body(q_ref, k_ref, v_ref, gq_ref, gk_ref, qo, ko, vo):
        qo[...] = _heads(lambda t: (_rms(t, gq_ref[...]),), H, q_ref[...])[0].astype(BF16)
        ko[...] = _heads(lambda t: (_rms(t, gk_ref[...]),), H, k_ref[...])[0].astype(BF16)
        vo[...] = v_ref[...].astype(BF16)

    o = ((S, FW), BF16, _rb(tr, FW))
    return _rows_call('qkv_fwd', body, S, tr,
                      [(proj, _rb(tr, FW, 0)), (proj, _rb(tr, FW, 1)), (proj, _rb(tr, FW, 2)),
                       (g_q, _fb((1, HEAD_DIM))), (g_k, _fb((1, HEAD_DIM)))], [o, o, o])


def qkv_bwd(proj, g_q, g_k, dqn, dkn, FW):
    S = proj.shape[0]
    H = FW // HEAD_DIM
    tr = _tile(S, (256, 128))

    def body(q_ref, k_ref, gq_ref, gk_ref, dq_ref, dk_ref, dqo, dko, dgq, dgk):
        dq, gq = _heads(lambda t, d: _rms_bwd(t, gq_ref[...], d), H, q_ref[...], dq_ref[...])
        dk, gk = _heads(lambda t, d: _rms_bwd(t, gk_ref[...], d), H, k_ref[...], dk_ref[...])
        dqo[...] = dq.astype(BF16)
        dko[...] = dk.astype(BF16)
        first = pl.program_id(0) == 0
        _acc_out(dgq, first, gq)
        _acc_out(dgk, first, gk)

    o = ((S, FW), BF16, _rb(tr, FW))
    og = ((1, HEAD_DIM), F32, _fb((1, HEAD_DIM)))
    return _rows_call('qkv_bwd', body, S, tr,
                      [(proj, _rb(tr, FW, 0)), (proj, _rb(tr, FW, 1)), (g_q, _fb((1, HEAD_DIM))),
                       (g_k, _fb((1, HEAD_DIM))), (dqn, _rb(tr, FW)), (dkn, _rb(tr, FW))], [o, o, og, og])


def _tri(n, upper):
    r = lax.broadcasted_iota(jnp.int32, (n, n), 0)
    c = lax.broadcasted_iota(jnp.int32, (n, n), 1)
    return jnp.where((c >= r) if upper else (c <= r), 1.0, 0.0).astype(BF16)


def _blocked_cumsum(val, S, blk, reverse):
    tri = _tri(blk, reverse)
    order = range(S // blk - 1, -1, -1) if reverse else range(S // blk)
    carry = jnp.zeros((1, LANES), F32)
    outs = {}
    for bi in order:
        part = val[bi * blk:(bi + 1) * blk]
        acc = carry
        for piece in _split3(part):
            acc = acc + jnp.dot(tri, piece, preferred_element_type=F32)
        outs[bi] = acc
        carry = carry + jnp.sum(part, axis=0, keepdims=True)
    return jnp.concatenate([outs[bi] for bi in range(S // blk)], axis=0)


def fgate_fwd(f_raw, b_f_pad):
    S = f_raw.shape[0]
    blk = _tile(S, (256, 128))

    def body(f_ref, b_ref, c_ref):
        z = f_ref[...] + b_ref[...]
        c_ref[...] = _blocked_cumsum(-_softplus(-z), S, blk, False)

    return pl.pallas_call(body, name='fgate_fwd', grid=(1,), in_specs=[_fb((S, LANES)), _fb((1, LANES))],
                          out_specs=_fb((S, LANES)), out_shape=jax.ShapeDtypeStruct((S, LANES), F32),
                          compiler_params=_params(('arbitrary',)))(f_raw, b_f_pad)


def fgate_bwd(f_raw, b_f_pad, dc, H):
    S = f_raw.shape[0]
    blk = _tile(S, (256, 128))

    def body(f_ref, b_ref, dc_ref, df_ref, db_ref):
        z = f_ref[...] + b_ref[...]
        dlogf = _blocked_cumsum(dc_ref[...], S, blk, True)
        lane = lax.broadcasted_iota(jnp.int32, (S, LANES), 1)
        df = jnp.where(lane < H, dlogf * _sigmoid(-z), 0.0)
        df_ref[...] = df.astype(BF16)
        db_ref[...] = jnp.sum(df, axis=0, keepdims=True)

    return pl.pallas_call(body, name='fgate_bwd', grid=(1,),
                          in_specs=[_fb((S, LANES)), _fb((1, LANES)), _fb((S, LANES))],
                          out_specs=[_fb((S, LANES)), _fb((1, LANES))],
                          out_shape=[jax.ShapeDtypeStruct((S, LANES), BF16), jax.ShapeDtypeStruct((1, LANES), F32)],
                          compiler_params=_params(('arbitrary',)))(f_raw, b_f_pad, dc)


def _fox_logits(q, k, c_blk, ct_blk, h, i, j, T):
    s = lax.dot_general(q, k, _DIMS['nt'], preferred_element_type=F32) * (1.0 / math.sqrt(HEAD_DIM))
    lane = lax.broadcasted_iota(jnp.int32, c_blk.shape, 1)
    cq = jnp.sum(jnp.where(lane == h, c_blk, 0.0), axis=1, keepdims=True)
    sub = lax.broadcasted_iota(jnp.int32, ct_blk.shape, 0)
    ck = jnp.sum(jnp.where(sub == h, ct_blk, 0.0), axis=0, keepdims=True)
    rows = i * T + lax.broadcasted_iota(jnp.int32, (T, T), 0)
    cols = j * T + lax.broadcasted_iota(jnp.int32, (T, T), 1)
    return jnp.where(cols <= rows, s + cq - ck, -jnp.inf)


def fox_fwd(qn, kn, vb, c, ct, T):
    S, FW = qn.shape
    H = FW // HEAD_DIM
    Hp = ct.shape[0]
    n = S // T

    def body(q_ref, k_ref, v_ref, c_ref, ct_ref, o_ref, lse_ref, m_s, l_s, acc_s):
        h, i, j = pl.program_id(0), pl.program_id(1), pl.program_id(2)

        @pl.when(j == 0)
        def _():
            m_s[...] = jnp.full_like(m_s, -jnp.inf)
            l_s[...] = jnp.zeros_like(l_s)
            acc_s[...] = jnp.zeros_like(acc_s)

        @pl.when(j <= i)
        def _():
            s = _fox_logits(q_ref[...], k_ref[...], c_ref[...], ct_ref[...], h, i, j, T)
            m_new = jnp.maximum(m_s[...], jnp.max(s, axis=1, keepdims=True))
            alpha = jnp.exp(m_s[...] - m_new)
            p = jnp.exp(s - m_new)
            l_s[...] = alpha * l_s[...] + jnp.sum(p, axis=1, keepdims=True)
            acc_s[...] = alpha * acc_s[...] + jnp.dot(p.astype(BF16), v_ref[...], preferred_element_type=F32)
            m_s[...] = m_new

        @pl.when(j == i)
        def _():
            o_ref[...] = acc_s[...] / l_s[...]
            lse_ref[...] = jnp.broadcast_to(m_s[...] + jnp.log(l_s[...]), (T, LANES))

    qs = pl.BlockSpec((T, HEAD_DIM), lambda h, i, j: (i, h))
    ks = pl.BlockSpec((T, HEAD_DIM), lambda h, i, j: (jnp.minimum(j, i), h))
    return pl.pallas_call(
        body, name='fox_fwd', grid=(H, n, n),
        in_specs=[qs, ks, ks, pl.BlockSpec((T, LANES), lambda h, i, j: (i, 0)),
                  pl.BlockSpec((Hp, T), lambda h, i, j: (0, jnp.minimum(j, i)))],
        out_specs=[qs, pl.BlockSpec((None, T, LANES), lambda h, i, j: (h, i, 0))],
        out_shape=[jax.ShapeDtypeStruct((S, FW), F32), jax.ShapeDtypeStruct((H, S, LANES), F32)],
        scratch_shapes=[pltpu.VMEM((T, 1), F32), pltpu.VMEM((T, 1), F32), pltpu.VMEM((T, HEAD_DIM), F32)],
        compiler_params=_params(('parallel', 'parallel', 'arbitrary')))(qn, kn, vb, c, ct)


def _fox_p_ds(q_ref, k_ref, v_ref, do_ref, c_ref, ct_ref, lse_ref, dl_ref, h, i, j, T):
    s = _fox_logits(q_ref[...], k_ref[...], c_ref[...], ct_ref[...], h, i, j, T)
    p = jnp.exp(s - jnp.tile(lse_ref[...], (1, T // LANES)))
    dp = lax.dot_general(do_ref[...], v_ref[...], _DIMS['nt'], preferred_element_type=F32)
    ds = p * (dp - jnp.tile(dl_ref[...], (1, T // LANES)))
    return p, dp, ds


def fox_bwd_q(qn, kn, vb, do, c, ct, lse, dl, T):
    S, FW = qn.shape
    H = FW // HEAD_DIM
    Hp = ct.shape[0]
    n = S // T

    def body(q_ref, k_ref, v_ref, do_ref, c_ref, ct_ref, lse_ref, dl_ref, dq_ref, dl2_ref, acc_s, rs_s):
        h, i, j = pl.program_id(0), pl.program_id(1), pl.program_id(2)

        @pl.when(j == 0)
        def _():
            acc_s[...] = jnp.zeros_like(acc_s)
            rs_s[...] = jnp.zeros_like(rs_s)

        @pl.when(j <= i)
        def _():
            p, dp, ds = _fox_p_ds(q_ref, k_ref, v_ref, do_ref, c_ref, ct_ref, lse_ref, dl_ref, h, i, j, T)
            acc_s[...] += jnp.dot(ds.astype(BF16), k_ref[...], preferred_element_type=F32)
            rs_s[...] += jnp.sum(p * dp, axis=1, keepdims=True)

        @pl.when(j == i)
        def _():
            dq_ref[...] = acc_s[...] * (1.0 / math.sqrt(HEAD_DIM))
            dl2_ref[...] = jnp.broadcast_to(rs_s[...], (T, LANES))

    qs = pl.BlockSpec((T, HEAD_DIM), lambda h, i, j: (i, h))
    ks = pl.BlockSpec((T, HEAD_DIM), lambda h, i, j: (jnp.minimum(j, i), h))
    st = pl.BlockSpec((None, T, LANES), lambda h, i, j: (h, i, 0))
    return pl.pallas_call(
        body, name='fox_bwd_q', grid=(H, n, n),
        in_specs=[qs, ks, ks, qs, pl.BlockSpec((T, LANES), lambda h, i, j: (i, 0)),
                  pl.BlockSpec((Hp, T), lambda h, i, j: (0, jnp.minimum(j, i))), st, st],
        out_specs=[qs, st], out_shape=[jax.ShapeDtypeStruct((S, FW), F32), jax.ShapeDtypeStruct((H, S, LANES), F32)],
        scratch_shapes=[pltpu.VMEM((T, HEAD_DIM), F32), pltpu.VMEM((T, 1), F32)],
        compiler_params=_params(('parallel', 'parallel', 'arbitrary')))(qn, kn, vb, do, c, ct, lse, dl)


def fox_bwd_kv(qn, kn, vb, do, c, ct, lse, dl, T):
    S, FW = qn.shape
    H = FW // HEAD_DIM
    Hp = ct.shape[0]
    n = S // T

    def body(q_ref, k_ref, v_ref, do_ref, c_ref, ct_ref, lse_ref, dl_ref, dk_ref, dv_ref, dc_ref, dk_s, dv_s, dc_s):
        h, j, i = pl.program_id(0), pl.program_id(1), pl.program_id(2)

        @pl.when(i == 0)
        def _():
            dk_s[...] = jnp.zeros_like(dk_s)
            dv_s[...] = jnp.zeros_like(dv_s)
            dc_s[...] = jnp.zeros_like(dc_s)

        @pl.when(i >= j)
        def _():
            p, _, ds = _fox_p_ds(q_ref, k_ref, v_ref, do_ref, c_ref, ct_ref, lse_ref, dl_ref, h, i, j, T)
            dv_s[...] += lax.dot_general(p.astype(BF16), do_ref[...], _DIMS['tn'], preferred_element_type=F32)
            dk_s[...] += lax.dot_general(ds.astype(BF16), q_ref[...], _DIMS['tn'], preferred_element_type=F32)
            dc_s[...] += jnp.sum(ds, axis=0, keepdims=True)

        @pl.when(i == n - 1)
        def _():
            dk_ref[...] = dk_s[...] * (1.0 / math.sqrt(HEAD_DIM))
            dv_ref[...] = dv_s[...].astype(BF16)
            dc_ref[...] = -dc_s[...]

    qs = pl.BlockSpec((T, HEAD_DIM), lambda h, j, i: (jnp.maximum(i, j), h))
    ks = pl.BlockSpec((T, HEAD_DIM), lambda h, j, i: (j, h))
    st = pl.BlockSpec((None, T, LANES), lambda h, j, i: (h, jnp.maximum(i, j), 0))
    return pl.pallas_call(
        body, name='fox_bwd_kv', grid=(H, n, n),
        in_specs=[qs, ks, ks, qs, pl.BlockSpec((T, LANES), lambda h, j, i: (jnp.maximum(i, j), 0)),
                  pl.BlockSpec((Hp, T), lambda h, j, i: (0, j)), st, st],
        out_specs=[ks, ks, pl.BlockSpec((None, 1, T), lambda h, j, i: (h, 0, j))],
        out_shape=[jax.ShapeDtypeStruct((S, FW), F32), jax.ShapeDtypeStruct((S, FW), BF16),
                   jax.ShapeDtypeStruct((H, 1, S), F32)],
        scratch_shapes=[pltpu.VMEM((T, HEAD_DIM), F32), pltpu.VMEM((T, HEAD_DIM), F32), pltpu.VMEM((1, T), F32)],
        compiler_params=_params(('parallel', 'parallel', 'arbitrary')))(qn, kn, vb, do, c, ct, lse, dl)


def _shift_down(v, d, rows, fill):
    return jnp.where(rows >= d, pltpu.roll(v, d, 0), fill)


def _shift_up(v, d, rows, S, fill):
    return jnp.where(rows < S - d, pltpu.roll(v, S - d, 0), fill)


def _scan(a, b, rows, S, reverse):
    d = 1
    while d < S:
        if reverse:
            a_s, b_s = _shift_up(a, d, rows, S, 1.0), _shift_up(b, d, rows, S, 0.0)
        else:
            a_s, b_s = _shift_down(a, d, rows, 1.0), _shift_down(b, d, rows, 0.0)
        b = a * b_s + b
        a = a * a_s
        d *= 2
    return b


def _lru_forward(u, cw, cb, wra, bra, wri, bri, lam, rows):
    uc = cb + cw[CONV_W - 1] * u
    for d in range(1, CONV_W):
        uc = uc + cw[CONV_W - 1 - d] * _shift_down(u, d, rows, 0.0)
    ucb = uc.astype(BF16)
    r = _sigmoid(jnp.dot(ucb, wra.astype(BF16), preferred_element_type=F32) + bra)
    ig = _sigmoid(jnp.dot(ucb, wri.astype(BF16), preferred_element_type=F32) + bri)
    sp = _softplus(-lam)
    log_a = -LRU_C * r * sp
    a = jnp.exp(log_a)
    sq = jnp.sqrt(_neg_expm1(2.0 * log_a))
    iu = ig * uc
    hseq = _scan(a, sq * iu, rows, u.shape[0], False)
    return uc, ucb, r, ig, sp, a, sq, iu, hseq


def _lru_specs(S, n_u, n_g):
    col = lambda off: pl.BlockSpec((S, LANES), lambda cbk: (0, off + cbk))
    vec = pl.BlockSpec((1, LANES), lambda cbk: (0, cbk))
    mat = pl.BlockSpec((None, LANES, LANES), lambda cbk: (cbk, 0, 0))
    cw = pl.BlockSpec((CONV_W, LANES), lambda cbk: (0, cbk))
    return col, vec, mat, cw


def lru_fwd(proj, conv_w, conv_b, w_ra, b_ra, w_ri, b_ri, lam, u_off, g_off):
    S = proj.shape[0]
    nb = w_ra.shape[0]
    col, vec, mat, cws = _lru_specs(S, u_off, g_off)

    def body(u_ref, g_ref, cw_ref, cb_ref, wra_ref, bra_ref, wri_ref, bri_ref, lam_ref, y_ref):
        rows = lax.broadcasted_iota(jnp.int32, (S, LANES), 0)
        cw = [cw_ref[t:t + 1, :] for t in range(CONV_W)]
        hseq = _lru_forward(u_ref[...], cw, cb_ref[...], wra_ref[...], bra_ref[...], wri_ref[...],
                            bri_ref[...], lam_ref[...], rows)[-1]
        y_ref[...] = hseq * _gelu_and_grad(g_ref[...])[0]

    return pl.pallas_call(
        body, name='lru_fwd', grid=(nb,),
        in_specs=[col(u_off), col(g_off), cws, vec, mat, vec, mat, vec, vec], out_specs=col(0),
        out_shape=jax.ShapeDtypeStruct((S, nb * LANES), F32),
        compiler_params=_params(('parallel',)))(proj, proj, conv_w, conv_b, w_ra, b_ra, w_ri, b_ri, lam)


def lru_bwd(proj, dy, conv_w, conv_b, w_ra, b_ra, w_ri, b_ri, lam, u_off, g_off):
    S = proj.shape[0]
    nb = w_ra.shape[0]
    LW = nb * LANES
    col, vec, mat, cws = _lru_specs(S, u_off, g_off)

    def body(u_ref, g_ref, dy_ref, cw_ref, cb_ref, wra_ref, bra_ref, wri_ref, bri_ref, lam_ref,
             du_ref, dg_ref, dcw_ref, dcb_ref, dwra_ref, dbra_ref, dwri_ref, dbri_ref, dlam_ref):
        rows = lax.broadcasted_iota(jnp.int32, (S, LANES), 0)
        u, lam_v = u_ref[...], lam_ref[...]
        cw = [cw_ref[t:t + 1, :] for t in range(CONV_W)]
        wra, wri = wra_ref[...].astype(BF16), wri_ref[...].astype(BF16)
        uc, ucb, r, ig, sp, a, sq, iu, hseq = _lru_forward(u, cw, cb_ref[...], wra, bra_ref[...], wri, bri_ref[...],
                                                           lam_v, rows)
        gl, dgl = _gelu_and_grad(g_ref[...])
        dy_v = dy_ref[...]
        dg_ref[...] = (dy_v * hseq * dgl).astype(BF16)
        G = _scan(_shift_up(a, 1, rows, S, 0.0), dy_v * gl, rows, S, True)
        da = G * _shift_down(hseq, 1, rows, 0.0)
        diu = G * sq
        dsq = G * iu
        dlog_a = da * a - dsq * a * a / jnp.maximum(sq, 1e-30)
        dr = dlog_a * (-LRU_C * sp)
        dsp = jnp.sum(dlog_a * (-LRU_C * r), axis=0, keepdims=True)
        dlam_ref[...] = -dsp * _sigmoid(-lam_v)
        dzr = dr * r * (1.0 - r)
        dzi = diu * uc * ig * (1.0 - ig)
        dzrb, dzib = dzr.astype(BF16), dzi.astype(BF16)
        duc = (diu * ig + lax.dot_general(dzrb, wra, _DIMS['nt'], preferred_element_type=F32)
               + lax.dot_general(dzib, wri, _DIMS['nt'], preferred_element_type=F32))
        dwra_ref[...] = lax.dot_general(ucb, dzrb, _DIMS['tn'], preferred_element_type=F32)
        dwri_ref[...] = lax.dot_general(ucb, dzib, _DIMS['tn'], preferred_element_type=F32)
        dbra_ref[...] = jnp.sum(dzr, axis=0, keepdims=True)
        dbri_ref[...] = jnp.sum(dzi, axis=0, keepdims=True)
        dcb_ref[...] = jnp.sum(duc, axis=0, keepdims=True)
        du = cw[CONV_W - 1] * duc
        dcw_ref[CONV_W - 1:CONV_W, :] = jnp.sum(duc * u, axis=0, keepdims=True)
        for d in range(1, CONV_W):
            du = du + cw[CONV_W - 1 - d] * _shift_up(duc, d, rows, S, 0.0)
            dcw_ref[CONV_W - 1 - d:CONV_W - d, :] = jnp.sum(duc * _shift_down(u, d, rows, 0.0), axis=0, keepdims=True)
        du_ref[...] = du.astype(BF16)

    sd = jax.ShapeDtypeStruct
    return pl.pallas_call(
        body, name='lru_bwd', grid=(nb,),
        in_specs=[col(u_off), col(g_off), col(0), cws, vec, mat, vec, mat, vec, vec],
        out_specs=[col(0), col(0), cws, vec, mat, vec, mat, vec, vec],
        out_shape=[sd((S, LW), BF16), sd((S, LW), BF16), sd((CONV_W, LW), F32), sd((1, LW), F32),
                   sd((nb, LANES, LANES), F32), sd((1, LW), F32), sd((nb, LANES, LANES), F32), sd((1, LW), F32),
                   sd((1, LW), F32)],
        compiler_params=_params(('parallel',)))(proj, proj, dy, conv_w, conv_b, w_ra, b_ra, w_ri, b_ri, lam)


def mix_fwd(o_fox, y_lru, g_fox, g_lru):
    S, FW = o_fox.shape
    tr = _tile(S, (256, 128))

    def body(o_ref, y_ref, gf_ref, gl_ref, m_ref):
        m_ref[...] = jnp.concatenate([_rms(o_ref[...], gf_ref[...]), _rms(y_ref[...], gl_ref[...])],
                                     axis=1).astype(BF16)

    return _rows_call('mix_fwd', body, S, tr,
                      [(o_fox, _rb(tr, FW)), (y_lru, _rb(tr, FW)), (g_fox, _fb((1, FW))), (g_lru, _fb((1, FW)))],
                      [((S, 2 * FW), BF16, _rb(tr, 2 * FW))])[0]


def mix_bwd(o_fox, y_lru, g_fox, g_lru, dmix):
    S, FW = o_fox.shape
    H = FW // HEAD_DIM
    tr = _tile(S, (256, 128))

    def body(o_ref, y_ref, gf_ref, gl_ref, df_ref, dl_ref, do_ref, dlt_ref, dy_ref, dgf_ref, dgl_ref):
        o = o_ref[...]
        do, dgf = _rms_bwd(o, gf_ref[...], df_ref[...])
        dyl, dgl = _rms_bwd(y_ref[...], gl_ref[...], dl_ref[...])
        do_ref[...] = do.astype(BF16)
        dy_ref[...] = dyl
        prod = do * o
        for h in range(H):
            dlt_ref[h] = jnp.broadcast_to(
                jnp.sum(prod[:, h * HEAD_DIM:(h + 1) * HEAD_DIM], axis=1, keepdims=True), (tr, LANES))
        first = pl.program_id(0) == 0
        _acc_out(dgf_ref, first, dgf)
        _acc_out(dgl_ref, first, dgl)

    g = _fb((1, FW))
    return _rows_call('mix_bwd', body, S, tr,
                      [(o_fox, _rb(tr, FW)), (y_lru, _rb(tr, FW)), (g_fox, g), (g_lru, g), (dmix, _rb(tr, FW, 0)),
                       (dmix, _rb(tr, FW, 1))],
                      [((S, FW), BF16, _rb(tr, FW)), ((H, S, LANES), F32, pl.BlockSpec((H, tr, LANES), lambda i: (0, i, 0))),
                       ((S, FW), F32, _rb(tr, FW)), ((1, FW), F32, g), ((1, FW), F32, g)])


def _xattn_heads(cq_raw, ckv, g_cq, g_ck, XW):
    out = []
    for h in range(XW // HEAD_DIM):
        sl = slice(h * HEAD_DIM, (h + 1) * HEAD_DIM)
        out.append((cq_raw[:, sl], _rms(cq_raw[:, sl], g_cq), ckv[:, sl], _rms(ckv[:, sl], g_ck),
                    ckv[:, XW + h * HEAD_DIM:XW + (h + 1) * HEAD_DIM].astype(BF16)))
    return out


def xattn_fwd(cq_raw, ckv, g_cq, g_ck):
    S, XW = cq_raw.shape
    M = ckv.shape[0]
    tr = _tile(S, (512, 256, 128))

    def body(q_ref, kv_ref, gq_ref, gk_ref, o_ref):
        outs = []
        for _, qn, _, kn, v in _xattn_heads(q_ref[...], kv_ref[...], gq_ref[...], gk_ref[...], XW):
            s = lax.dot_general(qn.astype(BF16), kn.astype(BF16), _DIMS['nt'], preferred_element_type=F32)
            s = s / math.sqrt(HEAD_DIM)
            p = jnp.exp(s - jnp.max(s, axis=1, keepdims=True))
            p = p / jnp.sum(p, axis=1, keepdims=True)
            outs.append(jnp.dot(p.astype(BF16), v, preferred_element_type=F32))
        o_ref[...] = jnp.concatenate(outs, axis=1).astype(BF16)

    g = _fb((1, HEAD_DIM))
    return _rows_call('xattn_fwd', body, S, tr,
                      [(cq_raw, _rb(tr, XW)), (ckv, _fb((M, 2 * XW))), (g_cq, g), (g_ck, g)],
                      [((S, XW), BF16, _rb(tr, XW))])[0]


def xattn_bwd(cq_raw, ckv, g_cq, g_ck, do):
    S, XW = cq_raw.shape
    M = ckv.shape[0]
    tr = _tile(S, (512, 256, 128))
    n = S // tr

    def body(q_ref, kv_ref, gq_ref, gk_ref, do_ref, dq_ref, dkv_ref, dgq_ref, dgk_ref):
        i = pl.program_id(0)
        do_v = do_ref[...]
        dqs, dkn, dvs = [], [], []
        dgq = jnp.zeros((1, HEAD_DIM), F32)
        for h, (q_raw, qn, _, kn, v) in enumerate(_xattn_heads(q_ref[...], kv_ref[...], gq_ref[...], gk_ref[...], XW)):
            qb, kb = qn.astype(BF16), kn.astype(BF16)
            doh = do_v[:, h * HEAD_DIM:(h + 1) * HEAD_DIM]
            s = lax.dot_general(qb, kb, _DIMS['nt'], preferred_element_type=F32) / math.sqrt(HEAD_DIM)
            p = jnp.exp(s - jnp.max(s, axis=1, keepdims=True))
            p = p / jnp.sum(p, axis=1, keepdims=True)
            dp = lax.dot_general(doh, v, _DIMS['nt'], preferred_element_type=F32)
            ds = (p * (dp - jnp.sum(p * dp, axis=1, keepdims=True)) / math.sqrt(HEAD_DIM)).astype(BF16)
            dvs.append(lax.dot_general(p.astype(BF16), doh, _DIMS['tn'], preferred_element_type=F32))
            dkn.append(lax.dot_general(ds, qb, _DIMS['tn'], preferred_element_type=F32))
            dq, g1 = _rms_bwd(q_raw, gq_ref[...], jnp.dot(ds, kb, preferred_element_type=F32))
            dqs.append(dq)
            dgq = dgq + g1
        dq_ref[...] = jnp.concatenate(dqs, axis=1).astype(BF16)
        first = i == 0
        _acc_out(dgq_ref, first, dgq)
        _acc_out(dkv_ref, first, jnp.concatenate(dkn + dvs, axis=1))

        @pl.when(i == n - 1)
        def _():
            kv = kv_ref[...]
            acc = dkv_ref[...]
            dk, gk = _heads(lambda t, d: _rms_bwd(t, gk_ref[...], d), XW // HEAD_DIM, kv[:, :XW], acc[:, :XW])
            dkv_ref[:, :XW] = dk
            dgk_ref[...] = gk

    g = _fb((1, HEAD_DIM))
    return _rows_call('xattn_bwd', body, S, tr,
                      [(cq_raw, _rb(tr, XW)), (ckv, _fb((M, 2 * XW))), (g_cq, g), (g_ck, g), (do, _rb(tr, XW))],
                      [((S, XW), BF16, _rb(tr, XW)), ((M, 2 * XW), F32, _fb((M, 2 * XW))), ((1, HEAD_DIM), F32, g),
                       ((1, HEAD_DIM), F32, g)])


def swiglu_fwd(gu, F):
    S = gu.shape[0]
    tr = _tile(S, (256, 128))
    tf = _tile(F, (1408, 1024, 512, 256, 128))
    nf = F // tf

    def body(g_ref, u_ref, a_ref):
        g = g_ref[...]
        a_ref[...] = (g * _sigmoid(g) * u_ref[...]).astype(BF16)

    return pl.pallas_call(
        body, name='swiglu_fwd', grid=(S // tr, nf),
        in_specs=[pl.BlockSpec((tr, tf), lambda i, n: (i, n)), pl.BlockSpec((tr, tf), lambda i, n: (i, n + nf))],
        out_specs=pl.BlockSpec((tr, tf), lambda i, n: (i, n)), out_shape=jax.ShapeDtypeStruct((S, F), BF16),
        compiler_params=_params(('parallel', 'parallel')))(gu, gu)


def swiglu_bwd(gu, dact, F, after):
    S = gu.shape[0]
    tr = _tile(S, (256, 128))
    tf = _tile(F, (1408, 1024, 512, 256, 128))
    nf = F // tf

    def body(g_ref, u_ref, da_ref, after_ref, o_ref):
        g, da = g_ref[...], da_ref[...]
        sg = _sigmoid(g)
        o_ref[0] = (da * u_ref[...] * sg * (1.0 + g * (1.0 - sg))).astype(BF16)
        o_ref[1] = (da * g * sg).astype(BF16)

    return pl.pallas_call(
        body, name='swiglu_bwd', grid=(S // tr, nf),
        in_specs=[pl.BlockSpec((tr, tf), lambda i, n: (i, n)), pl.BlockSpec((tr, tf), lambda i, n: (i, n + nf)),
                  pl.BlockSpec((tr, tf), lambda i, n: (i, n)), ANY],
        out_specs=pl.BlockSpec((2, tr, tf), lambda i, n: (0, i, n)), out_shape=jax.ShapeDtypeStruct((2, S, F), BF16),
        compiler_params=_params(('parallel', 'parallel')))(gu, gu, dact, after)


def loss_head(y, target):
    S, D = y.shape
    tr = _tile(S, (256, 128))

    def body(y_ref, t_ref, d_ref, db_ref, l_ref):
        err = y_ref[...] - t_ref[...]
        d = err * (1.0 / D)
        d_ref[...] = d
        db_ref[...] = d.astype(BF16)
        part = jnp.sum(jnp.sum(err * err, axis=1, keepdims=True), axis=0, keepdims=True) * (0.5 / D)
        _acc_out(l_ref, pl.program_id(0) == 0, jnp.broadcast_to(part, (1, LANES)))

    return _rows_call('loss_head', body, S, tr, [(y, _rb(tr, D)), (target, _rb(tr, D))],
                      [((S, D), F32, _rb(tr, D)), ((S, D), BF16, _rb(tr, D)), ((1, LANES), F32, _fb((1, LANES)))])


def _adamw_math(w, gv, m, v):
    mn = ADAM_B1 * m + (1.0 - ADAM_B1) * gv
    vn = ADAM_B2 * v + (1.0 - ADAM_B2) * (gv * gv)
    m_hat = mn / (1.0 - ADAM_B1 ** ADAM_STEP)
    v_hat = vn / (1.0 - ADAM_B2 ** ADAM_STEP)
    return -ADAM_LR * (m_hat / (jnp.sqrt(v_hat) + ADAM_EPS) + ADAM_WD * w), mn, vn


def adamw(name, w, g, m, v):
    R, C = w.shape
    tr = _row_tile(R, C)

    def body(w_ref, g_ref, m_ref, v_ref, d_ref, mo_ref, vo_ref):
        d_ref[...], mo_ref[...], vo_ref[...] = _adamw_math(w_ref[...], g_ref[...], m_ref[...], v_ref[...])

    spec = _rb(tr, C)
    return _rows_call(name, body, R, tr, [(w, spec), (g, spec), (m, spec), (v, spec)], [((R, C), F32, spec)] * 3)


def adamw_halves(name, w, mine, other, m, v, c_idx):
    R, C = w.shape
    hr = R // 2
    tr = _row_tile(hr, C)

    def body(c_ref, w_ref, a_ref, b_ref, m_ref, v_ref, g_ref, d_ref, mo_ref, vo_ref):
        gv = jnp.where(pl.program_id(0) == c_ref[0], a_ref[...], b_ref[...])
        g_ref[...] = gv
        d_ref[...], mo_ref[...], vo_ref[...] = _adamw_math(w_ref[...], gv, m_ref[...], v_ref[...])

    full = pl.BlockSpec((None, tr, C), lambda hh, i, c_ref: (hh, i, 0))
    half = pl.BlockSpec((tr, C), lambda hh, i, c_ref: (i, 0))
    outs = pl.pallas_call(
        body, name=name,
        grid_spec=pltpu.PrefetchScalarGridSpec(num_scalar_prefetch=1, grid=(2, hr // tr),
                                               in_specs=[full, half, half, full, full], out_specs=[full] * 4),
        out_shape=[jax.ShapeDtypeStruct((2, hr, C), F32)] * 4,
        compiler_params=_params(('parallel', 'parallel')))(
            c_idx, w.reshape(2, hr, C), mine, other, m.reshape(2, hr, C), v.reshape(2, hr, C))
    return [o.reshape(R, C) for o in outs]


def _place():
    x, y, c = lax.axis_index('x'), lax.axis_index('y'), lax.axis_index('c')
    return x, y, c, [(1 - x, y), (x, 1 - y), (1 - x, 1 - y)]


def _rcopy(src, dst, ssem, rsem, dev):
    return pltpu.make_async_remote_copy(src_ref=src, dst_ref=dst, send_sem=ssem, recv_sem=rsem, device_id=dev,
                                        device_id_type=MESH)


HBM = pl.BlockSpec(memory_space=pltpu.HBM)
SEM = pl.BlockSpec(memory_space=pltpu.SEMAPHORE)
EFFECT = pltpu.SideEffectType.DATAFLOW_SIDE_EFFECTING


def _in_hbm(a):
    return pltpu.with_memory_space_constraint(a, pltpu.HBM)


def _rows_part(shape, whole, half):
    return pl.ds(0, shape[0]) if whole else pl.ds(half * (shape[0] // 2), shape[0] // 2)


def gather_start(name, shards, whole):
    nT = len(shards)

    def body(*refs):
        srcs, lands = refs[:nT], refs[nT:2 * nT]
        ssem, rsem, token = refs[2 * nT], refs[2 * nT + 1], refs[-1]
        x, y, c, chips = _place()
        for t in range(nT):
            rows = _rows_part(shards[t].shape, whole[t], c)
            for k, (px, py) in enumerate(chips):
                _rcopy(srcs[t].at[rows], lands[t].at[2 * x + y, rows], ssem.at[3 * t + k], rsem.at[3 * t + k],
                       (px, py, c)).start()
        token[...] = jnp.zeros_like(token)

    zones = [lax.empty((N_CHIPS,) + s.shape, s.dtype) for s in shards]
    outs = pl.pallas_call(
        body, name=name,
        out_shape=(pltpu.SemaphoreType.DMA((3 * nT,)), pltpu.SemaphoreType.DMA((3 * nT,)),
                   *[pltpu.HBM(s.shape, s.dtype) for s in shards], *[pltpu.HBM(z.shape, z.dtype) for z in zones],
                   jax.ShapeDtypeStruct((8, LANES), F32)),
        in_specs=[HBM] * (2 * nT), out_specs=(SEM, SEM, *[HBM] * (2 * nT), pl.BlockSpec(memory_space=pltpu.VMEM)),
        input_output_aliases={i: 2 + i for i in range(2 * nT)},
        compiler_params=pltpu.CompilerParams(has_side_effects=EFFECT))(*[_in_hbm(a) for a in list(shards) + zones])
    return outs[0], outs[1], outs[2:2 + nT], outs[2 + nT:2 + 2 * nT], outs[-1]


def gather_wait(name, t, shard, zone, ssem, rsem, after, whole):
    def body(src_ref, land_ref, ssem_ref, rsem_ref, after_ref, src_out, land_out):
        x, y, c, chips = _place()
        rows = _rows_part(shard.shape, whole, c)
        for k, (px, py) in enumerate(chips):
            cp = _rcopy(src_ref.at[rows], land_ref.at[2 * px + py, rows], ssem_ref.at[3 * t + k], rsem_ref.at[3 * t + k],
                        (px, py, c))
            cp.wait_send()
            cp.wait_recv()

    return pl.pallas_call(
        body, name=name, out_shape=(pltpu.HBM(shard.shape, shard.dtype), pltpu.HBM(zone.shape, zone.dtype)),
        in_specs=(HBM, HBM, SEM, SEM, ANY), out_specs=(HBM, HBM), input_output_aliases={0: 0, 1: 1},
        compiler_params=pltpu.CompilerParams(has_side_effects=EFFECT))(shard, zone, ssem, rsem, after)


def pair_swap(name, zone):
    hr = zone.shape[1] // 2

    def body(z_in, z_ref, ssem, rsem):
        x, y, c, chips = _place()
        cps = []
        for k, (px, py) in enumerate(chips):
            blk = z_ref.at[2 * px + py, pl.ds(c * hr, hr)]
            cps.append(_rcopy(blk, blk, ssem.at[k], rsem.at[k], (x, y, 1 - c)))
            cps[-1].start()
        for k, (px, py) in enumerate(chips):
            blk = z_ref.at[2 * px + py, pl.ds((1 - c) * hr, hr)]
            _rcopy(blk, blk, ssem.at[k], rsem.at[k], (x, y, 1 - c)).wait_recv()
        for cp in cps:
            cp.wait_send()

    return pl.pallas_call(
        body, name=name, in_specs=[ANY], out_specs=ANY, out_shape=jax.ShapeDtypeStruct(zone.shape, zone.dtype),
        input_output_aliases={0: 0},
        scratch_shapes=[pltpu.SemaphoreType.DMA((3,)), pltpu.SemaphoreType.DMA((3,))],
        compiler_params=_params())(zone)


N_SENDERS = 7


def _scatter_copies(g_ref, l_ref, ssem, rsem):
    x, y, c, chips = _place()
    cps = []
    for k, (px, py) in enumerate(chips):
        for d in range(2):
            to = (c + d) % 2
            cps.append(_rcopy(g_ref.at[2 * px + py, to], l_ref.at[2 * k + d], ssem.at[2 * k + d], rsem.at[2 * k + d],
                              (px, py, to)))
    cps.append(_rcopy(g_ref.at[2 * x + y, 1 - c], l_ref.at[6], ssem.at[6], rsem.at[6], (x, y, 1 - c)))
    return cps


def scatter_start(name, g):
    def body(g_ref, l_ref, ssem, rsem, g_out, l_out, token):
        for cp in _scatter_copies(g_ref, l_ref, ssem, rsem):
            cp.start()
        token[...] = jnp.zeros_like(token)

    zone = lax.empty((N_SENDERS,) + g.shape[2:], g.dtype)
    return pl.pallas_call(
        body, name=name,
        out_shape=(pltpu.SemaphoreType.DMA((N_SENDERS,)), pltpu.SemaphoreType.DMA((N_SENDERS,)),
                   pltpu.HBM(g.shape, g.dtype), pltpu.HBM(zone.shape, zone.dtype), jax.ShapeDtypeStruct((8, LANES), F32)),
        in_specs=[HBM, HBM], out_specs=(SEM, SEM, HBM, HBM, pl.BlockSpec(memory_space=pltpu.VMEM)),
        input_output_aliases={0: 2, 1: 3},
        compiler_params=pltpu.CompilerParams(has_side_effects=EFFECT))(_in_hbm(g), _in_hbm(zone))


def scatter_wait(name, g, zone, ssem, rsem, after):
    def body(g_ref, l_ref, ssem_ref, rsem_ref, after_ref, g_out, l_out):
        for cp in _scatter_copies(g_ref, l_ref, ssem_ref, rsem_ref):
            cp.wait_send()
            cp.wait_recv()

    return pl.pallas_call(
        body, name=name, out_shape=(pltpu.HBM(g.shape, g.dtype), pltpu.HBM(zone.shape, zone.dtype)),
        in_specs=(HBM, HBM, SEM, SEM, ANY), out_specs=(HBM, HBM), input_output_aliases={0: 0, 1: 1},
        compiler_params=pltpu.CompilerParams(has_side_effects=EFFECT))(g, zone, ssem, rsem, after)


def sum_parts(name, g, landed, chip_idx, c_idx):
    hr, C = g.shape[2:]
    tr = _row_tile(hr, C, min_rows=16)

    def body(me_ref, c_ref, g_ref, l_ref, o_ref):
        acc = g_ref[...].astype(F32)
        for s in range(N_SENDERS):
            acc = acc + l_ref[s].astype(F32)
        o_ref[...] = acc

    return pl.pallas_call(
        body, name=name,
        grid_spec=pltpu.PrefetchScalarGridSpec(
            num_scalar_prefetch=2, grid=(hr // tr,),
            in_specs=[pl.BlockSpec((None, None, tr, C), lambda i, me_ref, c_ref: (me_ref[0], c_ref[0], i, 0)),
                      pl.BlockSpec((N_SENDERS, tr, C), lambda i, me_ref, c_ref: (0, i, 0))],
            out_specs=pl.BlockSpec((tr, C), lambda i, me_ref, c_ref: (i, 0))),
        out_shape=jax.ShapeDtypeStruct((hr, C), F32),
        compiler_params=_params(('parallel',)))(chip_idx, c_idx, g, landed)


def pair_join(name, halves):
    nT = len(halves)

    def body(*refs):
        ins, outs = refs[:nT], refs[nT:2 * nT]
        ssem, rsem = refs[2 * nT:]
        x, y, c, _ = _place()
        cps = [_rcopy(ins[t], outs[t], ssem.at[t], rsem.at[t], (x, y, 1 - c)) for t in range(nT)]
        for cp in cps:
            cp.start()
        for cp in cps:
            cp.wait()

    return pl.pallas_call(
        body, name=name, in_specs=[ANY] * nT, out_specs=[ANY] * nT,
        out_shape=[jax.ShapeDtypeStruct(h.shape, h.dtype) for h in halves],
        scratch_shapes=[pltpu.SemaphoreType.DMA((nT,)), pltpu.SemaphoreType.DMA((nT,))],
        compiler_params=_params())(*halves)


def allreduce_small(buf, after):
    R = buf.shape[0]
    VM = pl.BlockSpec(memory_space=pltpu.VMEM)

    def body(x_ref, after_ref, o_ref, all_ref, ssem, rsem, lsem):
        x, y, c, chips = _place()
        me, sibling = (x, y, c), (x, y, 1 - c)

        def rows(px, py, pc):
            return all_ref.at[pl.ds((4 * px + 2 * py + pc) * R, R), :]

        def copy(k, block, to, src=None):
            return _rcopy(rows(*block) if src is None else src, rows(*block), ssem.at[k], rsem.at[k], to)

        mine = pltpu.make_async_copy(x_ref, rows(*me), lsem)
        mine.start()
        first = [copy(0, me, sibling, src=x_ref)]
        first += [copy(1 + k, me, (*chip, c), src=x_ref) for k, chip in enumerate(chips)]
        for cp in first:
            cp.start()
        passed = [copy(4 + k, (*chip, c), sibling) for k, chip in enumerate(chips)]
        for k, chip in enumerate(chips):
            copy(1 + k, (*chip, c), me).wait_recv()
            passed[k].start()
        copy(0, sibling, me).wait_recv()
        for k, chip in enumerate(chips):
            copy(4 + k, (*chip, 1 - c), me).wait_recv()
        for cp in first + passed:
            cp.wait_send()
        mine.wait()
        acc = all_ref[0:R, :]
        for d in range(1, 8):
            acc = acc + all_ref[d * R:(d + 1) * R, :]
        o_ref[...] = acc

    return pl.pallas_call(
        body, name='allreduce_small', in_specs=[VM, ANY], out_specs=VM, out_shape=jax.ShapeDtypeStruct((R, LANES), F32),
        scratch_shapes=[pltpu.VMEM((8 * R, LANES), F32), pltpu.SemaphoreType.DMA((7,)), pltpu.SemaphoreType.DMA((7,)),
                        pltpu.SemaphoreType.DMA],
        compiler_params=_params())(buf, after)


class _InWindows:
    def __init__(self, FW, LW, H, C):
        gap = LANES - H
        padded = lambda o: o if o < 3 * FW + H else o + gap
        self.width = 3 * FW + LANES + 2 * LW
        self.first = [padded(C * j) // LANES for j in range(N_CHIPS)]
        self.blocks = max(padded(C * (j + 1) - 1) // LANES - self.first[j] + 1 for j in range(N_CHIPS))
        assert all((b + self.blocks) * LANES <= self.width for b in self.first)
        self.cols = self.blocks * LANES
        self.runs = []
        for j in range(N_CHIPS):
            cut = min(max(3 * FW + H - C * j, 0), C)
            spans = [(0, cut), (cut, C)]
            self.runs.append([(t0, t1, padded(C * j + t0) - LANES * self.first[j]) for t0, t1 in spans if t1 > t0])

    def to_window(self, shard, chip):
        def place(j, s):
            parts, pos = [], 0
            for t0, t1, w0 in self.runs[j]:
                parts += [jnp.zeros((s.shape[0], w0 - pos), s.dtype), s[:, t0:t1]]
                pos = w0 + t1 - t0
            parts.append(jnp.zeros((s.shape[0], self.cols - pos), s.dtype))
            return jnp.concatenate([p for p in parts if p.shape[1]], axis=1)
        return lax.switch(chip, [functools.partial(place, j) for j in range(N_CHIPS)], shard)

    def from_window(self, win, chip):
        def take(j, w):
            return jnp.concatenate([w[:, w0:w0 + t1 - t0] for t0, t1, w0 in self.runs[j]], axis=1)
        return lax.switch(chip, [functools.partial(take, j) for j in range(N_CHIPS)], win)

    def assemble(self, zone):
        total = None
        for j in range(N_CHIPS):
            lead = self.first[j] * LANES
            part = jnp.pad(zone[j], ((0, 0), (lead, self.width - lead - self.cols)))
            total = part if total is None else total + part
        return total

    def windows(self, padded_matrix):
        return jnp.stack([padded_matrix[:, b * LANES:b * LANES + self.cols] for b in self.first])


_PACK = 8 * LANES


def _pack(arrs):
    flat = []
    for a in arrs:
        v = a.reshape(-1).astype(F32)
        flat.append(jnp.pad(v, (0, (-v.shape[0]) % _PACK)))
    return jnp.concatenate(flat).reshape(-1, LANES)


def _unpack(buf, shapes):
    out, off = [], 0
    flat = buf.reshape(-1)
    for sh in shapes:
        n = math.prod(sh)
        out.append(flat[off:off + n].reshape(sh))
        off += n + (-n) % _PACK
    return out


def kernel(x, mem, g_mix, w_in, b_f, g_q, g_k, conv_w, conv_b, w_ra, b_ra, w_ri, b_ri, lam, g_fox_out, g_lru_out, w_out, g_xattn, g_mem, w_cq, w_ckv, g_cq, g_ck, w_co, g_ffn, w_gate_up, w_down, loss_target, m_g_mix, m_w_in, m_b_f, m_g_q, m_g_k, m_conv_w, m_conv_b, m_w_ra, m_b_ra, m_w_ri, m_b_ri, m_lam, m_g_fox_out, m_g_lru_out, m_w_out, m_g_xattn, m_g_mem, m_w_cq, m_w_ckv, m_g_cq, m_g_ck, m_w_co, m_g_ffn, m_w_gate_up, m_w_down, v_g_mix, v_w_in, v_b_f, v_g_q, v_g_k, v_conv_w, v_conv_b, v_w_ra, v_b_ra, v_w_ri, v_b_ri, v_lam, v_g_fox_out, v_g_lru_out, v_w_out, v_g_xattn, v_g_mem, v_w_cq, v_w_ckv, v_g_cq, v_g_ck, v_w_co, v_g_ffn, v_w_gate_up, v_w_down):
    given = dict(locals())
    W = {n: given[n][0] for n in WEIGHTS}
    M1 = {n: given['m_' + n][0] for n in WEIGHTS}
    V1 = {n: given['v_' + n][0] for n in WEIGHTS}
    xs, ms, tgt = x[0], mem[0], loss_target[0]
    S, D = xs.shape
    H = W['b_f'].shape[0]
    FW = H * HEAD_DIM
    LW = W['lam'].shape[0]
    nb = W['w_ra'].shape[0]
    XW = W['w_cq'].shape[1]
    F = W['w_down'].shape[0] * N_CHIPS
    IN_W = W['w_in'].shape[1] * N_CHIPS
    assert FW == LW and LW == nb * LANES and IN_W == 3 * FW + H + 2 * LW and H <= 8
    T = _tile(S, (512, 256, 128))
    c_idx = lax.axis_index('c').astype(jnp.int32).reshape(1)
    chip = 2 * lax.axis_index('x') + lax.axis_index('y')
    chip_idx = chip.astype(jnp.int32).reshape(1)
    vec = lambda n: W[n].reshape(1, -1)

    wins = _InWindows(FW, LW, H, W['w_in'].shape[1])
    started = {}
    g_tok = jnp.zeros((1, 1), F32)
    for call, names in (('gather_start_first', ['conv_w', 'w_in']), ('gather_start_rest', BIG[1:])):
        own = [W[n].reshape(-1, LANES) if n == 'conv_w' else W[n].astype(BF16) + g_tok.astype(BF16) for n in names]
        own = [wins.to_window(o, chip) if n == 'w_in' else o for n, o in zip(names, own)]
        ssem, rsem, srcs, zones, tok = gather_start(call, own, [n == 'conv_w' for n in names])
        g_tok = tok[0:1, 0:1]
        started.update({n: (t, srcs[t], zones[t], ssem, rsem) for t, n in enumerate(names)})

    def fetch(n, after):
        t, g_src, g_zone, g_ssem, g_rsem = started[n]
        src, zone = gather_wait('gather_wait_' + n, t, g_src, g_zone, g_ssem, g_rsem, after, n == 'conv_w')
        if n != 'conv_w':
            zone = pair_swap('pair_swap_' + n, zone)
        return lax.dynamic_update_index_in_dim(zone, src, chip, 0)

    b_f_pad = jnp.pad(vec('b_f'), ((0, 0), (0, LANES - H)))
    u_off, g_off = 3 * FW // LANES, (3 * FW + LW) // LANES

    h1 = norm_fwd('norm_mix', xs, vec('g_mix') + g_tok[0:1, 0:1])
    conv_full = fetch('conv_w', h1).reshape(N_CHIPS, CONV_W, LW // N_CHIPS).transpose(1, 0, 2).reshape(CONV_W, LW)
    w_in_pad = wins.assemble(fetch('w_in', h1))
    w5 = jnp.concatenate([w_in_pad[:, :3 * FW], w_in_pad[:, 3 * FW + LANES:]], axis=1)
    wf = w_in_pad[:, 3 * FW:3 * FW + LANES]
    proj = _mm('proj_in', h1, w5, 'nn', F32)
    f_raw = _mm('proj_f', h1, wf, 'nn', F32)
    qn, kn, vb = qkv_fwd(proj, vec('g_q'), vec('g_k'), FW)
    cc = fgate_fwd(f_raw, b_f_pad)
    ct = cc[:, :8].T
    o_fox, lse = fox_fwd(qn, kn, vb, cc, ct, T)
    lru_w = (conv_full, vec('conv_b'), W['w_ra'], vec('b_ra'), W['w_ri'], vec('b_ri'), vec('lam'))
    y_lru = lru_fwd(proj, *lru_w, u_off, g_off)
    mixn = mix_fwd(o_fox, y_lru, vec('g_fox_out'), vec('g_lru_out'))
    w_out_f = fetch('w_out', mixn).reshape(2 * FW, D)
    x1 = _mm('proj_out', mixn, w_out_f, 'nn', F32, res=xs)

    hq = norm_fwd('norm_xq', x1, vec('g_xattn'))
    mn = norm_fwd('norm_mem', ms, vec('g_mem'))
    w_cq_f = fetch('w_cq', hq).reshape(D, XW)
    w_ckv_f = fetch('w_ckv', hq).reshape(D, 2 * XW)
    cq_raw = _mm('proj_cq', hq, w_cq_f, 'nn', F32)
    ckv = _mm('proj_ckv', mn, w_ckv_f, 'nn', F32)
    o_x = xattn_fwd(cq_raw, ckv, vec('g_cq'), vec('g_ck'))
    w_co_g = fetch('w_co', o_x)
    x2 = _mm_colsharded('proj_co', o_x, w_co_g, F32, res=x1)

    hf = norm_fwd('norm_ffn', x2, vec('g_ffn'))
    w_gu_g = fetch('w_gate_up', hf)
    gu = _mm_colsharded('proj_gate_up', hf, w_gu_g, F32)
    act = swiglu_fwd(gu, F)
    w_down_f = fetch('w_down', act).reshape(F, D)
    yv = _mm('proj_down', act, w_down_f, 'nn', F32, res=x2)
    dy, dyb, loss_blk = loss_head(yv, tgt)

    gw, pending = {}, []

    def reduce_begin(n, g):
        sp = g.reshape(N_CHIPS, 2, g.shape[1] // 2, g.shape[2])
        ssem, rsem, sp, zone, tok = scatter_start('scatter_start_' + n, sp)
        pending.append((n, sp, zone, ssem, rsem))
        return tok[0:1, 0:1]

    dact = _mm('bwd_down_x', dyb, w_down_f, 'nt', F32)
    t_down = reduce_begin('w_down', _mm('bwd_down_w', act, dyb, 'tn', BF16).reshape(N_CHIPS, F // N_CHIPS, D))
    dgu = swiglu_bwd(gu, dact, F, t_down)
    dhf = _mm_colsharded_t('bwd_gate_up_x', dgu, w_gu_g, F32)
    t_gu = reduce_begin('w_gate_up', _mm_grad_colsharded('bwd_gate_up_w', hf, dgu, N_CHIPS, BF16))
    dx2, dx2b, gw['g_ffn'] = norm_bwd('norm_ffn_bwd', x2, vec('g_ffn') + t_down + t_gu, dhf, res=dy)

    do_x = _mm_colsharded_t('bwd_co_x', dx2b, w_co_g, BF16)
    t_co = reduce_begin('w_co', _mm_grad_colsharded('bwd_co_w', o_x, dx2b, N_CHIPS, BF16))
    dcq_raw, dckv, gw['g_cq'], gw['g_ck'] = xattn_bwd(cq_raw, ckv, vec('g_cq') + t_co, vec('g_ck'), do_x)
    dhq = _mm('bwd_cq_x', dcq_raw, w_cq_f, 'nt', F32)
    t_cq = reduce_begin('w_cq', _mm('bwd_cq_w', hq, dcq_raw, 'tn', BF16).reshape(N_CHIPS, D // N_CHIPS, XW))
    dmn = _mm('bwd_ckv_x', dckv, w_ckv_f, 'nt', F32)
    t_ckv = reduce_begin('w_ckv', _mm('bwd_ckv_w', mn, dckv, 'tn', BF16).reshape(N_CHIPS, D // N_CHIPS, 2 * XW))
    (gw['g_mem'],) = norm_bwd('norm_mem_bwd', ms, vec('g_mem'), dmn, want_dx=False)
    dx1, dx1b, gw['g_xattn'] = norm_bwd('norm_xq_bwd', x1, vec('g_xattn') + t_cq + t_ckv, dhq, res=dx2)

    dmix = _mm('bwd_out_x', dx1b, w_out_f, 'nt', F32)
    t_out = reduce_begin('w_out', _mm('bwd_out_w', mixn, dx1b, 'tn', BF16).reshape(N_CHIPS, 2 * FW // N_CHIPS, D))
    do_fox, delta, dy_lru, gw['g_fox_out'], gw['g_lru_out'] = mix_bwd(o_fox, y_lru, vec('g_fox_out') + t_out,
                                                                     vec('g_lru_out'), dmix)
    (du, dgate, gw['conv_w'], gw['conv_b'], gw['w_ra'], gw['b_ra'], gw['w_ri'], gw['b_ri'],
     gw['lam']) = lru_bwd(proj, dy_lru, *lru_w, u_off, g_off)
    dqn, delta2 = fox_bwd_q(qn, kn, vb, do_fox, cc, ct, lse, delta, T)
    dkn, dv, dct = fox_bwd_kv(qn, kn, vb, do_fox, cc, ct, lse, delta2, T)
    dq, dk, gw['g_q'], gw['g_k'] = qkv_bwd(proj, vec('g_q'), vec('g_k'), dqn, dkn, FW)
    dc = jnp.pad(dct.reshape(H, S).T, ((0, 0), (0, LANES - H)))
    df, db_f = fgate_bwd(f_raw, b_f_pad, dc, H)
    gw['b_f'] = db_f[:, :H]
    dproj = jnp.concatenate([dq, dk, dv, du, dgate], axis=1)
    dw5 = _mm('bwd_in_w', h1, dproj, 'tn', BF16)
    dwf = _mm('bwd_f_w', h1, df, 'tn', BF16)
    t_in = reduce_begin('w_in', wins.windows(jnp.concatenate([dw5[:, :3 * FW], dwf, dw5[:, 3 * FW:]], axis=1)))
    dh_a = _mm('bwd_f_x', df, wf, 'nt', F32)
    dh1 = _mm('bwd_in_x', dproj, w5, 'nt', F32, res=dh_a)
    grad_x, _, gw['g_mix'] = norm_bwd('norm_mix_bwd', xs, vec('g_mix') + t_in, dh1, res=dx1)

    grads, delta_w, new_m, new_v = {}, {}, {}, {}
    done = grad_x
    for n, part, zone, ssem, rsem in pending:
        part, landed = scatter_wait('scatter_wait_' + n, part, zone, ssem, rsem, done)
        mine = sum_parts('sum_parts_' + n, part, landed, chip_idx, c_idx)
        (other,) = pair_join('pair_join_' + n, [mine])
        if n == 'w_in':
            mine, other = wins.from_window(mine, chip), wins.from_window(other, chip)
        grads[n], delta_w[n], new_m[n], new_v[n] = adamw_halves('adamw_' + n, W[n], mine, other, M1[n], V1[n], c_idx)
        done = delta_w[n]

    small_shapes = [gw[n].shape for n in SMALL] + [(1, 1)]
    summed = _unpack(allreduce_small(_pack([gw[n] for n in SMALL] + [loss_blk[0:1, 0:1]]), delta_w[BIG[0]]), small_shapes)
    loss = summed[-1].reshape(())
    for n, g in zip(SMALL, summed):
        grads[n] = g.reshape(W[n].shape) if n != 'conv_w' else lax.dynamic_slice_in_dim(
            g, chip * (LW // N_CHIPS), LW // N_CHIPS, axis=1)
    packs = [_pack([d[n] for n in SMALL]) for d in (W, grads, M1, V1)]
    shapes = [W[n].shape for n in SMALL]
    for d, res in zip((delta_w, new_m, new_v), adamw('adamw_small', *packs)):
        d.update(zip(SMALL, _unpack(res, shapes)))

    lead = lambda d: [d[n][None] for n in WEIGHTS]
    return (loss, grad_x[None], *lead(grads), *lead(delta_w), *lead(new_m), *lead(new_v))
```

```python
import functools
import math

import jax
import jax.numpy as jnp
from jax import lax
from jax.experimental import pallas as pl
from jax.experimental.pallas import tpu as pltpu

F32 = jnp.float32
BF16 = jnp.bfloat16
HEAD_DIM = 128
LANES = 128
LRU_C = 8.0
RMS_EPS = 1e-6
CONV_W = 4
ADAM_LR = 0.001
ADAM_B1 = 0.9
ADAM_B2 = 0.999
ADAM_EPS = 1e-08
ADAM_WD = 0.01
ADAM_STEP = 10
VMEM_LIMIT = 56 * 1024 * 1024
N_CHIPS = 4
MESH = pl.DeviceIdType.MESH
ANY = pl.BlockSpec(memory_space=pl.ANY)

WEIGHTS = ['g_mix', 'w_in', 'b_f', 'g_q', 'g_k', 'conv_w', 'conv_b', 'w_ra', 'b_ra', 'w_ri', 'b_ri', 'lam',
           'g_fox_out', 'g_lru_out', 'w_out', 'g_xattn', 'g_mem', 'w_cq', 'w_ckv', 'g_cq', 'g_ck', 'w_co', 'g_ffn',
           'w_gate_up', 'w_down']
BIG = ['w_in', 'w_out', 'w_cq', 'w_ckv', 'w_co', 'w_gate_up', 'w_down']
SMALL = [n for n in WEIGHTS if n not in BIG]


def _params(sem=None):
    if sem is None:
        return pltpu.CompilerParams(vmem_limit_bytes=VMEM_LIMIT)
    return pltpu.CompilerParams(dimension_semantics=sem, vmem_limit_bytes=VMEM_LIMIT)


def _tile(n, cands):
    for t in cands:
        if n % t == 0:
            return t
    return n


ROW_BLOCK_BYTES = 1 << 20


def _row_tile(n_rows, n_cols, min_rows=8):
    cands = [t for t in (512, 256, 128, 64, 32, 16, 8) if t >= min_rows and t * n_cols * 4 <= ROW_BLOCK_BYTES]
    return _tile(n_rows, cands or [min_rows])


def _sigmoid(z):
    return 1.0 / (1.0 + jnp.exp(-z))


def _softplus(z):
    return jnp.maximum(z, 0.0) + jnp.log(1.0 + jnp.exp(-jnp.abs(z)))


def _neg_expm1(z):
    series = -z * (1.0 + z * (0.5 + z * (1.0 / 6.0 + z * (1.0 / 24.0 + z * (1.0 / 120.0)))))
    return jnp.where(z > -0.25, series, 1.0 - jnp.exp(z))


_GELU_K = math.sqrt(2.0 / math.pi)


def _gelu_and_grad(z):
    inner = _GELU_K * (z + 0.044715 * z * z * z)
    t = jnp.tanh(inner)
    g = 0.5 * z * (1.0 + t)
    dg = 0.5 * (1.0 + t) + 0.5 * z * (1.0 - t * t) * _GELU_K * (1.0 + 3.0 * 0.044715 * z * z)
    return g, dg


def _rms(xv, g):
    r = lax.rsqrt(jnp.mean(xv * xv, axis=-1, keepdims=True) + RMS_EPS)
    return xv * r * g


def _rms_bwd(xv, g, dy):
    r = lax.rsqrt(jnp.mean(xv * xv, axis=-1, keepdims=True) + RMS_EPS)
    xh = xv * r
    dyg = dy * g
    dx = r * (dyg - xh * jnp.mean(dyg * xh, axis=-1, keepdims=True))
    return dx, jnp.sum(dy * xh, axis=0, keepdims=True)


def _heads(fn, n_heads, *arrs):
    outs = [fn(*[a[:, h * HEAD_DIM:(h + 1) * HEAD_DIM] for a in arrs]) for h in range(n_heads)]
    first = jnp.concatenate([o[0] for o in outs], axis=1) if n_heads > 1 else outs[0][0]
    rest = [functools.reduce(lambda p, q: p + q, [o[i] for o in outs]) for i in range(1, len(outs[0]))]
    return (first, *rest)


def _split3(v):
    hi = v.astype(BF16)
    r1 = v - hi.astype(F32)
    mid = r1.astype(BF16)
    lo = (r1 - mid.astype(F32)).astype(BF16)
    return hi, mid, lo


def _acc_out(ref, first, val):
    @pl.when(first)
    def _():
        ref[...] = val

    @pl.when(jnp.logical_not(first))
    def _():
        ref[...] += val


_DIMS = {'nn': (((1,), (0,)), ((), ())), 'nt': (((1,), (1,)), ((), ())), 'tn': (((0,), (0,)), ((), ()))}


MM_VMEM_BYTES = 36 * 1024 * 1024


MXU_FLOPS = 800e12
HBM_BYTES_S = 3.2e12
VMEM_ADD_BYTES_S = 8e12
STEP_S = 0.35e-6


def _k_tile(K, tm, tn, a, b, o_dtype, res):
    fixed = tm * tn * (2 * jnp.dtype(o_dtype).itemsize + 4 + (8 if res is not None else 0))
    per_k = 2 * (tm * a.dtype.itemsize + tn * b.dtype.itemsize)
    per_k += 2 * tm * (a.dtype.itemsize > 2) + 2 * tn * (b.dtype.itemsize > 2)
    units = K // LANES
    for d in sorted((d for d in range(1, units + 1) if units % d == 0), reverse=True):
        if fixed + d * LANES * per_k <= MM_VMEM_BYTES:
            return d * LANES
    return None


def _mm_tiles(M, N, K, k_span, a, b, o_dtype, res, tn_cands=(2048, 1024, 512, 256, 128)):
    best = None
    for tm in (2048, 1024, 512, 256, 128):
        for tn in tn_cands:
            if M % tm or N % tn:
                continue
            tk = _k_tile(k_span, tm, tn, a, b, o_dtype, res)
            if tk is None:
                continue
            nk = K // tk
            traffic = (M * K * a.dtype.itemsize * (N // tn) + K * N * b.dtype.itemsize * (M // tm)
                       + M * N * (jnp.dtype(o_dtype).itemsize + (4 if res is not None else 0)))
            work = 2.0 * M * N * K / MXU_FLOPS + (M * N * 4 * nk / VMEM_ADD_BYTES_S if nk > 1 else 0.0)
            t = max(work, traffic / HBM_BYTES_S) + (M // tm) * (N // tn) * nk * STEP_S
            if best is None or t < best[0]:
                best = (t, tm, tn, tk)
    assert best is not None, (M, N, K)
    return best[1:]


def _mm_call(name, a, b, mode, grid, a_spec, b_spec, o_spec, o_shape, o_dtype, acc_shape, res=None):
    nk = grid[2]
    dn = _DIMS[mode]

    def body(*refs):
        a_ref, b_ref = refs[:2]
        r_ref = refs[2] if res is not None else None
        o_ref = refs[3] if res is not None else refs[2]
        part = lax.dot_general(a_ref[...].astype(BF16), b_ref[...].astype(BF16), dn, preferred_element_type=F32)

        def finish(r):
            if r_ref is not None:
                r = r + r_ref[...]
            o_ref[...] = r.astype(o_dtype)

        if nk == 1:
            finish(part)
            return
        acc = refs[-1]
        k = pl.program_id(2)

        @pl.when(k == 0)
        def _():
            acc[...] = part

        @pl.when(k > 0)
        def _():
            acc[...] += part

        @pl.when(k == nk - 1)
        def _():
            finish(acc[...])

    ins = [a, b] + ([] if res is None else [res])
    specs = [a_spec, b_spec] + ([] if res is None else [o_spec])
    return pl.pallas_call(
        body, name=name, grid=grid, in_specs=specs, out_specs=o_spec,
        out_shape=jax.ShapeDtypeStruct(o_shape, o_dtype),
        scratch_shapes=[] if nk == 1 else [pltpu.VMEM(acc_shape, F32)],
        compiler_params=_params(('parallel', 'parallel', 'arbitrary')))(*ins)


def _mm(name, a, b, mode, o_dtype, res=None):
    if mode == 'tn':
        K, M = a.shape
    else:
        M, K = a.shape
    N = b.shape[0] if mode == 'nt' else b.shape[1]
    tm, tn, tk = _mm_tiles(M, N, K, K, a, b, o_dtype, res)
    a_spec = (pl.BlockSpec((tk, tm), lambda m, n, k: (k, m)) if mode == 'tn'
              else pl.BlockSpec((tm, tk), lambda m, n, k: (m, k)))
    b_spec = (pl.BlockSpec((tn, tk), lambda m, n, k: (n, k)) if mode == 'nt'
              else pl.BlockSpec((tk, tn), lambda m, n, k: (k, n)))
    o_spec = pl.BlockSpec((tm, tn), lambda m, n, k: (m, n))
    return _mm_call(name, a, b, mode, (M // tm, N // tn, K // tk), a_spec, b_spec, o_spec, (M, N), o_dtype,
                    (tm, tn), res)


def _mm_colsharded(name, a, w, o_dtype, res=None):
    M, K = a.shape
    J, _, Nj = w.shape
    tm, tn, tk = _mm_tiles(M, J * Nj, K, K, a, w, o_dtype, res,
                           tn_cands=[t for t in (2816, 1408, 1024, 512, 256, 128) if Nj % t == 0])
    per = Nj // tn
    return _mm_call(name, a, w, 'nn', (M // tm, J * per, K // tk),
                    pl.BlockSpec((tm, tk), lambda m, n, k: (m, k)),
                    pl.BlockSpec((None, tk, tn), lambda m, n, k: (n // per, k, n % per)),
                    pl.BlockSpec((tm, tn), lambda m, n, k: (m, n)), (M, J * Nj), o_dtype, (tm, tn), res)


def _planes_spec(arr, rows, cols, row_of, col_of):
    if arr.ndim == 2:
        return pl.BlockSpec((rows, cols), lambda m, n, k: (row_of(m, n, k), col_of(m, n, k)))
    per_plane = arr.shape[2] // cols
    return pl.BlockSpec((None, rows, cols),
                        lambda m, n, k: (col_of(m, n, k) // per_plane, row_of(m, n, k), col_of(m, n, k) % per_plane))


def _mm_colsharded_t(name, a, w, o_dtype):
    M = a.shape[-2]
    J, K, Nj = w.shape
    tm, tn, tk = _mm_tiles(M, K, J * Nj, Nj, a, w, o_dtype, None)
    per = Nj // tk
    return _mm_call(name, a, w, 'nt', (M // tm, K // tn, J * per),
                    _planes_spec(a, tm, tk, lambda m, n, k: m, lambda m, n, k: k),
                    pl.BlockSpec((None, tn, tk), lambda m, n, k: (k // per, n, k % per)),
                    pl.BlockSpec((tm, tn), lambda m, n, k: (m, n)), (M, K), o_dtype, (tm, tn))


def _mm_grad_colsharded(name, a, dy, J, o_dtype):
    S, M = a.shape
    Nj = dy.shape[-1] * (dy.shape[0] if dy.ndim == 3 else 1) // J
    tm, tn, tk = _mm_tiles(M, J * Nj, S, S, a, dy, o_dtype, None,
                           tn_cands=[t for t in (2816, 1408, 1024, 512, 256, 128) if Nj % t == 0])
    per = Nj // tn
    return _mm_call(name, a, dy, 'tn', (M // tm, J * per, S // tk),
                    pl.BlockSpec((tk, tm), lambda m, n, k: (k, m)),
                    _planes_spec(dy, tk, tn, lambda m, n, k: k, lambda m, n, k: n),
                    pl.BlockSpec((None, tm, tn), lambda m, n, k: (n // per, m, n % per)), (J, M, Nj), o_dtype, (tm, tn))


def _rows_call(name, body, n_rows, tr, ins, outs):
    return pl.pallas_call(
        body, name=name, grid=(n_rows // tr,), in_specs=[s for _, s in ins], out_specs=[s for _, _, s in outs],
        out_shape=[jax.ShapeDtypeStruct(sh, dt) for sh, dt, _ in outs],
        compiler_params=_params(('arbitrary',)))(*[a for a, _ in ins])


def _rb(tr, w, cb=0):
    return pl.BlockSpec((tr, w), lambda i: (i, cb))


def _fb(shape):
    nd = len(shape)
    return pl.BlockSpec(shape, lambda i: (0,) * nd)


def norm_fwd(name, xv, g):
    S, D = xv.shape
    tr = _tile(S, (256, 128))

    def body(x_ref, g_ref, o_ref):
        o_ref[...] = _rms(x_ref[...], g_ref[...]).astype(BF16)

    return _rows_call(name, body, S, tr, [(xv, _rb(tr, D)), (g, _fb((1, D)))], [((S, D), BF16, _rb(tr, D))])[0]


def norm_bwd(name, xv, g, dy, res=None, want_dx=True):
    S, D = xv.shape
    tr = _tile(S, (256, 128))

    def body(*refs):
        if res is None:
            x_ref, g_ref, dy_ref = refs[:3]
            outs = refs[3:]
            r_ref = None
        else:
            x_ref, g_ref, dy_ref, r_ref = refs[:4]
            outs = refs[4:]
        dx, dg = _rms_bwd(x_ref[...], g_ref[...], dy_ref[...])
        if r_ref is not None:
            dx = dx + r_ref[...]
        if want_dx:
            outs[0][...] = dx
            outs[1][...] = dx.astype(BF16)
        _acc_out(outs[-1], pl.program_id(0) == 0, dg)

    ins = [(xv, _rb(tr, D)), (g, _fb((1, D))), (dy, _rb(tr, D))] + ([] if res is None else [(res, _rb(tr, D))])
    outs = ([((S, D), F32, _rb(tr, D)), ((S, D), BF16, _rb(tr, D))] if want_dx else []) + [((1, D), F32, _fb((1, D)))]
    return _rows_call(name, body, S, tr, ins, outs)


def qkv_fwd(proj, g_q, g_k, FW):
    S = proj.shape[0]
    H = FW // HEAD_DIM
    tr = _tile(S, (256, 128))

    def body(q_ref, k_ref, v_ref, gq_ref, gk_ref, qo, ko, vo):
        qo[...] = _heads(lambda t: (_rms(t, gq_ref[...]),), H, q_ref[...])[0].astype(BF16)
        ko[...] = _heads(lambda t: (_rms(t, gk_ref[...]),), H, k_ref[...])[0].astype(BF16)
        vo[...] = v_ref[...].astype(BF16)

    o = ((S, FW), BF16, _rb(tr, FW))
    return _rows_call('qkv_fwd', body, S, tr,
                      [(proj, _rb(tr, FW, 0)), (proj, _rb(tr, FW, 1)), (proj, _rb(tr, FW, 2)),
                       (g_q, _fb((1, HEAD_DIM))), (g_k, _fb((1, HEAD_DIM)))], [o, o, o])


def qkv_bwd(proj, g_q, g_k, dqn, dkn, FW):
    S = proj.shape[0]
    H = FW // HEAD_DIM
    tr = _tile(S, (256, 128))

    def body(q_ref, k_ref, gq_ref, gk_ref, dq_ref, dk_ref, dqo, dko, dgq, dgk):
        dq, gq = _heads(lambda t, d: _rms_bwd(t, gq_ref[...], d), H, q_ref[...], dq_ref[...])
        dk, gk = _heads(lambda t, d: _rms_bwd(t, gk_ref[...], d), H, k_ref[...], dk_ref[...])
        dqo[...] = dq.astype(BF16)
        dko[...] = dk.astype(BF16)
        first = pl.program_id(0) == 0
        _acc_out(dgq, first, gq)
        _acc_out(dgk, first, gk)

    o = ((S, FW), BF16, _rb(tr, FW))
    og = ((1, HEAD_DIM), F32, _fb((1, HEAD_DIM)))
    return _rows_call('qkv_bwd', body, S, tr,
                      [(proj, _rb(tr, FW, 0)), (proj, _rb(tr, FW, 1)), (g_q, _fb((1, HEAD_DIM))),
                       (g_k, _fb((1, HEAD_DIM))), (dqn, _rb(tr, FW)), (dkn, _rb(tr, FW))], [o, o, og, og])


def _tri(n, upper):
    r = lax.broadcasted_iota(jnp.int32, (n, n), 0)
    c = lax.broadcasted_iota(jnp.int32, (n, n), 1)
    return jnp.where((c >= r) if upper else (c <= r), 1.0, 0.0).astype(BF16)


def _blocked_cumsum(val, S, blk, reverse):
    tri = _tri(blk, reverse)
    order = range(S // blk - 1, -1, -1) if reverse else range(S // blk)
    carry = jnp.zeros((1, LANES), F32)
    outs = {}
    for bi in order:
        part = val[bi * blk:(bi + 1) * blk]
        acc = carry
        for piece in _split3(part):
            acc = acc + jnp.dot(tri, piece, preferred_element_type=F32)
        outs[bi] = acc
        carry = carry + jnp.sum(part, axis=0, keepdims=True)
    return jnp.concatenate([outs[bi] for bi in range(S // blk)], axis=0)


def fgate_fwd(f_raw, b_f_pad):
    S = f_raw.shape[0]
    blk = _tile(S, (256, 128))

    def body(f_ref, b_ref, c_ref):
        z = f_ref[...] + b_ref[...]
        c_ref[...] = _blocked_cumsum(-_softplus(-z), S, blk, False)

    return pl.pallas_call(body, name='fgate_fwd', grid=(1,), in_specs=[_fb((S, LANES)), _fb((1, LANES))],
                          out_specs=_fb((S, LANES)), out_shape=jax.ShapeDtypeStruct((S, LANES), F32),
                          compiler_params=_params(('arbitrary',)))(f_raw, b_f_pad)


def fgate_bwd(f_raw, b_f_pad, dc, H):
    S = f_raw.shape[0]
    blk = _tile(S, (256, 128))

    def body(f_ref, b_ref, dc_ref, df_ref, db_ref):
        z = f_ref[...] + b_ref[...]
        dlogf = _blocked_cumsum(dc_ref[...], S, blk, True)
        lane = lax.broadcasted_iota(jnp.int32, (S, LANES), 1)
        df = jnp.where(lane < H, dlogf * _sigmoid(-z), 0.0)
        df_ref[...] = df.astype(BF16)
        db_ref[...] = jnp.sum(df, axis=0, keepdims=True)

    return pl.pallas_call(body, name='fgate_bwd', grid=(1,),
                          in_specs=[_fb((S, LANES)), _fb((1, LANES)), _fb((S, LANES))],
                          out_specs=[_fb((S, LANES)), _fb((1, LANES))],
                          out_shape=[jax.ShapeDtypeStruct((S, LANES), BF16), jax.ShapeDtypeStruct((1, LANES), F32)],
                          compiler_params=_params(('arbitrary',)))(f_raw, b_f_pad, dc)


def _fox_logits(q, k, c_blk, ct_blk, h, T, diagonal):
    s = lax.dot_general(q, k, _DIMS['nt'], preferred_element_type=F32) * (1.0 / math.sqrt(HEAD_DIM))
    lane = lax.broadcasted_iota(jnp.int32, c_blk.shape, 1)
    cq = jnp.sum(jnp.where(lane == h, c_blk, 0.0), axis=1, keepdims=True)
    sub = lax.broadcasted_iota(jnp.int32, ct_blk.shape, 0)
    ck = jnp.sum(jnp.where(sub == h, ct_blk, 0.0), axis=0, keepdims=True)
    s = s + cq - ck
    if not diagonal:
        return s
    rows = lax.broadcasted_iota(jnp.int32, (T, T), 0)
    cols = lax.broadcasted_iota(jnp.int32, (T, T), 1)
    return jnp.where(cols <= rows, s, -jnp.inf)


def _below_and_on_diagonal(q_blk, k_blk, step):
    @pl.when(k_blk < q_blk)
    def _():
        step(False)

    @pl.when(k_blk == q_blk)
    def _():
        step(True)


def fox_fwd(qn, kn, vb, c, ct, T):
    S, FW = qn.shape
    H = FW // HEAD_DIM
    Hp = ct.shape[0]
    n = S // T

    def body(q_ref, k_ref, v_ref, c_ref, ct_ref, o_ref, lse_ref, m_s, l_s, acc_s):
        h, i, j = pl.program_id(0), pl.program_id(1), pl.program_id(2)

        @pl.when(j == 0)
        def _():
            m_s[...] = jnp.full_like(m_s, -jnp.inf)
            l_s[...] = jnp.zeros_like(l_s)
            acc_s[...] = jnp.zeros_like(acc_s)

        def step(diagonal):
            s = _fox_logits(q_ref[...], k_ref[...], c_ref[...], ct_ref[...], h, T, diagonal)
            m_new = jnp.maximum(m_s[...], jnp.max(s, axis=1, keepdims=True))
            alpha = jnp.exp(m_s[...] - m_new)
            p = jnp.exp(s - m_new)
            l_s[...] = alpha * l_s[...] + jnp.sum(p, axis=1, keepdims=True)
            acc_s[...] = alpha * acc_s[...] + jnp.dot(p.astype(BF16), v_ref[...], preferred_element_type=F32)
            m_s[...] = m_new

        _below_and_on_diagonal(i, j, step)

        @pl.when(j == i)
        def _():
            o_ref[...] = acc_s[...] / l_s[...]
            lse_ref[...] = jnp.broadcast_to(m_s[...] + jnp.log(l_s[...]), (T, LANES))

    qs = pl.BlockSpec((T, HEAD_DIM), lambda h, i, j: (i, h))
    ks = pl.BlockSpec((T, HEAD_DIM), lambda h, i, j: (jnp.minimum(j, i), h))
    return pl.pallas_call(
        body, name='fox_fwd', grid=(H, n, n),
        in_specs=[qs, ks, ks, pl.BlockSpec((T, LANES), lambda h, i, j: (i, 0)),
                  pl.BlockSpec((Hp, T), lambda h, i, j: (0, jnp.minimum(j, i)))],
        out_specs=[qs, pl.BlockSpec((None, T, LANES), lambda h, i, j: (h, i, 0))],
        out_shape=[jax.ShapeDtypeStruct((S, FW), F32), jax.ShapeDtypeStruct((H, S, LANES), F32)],
        scratch_shapes=[pltpu.VMEM((T, 1), F32), pltpu.VMEM((T, 1), F32), pltpu.VMEM((T, HEAD_DIM), F32)],
        compiler_params=_params(('parallel', 'parallel', 'arbitrary')))(qn, kn, vb, c, ct)


def _fox_p_ds(q_ref, k_ref, v_ref, do_ref, c_ref, ct_ref, lse_ref, dl_ref, h, T, diagonal):
    s = _fox_logits(q_ref[...], k_ref[...], c_ref[...], ct_ref[...], h, T, diagonal)
    p = jnp.exp(s - jnp.tile(lse_ref[...], (1, T // LANES)))
    dp = lax.dot_general(do_ref[...], v_ref[...], _DIMS['nt'], preferred_element_type=F32)
    ds = p * (dp - jnp.tile(dl_ref[...], (1, T // LANES)))
    return p, dp, ds


def fox_bwd_q(qn, kn, vb, do, c, ct, lse, dl, T):
    S, FW = qn.shape
    H = FW // HEAD_DIM
    Hp = ct.shape[0]
    n = S // T

    def body(q_ref, k_ref, v_ref, do_ref, c_ref, ct_ref, lse_ref, dl_ref, dq_ref, dl2_ref, acc_s, rs_s):
        h, i, j = pl.program_id(0), pl.program_id(1), pl.program_id(2)

        @pl.when(j == 0)
        def _():
            acc_s[...] = jnp.zeros_like(acc_s)
            rs_s[...] = jnp.zeros_like(rs_s)

        def step(diagonal):
            p, dp, ds = _fox_p_ds(q_ref, k_ref, v_ref, do_ref, c_ref, ct_ref, lse_ref, dl_ref, h, T, diagonal)
            acc_s[...] += jnp.dot(ds.astype(BF16), k_ref[...], preferred_element_type=F32)
            rs_s[...] += jnp.sum(p * dp, axis=1, keepdims=True)

        _below_and_on_diagonal(i, j, step)

        @pl.when(j == i)
        def _():
            dq_ref[...] = acc_s[...] * (1.0 / math.sqrt(HEAD_DIM))
            dl2_ref[...] = jnp.broadcast_to(rs_s[...], (T, LANES))

    qs = pl.BlockSpec((T, HEAD_DIM), lambda h, i, j: (i, h))
    ks = pl.BlockSpec((T, HEAD_DIM), lambda h, i, j: (jnp.minimum(j, i), h))
    st = pl.BlockSpec((None, T, LANES), lambda h, i, j: (h, i, 0))
    return pl.pallas_call(
        body, name='fox_bwd_q', grid=(H, n, n),
        in_specs=[qs, ks, ks, qs, pl.BlockSpec((T, LANES), lambda h, i, j: (i, 0)),
                  pl.BlockSpec((Hp, T), lambda h, i, j: (0, jnp.minimum(j, i))), st, st],
        out_specs=[qs, st], out_shape=[jax.ShapeDtypeStruct((S, FW), F32), jax.ShapeDtypeStruct((H, S, LANES), F32)],
        scratch_shapes=[pltpu.VMEM((T, HEAD_DIM), F32), pltpu.VMEM((T, 1), F32)],
        compiler_params=_params(('parallel', 'parallel', 'arbitrary')))(qn, kn, vb, do, c, ct, lse, dl)


def fox_bwd_kv(qn, kn, vb, do, c, ct, lse, dl, T):
    S, FW = qn.shape
    H = FW // HEAD_DIM
    Hp = ct.shape[0]
    n = S // T

    def body(q_ref, k_ref, v_ref, do_ref, c_ref, ct_ref, lse_ref, dl_ref, dk_ref, dv_ref, dc_ref, dk_s, dv_s, dc_s):
        h, j, i = pl.program_id(0), pl.program_id(1), pl.program_id(2)

        @pl.when(i == 0)
        def _():
            dk_s[...] = jnp.zeros_like(dk_s)
            dv_s[...] = jnp.zeros_like(dv_s)
            dc_s[...] = jnp.zeros_like(dc_s)

        def step(diagonal):
            p, _, ds = _fox_p_ds(q_ref, k_ref, v_ref, do_ref, c_ref, ct_ref, lse_ref, dl_ref, h, T, diagonal)
            dv_s[...] += lax.dot_general(p.astype(BF16), do_ref[...], _DIMS['tn'], preferred_element_type=F32)
            dk_s[...] += lax.dot_general(ds.astype(BF16), q_ref[...], _DIMS['tn'], preferred_element_type=F32)
            dc_s[...] += jnp.sum(ds, axis=0, keepdims=True)

        _below_and_on_diagonal(i, j, step)

        @pl.when(i == n - 1)
        def _():
            dk_ref[...] = dk_s[...] * (1.0 / math.sqrt(HEAD_DIM))
            dv_ref[...] = dv_s[...].astype(BF16)
            dc_ref[...] = -dc_s[...]

    qs = pl.BlockSpec((T, HEAD_DIM), lambda h, j, i: (jnp.maximum(i, j), h))
    ks = pl.BlockSpec((T, HEAD_DIM), lambda h, j, i: (j, h))
    st = pl.BlockSpec((None, T, LANES), lambda h, j, i: (h, jnp.maximum(i, j), 0))
    return pl.pallas_call(
        body, name='fox_bwd_kv', grid=(H, n, n),
        in_specs=[qs, ks, ks, qs, pl.BlockSpec((T, LANES), lambda h, j, i: (jnp.maximum(i, j), 0)),
                  pl.BlockSpec((Hp, T), lambda h, j, i: (0, j)), st, st],
        out_specs=[ks, ks, pl.BlockSpec((None, 1, T), lambda h, j, i: (h, 0, j))],
        out_shape=[jax.ShapeDtypeStruct((S, FW), F32), jax.ShapeDtypeStruct((S, FW), BF16),
                   jax.ShapeDtypeStruct((H, 1, S), F32)],
        scratch_shapes=[pltpu.VMEM((T, HEAD_DIM), F32), pltpu.VMEM((T, HEAD_DIM), F32), pltpu.VMEM((1, T), F32)],
        compiler_params=_params(('parallel', 'parallel', 'arbitrary')))(qn, kn, vb, do, c, ct, lse, dl)


def _shift_down(v, d, rows, fill):
    return jnp.where(rows >= d, pltpu.roll(v, d, 0), fill)


def _shift_up(v, d, rows, S, fill):
    return jnp.where(rows < S - d, pltpu.roll(v, S - d, 0), fill)


def _scan(a, b, rows, S, reverse):
    d = 1
    while d < S:
        if reverse:
            a_s, b_s = _shift_up(a, d, rows, S, 1.0), _shift_up(b, d, rows, S, 0.0)
        else:
            a_s, b_s = _shift_down(a, d, rows, 1.0), _shift_down(b, d, rows, 0.0)
        b = a * b_s + b
        a = a * a_s
        d *= 2
    return b


def _lru_forward(u, cw, cb, wra, bra, wri, bri, lam, rows):
    uc = cb + cw[CONV_W - 1] * u
    for d in range(1, CONV_W):
        uc = uc + cw[CONV_W - 1 - d] * _shift_down(u, d, rows, 0.0)
    ucb = uc.astype(BF16)
    r = _sigmoid(jnp.dot(ucb, wra.astype(BF16), preferred_element_type=F32) + bra)
    ig = _sigmoid(jnp.dot(ucb, wri.astype(BF16), preferred_element_type=F32) + bri)
    sp = _softplus(-lam)
    log_a = -LRU_C * r * sp
    a = jnp.exp(log_a)
    sq = jnp.sqrt(_neg_expm1(2.0 * log_a))
    iu = ig * uc
    hseq = _scan(a, sq * iu, rows, u.shape[0], False)
    return uc, ucb, r, ig, sp, a, sq, iu, hseq


def _lru_specs(S, n_u, n_g):
    col = lambda off: pl.BlockSpec((S, LANES), lambda cbk: (0, off + cbk))
    vec = pl.BlockSpec((1, LANES), lambda cbk: (0, cbk))
    mat = pl.BlockSpec((None, LANES, LANES), lambda cbk: (cbk, 0, 0))
    cw = pl.BlockSpec((CONV_W, LANES), lambda cbk: (0, cbk))
    return col, vec, mat, cw


def lru_fwd(proj, conv_w, conv_b, w_ra, b_ra, w_ri, b_ri, lam, u_off, g_off):
    S = proj.shape[0]
    nb = w_ra.shape[0]
    col, vec, mat, cws = _lru_specs(S, u_off, g_off)

    def body(u_ref, g_ref, cw_ref, cb_ref, wra_ref, bra_ref, wri_ref, bri_ref, lam_ref, y_ref):
        rows = lax.broadcasted_iota(jnp.int32, (S, LANES), 0)
        cw = [cw_ref[t:t + 1, :] for t in range(CONV_W)]
        hseq = _lru_forward(u_ref[...], cw, cb_ref[...], wra_ref[...], bra_ref[...], wri_ref[...],
                            bri_ref[...], lam_ref[...], rows)[-1]
        y_ref[...] = hseq * _gelu_and_grad(g_ref[...])[0]

    return pl.pallas_call(
        body, name='lru_fwd', grid=(nb,),
        in_specs=[col(u_off), col(g_off), cws, vec, mat, vec, mat, vec, vec], out_specs=col(0),
        out_shape=jax.ShapeDtypeStruct((S, nb * LANES), F32),
        compiler_params=_params(('parallel',)))(proj, proj, conv_w, conv_b, w_ra, b_ra, w_ri, b_ri, lam)


def lru_bwd(proj, dy, conv_w, conv_b, w_ra, b_ra, w_ri, b_ri, lam, u_off, g_off):
    S = proj.shape[0]
    nb = w_ra.shape[0]
    LW = nb * LANES
    col, vec, mat, cws = _lru_specs(S, u_off, g_off)

    def body(u_ref, g_ref, dy_ref, cw_ref, cb_ref, wra_ref, bra_ref, wri_ref, bri_ref, lam_ref,
             du_ref, dg_ref, dcw_ref, dcb_ref, dwra_ref, dbra_ref, dwri_ref, dbri_ref, dlam_ref):
        rows = lax.broadcasted_iota(jnp.int32, (S, LANES), 0)
        u, lam_v = u_ref[...], lam_ref[...]
        cw = [cw_ref[t:t + 1, :] for t in range(CONV_W)]
        wra, wri = wra_ref[...].astype(BF16), wri_ref[...].astype(BF16)
        uc, ucb, r, ig, sp, a, sq, iu, hseq = _lru_forward(u, cw, cb_ref[...], wra, bra_ref[...], wri, bri_ref[...],
                                                           lam_v, rows)
        gl, dgl = _gelu_and_grad(g_ref[...])
        dy_v = dy_ref[...]
        dg_ref[...] = (dy_v * hseq * dgl).astype(BF16)
        G = _scan(_shift_up(a, 1, rows, S, 0.0), dy_v * gl, rows, S, True)
        da = G * _shift_down(hseq, 1, rows, 0.0)
        diu = G * sq
        dsq = G * iu
        dlog_a = da * a - dsq * a * a / jnp.maximum(sq, 1e-30)
        dr = dlog_a * (-LRU_C * sp)
        dsp = jnp.sum(dlog_a * (-LRU_C * r), axis=0, keepdims=True)
        dlam_ref[...] = -dsp * _sigmoid(-lam_v)
        dzr = dr * r * (1.0 - r)
        dzi = diu * uc * ig * (1.0 - ig)
        dzrb, dzib = dzr.astype(BF16), dzi.astype(BF16)
        duc = (diu * ig + lax.dot_general(dzrb, wra, _DIMS['nt'], preferred_element_type=F32)
               + lax.dot_general(dzib, wri, _DIMS['nt'], preferred_element_type=F32))
        dwra_ref[...] = lax.dot_general(ucb, dzrb, _DIMS['tn'], preferred_element_type=F32)
        dwri_ref[...] = lax.dot_general(ucb, dzib, _DIMS['tn'], preferred_element_type=F32)
        dbra_ref[...] = jnp.sum(dzr, axis=0, keepdims=True)
        dbri_ref[...] = jnp.sum(dzi, axis=0, keepdims=True)
        dcb_ref[...] = jnp.sum(duc, axis=0, keepdims=True)
        du = cw[CONV_W - 1] * duc
        dcw_ref[CONV_W - 1:CONV_W, :] = jnp.sum(duc * u, axis=0, keepdims=True)
        for d in range(1, CONV_W):
            du = du + cw[CONV_W - 1 - d] * _shift_up(duc, d, rows, S, 0.0)
            dcw_ref[CONV_W - 1 - d:CONV_W - d, :] = jnp.sum(duc * _shift_down(u, d, rows, 0.0), axis=0, keepdims=True)
        du_ref[...] = du.astype(BF16)

    sd = jax.ShapeDtypeStruct
    return pl.pallas_call(
        body, name='lru_bwd', grid=(nb,),
        in_specs=[col(u_off), col(g_off), col(0), cws, vec, mat, vec, mat, vec, vec],
        out_specs=[col(0), col(0), cws, vec, mat, vec, mat, vec, vec],
        out_shape=[sd((S, LW), BF16), sd((S, LW), BF16), sd((CONV_W, LW), F32), sd((1, LW), F32),
                   sd((nb, LANES, LANES), F32), sd((1, LW), F32), sd((nb, LANES, LANES), F32), sd((1, LW), F32),
                   sd((1, LW), F32)],
        compiler_params=_params(('parallel',)))(proj, proj, dy, conv_w, conv_b, w_ra, b_ra, w_ri, b_ri, lam)


def mix_fwd(o_fox, y_lru, g_fox, g_lru):
    S, FW = o_fox.shape
    tr = _tile(S, (256, 128))

    def body(o_ref, y_ref, gf_ref, gl_ref, m_ref):
        m_ref[...] = jnp.concatenate([_rms(o_ref[...], gf_ref[...]), _rms(y_ref[...], gl_ref[...])],
                                     axis=1).astype(BF16)

    return _rows_call('mix_fwd', body, S, tr,
                      [(o_fox, _rb(tr, FW)), (y_lru, _rb(tr, FW)), (g_fox, _fb((1, FW))), (g_lru, _fb((1, FW)))],
                      [((S, 2 * FW), BF16, _rb(tr, 2 * FW))])[0]


def mix_bwd(o_fox, y_lru, g_fox, g_lru, dmix):
    S, FW = o_fox.shape
    H = FW // HEAD_DIM
    tr = _tile(S, (256, 128))

    def body(o_ref, y_ref, gf_ref, gl_ref, df_ref, dl_ref, do_ref, dlt_ref, dy_ref, dgf_ref, dgl_ref):
        o = o_ref[...]
        do, dgf = _rms_bwd(o, gf_ref[...], df_ref[...])
        dyl, dgl = _rms_bwd(y_ref[...], gl_ref[...], dl_ref[...])
        do_ref[...] = do.astype(BF16)
        dy_ref[...] = dyl
        prod = do * o
        for h in range(H):
            dlt_ref[h] = jnp.broadcast_to(
                jnp.sum(prod[:, h * HEAD_DIM:(h + 1) * HEAD_DIM], axis=1, keepdims=True), (tr, LANES))
        first = pl.program_id(0) == 0
        _acc_out(dgf_ref, first, dgf)
        _acc_out(dgl_ref, first, dgl)

    g = _fb((1, FW))
    return _rows_call('mix_bwd', body, S, tr,
                      [(o_fox, _rb(tr, FW)), (y_lru, _rb(tr, FW)), (g_fox, g), (g_lru, g), (dmix, _rb(tr, FW, 0)),
                       (dmix, _rb(tr, FW, 1))],
                      [((S, FW), BF16, _rb(tr, FW)), ((H, S, LANES), F32, pl.BlockSpec((H, tr, LANES), lambda i: (0, i, 0))),
                       ((S, FW), F32, _rb(tr, FW)), ((1, FW), F32, g), ((1, FW), F32, g)])


def _xattn_heads(cq_raw, ckv, g_cq, g_ck, XW):
    out = []
    for h in range(XW // HEAD_DIM):
        sl = slice(h * HEAD_DIM, (h + 1) * HEAD_DIM)
        out.append((cq_raw[:, sl], _rms(cq_raw[:, sl], g_cq), ckv[:, sl], _rms(ckv[:, sl], g_ck),
                    ckv[:, XW + h * HEAD_DIM:XW + (h + 1) * HEAD_DIM].astype(BF16)))
    return out


def xattn_fwd(cq_raw, ckv, g_cq, g_ck):
    S, XW = cq_raw.shape
    M = ckv.shape[0]
    tr = _tile(S, (512, 256, 128))

    def body(q_ref, kv_ref, gq_ref, gk_ref, o_ref):
        outs = []
        for _, qn, _, kn, v in _xattn_heads(q_ref[...], kv_ref[...], gq_ref[...], gk_ref[...], XW):
            s = lax.dot_general(qn.astype(BF16), kn.astype(BF16), _DIMS['nt'], preferred_element_type=F32)
            s = s / math.sqrt(HEAD_DIM)
            p = jnp.exp(s - jnp.max(s, axis=1, keepdims=True))
            p = p / jnp.sum(p, axis=1, keepdims=True)
            outs.append(jnp.dot(p.astype(BF16), v, preferred_element_type=F32))
        o_ref[...] = jnp.concatenate(outs, axis=1).astype(BF16)

    g = _fb((1, HEAD_DIM))
    return _rows_call('xattn_fwd', body, S, tr,
                      [(cq_raw, _rb(tr, XW)), (ckv, _fb((M, 2 * XW))), (g_cq, g), (g_ck, g)],
                      [((S, XW), BF16, _rb(tr, XW))])[0]


def xattn_bwd(cq_raw, ckv, g_cq, g_ck, do):
    S, XW = cq_raw.shape
    M = ckv.shape[0]
    tr = _tile(S, (512, 256, 128))
    n = S // tr

    def body(q_ref, kv_ref, gq_ref, gk_ref, do_ref, dq_ref, dkv_ref, dgq_ref, dgk_ref):
        i = pl.program_id(0)
        do_v = do_ref[...]
        dqs, dkn, dvs = [], [], []
        dgq = jnp.zeros((1, HEAD_DIM), F32)
        for h, (q_raw, qn, _, kn, v) in enumerate(_xattn_heads(q_ref[...], kv_ref[...], gq_ref[...], gk_ref[...], XW)):
            qb, kb = qn.astype(BF16), kn.astype(BF16)
            doh = do_v[:, h * HEAD_DIM:(h + 1) * HEAD_DIM]
            s = lax.dot_general(qb, kb, _DIMS['nt'], preferred_element_type=F32) / math.sqrt(HEAD_DIM)
            p = jnp.exp(s - jnp.max(s, axis=1, keepdims=True))
            p = p / jnp.sum(p, axis=1, keepdims=True)
            dp = lax.dot_general(doh, v, _DIMS['nt'], preferred_element_type=F32)
            ds = (p * (dp - jnp.sum(p * dp, axis=1, keepdims=True)) / math.sqrt(HEAD_DIM)).astype(BF16)
            dvs.append(lax.dot_general(p.astype(BF16), doh, _DIMS['tn'], preferred_element_type=F32))
            dkn.append(lax.dot_general(ds, qb, _DIMS['tn'], preferred_element_type=F32))
            dq, g1 = _rms_bwd(q_raw, gq_ref[...], jnp.dot(ds, kb, preferred_element_type=F32))
            dqs.append(dq)
            dgq = dgq + g1
        dq_ref[...] = jnp.concatenate(dqs, axis=1).astype(BF16)
        first = i == 0
        _acc_out(dgq_ref, first, dgq)
        _acc_out(dkv_ref, first, jnp.concatenate(dkn + dvs, axis=1))

        @pl.when(i == n - 1)
        def _():
            kv = kv_ref[...]
            acc = dkv_ref[...]
            dk, gk = _heads(lambda t, d: _rms_bwd(t, gk_ref[...], d), XW // HEAD_DIM, kv[:, :XW], acc[:, :XW])
            dkv_ref[:, :XW] = dk
            dgk_ref[...] = gk

    g = _fb((1, HEAD_DIM))
    return _rows_call('xattn_bwd', body, S, tr,
                      [(cq_raw, _rb(tr, XW)), (ckv, _fb((M, 2 * XW))), (g_cq, g), (g_ck, g), (do, _rb(tr, XW))],
                      [((S, XW), BF16, _rb(tr, XW)), ((M, 2 * XW), F32, _fb((M, 2 * XW))), ((1, HEAD_DIM), F32, g),
                       ((1, HEAD_DIM), F32, g)])


def swiglu_fwd(gu, F):
    S = gu.shape[0]
    tr = _tile(S, (256, 128))
    tf = _tile(F, (1408, 1024, 512, 256, 128))
    nf = F // tf

    def body(g_ref, u_ref, a_ref):
        g = g_ref[...]
        a_ref[...] = (g * _sigmoid(g) * u_ref[...]).astype(BF16)

    return pl.pallas_call(
        body, name='swiglu_fwd', grid=(S // tr, nf),
        in_specs=[pl.BlockSpec((tr, tf), lambda i, n: (i, n)), pl.BlockSpec((tr, tf), lambda i, n: (i, n + nf))],
        out_specs=pl.BlockSpec((tr, tf), lambda i, n: (i, n)), out_shape=jax.ShapeDtypeStruct((S, F), BF16),
        compiler_params=_params(('parallel', 'parallel')))(gu, gu)


def swiglu_bwd(gu, dact, F, after):
    S = gu.shape[0]
    tr = _tile(S, (256, 128))
    tf = _tile(F, (1408, 1024, 512, 256, 128))
    nf = F // tf

    def body(g_ref, u_ref, da_ref, after_ref, o_ref):
        g, da = g_ref[...], da_ref[...]
        sg = _sigmoid(g)
        o_ref[0] = (da * u_ref[...] * sg * (1.0 + g * (1.0 - sg))).astype(BF16)
        o_ref[1] = (da * g * sg).astype(BF16)

    return pl.pallas_call(
        body, name='swiglu_bwd', grid=(S // tr, nf),
        in_specs=[pl.BlockSpec((tr, tf), lambda i, n: (i, n)), pl.BlockSpec((tr, tf), lambda i, n: (i, n + nf)),
                  pl.BlockSpec((tr, tf), lambda i, n: (i, n)), ANY],
        out_specs=pl.BlockSpec((2, tr, tf), lambda i, n: (0, i, n)), out_shape=jax.ShapeDtypeStruct((2, S, F), BF16),
        compiler_params=_params(('parallel', 'parallel')))(gu, gu, dact, after)


def loss_head(y, target):
    S, D = y.shape
    tr = _tile(S, (256, 128))

    def body(y_ref, t_ref, d_ref, db_ref, l_ref):
        err = y_ref[...] - t_ref[...]
        d = err * (1.0 / D)
        d_ref[...] = d
        db_ref[...] = d.astype(BF16)
        part = jnp.sum(jnp.sum(err * err, axis=1, keepdims=True), axis=0, keepdims=True) * (0.5 / D)
        _acc_out(l_ref, pl.program_id(0) == 0, jnp.broadcast_to(part, (1, LANES)))

    return _rows_call('loss_head', body, S, tr, [(y, _rb(tr, D)), (target, _rb(tr, D))],
                      [((S, D), F32, _rb(tr, D)), ((S, D), BF16, _rb(tr, D)), ((1, LANES), F32, _fb((1, LANES)))])


def _adamw_math(w, gv, m, v):
    mn = ADAM_B1 * m + (1.0 - ADAM_B1) * gv
    vn = ADAM_B2 * v + (1.0 - ADAM_B2) * (gv * gv)
    m_hat = mn / (1.0 - ADAM_B1 ** ADAM_STEP)
    v_hat = vn / (1.0 - ADAM_B2 ** ADAM_STEP)
    return -ADAM_LR * (m_hat / (jnp.sqrt(v_hat) + ADAM_EPS) + ADAM_WD * w), mn, vn


def adamw(name, w, g, m, v):
    R, C = w.shape
    tr = _row_tile(R, C)

    def body(w_ref, g_ref, m_ref, v_ref, d_ref, mo_ref, vo_ref):
        d_ref[...], mo_ref[...], vo_ref[...] = _adamw_math(w_ref[...], g_ref[...], m_ref[...], v_ref[...])

    spec = _rb(tr, C)
    return _rows_call(name, body, R, tr, [(w, spec), (g, spec), (m, spec), (v, spec)], [((R, C), F32, spec)] * 3)


def adamw_halves(name, w, mine, other, m, v, c_idx):
    R, C = w.shape
    hr = R // 2
    tr = _row_tile(hr, C)

    def body(c_ref, w_ref, a_ref, b_ref, m_ref, v_ref, g_ref, d_ref, mo_ref, vo_ref):
        gv = jnp.where(pl.program_id(0) == c_ref[0], a_ref[...], b_ref[...])
        g_ref[...] = gv
        d_ref[...], mo_ref[...], vo_ref[...] = _adamw_math(w_ref[...], gv, m_ref[...], v_ref[...])

    full = pl.BlockSpec((None, tr, C), lambda hh, i, c_ref: (hh, i, 0))
    mine_spec = pl.BlockSpec((tr, C), lambda hh, i, c_ref: (jnp.where(hh == c_ref[0], i, 0), 0))
    other_spec = pl.BlockSpec((tr, C), lambda hh, i, c_ref: (jnp.where(hh == c_ref[0], 0, i), 0))
    outs = pl.pallas_call(
        body, name=name,
        grid_spec=pltpu.PrefetchScalarGridSpec(num_scalar_prefetch=1, grid=(2, hr // tr),
                                               in_specs=[full, mine_spec, other_spec, full, full], out_specs=[full] * 4),
        out_shape=[jax.ShapeDtypeStruct((2, hr, C), F32)] * 4,
        compiler_params=_params(('parallel', 'parallel')))(
            c_idx, w.reshape(2, hr, C), mine, other, m.reshape(2, hr, C), v.reshape(2, hr, C))
    return [o.reshape(R, C) for o in outs]


def _place():
    x, y, c = lax.axis_index('x'), lax.axis_index('y'), lax.axis_index('c')
    return x, y, c, [(1 - x, y), (x, 1 - y), (1 - x, 1 - y)]


def _rcopy(src, dst, ssem, rsem, dev):
    return pltpu.make_async_remote_copy(src_ref=src, dst_ref=dst, send_sem=ssem, recv_sem=rsem, device_id=dev,
                                        device_id_type=MESH)


HBM = pl.BlockSpec(memory_space=pltpu.HBM)
SEM = pl.BlockSpec(memory_space=pltpu.SEMAPHORE)
EFFECT = pltpu.SideEffectType.DATAFLOW_SIDE_EFFECTING


def _in_hbm(a):
    return pltpu.with_memory_space_constraint(a, pltpu.HBM)


def _rows_part(shape, whole, half):
    return pl.ds(0, shape[0]) if whole else pl.ds(half * (shape[0] // 2), shape[0] // 2)


def gather_start(name, shards, whole):
    nT = len(shards)

    def body(*refs):
        srcs, lands = refs[:nT], refs[nT:2 * nT]
        ssem, rsem, token = refs[2 * nT], refs[2 * nT + 1], refs[-1]
        x, y, c, chips = _place()
        for t in range(nT):
            rows = _rows_part(shards[t].shape, whole[t], c)
            for k, (px, py) in enumerate(chips):
                _rcopy(srcs[t].at[rows], lands[t].at[2 * x + y, rows], ssem.at[3 * t + k], rsem.at[3 * t + k],
                       (px, py, c)).start()
        token[...] = jnp.zeros_like(token)

    zones = [lax.empty((N_CHIPS,) + s.shape, s.dtype) for s in shards]
    outs = pl.pallas_call(
        body, name=name,
        out_shape=(pltpu.SemaphoreType.DMA((3 * nT,)), pltpu.SemaphoreType.DMA((3 * nT,)),
                   *[pltpu.HBM(s.shape, s.dtype) for s in shards], *[pltpu.HBM(z.shape, z.dtype) for z in zones],
                   jax.ShapeDtypeStruct((8, LANES), F32)),
        in_specs=[HBM] * (2 * nT), out_specs=(SEM, SEM, *[HBM] * (2 * nT), pl.BlockSpec(memory_space=pltpu.VMEM)),
        input_output_aliases={i: 2 + i for i in range(2 * nT)},
        compiler_params=pltpu.CompilerParams(has_side_effects=EFFECT))(*[_in_hbm(a) for a in list(shards) + zones])
    return outs[0], outs[1], outs[2:2 + nT], outs[2 + nT:2 + 2 * nT], outs[-1]


def gather_wait(name, t, shard, zone, ssem, rsem, after, whole):
    after = after if isinstance(after, (list, tuple)) else [after]

    def body(src_ref, land_ref, ssem_ref, rsem_ref, *rest):
        x, y, c, chips = _place()
        rows = _rows_part(shard.shape, whole, c)
        for k, (px, py) in enumerate(chips):
            cp = _rcopy(src_ref.at[rows], land_ref.at[2 * px + py, rows], ssem_ref.at[3 * t + k], rsem_ref.at[3 * t + k],
                        (px, py, c))
            cp.wait_send()
            cp.wait_recv()

    return pl.pallas_call(
        body, name=name, out_shape=(pltpu.HBM(shard.shape, shard.dtype), pltpu.HBM(zone.shape, zone.dtype)),
        in_specs=(HBM, HBM, SEM, SEM, *[ANY] * len(after)), out_specs=(HBM, HBM), input_output_aliases={0: 0, 1: 1},
        compiler_params=pltpu.CompilerParams(has_side_effects=EFFECT))(shard, zone, ssem, rsem, *after)


def pair_swap(name, zone):
    hr = zone.shape[1] // 2

    def body(z_in, z_ref, ssem, rsem):
        x, y, c, chips = _place()
        cps = []
        for k, (px, py) in enumerate(chips):
            blk = z_ref.at[2 * px + py, pl.ds(c * hr, hr)]
            cps.append(_rcopy(blk, blk, ssem.at[k], rsem.at[k], (x, y, 1 - c)))
            cps[-1].start()
        for k, (px, py) in enumerate(chips):
            blk = z_ref.at[2 * px + py, pl.ds((1 - c) * hr, hr)]
            _rcopy(blk, blk, ssem.at[k], rsem.at[k], (x, y, 1 - c)).wait_recv()
        for cp in cps:
            cp.wait_send()

    return pl.pallas_call(
        body, name=name, in_specs=[ANY], out_specs=ANY, out_shape=jax.ShapeDtypeStruct(zone.shape, zone.dtype),
        input_output_aliases={0: 0},
        scratch_shapes=[pltpu.SemaphoreType.DMA((3,)), pltpu.SemaphoreType.DMA((3,))],
        compiler_params=_params())(zone)


N_SENDERS = 7


def _scatter_copies(g_ref, l_ref, ssem, rsem):
    x, y, c, chips = _place()
    cps = []
    for k, (px, py) in enumerate(chips):
        for d in range(2):
            to = (c + d) % 2
            cps.append(_rcopy(g_ref.at[2 * px + py, to], l_ref.at[2 * k + d], ssem.at[2 * k + d], rsem.at[2 * k + d],
                              (px, py, to)))
    cps.append(_rcopy(g_ref.at[2 * x + y, 1 - c], l_ref.at[6], ssem.at[6], rsem.at[6], (x, y, 1 - c)))
    return cps


def scatter_start(name, g):
    def body(g_ref, l_ref, ssem, rsem, g_out, l_out, token):
        for cp in _scatter_copies(g_ref, l_ref, ssem, rsem):
            cp.start()
        token[...] = jnp.zeros_like(token)

    zone = lax.empty((N_SENDERS,) + g.shape[2:], g.dtype)
    return pl.pallas_call(
        body, name=name,
        out_shape=(pltpu.SemaphoreType.DMA((N_SENDERS,)), pltpu.SemaphoreType.DMA((N_SENDERS,)),
                   pltpu.HBM(g.shape, g.dtype), pltpu.HBM(zone.shape, zone.dtype), jax.ShapeDtypeStruct((8, LANES), F32)),
        in_specs=[HBM, HBM], out_specs=(SEM, SEM, HBM, HBM, pl.BlockSpec(memory_space=pltpu.VMEM)),
        input_output_aliases={0: 2, 1: 3},
        compiler_params=pltpu.CompilerParams(has_side_effects=EFFECT))(_in_hbm(g), _in_hbm(zone))


def scatter_wait(name, g, zone, ssem, rsem, after):
    def body(g_ref, l_ref, ssem_ref, rsem_ref, after_ref, g_out, l_out):
        for cp in _scatter_copies(g_ref, l_ref, ssem_ref, rsem_ref):
            cp.wait_send()
            cp.wait_recv()

    return pl.pallas_call(
        body, name=name, out_shape=(pltpu.HBM(g.shape, g.dtype), pltpu.HBM(zone.shape, zone.dtype)),
        in_specs=(HBM, HBM, SEM, SEM, ANY), out_specs=(HBM, HBM), input_output_aliases={0: 0, 1: 1},
        compiler_params=pltpu.CompilerParams(has_side_effects=EFFECT))(g, zone, ssem, rsem, after)


def sum_parts(name, g, landed, chip_idx, c_idx):
    hr, C = g.shape[2:]
    tr = _row_tile(hr, C, min_rows=16)

    def body(me_ref, c_ref, g_ref, l_ref, o_ref):
        acc = g_ref[...].astype(F32)
        for s in range(N_SENDERS):
            acc = acc + l_ref[s].astype(F32)
        o_ref[...] = acc

    return pl.pallas_call(
        body, name=name,
        grid_spec=pltpu.PrefetchScalarGridSpec(
            num_scalar_prefetch=2, grid=(hr // tr,),
            in_specs=[pl.BlockSpec((None, None, tr, C), lambda i, me_ref, c_ref: (me_ref[0], c_ref[0], i, 0)),
                      pl.BlockSpec((N_SENDERS, tr, C), lambda i, me_ref, c_ref: (0, i, 0))],
            out_specs=pl.BlockSpec((tr, C), lambda i, me_ref, c_ref: (i, 0))),
        out_shape=jax.ShapeDtypeStruct((hr, C), F32),
        compiler_params=_params(('parallel',)))(chip_idx, c_idx, g, landed)


def pair_join(name, halves):
    nT = len(halves)

    def body(*refs):
        ins, outs = refs[:nT], refs[nT:2 * nT]
        ssem, rsem = refs[2 * nT:]
        x, y, c, _ = _place()
        cps = [_rcopy(ins[t], outs[t], ssem.at[t], rsem.at[t], (x, y, 1 - c)) for t in range(nT)]
        for cp in cps:
            cp.start()
        for cp in cps:
            cp.wait()

    return pl.pallas_call(
        body, name=name, in_specs=[ANY] * nT, out_specs=[ANY] * nT,
        out_shape=[jax.ShapeDtypeStruct(h.shape, h.dtype) for h in halves],
        scratch_shapes=[pltpu.SemaphoreType.DMA((nT,)), pltpu.SemaphoreType.DMA((nT,))],
        compiler_params=_params())(*halves)


def allreduce_small(buf, after):
    R = buf.shape[0]
    VM = pl.BlockSpec(memory_space=pltpu.VMEM)

    def body(x_ref, after_ref, o_ref, all_ref, ssem, rsem, lsem):
        x, y, c, chips = _place()
        me, sibling = (x, y, c), (x, y, 1 - c)

        def rows(px, py, pc):
            return all_ref.at[pl.ds((4 * px + 2 * py + pc) * R, R), :]

        def copy(k, block, to, src=None):
            return _rcopy(rows(*block) if src is None else src, rows(*block), ssem.at[k], rsem.at[k], to)

        mine = pltpu.make_async_copy(x_ref, rows(*me), lsem)
        mine.start()
        first = [copy(0, me, sibling, src=x_ref)]
        first += [copy(1 + k, me, (*chip, c), src=x_ref) for k, chip in enumerate(chips)]
        for cp in first:
            cp.start()
        passed = [copy(4 + k, (*chip, c), sibling) for k, chip in enumerate(chips)]
        for k, chip in enumerate(chips):
            copy(1 + k, (*chip, c), me).wait_recv()
            passed[k].start()
        copy(0, sibling, me).wait_recv()
        for k, chip in enumerate(chips):
            copy(4 + k, (*chip, 1 - c), me).wait_recv()
        for cp in first + passed:
            cp.wait_send()
        mine.wait()
        acc = all_ref[0:R, :]
        for d in range(1, 8):
            acc = acc + all_ref[d * R:(d + 1) * R, :]
        o_ref[...] = acc

    return pl.pallas_call(
        body, name='allreduce_small', in_specs=[VM, ANY], out_specs=VM, out_shape=jax.ShapeDtypeStruct((R, LANES), F32),
        scratch_shapes=[pltpu.VMEM((8 * R, LANES), F32), pltpu.SemaphoreType.DMA((7,)), pltpu.SemaphoreType.DMA((7,)),
                        pltpu.SemaphoreType.DMA],
        compiler_params=_params())(buf, after)


class _InWindows:
    def __init__(self, FW, LW, H, C):
        gap = LANES - H
        padded = lambda o: o if o < 3 * FW + H else o + gap
        self.width = 3 * FW + LANES + 2 * LW
        self.first = [padded(C * j) // LANES for j in range(N_CHIPS)]
        self.blocks = max(padded(C * (j + 1) - 1) // LANES - self.first[j] + 1 for j in range(N_CHIPS))
        assert all((b + self.blocks) * LANES <= self.width for b in self.first)
        self.cols = self.blocks * LANES
        self.runs = []
        for j in range(N_CHIPS):
            cut = min(max(3 * FW + H - C * j, 0), C)
            spans = [(0, cut), (cut, C)]
            self.runs.append([(t0, t1, padded(C * j + t0) - LANES * self.first[j]) for t0, t1 in spans if t1 > t0])

    def to_window(self, shard, chip):
        def place(j, s):
            parts, pos = [], 0
            for t0, t1, w0 in self.runs[j]:
                parts += [jnp.zeros((s.shape[0], w0 - pos), s.dtype), s[:, t0:t1]]
                pos = w0 + t1 - t0
            parts.append(jnp.zeros((s.shape[0], self.cols - pos), s.dtype))
            return jnp.concatenate([p for p in parts if p.shape[1]], axis=1)
        return lax.switch(chip, [functools.partial(place, j) for j in range(N_CHIPS)], shard)

    def from_window(self, win, chip):
        def take(j, w):
            return jnp.concatenate([w[:, w0:w0 + t1 - t0] for t0, t1, w0 in self.runs[j]], axis=1)
        return lax.switch(chip, [functools.partial(take, j) for j in range(N_CHIPS)], win)

    def assemble(self, zone):
        total = None
        for j in range(N_CHIPS):
            lead = self.first[j] * LANES
            part = jnp.pad(zone[j], ((0, 0), (lead, self.width - lead - self.cols)))
            total = part if total is None else total + part
        return total

    def windows(self, padded_matrix):
        return jnp.stack([padded_matrix[:, b * LANES:b * LANES + self.cols] for b in self.first])


_PACK = 8 * LANES


def _pack(arrs):
    flat = []
    for a in arrs:
        v = a.reshape(-1).astype(F32)
        flat.append(jnp.pad(v, (0, (-v.shape[0]) % _PACK)))
    return jnp.concatenate(flat).reshape(-1, LANES)


def _unpack(buf, shapes):
    out, off = [], 0
    flat = buf.reshape(-1)
    for sh in shapes:
        n = math.prod(sh)
        out.append(flat[off:off + n].reshape(sh))
        off += n + (-n) % _PACK
    return out


def kernel(x, mem, g_mix, w_in, b_f, g_q, g_k, conv_w, conv_b, w_ra, b_ra, w_ri, b_ri, lam, g_fox_out, g_lru_out, w_out, g_xattn, g_mem, w_cq, w_ckv, g_cq, g_ck, w_co, g_ffn, w_gate_up, w_down, loss_target, m_g_mix, m_w_in, m_b_f, m_g_q, m_g_k, m_conv_w, m_conv_b, m_w_ra, m_b_ra, m_w_ri, m_b_ri, m_lam, m_g_fox_out, m_g_lru_out, m_w_out, m_g_xattn, m_g_mem, m_w_cq, m_w_ckv, m_g_cq, m_g_ck, m_w_co, m_g_ffn, m_w_gate_up, m_w_down, v_g_mix, v_w_in, v_b_f, v_g_q, v_g_k, v_conv_w, v_conv_b, v_w_ra, v_b_ra, v_w_ri, v_b_ri, v_lam, v_g_fox_out, v_g_lru_out, v_w_out, v_g_xattn, v_g_mem, v_w_cq, v_w_ckv, v_g_cq, v_g_ck, v_w_co, v_g_ffn, v_w_gate_up, v_w_down):
    given = dict(locals())
    W = {n: given[n][0] for n in WEIGHTS}
    M1 = {n: given['m_' + n][0] for n in WEIGHTS}
    V1 = {n: given['v_' + n][0] for n in WEIGHTS}
    xs, ms, tgt = x[0], mem[0], loss_target[0]
    S, D = xs.shape
    H = W['b_f'].shape[0]
    FW = H * HEAD_DIM
    LW = W['lam'].shape[0]
    nb = W['w_ra'].shape[0]
    XW = W['w_cq'].shape[1]
    F = W['w_down'].shape[0] * N_CHIPS
    IN_W = W['w_in'].shape[1] * N_CHIPS
    assert FW == LW and LW == nb * LANES and IN_W == 3 * FW + H + 2 * LW and H <= 8
    T = _tile(S, (512, 256, 128))
    c_idx = lax.axis_index('c').astype(jnp.int32).reshape(1)
    chip = 2 * lax.axis_index('x') + lax.axis_index('y')
    chip_idx = chip.astype(jnp.int32).reshape(1)
    vec = lambda n: W[n].reshape(1, -1)

    wins = _InWindows(FW, LW, H, W['w_in'].shape[1])
    started = {}
    g_tok = jnp.zeros((1, 1), F32)
    for call, names in (('gather_start_first', ['conv_w', 'w_in']), ('gather_start_rest', BIG[1:])):
        own = [W[n].reshape(-1, LANES) if n == 'conv_w' else W[n].astype(BF16) + g_tok.astype(BF16) for n in names]
        own = [wins.to_window(o, chip) if n == 'w_in' else o for n, o in zip(names, own)]
        ssem, rsem, srcs, zones, tok = gather_start(call, own, [n == 'conv_w' for n in names])
        g_tok = tok[0:1, 0:1]
        started.update({n: (t, srcs[t], zones[t], ssem, rsem) for t, n in enumerate(names)})

    def fetch(n, after):
        t, g_src, g_zone, g_ssem, g_rsem = started[n]
        src, zone = gather_wait('gather_wait_' + n, t, g_src, g_zone, g_ssem, g_rsem, after, n == 'conv_w')
        if n != 'conv_w':
            zone = pair_swap('pair_swap_' + n, zone)
        return lax.dynamic_update_index_in_dim(zone, src, chip, 0)

    b_f_pad = jnp.pad(vec('b_f'), ((0, 0), (0, LANES - H)))
    u_off, g_off = 3 * FW // LANES, (3 * FW + LW) // LANES

    h1 = norm_fwd('norm_mix', xs, vec('g_mix') + g_tok[0:1, 0:1])
    conv_full = fetch('conv_w', h1).reshape(N_CHIPS, CONV_W, LW // N_CHIPS).transpose(1, 0, 2).reshape(CONV_W, LW)
    w_in_pad = wins.assemble(fetch('w_in', [h1, M1['w_in'], V1['w_in']]))
    w5 = jnp.concatenate([w_in_pad[:, :3 * FW], w_in_pad[:, 3 * FW + LANES:]], axis=1)
    wf = w_in_pad[:, 3 * FW:3 * FW + LANES]
    proj = _mm('proj_in', h1, w5, 'nn', F32)
    f_raw = _mm('proj_f', h1, wf, 'nn', F32)
    qn, kn, vb = qkv_fwd(proj, vec('g_q'), vec('g_k'), FW)
    cc = fgate_fwd(f_raw, b_f_pad)
    ct = cc[:, :8].T
    o_fox, lse = fox_fwd(qn, kn, vb, cc, ct, T)
    lru_w = (conv_full, vec('conv_b'), W['w_ra'], vec('b_ra'), W['w_ri'], vec('b_ri'), vec('lam'))
    y_lru = lru_fwd(proj, *lru_w, u_off, g_off)
    mixn = mix_fwd(o_fox, y_lru, vec('g_fox_out'), vec('g_lru_out'))
    w_out_f = fetch('w_out', mixn).reshape(2 * FW, D)
    x1 = _mm('proj_out', mixn, w_out_f, 'nn', F32, res=xs)

    hq = norm_fwd('norm_xq', x1, vec('g_xattn'))
    mn = norm_fwd('norm_mem', ms, vec('g_mem'))
    w_cq_f = fetch('w_cq', hq).reshape(D, XW)
    w_ckv_f = fetch('w_ckv', hq).reshape(D, 2 * XW)
    cq_raw = _mm('proj_cq', hq, w_cq_f, 'nn', F32)
    ckv = _mm('proj_ckv', mn, w_ckv_f, 'nn', F32)
    o_x = xattn_fwd(cq_raw, ckv, vec('g_cq'), vec('g_ck'))
    w_co_g = fetch('w_co', o_x)
    x2 = _mm_colsharded('proj_co', o_x, w_co_g, F32, res=x1)

    hf = norm_fwd('norm_ffn', x2, vec('g_ffn'))
    w_gu_g = fetch('w_gate_up', hf)
    gu = _mm_colsharded('proj_gate_up', hf, w_gu_g, F32)
    act = swiglu_fwd(gu, F)
    w_down_f = fetch('w_down', act).reshape(F, D)
    yv = _mm('proj_down', act, w_down_f, 'nn', F32, res=x2)
    dy, dyb, loss_blk = loss_head(yv, tgt)

    gw, pending = {}, []

    def reduce_begin(n, g):
        sp = g.reshape(N_CHIPS, 2, g.shape[1] // 2, g.shape[2])
        ssem, rsem, sp, zone, tok = scatter_start('scatter_start_' + n, sp)
        pending.append((n, sp, zone, ssem, rsem))
        return tok[0:1, 0:1]

    dact = _mm('bwd_down_x', dyb, w_down_f, 'nt', F32)
    t_down = reduce_begin('w_down', _mm('bwd_down_w', act, dyb, 'tn', BF16).reshape(N_CHIPS, F // N_CHIPS, D))
    dgu = swiglu_bwd(gu, dact, F, t_down)
    dhf = _mm_colsharded_t('bwd_gate_up_x', dgu, w_gu_g, F32)
    t_gu = reduce_begin('w_gate_up', _mm_grad_colsharded('bwd_gate_up_w', hf, dgu, N_CHIPS, BF16))
    dx2, dx2b, gw['g_ffn'] = norm_bwd('norm_ffn_bwd', x2, vec('g_ffn') + t_down + t_gu, dhf, res=dy)

    do_x = _mm_colsharded_t('bwd_co_x', dx2b, w_co_g, BF16)
    t_co = reduce_begin('w_co', _mm_grad_colsharded('bwd_co_w', o_x, dx2b, N_CHIPS, BF16))
    dcq_raw, dckv, gw['g_cq'], gw['g_ck'] = xattn_bwd(cq_raw, ckv, vec('g_cq') + t_co, vec('g_ck'), do_x)
    dhq = _mm('bwd_cq_x', dcq_raw, w_cq_f, 'nt', F32)
    t_cq = reduce_begin('w_cq', _mm('bwd_cq_w', hq, dcq_raw, 'tn', BF16).reshape(N_CHIPS, D // N_CHIPS, XW))
    dmn = _mm('bwd_ckv_x', dckv, w_ckv_f, 'nt', F32)
    t_ckv = reduce_begin('w_ckv', _mm('bwd_ckv_w', mn, dckv, 'tn', BF16).reshape(N_CHIPS, D // N_CHIPS, 2 * XW))
    (gw['g_mem'],) = norm_bwd('norm_mem_bwd', ms, vec('g_mem'), dmn, want_dx=False)
    dx1, dx1b, gw['g_xattn'] = norm_bwd('norm_xq_bwd', x1, vec('g_xattn') + t_cq + t_ckv, dhq, res=dx2)

    dmix = _mm('bwd_out_x', dx1b, w_out_f, 'nt', F32)
    t_out = reduce_begin('w_out', _mm('bwd_out_w', mixn, dx1b, 'tn', BF16).reshape(N_CHIPS, 2 * FW // N_CHIPS, D))
    do_fox, delta, dy_lru, gw['g_fox_out'], gw['g_lru_out'] = mix_bwd(o_fox, y_lru, vec('g_fox_out') + t_out,
                                                                     vec('g_lru_out'), dmix)
    (du, dgate, gw['conv_w'], gw['conv_b'], gw['w_ra'], gw['b_ra'], gw['w_ri'], gw['b_ri'],
     gw['lam']) = lru_bwd(proj, dy_lru, *lru_w, u_off, g_off)
    dqn, delta2 = fox_bwd_q(qn, kn, vb, do_fox, cc, ct, lse, delta, T)
    dkn, dv, dct = fox_bwd_kv(qn, kn, vb, do_fox, cc, ct, lse, delta2, T)
    dq, dk, gw['g_q'], gw['g_k'] = qkv_bwd(proj, vec('g_q'), vec('g_k'), dqn, dkn, FW)
    dc = jnp.pad(dct.reshape(H, S).T, ((0, 0), (0, LANES - H)))
    df, db_f = fgate_bwd(f_raw, b_f_pad, dc, H)
    gw['b_f'] = db_f[:, :H]
    dproj = jnp.concatenate([dq, dk, dv, du, dgate], axis=1)
    dw5 = _mm('bwd_in_w', h1, dproj, 'tn', BF16)
    dwf = _mm('bwd_f_w', h1, df, 'tn', BF16)
    t_in = reduce_begin('w_in', wins.windows(jnp.concatenate([dw5[:, :3 * FW], dwf, dw5[:, 3 * FW:]], axis=1)))
    dh_a = _mm('bwd_f_x', df, wf, 'nt', F32)
    dh1 = _mm('bwd_in_x', dproj, w5, 'nt', F32, res=dh_a)
    grad_x, _, gw['g_mix'] = norm_bwd('norm_mix_bwd', xs, vec('g_mix') + t_in, dh1, res=dx1)

    grads, delta_w, new_m, new_v = {}, {}, {}, {}
    done = grad_x
    for n, part, zone, ssem, rsem in pending:
        part, landed = scatter_wait('scatter_wait_' + n, part, zone, ssem, rsem, done)
        mine = sum_parts('sum_parts_' + n, part, landed, chip_idx, c_idx)
        (other,) = pair_join('pair_join_' + n, [mine])
        if n == 'w_in':
            mine, other = wins.from_window(mine, chip), wins.from_window(other, chip)
        grads[n], delta_w[n], new_m[n], new_v[n] = adamw_halves('adamw_' + n, W[n], mine, other, M1[n], V1[n], c_idx)
        done = delta_w[n]

    small_shapes = [gw[n].shape for n in SMALL] + [(1, 1)]
    summed = _unpack(allreduce_small(_pack([gw[n] for n in SMALL] + [loss_blk[0:1, 0:1]]), delta_w[BIG[0]]), small_shapes)
    loss = summed[-1].reshape(())
    for n, g in zip(SMALL, summed):
        grads[n] = g.reshape(W[n].shape) if n != 'conv_w' else lax.dynamic_slice_in_dim(
            g, chip * (LW // N_CHIPS), LW // N_CHIPS, axis=1)
    packs = [_pack([d[n] for n in SMALL]) for d in (W, grads, M1, V1)]
    shapes = [W[n].shape for n in SMALL]
    for d, res in zip((delta_w, new_m, new_v), adamw('adamw_small', *packs)):
        d.update(zip(SMALL, _unpack(res, shapes)))

    lead = lambda d: [d[n][None] for n in WEIGHTS]
    return (loss, grad_x[None], *lead(grads), *lead(delta_w), *lead(new_m), *lead(new_v))
```

```python
import functools
import math

import jax
import jax.numpy as jnp
from jax import lax
from jax.experimental import pallas as pl
from jax.experimental.pallas import tpu as pltpu

F32 = jnp.float32
BF16 = jnp.bfloat16
HEAD_DIM = 128
LANES = 128
LRU_C = 8.0
RMS_EPS = 1e-6
CONV_W = 4
ADAM_LR = 0.001
ADAM_B1 = 0.9
ADAM_B2 = 0.999
ADAM_EPS = 1e-08
ADAM_WD = 0.01
ADAM_STEP = 10
VMEM_LIMIT = 56 * 1024 * 1024
N_CHIPS = 4
MESH = pl.DeviceIdType.MESH
ANY = pl.BlockSpec(memory_space=pl.ANY)

WEIGHTS = ['g_mix', 'w_in', 'b_f', 'g_q', 'g_k', 'conv_w', 'conv_b', 'w_ra', 'b_ra', 'w_ri', 'b_ri', 'lam',
           'g_fox_out', 'g_lru_out', 'w_out', 'g_xattn', 'g_mem', 'w_cq', 'w_ckv', 'g_cq', 'g_ck', 'w_co', 'g_ffn',
           'w_gate_up', 'w_down']
BIG = ['w_in', 'w_out', 'w_cq', 'w_ckv', 'w_co', 'w_gate_up', 'w_down']
SMALL = [n for n in WEIGHTS if n not in BIG]


def _params(sem=None):
    if sem is None:
        return pltpu.CompilerParams(vmem_limit_bytes=VMEM_LIMIT)
    return pltpu.CompilerParams(dimension_semantics=sem, vmem_limit_bytes=VMEM_LIMIT)


def _tile(n, cands):
    for t in cands:
        if n % t == 0:
            return t
    return n


ROW_BLOCK_BYTES = 1 << 20


def _row_tile(n_rows, n_cols, min_rows=8):
    cands = [t for t in (512, 256, 128, 64, 32, 16, 8) if t >= min_rows and t * n_cols * 4 <= ROW_BLOCK_BYTES]
    return _tile(n_rows, cands or [min_rows])


def _sigmoid(z):
    return 1.0 / (1.0 + jnp.exp(-z))


def _softplus(z):
    return jnp.maximum(z, 0.0) + jnp.log(1.0 + jnp.exp(-jnp.abs(z)))


def _neg_expm1(z):
    series = -z * (1.0 + z * (0.5 + z * (1.0 / 6.0 + z * (1.0 / 24.0 + z * (1.0 / 120.0)))))
    return jnp.where(z > -0.25, series, 1.0 - jnp.exp(z))


_GELU_K = math.sqrt(2.0 / math.pi)


def _gelu_and_grad(z):
    inner = _GELU_K * (z + 0.044715 * z * z * z)
    t = jnp.tanh(inner)
    g = 0.5 * z * (1.0 + t)
    dg = 0.5 * (1.0 + t) + 0.5 * z * (1.0 - t * t) * _GELU_K * (1.0 + 3.0 * 0.044715 * z * z)
    return g, dg


def _rms(xv, g):
    r = lax.rsqrt(jnp.mean(xv * xv, axis=-1, keepdims=True) + RMS_EPS)
    return xv * r * g


def _rms_bwd(xv, g, dy):
    r = lax.rsqrt(jnp.mean(xv * xv, axis=-1, keepdims=True) + RMS_EPS)
    xh = xv * r
    dyg = dy * g
    dx = r * (dyg - xh * jnp.mean(dyg * xh, axis=-1, keepdims=True))
    return dx, jnp.sum(dy * xh, axis=0, keepdims=True)


def _heads(fn, n_heads, *arrs):
    outs = [fn(*[a[:, h * HEAD_DIM:(h + 1) * HEAD_DIM] for a in arrs]) for h in range(n_heads)]
    first = jnp.concatenate([o[0] for o in outs], axis=1) if n_heads > 1 else outs[0][0]
    rest = [functools.reduce(lambda p, q: p + q, [o[i] for o in outs]) for i in range(1, len(outs[0]))]
    return (first, *rest)


def _split3(v):
    hi = v.astype(BF16)
    r1 = v - hi.astype(F32)
    mid = r1.astype(BF16)
    lo = (r1 - mid.astype(F32)).astype(BF16)
    return hi, mid, lo


def _acc_out(ref, first, val):
    @pl.when(first)
    def _():
        ref[...] = val

    @pl.when(jnp.logical_not(first))
    def _():
        ref[...] += val


_DIMS = {'nn': (((1,), (0,)), ((), ())), 'nt': (((1,), (1,)), ((), ())), 'tn': (((0,), (0,)), ((), ()))}


MM_VMEM_BYTES = 36 * 1024 * 1024


MXU_FLOPS = 800e12
HBM_BYTES_S = 3.2e12
VMEM_ADD_BYTES_S = 8e12
STEP_S = 0.35e-6


def _k_tile(K, tm, tn, a, b, o_dtype, res):
    fixed = tm * tn * (2 * jnp.dtype(o_dtype).itemsize + 4 + (8 if res is not None else 0))
    per_k = 2 * (tm * a.dtype.itemsize + tn * b.dtype.itemsize)
    per_k += 2 * tm * (a.dtype.itemsize > 2) + 2 * tn * (b.dtype.itemsize > 2)
    units = K // LANES
    for d in sorted((d for d in range(1, units + 1) if units % d == 0), reverse=True):
        if fixed + d * LANES * per_k <= MM_VMEM_BYTES:
            return d * LANES
    return None


def _mm_tiles(M, N, K, k_span, a, b, o_dtype, res, tn_cands=(2048, 1024, 512, 256, 128)):
    best = None
    for tm in (2048, 1024, 512, 256, 128):
        for tn in tn_cands:
            if M % tm or N % tn:
                continue
            tk = _k_tile(k_span, tm, tn, a, b, o_dtype, res)
            if tk is None:
                continue
            nk = K // tk
            traffic = (M * K * a.dtype.itemsize * (N // tn) + K * N * b.dtype.itemsize * (M // tm)
                       + M * N * (jnp.dtype(o_dtype).itemsize + (4 if res is not None else 0)))
            work = 2.0 * M * N * K / MXU_FLOPS + (M * N * 4 * nk / VMEM_ADD_BYTES_S if nk > 1 else 0.0)
            t = max(work, traffic / HBM_BYTES_S) + (M // tm) * (N // tn) * nk * STEP_S
            if best is None or t < best[0]:
                best = (t, tm, tn, tk)
    assert best is not None, (M, N, K)
    return best[1:]


def _mm_call(name, a, b, mode, grid, a_spec, b_spec, o_spec, o_shape, o_dtype, acc_shape, res=None):
    nk = grid[2]
    dn = _DIMS[mode]

    def body(*refs):
        a_ref, b_ref = refs[:2]
        r_ref = refs[2] if res is not None else None
        o_ref = refs[3] if res is not None else refs[2]
        part = lax.dot_general(a_ref[...].astype(BF16), b_ref[...].astype(BF16), dn, preferred_element_type=F32)

        def finish(r):
            if r_ref is not None:
                r = r + r_ref[...]
            o_ref[...] = r.astype(o_dtype)

        if nk == 1:
            finish(part)
            return
        acc = refs[-1]
        k = pl.program_id(2)

        @pl.when(k == 0)
        def _():
            acc[...] = part

        @pl.when(k > 0)
        def _():
            acc[...] += part

        @pl.when(k == nk - 1)
        def _():
            finish(acc[...])

    ins = [a, b] + ([] if res is None else [res])
    specs = [a_spec, b_spec] + ([] if res is None else [o_spec])
    return pl.pallas_call(
        body, name=name, grid=grid, in_specs=specs, out_specs=o_spec,
        out_shape=jax.ShapeDtypeStruct(o_shape, o_dtype),
        scratch_shapes=[] if nk == 1 else [pltpu.VMEM(acc_shape, F32)],
        compiler_params=_params(('parallel', 'parallel', 'arbitrary')))(*ins)


def _mm(name, a, b, mode, o_dtype, res=None):
    if mode == 'tn':
        K, M = a.shape
    else:
        M, K = a.shape
    N = b.shape[0] if mode == 'nt' else b.shape[1]
    tm, tn, tk = _mm_tiles(M, N, K, K, a, b, o_dtype, res)
    a_spec = (pl.BlockSpec((tk, tm), lambda m, n, k: (k, m)) if mode == 'tn'
              else pl.BlockSpec((tm, tk), lambda m, n, k: (m, k)))
    b_spec = (pl.BlockSpec((tn, tk), lambda m, n, k: (n, k)) if mode == 'nt'
              else pl.BlockSpec((tk, tn), lambda m, n, k: (k, n)))
    o_spec = pl.BlockSpec((tm, tn), lambda m, n, k: (m, n))
    return _mm_call(name, a, b, mode, (M // tm, N // tn, K // tk), a_spec, b_spec, o_spec, (M, N), o_dtype,
                    (tm, tn), res)


def _mm_colsharded(name, a, w, o_dtype, res=None):
    M, K = a.shape
    J, _, Nj = w.shape
    tm, tn, tk = _mm_tiles(M, J * Nj, K, K, a, w, o_dtype, res,
                           tn_cands=[t for t in (2816, 1408, 1024, 512, 256, 128) if Nj % t == 0])
    per = Nj // tn
    return _mm_call(name, a, w, 'nn', (M // tm, J * per, K // tk),
                    pl.BlockSpec((tm, tk), lambda m, n, k: (m, k)),
                    pl.BlockSpec((None, tk, tn), lambda m, n, k: (n // per, k, n % per)),
                    pl.BlockSpec((tm, tn), lambda m, n, k: (m, n)), (M, J * Nj), o_dtype, (tm, tn), res)


def _planes_spec(arr, rows, cols, row_of, col_of):
    if arr.ndim == 2:
        return pl.BlockSpec((rows, cols), lambda m, n, k: (row_of(m, n, k), col_of(m, n, k)))
    per_plane = arr.shape[2] // cols
    return pl.BlockSpec((None, rows, cols),
                        lambda m, n, k: (col_of(m, n, k) // per_plane, row_of(m, n, k), col_of(m, n, k) % per_plane))


def _mm_colsharded_t(name, a, w, o_dtype):
    M = a.shape[-2]
    J, K, Nj = w.shape
    tm, tn, tk = _mm_tiles(M, K, J * Nj, Nj, a, w, o_dtype, None)
    per = Nj // tk
    return _mm_call(name, a, w, 'nt', (M // tm, K // tn, J * per),
                    _planes_spec(a, tm, tk, lambda m, n, k: m, lambda m, n, k: k),
                    pl.BlockSpec((None, tn, tk), lambda m, n, k: (k // per, n, k % per)),
                    pl.BlockSpec((tm, tn), lambda m, n, k: (m, n)), (M, K), o_dtype, (tm, tn))


def _mm_grad_colsharded(name, a, dy, J, o_dtype):
    S, M = a.shape
    Nj = dy.shape[-1] * (dy.shape[0] if dy.ndim == 3 else 1) // J
    tm, tn, tk = _mm_tiles(M, J * Nj, S, S, a, dy, o_dtype, None,
                           tn_cands=[t for t in (2816, 1408, 1024, 512, 256, 128) if Nj % t == 0])
    per = Nj // tn
    return _mm_call(name, a, dy, 'tn', (M // tm, J * per, S // tk),
                    pl.BlockSpec((tk, tm), lambda m, n, k: (k, m)),
                    _planes_spec(dy, tk, tn, lambda m, n, k: k, lambda m, n, k: n),
                    pl.BlockSpec((None, tm, tn), lambda m, n, k: (n // per, m, n % per)), (J, M, Nj), o_dtype, (tm, tn))


def _rows_call(name, body, n_rows, tr, ins, outs):
    return pl.pallas_call(
        body, name=name, grid=(n_rows // tr,), in_specs=[s for _, s in ins], out_specs=[s for _, _, s in outs],
        out_shape=[jax.ShapeDtypeStruct(sh, dt) for sh, dt, _ in outs],
        compiler_params=_params(('arbitrary',)))(*[a for a, _ in ins])


def _rb(tr, w, cb=0):
    return pl.BlockSpec((tr, w), lambda i: (i, cb))


def _fb(shape):
    nd = len(shape)
    return pl.BlockSpec(shape, lambda i: (0,) * nd)


def norm_fwd(name, xv, g):
    S, D = xv.shape
    tr = _tile(S, (256, 128))

    def body(x_ref, g_ref, o_ref):
        o_ref[...] = _rms(x_ref[...], g_ref[...]).astype(BF16)

    return _rows_call(name, body, S, tr, [(xv, _rb(tr, D)), (g, _fb((1, D)))], [((S, D), BF16, _rb(tr, D))])[0]


def norm_bwd(name, xv, g, dy, res=None, want_dx=True):
    S, D = xv.shape
    tr = _tile(S, (256, 128))

    def body(*refs):
        if res is None:
            x_ref, g_ref, dy_ref = refs[:3]
            outs = refs[3:]
            r_ref = None
        else:
            x_ref, g_ref, dy_ref, r_ref = refs[:4]
            outs = refs[4:]
        dx, dg = _rms_bwd(x_ref[...], g_ref[...], dy_ref[...])
        if r_ref is not None:
            dx = dx + r_ref[...]
        if want_dx:
            outs[0][...] = dx
            outs[1][...] = dx.astype(BF16)
        _acc_out(outs[-1], pl.program_id(0) == 0, dg)

    ins = [(xv, _rb(tr, D)), (g, _fb((1, D))), (dy, _rb(tr, D))] + ([] if res is None else [(res, _rb(tr, D))])
    outs = ([((S, D), F32, _rb(tr, D)), ((S, D), BF16, _rb(tr, D))] if want_dx else []) + [((1, D), F32, _fb((1, D)))]
    return _rows_call(name, body, S, tr, ins, outs)


def qkv_fwd(proj, g_q, g_k, FW):
    S = proj.shape[0]
    H = FW // HEAD_DIM
    tr = _tile(S, (256, 128))

    def body(q_ref, k_ref, v_ref, gq_ref, gk_ref, qo, ko, vo):
        qo[...] = _heads(lambda t: (_rms(t, gq_ref[...]),), H, q_ref[...])[0].astype(BF16)
        ko[...] = _heads(lambda t: (_rms(t, gk_ref[...]),), H, k_ref[...])[0].astype(BF16)
        vo[...] = v_ref[...].astype(BF16)

    o = ((S, FW), BF16, _rb(tr, FW))
    return _rows_call('qkv_fwd', body, S, tr,
                      [(proj, _rb(tr, FW, 0)), (proj, _rb(tr, FW, 1)), (proj, _rb(tr, FW, 2)),
                       (g_q, _fb((1, HEAD_DIM))), (g_k, _fb((1, HEAD_DIM)))], [o, o, o])


def qkv_bwd(proj, g_q, g_k, dqn, dkn, FW):
    S = proj.shape[0]
    H = FW // HEAD_DIM
    tr = _tile(S, (256, 128))

    def body(q_ref, k_ref, gq_ref, gk_ref, dq_ref, dk_ref, dqo, dko, dgq, dgk):
        dq, gq = _heads(lambda t, d: _rms_bwd(t, gq_ref[...], d), H, q_ref[...], dq_ref[...])
        dk, gk = _heads(lambda t, d: _rms_bwd(t, gk_ref[...], d), H, k_ref[...], dk_ref[...])
        dqo[...] = dq.astype(BF16)
        dko[...] = dk.astype(BF16)
        first = pl.program_id(0) == 0
        _acc_out(dgq, first, gq)
        _acc_out(dgk, first, gk)

    o = ((S, FW), BF16, _rb(tr, FW))
    og = ((1, HEAD_DIM), F32, _fb((1, HEAD_DIM)))
    return _rows_call('qkv_bwd', body, S, tr,
                      [(proj, _rb(tr, FW, 0)), (proj, _rb(tr, FW, 1)), (g_q, _fb((1, HEAD_DIM))),
                       (g_k, _fb((1, HEAD_DIM))), (dqn, _rb(tr, FW)), (dkn, _rb(tr, FW))], [o, o, og, og])


def _tri(n, upper):
    r = lax.broadcasted_iota(jnp.int32, (n, n), 0)
    c = lax.broadcasted_iota(jnp.int32, (n, n), 1)
    return jnp.where((c >= r) if upper else (c <= r), 1.0, 0.0).astype(BF16)


def _blocked_cumsum(val, S, blk, reverse):
    tri = _tri(blk, reverse)
    order = range(S // blk - 1, -1, -1) if reverse else range(S // blk)
    carry = jnp.zeros((1, LANES), F32)
    outs = {}
    for bi in order:
        part = val[bi * blk:(bi + 1) * blk]
        acc = carry
        for piece in _split3(part):
            acc = acc + jnp.dot(tri, piece, preferred_element_type=F32)
        outs[bi] = acc
        carry = carry + jnp.sum(part, axis=0, keepdims=True)
    return jnp.concatenate([outs[bi] for bi in range(S // blk)], axis=0)


def fgate_fwd(f_raw, b_f_pad):
    S = f_raw.shape[0]
    blk = _tile(S, (256, 128))

    def body(f_ref, b_ref, c_ref):
        z = f_ref[...] + b_ref[...]
        c_ref[...] = _blocked_cumsum(-_softplus(-z), S, blk, False)

    return pl.pallas_call(body, name='fgate_fwd', grid=(1,), in_specs=[_fb((S, LANES)), _fb((1, LANES))],
                          out_specs=_fb((S, LANES)), out_shape=jax.ShapeDtypeStruct((S, LANES), F32),
                          compiler_params=_params(('arbitrary',)))(f_raw, b_f_pad)


def fgate_bwd(f_raw, b_f_pad, dc, H):
    S = f_raw.shape[0]
    blk = _tile(S, (256, 128))

    def body(f_ref, b_ref, dc_ref, df_ref, db_ref):
        z = f_ref[...] + b_ref[...]
        dlogf = _blocked_cumsum(dc_ref[...], S, blk, True)
        lane = lax.broadcasted_iota(jnp.int32, (S, LANES), 1)
        df = jnp.where(lane < H, dlogf * _sigmoid(-z), 0.0)
        df_ref[...] = df.astype(BF16)
        db_ref[...] = jnp.sum(df, axis=0, keepdims=True)

    return pl.pallas_call(body, name='fgate_bwd', grid=(1,),
                          in_specs=[_fb((S, LANES)), _fb((1, LANES)), _fb((S, LANES))],
                          out_specs=[_fb((S, LANES)), _fb((1, LANES))],
                          out_shape=[jax.ShapeDtypeStruct((S, LANES), BF16), jax.ShapeDtypeStruct((1, LANES), F32)],
                          compiler_params=_params(('arbitrary',)))(f_raw, b_f_pad, dc)


def _fox_logits(q, k, c_blk, ct_blk, h, T, diagonal):
    s = lax.dot_general(q, k, _DIMS['nt'], preferred_element_type=F32) * (1.0 / math.sqrt(HEAD_DIM))
    lane = lax.broadcasted_iota(jnp.int32, c_blk.shape, 1)
    cq = jnp.sum(jnp.where(lane == h, c_blk, 0.0), axis=1, keepdims=True)
    sub = lax.broadcasted_iota(jnp.int32, ct_blk.shape, 0)
    ck = jnp.sum(jnp.where(sub == h, ct_blk, 0.0), axis=0, keepdims=True)
    s = s + cq - ck
    if not diagonal:
        return s
    rows = lax.broadcasted_iota(jnp.int32, (T, T), 0)
    cols = lax.broadcasted_iota(jnp.int32, (T, T), 1)
    return jnp.where(cols <= rows, s, -jnp.inf)


def _below_and_on_diagonal(q_blk, k_blk, step):
    @pl.when(k_blk < q_blk)
    def _():
        step(False)

    @pl.when(k_blk == q_blk)
    def _():
        step(True)


def fox_fwd(qn, kn, vb, c, ct, T):
    S, FW = qn.shape
    H = FW // HEAD_DIM
    Hp = ct.shape[0]
    n = S // T

    def body(q_ref, k_ref, v_ref, c_ref, ct_ref, o_ref, lse_ref, m_s, l_s, acc_s):
        h, i, j = pl.program_id(0), pl.program_id(1), pl.program_id(2)

        @pl.when(j == 0)
        def _():
            m_s[...] = jnp.full_like(m_s, -jnp.inf)
            l_s[...] = jnp.zeros_like(l_s)
            acc_s[...] = jnp.zeros_like(acc_s)

        def step(diagonal):
            s = _fox_logits(q_ref[...], k_ref[...], c_ref[...], ct_ref[...], h, T, diagonal)
            m_new = jnp.maximum(m_s[...], jnp.max(s, axis=1, keepdims=True))
            alpha = jnp.exp(m_s[...] - m_new)
            p = jnp.exp(s - m_new)
            l_s[...] = alpha * l_s[...] + jnp.sum(p, axis=1, keepdims=True)
            acc_s[...] = alpha * acc_s[...] + jnp.dot(p.astype(BF16), v_ref[...], preferred_element_type=F32)
            m_s[...] = m_new

        _below_and_on_diagonal(i, j, step)

        @pl.when(j == i)
        def _():
            o_ref[...] = acc_s[...] / l_s[...]
            lse_ref[...] = jnp.broadcast_to(m_s[...] + jnp.log(l_s[...]), (T, LANES))

    qs = pl.BlockSpec((T, HEAD_DIM), lambda h, i, j: (i, h))
    ks = pl.BlockSpec((T, HEAD_DIM), lambda h, i, j: (jnp.minimum(j, i), h))
    return pl.pallas_call(
        body, name='fox_fwd', grid=(H, n, n),
        in_specs=[qs, ks, ks, pl.BlockSpec((T, LANES), lambda h, i, j: (i, 0)),
                  pl.BlockSpec((Hp, T), lambda h, i, j: (0, jnp.minimum(j, i)))],
        out_specs=[qs, pl.BlockSpec((None, T, LANES), lambda h, i, j: (h, i, 0))],
        out_shape=[jax.ShapeDtypeStruct((S, FW), F32), jax.ShapeDtypeStruct((H, S, LANES), F32)],
        scratch_shapes=[pltpu.VMEM((T, 1), F32), pltpu.VMEM((T, 1), F32), pltpu.VMEM((T, HEAD_DIM), F32)],
        compiler_params=_params(('parallel', 'parallel', 'arbitrary')))(qn, kn, vb, c, ct)


def _fox_p_ds(q_ref, k_ref, v_ref, do_ref, c_ref, ct_ref, lse_ref, dl_ref, h, T, diagonal):
    s = _fox_logits(q_ref[...], k_ref[...], c_ref[...], ct_ref[...], h, T, diagonal)
    p = jnp.exp(s - jnp.tile(lse_ref[...], (1, T // LANES)))
    dp = lax.dot_general(do_ref[...], v_ref[...], _DIMS['nt'], preferred_element_type=F32)
    ds = p * (dp - jnp.tile(dl_ref[...], (1, T // LANES)))
    return p, dp, ds


def fox_bwd_q(qn, kn, vb, do, c, ct, lse, dl, T):
    S, FW = qn.shape
    H = FW // HEAD_DIM
    Hp = ct.shape[0]
    n = S // T

    def body(q_ref, k_ref, v_ref, do_ref, c_ref, ct_ref, lse_ref, dl_ref, dq_ref, dl2_ref, acc_s, rs_s):
        h, i, j = pl.program_id(0), pl.program_id(1), pl.program_id(2)

        @pl.when(j == 0)
        def _():
            acc_s[...] = jnp.zeros_like(acc_s)
            rs_s[...] = jnp.zeros_like(rs_s)

        def step(diagonal):
            p, dp, ds = _fox_p_ds(q_ref, k_ref, v_ref, do_ref, c_ref, ct_ref, lse_ref, dl_ref, h, T, diagonal)
            acc_s[...] += jnp.dot(ds.astype(BF16), k_ref[...], preferred_element_type=F32)
            rs_s[...] += jnp.sum(p * dp, axis=1, keepdims=True)

        _below_and_on_diagonal(i, j, step)

        @pl.when(j == i)
        def _():
            dq_ref[...] = acc_s[...] * (1.0 / math.sqrt(HEAD_DIM))
            dl2_ref[...] = jnp.broadcast_to(rs_s[...], (T, LANES))

    qs = pl.BlockSpec((T, HEAD_DIM), lambda h, i, j: (i, h))
    ks = pl.BlockSpec((T, HEAD_DIM), lambda h, i, j: (jnp.minimum(j, i), h))
    st = pl.BlockSpec((None, T, LANES), lambda h, i, j: (h, i, 0))
    return pl.pallas_call(
        body, name='fox_bwd_q', grid=(H, n, n),
        in_specs=[qs, ks, ks, qs, pl.BlockSpec((T, LANES), lambda h, i, j: (i, 0)),
                  pl.BlockSpec((Hp, T), lambda h, i, j: (0, jnp.minimum(j, i))), st, st],
        out_specs=[qs, st], out_shape=[jax.ShapeDtypeStruct((S, FW), F32), jax.ShapeDtypeStruct((H, S, LANES), F32)],
        scratch_shapes=[pltpu.VMEM((T, HEAD_DIM), F32), pltpu.VMEM((T, 1), F32)],
        compiler_params=_params(('parallel', 'parallel', 'arbitrary')))(qn, kn, vb, do, c, ct, lse, dl)


def fox_bwd_kv(qn, kn, vb, do, c, ct, lse, dl, T):
    S, FW = qn.shape
    H = FW // HEAD_DIM
    Hp = ct.shape[0]
    n = S // T

    def body(q_ref, k_ref, v_ref, do_ref, c_ref, ct_ref, lse_ref, dl_ref, dk_ref, dv_ref, dc_ref, dk_s, dv_s, dc_s):
        h, j, i = pl.program_id(0), pl.program_id(1), pl.program_id(2)

        @pl.when(i == 0)
        def _():
            dk_s[...] = jnp.zeros_like(dk_s)
            dv_s[...] = jnp.zeros_like(dv_s)
            dc_s[...] = jnp.zeros_like(dc_s)

        def step(diagonal):
            p, _, ds = _fox_p_ds(q_ref, k_ref, v_ref, do_ref, c_ref, ct_ref, lse_ref, dl_ref, h, T, diagonal)
            dv_s[...] += lax.dot_general(p.astype(BF16), do_ref[...], _DIMS['tn'], preferred_element_type=F32)
            dk_s[...] += lax.dot_general(ds.astype(BF16), q_ref[...], _DIMS['tn'], preferred_element_type=F32)
            dc_s[...] += jnp.sum(ds, axis=0, keepdims=True)

        _below_and_on_diagonal(i, j, step)

        @pl.when(i == n - 1)
        def _():
            dk_ref[...] = dk_s[...] * (1.0 / math.sqrt(HEAD_DIM))
            dv_ref[...] = dv_s[...].astype(BF16)
            dc_ref[...] = -dc_s[...]

    qs = pl.BlockSpec((T, HEAD_DIM), lambda h, j, i: (jnp.maximum(i, j), h))
    ks = pl.BlockSpec((T, HEAD_DIM), lambda h, j, i: (j, h))
    st = pl.BlockSpec((None, T, LANES), lambda h, j, i: (h, jnp.maximum(i, j), 0))
    return pl.pallas_call(
        body, name='fox_bwd_kv', grid=(H, n, n),
        in_specs=[qs, ks, ks, qs, pl.BlockSpec((T, LANES), lambda h, j, i: (jnp.maximum(i, j), 0)),
                  pl.BlockSpec((Hp, T), lambda h, j, i: (0, j)), st, st],
        out_specs=[ks, ks, pl.BlockSpec((None, 1, T), lambda h, j, i: (h, 0, j))],
        out_shape=[jax.ShapeDtypeStruct((S, FW), F32), jax.ShapeDtypeStruct((S, FW), BF16),
                   jax.ShapeDtypeStruct((H, 1, S), F32)],
        scratch_shapes=[pltpu.VMEM((T, HEAD_DIM), F32), pltpu.VMEM((T, HEAD_DIM), F32), pltpu.VMEM((1, T), F32)],
        compiler_params=_params(('parallel', 'parallel', 'arbitrary')))(qn, kn, vb, do, c, ct, lse, dl)


def _shift_down(v, d, rows, fill):
    return jnp.where(rows >= d, pltpu.roll(v, d, 0), fill)


def _shift_up(v, d, rows, S, fill):
    return jnp.where(rows < S - d, pltpu.roll(v, S - d, 0), fill)


def _scan(a, b, rows, S, reverse):
    d = 1
    while d < S:
        if reverse:
            a_s, b_s = _shift_up(a, d, rows, S, 1.0), _shift_up(b, d, rows, S, 0.0)
        else:
            a_s, b_s = _shift_down(a, d, rows, 1.0), _shift_down(b, d, rows, 0.0)
        b = a * b_s + b
        a = a * a_s
        d *= 2
    return b


def _lru_forward(u, cw, cb, wra, bra, wri, bri, lam, rows):
    uc = cb + cw[CONV_W - 1] * u
    for d in range(1, CONV_W):
        uc = uc + cw[CONV_W - 1 - d] * _shift_down(u, d, rows, 0.0)
    ucb = uc.astype(BF16)
    r = _sigmoid(jnp.dot(ucb, wra.astype(BF16), preferred_element_type=F32) + bra)
    ig = _sigmoid(jnp.dot(ucb, wri.astype(BF16), preferred_element_type=F32) + bri)
    sp = _softplus(-lam)
    log_a = -LRU_C * r * sp
    a = jnp.exp(log_a)
    sq = jnp.sqrt(_neg_expm1(2.0 * log_a))
    iu = ig * uc
    hseq = _scan(a, sq * iu, rows, u.shape[0], False)
    return uc, ucb, r, ig, sp, a, sq, iu, hseq


def _lru_specs(S, n_u, n_g):
    col = lambda off: pl.BlockSpec((S, LANES), lambda cbk: (0, off + cbk))
    vec = pl.BlockSpec((1, LANES), lambda cbk: (0, cbk))
    mat = pl.BlockSpec((None, LANES, LANES), lambda cbk: (cbk, 0, 0))
    cw = pl.BlockSpec((CONV_W, LANES), lambda cbk: (0, cbk))
    return col, vec, mat, cw


def lru_fwd(proj, conv_w, conv_b, w_ra, b_ra, w_ri, b_ri, lam, u_off, g_off):
    S = proj.shape[0]
    nb = w_ra.shape[0]
    col, vec, mat, cws = _lru_specs(S, u_off, g_off)

    def body(u_ref, g_ref, cw_ref, cb_ref, wra_ref, bra_ref, wri_ref, bri_ref, lam_ref, y_ref):
        rows = lax.broadcasted_iota(jnp.int32, (S, LANES), 0)
        cw = [cw_ref[t:t + 1, :] for t in range(CONV_W)]
        hseq = _lru_forward(u_ref[...], cw, cb_ref[...], wra_ref[...], bra_ref[...], wri_ref[...],
                            bri_ref[...], lam_ref[...], rows)[-1]
        y_ref[...] = hseq * _gelu_and_grad(g_ref[...])[0]

    return pl.pallas_call(
        body, name='lru_fwd', grid=(nb,),
        in_specs=[col(u_off), col(g_off), cws, vec, mat, vec, mat, vec, vec], out_specs=col(0),
        out_shape=jax.ShapeDtypeStruct((S, nb * LANES), F32),
        compiler_params=_params(('parallel',)))(proj, proj, conv_w, conv_b, w_ra, b_ra, w_ri, b_ri, lam)


def lru_bwd(proj, dy, conv_w, conv_b, w_ra, b_ra, w_ri, b_ri, lam, u_off, g_off):
    S = proj.shape[0]
    nb = w_ra.shape[0]
    LW = nb * LANES
    col, vec, mat, cws = _lru_specs(S, u_off, g_off)

    def body(u_ref, g_ref, dy_ref, cw_ref, cb_ref, wra_ref, bra_ref, wri_ref, bri_ref, lam_ref,
             du_ref, dg_ref, dcw_ref, dcb_ref, dwra_ref, dbra_ref, dwri_ref, dbri_ref, dlam_ref):
        rows = lax.broadcasted_iota(jnp.int32, (S, LANES), 0)
        u, lam_v = u_ref[...], lam_ref[...]
        cw = [cw_ref[t:t + 1, :] for t in range(CONV_W)]
        wra, wri = wra_ref[...].astype(BF16), wri_ref[...].astype(BF16)
        uc, ucb, r, ig, sp, a, sq, iu, hseq = _lru_forward(u, cw, cb_ref[...], wra, bra_ref[...], wri, bri_ref[...],
                                                           lam_v, rows)
        gl, dgl = _gelu_and_grad(g_ref[...])
        dy_v = dy_ref[...]
        dg_ref[...] = (dy_v * hseq * dgl).astype(BF16)
        G = _scan(_shift_up(a, 1, rows, S, 0.0), dy_v * gl, rows, S, True)
        da = G * _shift_down(hseq, 1, rows, 0.0)
        diu = G * sq
        dsq = G * iu
        dlog_a = da * a - dsq * a * a / jnp.maximum(sq, 1e-30)
        dr = dlog_a * (-LRU_C * sp)
        dsp = jnp.sum(dlog_a * (-LRU_C * r), axis=0, keepdims=True)
        dlam_ref[...] = -dsp * _sigmoid(-lam_v)
        dzr = dr * r * (1.0 - r)
        dzi = diu * uc * ig * (1.0 - ig)
        dzrb, dzib = dzr.astype(BF16), dzi.astype(BF16)
        duc = (diu * ig + lax.dot_general(dzrb, wra, _DIMS['nt'], preferred_element_type=F32)
               + lax.dot_general(dzib, wri, _DIMS['nt'], preferred_element_type=F32))
        dwra_ref[...] = lax.dot_general(ucb, dzrb, _DIMS['tn'], preferred_element_type=F32)
        dwri_ref[...] = lax.dot_general(ucb, dzib, _DIMS['tn'], preferred_element_type=F32)
        dbra_ref[...] = jnp.sum(dzr, axis=0, keepdims=True)
        dbri_ref[...] = jnp.sum(dzi, axis=0, keepdims=True)
        dcb_ref[...] = jnp.sum(duc, axis=0, keepdims=True)
        du = cw[CONV_W - 1] * duc
        dcw_ref[CONV_W - 1:CONV_W, :] = jnp.sum(duc * u, axis=0, keepdims=True)
        for d in range(1, CONV_W):
            du = du + cw[CONV_W - 1 - d] * _shift_up(duc, d, rows, S, 0.0)
            dcw_ref[CONV_W - 1 - d:CONV_W - d, :] = jnp.sum(duc * _shift_down(u, d, rows, 0.0), axis=0, keepdims=True)
        du_ref[...] = du.astype(BF16)

    sd = jax.ShapeDtypeStruct
    return pl.pallas_call(
        body, name='lru_bwd', grid=(nb,),
        in_specs=[col(u_off), col(g_off), col(0), cws, vec, mat, vec, mat, vec, vec],
        out_specs=[col(0), col(0), cws, vec, mat, vec, mat, vec, vec],
        out_shape=[sd((S, LW), BF16), sd((S, LW), BF16), sd((CONV_W, LW), F32), sd((1, LW), F32),
                   sd((nb, LANES, LANES), F32), sd((1, LW), F32), sd((nb, LANES, LANES), F32), sd((1, LW), F32),
                   sd((1, LW), F32)],
        compiler_params=_params(('parallel',)))(proj, proj, dy, conv_w, conv_b, w_ra, b_ra, w_ri, b_ri, lam)


def mix_fwd(o_fox, y_lru, g_fox, g_lru):
    S, FW = o_fox.shape
    tr = _tile(S, (256, 128))

    def body(o_ref, y_ref, gf_ref, gl_ref, m_ref):
        m_ref[...] = jnp.concatenate([_rms(o_ref[...], gf_ref[...]), _rms(y_ref[...], gl_ref[...])],
                                     axis=1).astype(BF16)

    return _rows_call('mix_fwd', body, S, tr,
                      [(o_fox, _rb(tr, FW)), (y_lru, _rb(tr, FW)), (g_fox, _fb((1, FW))), (g_lru, _fb((1, FW)))],
                      [((S, 2 * FW), BF16, _rb(tr, 2 * FW))])[0]


def mix_bwd(o_fox, y_lru, g_fox, g_lru, dmix):
    S, FW = o_fox.shape
    H = FW // HEAD_DIM
    tr = _tile(S, (256, 128))

    def body(o_ref, y_ref, gf_ref, gl_ref, df_ref, dl_ref, do_ref, dlt_ref, dy_ref, dgf_ref, dgl_ref):
        o = o_ref[...]
        do, dgf = _rms_bwd(o, gf_ref[...], df_ref[...])
        dyl, dgl = _rms_bwd(y_ref[...], gl_ref[...], dl_ref[...])
        do_ref[...] = do.astype(BF16)
        dy_ref[...] = dyl
        prod = do * o
        for h in range(H):
            dlt_ref[h] = jnp.broadcast_to(
                jnp.sum(prod[:, h * HEAD_DIM:(h + 1) * HEAD_DIM], axis=1, keepdims=True), (tr, LANES))
        first = pl.program_id(0) == 0
        _acc_out(dgf_ref, first, dgf)
        _acc_out(dgl_ref, first, dgl)

    g = _fb((1, FW))
    return _rows_call('mix_bwd', body, S, tr,
                      [(o_fox, _rb(tr, FW)), (y_lru, _rb(tr, FW)), (g_fox, g), (g_lru, g), (dmix, _rb(tr, FW, 0)),
                       (dmix, _rb(tr, FW, 1))],
                      [((S, FW), BF16, _rb(tr, FW)), ((H, S, LANES), F32, pl.BlockSpec((H, tr, LANES), lambda i: (0, i, 0))),
                       ((S, FW), F32, _rb(tr, FW)), ((1, FW), F32, g), ((1, FW), F32, g)])


def _xattn_heads(cq_raw, ckv, g_cq, g_ck, XW):
    out = []
    for h in range(XW // HEAD_DIM):
        sl = slice(h * HEAD_DIM, (h + 1) * HEAD_DIM)
        out.append((cq_raw[:, sl], _rms(cq_raw[:, sl], g_cq), ckv[:, sl], _rms(ckv[:, sl], g_ck),
                    ckv[:, XW + h * HEAD_DIM:XW + (h + 1) * HEAD_DIM].astype(BF16)))
    return out


def xattn_fwd(cq_raw, ckv, g_cq, g_ck):
    S, XW = cq_raw.shape
    M = ckv.shape[0]
    tr = _tile(S, (512, 256, 128))

    def body(q_ref, kv_ref, gq_ref, gk_ref, o_ref):
        outs = []
        for _, qn, _, kn, v in _xattn_heads(q_ref[...], kv_ref[...], gq_ref[...], gk_ref[...], XW):
            s = lax.dot_general(qn.astype(BF16), kn.astype(BF16), _DIMS['nt'], preferred_element_type=F32)
            s = s / math.sqrt(HEAD_DIM)
            p = jnp.exp(s - jnp.max(s, axis=1, keepdims=True))
            p = p / jnp.sum(p, axis=1, keepdims=True)
            outs.append(jnp.dot(p.astype(BF16), v, preferred_element_type=F32))
        o_ref[...] = jnp.concatenate(outs, axis=1).astype(BF16)

    g = _fb((1, HEAD_DIM))
    return _rows_call('xattn_fwd', body, S, tr,
                      [(cq_raw, _rb(tr, XW)), (ckv, _fb((M, 2 * XW))), (g_cq, g), (g_ck, g)],
                      [((S, XW), BF16, _rb(tr, XW))])[0]


def xattn_bwd(cq_raw, ckv, g_cq, g_ck, do):
    S, XW = cq_raw.shape
    M = ckv.shape[0]
    tr = _tile(S, (512, 256, 128))
    n = S // tr

    def body(q_ref, kv_ref, gq_ref, gk_ref, do_ref, dq_ref, dkv_ref, dgq_ref, dgk_ref):
        i = pl.program_id(0)
        do_v = do_ref[...]
        dqs, dkn, dvs = [], [], []
        dgq = jnp.zeros((1, HEAD_DIM), F32)
        for h, (q_raw, qn, _, kn, v) in enumerate(_xattn_heads(q_ref[...], kv_ref[...], gq_ref[...], gk_ref[...], XW)):
            qb, kb = qn.astype(BF16), kn.astype(BF16)
            doh = do_v[:, h * HEAD_DIM:(h + 1) * HEAD_DIM]
            s = lax.dot_general(qb, kb, _DIMS['nt'], preferred_element_type=F32) / math.sqrt(HEAD_DIM)
            p = jnp.exp(s - jnp.max(s, axis=1, keepdims=True))
            p = p / jnp.sum(p, axis=1, keepdims=True)
            dp = lax.dot_general(doh, v, _DIMS['nt'], preferred_element_type=F32)
            ds = (p * (dp - jnp.sum(p * dp, axis=1, keepdims=True)) / math.sqrt(HEAD_DIM)).astype(BF16)
            dvs.append(lax.dot_general(p.astype(BF16), doh, _DIMS['tn'], preferred_element_type=F32))
            dkn.append(lax.dot_general(ds, qb, _DIMS['tn'], preferred_element_type=F32))
            dq, g1 = _rms_bwd(q_raw, gq_ref[...], jnp.dot(ds, kb, preferred_element_type=F32))
            dqs.append(dq)
            dgq = dgq + g1
        dq_ref[...] = jnp.concatenate(dqs, axis=1).astype(BF16)
        first = i == 0
        _acc_out(dgq_ref, first, dgq)
        _acc_out(dkv_ref, first, jnp.concatenate(dkn + dvs, axis=1))

        @pl.when(i == n - 1)
        def _():
            kv = kv_ref[...]
            acc = dkv_ref[...]
            dk, gk = _heads(lambda t, d: _rms_bwd(t, gk_ref[...], d), XW // HEAD_DIM, kv[:, :XW], acc[:, :XW])
            dkv_ref[:, :XW] = dk
            dgk_ref[...] = gk

    g = _fb((1, HEAD_DIM))
    return _rows_call('xattn_bwd', body, S, tr,
                      [(cq_raw, _rb(tr, XW)), (ckv, _fb((M, 2 * XW))), (g_cq, g), (g_ck, g), (do, _rb(tr, XW))],
                      [((S, XW), BF16, _rb(tr, XW)), ((M, 2 * XW), F32, _fb((M, 2 * XW))), ((1, HEAD_DIM), F32, g),
                       ((1, HEAD_DIM), F32, g)])


def swiglu_fwd(gu, F):
    S = gu.shape[0]
    tr = _tile(S, (256, 128))
    tf = _tile(F, (1408, 1024, 512, 256, 128))
    nf = F // tf

    def body(g_ref, u_ref, a_ref):
        g = g_ref[...]
        a_ref[...] = (g * _sigmoid(g) * u_ref[...]).astype(BF16)

    return pl.pallas_call(
        body, name='swiglu_fwd', grid=(S // tr, nf),
        in_specs=[pl.BlockSpec((tr, tf), lambda i, n: (i, n)), pl.BlockSpec((tr, tf), lambda i, n: (i, n + nf))],
        out_specs=pl.BlockSpec((tr, tf), lambda i, n: (i, n)), out_shape=jax.ShapeDtypeStruct((S, F), BF16),
        compiler_params=_params(('parallel', 'parallel')))(gu, gu)


def swiglu_bwd(gu, dact, F, after):
    S = gu.shape[0]
    tr = _tile(S, (256, 128))
    tf = _tile(F, (1408, 1024, 512, 256, 128))
    nf = F // tf

    def body(g_ref, u_ref, da_ref, after_ref, o_ref):
        g, da = g_ref[...], da_ref[...]
        sg = _sigmoid(g)
        o_ref[0] = (da * u_ref[...] * sg * (1.0 + g * (1.0 - sg))).astype(BF16)
        o_ref[1] = (da * g * sg).astype(BF16)

    return pl.pallas_call(
        body, name='swiglu_bwd', grid=(S // tr, nf),
        in_specs=[pl.BlockSpec((tr, tf), lambda i, n: (i, n)), pl.BlockSpec((tr, tf), lambda i, n: (i, n + nf)),
                  pl.BlockSpec((tr, tf), lambda i, n: (i, n)), ANY],
        out_specs=pl.BlockSpec((2, tr, tf), lambda i, n: (0, i, n)), out_shape=jax.ShapeDtypeStruct((2, S, F), BF16),
        compiler_params=_params(('parallel', 'parallel')))(gu, gu, dact, after)


def loss_head(y, target):
    S, D = y.shape
    tr = _tile(S, (256, 128))

    def body(y_ref, t_ref, d_ref, db_ref, l_ref):
        err = y_ref[...] - t_ref[...]
        d = err * (1.0 / D)
        d_ref[...] = d
        db_ref[...] = d.astype(BF16)
        part = jnp.sum(jnp.sum(err * err, axis=1, keepdims=True), axis=0, keepdims=True) * (0.5 / D)
        _acc_out(l_ref, pl.program_id(0) == 0, jnp.broadcast_to(part, (1, LANES)))

    return _rows_call('loss_head', body, S, tr, [(y, _rb(tr, D)), (target, _rb(tr, D))],
                      [((S, D), F32, _rb(tr, D)), ((S, D), BF16, _rb(tr, D)), ((1, LANES), F32, _fb((1, LANES)))])


def _adamw_math(w, gv, m, v):
    mn = ADAM_B1 * m + (1.0 - ADAM_B1) * gv
    vn = ADAM_B2 * v + (1.0 - ADAM_B2) * (gv * gv)
    m_hat = mn / (1.0 - ADAM_B1 ** ADAM_STEP)
    v_hat = vn / (1.0 - ADAM_B2 ** ADAM_STEP)
    return -ADAM_LR * (m_hat / (jnp.sqrt(v_hat) + ADAM_EPS) + ADAM_WD * w), mn, vn


def adamw(name, w, g, m, v):
    R, C = w.shape
    tr = _row_tile(R, C)

    def body(w_ref, g_ref, m_ref, v_ref, d_ref, mo_ref, vo_ref):
        d_ref[...], mo_ref[...], vo_ref[...] = _adamw_math(w_ref[...], g_ref[...], m_ref[...], v_ref[...])

    spec = _rb(tr, C)
    return _rows_call(name, body, R, tr, [(w, spec), (g, spec), (m, spec), (v, spec)], [((R, C), F32, spec)] * 3)


def adamw_halves(name, w, mine, other, m, v, c_idx):
    R, C = w.shape
    hr = R // 2
    tr = _row_tile(hr, C)

    def body(c_ref, w_ref, a_ref, b_ref, m_ref, v_ref, g_ref, d_ref, mo_ref, vo_ref):
        gv = jnp.where(pl.program_id(0) == c_ref[0], a_ref[...], b_ref[...])
        g_ref[...] = gv
        d_ref[...], mo_ref[...], vo_ref[...] = _adamw_math(w_ref[...], gv, m_ref[...], v_ref[...])

    full = pl.BlockSpec((None, tr, C), lambda hh, i, c_ref: (hh, i, 0))
    mine_spec = pl.BlockSpec((tr, C), lambda hh, i, c_ref: (jnp.where(hh == c_ref[0], i, 0), 0))
    other_spec = pl.BlockSpec((tr, C), lambda hh, i, c_ref: (jnp.where(hh == c_ref[0], 0, i), 0))
    outs = pl.pallas_call(
        body, name=name,
        grid_spec=pltpu.PrefetchScalarGridSpec(num_scalar_prefetch=1, grid=(2, hr // tr),
                                               in_specs=[full, mine_spec, other_spec, full, full], out_specs=[full] * 4),
        out_shape=[jax.ShapeDtypeStruct((2, hr, C), F32)] * 4,
        compiler_params=_params(('parallel', 'parallel')))(
            c_idx, w.reshape(2, hr, C), mine, other, m.reshape(2, hr, C), v.reshape(2, hr, C))
    return [o.reshape(R, C) for o in outs]


def _place():
    x, y, c = lax.axis_index('x'), lax.axis_index('y'), lax.axis_index('c')
    return x, y, c, [(1 - x, y), (x, 1 - y), (1 - x, 1 - y)]


def _rcopy(src, dst, ssem, rsem, dev):
    return pltpu.make_async_remote_copy(src_ref=src, dst_ref=dst, send_sem=ssem, recv_sem=rsem, device_id=dev,
                                        device_id_type=MESH)


HBM = pl.BlockSpec(memory_space=pltpu.HBM)
SEM = pl.BlockSpec(memory_space=pltpu.SEMAPHORE)
EFFECT = pltpu.SideEffectType.DATAFLOW_SIDE_EFFECTING


def _in_hbm(a):
    return pltpu.with_memory_space_constraint(a, pltpu.HBM)


def _rows_part(shape, whole, half):
    return pl.ds(0, shape[0]) if whole else pl.ds(half * (shape[0] // 2), shape[0] // 2)


def gather_start(name, shards, whole):
    nT = len(shards)

    def body(*refs):
        srcs, lands = refs[:nT], refs[nT:2 * nT]
        ssem, rsem, token = refs[2 * nT], refs[2 * nT + 1], refs[-1]
        x, y, c, chips = _place()
        for t in range(nT):
            rows = _rows_part(shards[t].shape, whole[t], c)
            for k, (px, py) in enumerate(chips):
                _rcopy(srcs[t].at[rows], lands[t].at[2 * x + y, rows], ssem.at[3 * t + k], rsem.at[3 * t + k],
                       (px, py, c)).start()
        token[...] = jnp.zeros_like(token)

    zones = [lax.empty((N_CHIPS,) + s.shape, s.dtype) for s in shards]
    outs = pl.pallas_call(
        body, name=name,
        out_shape=(pltpu.SemaphoreType.DMA((3 * nT,)), pltpu.SemaphoreType.DMA((3 * nT,)),
                   *[pltpu.HBM(s.shape, s.dtype) for s in shards], *[pltpu.HBM(z.shape, z.dtype) for z in zones],
                   jax.ShapeDtypeStruct((8, LANES), F32)),
        in_specs=[HBM] * (2 * nT), out_specs=(SEM, SEM, *[HBM] * (2 * nT), pl.BlockSpec(memory_space=pltpu.VMEM)),
        input_output_aliases={i: 2 + i for i in range(2 * nT)},
        compiler_params=pltpu.CompilerParams(has_side_effects=EFFECT))(*[_in_hbm(a) for a in list(shards) + zones])
    return outs[0], outs[1], outs[2:2 + nT], outs[2 + nT:2 + 2 * nT], outs[-1]


def gather_wait(name, t, shard, zone, ssem, rsem, after, whole):
    after = after if isinstance(after, (list, tuple)) else [after]

    def body(src_ref, land_ref, ssem_ref, rsem_ref, *rest):
        x, y, c, chips = _place()
        rows = _rows_part(shard.shape, whole, c)
        for k, (px, py) in enumerate(chips):
            cp = _rcopy(src_ref.at[rows], land_ref.at[2 * px + py, rows], ssem_ref.at[3 * t + k], rsem_ref.at[3 * t + k],
                        (px, py, c))
            cp.wait_send()
            cp.wait_recv()

    return pl.pallas_call(
        body, name=name, out_shape=(pltpu.HBM(shard.shape, shard.dtype), pltpu.HBM(zone.shape, zone.dtype)),
        in_specs=(HBM, HBM, SEM, SEM, *[ANY] * len(after)), out_specs=(HBM, HBM), input_output_aliases={0: 0, 1: 1},
        compiler_params=pltpu.CompilerParams(has_side_effects=EFFECT))(shard, zone, ssem, rsem, *after)


def pair_swap(name, zone):
    hr = zone.shape[1] // 2

    def body(z_in, z_ref, ssem, rsem):
        x, y, c, chips = _place()
        cps = []
        for k, (px, py) in enumerate(chips):
            blk = z_ref.at[2 * px + py, pl.ds(c * hr, hr)]
            cps.append(_rcopy(blk, blk, ssem.at[k], rsem.at[k], (x, y, 1 - c)))
            cps[-1].start()
        for k, (px, py) in enumerate(chips):
            blk = z_ref.at[2 * px + py, pl.ds((1 - c) * hr, hr)]
            _rcopy(blk, blk, ssem.at[k], rsem.at[k], (x, y, 1 - c)).wait_recv()
        for cp in cps:
            cp.wait_send()

    return pl.pallas_call(
        body, name=name, in_specs=[ANY], out_specs=ANY, out_shape=jax.ShapeDtypeStruct(zone.shape, zone.dtype),
        input_output_aliases={0: 0},
        scratch_shapes=[pltpu.SemaphoreType.DMA((3,)), pltpu.SemaphoreType.DMA((3,))],
        compiler_params=_params())(zone)


N_SENDERS = 7


def _scatter_copies(g_ref, l_ref, ssem, rsem):
    x, y, c, chips = _place()
    cps = []
    for k, (px, py) in enumerate(chips):
        for d in range(2):
            to = (c + d) % 2
            cps.append(_rcopy(g_ref.at[2 * px + py, to], l_ref.at[2 * k + d], ssem.at[2 * k + d], rsem.at[2 * k + d],
                              (px, py, to)))
    cps.append(_rcopy(g_ref.at[2 * x + y, 1 - c], l_ref.at[6], ssem.at[6], rsem.at[6], (x, y, 1 - c)))
    return cps


def scatter_start(name, g):
    def body(g_ref, l_ref, ssem, rsem, g_out, l_out, token):
        for cp in _scatter_copies(g_ref, l_ref, ssem, rsem):
            cp.start()
        token[...] = jnp.zeros_like(token)

    zone = lax.empty((N_SENDERS,) + g.shape[2:], g.dtype)
    return pl.pallas_call(
        body, name=name,
        out_shape=(pltpu.SemaphoreType.DMA((N_SENDERS,)), pltpu.SemaphoreType.DMA((N_SENDERS,)),
                   pltpu.HBM(g.shape, g.dtype), pltpu.HBM(zone.shape, zone.dtype), jax.ShapeDtypeStruct((8, LANES), F32)),
        in_specs=[HBM, HBM], out_specs=(SEM, SEM, HBM, HBM, pl.BlockSpec(memory_space=pltpu.VMEM)),
        input_output_aliases={0: 2, 1: 3},
        compiler_params=pltpu.CompilerParams(has_side_effects=EFFECT))(_in_hbm(g), _in_hbm(zone))


def scatter_wait(name, g, zone, ssem, rsem, after):
    def body(g_ref, l_ref, ssem_ref, rsem_ref, after_ref, g_out, l_out):
        for cp in _scatter_copies(g_ref, l_ref, ssem_ref, rsem_ref):
            cp.wait_send()
            cp.wait_recv()

    return pl.pallas_call(
        body, name=name, out_shape=(pltpu.HBM(g.shape, g.dtype), pltpu.HBM(zone.shape, zone.dtype)),
        in_specs=(HBM, HBM, SEM, SEM, ANY), out_specs=(HBM, HBM), input_output_aliases={0: 0, 1: 1},
        compiler_params=pltpu.CompilerParams(has_side_effects=EFFECT))(g, zone, ssem, rsem, after)


def sum_parts(name, g, landed, chip_idx, c_idx):
    hr, C = g.shape[2:]
    tr = _row_tile(hr, C, min_rows=16)

    def body(me_ref, c_ref, g_ref, l_ref, o_ref):
        acc = g_ref[...].astype(F32)
        for s in range(N_SENDERS):
            acc = acc + l_ref[s].astype(F32)
        o_ref[...] = acc

    return pl.pallas_call(
        body, name=name,
        grid_spec=pltpu.PrefetchScalarGridSpec(
            num_scalar_prefetch=2, grid=(hr // tr,),
            in_specs=[pl.BlockSpec((None, None, tr, C), lambda i, me_ref, c_ref: (me_ref[0], c_ref[0], i, 0)),
                      pl.BlockSpec((N_SENDERS, tr, C), lambda i, me_ref, c_ref: (0, i, 0))],
            out_specs=pl.BlockSpec((tr, C), lambda i, me_ref, c_ref: (i, 0))),
        out_shape=jax.ShapeDtypeStruct((hr, C), F32),
        compiler_params=_params(('parallel',)))(chip_idx, c_idx, g, landed)


def pair_join(name, halves):
    nT = len(halves)

    def body(*refs):
        ins, outs = refs[:nT], refs[nT:2 * nT]
        ssem, rsem = refs[2 * nT:]
        x, y, c, _ = _place()
        cps = [_rcopy(ins[t], outs[t], ssem.at[t], rsem.at[t], (x, y, 1 - c)) for t in range(nT)]
        for cp in cps:
            cp.start()
        for cp in cps:
            cp.wait()

    return pl.pallas_call(
        body, name=name, in_specs=[ANY] * nT, out_specs=[ANY] * nT,
        out_shape=[jax.ShapeDtypeStruct(h.shape, h.dtype) for h in halves],
        scratch_shapes=[pltpu.SemaphoreType.DMA((nT,)), pltpu.SemaphoreType.DMA((nT,))],
        compiler_params=_params())(*halves)


N_DEVICES = 8


def _spread_copies(b_ref, l_ref, ssem, rsem):
    x, y, c, chips = _place()
    me = 4 * x + 2 * y + c
    pairs = []
    for px, py, pc in [(px, py, pc) for px, py in chips for pc in (c, 1 - c)] + [(x, y, 1 - c)]:
        it = 4 * px + 2 * py + pc
        pairs.append((_rcopy(b_ref, l_ref.at[me], ssem.at[it], rsem.at[me], (px, py, pc)),
                      _rcopy(b_ref, l_ref.at[it], ssem.at[it], rsem.at[it], (px, py, pc))))
    return pairs


def spread_start(name, buf):
    def body(b_ref, l_ref, ssem, rsem, b_out, l_out, token):
        for mine, _ in _spread_copies(b_ref, l_ref, ssem, rsem):
            mine.start()
        token[...] = jnp.zeros_like(token)

    zone = lax.empty((N_DEVICES,) + buf.shape, buf.dtype)
    return pl.pallas_call(
        body, name=name,
        out_shape=(pltpu.SemaphoreType.DMA((N_DEVICES,)), pltpu.SemaphoreType.DMA((N_DEVICES,)),
                   pltpu.HBM(buf.shape, buf.dtype), pltpu.HBM(zone.shape, zone.dtype), jax.ShapeDtypeStruct((8, LANES), F32)),
        in_specs=[HBM, HBM], out_specs=(SEM, SEM, HBM, HBM, pl.BlockSpec(memory_space=pltpu.VMEM)),
        input_output_aliases={0: 2, 1: 3},
        compiler_params=pltpu.CompilerParams(has_side_effects=EFFECT))(_in_hbm(buf), _in_hbm(zone))


def spread_wait(name, buf, zone, ssem, rsem, after):
    def body(b_ref, l_ref, ssem_ref, rsem_ref, after_ref, b_out, l_out):
        for mine, theirs in _spread_copies(b_ref, l_ref, ssem_ref, rsem_ref):
            mine.wait_send()
            theirs.wait_recv()

    return pl.pallas_call(
        body, name=name, out_shape=(pltpu.HBM(buf.shape, buf.dtype), pltpu.HBM(zone.shape, zone.dtype)),
        in_specs=(HBM, HBM, SEM, SEM, ANY), out_specs=(HBM, HBM), input_output_aliases={0: 0, 1: 1},
        compiler_params=pltpu.CompilerParams(has_side_effects=EFFECT))(buf, zone, ssem, rsem, after)


def sum_devices(name, zone):
    _, R, C = zone.shape
    tr = _row_tile(R, C)

    def body(z_ref, o_ref):
        acc = z_ref[0]
        for d in range(1, N_DEVICES):
            acc = acc + z_ref[d]
        o_ref[...] = acc

    return pl.pallas_call(
        body, name=name, grid=(R // tr,), in_specs=[pl.BlockSpec((N_DEVICES, tr, C), lambda i: (0, i, 0))],
        out_specs=pl.BlockSpec((tr, C), lambda i: (i, 0)), out_shape=jax.ShapeDtypeStruct((R, C), F32),
        compiler_params=_params(('parallel',)))(zone)


class _InWindows:
    def __init__(self, FW, LW, H, C):
        gap = LANES - H
        padded = lambda o: o if o < 3 * FW + H else o + gap
        self.width = 3 * FW + LANES + 2 * LW
        self.first = [padded(C * j) // LANES for j in range(N_CHIPS)]
        self.blocks = max(padded(C * (j + 1) - 1) // LANES - self.first[j] + 1 for j in range(N_CHIPS))
        assert all((b + self.blocks) * LANES <= self.width for b in self.first)
        self.cols = self.blocks * LANES
        self.runs = []
        for j in range(N_CHIPS):
            cut = min(max(3 * FW + H - C * j, 0), C)
            spans = [(0, cut), (cut, C)]
            self.runs.append([(t0, t1, padded(C * j + t0) - LANES * self.first[j]) for t0, t1 in spans if t1 > t0])

    def to_window(self, shard, chip):
        def place(j, s):
            parts, pos = [], 0
            for t0, t1, w0 in self.runs[j]:
                parts += [jnp.zeros((s.shape[0], w0 - pos), s.dtype), s[:, t0:t1]]
                pos = w0 + t1 - t0
            parts.append(jnp.zeros((s.shape[0], self.cols - pos), s.dtype))
            return jnp.concatenate([p for p in parts if p.shape[1]], axis=1)
        return lax.switch(chip, [functools.partial(place, j) for j in range(N_CHIPS)], shard)

    def from_window(self, win, chip):
        def take(j, w):
            return jnp.concatenate([w[:, w0:w0 + t1 - t0] for t0, t1, w0 in self.runs[j]], axis=1)
        return lax.switch(chip, [functools.partial(take, j) for j in range(N_CHIPS)], win)

    def assemble(self, zone):
        total = None
        for j in range(N_CHIPS):
            lead = self.first[j] * LANES
            part = jnp.pad(zone[j], ((0, 0), (lead, self.width - lead - self.cols)))
            total = part if total is None else total + part
        return total

    def windows(self, padded_matrix):
        return jnp.stack([padded_matrix[:, b * LANES:b * LANES + self.cols] for b in self.first])


_PACK = 8 * LANES


def _pack(arrs):
    flat = []
    for a in arrs:
        v = a.reshape(-1).astype(F32)
        flat.append(jnp.pad(v, (0, (-v.shape[0]) % _PACK)))
    return jnp.concatenate(flat).reshape(-1, LANES)


def _unpack(buf, shapes):
    out, off = [], 0
    flat = buf.reshape(-1)
    for sh in shapes:
        n = math.prod(sh)
        out.append(flat[off:off + n].reshape(sh))
        off += n + (-n) % _PACK
    return out


def kernel(x, mem, g_mix, w_in, b_f, g_q, g_k, conv_w, conv_b, w_ra, b_ra, w_ri, b_ri, lam, g_fox_out, g_lru_out, w_out, g_xattn, g_mem, w_cq, w_ckv, g_cq, g_ck, w_co, g_ffn, w_gate_up, w_down, loss_target, m_g_mix, m_w_in, m_b_f, m_g_q, m_g_k, m_conv_w, m_conv_b, m_w_ra, m_b_ra, m_w_ri, m_b_ri, m_lam, m_g_fox_out, m_g_lru_out, m_w_out, m_g_xattn, m_g_mem, m_w_cq, m_w_ckv, m_g_cq, m_g_ck, m_w_co, m_g_ffn, m_w_gate_up, m_w_down, v_g_mix, v_w_in, v_b_f, v_g_q, v_g_k, v_conv_w, v_conv_b, v_w_ra, v_b_ra, v_w_ri, v_b_ri, v_lam, v_g_fox_out, v_g_lru_out, v_w_out, v_g_xattn, v_g_mem, v_w_cq, v_w_ckv, v_g_cq, v_g_ck, v_w_co, v_g_ffn, v_w_gate_up, v_w_down):
    given = dict(locals())
    W = {n: given[n][0] for n in WEIGHTS}
    M1 = {n: given['m_' + n][0] for n in WEIGHTS}
    V1 = {n: given['v_' + n][0] for n in WEIGHTS}
    xs, ms, tgt = x[0], mem[0], loss_target[0]
    S, D = xs.shape
    H = W['b_f'].shape[0]
    FW = H * HEAD_DIM
    LW = W['lam'].shape[0]
    nb = W['w_ra'].shape[0]
    XW = W['w_cq'].shape[1]
    F = W['w_down'].shape[0] * N_CHIPS
    IN_W = W['w_in'].shape[1] * N_CHIPS
    assert FW == LW and LW == nb * LANES and IN_W == 3 * FW + H + 2 * LW and H <= 8
    T = _tile(S, (512, 256, 128))
    c_idx = lax.axis_index('c').astype(jnp.int32).reshape(1)
    chip = 2 * lax.axis_index('x') + lax.axis_index('y')
    chip_idx = chip.astype(jnp.int32).reshape(1)
    vec = lambda n: W[n].reshape(1, -1)

    wins = _InWindows(FW, LW, H, W['w_in'].shape[1])
    started = {}
    g_tok = jnp.zeros((1, 1), F32)
    for call, names in (('gather_start_first', ['conv_w', 'w_in']), ('gather_start_rest', BIG[1:])):
        own = [W[n].reshape(-1, LANES) if n == 'conv_w' else W[n].astype(BF16) + g_tok.astype(BF16) for n in names]
        own = [wins.to_window(o, chip) if n == 'w_in' else o for n, o in zip(names, own)]
        ssem, rsem, srcs, zones, tok = gather_start(call, own, [n == 'conv_w' for n in names])
        g_tok = tok[0:1, 0:1]
        started.update({n: (t, srcs[t], zones[t], ssem, rsem) for t, n in enumerate(names)})

    def fetch(n, after):
        t, g_src, g_zone, g_ssem, g_rsem = started[n]
        src, zone = gather_wait('gather_wait_' + n, t, g_src, g_zone, g_ssem, g_rsem, after, n == 'conv_w')
        if n != 'conv_w':
            zone = pair_swap('pair_swap_' + n, zone)
        return lax.dynamic_update_index_in_dim(zone, src, chip, 0)

    b_f_pad = jnp.pad(vec('b_f'), ((0, 0), (0, LANES - H)))
    u_off, g_off = 3 * FW // LANES, (3 * FW + LW) // LANES

    h1 = norm_fwd('norm_mix', xs, vec('g_mix') + g_tok[0:1, 0:1])
    conv_full = fetch('conv_w', h1).reshape(N_CHIPS, CONV_W, LW // N_CHIPS).transpose(1, 0, 2).reshape(CONV_W, LW)
    w_in_pad = wins.assemble(fetch('w_in', [h1, M1['w_in'], V1['w_in']]))
    w5 = jnp.concatenate([w_in_pad[:, :3 * FW], w_in_pad[:, 3 * FW + LANES:]], axis=1)
    wf = w_in_pad[:, 3 * FW:3 * FW + LANES]
    proj = _mm('proj_in', h1, w5, 'nn', F32)
    f_raw = _mm('proj_f', h1, wf, 'nn', F32)
    qn, kn, vb = qkv_fwd(proj, vec('g_q'), vec('g_k'), FW)
    cc = fgate_fwd(f_raw, b_f_pad)
    ct = cc[:, :8].T
    o_fox, lse = fox_fwd(qn, kn, vb, cc, ct, T)
    lru_w = (conv_full, vec('conv_b'), W['w_ra'], vec('b_ra'), W['w_ri'], vec('b_ri'), vec('lam'))
    y_lru = lru_fwd(proj, *lru_w, u_off, g_off)
    mixn = mix_fwd(o_fox, y_lru, vec('g_fox_out'), vec('g_lru_out'))
    w_out_f = fetch('w_out', mixn).reshape(2 * FW, D)
    x1 = _mm('proj_out', mixn, w_out_f, 'nn', F32, res=xs)

    hq = norm_fwd('norm_xq', x1, vec('g_xattn'))
    mn = norm_fwd('norm_mem', ms, vec('g_mem'))
    w_cq_f = fetch('w_cq', hq).reshape(D, XW)
    w_ckv_f = fetch('w_ckv', hq).reshape(D, 2 * XW)
    cq_raw = _mm('proj_cq', hq, w_cq_f, 'nn', F32)
    ckv = _mm('proj_ckv', mn, w_ckv_f, 'nn', F32)
    o_x = xattn_fwd(cq_raw, ckv, vec('g_cq'), vec('g_ck'))
    w_co_g = fetch('w_co', o_x)
    x2 = _mm_colsharded('proj_co', o_x, w_co_g, F32, res=x1)

    hf = norm_fwd('norm_ffn', x2, vec('g_ffn'))
    w_gu_g = fetch('w_gate_up', hf)
    gu = _mm_colsharded('proj_gate_up', hf, w_gu_g, F32)
    act = swiglu_fwd(gu, F)
    w_down_f = fetch('w_down', act).reshape(F, D)
    yv = _mm('proj_down', act, w_down_f, 'nn', F32, res=x2)
    dy, dyb, loss_blk = loss_head(yv, tgt)

    gw, pending = {}, []

    def reduce_begin(n, g):
        sp = g.reshape(N_CHIPS, 2, g.shape[1] // 2, g.shape[2])
        ssem, rsem, sp, zone, tok = scatter_start('scatter_start_' + n, sp)
        pending.append((n, sp, zone, ssem, rsem))
        return tok[0:1, 0:1]

    dact = _mm('bwd_down_x', dyb, w_down_f, 'nt', F32)
    t_down = reduce_begin('w_down', _mm('bwd_down_w', act, dyb, 'tn', BF16).reshape(N_CHIPS, F // N_CHIPS, D))
    dgu = swiglu_bwd(gu, dact, F, t_down)
    dhf = _mm_colsharded_t('bwd_gate_up_x', dgu, w_gu_g, F32)
    t_gu = reduce_begin('w_gate_up', _mm_grad_colsharded('bwd_gate_up_w', hf, dgu, N_CHIPS, BF16))
    dx2, dx2b, gw['g_ffn'] = norm_bwd('norm_ffn_bwd', x2, vec('g_ffn') + t_down + t_gu, dhf, res=dy)

    do_x = _mm_colsharded_t('bwd_co_x', dx2b, w_co_g, BF16)
    t_co = reduce_begin('w_co', _mm_grad_colsharded('bwd_co_w', o_x, dx2b, N_CHIPS, BF16))
    dcq_raw, dckv, gw['g_cq'], gw['g_ck'] = xattn_bwd(cq_raw, ckv, vec('g_cq') + t_co, vec('g_ck'), do_x)
    dhq = _mm('bwd_cq_x', dcq_raw, w_cq_f, 'nt', F32)
    t_cq = reduce_begin('w_cq', _mm('bwd_cq_w', hq, dcq_raw, 'tn', BF16).reshape(N_CHIPS, D // N_CHIPS, XW))
    dmn = _mm('bwd_ckv_x', dckv, w_ckv_f, 'nt', F32)
    t_ckv = reduce_begin('w_ckv', _mm('bwd_ckv_w', mn, dckv, 'tn', BF16).reshape(N_CHIPS, D // N_CHIPS, 2 * XW))
    (gw['g_mem'],) = norm_bwd('norm_mem_bwd', ms, vec('g_mem'), dmn, want_dx=False)
    dx1, dx1b, gw['g_xattn'] = norm_bwd('norm_xq_bwd', x1, vec('g_xattn') + t_cq + t_ckv, dhq, res=dx2)

    dmix = _mm('bwd_out_x', dx1b, w_out_f, 'nt', F32)
    t_out = reduce_begin('w_out', _mm('bwd_out_w', mixn, dx1b, 'tn', BF16).reshape(N_CHIPS, 2 * FW // N_CHIPS, D))
    do_fox, delta, dy_lru, gw['g_fox_out'], gw['g_lru_out'] = mix_bwd(o_fox, y_lru, vec('g_fox_out') + t_out,
                                                                     vec('g_lru_out'), dmix)
    (du, dgate, gw['conv_w'], gw['conv_b'], gw['w_ra'], gw['b_ra'], gw['w_ri'], gw['b_ri'],
     gw['lam']) = lru_bwd(proj, dy_lru, *lru_w, u_off, g_off)
    early = [n for n in SMALL if n not in ('g_q', 'g_k', 'b_f', 'g_mix')]
    late = [n for n in SMALL if n not in early]
    e_ssem, e_rsem, e_buf, e_zone, e_tok = spread_start('spread_start_early', _pack([gw[n] for n in early]))
    dqn, delta2 = fox_bwd_q(qn, kn, vb, do_fox, cc, ct, lse, delta, T)
    dkn, dv, dct = fox_bwd_kv(qn, kn, vb, do_fox, cc, ct, lse, delta2, T)
    dq, dk, gw['g_q'], gw['g_k'] = qkv_bwd(proj, vec('g_q') + e_tok[0:1, 0:1], vec('g_k'), dqn, dkn, FW)
    dc = jnp.pad(dct.reshape(H, S).T, ((0, 0), (0, LANES - H)))
    df, db_f = fgate_bwd(f_raw, b_f_pad, dc, H)
    gw['b_f'] = db_f[:, :H]
    dproj = jnp.concatenate([dq, dk, dv, du, dgate], axis=1)
    dw5 = _mm('bwd_in_w', h1, dproj, 'tn', BF16)
    dwf = _mm('bwd_f_w', h1, df, 'tn', BF16)
    t_in = reduce_begin('w_in', wins.windows(jnp.concatenate([dw5[:, :3 * FW], dwf, dw5[:, 3 * FW:]], axis=1)))
    dh_a = _mm('bwd_f_x', df, wf, 'nt', F32)
    dh1 = _mm('bwd_in_x', dproj, w5, 'nt', F32, res=dh_a)
    grad_x, _, gw['g_mix'] = norm_bwd('norm_mix_bwd', xs, vec('g_mix') + t_in, dh1, res=dx1)
    l_ssem, l_rsem, l_buf, l_zone, _ = spread_start('spread_start_late',
                                                    _pack([gw[n] for n in late] + [loss_blk[0:1, 0:1]]))

    grads, delta_w, new_m, new_v = {}, {}, {}, {}
    done = grad_x
    for n, part, zone, ssem, rsem in pending:
        part, landed = scatter_wait('scatter_wait_' + n, part, zone, ssem, rsem, done)
        mine = sum_parts('sum_parts_' + n, part, landed, chip_idx, c_idx)
        (other,) = pair_join('pair_join_' + n, [mine])
        if n == 'w_in':
            mine, other = wins.from_window(mine, chip), wins.from_window(other, chip)
        grads[n], delta_w[n], new_m[n], new_v[n] = adamw_halves('adamw_' + n, W[n], mine, other, M1[n], V1[n], c_idx)
        done = delta_w[n]

    device = 4 * lax.axis_index('x') + 2 * lax.axis_index('y') + lax.axis_index('c')
    summed = {}
    for tag, names, buf, zone, ssem, rsem in (('early', early, e_buf, e_zone, e_ssem, e_rsem),
                                              ('late', late + ['loss'], l_buf, l_zone, l_ssem, l_rsem)):
        buf, zone = spread_wait('spread_wait_' + tag, buf, zone, ssem, rsem, done)
        total = sum_devices('sum_small_' + tag, lax.dynamic_update_index_in_dim(zone, buf, device, 0))
        summed.update(zip(names, _unpack(total, [gw[n].shape if n != 'loss' else (1, 1) for n in names])))
    loss = summed['loss'].reshape(())
    for n in SMALL:
        g = summed[n]
        grads[n] = g.reshape(W[n].shape) if n != 'conv_w' else lax.dynamic_slice_in_dim(
            g, chip * (LW // N_CHIPS), LW // N_CHIPS, axis=1)
    packs = [_pack([d[n] for n in SMALL]) for d in (W, grads, M1, V1)]
    shapes = [W[n].shape for n in SMALL]
    for d, res in zip((delta_w, new_m, new_v), adamw('adamw_small', *packs)):
        d.update(zip(SMALL, _unpack(res, shapes)))

    lead = lambda d: [d[n][None] for n in WEIGHTS]
    return (loss, grad_x[None], *lead(grads), *lead(delta_w), *lead(new_m), *lead(new_v))
```

```python
import functools
import math

import jax
import jax.numpy as jnp
from jax import lax
from jax.experimental import pallas as pl
from jax.experimental.pallas import tpu as pltpu

F32 = jnp.float32
BF16 = jnp.bfloat16
HEAD_DIM = 128
LANES = 128
LRU_C = 8.0
RMS_EPS = 1e-6
CONV_W = 4
ADAM_LR = 0.001
ADAM_B1 = 0.9
ADAM_B2 = 0.999
ADAM_EPS = 1e-08
ADAM_WD = 0.01
ADAM_STEP = 10
VMEM_LIMIT = 56 * 1024 * 1024
N_CHIPS = 4
MESH = pl.DeviceIdType.MESH
ANY = pl.BlockSpec(memory_space=pl.ANY)

WEIGHTS = ['g_mix', 'w_in', 'b_f', 'g_q', 'g_k', 'conv_w', 'conv_b', 'w_ra', 'b_ra', 'w_ri', 'b_ri', 'lam',
           'g_fox_out', 'g_lru_out', 'w_out', 'g_xattn', 'g_mem', 'w_cq', 'w_ckv', 'g_cq', 'g_ck', 'w_co', 'g_ffn',
           'w_gate_up', 'w_down']
BIG = ['w_in', 'w_out', 'w_cq', 'w_ckv', 'w_co', 'w_gate_up', 'w_down']
SMALL = [n for n in WEIGHTS if n not in BIG]


def _params(sem=None):
    if sem is None:
        return pltpu.CompilerParams(vmem_limit_bytes=VMEM_LIMIT)
    return pltpu.CompilerParams(dimension_semantics=sem, vmem_limit_bytes=VMEM_LIMIT)


def _tile(n, cands):
    for t in cands:
        if n % t == 0:
            return t
    return n


ROW_BLOCK_BYTES = 1 << 20


def _row_tile(n_rows, n_cols, min_rows=8):
    cands = [t for t in (512, 256, 128, 64, 32, 16, 8) if t >= min_rows and t * n_cols * 4 <= ROW_BLOCK_BYTES]
    return _tile(n_rows, cands or [min_rows])


def _sigmoid(z):
    return 1.0 / (1.0 + jnp.exp(-z))


def _softplus(z):
    return jnp.maximum(z, 0.0) + jnp.log(1.0 + jnp.exp(-jnp.abs(z)))


def _neg_expm1(z):
    series = -z * (1.0 + z * (0.5 + z * (1.0 / 6.0 + z * (1.0 / 24.0 + z * (1.0 / 120.0)))))
    return jnp.where(z > -0.25, series, 1.0 - jnp.exp(z))


_GELU_K = math.sqrt(2.0 / math.pi)


def _gelu_and_grad(z):
    inner = _GELU_K * (z + 0.044715 * z * z * z)
    t = jnp.tanh(inner)
    g = 0.5 * z * (1.0 + t)
    dg = 0.5 * (1.0 + t) + 0.5 * z * (1.0 - t * t) * _GELU_K * (1.0 + 3.0 * 0.044715 * z * z)
    return g, dg


def _rms(xv, g):
    r = lax.rsqrt(jnp.mean(xv * xv, axis=-1, keepdims=True) + RMS_EPS)
    return xv * r * g


def _rms_bwd(xv, g, dy):
    r = lax.rsqrt(jnp.mean(xv * xv, axis=-1, keepdims=True) + RMS_EPS)
    xh = xv * r
    dyg = dy * g
    dx = r * (dyg - xh * jnp.mean(dyg * xh, axis=-1, keepdims=True))
    return dx, jnp.sum(dy * xh, axis=0, keepdims=True)


def _heads(fn, n_heads, *arrs):
    outs = [fn(*[a[:, h * HEAD_DIM:(h + 1) * HEAD_DIM] for a in arrs]) for h in range(n_heads)]
    first = jnp.concatenate([o[0] for o in outs], axis=1) if n_heads > 1 else outs[0][0]
    rest = [functools.reduce(lambda p, q: p + q, [o[i] for o in outs]) for i in range(1, len(outs[0]))]
    return (first, *rest)


def _split3(v):
    hi = v.astype(BF16)
    r1 = v - hi.astype(F32)
    mid = r1.astype(BF16)
    lo = (r1 - mid.astype(F32)).astype(BF16)
    return hi, mid, lo


def _acc_out(ref, first, val):
    @pl.when(first)
    def _():
        ref[...] = val

    @pl.when(jnp.logical_not(first))
    def _():
        ref[...] += val


_DIMS = {'nn': (((1,), (0,)), ((), ())), 'nt': (((1,), (1,)), ((), ())), 'tn': (((0,), (0,)), ((), ()))}


MM_VMEM_BYTES = 36 * 1024 * 1024


MXU_FLOPS = 800e12
HBM_BYTES_S = 3.2e12
VMEM_ADD_BYTES_S = 8e12
STEP_S = 0.35e-6


def _k_tile(K, tm, tn, a, b, o_dtype, res):
    fixed = tm * tn * (2 * jnp.dtype(o_dtype).itemsize + 4 + (8 if res is not None else 0))
    per_k = 2 * (tm * a.dtype.itemsize + tn * b.dtype.itemsize)
    per_k += 2 * tm * (a.dtype.itemsize > 2) + 2 * tn * (b.dtype.itemsize > 2)
    units = K // LANES
    for d in sorted((d for d in range(1, units + 1) if units % d == 0), reverse=True):
        if fixed + d * LANES * per_k <= MM_VMEM_BYTES:
            return d * LANES
    return None


def _mm_tiles(M, N, K, k_span, a, b, o_dtype, res, tn_cands=(2048, 1024, 512, 256, 128)):
    best = None
    for tm in (2048, 1024, 512, 256, 128):
        for tn in tn_cands:
            if M % tm or N % tn:
                continue
            tk = _k_tile(k_span, tm, tn, a, b, o_dtype, res)
            if tk is None:
                continue
            nk = K // tk
            traffic = (M * K * a.dtype.itemsize * (N // tn) + K * N * b.dtype.itemsize * (M // tm)
                       + M * N * (jnp.dtype(o_dtype).itemsize + (4 if res is not None else 0)))
            work = 2.0 * M * N * K / MXU_FLOPS + (M * N * 4 * nk / VMEM_ADD_BYTES_S if nk > 1 else 0.0)
            t = max(work, traffic / HBM_BYTES_S) + (M // tm) * (N // tn) * nk * STEP_S
            if best is None or t < best[0]:
                best = (t, tm, tn, tk)
    assert best is not None, (M, N, K)
    return best[1:]


def _mm_call(name, a, b, mode, grid, a_spec, b_spec, o_spec, o_shape, o_dtype, acc_shape, res=None):
    nk = grid[2]
    dn = _DIMS[mode]

    def body(*refs):
        a_ref, b_ref = refs[:2]
        r_ref = refs[2] if res is not None else None
        o_ref = refs[3] if res is not None else refs[2]
        part = lax.dot_general(a_ref[...].astype(BF16), b_ref[...].astype(BF16), dn, preferred_element_type=F32)

        def finish(r):
            if r_ref is not None:
                r = r + r_ref[...]
            o_ref[...] = r.astype(o_dtype)

        if nk == 1:
            finish(part)
            return
        acc = refs[-1]
        k = pl.program_id(2)

        @pl.when(k == 0)
        def _():
            acc[...] = part

        @pl.when(k > 0)
        def _():
            acc[...] += part

        @pl.when(k == nk - 1)
        def _():
            finish(acc[...])

    ins = [a, b] + ([] if res is None else [res])
    specs = [a_spec, b_spec] + ([] if res is None else [o_spec])
    return pl.pallas_call(
        body, name=name, grid=grid, in_specs=specs, out_specs=o_spec,
        out_shape=jax.ShapeDtypeStruct(o_shape, o_dtype),
        scratch_shapes=[] if nk == 1 else [pltpu.VMEM(acc_shape, F32)],
        compiler_params=_params(('parallel', 'parallel', 'arbitrary')))(*ins)


def _mm(name, a, b, mode, o_dtype, res=None):
    if mode == 'tn':
        K, M = a.shape
    else:
        M, K = a.shape
    N = b.shape[0] if mode == 'nt' else b.shape[1]
    tm, tn, tk = _mm_tiles(M, N, K, K, a, b, o_dtype, res)
    a_spec = (pl.BlockSpec((tk, tm), lambda m, n, k: (k, m)) if mode == 'tn'
              else pl.BlockSpec((tm, tk), lambda m, n, k: (m, k)))
    b_spec = (pl.BlockSpec((tn, tk), lambda m, n, k: (n, k)) if mode == 'nt'
              else pl.BlockSpec((tk, tn), lambda m, n, k: (k, n)))
    o_spec = pl.BlockSpec((tm, tn), lambda m, n, k: (m, n))
    return _mm_call(name, a, b, mode, (M // tm, N // tn, K // tk), a_spec, b_spec, o_spec, (M, N), o_dtype,
                    (tm, tn), res)


def _mm_colsharded(name, a, w, o_dtype, res=None):
    M, K = a.shape
    J, _, Nj = w.shape
    tm, tn, tk = _mm_tiles(M, J * Nj, K, K, a, w, o_dtype, res,
                           tn_cands=[t for t in (2816, 1408, 1024, 512, 256, 128) if Nj % t == 0])
    per = Nj // tn
    return _mm_call(name, a, w, 'nn', (M // tm, J * per, K // tk),
                    pl.BlockSpec((tm, tk), lambda m, n, k: (m, k)),
                    pl.BlockSpec((None, tk, tn), lambda m, n, k: (n // per, k, n % per)),
                    pl.BlockSpec((tm, tn), lambda m, n, k: (m, n)), (M, J * Nj), o_dtype, (tm, tn), res)


def _planes_spec(arr, rows, cols, row_of, col_of):
    if arr.ndim == 2:
        return pl.BlockSpec((rows, cols), lambda m, n, k: (row_of(m, n, k), col_of(m, n, k)))
    per_plane = arr.shape[2] // cols
    return pl.BlockSpec((None, rows, cols),
                        lambda m, n, k: (col_of(m, n, k) // per_plane, row_of(m, n, k), col_of(m, n, k) % per_plane))


def _mm_colsharded_t(name, a, w, o_dtype):
    M = a.shape[-2]
    J, K, Nj = w.shape
    tm, tn, tk = _mm_tiles(M, K, J * Nj, Nj, a, w, o_dtype, None)
    per = Nj // tk
    return _mm_call(name, a, w, 'nt', (M // tm, K // tn, J * per),
                    _planes_spec(a, tm, tk, lambda m, n, k: m, lambda m, n, k: k),
                    pl.BlockSpec((None, tn, tk), lambda m, n, k: (k // per, n, k % per)),
                    pl.BlockSpec((tm, tn), lambda m, n, k: (m, n)), (M, K), o_dtype, (tm, tn))


def _mm_grad_colsharded(name, a, dy, J, o_dtype):
    S, M = a.shape
    Nj = dy.shape[-1] * (dy.shape[0] if dy.ndim == 3 else 1) // J
    tm, tn, tk = _mm_tiles(M, J * Nj, S, S, a, dy, o_dtype, None,
                           tn_cands=[t for t in (2816, 1408, 1024, 512, 256, 128) if Nj % t == 0])
    per = Nj // tn
    return _mm_call(name, a, dy, 'tn', (M // tm, J * per, S // tk),
                    pl.BlockSpec((tk, tm), lambda m, n, k: (k, m)),
                    _planes_spec(dy, tk, tn, lambda m, n, k: k, lambda m, n, k: n),
                    pl.BlockSpec((None, tm, tn), lambda m, n, k: (n // per, m, n % per)), (J, M, Nj), o_dtype, (tm, tn))


def _rows_call(name, body, n_rows, tr, ins, outs):
    return pl.pallas_call(
        body, name=name, grid=(n_rows // tr,), in_specs=[s for _, s in ins], out_specs=[s for _, _, s in outs],
        out_shape=[jax.ShapeDtypeStruct(sh, dt) for sh, dt, _ in outs],
        compiler_params=_params(('arbitrary',)))(*[a for a, _ in ins])


def _rb(tr, w, cb=0):
    return pl.BlockSpec((tr, w), lambda i: (i, cb))


def _fb(shape):
    nd = len(shape)
    return pl.BlockSpec(shape, lambda i: (0,) * nd)


def norm_fwd(name, xv, g):
    S, D = xv.shape
    tr = _tile(S, (256, 128))

    def body(x_ref, g_ref, o_ref):
        o_ref[...] = _rms(x_ref[...], g_ref[...]).astype(BF16)

    return _rows_call(name, body, S, tr, [(xv, _rb(tr, D)), (g, _fb((1, D)))], [((S, D), BF16, _rb(tr, D))])[0]


def norm_bwd(name, xv, g, dy, res=None, want_dx=True):
    S, D = xv.shape
    tr = _tile(S, (256, 128))

    def body(*refs):
        if res is None:
            x_ref, g_ref, dy_ref = refs[:3]
            outs = refs[3:]
            r_ref = None
        else:
            x_ref, g_ref, dy_ref, r_ref = refs[:4]
            outs = refs[4:]
        dx, dg = _rms_bwd(x_ref[...], g_ref[...], dy_ref[...])
        if r_ref is not None:
            dx = dx + r_ref[...]
        if want_dx:
            outs[0][...] = dx
            outs[1][...] = dx.astype(BF16)
        _acc_out(outs[-1], pl.program_id(0) == 0, dg)

    ins = [(xv, _rb(tr, D)), (g, _fb((1, D))), (dy, _rb(tr, D))] + ([] if res is None else [(res, _rb(tr, D))])
    outs = ([((S, D), F32, _rb(tr, D)), ((S, D), BF16, _rb(tr, D))] if want_dx else []) + [((1, D), F32, _fb((1, D)))]
    return _rows_call(name, body, S, tr, ins, outs)


def qkv_fwd(proj, g_q, g_k, FW):
    S = proj.shape[0]
    H = FW // HEAD_DIM
    tr = _tile(S, (256, 128))

    def body(q_ref, k_ref, v_ref, gq_ref, gk_ref, qo, ko, vo):
        qo[...] = _heads(lambda t: (_rms(t, gq_ref[...]),), H, q_ref[...])[0].astype(BF16)
        ko[...] = _heads(lambda t: (_rms(t, gk_ref[...]),), H, k_ref[...])[0].astype(BF16)
        vo[...] = v_ref[...].astype(BF16)

    o = ((S, FW), BF16, _rb(tr, FW))
    return _rows_call('qkv_fwd', body, S, tr,
                      [(proj, _rb(tr, FW, 0)), (proj, _rb(tr, FW, 1)), (proj, _rb(tr, FW, 2)),
                       (g_q, _fb((1, HEAD_DIM))), (g_k, _fb((1, HEAD_DIM)))], [o, o, o])


def qkv_bwd(proj, g_q, g_k, dqn, dkn, FW):
    S = proj.shape[0]
    H = FW // HEAD_DIM
    tr = _tile(S, (256, 128))

    def body(q_ref, k_ref, gq_ref, gk_ref, dq_ref, dk_ref, dqo, dko, dgq, dgk):
        dq, gq = _heads(lambda t, d: _rms_bwd(t, gq_ref[...], d), H, q_ref[...], dq_ref[...])
        dk, gk = _heads(lambda t, d: _rms_bwd(t, gk_ref[...], d), H, k_ref[...], dk_ref[...])
        dqo[...] = dq.astype(BF16)
        dko[...] = dk.astype(BF16)
        first = pl.program_id(0) == 0
        _acc_out(dgq, first, gq)
        _acc_out(dgk, first, gk)

    o = ((S, FW), BF16, _rb(tr, FW))
    og = ((1, HEAD_DIM), F32, _fb((1, HEAD_DIM)))
    return _rows_call('qkv_bwd', body, S, tr,
                      [(proj, _rb(tr, FW, 0)), (proj, _rb(tr, FW, 1)), (g_q, _fb((1, HEAD_DIM))),
                       (g_k, _fb((1, HEAD_DIM))), (dqn, _rb(tr, FW)), (dkn, _rb(tr, FW))], [o, o, og, og])


def _tri(n, upper):
    r = lax.broadcasted_iota(jnp.int32, (n, n), 0)
    c = lax.broadcasted_iota(jnp.int32, (n, n), 1)
    return jnp.where((c >= r) if upper else (c <= r), 1.0, 0.0).astype(BF16)


def _blocked_cumsum(val, S, blk, reverse):
    tri = _tri(blk, reverse)
    order = range(S // blk - 1, -1, -1) if reverse else range(S // blk)
    carry = jnp.zeros((1, LANES), F32)
    outs = {}
    for bi in order:
        part = val[bi * blk:(bi + 1) * blk]
        acc = carry
        for piece in _split3(part):
            acc = acc + jnp.dot(tri, piece, preferred_element_type=F32)
        outs[bi] = acc
        carry = carry + jnp.sum(part, axis=0, keepdims=True)
    return jnp.concatenate([outs[bi] for bi in range(S // blk)], axis=0)


def fgate_fwd(f_raw, b_f_pad):
    S = f_raw.shape[0]
    blk = _tile(S, (256, 128))

    def body(f_ref, b_ref, c_ref):
        z = f_ref[...] + b_ref[...]
        c_ref[...] = _blocked_cumsum(-_softplus(-z), S, blk, False)

    return pl.pallas_call(body, name='fgate_fwd', grid=(1,), in_specs=[_fb((S, LANES)), _fb((1, LANES))],
                          out_specs=_fb((S, LANES)), out_shape=jax.ShapeDtypeStruct((S, LANES), F32),
                          compiler_params=_params(('arbitrary',)))(f_raw, b_f_pad)


def fgate_bwd(f_raw, b_f_pad, dc, H):
    S = f_raw.shape[0]
    blk = _tile(S, (256, 128))

    def body(f_ref, b_ref, dc_ref, df_ref, db_ref):
        z = f_ref[...] + b_ref[...]
        dlogf = _blocked_cumsum(dc_ref[...], S, blk, True)
        lane = lax.broadcasted_iota(jnp.int32, (S, LANES), 1)
        df = jnp.where(lane < H, dlogf * _sigmoid(-z), 0.0)
        df_ref[...] = df.astype(BF16)
        db_ref[...] = jnp.sum(df, axis=0, keepdims=True)

    return pl.pallas_call(body, name='fgate_bwd', grid=(1,),
                          in_specs=[_fb((S, LANES)), _fb((1, LANES)), _fb((S, LANES))],
                          out_specs=[_fb((S, LANES)), _fb((1, LANES))],
                          out_shape=[jax.ShapeDtypeStruct((S, LANES), BF16), jax.ShapeDtypeStruct((1, LANES), F32)],
                          compiler_params=_params(('arbitrary',)))(f_raw, b_f_pad, dc)


def _fox_logits(q, k, c_blk, ct_blk, h, T, diagonal):
    s = lax.dot_general(q, k, _DIMS['nt'], preferred_element_type=F32) * (1.0 / math.sqrt(HEAD_DIM))
    lane = lax.broadcasted_iota(jnp.int32, c_blk.shape, 1)
    cq = jnp.sum(jnp.where(lane == h, c_blk, 0.0), axis=1, keepdims=True)
    sub = lax.broadcasted_iota(jnp.int32, ct_blk.shape, 0)
    ck = jnp.sum(jnp.where(sub == h, ct_blk, 0.0), axis=0, keepdims=True)
    s = s + cq - ck
    if not diagonal:
        return s
    rows = lax.broadcasted_iota(jnp.int32, (T, T), 0)
    cols = lax.broadcasted_iota(jnp.int32, (T, T), 1)
    return jnp.where(cols <= rows, s, -jnp.inf)


def _below_and_on_diagonal(q_blk, k_blk, step):
    @pl.when(k_blk < q_blk)
    def _():
        step(False)

    @pl.when(k_blk == q_blk)
    def _():
        step(True)


def fox_fwd(qn, kn, vb, c, ct, T):
    S, FW = qn.shape
    H = FW // HEAD_DIM
    Hp = ct.shape[0]
    n = S // T

    def body(q_ref, k_ref, v_ref, c_ref, ct_ref, o_ref, lse_ref, m_s, l_s, acc_s):
        h, i, j = pl.program_id(0), pl.program_id(1), pl.program_id(2)

        @pl.when(j == 0)
        def _():
            m_s[...] = jnp.full_like(m_s, -jnp.inf)
            l_s[...] = jnp.zeros_like(l_s)
            acc_s[...] = jnp.zeros_like(acc_s)

        def step(diagonal):
            s = _fox_logits(q_ref[...], k_ref[...], c_ref[...], ct_ref[...], h, T, diagonal)
            m_new = jnp.maximum(m_s[...], jnp.max(s, axis=1, keepdims=True))
            alpha = jnp.exp(m_s[...] - m_new)
            p = jnp.exp(s - m_new)
            l_s[...] = alpha * l_s[...] + jnp.sum(p, axis=1, keepdims=True)
            acc_s[...] = alpha * acc_s[...] + jnp.dot(p.astype(BF16), v_ref[...], preferred_element_type=F32)
            m_s[...] = m_new

        _below_and_on_diagonal(i, j, step)

        @pl.when(j == i)
        def _():
            o_ref[...] = acc_s[...] / l_s[...]
            lse_ref[...] = jnp.broadcast_to(m_s[...] + jnp.log(l_s[...]), (T, LANES))

    qs = pl.BlockSpec((T, HEAD_DIM), lambda h, i, j: (i, h))
    ks = pl.BlockSpec((T, HEAD_DIM), lambda h, i, j: (jnp.minimum(j, i), h))
    return pl.pallas_call(
        body, name='fox_fwd', grid=(H, n, n),
        in_specs=[qs, ks, ks, pl.BlockSpec((T, LANES), lambda h, i, j: (i, 0)),
                  pl.BlockSpec((Hp, T), lambda h, i, j: (0, jnp.minimum(j, i)))],
        out_specs=[qs, pl.BlockSpec((None, T, LANES), lambda h, i, j: (h, i, 0))],
        out_shape=[jax.ShapeDtypeStruct((S, FW), F32), jax.ShapeDtypeStruct((H, S, LANES), F32)],
        scratch_shapes=[pltpu.VMEM((T, 1), F32), pltpu.VMEM((T, 1), F32), pltpu.VMEM((T, HEAD_DIM), F32)],
        compiler_params=_params(('parallel', 'parallel', 'arbitrary')))(qn, kn, vb, c, ct)


def _fox_p_ds(q_ref, k_ref, v_ref, do_ref, c_ref, ct_ref, lse_ref, dl_ref, h, T, diagonal):
    s = _fox_logits(q_ref[...], k_ref[...], c_ref[...], ct_ref[...], h, T, diagonal)
    p = jnp.exp(s - jnp.tile(lse_ref[...], (1, T // LANES)))
    dp = lax.dot_general(do_ref[...], v_ref[...], _DIMS['nt'], preferred_element_type=F32)
    ds = p * (dp - jnp.tile(dl_ref[...], (1, T // LANES)))
    return p, dp, ds


def fox_bwd_q(qn, kn, vb, do, c, ct, lse, dl, T):
    S, FW = qn.shape
    H = FW // HEAD_DIM
    Hp = ct.shape[0]
    n = S // T

    def body(q_ref, k_ref, v_ref, do_ref, c_ref, ct_ref, lse_ref, dl_ref, dq_ref, dl2_ref, acc_s, rs_s):
        h, i, j = pl.program_id(0), pl.program_id(1), pl.program_id(2)

        @pl.when(j == 0)
        def _():
            acc_s[...] = jnp.zeros_like(acc_s)
            rs_s[...] = jnp.zeros_like(rs_s)

        def step(diagonal):
            p, dp, ds = _fox_p_ds(q_ref, k_ref, v_ref, do_ref, c_ref, ct_ref, lse_ref, dl_ref, h, T, diagonal)
            acc_s[...] += jnp.dot(ds.astype(BF16), k_ref[...], preferred_element_type=F32)
            rs_s[...] += jnp.sum(p * dp, axis=1, keepdims=True)

        _below_and_on_diagonal(i, j, step)

        @pl.when(j == i)
        def _():
            dq_ref[...] = acc_s[...] * (1.0 / math.sqrt(HEAD_DIM))
            dl2_ref[...] = jnp.broadcast_to(rs_s[...], (T, LANES))

    qs = pl.BlockSpec((T, HEAD_DIM), lambda h, i, j: (i, h))
    ks = pl.BlockSpec((T, HEAD_DIM), lambda h, i, j: (jnp.minimum(j, i), h))
    st = pl.BlockSpec((None, T, LANES), lambda h, i, j: (h, i, 0))
    return pl.pallas_call(
        body, name='fox_bwd_q', grid=(H, n, n),
        in_specs=[qs, ks, ks, qs, pl.BlockSpec((T, LANES), lambda h, i, j: (i, 0)),
                  pl.BlockSpec((Hp, T), lambda h, i, j: (0, jnp.minimum(j, i))), st, st],
        out_specs=[qs, st], out_shape=[jax.ShapeDtypeStruct((S, FW), F32), jax.ShapeDtypeStruct((H, S, LANES), F32)],
        scratch_shapes=[pltpu.VMEM((T, HEAD_DIM), F32), pltpu.VMEM((T, 1), F32)],
        compiler_params=_params(('parallel', 'parallel', 'arbitrary')))(qn, kn, vb, do, c, ct, lse, dl)


def fox_bwd_kv(qn, kn, vb, do, c, ct, lse, dl, T):
    S, FW = qn.shape
    H = FW // HEAD_DIM
    Hp = ct.shape[0]
    n = S // T

    def body(q_ref, k_ref, v_ref, do_ref, c_ref, ct_ref, lse_ref, dl_ref, dk_ref, dv_ref, dc_ref, dk_s, dv_s, dc_s):
        h, j, i = pl.program_id(0), pl.program_id(1), pl.program_id(2)

        @pl.when(i == 0)
        def _():
            dk_s[...] = jnp.zeros_like(dk_s)
            dv_s[...] = jnp.zeros_like(dv_s)
            dc_s[...] = jnp.zeros_like(dc_s)

        def step(diagonal):
            p, _, ds = _fox_p_ds(q_ref, k_ref, v_ref, do_ref, c_ref, ct_ref, lse_ref, dl_ref, h, T, diagonal)
            dv_s[...] += lax.dot_general(p.astype(BF16), do_ref[...], _DIMS['tn'], preferred_element_type=F32)
            dk_s[...] += lax.dot_general(ds.astype(BF16), q_ref[...], _DIMS['tn'], preferred_element_type=F32)
            dc_s[...] += jnp.sum(ds, axis=0, keepdims=True)

        _below_and_on_diagonal(i, j, step)

        @pl.when(i == n - 1)
        def _():
            dk_ref[...] = dk_s[...] * (1.0 / math.sqrt(HEAD_DIM))
            dv_ref[...] = dv_s[...].astype(BF16)
            dc_ref[...] = -dc_s[...]

    qs = pl.BlockSpec((T, HEAD_DIM), lambda h, j, i: (jnp.maximum(i, j), h))
    ks = pl.BlockSpec((T, HEAD_DIM), lambda h, j, i: (j, h))
    st = pl.BlockSpec((None, T, LANES), lambda h, j, i: (h, jnp.maximum(i, j), 0))
    return pl.pallas_call(
        body, name='fox_bwd_kv', grid=(H, n, n),
        in_specs=[qs, ks, ks, qs, pl.BlockSpec((T, LANES), lambda h, j, i: (jnp.maximum(i, j), 0)),
                  pl.BlockSpec((Hp, T), lambda h, j, i: (0, j)), st, st],
        out_specs=[ks, ks, pl.BlockSpec((None, 1, T), lambda h, j, i: (h, 0, j))],
        out_shape=[jax.ShapeDtypeStruct((S, FW), F32), jax.ShapeDtypeStruct((S, FW), BF16),
                   jax.ShapeDtypeStruct((H, 1, S), F32)],
        scratch_shapes=[pltpu.VMEM((T, HEAD_DIM), F32), pltpu.VMEM((T, HEAD_DIM), F32), pltpu.VMEM((1, T), F32)],
        compiler_params=_params(('parallel', 'parallel', 'arbitrary')))(qn, kn, vb, do, c, ct, lse, dl)


def _shift_down(v, d, rows, fill):
    return jnp.where(rows >= d, pltpu.roll(v, d, 0), fill)


def _shift_up(v, d, rows, S, fill):
    return jnp.where(rows < S - d, pltpu.roll(v, S - d, 0), fill)


SUBLANES = 8


def _scan_by_doubling(a, b, pos, span, reverse):
    n = a.shape[0]
    d = 1
    while d < span:
        if reverse:
            keep = pos < span - d
            a_s, b_s = jnp.where(keep, pltpu.roll(a, n - d, 0), 1.0), jnp.where(keep, pltpu.roll(b, n - d, 0), 0.0)
        else:
            keep = pos >= d
            a_s, b_s = jnp.where(keep, pltpu.roll(a, d, 0), 1.0), jnp.where(keep, pltpu.roll(b, d, 0), 0.0)
        b = a * b_s + b
        a = a * a_s
        d *= 2
    return a, b


def _scan(a, b, rows, S, reverse, scr):
    groups = S // SUBLANES
    a, b = _scan_by_doubling(a, b, jnp.bitwise_and(rows, SUBLANES - 1), SUBLANES, reverse)
    scr[0][...] = a
    scr[1][...] = b
    edge = 0 if reverse else SUBLANES - 1
    a_g = scr[0][pl.ds(edge, groups, stride=SUBLANES), :]
    b_g = scr[1][pl.ds(edge, groups, stride=SUBLANES), :]
    g_pos = lax.broadcasted_iota(jnp.int32, (groups, LANES), 0)
    _, h_g = _scan_by_doubling(a_g, b_g, g_pos, groups, reverse)
    if reverse:
        carry = jnp.where(g_pos < groups - 1, pltpu.roll(h_g, groups - 1, 0), 0.0)
    else:
        carry = jnp.where(g_pos >= 1, pltpu.roll(h_g, 1, 0), 0.0)
    for r in range(SUBLANES):
        scr[0][pl.ds(r, groups, stride=SUBLANES), :] = carry
    return b + a * scr[0][...]


def _lru_forward(u, cw, cb, wra, bra, wri, bri, lam, rows, scr):
    uc = cb + cw[CONV_W - 1] * u
    for d in range(1, CONV_W):
        uc = uc + cw[CONV_W - 1 - d] * _shift_down(u, d, rows, 0.0)
    ucb = uc.astype(BF16)
    r = _sigmoid(jnp.dot(ucb, wra.astype(BF16), preferred_element_type=F32) + bra)
    ig = _sigmoid(jnp.dot(ucb, wri.astype(BF16), preferred_element_type=F32) + bri)
    sp = _softplus(-lam)
    log_a = -LRU_C * r * sp
    a = jnp.exp(log_a)
    sq = jnp.sqrt(_neg_expm1(2.0 * log_a))
    iu = ig * uc
    hseq = _scan(a, sq * iu, rows, u.shape[0], False, scr)
    return uc, ucb, r, ig, sp, a, sq, iu, hseq


def _lru_specs(S, n_u, n_g):
    col = lambda off: pl.BlockSpec((S, LANES), lambda cbk: (0, off + cbk))
    vec = pl.BlockSpec((1, LANES), lambda cbk: (0, cbk))
    mat = pl.BlockSpec((None, LANES, LANES), lambda cbk: (cbk, 0, 0))
    cw = pl.BlockSpec((CONV_W, LANES), lambda cbk: (0, cbk))
    return col, vec, mat, cw


def lru_fwd(proj, conv_w, conv_b, w_ra, b_ra, w_ri, b_ri, lam, u_off, g_off):
    S = proj.shape[0]
    nb = w_ra.shape[0]
    col, vec, mat, cws = _lru_specs(S, u_off, g_off)

    def body(u_ref, g_ref, cw_ref, cb_ref, wra_ref, bra_ref, wri_ref, bri_ref, lam_ref, y_ref, scr0, scr1):
        rows = lax.broadcasted_iota(jnp.int32, (S, LANES), 0)
        cw = [cw_ref[t:t + 1, :] for t in range(CONV_W)]
        hseq = _lru_forward(u_ref[...], cw, cb_ref[...], wra_ref[...], bra_ref[...], wri_ref[...],
                            bri_ref[...], lam_ref[...], rows, (scr0, scr1))[-1]
        y_ref[...] = hseq * _gelu_and_grad(g_ref[...])[0]

    return pl.pallas_call(
        body, name='lru_fwd', grid=(nb,),
        in_specs=[col(u_off), col(g_off), cws, vec, mat, vec, mat, vec, vec], out_specs=col(0),
        out_shape=jax.ShapeDtypeStruct((S, nb * LANES), F32),
        scratch_shapes=[pltpu.VMEM((S, LANES), F32), pltpu.VMEM((S, LANES), F32)],
        compiler_params=_params(('parallel',)))(proj, proj, conv_w, conv_b, w_ra, b_ra, w_ri, b_ri, lam)


def lru_bwd(proj, dy, conv_w, conv_b, w_ra, b_ra, w_ri, b_ri, lam, u_off, g_off):
    S = proj.shape[0]
    nb = w_ra.shape[0]
    LW = nb * LANES
    col, vec, mat, cws = _lru_specs(S, u_off, g_off)

    def body(u_ref, g_ref, dy_ref, cw_ref, cb_ref, wra_ref, bra_ref, wri_ref, bri_ref, lam_ref,
             du_ref, dg_ref, dcw_ref, dcb_ref, dwra_ref, dbra_ref, dwri_ref, dbri_ref, dlam_ref, scr0, scr1):
        rows = lax.broadcasted_iota(jnp.int32, (S, LANES), 0)
        u, lam_v = u_ref[...], lam_ref[...]
        cw = [cw_ref[t:t + 1, :] for t in range(CONV_W)]
        wra, wri = wra_ref[...].astype(BF16), wri_ref[...].astype(BF16)
        uc, ucb, r, ig, sp, a, sq, iu, hseq = _lru_forward(u, cw, cb_ref[...], wra, bra_ref[...], wri, bri_ref[...],
                                                           lam_v, rows, (scr0, scr1))
        gl, dgl = _gelu_and_grad(g_ref[...])
        dy_v = dy_ref[...]
        dg_ref[...] = (dy_v * hseq * dgl).astype(BF16)
        G = _scan(_shift_up(a, 1, rows, S, 0.0), dy_v * gl, rows, S, True, (scr0, scr1))
        da = G * _shift_down(hseq, 1, rows, 0.0)
        diu = G * sq
        dsq = G * iu
        dlog_a = da * a - dsq * a * a / jnp.maximum(sq, 1e-30)
        dr = dlog_a * (-LRU_C * sp)
        dsp = jnp.sum(dlog_a * (-LRU_C * r), axis=0, keepdims=True)
        dlam_ref[...] = -dsp * _sigmoid(-lam_v)
        dzr = dr * r * (1.0 - r)
        dzi = diu * uc * ig * (1.0 - ig)
        dzrb, dzib = dzr.astype(BF16), dzi.astype(BF16)
        duc = (diu * ig + lax.dot_general(dzrb, wra, _DIMS['nt'], preferred_element_type=F32)
               + lax.dot_general(dzib, wri, _DIMS['nt'], preferred_element_type=F32))
        dwra_ref[...] = lax.dot_general(ucb, dzrb, _DIMS['tn'], preferred_element_type=F32)
        dwri_ref[...] = lax.dot_general(ucb, dzib, _DIMS['tn'], preferred_element_type=F32)
        dbra_ref[...] = jnp.sum(dzr, axis=0, keepdims=True)
        dbri_ref[...] = jnp.sum(dzi, axis=0, keepdims=True)
        dcb_ref[...] = jnp.sum(duc, axis=0, keepdims=True)
        du = cw[CONV_W - 1] * duc
        dcw_ref[CONV_W - 1:CONV_W, :] = jnp.sum(duc * u, axis=0, keepdims=True)
        for d in range(1, CONV_W):
            du = du + cw[CONV_W - 1 - d] * _shift_up(duc, d, rows, S, 0.0)
            dcw_ref[CONV_W - 1 - d:CONV_W - d, :] = jnp.sum(duc * _shift_down(u, d, rows, 0.0), axis=0, keepdims=True)
        du_ref[...] = du.astype(BF16)

    sd = jax.ShapeDtypeStruct
    return pl.pallas_call(
        body, name='lru_bwd', grid=(nb,),
        in_specs=[col(u_off), col(g_off), col(0), cws, vec, mat, vec, mat, vec, vec],
        out_specs=[col(0), col(0), cws, vec, mat, vec, mat, vec, vec],
        out_shape=[sd((S, LW), BF16), sd((S, LW), BF16), sd((CONV_W, LW), F32), sd((1, LW), F32),
                   sd((nb, LANES, LANES), F32), sd((1, LW), F32), sd((nb, LANES, LANES), F32), sd((1, LW), F32),
                   sd((1, LW), F32)],
        scratch_shapes=[pltpu.VMEM((S, LANES), F32), pltpu.VMEM((S, LANES), F32)],
        compiler_params=_params(('parallel',)))(proj, proj, dy, conv_w, conv_b, w_ra, b_ra, w_ri, b_ri, lam)


def mix_fwd(o_fox, y_lru, g_fox, g_lru):
    S, FW = o_fox.shape
    tr = _tile(S, (256, 128))

    def body(o_ref, y_ref, gf_ref, gl_ref, m_ref):
        m_ref[...] = jnp.concatenate([_rms(o_ref[...], gf_ref[...]), _rms(y_ref[...], gl_ref[...])],
                                     axis=1).astype(BF16)

    return _rows_call('mix_fwd', body, S, tr,
                      [(o_fox, _rb(tr, FW)), (y_lru, _rb(tr, FW)), (g_fox, _fb((1, FW))), (g_lru, _fb((1, FW)))],
                      [((S, 2 * FW), BF16, _rb(tr, 2 * FW))])[0]


def mix_bwd(o_fox, y_lru, g_fox, g_lru, dmix):
    S, FW = o_fox.shape
    H = FW // HEAD_DIM
    tr = _tile(S, (256, 128))

    def body(o_ref, y_ref, gf_ref, gl_ref, df_ref, dl_ref, do_ref, dlt_ref, dy_ref, dgf_ref, dgl_ref):
        o = o_ref[...]
        do, dgf = _rms_bwd(o, gf_ref[...], df_ref[...])
        dyl, dgl = _rms_bwd(y_ref[...], gl_ref[...], dl_ref[...])
        do_ref[...] = do.astype(BF16)
        dy_ref[...] = dyl
        prod = do * o
        for h in range(H):
            dlt_ref[h] = jnp.broadcast_to(
                jnp.sum(prod[:, h * HEAD_DIM:(h + 1) * HEAD_DIM], axis=1, keepdims=True), (tr, LANES))
        first = pl.program_id(0) == 0
        _acc_out(dgf_ref, first, dgf)
        _acc_out(dgl_ref, first, dgl)

    g = _fb((1, FW))
    return _rows_call('mix_bwd', body, S, tr,
                      [(o_fox, _rb(tr, FW)), (y_lru, _rb(tr, FW)), (g_fox, g), (g_lru, g), (dmix, _rb(tr, FW, 0)),
                       (dmix, _rb(tr, FW, 1))],
                      [((S, FW), BF16, _rb(tr, FW)), ((H, S, LANES), F32, pl.BlockSpec((H, tr, LANES), lambda i: (0, i, 0))),
                       ((S, FW), F32, _rb(tr, FW)), ((1, FW), F32, g), ((1, FW), F32, g)])


def _xattn_heads(cq_raw, ckv, g_cq, g_ck, XW):
    out = []
    for h in range(XW // HEAD_DIM):
        sl = slice(h * HEAD_DIM, (h + 1) * HEAD_DIM)
        out.append((cq_raw[:, sl], _rms(cq_raw[:, sl], g_cq), ckv[:, sl], _rms(ckv[:, sl], g_ck),
                    ckv[:, XW + h * HEAD_DIM:XW + (h + 1) * HEAD_DIM].astype(BF16)))
    return out


def xattn_fwd(cq_raw, ckv, g_cq, g_ck):
    S, XW = cq_raw.shape
    M = ckv.shape[0]
    tr = _tile(S, (512, 256, 128))

    def body(q_ref, kv_ref, gq_ref, gk_ref, o_ref):
        outs = []
        for _, qn, _, kn, v in _xattn_heads(q_ref[...], kv_ref[...], gq_ref[...], gk_ref[...], XW):
            s = lax.dot_general(qn.astype(BF16), kn.astype(BF16), _DIMS['nt'], preferred_element_type=F32)
            s = s / math.sqrt(HEAD_DIM)
            p = jnp.exp(s - jnp.max(s, axis=1, keepdims=True))
            p = p / jnp.sum(p, axis=1, keepdims=True)
            outs.append(jnp.dot(p.astype(BF16), v, preferred_element_type=F32))
        o_ref[...] = jnp.concatenate(outs, axis=1).astype(BF16)

    g = _fb((1, HEAD_DIM))
    return _rows_call('xattn_fwd', body, S, tr,
                      [(cq_raw, _rb(tr, XW)), (ckv, _fb((M, 2 * XW))), (g_cq, g), (g_ck, g)],
                      [((S, XW), BF16, _rb(tr, XW))])[0]


def xattn_bwd(cq_raw, ckv, g_cq, g_ck, do):
    S, XW = cq_raw.shape
    M = ckv.shape[0]
    tr = _tile(S, (512, 256, 128))
    n = S // tr

    def body(q_ref, kv_ref, gq_ref, gk_ref, do_ref, dq_ref, dkv_ref, dgq_ref, dgk_ref):
        i = pl.program_id(0)
        do_v = do_ref[...]
        dqs, dkn, dvs = [], [], []
        dgq = jnp.zeros((1, HEAD_DIM), F32)
        for h, (q_raw, qn, _, kn, v) in enumerate(_xattn_heads(q_ref[...], kv_ref[...], gq_ref[...], gk_ref[...], XW)):
            qb, kb = qn.astype(BF16), kn.astype(BF16)
            doh = do_v[:, h * HEAD_DIM:(h + 1) * HEAD_DIM]
            s = lax.dot_general(qb, kb, _DIMS['nt'], preferred_element_type=F32) / math.sqrt(HEAD_DIM)
            p = jnp.exp(s - jnp.max(s, axis=1, keepdims=True))
            p = p / jnp.sum(p, axis=1, keepdims=True)
            dp = lax.dot_general(doh, v, _DIMS['nt'], preferred_element_type=F32)
            ds = (p * (dp - jnp.sum(p * dp, axis=1, keepdims=True)) / math.sqrt(HEAD_DIM)).astype(BF16)
            dvs.append(lax.dot_general(p.astype(BF16), doh, _DIMS['tn'], preferred_element_type=F32))
            dkn.append(lax.dot_general(ds, qb, _DIMS['tn'], preferred_element_type=F32))
            dq, g1 = _rms_bwd(q_raw, gq_ref[...], jnp.dot(ds, kb, preferred_element_type=F32))
            dqs.append(dq)
            dgq = dgq + g1
        dq_ref[...] = jnp.concatenate(dqs, axis=1).astype(BF16)
        first = i == 0
        _acc_out(dgq_ref, first, dgq)
        _acc_out(dkv_ref, first, jnp.concatenate(dkn + dvs, axis=1))

        @pl.when(i == n - 1)
        def _():
            kv = kv_ref[...]
            acc = dkv_ref[...]
            dk, gk = _heads(lambda t, d: _rms_bwd(t, gk_ref[...], d), XW // HEAD_DIM, kv[:, :XW], acc[:, :XW])
            dkv_ref[:, :XW] = dk
            dgk_ref[...] = gk

    g = _fb((1, HEAD_DIM))
    return _rows_call('xattn_bwd', body, S, tr,
                      [(cq_raw, _rb(tr, XW)), (ckv, _fb((M, 2 * XW))), (g_cq, g), (g_ck, g), (do, _rb(tr, XW))],
                      [((S, XW), BF16, _rb(tr, XW)), ((M, 2 * XW), F32, _fb((M, 2 * XW))), ((1, HEAD_DIM), F32, g),
                       ((1, HEAD_DIM), F32, g)])


def swiglu_fwd(gu, F):
    S = gu.shape[0]
    tr = _tile(S, (256, 128))
    tf = _tile(F, (1408, 1024, 512, 256, 128))
    nf = F // tf

    def body(g_ref, u_ref, a_ref):
        g = g_ref[...]
        a_ref[...] = (g * _sigmoid(g) * u_ref[...]).astype(BF16)

    return pl.pallas_call(
        body, name='swiglu_fwd', grid=(S // tr, nf),
        in_specs=[pl.BlockSpec((tr, tf), lambda i, n: (i, n)), pl.BlockSpec((tr, tf), lambda i, n: (i, n + nf))],
        out_specs=pl.BlockSpec((tr, tf), lambda i, n: (i, n)), out_shape=jax.ShapeDtypeStruct((S, F), BF16),
        compiler_params=_params(('parallel', 'parallel')))(gu, gu)


def swiglu_bwd(gu, dact, F, after):
    S = gu.shape[0]
    tr = _tile(S, (256, 128))
    tf = _tile(F, (1408, 1024, 512, 256, 128))
    nf = F // tf

    def body(g_ref, u_ref, da_ref, after_ref, o_ref):
        g, da = g_ref[...], da_ref[...]
        sg = _sigmoid(g)
        o_ref[0] = (da * u_ref[...] * sg * (1.0 + g * (1.0 - sg))).astype(BF16)
        o_ref[1] = (da * g * sg).astype(BF16)

    return pl.pallas_call(
        body, name='swiglu_bwd', grid=(S // tr, nf),
        in_specs=[pl.BlockSpec((tr, tf), lambda i, n: (i, n)), pl.BlockSpec((tr, tf), lambda i, n: (i, n + nf)),
                  pl.BlockSpec((tr, tf), lambda i, n: (i, n)), ANY],
        out_specs=pl.BlockSpec((2, tr, tf), lambda i, n: (0, i, n)), out_shape=jax.ShapeDtypeStruct((2, S, F), BF16),
        compiler_params=_params(('parallel', 'parallel')))(gu, gu, dact, after)


def loss_head(y, target):
    S, D = y.shape
    tr = _tile(S, (256, 128))

    def body(y_ref, t_ref, d_ref, db_ref, l_ref):
        err = y_ref[...] - t_ref[...]
        d = err * (1.0 / D)
        d_ref[...] = d
        db_ref[...] = d.astype(BF16)
        part = jnp.sum(jnp.sum(err * err, axis=1, keepdims=True), axis=0, keepdims=True) * (0.5 / D)
        _acc_out(l_ref, pl.program_id(0) == 0, jnp.broadcast_to(part, (1, LANES)))

    return _rows_call('loss_head', body, S, tr, [(y, _rb(tr, D)), (target, _rb(tr, D))],
                      [((S, D), F32, _rb(tr, D)), ((S, D), BF16, _rb(tr, D)), ((1, LANES), F32, _fb((1, LANES)))])


def _adamw_math(w, gv, m, v):
    mn = ADAM_B1 * m + (1.0 - ADAM_B1) * gv
    vn = ADAM_B2 * v + (1.0 - ADAM_B2) * (gv * gv)
    m_hat = mn / (1.0 - ADAM_B1 ** ADAM_STEP)
    v_hat = vn / (1.0 - ADAM_B2 ** ADAM_STEP)
    return -ADAM_LR * (m_hat / (jnp.sqrt(v_hat) + ADAM_EPS) + ADAM_WD * w), mn, vn


def adamw(name, w, g, m, v):
    R, C = w.shape
    tr = _row_tile(R, C)

    def body(w_ref, g_ref, m_ref, v_ref, d_ref, mo_ref, vo_ref):
        d_ref[...], mo_ref[...], vo_ref[...] = _adamw_math(w_ref[...], g_ref[...], m_ref[...], v_ref[...])

    spec = _rb(tr, C)
    return _rows_call(name, body, R, tr, [(w, spec), (g, spec), (m, spec), (v, spec)], [((R, C), F32, spec)] * 3)


def adamw_halves(name, w, mine, other, m, v, c_idx):
    R, C = w.shape
    hr = R // 2
    tr = _row_tile(hr, C)

    def body(c_ref, w_ref, a_ref, b_ref, m_ref, v_ref, g_ref, d_ref, mo_ref, vo_ref):
        gv = jnp.where(pl.program_id(0) == c_ref[0], a_ref[...], b_ref[...])
        g_ref[...] = gv
        d_ref[...], mo_ref[...], vo_ref[...] = _adamw_math(w_ref[...], gv, m_ref[...], v_ref[...])

    full = pl.BlockSpec((None, tr, C), lambda hh, i, c_ref: (hh, i, 0))
    mine_spec = pl.BlockSpec((tr, C), lambda hh, i, c_ref: (jnp.where(hh == c_ref[0], i, 0), 0))
    other_spec = pl.BlockSpec((tr, C), lambda hh, i, c_ref: (jnp.where(hh == c_ref[0], 0, i), 0))
    outs = pl.pallas_call(
        body, name=name,
        grid_spec=pltpu.PrefetchScalarGridSpec(num_scalar_prefetch=1, grid=(2, hr // tr),
                                               in_specs=[full, mine_spec, other_spec, full, full], out_specs=[full] * 4),
        out_shape=[jax.ShapeDtypeStruct((2, hr, C), F32)] * 4,
        compiler_params=_params(('parallel', 'parallel')))(
            c_idx, w.reshape(2, hr, C), mine, other, m.reshape(2, hr, C), v.reshape(2, hr, C))
    return [o.reshape(R, C) for o in outs]


def _place():
    x, y, c = lax.axis_index('x'), lax.axis_index('y'), lax.axis_index('c')
    return x, y, c, [(1 - x, y), (x, 1 - y), (1 - x, 1 - y)]


def _rcopy(src, dst, ssem, rsem, dev):
    return pltpu.make_async_remote_copy(src_ref=src, dst_ref=dst, send_sem=ssem, recv_sem=rsem, device_id=dev,
                                        device_id_type=MESH)


HBM = pl.BlockSpec(memory_space=pltpu.HBM)
SEM = pl.BlockSpec(memory_space=pltpu.SEMAPHORE)
EFFECT = pltpu.SideEffectType.DATAFLOW_SIDE_EFFECTING


def _in_hbm(a):
    return pltpu.with_memory_space_constraint(a, pltpu.HBM)


def _rows_part(shape, whole, half):
    return pl.ds(0, shape[0]) if whole else pl.ds(half * (shape[0] // 2), shape[0] // 2)


def gather_start(name, shards, whole):
    nT = len(shards)

    def body(*refs):
        srcs, lands = refs[:nT], refs[nT:2 * nT]
        ssem, rsem, token = refs[2 * nT], refs[2 * nT + 1], refs[-1]
        x, y, c, chips = _place()
        for t in range(nT):
            rows = _rows_part(shards[t].shape, whole[t], c)
            for k, (px, py) in enumerate(chips):
                _rcopy(srcs[t].at[rows], lands[t].at[2 * x + y, rows], ssem.at[3 * t + k], rsem.at[3 * t + k],
                       (px, py, c)).start()
        token[...] = jnp.zeros_like(token)

    zones = [lax.empty((N_CHIPS,) + s.shape, s.dtype) for s in shards]
    outs = pl.pallas_call(
        body, name=name,
        out_shape=(pltpu.SemaphoreType.DMA((3 * nT,)), pltpu.SemaphoreType.DMA((3 * nT,)),
                   *[pltpu.HBM(s.shape, s.dtype) for s in shards], *[pltpu.HBM(z.shape, z.dtype) for z in zones],
                   jax.ShapeDtypeStruct((8, LANES), F32)),
        in_specs=[HBM] * (2 * nT), out_specs=(SEM, SEM, *[HBM] * (2 * nT), pl.BlockSpec(memory_space=pltpu.VMEM)),
        input_output_aliases={i: 2 + i for i in range(2 * nT)},
        compiler_params=pltpu.CompilerParams(has_side_effects=EFFECT))(*[_in_hbm(a) for a in list(shards) + zones])
    return outs[0], outs[1], outs[2:2 + nT], outs[2 + nT:2 + 2 * nT], outs[-1]


def gather_wait(name, t, shard, zone, ssem, rsem, after, whole):
    after = after if isinstance(after, (list, tuple)) else [after]

    def body(src_ref, land_ref, ssem_ref, rsem_ref, *rest):
        x, y, c, chips = _place()
        rows = _rows_part(shard.shape, whole, c)
        for k, (px, py) in enumerate(chips):
            cp = _rcopy(src_ref.at[rows], land_ref.at[2 * px + py, rows], ssem_ref.at[3 * t + k], rsem_ref.at[3 * t + k],
                        (px, py, c))
            cp.wait_send()
            cp.wait_recv()

    return pl.pallas_call(
        body, name=name, out_shape=(pltpu.HBM(shard.shape, shard.dtype), pltpu.HBM(zone.shape, zone.dtype)),
        in_specs=(HBM, HBM, SEM, SEM, *[ANY] * len(after)), out_specs=(HBM, HBM), input_output_aliases={0: 0, 1: 1},
        compiler_params=pltpu.CompilerParams(has_side_effects=EFFECT))(shard, zone, ssem, rsem, *after)


def pair_swap(name, zone):
    hr = zone.shape[1] // 2

    def body(z_in, z_ref, ssem, rsem):
        x, y, c, chips = _place()
        cps = []
        for k, (px, py) in enumerate(chips):
            blk = z_ref.at[2 * px + py, pl.ds(c * hr, hr)]
            cps.append(_rcopy(blk, blk, ssem.at[k], rsem.at[k], (x, y, 1 - c)))
            cps[-1].start()
        for k, (px, py) in enumerate(chips):
            blk = z_ref.at[2 * px + py, pl.ds((1 - c) * hr, hr)]
            _rcopy(blk, blk, ssem.at[k], rsem.at[k], (x, y, 1 - c)).wait_recv()
        for cp in cps:
            cp.wait_send()

    return pl.pallas_call(
        body, name=name, in_specs=[ANY], out_specs=ANY, out_shape=jax.ShapeDtypeStruct(zone.shape, zone.dtype),
        input_output_aliases={0: 0},
        scratch_shapes=[pltpu.SemaphoreType.DMA((3,)), pltpu.SemaphoreType.DMA((3,))],
        compiler_params=_params())(zone)


N_SENDERS = 7


def _scatter_copies(g_ref, l_ref, ssem, rsem):
    x, y, c, chips = _place()
    cps = []
    for k, (px, py) in enumerate(chips):
        for d in range(2):
            to = (c + d) % 2
            cps.append(_rcopy(g_ref.at[2 * px + py, to], l_ref.at[2 * k + d], ssem.at[2 * k + d], rsem.at[2 * k + d],
                              (px, py, to)))
    cps.append(_rcopy(g_ref.at[2 * x + y, 1 - c], l_ref.at[6], ssem.at[6], rsem.at[6], (x, y, 1 - c)))
    return cps


def scatter_start(name, g):
    def body(g_ref, l_ref, ssem, rsem, g_out, l_out, token):
        for cp in _scatter_copies(g_ref, l_ref, ssem, rsem):
            cp.start()
        token[...] = jnp.zeros_like(token)

    zone = lax.empty((N_SENDERS,) + g.shape[2:], g.dtype)
    return pl.pallas_call(
        body, name=name,
        out_shape=(pltpu.SemaphoreType.DMA((N_SENDERS,)), pltpu.SemaphoreType.DMA((N_SENDERS,)),
                   pltpu.HBM(g.shape, g.dtype), pltpu.HBM(zone.shape, zone.dtype), jax.ShapeDtypeStruct((8, LANES), F32)),
        in_specs=[HBM, HBM], out_specs=(SEM, SEM, HBM, HBM, pl.BlockSpec(memory_space=pltpu.VMEM)),
        input_output_aliases={0: 2, 1: 3},
        compiler_params=pltpu.CompilerParams(has_side_effects=EFFECT))(_in_hbm(g), _in_hbm(zone))


def scatter_wait(name, g, zone, ssem, rsem, after):
    def body(g_ref, l_ref, ssem_ref, rsem_ref, after_ref, g_out, l_out):
        for cp in _scatter_copies(g_ref, l_ref, ssem_ref, rsem_ref):
            cp.wait_send()
            cp.wait_recv()

    return pl.pallas_call(
        body, name=name, out_shape=(pltpu.HBM(g.shape, g.dtype), pltpu.HBM(zone.shape, zone.dtype)),
        in_specs=(HBM, HBM, SEM, SEM, ANY), out_specs=(HBM, HBM), input_output_aliases={0: 0, 1: 1},
        compiler_params=pltpu.CompilerParams(has_side_effects=EFFECT))(g, zone, ssem, rsem, after)


def sum_parts(name, g, landed, chip_idx, c_idx):
    hr, C = g.shape[2:]
    tr = _row_tile(hr, C, min_rows=16)

    def body(me_ref, c_ref, g_ref, l_ref, o_ref):
        acc = g_ref[...].astype(F32)
        for s in range(N_SENDERS):
            acc = acc + l_ref[s].astype(F32)
        o_ref[...] = acc

    return pl.pallas_call(
        body, name=name,
        grid_spec=pltpu.PrefetchScalarGridSpec(
            num_scalar_prefetch=2, grid=(hr // tr,),
            in_specs=[pl.BlockSpec((None, None, tr, C), lambda i, me_ref, c_ref: (me_ref[0], c_ref[0], i, 0)),
                      pl.BlockSpec((N_SENDERS, tr, C), lambda i, me_ref, c_ref: (0, i, 0))],
            out_specs=pl.BlockSpec((tr, C), lambda i, me_ref, c_ref: (i, 0))),
        out_shape=jax.ShapeDtypeStruct((hr, C), F32),
        compiler_params=_params(('parallel',)))(chip_idx, c_idx, g, landed)


def pair_join(name, halves):
    nT = len(halves)

    def body(*refs):
        ins, outs = refs[:nT], refs[nT:2 * nT]
        ssem, rsem = refs[2 * nT:]
        x, y, c, _ = _place()
        cps = [_rcopy(ins[t], outs[t], ssem.at[t], rsem.at[t], (x, y, 1 - c)) for t in range(nT)]
        for cp in cps:
            cp.start()
        for cp in cps:
            cp.wait()

    return pl.pallas_call(
        body, name=name, in_specs=[ANY] * nT, out_specs=[ANY] * nT,
        out_shape=[jax.ShapeDtypeStruct(h.shape, h.dtype) for h in halves],
        scratch_shapes=[pltpu.SemaphoreType.DMA((nT,)), pltpu.SemaphoreType.DMA((nT,))],
        compiler_params=_params())(*halves)


N_DEVICES = 8


def _spread_copies(b_ref, l_ref, ssem, rsem):
    x, y, c, chips = _place()
    me = 4 * x + 2 * y + c
    pairs = []
    for px, py, pc in [(px, py, pc) for px, py in chips for pc in (c, 1 - c)] + [(x, y, 1 - c)]:
        it = 4 * px + 2 * py + pc
        pairs.append((_rcopy(b_ref, l_ref.at[me], ssem.at[it], rsem.at[me], (px, py, pc)),
                      _rcopy(b_ref, l_ref.at[it], ssem.at[it], rsem.at[it], (px, py, pc))))
    return pairs


def spread_start(name, buf):
    def body(b_ref, l_ref, ssem, rsem, b_out, l_out, token):
        for mine, _ in _spread_copies(b_ref, l_ref, ssem, rsem):
            mine.start()
        token[...] = jnp.zeros_like(token)

    zone = lax.empty((N_DEVICES,) + buf.shape, buf.dtype)
    return pl.pallas_call(
        body, name=name,
        out_shape=(pltpu.SemaphoreType.DMA((N_DEVICES,)), pltpu.SemaphoreType.DMA((N_DEVICES,)),
                   pltpu.HBM(buf.shape, buf.dtype), pltpu.HBM(zone.shape, zone.dtype), jax.ShapeDtypeStruct((8, LANES), F32)),
        in_specs=[HBM, HBM], out_specs=(SEM, SEM, HBM, HBM, pl.BlockSpec(memory_space=pltpu.VMEM)),
        input_output_aliases={0: 2, 1: 3},
        compiler_params=pltpu.CompilerParams(has_side_effects=EFFECT))(_in_hbm(buf), _in_hbm(zone))


def spread_wait(name, buf, zone, ssem, rsem, after):
    def body(b_ref, l_ref, ssem_ref, rsem_ref, after_ref, b_out, l_out):
        for mine, theirs in _spread_copies(b_ref, l_ref, ssem_ref, rsem_ref):
            mine.wait_send()
            theirs.wait_recv()

    return pl.pallas_call(
        body, name=name, out_shape=(pltpu.HBM(buf.shape, buf.dtype), pltpu.HBM(zone.shape, zone.dtype)),
        in_specs=(HBM, HBM, SEM, SEM, ANY), out_specs=(HBM, HBM), input_output_aliases={0: 0, 1: 1},
        compiler_params=pltpu.CompilerParams(has_side_effects=EFFECT))(buf, zone, ssem, rsem, after)


def sum_devices(name, zone):
    _, R, C = zone.shape
    tr = _row_tile(R, C)

    def body(z_ref, o_ref):
        acc = z_ref[0]
        for d in range(1, N_DEVICES):
            acc = acc + z_ref[d]
        o_ref[...] = acc

    return pl.pallas_call(
        body, name=name, grid=(R // tr,), in_specs=[pl.BlockSpec((N_DEVICES, tr, C), lambda i: (0, i, 0))],
        out_specs=pl.BlockSpec((tr, C), lambda i: (i, 0)), out_shape=jax.ShapeDtypeStruct((R, C), F32),
        compiler_params=_params(('parallel',)))(zone)


class _InWindows:
    def __init__(self, FW, LW, H, C):
        gap = LANES - H
        padded = lambda o: o if o < 3 * FW + H else o + gap
        self.width = 3 * FW + LANES + 2 * LW
        self.first = [padded(C * j) // LANES for j in range(N_CHIPS)]
        self.blocks = max(padded(C * (j + 1) - 1) // LANES - self.first[j] + 1 for j in range(N_CHIPS))
        assert all((b + self.blocks) * LANES <= self.width for b in self.first)
        self.cols = self.blocks * LANES
        self.runs = []
        for j in range(N_CHIPS):
            cut = min(max(3 * FW + H - C * j, 0), C)
            spans = [(0, cut), (cut, C)]
            self.runs.append([(t0, t1, padded(C * j + t0) - LANES * self.first[j]) for t0, t1 in spans if t1 > t0])

    def to_window(self, shard, chip):
        def place(j, s):
            parts, pos = [], 0
            for t0, t1, w0 in self.runs[j]:
                parts += [jnp.zeros((s.shape[0], w0 - pos), s.dtype), s[:, t0:t1]]
                pos = w0 + t1 - t0
            parts.append(jnp.zeros((s.shape[0], self.cols - pos), s.dtype))
            return jnp.concatenate([p for p in parts if p.shape[1]], axis=1)
        return lax.switch(chip, [functools.partial(place, j) for j in range(N_CHIPS)], shard)

    def from_window(self, win, chip):
        def take(j, w):
            return jnp.concatenate([w[:, w0:w0 + t1 - t0] for t0, t1, w0 in self.runs[j]], axis=1)
        return lax.switch(chip, [functools.partial(take, j) for j in range(N_CHIPS)], win)

    def assemble(self, zone):
        total = None
        for j in range(N_CHIPS):
            lead = self.first[j] * LANES
            part = jnp.pad(zone[j], ((0, 0), (lead, self.width - lead - self.cols)))
            total = part if total is None else total + part
        return total

    def windows(self, padded_matrix):
        return jnp.stack([padded_matrix[:, b * LANES:b * LANES + self.cols] for b in self.first])


_PACK = 8 * LANES


PACK_ROWS = 256


def _pack(arrs):
    flat = []
    for a in arrs:
        v = a.reshape(-1).astype(F32)
        flat.append(jnp.pad(v, (0, (-v.shape[0]) % _PACK)))
    rows = sum(v.shape[0] for v in flat) // LANES
    flat.append(jnp.zeros(((-rows) % PACK_ROWS) * LANES, F32))
    return jnp.concatenate(flat).reshape(-1, LANES)


def _unpack(buf, shapes):
    out, off = [], 0
    flat = buf.reshape(-1)
    for sh in shapes:
        n = math.prod(sh)
        out.append(flat[off:off + n].reshape(sh))
        off += n + (-n) % _PACK
    return out


def kernel(x, mem, g_mix, w_in, b_f, g_q, g_k, conv_w, conv_b, w_ra, b_ra, w_ri, b_ri, lam, g_fox_out, g_lru_out, w_out, g_xattn, g_mem, w_cq, w_ckv, g_cq, g_ck, w_co, g_ffn, w_gate_up, w_down, loss_target, m_g_mix, m_w_in, m_b_f, m_g_q, m_g_k, m_conv_w, m_conv_b, m_w_ra, m_b_ra, m_w_ri, m_b_ri, m_lam, m_g_fox_out, m_g_lru_out, m_w_out, m_g_xattn, m_g_mem, m_w_cq, m_w_ckv, m_g_cq, m_g_ck, m_w_co, m_g_ffn, m_w_gate_up, m_w_down, v_g_mix, v_w_in, v_b_f, v_g_q, v_g_k, v_conv_w, v_conv_b, v_w_ra, v_b_ra, v_w_ri, v_b_ri, v_lam, v_g_fox_out, v_g_lru_out, v_w_out, v_g_xattn, v_g_mem, v_w_cq, v_w_ckv, v_g_cq, v_g_ck, v_w_co, v_g_ffn, v_w_gate_up, v_w_down):
    given = dict(locals())
    W = {n: given[n][0] for n in WEIGHTS}
    M1 = {n: given['m_' + n][0] for n in WEIGHTS}
    V1 = {n: given['v_' + n][0] for n in WEIGHTS}
    xs, ms, tgt = x[0], mem[0], loss_target[0]
    S, D = xs.shape
    H = W['b_f'].shape[0]
    FW = H * HEAD_DIM
    LW = W['lam'].shape[0]
    nb = W['w_ra'].shape[0]
    XW = W['w_cq'].shape[1]
    F = W['w_down'].shape[0] * N_CHIPS
    IN_W = W['w_in'].shape[1] * N_CHIPS
    assert FW == LW and LW == nb * LANES and IN_W == 3 * FW + H + 2 * LW and H <= 8
    T = _tile(S, (512, 256, 128))
    c_idx = lax.axis_index('c').astype(jnp.int32).reshape(1)
    chip = 2 * lax.axis_index('x') + lax.axis_index('y')
    chip_idx = chip.astype(jnp.int32).reshape(1)
    vec = lambda n: W[n].reshape(1, -1)

    wins = _InWindows(FW, LW, H, W['w_in'].shape[1])
    started = {}
    g_tok = jnp.zeros((1, 1), F32)
    for call, names in (('gather_start_first', ['conv_w', 'w_in']), ('gather_start_rest', BIG[1:])):
        own = [W[n].reshape(-1, LANES) if n == 'conv_w' else W[n].astype(BF16) + g_tok.astype(BF16) for n in names]
        own = [wins.to_window(o, chip) if n == 'w_in' else o for n, o in zip(names, own)]
        ssem, rsem, srcs, zones, tok = gather_start(call, own, [n == 'conv_w' for n in names])
        g_tok = tok[0:1, 0:1]
        started.update({n: (t, srcs[t], zones[t], ssem, rsem) for t, n in enumerate(names)})

    def fetch(n, after):
        t, g_src, g_zone, g_ssem, g_rsem = started[n]
        src, zone = gather_wait('gather_wait_' + n, t, g_src, g_zone, g_ssem, g_rsem, after, n == 'conv_w')
        if n != 'conv_w':
            zone = pair_swap('pair_swap_' + n, zone)
        return lax.dynamic_update_index_in_dim(zone, src, chip, 0)

    b_f_pad = jnp.pad(vec('b_f'), ((0, 0), (0, LANES - H)))
    u_off, g_off = 3 * FW // LANES, (3 * FW + LW) // LANES

    h1 = norm_fwd('norm_mix', xs, vec('g_mix') + g_tok[0:1, 0:1])
    conv_full = fetch('conv_w', h1).reshape(N_CHIPS, CONV_W, LW // N_CHIPS).transpose(1, 0, 2).reshape(CONV_W, LW)
    w_in_pad = wins.assemble(fetch('w_in', [h1, M1['w_in'], V1['w_in']]))
    w5 = jnp.concatenate([w_in_pad[:, :3 * FW], w_in_pad[:, 3 * FW + LANES:]], axis=1)
    wf = w_in_pad[:, 3 * FW:3 * FW + LANES]
    proj = _mm('proj_in', h1, w5, 'nn', F32)
    f_raw = _mm('proj_f', h1, wf, 'nn', F32)
    qn, kn, vb = qkv_fwd(proj, vec('g_q'), vec('g_k'), FW)
    cc = fgate_fwd(f_raw, b_f_pad)
    ct = cc[:, :8].T
    o_fox, lse = fox_fwd(qn, kn, vb, cc, ct, T)
    lru_w = (conv_full, vec('conv_b'), W['w_ra'], vec('b_ra'), W['w_ri'], vec('b_ri'), vec('lam'))
    y_lru = lru_fwd(proj, *lru_w, u_off, g_off)
    mixn = mix_fwd(o_fox, y_lru, vec('g_fox_out'), vec('g_lru_out'))
    w_out_f = fetch('w_out', mixn).reshape(2 * FW, D)
    x1 = _mm('proj_out', mixn, w_out_f, 'nn', F32, res=xs)

    hq = norm_fwd('norm_xq', x1, vec('g_xattn'))
    mn = norm_fwd('norm_mem', ms, vec('g_mem'))
    w_cq_f = fetch('w_cq', hq).reshape(D, XW)
    w_ckv_f = fetch('w_ckv', hq).reshape(D, 2 * XW)
    cq_raw = _mm('proj_cq', hq, w_cq_f, 'nn', F32)
    ckv = _mm('proj_ckv', mn, w_ckv_f, 'nn', F32)
    o_x = xattn_fwd(cq_raw, ckv, vec('g_cq'), vec('g_ck'))
    w_co_g = fetch('w_co', o_x)
    x2 = _mm_colsharded('proj_co', o_x, w_co_g, F32, res=x1)

    hf = norm_fwd('norm_ffn', x2, vec('g_ffn'))
    w_gu_g = fetch('w_gate_up', hf)
    gu = _mm_colsharded('proj_gate_up', hf, w_gu_g, F32)
    act = swiglu_fwd(gu, F)
    w_down_f = fetch('w_down', act).reshape(F, D)
    yv = _mm('proj_down', act, w_down_f, 'nn', F32, res=x2)
    dy, dyb, loss_blk = loss_head(yv, tgt)

    gw, pending = {}, []

    def reduce_begin(n, g):
        sp = g.reshape(N_CHIPS, 2, g.shape[1] // 2, g.shape[2])
        ssem, rsem, sp, zone, tok = scatter_start('scatter_start_' + n, sp)
        pending.append((n, sp, zone, ssem, rsem))
        return tok[0:1, 0:1]

    dact = _mm('bwd_down_x', dyb, w_down_f, 'nt', F32)
    t_down = reduce_begin('w_down', _mm('bwd_down_w', act, dyb, 'tn', BF16).reshape(N_CHIPS, F // N_CHIPS, D))
    dgu = swiglu_bwd(gu, dact, F, t_down)
    dhf = _mm_colsharded_t('bwd_gate_up_x', dgu, w_gu_g, F32)
    t_gu = reduce_begin('w_gate_up', _mm_grad_colsharded('bwd_gate_up_w', hf, dgu, N_CHIPS, BF16))
    dx2, dx2b, gw['g_ffn'] = norm_bwd('norm_ffn_bwd', x2, vec('g_ffn') + t_down + t_gu, dhf, res=dy)

    do_x = _mm_colsharded_t('bwd_co_x', dx2b, w_co_g, BF16)
    t_co = reduce_begin('w_co', _mm_grad_colsharded('bwd_co_w', o_x, dx2b, N_CHIPS, BF16))
    dcq_raw, dckv, gw['g_cq'], gw['g_ck'] = xattn_bwd(cq_raw, ckv, vec('g_cq') + t_co, vec('g_ck'), do_x)
    dhq = _mm('bwd_cq_x', dcq_raw, w_cq_f, 'nt', F32)
    t_cq = reduce_begin('w_cq', _mm('bwd_cq_w', hq, dcq_raw, 'tn', BF16).reshape(N_CHIPS, D // N_CHIPS, XW))
    dmn = _mm('bwd_ckv_x', dckv, w_ckv_f, 'nt', F32)
    t_ckv = reduce_begin('w_ckv', _mm('bwd_ckv_w', mn, dckv, 'tn', BF16).reshape(N_CHIPS, D // N_CHIPS, 2 * XW))
    (gw['g_mem'],) = norm_bwd('norm_mem_bwd', ms, vec('g_mem'), dmn, want_dx=False)
    dx1, dx1b, gw['g_xattn'] = norm_bwd('norm_xq_bwd', x1, vec('g_xattn') + t_cq + t_ckv, dhq, res=dx2)

    dmix = _mm('bwd_out_x', dx1b, w_out_f, 'nt', F32)
    t_out = reduce_begin('w_out', _mm('bwd_out_w', mixn, dx1b, 'tn', BF16).reshape(N_CHIPS, 2 * FW // N_CHIPS, D))
    do_fox, delta, dy_lru, gw['g_fox_out'], gw['g_lru_out'] = mix_bwd(o_fox, y_lru, vec('g_fox_out') + t_out,
                                                                     vec('g_lru_out'), dmix)
    (du, dgate, gw['conv_w'], gw['conv_b'], gw['w_ra'], gw['b_ra'], gw['w_ri'], gw['b_ri'],
     gw['lam']) = lru_bwd(proj, dy_lru, *lru_w, u_off, g_off)
    early = [n for n in SMALL if n not in ('g_q', 'g_k', 'b_f', 'g_mix')]
    late = [n for n in SMALL if n not in early]
    e_ssem, e_rsem, e_buf, e_zone, e_tok = spread_start('spread_start_early', _pack([gw[n] for n in early]))
    dqn, delta2 = fox_bwd_q(qn, kn, vb, do_fox, cc, ct, lse, delta, T)
    dkn, dv, dct = fox_bwd_kv(qn, kn, vb, do_fox, cc, ct, lse, delta2, T)
    dq, dk, gw['g_q'], gw['g_k'] = qkv_bwd(proj, vec('g_q') + e_tok[0:1, 0:1], vec('g_k'), dqn, dkn, FW)
    dc = jnp.pad(dct.reshape(H, S).T, ((0, 0), (0, LANES - H)))
    df, db_f = fgate_bwd(f_raw, b_f_pad, dc, H)
    gw['b_f'] = db_f[:, :H]
    dproj = jnp.concatenate([dq, dk, dv, du, dgate], axis=1)
    dw5 = _mm('bwd_in_w', h1, dproj, 'tn', BF16)
    dwf = _mm('bwd_f_w', h1, df, 'tn', BF16)
    t_in = reduce_begin('w_in', wins.windows(jnp.concatenate([dw5[:, :3 * FW], dwf, dw5[:, 3 * FW:]], axis=1)))
    dh_a = _mm('bwd_f_x', df, wf, 'nt', F32)
    dh1 = _mm('bwd_in_x', dproj, w5, 'nt', F32, res=dh_a)
    grad_x, _, gw['g_mix'] = norm_bwd('norm_mix_bwd', xs, vec('g_mix') + t_in, dh1, res=dx1)
    l_ssem, l_rsem, l_buf, l_zone, _ = spread_start('spread_start_late',
                                                    _pack([gw[n] for n in late] + [loss_blk[0:1, 0:1]]))

    grads, delta_w, new_m, new_v = {}, {}, {}, {}
    done = grad_x
    for n, part, zone, ssem, rsem in pending:
        part, landed = scatter_wait('scatter_wait_' + n, part, zone, ssem, rsem, done)
        mine = sum_parts('sum_parts_' + n, part, landed, chip_idx, c_idx)
        (other,) = pair_join('pair_join_' + n, [mine])
        if n == 'w_in':
            mine, other = wins.from_window(mine, chip), wins.from_window(other, chip)
        grads[n], delta_w[n], new_m[n], new_v[n] = adamw_halves('adamw_' + n, W[n], mine, other, M1[n], V1[n], c_idx)
        done = delta_w[n]

    device = 4 * lax.axis_index('x') + 2 * lax.axis_index('y') + lax.axis_index('c')
    summed = {}
    for tag, names, buf, zone, ssem, rsem in (('early', early, e_buf, e_zone, e_ssem, e_rsem),
                                              ('late', late + ['loss'], l_buf, l_zone, l_ssem, l_rsem)):
        buf, zone = spread_wait('spread_wait_' + tag, buf, zone, ssem, rsem, done)
        total = sum_devices('sum_small_' + tag, lax.dynamic_update_index_in_dim(zone, buf, device, 0))
        summed.update(zip(names, _unpack(total, [gw[n].shape if n != 'loss' else (1, 1) for n in names])))
    loss = summed['loss'].reshape(())
    for n in SMALL:
        g = summed[n]
        grads[n] = g.reshape(W[n].shape) if n != 'conv_w' else lax.dynamic_slice_in_dim(
            g, chip * (LW // N_CHIPS), LW // N_CHIPS, axis=1)
    packs = [_pack([d[n] for n in SMALL]) for d in (W, grads, M1, V1)]
    shapes = [W[n].shape for n in SMALL]
    for d, res in zip((delta_w, new_m, new_v), adamw('adamw_small', *packs)):
        d.update(zip(SMALL, _unpack(res, shapes)))

    lead = lambda d: [d[n][None] for n in WEIGHTS]
    return (loss, grad_x[None], *lead(grads), *lead(delta_w), *lead(new_m), *lead(new_v))
```

```python
import functools
import math

import jax
import jax.numpy as jnp
from jax import lax
from jax.experimental import pallas as pl
from jax.experimental.pallas import tpu as pltpu

F32 = jnp.float32
BF16 = jnp.bfloat16
HEAD_DIM = 128
LANES = 128
LRU_C = 8.0
RMS_EPS = 1e-6
CONV_W = 4
ADAM_LR = 0.001
ADAM_B1 = 0.9
ADAM_B2 = 0.999
ADAM_EPS = 1e-08
ADAM_WD = 0.01
ADAM_STEP = 10
VMEM_LIMIT = 56 * 1024 * 1024
N_CHIPS = 4
MESH = pl.DeviceIdType.MESH
ANY = pl.BlockSpec(memory_space=pl.ANY)

WEIGHTS = ['g_mix', 'w_in', 'b_f', 'g_q', 'g_k', 'conv_w', 'conv_b', 'w_ra', 'b_ra', 'w_ri', 'b_ri', 'lam',
           'g_fox_out', 'g_lru_out', 'w_out', 'g_xattn', 'g_mem', 'w_cq', 'w_ckv', 'g_cq', 'g_ck', 'w_co', 'g_ffn',
           'w_gate_up', 'w_down']
BIG = ['w_in', 'w_out', 'w_cq', 'w_ckv', 'w_co', 'w_gate_up', 'w_down']
SMALL = [n for n in WEIGHTS if n not in BIG]


def _params(sem=None):
    if sem is None:
        return pltpu.CompilerParams(vmem_limit_bytes=VMEM_LIMIT)
    return pltpu.CompilerParams(dimension_semantics=sem, vmem_limit_bytes=VMEM_LIMIT)


def _tile(n, cands):
    for t in cands:
        if n % t == 0:
            return t
    return n


ROW_BLOCK_BYTES = 1 << 20


def _row_tile(n_rows, n_cols, min_rows=8):
    cands = [t for t in (512, 256, 128, 64, 32, 16, 8) if t >= min_rows and t * n_cols * 4 <= ROW_BLOCK_BYTES]
    return _tile(n_rows, cands or [min_rows])


def _sigmoid(z):
    return 1.0 / (1.0 + jnp.exp(-z))


def _softplus(z):
    return jnp.maximum(z, 0.0) + jnp.log(1.0 + jnp.exp(-jnp.abs(z)))


def _neg_expm1(z):
    series = -z * (1.0 + z * (0.5 + z * (1.0 / 6.0 + z * (1.0 / 24.0 + z * (1.0 / 120.0)))))
    return jnp.where(z > -0.25, series, 1.0 - jnp.exp(z))


_GELU_K = math.sqrt(2.0 / math.pi)


def _gelu_and_grad(z):
    inner = _GELU_K * (z + 0.044715 * z * z * z)
    t = jnp.tanh(inner)
    g = 0.5 * z * (1.0 + t)
    dg = 0.5 * (1.0 + t) + 0.5 * z * (1.0 - t * t) * _GELU_K * (1.0 + 3.0 * 0.044715 * z * z)
    return g, dg


def _rms(xv, g):
    r = lax.rsqrt(jnp.mean(xv * xv, axis=-1, keepdims=True) + RMS_EPS)
    return xv * r * g


def _rms_bwd(xv, g, dy):
    r = lax.rsqrt(jnp.mean(xv * xv, axis=-1, keepdims=True) + RMS_EPS)
    xh = xv * r
    dyg = dy * g
    dx = r * (dyg - xh * jnp.mean(dyg * xh, axis=-1, keepdims=True))
    return dx, jnp.sum(dy * xh, axis=0, keepdims=True)


def _heads(fn, n_heads, *arrs):
    outs = [fn(*[a[:, h * HEAD_DIM:(h + 1) * HEAD_DIM] for a in arrs]) for h in range(n_heads)]
    first = jnp.concatenate([o[0] for o in outs], axis=1) if n_heads > 1 else outs[0][0]
    rest = [functools.reduce(lambda p, q: p + q, [o[i] for o in outs]) for i in range(1, len(outs[0]))]
    return (first, *rest)


def _split3(v):
    hi = v.astype(BF16)
    r1 = v - hi.astype(F32)
    mid = r1.astype(BF16)
    lo = (r1 - mid.astype(F32)).astype(BF16)
    return hi, mid, lo


def _acc_out(ref, first, val):
    @pl.when(first)
    def _():
        ref[...] = val

    @pl.when(jnp.logical_not(first))
    def _():
        ref[...] += val


_DIMS = {'nn': (((1,), (0,)), ((), ())), 'nt': (((1,), (1,)), ((), ())), 'tn': (((0,), (0,)), ((), ()))}


MM_VMEM_BYTES = 36 * 1024 * 1024


MXU_FLOPS = 800e12
HBM_BYTES_S = 3.2e12
VMEM_ADD_BYTES_S = 8e12
STEP_S = 0.35e-6


def _k_tile(K, tm, tn, a, b, o_dtype, res):
    fixed = tm * tn * (2 * jnp.dtype(o_dtype).itemsize + 4 + (8 if res is not None else 0))
    per_k = 2 * (tm * a.dtype.itemsize + tn * b.dtype.itemsize)
    per_k += 2 * tm * (a.dtype.itemsize > 2) + 2 * tn * (b.dtype.itemsize > 2)
    units = K // LANES
    for d in sorted((d for d in range(1, units + 1) if units % d == 0), reverse=True):
        if fixed + d * LANES * per_k <= MM_VMEM_BYTES:
            return d * LANES
    return None


def _mm_tiles(M, N, K, k_span, a, b, o_dtype, res, tn_cands=(2048, 1024, 512, 256, 128)):
    best = None
    for tm in (2048, 1024, 512, 256, 128):
        for tn in tn_cands:
            if M % tm or N % tn:
                continue
            tk = _k_tile(k_span, tm, tn, a, b, o_dtype, res)
            if tk is None:
                continue
            nk = K // tk
            traffic = (M * K * a.dtype.itemsize * (N // tn) + K * N * b.dtype.itemsize * (M // tm)
                       + M * N * (jnp.dtype(o_dtype).itemsize + (4 if res is not None else 0)))
            work = 2.0 * M * N * K / MXU_FLOPS + (M * N * 4 * nk / VMEM_ADD_BYTES_S if nk > 1 else 0.0)
            t = max(work, traffic / HBM_BYTES_S) + (M // tm) * (N // tn) * nk * STEP_S
            if best is None or t < best[0]:
                best = (t, tm, tn, tk)
    assert best is not None, (M, N, K)
    return best[1:]


def _mm_call(name, a, b, mode, grid, a_spec, b_spec, o_spec, o_shape, o_dtype, acc_shape, res=None):
    nk = grid[2]
    dn = _DIMS[mode]

    def body(*refs):
        a_ref, b_ref = refs[:2]
        r_ref = refs[2] if res is not None else None
        o_ref = refs[3] if res is not None else refs[2]
        part = lax.dot_general(a_ref[...].astype(BF16), b_ref[...].astype(BF16), dn, preferred_element_type=F32)

        def finish(r):
            if r_ref is not None:
                r = r + r_ref[...]
            o_ref[...] = r.astype(o_dtype)

        if nk == 1:
            finish(part)
            return
        acc = refs[-1]
        k = pl.program_id(2)

        @pl.when(k == 0)
        def _():
            acc[...] = part

        @pl.when(k > 0)
        def _():
            acc[...] += part

        @pl.when(k == nk - 1)
        def _():
            finish(acc[...])

    ins = [a, b] + ([] if res is None else [res])
    specs = [a_spec, b_spec] + ([] if res is None else [o_spec])
    return pl.pallas_call(
        body, name=name, grid=grid, in_specs=specs, out_specs=o_spec,
        out_shape=jax.ShapeDtypeStruct(o_shape, o_dtype),
        scratch_shapes=[] if nk == 1 else [pltpu.VMEM(acc_shape, F32)],
        compiler_params=_params(('parallel', 'parallel', 'arbitrary')))(*ins)


def _mm(name, a, b, mode, o_dtype, res=None):
    if mode == 'tn':
        K, M = a.shape
    else:
        M, K = a.shape
    N = b.shape[0] if mode == 'nt' else b.shape[1]
    tm, tn, tk = _mm_tiles(M, N, K, K, a, b, o_dtype, res)
    a_spec = (pl.BlockSpec((tk, tm), lambda m, n, k: (k, m)) if mode == 'tn'
              else pl.BlockSpec((tm, tk), lambda m, n, k: (m, k)))
    b_spec = (pl.BlockSpec((tn, tk), lambda m, n, k: (n, k)) if mode == 'nt'
              else pl.BlockSpec((tk, tn), lambda m, n, k: (k, n)))
    o_spec = pl.BlockSpec((tm, tn), lambda m, n, k: (m, n))
    return _mm_call(name, a, b, mode, (M // tm, N // tn, K // tk), a_spec, b_spec, o_spec, (M, N), o_dtype,
                    (tm, tn), res)


def _mm_colsharded(name, a, w, o_dtype, res=None):
    M, K = a.shape
    J, _, Nj = w.shape
    tm, tn, tk = _mm_tiles(M, J * Nj, K, K, a, w, o_dtype, res,
                           tn_cands=[t for t in (2816, 1408, 1024, 512, 256, 128) if Nj % t == 0])
    per = Nj // tn
    return _mm_call(name, a, w, 'nn', (M // tm, J * per, K // tk),
                    pl.BlockSpec((tm, tk), lambda m, n, k: (m, k)),
                    pl.BlockSpec((None, tk, tn), lambda m, n, k: (n // per, k, n % per)),
                    pl.BlockSpec((tm, tn), lambda m, n, k: (m, n)), (M, J * Nj), o_dtype, (tm, tn), res)


def _planes_spec(arr, rows, cols, row_of, col_of):
    if arr.ndim == 2:
        return pl.BlockSpec((rows, cols), lambda m, n, k: (row_of(m, n, k), col_of(m, n, k)))
    per_plane = arr.shape[2] // cols
    return pl.BlockSpec((None, rows, cols),
                        lambda m, n, k: (col_of(m, n, k) // per_plane, row_of(m, n, k), col_of(m, n, k) % per_plane))


def _mm_colsharded_t(name, a, w, o_dtype):
    M = a.shape[-2]
    J, K, Nj = w.shape
    tm, tn, tk = _mm_tiles(M, K, J * Nj, Nj, a, w, o_dtype, None)
    per = Nj // tk
    return _mm_call(name, a, w, 'nt', (M // tm, K // tn, J * per),
                    _planes_spec(a, tm, tk, lambda m, n, k: m, lambda m, n, k: k),
                    pl.BlockSpec((None, tn, tk), lambda m, n, k: (k // per, n, k % per)),
                    pl.BlockSpec((tm, tn), lambda m, n, k: (m, n)), (M, K), o_dtype, (tm, tn))


def _mm_grad_colsharded(name, a, dy, J, o_dtype):
    S, M = a.shape
    Nj = dy.shape[-1] * (dy.shape[0] if dy.ndim == 3 else 1) // J
    tm, tn, tk = _mm_tiles(M, J * Nj, S, S, a, dy, o_dtype, None,
                           tn_cands=[t for t in (2816, 1408, 1024, 512, 256, 128) if Nj % t == 0])
    per = Nj // tn
    return _mm_call(name, a, dy, 'tn', (M // tm, J * per, S // tk),
                    pl.BlockSpec((tk, tm), lambda m, n, k: (k, m)),
                    _planes_spec(dy, tk, tn, lambda m, n, k: k, lambda m, n, k: n),
                    pl.BlockSpec((None, tm, tn), lambda m, n, k: (n // per, m, n % per)), (J, M, Nj), o_dtype, (tm, tn))


def _rows_call(name, body, n_rows, tr, ins, outs):
    return pl.pallas_call(
        body, name=name, grid=(n_rows // tr,), in_specs=[s for _, s in ins], out_specs=[s for _, _, s in outs],
        out_shape=[jax.ShapeDtypeStruct(sh, dt) for sh, dt, _ in outs],
        compiler_params=_params(('arbitrary',)))(*[a for a, _ in ins])


def _rb(tr, w, cb=0):
    return pl.BlockSpec((tr, w), lambda i: (i, cb))


def _fb(shape):
    nd = len(shape)
    return pl.BlockSpec(shape, lambda i: (0,) * nd)


def norm_fwd(name, xv, g):
    S, D = xv.shape
    tr = _tile(S, (256, 128))

    def body(x_ref, g_ref, o_ref):
        o_ref[...] = _rms(x_ref[...], g_ref[...]).astype(BF16)

    return _rows_call(name, body, S, tr, [(xv, _rb(tr, D)), (g, _fb((1, D)))], [((S, D), BF16, _rb(tr, D))])[0]


def norm_bwd(name, xv, g, dy, res=None, want_dx=True):
    S, D = xv.shape
    tr = _tile(S, (256, 128))

    def body(*refs):
        if res is None:
            x_ref, g_ref, dy_ref = refs[:3]
            outs = refs[3:]
            r_ref = None
        else:
            x_ref, g_ref, dy_ref, r_ref = refs[:4]
            outs = refs[4:]
        dx, dg = _rms_bwd(x_ref[...], g_ref[...], dy_ref[...])
        if r_ref is not None:
            dx = dx + r_ref[...]
        if want_dx:
            outs[0][...] = dx
            outs[1][...] = dx.astype(BF16)
        _acc_out(outs[-1], pl.program_id(0) == 0, dg)

    ins = [(xv, _rb(tr, D)), (g, _fb((1, D))), (dy, _rb(tr, D))] + ([] if res is None else [(res, _rb(tr, D))])
    outs = ([((S, D), F32, _rb(tr, D)), ((S, D), BF16, _rb(tr, D))] if want_dx else []) + [((1, D), F32, _fb((1, D)))]
    return _rows_call(name, body, S, tr, ins, outs)


def qkv_fwd(proj, g_q, g_k, FW):
    S = proj.shape[0]
    H = FW // HEAD_DIM
    tr = _tile(S, (256, 128))

    def body(q_ref, k_ref, v_ref, gq_ref, gk_ref, qo, ko, vo):
        qo[...] = _heads(lambda t: (_rms(t, gq_ref[...]),), H, q_ref[...])[0].astype(BF16)
        ko[...] = _heads(lambda t: (_rms(t, gk_ref[...]),), H, k_ref[...])[0].astype(BF16)
        vo[...] = v_ref[...].astype(BF16)

    o = ((S, FW), BF16, _rb(tr, FW))
    return _rows_call('qkv_fwd', body, S, tr,
                      [(proj, _rb(tr, FW, 0)), (proj, _rb(tr, FW, 1)), (proj, _rb(tr, FW, 2)),
                       (g_q, _fb((1, HEAD_DIM))), (g_k, _fb((1, HEAD_DIM)))], [o, o, o])


def qkv_bwd(proj, g_q, g_k, dqn, dkn, FW):
    S = proj.shape[0]
    H = FW // HEAD_DIM
    tr = _tile(S, (256, 128))

    def body(q_ref, k_ref, gq_ref, gk_ref, dq_ref, dk_ref, dqo, dko, dgq, dgk):
        dq, gq = _heads(lambda t, d: _rms_bwd(t, gq_ref[...], d), H, q_ref[...], dq_ref[...])
        dk, gk = _heads(lambda t, d: _rms_bwd(t, gk_ref[...], d), H, k_ref[...], dk_ref[...])
        dqo[...] = dq.astype(BF16)
        dko[...] = dk.astype(BF16)
        first = pl.program_id(0) == 0
        _acc_out(dgq, first, gq)
        _acc_out(dgk, first, gk)

    o = ((S, FW), BF16, _rb(tr, FW))
    og = ((1, HEAD_DIM), F32, _fb((1, HEAD_DIM)))
    return _rows_call('qkv_bwd', body, S, tr,
                      [(proj, _rb(tr, FW, 0)), (proj, _rb(tr, FW, 1)), (g_q, _fb((1, HEAD_DIM))),
                       (g_k, _fb((1, HEAD_DIM))), (dqn, _rb(tr, FW)), (dkn, _rb(tr, FW))], [o, o, og, og])


def _tri(n, upper):
    r = lax.broadcasted_iota(jnp.int32, (n, n), 0)
    c = lax.broadcasted_iota(jnp.int32, (n, n), 1)
    return jnp.where((c >= r) if upper else (c <= r), 1.0, 0.0).astype(BF16)


def _blocked_cumsum(val, S, blk, reverse):
    tri = _tri(blk, reverse)
    order = range(S // blk - 1, -1, -1) if reverse else range(S // blk)
    carry = jnp.zeros((1, LANES), F32)
    outs = {}
    for bi in order:
        part = val[bi * blk:(bi + 1) * blk]
        acc = carry
        for piece in _split3(part):
            acc = acc + jnp.dot(tri, piece, preferred_element_type=F32)
        outs[bi] = acc
        carry = carry + jnp.sum(part, axis=0, keepdims=True)
    return jnp.concatenate([outs[bi] for bi in range(S // blk)], axis=0)


def fgate_fwd(f_raw, b_f_pad):
    S = f_raw.shape[0]
    blk = _tile(S, (256, 128))

    def body(f_ref, b_ref, c_ref):
        z = f_ref[...] + b_ref[...]
        c_ref[...] = _blocked_cumsum(-_softplus(-z), S, blk, False)

    return pl.pallas_call(body, name='fgate_fwd', grid=(1,), in_specs=[_fb((S, LANES)), _fb((1, LANES))],
                          out_specs=_fb((S, LANES)), out_shape=jax.ShapeDtypeStruct((S, LANES), F32),
                          compiler_params=_params(('arbitrary',)))(f_raw, b_f_pad)


def fgate_bwd(f_raw, b_f_pad, dc, H):
    S = f_raw.shape[0]
    blk = _tile(S, (256, 128))

    def body(f_ref, b_ref, dc_ref, df_ref, db_ref):
        z = f_ref[...] + b_ref[...]
        dlogf = _blocked_cumsum(dc_ref[...], S, blk, True)
        lane = lax.broadcasted_iota(jnp.int32, (S, LANES), 1)
        df = jnp.where(lane < H, dlogf * _sigmoid(-z), 0.0)
        df_ref[...] = df.astype(BF16)
        db_ref[...] = jnp.sum(df, axis=0, keepdims=True)

    return pl.pallas_call(body, name='fgate_bwd', grid=(1,),
                          in_specs=[_fb((S, LANES)), _fb((1, LANES)), _fb((S, LANES))],
                          out_specs=[_fb((S, LANES)), _fb((1, LANES))],
                          out_shape=[jax.ShapeDtypeStruct((S, LANES), BF16), jax.ShapeDtypeStruct((1, LANES), F32)],
                          compiler_params=_params(('arbitrary',)))(f_raw, b_f_pad, dc)


def _fox_logits(q, k, c_blk, ct_blk, h, T, diagonal):
    s = lax.dot_general(q, k, _DIMS['nt'], preferred_element_type=F32) * (1.0 / math.sqrt(HEAD_DIM))
    lane = lax.broadcasted_iota(jnp.int32, c_blk.shape, 1)
    cq = jnp.sum(jnp.where(lane == h, c_blk, 0.0), axis=1, keepdims=True)
    sub = lax.broadcasted_iota(jnp.int32, ct_blk.shape, 0)
    ck = jnp.sum(jnp.where(sub == h, ct_blk, 0.0), axis=0, keepdims=True)
    s = s + cq - ck
    if not diagonal:
        return s
    rows = lax.broadcasted_iota(jnp.int32, (T, T), 0)
    cols = lax.broadcasted_iota(jnp.int32, (T, T), 1)
    return jnp.where(cols <= rows, s, -jnp.inf)


def _below_and_on_diagonal(q_blk, k_blk, step):
    @pl.when(k_blk < q_blk)
    def _():
        step(False)

    @pl.when(k_blk == q_blk)
    def _():
        step(True)


def fox_fwd(qn, kn, vb, c, ct, T):
    S, FW = qn.shape
    H = FW // HEAD_DIM
    Hp = ct.shape[0]
    n = S // T

    HB = 2 if H % 2 == 0 else 1
    W2 = HB * HEAD_DIM

    def body(q_ref, k_ref, v_ref, c_ref, ct_ref, o_ref, lse_ref, m_s, l_s, acc_s):
        hb, i, j = pl.program_id(0), pl.program_id(1), pl.program_id(2)

        @pl.when(j == 0)
        def _():
            m_s[...] = jnp.full_like(m_s, -jnp.inf)
            l_s[...] = jnp.zeros_like(l_s)
            acc_s[...] = jnp.zeros_like(acc_s)

        def step(diagonal):
            for hh in range(HB):
                sl = slice(hh * HEAD_DIM, (hh + 1) * HEAD_DIM)
                s = _fox_logits(q_ref[:, sl], k_ref[:, sl], c_ref[...], ct_ref[...], hb * HB + hh, T, diagonal)
                m_old = m_s[hh]
                m_new = jnp.maximum(m_old, jnp.max(s, axis=1, keepdims=True))
                alpha = jnp.exp(m_old - m_new)
                p = jnp.exp(s - m_new)
                l_s[hh] = alpha * l_s[hh] + jnp.sum(p, axis=1, keepdims=True)
                acc_s[hh] = alpha * acc_s[hh] + jnp.dot(p.astype(BF16), v_ref[:, sl], preferred_element_type=F32)
                m_s[hh] = m_new

        _below_and_on_diagonal(i, j, step)

        @pl.when(j == i)
        def _():
            for hh in range(HB):
                o_ref[:, hh * HEAD_DIM:(hh + 1) * HEAD_DIM] = acc_s[hh] / l_s[hh]
                lse_ref[hh] = jnp.broadcast_to(m_s[hh] + jnp.log(l_s[hh]), (T, LANES))

    qs = pl.BlockSpec((T, W2), lambda h, i, j: (i, h))
    ks = pl.BlockSpec((T, W2), lambda h, i, j: (jnp.minimum(j, i), h))
    return pl.pallas_call(
        body, name='fox_fwd', grid=(H // HB, n, n),
        in_specs=[qs, ks, ks, pl.BlockSpec((T, LANES), lambda h, i, j: (i, 0)),
                  pl.BlockSpec((Hp, T), lambda h, i, j: (0, jnp.minimum(j, i)))],
        out_specs=[qs, pl.BlockSpec((HB, T, LANES), lambda h, i, j: (h, i, 0))],
        out_shape=[jax.ShapeDtypeStruct((S, FW), F32), jax.ShapeDtypeStruct((H, S, LANES), F32)],
        scratch_shapes=[pltpu.VMEM((HB, T, 1), F32), pltpu.VMEM((HB, T, 1), F32), pltpu.VMEM((HB, T, HEAD_DIM), F32)],
        compiler_params=_params(('parallel', 'parallel', 'arbitrary')))(qn, kn, vb, c, ct)


def _fox_p_ds(q_ref, k_ref, v_ref, do_ref, c_ref, ct_ref, lse_ref, dl_ref, h, T, diagonal):
    s = _fox_logits(q_ref[...], k_ref[...], c_ref[...], ct_ref[...], h, T, diagonal)
    p = jnp.exp(s - jnp.tile(lse_ref[...], (1, T // LANES)))
    dp = lax.dot_general(do_ref[...], v_ref[...], _DIMS['nt'], preferred_element_type=F32)
    ds = p * (dp - jnp.tile(dl_ref[...], (1, T // LANES)))
    return p, dp, ds


def fox_bwd_q(qn, kn, vb, do, c, ct, lse, dl, T):
    S, FW = qn.shape
    H = FW // HEAD_DIM
    Hp = ct.shape[0]
    n = S // T

    def body(q_ref, k_ref, v_ref, do_ref, c_ref, ct_ref, lse_ref, dl_ref, dq_ref, dl2_ref, acc_s, rs_s):
        h, i, j = pl.program_id(0), pl.program_id(1), pl.program_id(2)

        @pl.when(j == 0)
        def _():
            acc_s[...] = jnp.zeros_like(acc_s)
            rs_s[...] = jnp.zeros_like(rs_s)

        def step(diagonal):
            p, dp, ds = _fox_p_ds(q_ref, k_ref, v_ref, do_ref, c_ref, ct_ref, lse_ref, dl_ref, h, T, diagonal)
            acc_s[...] += jnp.dot(ds.astype(BF16), k_ref[...], preferred_element_type=F32)
            rs_s[...] += jnp.sum(p * dp, axis=1, keepdims=True)

        _below_and_on_diagonal(i, j, step)

        @pl.when(j == i)
        def _():
            dq_ref[...] = acc_s[...] * (1.0 / math.sqrt(HEAD_DIM))
            dl2_ref[...] = jnp.broadcast_to(rs_s[...], (T, LANES))

    qs = pl.BlockSpec((T, HEAD_DIM), lambda h, i, j: (i, h))
    ks = pl.BlockSpec((T, HEAD_DIM), lambda h, i, j: (jnp.minimum(j, i), h))
    st = pl.BlockSpec((None, T, LANES), lambda h, i, j: (h, i, 0))
    return pl.pallas_call(
        body, name='fox_bwd_q', grid=(H, n, n),
        in_specs=[qs, ks, ks, qs, pl.BlockSpec((T, LANES), lambda h, i, j: (i, 0)),
                  pl.BlockSpec((Hp, T), lambda h, i, j: (0, jnp.minimum(j, i))), st, st],
        out_specs=[qs, st], out_shape=[jax.ShapeDtypeStruct((S, FW), F32), jax.ShapeDtypeStruct((H, S, LANES), F32)],
        scratch_shapes=[pltpu.VMEM((T, HEAD_DIM), F32), pltpu.VMEM((T, 1), F32)],
        compiler_params=_params(('parallel', 'parallel', 'arbitrary')))(qn, kn, vb, do, c, ct, lse, dl)


def fox_bwd_kv(qn, kn, vb, do, c, ct, lse, dl, T):
    S, FW = qn.shape
    H = FW // HEAD_DIM
    Hp = ct.shape[0]
    n = S // T

    def body(q_ref, k_ref, v_ref, do_ref, c_ref, ct_ref, lse_ref, dl_ref, dk_ref, dv_ref, dc_ref, dk_s, dv_s, dc_s):
        h, j, i = pl.program_id(0), pl.program_id(1), pl.program_id(2)

        @pl.when(i == 0)
        def _():
            dk_s[...] = jnp.zeros_like(dk_s)
            dv_s[...] = jnp.zeros_like(dv_s)
            dc_s[...] = jnp.zeros_like(dc_s)

        def step(diagonal):
            p, _, ds = _fox_p_ds(q_ref, k_ref, v_ref, do_ref, c_ref, ct_ref, lse_ref, dl_ref, h, T, diagonal)
            dv_s[...] += lax.dot_general(p.astype(BF16), do_ref[...], _DIMS['tn'], preferred_element_type=F32)
            dk_s[...] += lax.dot_general(ds.astype(BF16), q_ref[...], _DIMS['tn'], preferred_element_type=F32)
            dc_s[...] += jnp.sum(ds, axis=0, keepdims=True)

        _below_and_on_diagonal(i, j, step)

        @pl.when(i == n - 1)
        def _():
            dk_ref[...] = dk_s[...] * (1.0 / math.sqrt(HEAD_DIM))
            dv_ref[...] = dv_s[...].astype(BF16)
            dc_ref[...] = -dc_s[...]

    qs = pl.BlockSpec((T, HEAD_DIM), lambda h, j, i: (jnp.maximum(i, j), h))
    ks = pl.BlockSpec((T, HEAD_DIM), lambda h, j, i: (j, h))
    st = pl.BlockSpec((None, T, LANES), lambda h, j, i: (h, jnp.maximum(i, j), 0))
    return pl.pallas_call(
        body, name='fox_bwd_kv', grid=(H, n, n),
        in_specs=[qs, ks, ks, qs, pl.BlockSpec((T, LANES), lambda h, j, i: (jnp.maximum(i, j), 0)),
                  pl.BlockSpec((Hp, T), lambda h, j, i: (0, j)), st, st],
        out_specs=[ks, ks, pl.BlockSpec((None, 1, T), lambda h, j, i: (h, 0, j))],
        out_shape=[jax.ShapeDtypeStruct((S, FW), F32), jax.ShapeDtypeStruct((S, FW), BF16),
                   jax.ShapeDtypeStruct((H, 1, S), F32)],
        scratch_shapes=[pltpu.VMEM((T, HEAD_DIM), F32), pltpu.VMEM((T, HEAD_DIM), F32), pltpu.VMEM((1, T), F32)],
        compiler_params=_params(('parallel', 'parallel', 'arbitrary')))(qn, kn, vb, do, c, ct, lse, dl)


def _shift_down(v, d, rows, fill):
    return jnp.where(rows >= d, pltpu.roll(v, d, 0), fill)


def _shift_up(v, d, rows, S, fill):
    return jnp.where(rows < S - d, pltpu.roll(v, S - d, 0), fill)


SUBLANES = 8


def _scan_by_doubling(a, b, pos, span, reverse):
    n = a.shape[0]
    d = 1
    while d < span:
        if reverse:
            keep = pos < span - d
            a_s, b_s = jnp.where(keep, pltpu.roll(a, n - d, 0), 1.0), jnp.where(keep, pltpu.roll(b, n - d, 0), 0.0)
        else:
            keep = pos >= d
            a_s, b_s = jnp.where(keep, pltpu.roll(a, d, 0), 1.0), jnp.where(keep, pltpu.roll(b, d, 0), 0.0)
        b = a * b_s + b
        a = a * a_s
        d *= 2
    return a, b


def _scan(a, b, rows, S, reverse, scr):
    groups = S // SUBLANES
    a, b = _scan_by_doubling(a, b, jnp.bitwise_and(rows, SUBLANES - 1), SUBLANES, reverse)
    scr[0][...] = a
    scr[1][...] = b
    edge = 0 if reverse else SUBLANES - 1
    a_g = scr[0][pl.ds(edge, groups, stride=SUBLANES), :]
    b_g = scr[1][pl.ds(edge, groups, stride=SUBLANES), :]
    g_pos = lax.broadcasted_iota(jnp.int32, (groups, LANES), 0)
    _, h_g = _scan_by_doubling(a_g, b_g, g_pos, groups, reverse)
    if reverse:
        carry = jnp.where(g_pos < groups - 1, pltpu.roll(h_g, groups - 1, 0), 0.0)
    else:
        carry = jnp.where(g_pos >= 1, pltpu.roll(h_g, 1, 0), 0.0)
    for r in range(SUBLANES):
        scr[0][pl.ds(r, groups, stride=SUBLANES), :] = carry
    return b + a * scr[0][...]


def _lru_forward(u, cw, cb, wra, bra, wri, bri, lam, rows, scr):
    uc = cb + cw[CONV_W - 1] * u
    for d in range(1, CONV_W):
        uc = uc + cw[CONV_W - 1 - d] * _shift_down(u, d, rows, 0.0)
    ucb = uc.astype(BF16)
    r = _sigmoid(jnp.dot(ucb, wra.astype(BF16), preferred_element_type=F32) + bra)
    ig = _sigmoid(jnp.dot(ucb, wri.astype(BF16), preferred_element_type=F32) + bri)
    sp = _softplus(-lam)
    log_a = -LRU_C * r * sp
    a = jnp.exp(log_a)
    sq = jnp.sqrt(_neg_expm1(2.0 * log_a))
    iu = ig * uc
    hseq = _scan(a, sq * iu, rows, u.shape[0], False, scr)
    return uc, ucb, r, ig, sp, a, sq, iu, hseq


def _lru_specs(S, n_u, n_g):
    col = lambda off: pl.BlockSpec((S, LANES), lambda cbk: (0, off + cbk))
    vec = pl.BlockSpec((1, LANES), lambda cbk: (0, cbk))
    mat = pl.BlockSpec((None, LANES, LANES), lambda cbk: (cbk, 0, 0))
    cw = pl.BlockSpec((CONV_W, LANES), lambda cbk: (0, cbk))
    return col, vec, mat, cw


def lru_fwd(proj, conv_w, conv_b, w_ra, b_ra, w_ri, b_ri, lam, u_off, g_off):
    S = proj.shape[0]
    nb = w_ra.shape[0]
    col, vec, mat, cws = _lru_specs(S, u_off, g_off)

    def body(u_ref, g_ref, cw_ref, cb_ref, wra_ref, bra_ref, wri_ref, bri_ref, lam_ref, y_ref, scr0, scr1):
        rows = lax.broadcasted_iota(jnp.int32, (S, LANES), 0)
        cw = [cw_ref[t:t + 1, :] for t in range(CONV_W)]
        hseq = _lru_forward(u_ref[...], cw, cb_ref[...], wra_ref[...], bra_ref[...], wri_ref[...],
                            bri_ref[...], lam_ref[...], rows, (scr0, scr1))[-1]
        y_ref[...] = hseq * _gelu_and_grad(g_ref[...])[0]

    return pl.pallas_call(
        body, name='lru_fwd', grid=(nb,),
        in_specs=[col(u_off), col(g_off), cws, vec, mat, vec, mat, vec, vec], out_specs=col(0),
        out_shape=jax.ShapeDtypeStruct((S, nb * LANES), F32),
        scratch_shapes=[pltpu.VMEM((S, LANES), F32), pltpu.VMEM((S, LANES), F32)],
        compiler_params=_params(('parallel',)))(proj, proj, conv_w, conv_b, w_ra, b_ra, w_ri, b_ri, lam)


def lru_bwd(proj, dy, conv_w, conv_b, w_ra, b_ra, w_ri, b_ri, lam, u_off, g_off):
    S = proj.shape[0]
    nb = w_ra.shape[0]
    LW = nb * LANES
    col, vec, mat, cws = _lru_specs(S, u_off, g_off)

    def body(u_ref, g_ref, dy_ref, cw_ref, cb_ref, wra_ref, bra_ref, wri_ref, bri_ref, lam_ref,
             du_ref, dg_ref, dcw_ref, dcb_ref, dwra_ref, dbra_ref, dwri_ref, dbri_ref, dlam_ref, scr0, scr1):
        rows = lax.broadcasted_iota(jnp.int32, (S, LANES), 0)
        u, lam_v = u_ref[...], lam_ref[...]
        cw = [cw_ref[t:t + 1, :] for t in range(CONV_W)]
        wra, wri = wra_ref[...].astype(BF16), wri_ref[...].astype(BF16)
        uc, ucb, r, ig, sp, a, sq, iu, hseq = _lru_forward(u, cw, cb_ref[...], wra, bra_ref[...], wri, bri_ref[...],
                                                           lam_v, rows, (scr0, scr1))
        gl, dgl = _gelu_and_grad(g_ref[...])
        dy_v = dy_ref[...]
        dg_ref[...] = (dy_v * hseq * dgl).astype(BF16)
        G = _scan(_shift_up(a, 1, rows, S, 0.0), dy_v * gl, rows, S, True, (scr0, scr1))
        da = G * _shift_down(hseq, 1, rows, 0.0)
        diu = G * sq
        dsq = G * iu
        dlog_a = da * a - dsq * a * a / jnp.maximum(sq, 1e-30)
        dr = dlog_a * (-LRU_C * sp)
        dsp = jnp.sum(dlog_a * (-LRU_C * r), axis=0, keepdims=True)
        dlam_ref[...] = -dsp * _sigmoid(-lam_v)
        dzr = dr * r * (1.0 - r)
        dzi = diu * uc * ig * (1.0 - ig)
        dzrb, dzib = dzr.astype(BF16), dzi.astype(BF16)
        duc = (diu * ig + lax.dot_general(dzrb, wra, _DIMS['nt'], preferred_element_type=F32)
               + lax.dot_general(dzib, wri, _DIMS['nt'], preferred_element_type=F32))
        dwra_ref[...] = lax.dot_general(ucb, dzrb, _DIMS['tn'], preferred_element_type=F32)
        dwri_ref[...] = lax.dot_general(ucb, dzib, _DIMS['tn'], preferred_element_type=F32)
        dbra_ref[...] = jnp.sum(dzr, axis=0, keepdims=True)
        dbri_ref[...] = jnp.sum(dzi, axis=0, keepdims=True)
        dcb_ref[...] = jnp.sum(duc, axis=0, keepdims=True)
        du = cw[CONV_W - 1] * duc
        dcw_ref[CONV_W - 1:CONV_W, :] = jnp.sum(duc * u, axis=0, keepdims=True)
        for d in range(1, CONV_W):
            du = du + cw[CONV_W - 1 - d] * _shift_up(duc, d, rows, S, 0.0)
            dcw_ref[CONV_W - 1 - d:CONV_W - d, :] = jnp.sum(duc * _shift_down(u, d, rows, 0.0), axis=0, keepdims=True)
        du_ref[...] = du.astype(BF16)

    sd = jax.ShapeDtypeStruct
    return pl.pallas_call(
        body, name='lru_bwd', grid=(nb,),
        in_specs=[col(u_off), col(g_off), col(0), cws, vec, mat, vec, mat, vec, vec],
        out_specs=[col(0), col(0), cws, vec, mat, vec, mat, vec, vec],
        out_shape=[sd((S, LW), BF16), sd((S, LW), BF16), sd((CONV_W, LW), F32), sd((1, LW), F32),
                   sd((nb, LANES, LANES), F32), sd((1, LW), F32), sd((nb, LANES, LANES), F32), sd((1, LW), F32),
                   sd((1, LW), F32)],
        scratch_shapes=[pltpu.VMEM((S, LANES), F32), pltpu.VMEM((S, LANES), F32)],
        compiler_params=_params(('parallel',)))(proj, proj, dy, conv_w, conv_b, w_ra, b_ra, w_ri, b_ri, lam)


def mix_fwd(o_fox, y_lru, g_fox, g_lru):
    S, FW = o_fox.shape
    tr = _tile(S, (256, 128))

    def body(o_ref, y_ref, gf_ref, gl_ref, m_ref):
        m_ref[...] = jnp.concatenate([_rms(o_ref[...], gf_ref[...]), _rms(y_ref[...], gl_ref[...])],
                                     axis=1).astype(BF16)

    return _rows_call('mix_fwd', body, S, tr,
                      [(o_fox, _rb(tr, FW)), (y_lru, _rb(tr, FW)), (g_fox, _fb((1, FW))), (g_lru, _fb((1, FW)))],
                      [((S, 2 * FW), BF16, _rb(tr, 2 * FW))])[0]


def mix_bwd(o_fox, y_lru, g_fox, g_lru, dmix):
    S, FW = o_fox.shape
    H = FW // HEAD_DIM
    tr = _tile(S, (256, 128))

    def body(o_ref, y_ref, gf_ref, gl_ref, df_ref, dl_ref, do_ref, dlt_ref, dy_ref, dgf_ref, dgl_ref):
        o = o_ref[...]
        do, dgf = _rms_bwd(o, gf_ref[...], df_ref[...])
        dyl, dgl = _rms_bwd(y_ref[...], gl_ref[...], dl_ref[...])
        do_ref[...] = do.astype(BF16)
        dy_ref[...] = dyl
        prod = do * o
        for h in range(H):
            dlt_ref[h] = jnp.broadcast_to(
                jnp.sum(prod[:, h * HEAD_DIM:(h + 1) * HEAD_DIM], axis=1, keepdims=True), (tr, LANES))
        first = pl.program_id(0) == 0
        _acc_out(dgf_ref, first, dgf)
        _acc_out(dgl_ref, first, dgl)

    g = _fb((1, FW))
    return _rows_call('mix_bwd', body, S, tr,
                      [(o_fox, _rb(tr, FW)), (y_lru, _rb(tr, FW)), (g_fox, g), (g_lru, g), (dmix, _rb(tr, FW, 0)),
                       (dmix, _rb(tr, FW, 1))],
                      [((S, FW), BF16, _rb(tr, FW)), ((H, S, LANES), F32, pl.BlockSpec((H, tr, LANES), lambda i: (0, i, 0))),
                       ((S, FW), F32, _rb(tr, FW)), ((1, FW), F32, g), ((1, FW), F32, g)])


def _xattn_heads(cq_raw, ckv, g_cq, g_ck, XW):
    out = []
    for h in range(XW // HEAD_DIM):
        sl = slice(h * HEAD_DIM, (h + 1) * HEAD_DIM)
        out.append((cq_raw[:, sl], _rms(cq_raw[:, sl], g_cq), ckv[:, sl], _rms(ckv[:, sl], g_ck),
                    ckv[:, XW + h * HEAD_DIM:XW + (h + 1) * HEAD_DIM].astype(BF16)))
    return out


def xattn_fwd(cq_raw, ckv, g_cq, g_ck):
    S, XW = cq_raw.shape
    M = ckv.shape[0]
    tr = _tile(S, (512, 256, 128))

    def body(q_ref, kv_ref, gq_ref, gk_ref, o_ref):
        outs = []
        for _, qn, _, kn, v in _xattn_heads(q_ref[...], kv_ref[...], gq_ref[...], gk_ref[...], XW):
            s = lax.dot_general(qn.astype(BF16), kn.astype(BF16), _DIMS['nt'], preferred_element_type=F32)
            s = s / math.sqrt(HEAD_DIM)
            p = jnp.exp(s - jnp.max(s, axis=1, keepdims=True))
            p = p / jnp.sum(p, axis=1, keepdims=True)
            outs.append(jnp.dot(p.astype(BF16), v, preferred_element_type=F32))
        o_ref[...] = jnp.concatenate(outs, axis=1).astype(BF16)

    g = _fb((1, HEAD_DIM))
    return _rows_call('xattn_fwd', body, S, tr,
                      [(cq_raw, _rb(tr, XW)), (ckv, _fb((M, 2 * XW))), (g_cq, g), (g_ck, g)],
                      [((S, XW), BF16, _rb(tr, XW))])[0]


def xattn_bwd(cq_raw, ckv, g_cq, g_ck, do):
    S, XW = cq_raw.shape
    M = ckv.shape[0]
    tr = _tile(S, (512, 256, 128))
    n = S // tr

    def body(q_ref, kv_ref, gq_ref, gk_ref, do_ref, dq_ref, dkv_ref, dgq_ref, dgk_ref):
        i = pl.program_id(0)
        do_v = do_ref[...]
        dqs, dkn, dvs = [], [], []
        dgq = jnp.zeros((1, HEAD_DIM), F32)
        for h, (q_raw, qn, _, kn, v) in enumerate(_xattn_heads(q_ref[...], kv_ref[...], gq_ref[...], gk_ref[...], XW)):
            qb, kb = qn.astype(BF16), kn.astype(BF16)
            doh = do_v[:, h * HEAD_DIM:(h + 1) * HEAD_DIM]
            s = lax.dot_general(qb, kb, _DIMS['nt'], preferred_element_type=F32) / math.sqrt(HEAD_DIM)
            p = jnp.exp(s - jnp.max(s, axis=1, keepdims=True))
            p = p / jnp.sum(p, axis=1, keepdims=True)
            dp = lax.dot_general(doh, v, _DIMS['nt'], preferred_element_type=F32)
            ds = (p * (dp - jnp.sum(p * dp, axis=1, keepdims=True)) / math.sqrt(HEAD_DIM)).astype(BF16)
            dvs.append(lax.dot_general(p.astype(BF16), doh, _DIMS['tn'], preferred_element_type=F32))
            dkn.append(lax.dot_general(ds, qb, _DIMS['tn'], preferred_element_type=F32))
            dq, g1 = _rms_bwd(q_raw, gq_ref[...], jnp.dot(ds, kb, preferred_element_type=F32))
            dqs.append(dq)
            dgq = dgq + g1
        dq_ref[...] = jnp.concatenate(dqs, axis=1).astype(BF16)
        first = i == 0
        _acc_out(dgq_ref, first, dgq)
        _acc_out(dkv_ref, first, jnp.concatenate(dkn + dvs, axis=1))

        @pl.when(i == n - 1)
        def _():
            kv = kv_ref[...]
            acc = dkv_ref[...]
            dk, gk = _heads(lambda t, d: _rms_bwd(t, gk_ref[...], d), XW // HEAD_DIM, kv[:, :XW], acc[:, :XW])
            dkv_ref[:, :XW] = dk
            dgk_ref[...] = gk

    g = _fb((1, HEAD_DIM))
    return _rows_call('xattn_bwd', body, S, tr,
                      [(cq_raw, _rb(tr, XW)), (ckv, _fb((M, 2 * XW))), (g_cq, g), (g_ck, g), (do, _rb(tr, XW))],
                      [((S, XW), BF16, _rb(tr, XW)), ((M, 2 * XW), F32, _fb((M, 2 * XW))), ((1, HEAD_DIM), F32, g),
                       ((1, HEAD_DIM), F32, g)])


def gate_up_fwd(hf, w, F):
    S, D = hf.shape
    J, _, Nj = w.shape
    tm = _tile(S, (1024, 512, 256, 128))
    tn = _tile(Nj, (256, 128))
    per = Nj // tn
    half = J // 2 * per

    def body(a_ref, bg_ref, bu_ref, gu_ref, act_ref):
        a = a_ref[...]
        g = jnp.dot(a, bg_ref[...], preferred_element_type=F32)
        u = jnp.dot(a, bu_ref[...], preferred_element_type=F32)
        gu_ref[0] = g
        gu_ref[1] = u
        act_ref[...] = (g * _sigmoid(g) * u).astype(BF16)

    return pl.pallas_call(
        body, name='proj_gate_up', grid=(S // tm, half),
        in_specs=[pl.BlockSpec((tm, D), lambda m, n: (m, 0)),
                  pl.BlockSpec((None, D, tn), lambda m, n: (n // per, 0, n % per)),
                  pl.BlockSpec((None, D, tn), lambda m, n: ((n + half) // per, 0, n % per))],
        out_specs=[pl.BlockSpec((2, tm, tn), lambda m, n: (0, m, n)), pl.BlockSpec((tm, tn), lambda m, n: (m, n))],
        out_shape=[jax.ShapeDtypeStruct((2, S, F), F32), jax.ShapeDtypeStruct((S, F), BF16)],
        compiler_params=_params(('parallel', 'parallel')))(hf, w, w)


def swiglu_bwd(gu, dact, F, after):
    S = gu.shape[1]
    tr = _tile(S, (256, 128))
    tf = _tile(F, (1408, 1024, 512, 256, 128))
    nf = F // tf

    def body(gu_ref, da_ref, after_ref, o_ref):
        g, da = gu_ref[0], da_ref[...]
        sg = _sigmoid(g)
        o_ref[0] = (da * gu_ref[1] * sg * (1.0 + g * (1.0 - sg))).astype(BF16)
        o_ref[1] = (da * g * sg).astype(BF16)

    planes = pl.BlockSpec((2, tr, tf), lambda i, n: (0, i, n))
    return pl.pallas_call(
        body, name='swiglu_bwd', grid=(S // tr, nf),
        in_specs=[planes, pl.BlockSpec((tr, tf), lambda i, n: (i, n)), ANY],
        out_specs=planes, out_shape=jax.ShapeDtypeStruct((2, S, F), BF16),
        compiler_params=_params(('parallel', 'parallel')))(gu, dact, after)


def loss_head(y, target):
    S, D = y.shape
    tr = _tile(S, (256, 128))

    def body(y_ref, t_ref, d_ref, db_ref, l_ref):
        err = y_ref[...] - t_ref[...]
        d = err * (1.0 / D)
        d_ref[...] = d
        db_ref[...] = d.astype(BF16)
        part = jnp.sum(jnp.sum(err * err, axis=1, keepdims=True), axis=0, keepdims=True) * (0.5 / D)
        _acc_out(l_ref, pl.program_id(0) == 0, jnp.broadcast_to(part, (1, LANES)))

    return _rows_call('loss_head', body, S, tr, [(y, _rb(tr, D)), (target, _rb(tr, D))],
                      [((S, D), F32, _rb(tr, D)), ((S, D), BF16, _rb(tr, D)), ((1, LANES), F32, _fb((1, LANES)))])


def _adamw_math(w, gv, m, v):
    mn = ADAM_B1 * m + (1.0 - ADAM_B1) * gv
    vn = ADAM_B2 * v + (1.0 - ADAM_B2) * (gv * gv)
    m_hat = mn / (1.0 - ADAM_B1 ** ADAM_STEP)
    v_hat = vn / (1.0 - ADAM_B2 ** ADAM_STEP)
    return -ADAM_LR * (m_hat / (jnp.sqrt(v_hat) + ADAM_EPS) + ADAM_WD * w), mn, vn


def adamw(name, w, g, m, v):
    R, C = w.shape
    tr = _row_tile(R, C)

    def body(w_ref, g_ref, m_ref, v_ref, d_ref, mo_ref, vo_ref):
        d_ref[...], mo_ref[...], vo_ref[...] = _adamw_math(w_ref[...], g_ref[...], m_ref[...], v_ref[...])

    spec = _rb(tr, C)
    return _rows_call(name, body, R, tr, [(w, spec), (g, spec), (m, spec), (v, spec)], [((R, C), F32, spec)] * 3)


def adamw_halves(name, w, mine, other, m, v, c_idx):
    R, C = w.shape
    hr = R // 2
    tr = _row_tile(hr, C)

    def body(c_ref, w_ref, a_ref, b_ref, m_ref, v_ref, g_ref, d_ref, mo_ref, vo_ref):
        gv = jnp.where(pl.program_id(0) == c_ref[0], a_ref[...], b_ref[...])
        g_ref[...] = gv
        d_ref[...], mo_ref[...], vo_ref[...] = _adamw_math(w_ref[...], gv, m_ref[...], v_ref[...])

    full = pl.BlockSpec((None, tr, C), lambda hh, i, c_ref: (hh, i, 0))
    mine_spec = pl.BlockSpec((tr, C), lambda hh, i, c_ref: (jnp.where(hh == c_ref[0], i, 0), 0))
    other_spec = pl.BlockSpec((tr, C), lambda hh, i, c_ref: (jnp.where(hh == c_ref[0], 0, i), 0))
    outs = pl.pallas_call(
        body, name=name,
        grid_spec=pltpu.PrefetchScalarGridSpec(num_scalar_prefetch=1, grid=(2, hr // tr),
                                               in_specs=[full, mine_spec, other_spec, full, full], out_specs=[full] * 4),
        out_shape=[jax.ShapeDtypeStruct((2, hr, C), F32)] * 4,
        compiler_params=_params(('parallel', 'parallel')))(
            c_idx, w.reshape(2, hr, C), mine, other, m.reshape(2, hr, C), v.reshape(2, hr, C))
    return [o.reshape(R, C) for o in outs]


def _place():
    x, y, c = lax.axis_index('x'), lax.axis_index('y'), lax.axis_index('c')
    return x, y, c, [(1 - x, y), (x, 1 - y), (1 - x, 1 - y)]


def _rcopy(src, dst, ssem, rsem, dev):
    return pltpu.make_async_remote_copy(src_ref=src, dst_ref=dst, send_sem=ssem, recv_sem=rsem, device_id=dev,
                                        device_id_type=MESH)


HBM = pl.BlockSpec(memory_space=pltpu.HBM)
SEM = pl.BlockSpec(memory_space=pltpu.SEMAPHORE)
EFFECT = pltpu.SideEffectType.DATAFLOW_SIDE_EFFECTING


def _in_hbm(a):
    return pltpu.with_memory_space_constraint(a, pltpu.HBM)


def _rows_part(shape, whole, half):
    return pl.ds(0, shape[0]) if whole else pl.ds(half * (shape[0] // 2), shape[0] // 2)


def gather_start(name, shards, whole):
    nT = len(shards)

    def body(*refs):
        srcs, lands = refs[:nT], refs[nT:2 * nT]
        ssem, rsem, token = refs[2 * nT], refs[2 * nT + 1], refs[-1]
        x, y, c, chips = _place()
        for t in range(nT):
            rows = _rows_part(shards[t].shape, whole[t], c)
            for k, (px, py) in enumerate(chips):
                _rcopy(srcs[t].at[rows], lands[t].at[2 * x + y, rows], ssem.at[3 * t + k], rsem.at[3 * t + k],
                       (px, py, c)).start()
        token[...] = jnp.zeros_like(token)

    zones = [lax.empty((N_CHIPS,) + s.shape, s.dtype) for s in shards]
    outs = pl.pallas_call(
        body, name=name,
        out_shape=(pltpu.SemaphoreType.DMA((3 * nT,)), pltpu.SemaphoreType.DMA((3 * nT,)),
                   *[pltpu.HBM(s.shape, s.dtype) for s in shards], *[pltpu.HBM(z.shape, z.dtype) for z in zones],
                   jax.ShapeDtypeStruct((8, LANES), F32)),
        in_specs=[HBM] * (2 * nT), out_specs=(SEM, SEM, *[HBM] * (2 * nT), pl.BlockSpec(memory_space=pltpu.VMEM)),
        input_output_aliases={i: 2 + i for i in range(2 * nT)},
        compiler_params=pltpu.CompilerParams(has_side_effects=EFFECT))(*[_in_hbm(a) for a in list(shards) + zones])
    return outs[0], outs[1], outs[2:2 + nT], outs[2 + nT:2 + 2 * nT], outs[-1]


def gather_wait(name, t, shard, zone, ssem, rsem, after, whole):
    after = after if isinstance(after, (list, tuple)) else [after]

    def body(src_ref, land_ref, ssem_ref, rsem_ref, *rest):
        x, y, c, chips = _place()
        rows = _rows_part(shard.shape, whole, c)
        for k, (px, py) in enumerate(chips):
            cp = _rcopy(src_ref.at[rows], land_ref.at[2 * px + py, rows], ssem_ref.at[3 * t + k], rsem_ref.at[3 * t + k],
                        (px, py, c))
            cp.wait_send()
            cp.wait_recv()

    return pl.pallas_call(
        body, name=name, out_shape=(pltpu.HBM(shard.shape, shard.dtype), pltpu.HBM(zone.shape, zone.dtype)),
        in_specs=(HBM, HBM, SEM, SEM, *[ANY] * len(after)), out_specs=(HBM, HBM), input_output_aliases={0: 0, 1: 1},
        compiler_params=pltpu.CompilerParams(has_side_effects=EFFECT))(shard, zone, ssem, rsem, *after)


def pair_swap(name, zone):
    hr = zone.shape[1] // 2

    def body(z_in, z_ref, ssem, rsem):
        x, y, c, chips = _place()
        cps = []
        for k, (px, py) in enumerate(chips):
            blk = z_ref.at[2 * px + py, pl.ds(c * hr, hr)]
            cps.append(_rcopy(blk, blk, ssem.at[k], rsem.at[k], (x, y, 1 - c)))
            cps[-1].start()
        for k, (px, py) in enumerate(chips):
            blk = z_ref.at[2 * px + py, pl.ds((1 - c) * hr, hr)]
            _rcopy(blk, blk, ssem.at[k], rsem.at[k], (x, y, 1 - c)).wait_recv()
        for cp in cps:
            cp.wait_send()

    return pl.pallas_call(
        body, name=name, in_specs=[ANY], out_specs=ANY, out_shape=jax.ShapeDtypeStruct(zone.shape, zone.dtype),
        input_output_aliases={0: 0},
        scratch_shapes=[pltpu.SemaphoreType.DMA((3,)), pltpu.SemaphoreType.DMA((3,))],
        compiler_params=_params())(zone)


N_SENDERS = 7


def _scatter_copies(g_ref, l_ref, ssem, rsem):
    x, y, c, chips = _place()
    cps = []
    for k, (px, py) in enumerate(chips):
        for d in range(2):
            to = (c + d) % 2
            cps.append(_rcopy(g_ref.at[2 * px + py, to], l_ref.at[2 * k + d], ssem.at[2 * k + d], rsem.at[2 * k + d],
                              (px, py, to)))
    cps.append(_rcopy(g_ref.at[2 * x + y, 1 - c], l_ref.at[6], ssem.at[6], rsem.at[6], (x, y, 1 - c)))
    return cps


def scatter_start(name, g):
    def body(g_ref, l_ref, ssem, rsem, g_out, l_out, token):
        for cp in _scatter_copies(g_ref, l_ref, ssem, rsem):
            cp.start()
        token[...] = jnp.zeros_like(token)

    zone = lax.empty((N_SENDERS,) + g.shape[2:], g.dtype)
    return pl.pallas_call(
        body, name=name,
        out_shape=(pltpu.SemaphoreType.DMA((N_SENDERS,)), pltpu.SemaphoreType.DMA((N_SENDERS,)),
                   pltpu.HBM(g.shape, g.dtype), pltpu.HBM(zone.shape, zone.dtype), jax.ShapeDtypeStruct((8, LANES), F32)),
        in_specs=[HBM, HBM], out_specs=(SEM, SEM, HBM, HBM, pl.BlockSpec(memory_space=pltpu.VMEM)),
        input_output_aliases={0: 2, 1: 3},
        compiler_params=pltpu.CompilerParams(has_side_effects=EFFECT))(_in_hbm(g), _in_hbm(zone))


def scatter_wait(name, g, zone, ssem, rsem, after):
    def body(g_ref, l_ref, ssem_ref, rsem_ref, after_ref, g_out, l_out):
        for cp in _scatter_copies(g_ref, l_ref, ssem_ref, rsem_ref):
            cp.wait_send()
            cp.wait_recv()

    return pl.pallas_call(
        body, name=name, out_shape=(pltpu.HBM(g.shape, g.dtype), pltpu.HBM(zone.shape, zone.dtype)),
        in_specs=(HBM, HBM, SEM, SEM, ANY), out_specs=(HBM, HBM), input_output_aliases={0: 0, 1: 1},
        compiler_params=pltpu.CompilerParams(has_side_effects=EFFECT))(g, zone, ssem, rsem, after)


def sum_parts(name, g, landed, chip_idx, c_idx):
    hr, C = g.shape[2:]
    tr = _row_tile(hr, C, min_rows=16)

    def body(me_ref, c_ref, g_ref, l_ref, o_ref):
        acc = g_ref[...].astype(F32)
        for s in range(N_SENDERS):
            acc = acc + l_ref[s].astype(F32)
        o_ref[...] = acc

    return pl.pallas_call(
        body, name=name,
        grid_spec=pltpu.PrefetchScalarGridSpec(
            num_scalar_prefetch=2, grid=(hr // tr,),
            in_specs=[pl.BlockSpec((None, None, tr, C), lambda i, me_ref, c_ref: (me_ref[0], c_ref[0], i, 0)),
                      pl.BlockSpec((N_SENDERS, tr, C), lambda i, me_ref, c_ref: (0, i, 0))],
            out_specs=pl.BlockSpec((tr, C), lambda i, me_ref, c_ref: (i, 0))),
        out_shape=jax.ShapeDtypeStruct((hr, C), F32),
        compiler_params=_params(('parallel',)))(chip_idx, c_idx, g, landed)


def pair_join(name, halves):
    nT = len(halves)

    def body(*refs):
        ins, outs = refs[:nT], refs[nT:2 * nT]
        ssem, rsem = refs[2 * nT:]
        x, y, c, _ = _place()
        cps = [_rcopy(ins[t], outs[t], ssem.at[t], rsem.at[t], (x, y, 1 - c)) for t in range(nT)]
        for cp in cps:
            cp.start()
        for cp in cps:
            cp.wait()

    return pl.pallas_call(
        body, name=name, in_specs=[ANY] * nT, out_specs=[ANY] * nT,
        out_shape=[jax.ShapeDtypeStruct(h.shape, h.dtype) for h in halves],
        scratch_shapes=[pltpu.SemaphoreType.DMA((nT,)), pltpu.SemaphoreType.DMA((nT,))],
        compiler_params=_params())(*halves)


N_DEVICES = 8


def _spread_copies(b_ref, l_ref, ssem, rsem):
    x, y, c, chips = _place()
    me = 4 * x + 2 * y + c
    pairs = []
    for px, py, pc in [(px, py, pc) for px, py in chips for pc in (c, 1 - c)] + [(x, y, 1 - c)]:
        it = 4 * px + 2 * py + pc
        pairs.append((_rcopy(b_ref, l_ref.at[me], ssem.at[it], rsem.at[me], (px, py, pc)),
                      _rcopy(b_ref, l_ref.at[it], ssem.at[it], rsem.at[it], (px, py, pc))))
    return pairs


def spread_start(name, buf):
    def body(b_ref, l_ref, ssem, rsem, b_out, l_out, token):
        for mine, _ in _spread_copies(b_ref, l_ref, ssem, rsem):
            mine.start()
        token[...] = jnp.zeros_like(token)

    zone = lax.empty((N_DEVICES,) + buf.shape, buf.dtype)
    return pl.pallas_call(
        body, name=name,
        out_shape=(pltpu.SemaphoreType.DMA((N_DEVICES,)), pltpu.SemaphoreType.DMA((N_DEVICES,)),
                   pltpu.HBM(buf.shape, buf.dtype), pltpu.HBM(zone.shape, zone.dtype), jax.ShapeDtypeStruct((8, LANES), F32)),
        in_specs=[HBM, HBM], out_specs=(SEM, SEM, HBM, HBM, pl.BlockSpec(memory_space=pltpu.VMEM)),
        input_output_aliases={0: 2, 1: 3},
        compiler_params=pltpu.CompilerParams(has_side_effects=EFFECT))(_in_hbm(buf), _in_hbm(zone))


def spread_wait(name, buf, zone, ssem, rsem, after):
    def body(b_ref, l_ref, ssem_ref, rsem_ref, after_ref, b_out, l_out):
        for mine, theirs in _spread_copies(b_ref, l_ref, ssem_ref, rsem_ref):
            mine.wait_send()
            theirs.wait_recv()

    return pl.pallas_call(
        body, name=name, out_shape=(pltpu.HBM(buf.shape, buf.dtype), pltpu.HBM(zone.shape, zone.dtype)),
        in_specs=(HBM, HBM, SEM, SEM, ANY), out_specs=(HBM, HBM), input_output_aliases={0: 0, 1: 1},
        compiler_params=pltpu.CompilerParams(has_side_effects=EFFECT))(buf, zone, ssem, rsem, after)


def sum_devices(name, zone):
    _, R, C = zone.shape
    tr = _row_tile(R, C)

    def body(z_ref, o_ref):
        acc = z_ref[0]
        for d in range(1, N_DEVICES):
            acc = acc + z_ref[d]
        o_ref[...] = acc

    return pl.pallas_call(
        body, name=name, grid=(R // tr,), in_specs=[pl.BlockSpec((N_DEVICES, tr, C), lambda i: (0, i, 0))],
        out_specs=pl.BlockSpec((tr, C), lambda i: (i, 0)), out_shape=jax.ShapeDtypeStruct((R, C), F32),
        compiler_params=_params(('parallel',)))(zone)


class _InWindows:
    def __init__(self, FW, LW, H, C):
        gap = LANES - H
        padded = lambda o: o if o < 3 * FW + H else o + gap
        self.width = 3 * FW + LANES + 2 * LW
        self.first = [padded(C * j) // LANES for j in range(N_CHIPS)]
        self.blocks = max(padded(C * (j + 1) - 1) // LANES - self.first[j] + 1 for j in range(N_CHIPS))
        assert all((b + self.blocks) * LANES <= self.width for b in self.first)
        self.cols = self.blocks * LANES
        self.runs = []
        for j in range(N_CHIPS):
            cut = min(max(3 * FW + H - C * j, 0), C)
            spans = [(0, cut), (cut, C)]
            self.runs.append([(t0, t1, padded(C * j + t0) - LANES * self.first[j]) for t0, t1 in spans if t1 > t0])

    def to_window(self, shard, chip):
        def place(j, s):
            parts, pos = [], 0
            for t0, t1, w0 in self.runs[j]:
                parts += [jnp.zeros((s.shape[0], w0 - pos), s.dtype), s[:, t0:t1]]
                pos = w0 + t1 - t0
            parts.append(jnp.zeros((s.shape[0], self.cols - pos), s.dtype))
            return jnp.concatenate([p for p in parts if p.shape[1]], axis=1)
        return lax.switch(chip, [functools.partial(place, j) for j in range(N_CHIPS)], shard)

    def from_window(self, win, chip):
        def take(j, w):
            return jnp.concatenate([w[:, w0:w0 + t1 - t0] for t0, t1, w0 in self.runs[j]], axis=1)
        return lax.switch(chip, [functools.partial(take, j) for j in range(N_CHIPS)], win)

    def assemble(self, zone):
        total = None
        for j in range(N_CHIPS):
            lead = self.first[j] * LANES
            part = jnp.pad(zone[j], ((0, 0), (lead, self.width - lead - self.cols)))
            total = part if total is None else total + part
        return total

    def windows(self, padded_matrix):
        return jnp.stack([padded_matrix[:, b * LANES:b * LANES + self.cols] for b in self.first])


_PACK = 8 * LANES


PACK_ROWS = 256


def _pack(arrs):
    flat = []
    for a in arrs:
        v = a.reshape(-1).astype(F32)
        flat.append(jnp.pad(v, (0, (-v.shape[0]) % _PACK)))
    rows = sum(v.shape[0] for v in flat) // LANES
    flat.append(jnp.zeros(((-rows) % PACK_ROWS) * LANES, F32))
    return jnp.concatenate(flat).reshape(-1, LANES)


def _unpack(buf, shapes):
    out, off = [], 0
    flat = buf.reshape(-1)
    for sh in shapes:
        n = math.prod(sh)
        out.append(flat[off:off + n].reshape(sh))
        off += n + (-n) % _PACK
    return out


def kernel(x, mem, g_mix, w_in, b_f, g_q, g_k, conv_w, conv_b, w_ra, b_ra, w_ri, b_ri, lam, g_fox_out, g_lru_out, w_out, g_xattn, g_mem, w_cq, w_ckv, g_cq, g_ck, w_co, g_ffn, w_gate_up, w_down, loss_target, m_g_mix, m_w_in, m_b_f, m_g_q, m_g_k, m_conv_w, m_conv_b, m_w_ra, m_b_ra, m_w_ri, m_b_ri, m_lam, m_g_fox_out, m_g_lru_out, m_w_out, m_g_xattn, m_g_mem, m_w_cq, m_w_ckv, m_g_cq, m_g_ck, m_w_co, m_g_ffn, m_w_gate_up, m_w_down, v_g_mix, v_w_in, v_b_f, v_g_q, v_g_k, v_conv_w, v_conv_b, v_w_ra, v_b_ra, v_w_ri, v_b_ri, v_lam, v_g_fox_out, v_g_lru_out, v_w_out, v_g_xattn, v_g_mem, v_w_cq, v_w_ckv, v_g_cq, v_g_ck, v_w_co, v_g_ffn, v_w_gate_up, v_w_down):
    given = dict(locals())
    W = {n: given[n][0] for n in WEIGHTS}
    M1 = {n: given['m_' + n][0] for n in WEIGHTS}
    V1 = {n: given['v_' + n][0] for n in WEIGHTS}
    xs, ms, tgt = x[0], mem[0], loss_target[0]
    S, D = xs.shape
    H = W['b_f'].shape[0]
    FW = H * HEAD_DIM
    LW = W['lam'].shape[0]
    nb = W['w_ra'].shape[0]
    XW = W['w_cq'].shape[1]
    F = W['w_down'].shape[0] * N_CHIPS
    IN_W = W['w_in'].shape[1] * N_CHIPS
    assert FW == LW and LW == nb * LANES and IN_W == 3 * FW + H + 2 * LW and H <= 8
    T = _tile(S, (512, 256, 128))
    c_idx = lax.axis_index('c').astype(jnp.int32).reshape(1)
    chip = 2 * lax.axis_index('x') + lax.axis_index('y')
    chip_idx = chip.astype(jnp.int32).reshape(1)
    vec = lambda n: W[n].reshape(1, -1)

    wins = _InWindows(FW, LW, H, W['w_in'].shape[1])
    started = {}
    g_tok = jnp.zeros((1, 1), F32)
    for call, names in (('gather_start_first', ['conv_w', 'w_in']), ('gather_start_rest', BIG[1:])):
        own = [W[n].reshape(-1, LANES) if n == 'conv_w' else W[n].astype(BF16) + g_tok.astype(BF16) for n in names]
        own = [wins.to_window(o, chip) if n == 'w_in' else o for n, o in zip(names, own)]
        ssem, rsem, srcs, zones, tok = gather_start(call, own, [n == 'conv_w' for n in names])
        g_tok = tok[0:1, 0:1]
        started.update({n: (t, srcs[t], zones[t], ssem, rsem) for t, n in enumerate(names)})

    def fetch(n, after):
        t, g_src, g_zone, g_ssem, g_rsem = started[n]
        src, zone = gather_wait('gather_wait_' + n, t, g_src, g_zone, g_ssem, g_rsem, after, n == 'conv_w')
        if n != 'conv_w':
            zone = pair_swap('pair_swap_' + n, zone)
        return lax.dynamic_update_index_in_dim(zone, src, chip, 0)

    b_f_pad = jnp.pad(vec('b_f'), ((0, 0), (0, LANES - H)))
    u_off, g_off = 3 * FW // LANES, (3 * FW + LW) // LANES

    h1 = norm_fwd('norm_mix', xs, vec('g_mix') + g_tok[0:1, 0:1])
    conv_full = fetch('conv_w', h1).reshape(N_CHIPS, CONV_W, LW // N_CHIPS).transpose(1, 0, 2).reshape(CONV_W, LW)
    w_in_pad = wins.assemble(fetch('w_in', [h1, M1['w_in'], V1['w_in']]))
    w5 = jnp.concatenate([w_in_pad[:, :3 * FW], w_in_pad[:, 3 * FW + LANES:]], axis=1)
    wf = w_in_pad[:, 3 * FW:3 * FW + LANES]
    proj = _mm('proj_in', h1, w5, 'nn', F32)
    f_raw = _mm('proj_f', h1, wf, 'nn', F32)
    qn, kn, vb = qkv_fwd(proj, vec('g_q'), vec('g_k'), FW)
    cc = fgate_fwd(f_raw, b_f_pad)
    ct = cc[:, :8].T
    o_fox, lse = fox_fwd(qn, kn, vb, cc, ct, T)
    lru_w = (conv_full, vec('conv_b'), W['w_ra'], vec('b_ra'), W['w_ri'], vec('b_ri'), vec('lam'))
    y_lru = lru_fwd(proj, *lru_w, u_off, g_off)
    mixn = mix_fwd(o_fox, y_lru, vec('g_fox_out'), vec('g_lru_out'))
    w_out_f = fetch('w_out', mixn).reshape(2 * FW, D)
    x1 = _mm('proj_out', mixn, w_out_f, 'nn', F32, res=xs)

    hq = norm_fwd('norm_xq', x1, vec('g_xattn'))
    mn = norm_fwd('norm_mem', ms, vec('g_mem'))
    w_cq_f = fetch('w_cq', hq).reshape(D, XW)
    w_ckv_f = fetch('w_ckv', hq).reshape(D, 2 * XW)
    cq_raw = _mm('proj_cq', hq, w_cq_f, 'nn', F32)
    ckv = _mm('proj_ckv', mn, w_ckv_f, 'nn', F32)
    o_x = xattn_fwd(cq_raw, ckv, vec('g_cq'), vec('g_ck'))
    w_co_g = fetch('w_co', o_x)
    x2 = _mm_colsharded('proj_co', o_x, w_co_g, F32, res=x1)

    hf = norm_fwd('norm_ffn', x2, vec('g_ffn'))
    w_gu_g = fetch('w_gate_up', hf)
    gu, act = gate_up_fwd(hf, w_gu_g, F)
    w_down_f = fetch('w_down', act).reshape(F, D)
    yv = _mm('proj_down', act, w_down_f, 'nn', F32, res=x2)
    dy, dyb, loss_blk = loss_head(yv, tgt)

    gw, pending = {}, []

    def reduce_begin(n, g):
        sp = g.reshape(N_CHIPS, 2, g.shape[1] // 2, g.shape[2])
        ssem, rsem, sp, zone, tok = scatter_start('scatter_start_' + n, sp)
        pending.append((n, sp, zone, ssem, rsem))
        return tok[0:1, 0:1]

    dact = _mm('bwd_down_x', dyb, w_down_f, 'nt', F32)
    t_down = reduce_begin('w_down', _mm('bwd_down_w', act, dyb, 'tn', BF16).reshape(N_CHIPS, F // N_CHIPS, D))
    dgu = swiglu_bwd(gu, dact, F, t_down)
    dhf = _mm_colsharded_t('bwd_gate_up_x', dgu, w_gu_g, F32)
    t_gu = reduce_begin('w_gate_up', _mm_grad_colsharded('bwd_gate_up_w', hf, dgu, N_CHIPS, BF16))
    dx2, dx2b, gw['g_ffn'] = norm_bwd('norm_ffn_bwd', x2, vec('g_ffn') + t_down + t_gu, dhf, res=dy)

    do_x = _mm_colsharded_t('bwd_co_x', dx2b, w_co_g, BF16)
    t_co = reduce_begin('w_co', _mm_grad_colsharded('bwd_co_w', o_x, dx2b, N_CHIPS, BF16))
    dcq_raw, dckv, gw['g_cq'], gw['g_ck'] = xattn_bwd(cq_raw, ckv, vec('g_cq') + t_co, vec('g_ck'), do_x)
    dhq = _mm('bwd_cq_x', dcq_raw, w_cq_f, 'nt', F32)
    t_cq = reduce_begin('w_cq', _mm('bwd_cq_w', hq, dcq_raw, 'tn', BF16).reshape(N_CHIPS, D // N_CHIPS, XW))
    dmn = _mm('bwd_ckv_x', dckv, w_ckv_f, 'nt', F32)
    t_ckv = reduce_begin('w_ckv', _mm('bwd_ckv_w', mn, dckv, 'tn', BF16).reshape(N_CHIPS, D // N_CHIPS, 2 * XW))
    (gw['g_mem'],) = norm_bwd('norm_mem_bwd', ms, vec('g_mem'), dmn, want_dx=False)
    dx1, dx1b, gw['g_xattn'] = norm_bwd('norm_xq_bwd', x1, vec('g_xattn') + t_cq + t_ckv, dhq, res=dx2)

    dmix = _mm('bwd_out_x', dx1b, w_out_f, 'nt', F32)
    t_out = reduce_begin('w_out', _mm('bwd_out_w', mixn, dx1b, 'tn', BF16).reshape(N_CHIPS, 2 * FW // N_CHIPS, D))
    do_fox, delta, dy_lru, gw['g_fox_out'], gw['g_lru_out'] = mix_bwd(o_fox, y_lru, vec('g_fox_out') + t_out,
                                                                     vec('g_lru_out'), dmix)
    (du, dgate, gw['conv_w'], gw['conv_b'], gw['w_ra'], gw['b_ra'], gw['w_ri'], gw['b_ri'],
     gw['lam']) = lru_bwd(proj, dy_lru, *lru_w, u_off, g_off)
    early = [n for n in SMALL if n not in ('g_q', 'g_k', 'b_f', 'g_mix')]
    late = [n for n in SMALL if n not in early]
    e_ssem, e_rsem, e_buf, e_zone, e_tok = spread_start('spread_start_early', _pack([gw[n] for n in early]))
    dqn, delta2 = fox_bwd_q(qn, kn, vb, do_fox, cc, ct, lse, delta, T)
    dkn, dv, dct = fox_bwd_kv(qn, kn, vb, do_fox, cc, ct, lse, delta2, T)
    dq, dk, gw['g_q'], gw['g_k'] = qkv_bwd(proj, vec('g_q') + e_tok[0:1, 0:1], vec('g_k'), dqn, dkn, FW)
    dc = jnp.pad(dct.reshape(H, S).T, ((0, 0), (0, LANES - H)))
    df, db_f = fgate_bwd(f_raw, b_f_pad, dc, H)
    gw['b_f'] = db_f[:, :H]
    dproj = jnp.concatenate([dq, dk, dv, du, dgate], axis=1)
    dw5 = _mm('bwd_in_w', h1, dproj, 'tn', BF16)
    dwf = _mm('bwd_f_w', h1, df, 'tn', BF16)
    t_in = reduce_begin('w_in', wins.windows(jnp.concatenate([dw5[:, :3 * FW], dwf, dw5[:, 3 * FW:]], axis=1)))
    dh_a = _mm('bwd_f_x', df, wf, 'nt', F32)
    dh1 = _mm('bwd_in_x', dproj, w5, 'nt', F32, res=dh_a)
    grad_x, _, gw['g_mix'] = norm_bwd('norm_mix_bwd', xs, vec('g_mix') + t_in, dh1, res=dx1)
    l_ssem, l_rsem, l_buf, l_zone, _ = spread_start('spread_start_late',
                                                    _pack([gw[n] for n in late] + [loss_blk[0:1, 0:1]]))

    grads, delta_w, new_m, new_v = {}, {}, {}, {}
    done = grad_x
    for n, part, zone, ssem, rsem in pending:
        part, landed = scatter_wait('scatter_wait_' + n, part, zone, ssem, rsem, done)
        mine = sum_parts('sum_parts_' + n, part, landed, chip_idx, c_idx)
        (other,) = pair_join('pair_join_' + n, [mine])
        if n == 'w_in':
            mine, other = wins.from_window(mine, chip), wins.from_window(other, chip)
        grads[n], delta_w[n], new_m[n], new_v[n] = adamw_halves('adamw_' + n, W[n], mine, other, M1[n], V1[n], c_idx)
        done = delta_w[n]

    device = 4 * lax.axis_index('x') + 2 * lax.axis_index('y') + lax.axis_index('c')
    summed = {}
    for tag, names, buf, zone, ssem, rsem in (('early', early, e_buf, e_zone, e_ssem, e_rsem),
                                              ('late', late + ['loss'], l_buf, l_zone, l_ssem, l_rsem)):
        buf, zone = spread_wait('spread_wait_' + tag, buf, zone, ssem, rsem, done)
        total = sum_devices('sum_small_' + tag, lax.dynamic_update_index_in_dim(zone, buf, device, 0))
        summed.update(zip(names, _unpack(total, [gw[n].shape if n != 'loss' else (1, 1) for n in names])))
    loss = summed['loss'].reshape(())
    for n in SMALL:
        g = summed[n]
        grads[n] = g.reshape(W[n].shape) if n != 'conv_w' else lax.dynamic_slice_in_dim(
            g, chip * (LW // N_CHIPS), LW // N_CHIPS, axis=1)
    packs = [_pack([d[n] for n in SMALL]) for d in (W, grads, M1, V1)]
    shapes = [W[n].shape for n in SMALL]
    for d, res in zip((delta_w, new_m, new_v), adamw('adamw_small', *packs)):
        d.update(zip(SMALL, _unpack(res, shapes)))

    lead = lambda d: [d[n][None] for n in WEIGHTS]
    return (loss, grad_x[None], *lead(grads), *lead(delta_w), *lead(new_m), *lead(new_v))
```

```python
import functools
import math

import jax
import jax.numpy as jnp
from jax import lax
from jax.experimental import pallas as pl
from jax.experimental.pallas import tpu as pltpu

F32 = jnp.float32
BF16 = jnp.bfloat16
HEAD_DIM = 128
LANES = 128
LRU_C = 8.0
RMS_EPS = 1e-6
CONV_W = 4
ADAM_LR = 0.001
ADAM_B1 = 0.9
ADAM_B2 = 0.999
ADAM_EPS = 1e-08
ADAM_WD = 0.01
ADAM_STEP = 10
VMEM_LIMIT = 56 * 1024 * 1024
N_CHIPS = 4
MESH = pl.DeviceIdType.MESH
ANY = pl.BlockSpec(memory_space=pl.ANY)

WEIGHTS = ['g_mix', 'w_in', 'b_f', 'g_q', 'g_k', 'conv_w', 'conv_b', 'w_ra', 'b_ra', 'w_ri', 'b_ri', 'lam',
           'g_fox_out', 'g_lru_out', 'w_out', 'g_xattn', 'g_mem', 'w_cq', 'w_ckv', 'g_cq', 'g_ck', 'w_co', 'g_ffn',
           'w_gate_up', 'w_down']
BIG = ['w_in', 'w_out', 'w_cq', 'w_ckv', 'w_co', 'w_gate_up', 'w_down']
SMALL = [n for n in WEIGHTS if n not in BIG]


def _params(sem=None):
    if sem is None:
        return pltpu.CompilerParams(vmem_limit_bytes=VMEM_LIMIT)
    return pltpu.CompilerParams(dimension_semantics=sem, vmem_limit_bytes=VMEM_LIMIT)


def _tile(n, cands):
    for t in cands:
        if n % t == 0:
            return t
    return n


ROW_BLOCK_BYTES = 1 << 20


def _row_tile(n_rows, n_cols, min_rows=8):
    cands = [t for t in (512, 256, 128, 64, 32, 16, 8) if t >= min_rows and t * n_cols * 4 <= ROW_BLOCK_BYTES]
    return _tile(n_rows, cands or [min_rows])


def _sigmoid(z):
    return 1.0 / (1.0 + jnp.exp(-z))


def _softplus(z):
    return jnp.maximum(z, 0.0) + jnp.log(1.0 + jnp.exp(-jnp.abs(z)))


def _neg_expm1(z):
    series = -z * (1.0 + z * (0.5 + z * (1.0 / 6.0 + z * (1.0 / 24.0 + z * (1.0 / 120.0)))))
    return jnp.where(z > -0.25, series, 1.0 - jnp.exp(z))


_GELU_K = math.sqrt(2.0 / math.pi)


def _gelu_and_grad(z):
    inner = _GELU_K * (z + 0.044715 * z * z * z)
    t = jnp.tanh(inner)
    g = 0.5 * z * (1.0 + t)
    dg = 0.5 * (1.0 + t) + 0.5 * z * (1.0 - t * t) * _GELU_K * (1.0 + 3.0 * 0.044715 * z * z)
    return g, dg


def _rms(xv, g):
    r = lax.rsqrt(jnp.mean(xv * xv, axis=-1, keepdims=True) + RMS_EPS)
    return xv * r * g


def _rms_bwd(xv, g, dy):
    r = lax.rsqrt(jnp.mean(xv * xv, axis=-1, keepdims=True) + RMS_EPS)
    xh = xv * r
    dyg = dy * g
    dx = r * (dyg - xh * jnp.mean(dyg * xh, axis=-1, keepdims=True))
    return dx, jnp.sum(dy * xh, axis=0, keepdims=True)


def _heads(fn, n_heads, *arrs):
    outs = [fn(*[a[:, h * HEAD_DIM:(h + 1) * HEAD_DIM] for a in arrs]) for h in range(n_heads)]
    first = jnp.concatenate([o[0] for o in outs], axis=1) if n_heads > 1 else outs[0][0]
    rest = [functools.reduce(lambda p, q: p + q, [o[i] for o in outs]) for i in range(1, len(outs[0]))]
    return (first, *rest)


def _split3(v):
    hi = v.astype(BF16)
    r1 = v - hi.astype(F32)
    mid = r1.astype(BF16)
    lo = (r1 - mid.astype(F32)).astype(BF16)
    return hi, mid, lo


def _acc_out(ref, first, val):
    @pl.when(first)
    def _():
        ref[...] = val

    @pl.when(jnp.logical_not(first))
    def _():
        ref[...] += val


_DIMS = {'nn': (((1,), (0,)), ((), ())), 'nt': (((1,), (1,)), ((), ())), 'tn': (((0,), (0,)), ((), ()))}


MM_VMEM_BYTES = 36 * 1024 * 1024


MXU_FLOPS = 800e12
HBM_BYTES_S = 3.2e12
VMEM_ADD_BYTES_S = 8e12
STEP_S = 0.35e-6


def _k_tile(K, tm, tn, a, b, o_dtype, res):
    fixed = tm * tn * (2 * jnp.dtype(o_dtype).itemsize + 4 + (8 if res is not None else 0))
    per_k = 2 * (tm * a.dtype.itemsize + tn * b.dtype.itemsize)
    per_k += 2 * tm * (a.dtype.itemsize > 2) + 2 * tn * (b.dtype.itemsize > 2)
    units = K // LANES
    for d in sorted((d for d in range(1, units + 1) if units % d == 0), reverse=True):
        if fixed + d * LANES * per_k <= MM_VMEM_BYTES:
            return d * LANES
    return None


def _mm_tiles(M, N, K, k_span, a, b, o_dtype, res, tn_cands=(2048, 1024, 512, 256, 128)):
    best = None
    for tm in (2048, 1024, 512, 256, 128):
        for tn in tn_cands:
            if M % tm or N % tn:
                continue
            tk = _k_tile(k_span, tm, tn, a, b, o_dtype, res)
            if tk is None:
                continue
            nk = K // tk
            traffic = (M * K * a.dtype.itemsize * (N // tn) + K * N * b.dtype.itemsize * (M // tm)
                       + M * N * (jnp.dtype(o_dtype).itemsize + (4 if res is not None else 0)))
            work = 2.0 * M * N * K / MXU_FLOPS + (M * N * 4 * nk / VMEM_ADD_BYTES_S if nk > 1 else 0.0)
            t = max(work, traffic / HBM_BYTES_S) + (M // tm) * (N // tn) * nk * STEP_S
            if best is None or t < best[0]:
                best = (t, tm, tn, tk)
    assert best is not None, (M, N, K)
    return best[1:]


def _mm_call(name, a, b, mode, grid, a_spec, b_spec, o_spec, o_shape, o_dtype, acc_shape, res=None):
    nk = grid[2]
    dn = _DIMS[mode]

    def body(*refs):
        a_ref, b_ref = refs[:2]
        r_ref = refs[2] if res is not None else None
        o_ref = refs[3] if res is not None else refs[2]
        part = lax.dot_general(a_ref[...].astype(BF16), b_ref[...].astype(BF16), dn, preferred_element_type=F32)

        def finish(r):
            if r_ref is not None:
                r = r + r_ref[...]
            o_ref[...] = r.astype(o_dtype)

        if nk == 1:
            finish(part)
            return
        acc = refs[-1]
        k = pl.program_id(2)

        @pl.when(k == 0)
        def _():
            acc[...] = part

        @pl.when(k > 0)
        def _():
            acc[...] += part

        @pl.when(k == nk - 1)
        def _():
            finish(acc[...])

    ins = [a, b] + ([] if res is None else [res])
    specs = [a_spec, b_spec] + ([] if res is None else [o_spec])
    return pl.pallas_call(
        body, name=name, grid=grid, in_specs=specs, out_specs=o_spec,
        out_shape=jax.ShapeDtypeStruct(o_shape, o_dtype),
        scratch_shapes=[] if nk == 1 else [pltpu.VMEM(acc_shape, F32)],
        compiler_params=_params(('parallel', 'parallel', 'arbitrary')))(*ins)


def _mm(name, a, b, mode, o_dtype, res=None):
    if mode == 'tn':
        K, M = a.shape
    else:
        M, K = a.shape
    N = b.shape[0] if mode == 'nt' else b.shape[1]
    tm, tn, tk = _mm_tiles(M, N, K, K, a, b, o_dtype, res)
    a_spec = (pl.BlockSpec((tk, tm), lambda m, n, k: (k, m)) if mode == 'tn'
              else pl.BlockSpec((tm, tk), lambda m, n, k: (m, k)))
    b_spec = (pl.BlockSpec((tn, tk), lambda m, n, k: (n, k)) if mode == 'nt'
              else pl.BlockSpec((tk, tn), lambda m, n, k: (k, n)))
    o_spec = pl.BlockSpec((tm, tn), lambda m, n, k: (m, n))
    return _mm_call(name, a, b, mode, (M // tm, N // tn, K // tk), a_spec, b_spec, o_spec, (M, N), o_dtype,
                    (tm, tn), res)


def _mm_colsharded(name, a, w, o_dtype, res=None):
    M, K = a.shape
    J, _, Nj = w.shape
    tm, tn, tk = _mm_tiles(M, J * Nj, K, K, a, w, o_dtype, res,
                           tn_cands=[t for t in (2816, 1408, 1024, 512, 256, 128) if Nj % t == 0])
    per = Nj // tn
    return _mm_call(name, a, w, 'nn', (M // tm, J * per, K // tk),
                    pl.BlockSpec((tm, tk), lambda m, n, k: (m, k)),
                    pl.BlockSpec((None, tk, tn), lambda m, n, k: (n // per, k, n % per)),
                    pl.BlockSpec((tm, tn), lambda m, n, k: (m, n)), (M, J * Nj), o_dtype, (tm, tn), res)


def _planes_spec(arr, rows, cols, row_of, col_of):
    if arr.ndim == 2:
        return pl.BlockSpec((rows, cols), lambda m, n, k: (row_of(m, n, k), col_of(m, n, k)))
    per_plane = arr.shape[2] // cols
    return pl.BlockSpec((None, rows, cols),
                        lambda m, n, k: (col_of(m, n, k) // per_plane, row_of(m, n, k), col_of(m, n, k) % per_plane))


def _mm_colsharded_t(name, a, w, o_dtype):
    M = a.shape[-2]
    J, K, Nj = w.shape
    tm, tn, tk = _mm_tiles(M, K, J * Nj, Nj, a, w, o_dtype, None)
    per = Nj // tk
    return _mm_call(name, a, w, 'nt', (M // tm, K // tn, J * per),
                    _planes_spec(a, tm, tk, lambda m, n, k: m, lambda m, n, k: k),
                    pl.BlockSpec((None, tn, tk), lambda m, n, k: (k // per, n, k % per)),
                    pl.BlockSpec((tm, tn), lambda m, n, k: (m, n)), (M, K), o_dtype, (tm, tn))


def _mm_grad_colsharded(name, a, dy, J, o_dtype):
    S, M = a.shape
    Nj = dy.shape[-1] * (dy.shape[0] if dy.ndim == 3 else 1) // J
    tm, tn, tk = _mm_tiles(M, J * Nj, S, S, a, dy, o_dtype, None,
                           tn_cands=[t for t in (2816, 1408, 1024, 512, 256, 128) if Nj % t == 0])
    per = Nj // tn
    return _mm_call(name, a, dy, 'tn', (M // tm, J * per, S // tk),
                    pl.BlockSpec((tk, tm), lambda m, n, k: (k, m)),
                    _planes_spec(dy, tk, tn, lambda m, n, k: k, lambda m, n, k: n),
                    pl.BlockSpec((None, tm, tn), lambda m, n, k: (n // per, m, n % per)), (J, M, Nj), o_dtype, (tm, tn))


def _rows_call(name, body, n_rows, tr, ins, outs):
    return pl.pallas_call(
        body, name=name, grid=(n_rows // tr,), in_specs=[s for _, s in ins], out_specs=[s for _, _, s in outs],
        out_shape=[jax.ShapeDtypeStruct(sh, dt) for sh, dt, _ in outs],
        compiler_params=_params(('arbitrary',)))(*[a for a, _ in ins])


def _rb(tr, w, cb=0):
    return pl.BlockSpec((tr, w), lambda i: (i, cb))


def _fb(shape):
    nd = len(shape)
    return pl.BlockSpec(shape, lambda i: (0,) * nd)


def norm_fwd(name, xv, g):
    S, D = xv.shape
    tr = _tile(S, (256, 128))

    def body(x_ref, g_ref, o_ref):
        o_ref[...] = _rms(x_ref[...], g_ref[...]).astype(BF16)

    return _rows_call(name, body, S, tr, [(xv, _rb(tr, D)), (g, _fb((1, D)))], [((S, D), BF16, _rb(tr, D))])[0]


def norm_bwd(name, xv, g, dy, res=None, want_dx=True):
    S, D = xv.shape
    tr = _tile(S, (256, 128))

    def body(*refs):
        if res is None:
            x_ref, g_ref, dy_ref = refs[:3]
            outs = refs[3:]
            r_ref = None
        else:
            x_ref, g_ref, dy_ref, r_ref = refs[:4]
            outs = refs[4:]
        dx, dg = _rms_bwd(x_ref[...], g_ref[...], dy_ref[...])
        if r_ref is not None:
            dx = dx + r_ref[...]
        if want_dx:
            outs[0][...] = dx
            outs[1][...] = dx.astype(BF16)
        _acc_out(outs[-1], pl.program_id(0) == 0, dg)

    ins = [(xv, _rb(tr, D)), (g, _fb((1, D))), (dy, _rb(tr, D))] + ([] if res is None else [(res, _rb(tr, D))])
    outs = ([((S, D), F32, _rb(tr, D)), ((S, D), BF16, _rb(tr, D))] if want_dx else []) + [((1, D), F32, _fb((1, D)))]
    return _rows_call(name, body, S, tr, ins, outs)


def qkv_fwd(proj, g_q, g_k, FW):
    S = proj.shape[0]
    H = FW // HEAD_DIM
    tr = _tile(S, (256, 128))

    def body(q_ref, k_ref, v_ref, gq_ref, gk_ref, qo, ko, vo):
        qo[...] = _heads(lambda t: (_rms(t, gq_ref[...]),), H, q_ref[...])[0].astype(BF16)
        ko[...] = _heads(lambda t: (_rms(t, gk_ref[...]),), H, k_ref[...])[0].astype(BF16)
        vo[...] = v_ref[...].astype(BF16)

    o = ((S, FW), BF16, _rb(tr, FW))
    return _rows_call('qkv_fwd', body, S, tr,
                      [(proj, _rb(tr, FW, 0)), (proj, _rb(tr, FW, 1)), (proj, _rb(tr, FW, 2)),
                       (g_q, _fb((1, HEAD_DIM))), (g_k, _fb((1, HEAD_DIM)))], [o, o, o])


def qkv_bwd(proj, g_q, g_k, dqn, dkn, FW):
    S = proj.shape[0]
    H = FW // HEAD_DIM
    tr = _tile(S, (256, 128))

    def body(q_ref, k_ref, gq_ref, gk_ref, dq_ref, dk_ref, dqo, dko, dgq, dgk):
        dq, gq = _heads(lambda t, d: _rms_bwd(t, gq_ref[...], d), H, q_ref[...], dq_ref[...])
        dk, gk = _heads(lambda t, d: _rms_bwd(t, gk_ref[...], d), H, k_ref[...], dk_ref[...])
        dqo[...] = dq.astype(BF16)
        dko[...] = dk.astype(BF16)
        first = pl.program_id(0) == 0
        _acc_out(dgq, first, gq)
        _acc_out(dgk, first, gk)

    o = ((S, FW), BF16, _rb(tr, FW))
    og = ((1, HEAD_DIM), F32, _fb((1, HEAD_DIM)))
    return _rows_call('qkv_bwd', body, S, tr,
                      [(proj, _rb(tr, FW, 0)), (proj, _rb(tr, FW, 1)), (g_q, _fb((1, HEAD_DIM))),
                       (g_k, _fb((1, HEAD_DIM))), (dqn, _rb(tr, FW)), (dkn, _rb(tr, FW))], [o, o, og, og])


def _tri(n, upper):
    r = lax.broadcasted_iota(jnp.int32, (n, n), 0)
    c = lax.broadcasted_iota(jnp.int32, (n, n), 1)
    return jnp.where((c >= r) if upper else (c <= r), 1.0, 0.0).astype(BF16)


def _blocked_cumsum(val, S, blk, reverse):
    tri = _tri(blk, reverse)
    order = range(S // blk - 1, -1, -1) if reverse else range(S // blk)
    carry = jnp.zeros((1, LANES), F32)
    outs = {}
    for bi in order:
        part = val[bi * blk:(bi + 1) * blk]
        acc = carry
        for piece in _split3(part):
            acc = acc + jnp.dot(tri, piece, preferred_element_type=F32)
        outs[bi] = acc
        carry = carry + jnp.sum(part, axis=0, keepdims=True)
    return jnp.concatenate([outs[bi] for bi in range(S // blk)], axis=0)


def fgate_fwd(f_raw, b_f_pad):
    S = f_raw.shape[0]
    blk = _tile(S, (256, 128))

    def body(f_ref, b_ref, c_ref):
        z = f_ref[...] + b_ref[...]
        c_ref[...] = _blocked_cumsum(-_softplus(-z), S, blk, False)

    return pl.pallas_call(body, name='fgate_fwd', grid=(1,), in_specs=[_fb((S, LANES)), _fb((1, LANES))],
                          out_specs=_fb((S, LANES)), out_shape=jax.ShapeDtypeStruct((S, LANES), F32),
                          compiler_params=_params(('arbitrary',)))(f_raw, b_f_pad)


def fgate_bwd(f_raw, b_f_pad, dc, H):
    S = f_raw.shape[0]
    blk = _tile(S, (256, 128))

    def body(f_ref, b_ref, dc_ref, df_ref, db_ref):
        z = f_ref[...] + b_ref[...]
        dlogf = _blocked_cumsum(dc_ref[...], S, blk, True)
        lane = lax.broadcasted_iota(jnp.int32, (S, LANES), 1)
        df = jnp.where(lane < H, dlogf * _sigmoid(-z), 0.0)
        df_ref[...] = df.astype(BF16)
        db_ref[...] = jnp.sum(df, axis=0, keepdims=True)

    return pl.pallas_call(body, name='fgate_bwd', grid=(1,),
                          in_specs=[_fb((S, LANES)), _fb((1, LANES)), _fb((S, LANES))],
                          out_specs=[_fb((S, LANES)), _fb((1, LANES))],
                          out_shape=[jax.ShapeDtypeStruct((S, LANES), BF16), jax.ShapeDtypeStruct((1, LANES), F32)],
                          compiler_params=_params(('arbitrary',)))(f_raw, b_f_pad, dc)


def _fox_logits(q, k, c_blk, ct_blk, h, T, diagonal):
    s = lax.dot_general(q, k, _DIMS['nt'], preferred_element_type=F32) * (1.0 / math.sqrt(HEAD_DIM))
    lane = lax.broadcasted_iota(jnp.int32, c_blk.shape, 1)
    cq = jnp.sum(jnp.where(lane == h, c_blk, 0.0), axis=1, keepdims=True)
    sub = lax.broadcasted_iota(jnp.int32, ct_blk.shape, 0)
    ck = jnp.sum(jnp.where(sub == h, ct_blk, 0.0), axis=0, keepdims=True)
    s = s + cq - ck
    if not diagonal:
        return s
    rows = lax.broadcasted_iota(jnp.int32, (T, T), 0)
    cols = lax.broadcasted_iota(jnp.int32, (T, T), 1)
    return jnp.where(cols <= rows, s, -jnp.inf)


def _below_and_on_diagonal(q_blk, k_blk, step):
    @pl.when(k_blk < q_blk)
    def _():
        step(False)

    @pl.when(k_blk == q_blk)
    def _():
        step(True)


def fox_fwd(qn, kn, vb, c, ct, T):
    S, FW = qn.shape
    H = FW // HEAD_DIM
    Hp = ct.shape[0]
    n = S // T

    HB = 2 if H % 2 == 0 else 1
    W2 = HB * HEAD_DIM

    def body(q_ref, k_ref, v_ref, c_ref, ct_ref, o_ref, lse_ref, m_s, l_s, acc_s):
        hb, i, j = pl.program_id(0), pl.program_id(1), pl.program_id(2)

        @pl.when(j == 0)
        def _():
            m_s[...] = jnp.full_like(m_s, -jnp.inf)
            l_s[...] = jnp.zeros_like(l_s)
            acc_s[...] = jnp.zeros_like(acc_s)

        def step(diagonal):
            for hh in range(HB):
                sl = slice(hh * HEAD_DIM, (hh + 1) * HEAD_DIM)
                s = _fox_logits(q_ref[:, sl], k_ref[:, sl], c_ref[...], ct_ref[...], hb * HB + hh, T, diagonal)
                m_old = m_s[hh]
                m_new = jnp.maximum(m_old, jnp.max(s, axis=1, keepdims=True))
                alpha = jnp.exp(m_old - m_new)
                p = jnp.exp(s - m_new)
                l_s[hh] = alpha * l_s[hh] + jnp.sum(p, axis=1, keepdims=True)
                acc_s[hh] = alpha * acc_s[hh] + jnp.dot(p.astype(BF16), v_ref[:, sl], preferred_element_type=F32)
                m_s[hh] = m_new

        _below_and_on_diagonal(i, j, step)

        @pl.when(j == i)
        def _():
            for hh in range(HB):
                o_ref[:, hh * HEAD_DIM:(hh + 1) * HEAD_DIM] = acc_s[hh] / l_s[hh]
                lse_ref[hh] = jnp.broadcast_to(m_s[hh] + jnp.log(l_s[hh]), (T, LANES))

    qs = pl.BlockSpec((T, W2), lambda h, i, j: (i, h))
    ks = pl.BlockSpec((T, W2), lambda h, i, j: (jnp.minimum(j, i), h))
    return pl.pallas_call(
        body, name='fox_fwd', grid=(H // HB, n, n),
        in_specs=[qs, ks, ks, pl.BlockSpec((T, LANES), lambda h, i, j: (i, 0)),
                  pl.BlockSpec((Hp, T), lambda h, i, j: (0, jnp.minimum(j, i)))],
        out_specs=[qs, pl.BlockSpec((HB, T, LANES), lambda h, i, j: (h, i, 0))],
        out_shape=[jax.ShapeDtypeStruct((S, FW), F32), jax.ShapeDtypeStruct((H, S, LANES), F32)],
        scratch_shapes=[pltpu.VMEM((HB, T, 1), F32), pltpu.VMEM((HB, T, 1), F32), pltpu.VMEM((HB, T, HEAD_DIM), F32)],
        compiler_params=_params(('parallel', 'parallel', 'arbitrary')))(qn, kn, vb, c, ct)


def _fox_p_ds(q_ref, k_ref, v_ref, do_ref, c_ref, ct_ref, lse_ref, dl_ref, h, T, diagonal):
    s = _fox_logits(q_ref[...], k_ref[...], c_ref[...], ct_ref[...], h, T, diagonal)
    p = jnp.exp(s - jnp.tile(lse_ref[...], (1, T // LANES)))
    dp = lax.dot_general(do_ref[...], v_ref[...], _DIMS['nt'], preferred_element_type=F32)
    ds = p * (dp - jnp.tile(dl_ref[...], (1, T // LANES)))
    return p, dp, ds


def fox_bwd_q(qn, kn, vb, do, c, ct, lse, dl, T):
    S, FW = qn.shape
    H = FW // HEAD_DIM
    Hp = ct.shape[0]
    n = S // T
    HB = 2 if H % 2 == 0 else 1
    W2 = HB * HEAD_DIM

    def body(q_ref, k_ref, v_ref, do_ref, c_ref, ct_ref, lse_ref, dl_ref, dq_ref, dl2_ref, acc_s, rs_s):
        hb, i, j = pl.program_id(0), pl.program_id(1), pl.program_id(2)

        @pl.when(j == 0)
        def _():
            acc_s[...] = jnp.zeros_like(acc_s)
            rs_s[...] = jnp.zeros_like(rs_s)

        def step(diagonal):
            for hh in range(HB):
                sl = slice(hh * HEAD_DIM, (hh + 1) * HEAD_DIM)
                p, dp, ds = _fox_p_ds(q_ref.at[:, sl], k_ref.at[:, sl], v_ref.at[:, sl], do_ref.at[:, sl], c_ref, ct_ref,
                                      lse_ref.at[hh], dl_ref.at[hh], hb * HB + hh, T, diagonal)
                acc_s[hh] += jnp.dot(ds.astype(BF16), k_ref[:, sl], preferred_element_type=F32)
                rs_s[hh] += jnp.sum(p * dp, axis=1, keepdims=True)

        _below_and_on_diagonal(i, j, step)

        @pl.when(j == i)
        def _():
            for hh in range(HB):
                dq_ref[:, hh * HEAD_DIM:(hh + 1) * HEAD_DIM] = acc_s[hh] * (1.0 / math.sqrt(HEAD_DIM))
                dl2_ref[hh] = jnp.broadcast_to(rs_s[hh], (T, LANES))

    qs = pl.BlockSpec((T, W2), lambda h, i, j: (i, h))
    ks = pl.BlockSpec((T, W2), lambda h, i, j: (jnp.minimum(j, i), h))
    st = pl.BlockSpec((HB, T, LANES), lambda h, i, j: (h, i, 0))
    return pl.pallas_call(
        body, name='fox_bwd_q', grid=(H // HB, n, n),
        in_specs=[qs, ks, ks, qs, pl.BlockSpec((T, LANES), lambda h, i, j: (i, 0)),
                  pl.BlockSpec((Hp, T), lambda h, i, j: (0, jnp.minimum(j, i))), st, st],
        out_specs=[qs, st], out_shape=[jax.ShapeDtypeStruct((S, FW), F32), jax.ShapeDtypeStruct((H, S, LANES), F32)],
        scratch_shapes=[pltpu.VMEM((HB, T, HEAD_DIM), F32), pltpu.VMEM((HB, T, 1), F32)],
        compiler_params=_params(('parallel', 'parallel', 'arbitrary')))(qn, kn, vb, do, c, ct, lse, dl)


def fox_bwd_kv(qn, kn, vb, do, c, ct, lse, dl, T):
    S, FW = qn.shape
    H = FW // HEAD_DIM
    Hp = ct.shape[0]
    n = S // T

    HB = 2 if H % 2 == 0 else 1
    W2 = HB * HEAD_DIM

    def body(q_ref, k_ref, v_ref, do_ref, c_ref, ct_ref, lse_ref, dl_ref, dk_ref, dv_ref, dc_ref, dk_s, dv_s, dc_s):
        hb, j, i = pl.program_id(0), pl.program_id(1), pl.program_id(2)

        @pl.when(i == 0)
        def _():
            dk_s[...] = jnp.zeros_like(dk_s)
            dv_s[...] = jnp.zeros_like(dv_s)
            dc_s[...] = jnp.zeros_like(dc_s)

        def step(diagonal):
            for hh in range(HB):
                sl = slice(hh * HEAD_DIM, (hh + 1) * HEAD_DIM)
                p, _, ds = _fox_p_ds(q_ref.at[:, sl], k_ref.at[:, sl], v_ref.at[:, sl], do_ref.at[:, sl], c_ref, ct_ref,
                                     lse_ref.at[hh], dl_ref.at[hh], hb * HB + hh, T, diagonal)
                dv_s[hh] += lax.dot_general(p.astype(BF16), do_ref[:, sl], _DIMS['tn'], preferred_element_type=F32)
                dk_s[hh] += lax.dot_general(ds.astype(BF16), q_ref[:, sl], _DIMS['tn'], preferred_element_type=F32)
                dc_s[hh] += jnp.sum(ds, axis=0, keepdims=True)

        _below_and_on_diagonal(i, j, step)

        @pl.when(i == n - 1)
        def _():
            for hh in range(HB):
                sl = slice(hh * HEAD_DIM, (hh + 1) * HEAD_DIM)
                dk_ref[:, sl] = dk_s[hh] * (1.0 / math.sqrt(HEAD_DIM))
                dv_ref[:, sl] = dv_s[hh].astype(BF16)
                dc_ref[hh] = -dc_s[hh]

    qs = pl.BlockSpec((T, W2), lambda h, j, i: (jnp.maximum(i, j), h))
    ks = pl.BlockSpec((T, W2), lambda h, j, i: (j, h))
    st = pl.BlockSpec((HB, T, LANES), lambda h, j, i: (h, jnp.maximum(i, j), 0))
    return pl.pallas_call(
        body, name='fox_bwd_kv', grid=(H // HB, n, n),
        in_specs=[qs, ks, ks, qs, pl.BlockSpec((T, LANES), lambda h, j, i: (jnp.maximum(i, j), 0)),
                  pl.BlockSpec((Hp, T), lambda h, j, i: (0, j)), st, st],
        out_specs=[ks, ks, pl.BlockSpec((HB, 1, T), lambda h, j, i: (h, 0, j))],
        out_shape=[jax.ShapeDtypeStruct((S, FW), F32), jax.ShapeDtypeStruct((S, FW), BF16),
                   jax.ShapeDtypeStruct((H, 1, S), F32)],
        scratch_shapes=[pltpu.VMEM((HB, T, HEAD_DIM), F32), pltpu.VMEM((HB, T, HEAD_DIM), F32),
                        pltpu.VMEM((HB, 1, T), F32)],
        compiler_params=_params(('parallel', 'parallel', 'arbitrary')))(qn, kn, vb, do, c, ct, lse, dl)


def _shift_down(v, d, rows, fill):
    return jnp.where(rows >= d, pltpu.roll(v, d, 0), fill)


def _shift_up(v, d, rows, S, fill):
    return jnp.where(rows < S - d, pltpu.roll(v, S - d, 0), fill)


SUBLANES = 8


def _scan_by_doubling(a, b, pos, span, reverse):
    n = a.shape[0]
    d = 1
    while d < span:
        if reverse:
            keep = pos < span - d
            a_s, b_s = jnp.where(keep, pltpu.roll(a, n - d, 0), 1.0), jnp.where(keep, pltpu.roll(b, n - d, 0), 0.0)
        else:
            keep = pos >= d
            a_s, b_s = jnp.where(keep, pltpu.roll(a, d, 0), 1.0), jnp.where(keep, pltpu.roll(b, d, 0), 0.0)
        b = a * b_s + b
        a = a * a_s
        d *= 2
    return a, b


def _scan(a, b, rows, S, reverse, scr):
    groups = S // SUBLANES
    a, b = _scan_by_doubling(a, b, jnp.bitwise_and(rows, SUBLANES - 1), SUBLANES, reverse)
    scr[0][...] = a
    scr[1][...] = b
    edge = 0 if reverse else SUBLANES - 1
    a_g = scr[0][pl.ds(edge, groups, stride=SUBLANES), :]
    b_g = scr[1][pl.ds(edge, groups, stride=SUBLANES), :]
    g_pos = lax.broadcasted_iota(jnp.int32, (groups, LANES), 0)
    _, h_g = _scan_by_doubling(a_g, b_g, g_pos, groups, reverse)
    if reverse:
        carry = jnp.where(g_pos < groups - 1, pltpu.roll(h_g, groups - 1, 0), 0.0)
    else:
        carry = jnp.where(g_pos >= 1, pltpu.roll(h_g, 1, 0), 0.0)
    for r in range(SUBLANES):
        scr[0][pl.ds(r, groups, stride=SUBLANES), :] = carry
    return b + a * scr[0][...]


def _lru_forward(u, cw, cb, wra, bra, wri, bri, lam, rows, scr):
    uc = cb + cw[CONV_W - 1] * u
    for d in range(1, CONV_W):
        uc = uc + cw[CONV_W - 1 - d] * _shift_down(u, d, rows, 0.0)
    ucb = uc.astype(BF16)
    r = _sigmoid(jnp.dot(ucb, wra.astype(BF16), preferred_element_type=F32) + bra)
    ig = _sigmoid(jnp.dot(ucb, wri.astype(BF16), preferred_element_type=F32) + bri)
    sp = _softplus(-lam)
    log_a = -LRU_C * r * sp
    a = jnp.exp(log_a)
    sq = jnp.sqrt(_neg_expm1(2.0 * log_a))
    iu = ig * uc
    hseq = _scan(a, sq * iu, rows, u.shape[0], False, scr)
    return uc, ucb, r, ig, sp, a, sq, iu, hseq


def _lru_specs(S, n_u, n_g):
    col = lambda off: pl.BlockSpec((S, LANES), lambda cbk: (0, off + cbk))
    vec = pl.BlockSpec((1, LANES), lambda cbk: (0, cbk))
    mat = pl.BlockSpec((None, LANES, LANES), lambda cbk: (cbk, 0, 0))
    cw = pl.BlockSpec((CONV_W, LANES), lambda cbk: (0, cbk))
    return col, vec, mat, cw


def lru_fwd(proj, conv_w, conv_b, w_ra, b_ra, w_ri, b_ri, lam, u_off, g_off):
    S = proj.shape[0]
    nb = w_ra.shape[0]
    col, vec, mat, cws = _lru_specs(S, u_off, g_off)

    def body(u_ref, g_ref, cw_ref, cb_ref, wra_ref, bra_ref, wri_ref, bri_ref, lam_ref, y_ref, scr0, scr1):
        rows = lax.broadcasted_iota(jnp.int32, (S, LANES), 0)
        cw = [cw_ref[t:t + 1, :] for t in range(CONV_W)]
        hseq = _lru_forward(u_ref[...], cw, cb_ref[...], wra_ref[...], bra_ref[...], wri_ref[...],
                            bri_ref[...], lam_ref[...], rows, (scr0, scr1))[-1]
        y_ref[...] = hseq * _gelu_and_grad(g_ref[...])[0]

    return pl.pallas_call(
        body, name='lru_fwd', grid=(nb,),
        in_specs=[col(u_off), col(g_off), cws, vec, mat, vec, mat, vec, vec], out_specs=col(0),
        out_shape=jax.ShapeDtypeStruct((S, nb * LANES), F32),
        scratch_shapes=[pltpu.VMEM((S, LANES), F32), pltpu.VMEM((S, LANES), F32)],
        compiler_params=_params(('parallel',)))(proj, proj, conv_w, conv_b, w_ra, b_ra, w_ri, b_ri, lam)


def lru_bwd(proj, dy, conv_w, conv_b, w_ra, b_ra, w_ri, b_ri, lam, u_off, g_off):
    S = proj.shape[0]
    nb = w_ra.shape[0]
    LW = nb * LANES
    col, vec, mat, cws = _lru_specs(S, u_off, g_off)

    def body(u_ref, g_ref, dy_ref, cw_ref, cb_ref, wra_ref, bra_ref, wri_ref, bri_ref, lam_ref,
             du_ref, dg_ref, dcw_ref, dcb_ref, dwra_ref, dbra_ref, dwri_ref, dbri_ref, dlam_ref, scr0, scr1):
        rows = lax.broadcasted_iota(jnp.int32, (S, LANES), 0)
        u, lam_v = u_ref[...], lam_ref[...]
        cw = [cw_ref[t:t + 1, :] for t in range(CONV_W)]
        wra, wri = wra_ref[...].astype(BF16), wri_ref[...].astype(BF16)
        uc, ucb, r, ig, sp, a, sq, iu, hseq = _lru_forward(u, cw, cb_ref[...], wra, bra_ref[...], wri, bri_ref[...],
                                                           lam_v, rows, (scr0, scr1))
        gl, dgl = _gelu_and_grad(g_ref[...])
        dy_v = dy_ref[...]
        dg_ref[...] = (dy_v * hseq * dgl).astype(BF16)
        G = _scan(_shift_up(a, 1, rows, S, 0.0), dy_v * gl, rows, S, True, (scr0, scr1))
        da = G * _shift_down(hseq, 1, rows, 0.0)
        diu = G * sq
        dsq = G * iu
        dlog_a = da * a - dsq * a * a / jnp.maximum(sq, 1e-30)
        dr = dlog_a * (-LRU_C * sp)
        dsp = jnp.sum(dlog_a * (-LRU_C * r), axis=0, keepdims=True)
        dlam_ref[...] = -dsp * _sigmoid(-lam_v)
        dzr = dr * r * (1.0 - r)
        dzi = diu * uc * ig * (1.0 - ig)
        dzrb, dzib = dzr.astype(BF16), dzi.astype(BF16)
        duc = (diu * ig + lax.dot_general(dzrb, wra, _DIMS['nt'], preferred_element_type=F32)
               + lax.dot_general(dzib, wri, _DIMS['nt'], preferred_element_type=F32))
        dwra_ref[...] = lax.dot_general(ucb, dzrb, _DIMS['tn'], preferred_element_type=F32)
        dwri_ref[...] = lax.dot_general(ucb, dzib, _DIMS['tn'], preferred_element_type=F32)
        dbra_ref[...] = jnp.sum(dzr, axis=0, keepdims=True)
        dbri_ref[...] = jnp.sum(dzi, axis=0, keepdims=True)
        dcb_ref[...] = jnp.sum(duc, axis=0, keepdims=True)
        du = cw[CONV_W - 1] * duc
        dcw_ref[CONV_W - 1:CONV_W, :] = jnp.sum(duc * u, axis=0, keepdims=True)
        for d in range(1, CONV_W):
            du = du + cw[CONV_W - 1 - d] * _shift_up(duc, d, rows, S, 0.0)
            dcw_ref[CONV_W - 1 - d:CONV_W - d, :] = jnp.sum(duc * _shift_down(u, d, rows, 0.0), axis=0, keepdims=True)
        du_ref[...] = du.astype(BF16)

    sd = jax.ShapeDtypeStruct
    return pl.pallas_call(
        body, name='lru_bwd', grid=(nb,),
        in_specs=[col(u_off), col(g_off), col(0), cws, vec, mat, vec, mat, vec, vec],
        out_specs=[col(0), col(0), cws, vec, mat, vec, mat, vec, vec],
        out_shape=[sd((S, LW), BF16), sd((S, LW), BF16), sd((CONV_W, LW), F32), sd((1, LW), F32),
                   sd((nb, LANES, LANES), F32), sd((1, LW), F32), sd((nb, LANES, LANES), F32), sd((1, LW), F32),
                   sd((1, LW), F32)],
        scratch_shapes=[pltpu.VMEM((S, LANES), F32), pltpu.VMEM((S, LANES), F32)],
        compiler_params=_params(('parallel',)))(proj, proj, dy, conv_w, conv_b, w_ra, b_ra, w_ri, b_ri, lam)


def mix_fwd(o_fox, y_lru, g_fox, g_lru):
    S, FW = o_fox.shape
    tr = _tile(S, (256, 128))

    def body(o_ref, y_ref, gf_ref, gl_ref, m_ref):
        m_ref[...] = jnp.concatenate([_rms(o_ref[...], gf_ref[...]), _rms(y_ref[...], gl_ref[...])],
                                     axis=1).astype(BF16)

    return _rows_call('mix_fwd', body, S, tr,
                      [(o_fox, _rb(tr, FW)), (y_lru, _rb(tr, FW)), (g_fox, _fb((1, FW))), (g_lru, _fb((1, FW)))],
                      [((S, 2 * FW), BF16, _rb(tr, 2 * FW))])[0]


def mix_bwd(o_fox, y_lru, g_fox, g_lru, dmix):
    S, FW = o_fox.shape
    H = FW // HEAD_DIM
    tr = _tile(S, (256, 128))

    def body(o_ref, y_ref, gf_ref, gl_ref, df_ref, dl_ref, do_ref, dlt_ref, dy_ref, dgf_ref, dgl_ref):
        o = o_ref[...]
        do, dgf = _rms_bwd(o, gf_ref[...], df_ref[...])
        dyl, dgl = _rms_bwd(y_ref[...], gl_ref[...], dl_ref[...])
        do_ref[...] = do.astype(BF16)
        dy_ref[...] = dyl
        prod = do * o
        for h in range(H):
            dlt_ref[h] = jnp.broadcast_to(
                jnp.sum(prod[:, h * HEAD_DIM:(h + 1) * HEAD_DIM], axis=1, keepdims=True), (tr, LANES))
        first = pl.program_id(0) == 0
        _acc_out(dgf_ref, first, dgf)
        _acc_out(dgl_ref, first, dgl)

    g = _fb((1, FW))
    return _rows_call('mix_bwd', body, S, tr,
                      [(o_fox, _rb(tr, FW)), (y_lru, _rb(tr, FW)), (g_fox, g), (g_lru, g), (dmix, _rb(tr, FW, 0)),
                       (dmix, _rb(tr, FW, 1))],
                      [((S, FW), BF16, _rb(tr, FW)), ((H, S, LANES), F32, pl.BlockSpec((H, tr, LANES), lambda i: (0, i, 0))),
                       ((S, FW), F32, _rb(tr, FW)), ((1, FW), F32, g), ((1, FW), F32, g)])


def _xattn_heads(cq_raw, ckv, g_cq, g_ck, XW):
    out = []
    for h in range(XW // HEAD_DIM):
        sl = slice(h * HEAD_DIM, (h + 1) * HEAD_DIM)
        out.append((cq_raw[:, sl], _rms(cq_raw[:, sl], g_cq), ckv[:, sl], _rms(ckv[:, sl], g_ck),
                    ckv[:, XW + h * HEAD_DIM:XW + (h + 1) * HEAD_DIM].astype(BF16)))
    return out


def xattn_fwd(cq_raw, ckv, g_cq, g_ck):
    S, XW = cq_raw.shape
    M = ckv.shape[0]
    tr = _tile(S, (512, 256, 128))

    def body(q_ref, kv_ref, gq_ref, gk_ref, o_ref):
        outs = []
        for _, qn, _, kn, v in _xattn_heads(q_ref[...], kv_ref[...], gq_ref[...], gk_ref[...], XW):
            s = lax.dot_general(qn.astype(BF16), kn.astype(BF16), _DIMS['nt'], preferred_element_type=F32)
            s = s / math.sqrt(HEAD_DIM)
            p = jnp.exp(s - jnp.max(s, axis=1, keepdims=True))
            p = p / jnp.sum(p, axis=1, keepdims=True)
            outs.append(jnp.dot(p.astype(BF16), v, preferred_element_type=F32))
        o_ref[...] = jnp.concatenate(outs, axis=1).astype(BF16)

    g = _fb((1, HEAD_DIM))
    return _rows_call('xattn_fwd', body, S, tr,
                      [(cq_raw, _rb(tr, XW)), (ckv, _fb((M, 2 * XW))), (g_cq, g), (g_ck, g)],
                      [((S, XW), BF16, _rb(tr, XW))])[0]


def xattn_bwd(cq_raw, ckv, g_cq, g_ck, do):
    S, XW = cq_raw.shape
    M = ckv.shape[0]
    tr = _tile(S, (512, 256, 128))
    n = S // tr

    def body(q_ref, kv_ref, gq_ref, gk_ref, do_ref, dq_ref, dkv_ref, dgq_ref, dgk_ref):
        i = pl.program_id(0)
        do_v = do_ref[...]
        dqs, dkn, dvs = [], [], []
        dgq = jnp.zeros((1, HEAD_DIM), F32)
        for h, (q_raw, qn, _, kn, v) in enumerate(_xattn_heads(q_ref[...], kv_ref[...], gq_ref[...], gk_ref[...], XW)):
            qb, kb = qn.astype(BF16), kn.astype(BF16)
            doh = do_v[:, h * HEAD_DIM:(h + 1) * HEAD_DIM]
            s = lax.dot_general(qb, kb, _DIMS['nt'], preferred_element_type=F32) / math.sqrt(HEAD_DIM)
            p = jnp.exp(s - jnp.max(s, axis=1, keepdims=True))
            p = p / jnp.sum(p, axis=1, keepdims=True)
            dp = lax.dot_general(doh, v, _DIMS['nt'], preferred_element_type=F32)
            ds = (p * (dp - jnp.sum(p * dp, axis=1, keepdims=True)) / math.sqrt(HEAD_DIM)).astype(BF16)
            dvs.append(lax.dot_general(p.astype(BF16), doh, _DIMS['tn'], preferred_element_type=F32))
            dkn.append(lax.dot_general(ds, qb, _DIMS['tn'], preferred_element_type=F32))
            dq, g1 = _rms_bwd(q_raw, gq_ref[...], jnp.dot(ds, kb, preferred_element_type=F32))
            dqs.append(dq)
            dgq = dgq + g1
        dq_ref[...] = jnp.concatenate(dqs, axis=1).astype(BF16)
        first = i == 0
        _acc_out(dgq_ref, first, dgq)
        _acc_out(dkv_ref, first, jnp.concatenate(dkn + dvs, axis=1))

        @pl.when(i == n - 1)
        def _():
            kv = kv_ref[...]
            acc = dkv_ref[...]
            dk, gk = _heads(lambda t, d: _rms_bwd(t, gk_ref[...], d), XW // HEAD_DIM, kv[:, :XW], acc[:, :XW])
            dkv_ref[:, :XW] = dk
            dgk_ref[...] = gk

    g = _fb((1, HEAD_DIM))
    return _rows_call('xattn_bwd', body, S, tr,
                      [(cq_raw, _rb(tr, XW)), (ckv, _fb((M, 2 * XW))), (g_cq, g), (g_ck, g), (do, _rb(tr, XW))],
                      [((S, XW), BF16, _rb(tr, XW)), ((M, 2 * XW), F32, _fb((M, 2 * XW))), ((1, HEAD_DIM), F32, g),
                       ((1, HEAD_DIM), F32, g)])


def gate_up_fwd(hf, w, F):
    S, D = hf.shape
    J, _, Nj = w.shape
    tm = _tile(S, (1024, 512, 256, 128))
    tn = _tile(Nj, (256, 128))
    per = Nj // tn
    half = J // 2 * per

    def body(a_ref, bg_ref, bu_ref, gu_ref, act_ref):
        a = a_ref[...]
        g = jnp.dot(a, bg_ref[...], preferred_element_type=F32)
        u = jnp.dot(a, bu_ref[...], preferred_element_type=F32)
        gu_ref[0] = g
        gu_ref[1] = u
        act_ref[...] = (g * _sigmoid(g) * u).astype(BF16)

    return pl.pallas_call(
        body, name='proj_gate_up', grid=(S // tm, half),
        in_specs=[pl.BlockSpec((tm, D), lambda m, n: (m, 0)),
                  pl.BlockSpec((None, D, tn), lambda m, n: (n // per, 0, n % per)),
                  pl.BlockSpec((None, D, tn), lambda m, n: ((n + half) // per, 0, n % per))],
        out_specs=[pl.BlockSpec((2, tm, tn), lambda m, n: (0, m, n)), pl.BlockSpec((tm, tn), lambda m, n: (m, n))],
        out_shape=[jax.ShapeDtypeStruct((2, S, F), F32), jax.ShapeDtypeStruct((S, F), BF16)],
        compiler_params=_params(('parallel', 'parallel')))(hf, w, w)


def down_bwd_x(dyb, w_down, gu, after):
    S, D = dyb.shape
    F = w_down.shape[0]
    tm = _tile(S, (1024, 512, 256, 128))
    tn = _tile(F, (512, 256, 128))

    def body(a_ref, b_ref, gu_ref, after_ref, o_ref):
        da = lax.dot_general(a_ref[...], b_ref[...], _DIMS['nt'], preferred_element_type=F32)
        g = gu_ref[0]
        sg = _sigmoid(g)
        o_ref[0] = (da * gu_ref[1] * sg * (1.0 + g * (1.0 - sg))).astype(BF16)
        o_ref[1] = (da * g * sg).astype(BF16)

    planes = pl.BlockSpec((2, tm, tn), lambda m, n: (0, m, n))
    return pl.pallas_call(
        body, name='bwd_down_x', grid=(S // tm, F // tn),
        in_specs=[pl.BlockSpec((tm, D), lambda m, n: (m, 0)), pl.BlockSpec((tn, D), lambda m, n: (n, 0)), planes, ANY],
        out_specs=planes, out_shape=jax.ShapeDtypeStruct((2, S, F), BF16),
        compiler_params=_params(('parallel', 'parallel')))(dyb, w_down, gu, after)


def down_fwd_loss(act, w_down, x2, target):
    S, F = act.shape
    D = w_down.shape[1]
    tm, tn, tk = _mm_tiles(S, D, F, F, act, w_down, F32, x2, tn_cands=(512, 256, 128))
    nk = F // tk

    def body(a_ref, b_ref, x_ref, t_ref, d_ref, db_ref, l_ref, acc):
        m, n, k = pl.program_id(0), pl.program_id(1), pl.program_id(2)
        part = jnp.dot(a_ref[...], b_ref[...], preferred_element_type=F32)

        @pl.when(k == 0)
        def _():
            acc[...] = part

        @pl.when(k > 0)
        def _():
            acc[...] += part

        @pl.when(k == nk - 1)
        def _():
            err = acc[...] + x_ref[...] - t_ref[...]
            d = err * (1.0 / D)
            d_ref[...] = d
            db_ref[...] = d.astype(BF16)
            tot = jnp.sum(jnp.sum(err * err, axis=1, keepdims=True), axis=0, keepdims=True) * (0.5 / D)
            _acc_out(l_ref, jnp.logical_and(m == 0, n == 0), jnp.broadcast_to(tot, (1, LANES)))

    tile = pl.BlockSpec((tm, tn), lambda m, n, k: (m, n))
    return pl.pallas_call(
        body, name='proj_down', grid=(S // tm, D // tn, nk),
        in_specs=[pl.BlockSpec((tm, tk), lambda m, n, k: (m, k)), pl.BlockSpec((tk, tn), lambda m, n, k: (k, n)), tile, tile],
        out_specs=[tile, tile, pl.BlockSpec((1, LANES), lambda m, n, k: (0, 0))],
        out_shape=[jax.ShapeDtypeStruct((S, D), F32), jax.ShapeDtypeStruct((S, D), BF16),
                   jax.ShapeDtypeStruct((1, LANES), F32)],
        scratch_shapes=[pltpu.VMEM((tm, tn), F32)],
        compiler_params=_params(('arbitrary', 'arbitrary', 'arbitrary')))(act, w_down, x2, target)


def swiglu_bwd(gu, dact, F, after):
    S = gu.shape[1]
    tr = _tile(S, (256, 128))
    tf = _tile(F, (1408, 1024, 512, 256, 128))
    nf = F // tf

    def body(gu_ref, da_ref, after_ref, o_ref):
        g, da = gu_ref[0], da_ref[...]
        sg = _sigmoid(g)
        o_ref[0] = (da * gu_ref[1] * sg * (1.0 + g * (1.0 - sg))).astype(BF16)
        o_ref[1] = (da * g * sg).astype(BF16)

    planes = pl.BlockSpec((2, tr, tf), lambda i, n: (0, i, n))
    return pl.pallas_call(
        body, name='swiglu_bwd', grid=(S // tr, nf),
        in_specs=[planes, pl.BlockSpec((tr, tf), lambda i, n: (i, n)), ANY],
        out_specs=planes, out_shape=jax.ShapeDtypeStruct((2, S, F), BF16),
        compiler_params=_params(('parallel', 'parallel')))(gu, dact, after)


def loss_head(y, target):
    S, D = y.shape
    tr = _tile(S, (256, 128))

    def body(y_ref, t_ref, d_ref, db_ref, l_ref):
        err = y_ref[...] - t_ref[...]
        d = err * (1.0 / D)
        d_ref[...] = d
        db_ref[...] = d.astype(BF16)
        part = jnp.sum(jnp.sum(err * err, axis=1, keepdims=True), axis=0, keepdims=True) * (0.5 / D)
        _acc_out(l_ref, pl.program_id(0) == 0, jnp.broadcast_to(part, (1, LANES)))

    return _rows_call('loss_head', body, S, tr, [(y, _rb(tr, D)), (target, _rb(tr, D))],
                      [((S, D), F32, _rb(tr, D)), ((S, D), BF16, _rb(tr, D)), ((1, LANES), F32, _fb((1, LANES)))])


def _adamw_math(w, gv, m, v):
    mn = ADAM_B1 * m + (1.0 - ADAM_B1) * gv
    vn = ADAM_B2 * v + (1.0 - ADAM_B2) * (gv * gv)
    m_hat = mn / (1.0 - ADAM_B1 ** ADAM_STEP)
    v_hat = vn / (1.0 - ADAM_B2 ** ADAM_STEP)
    return -ADAM_LR * (m_hat / (jnp.sqrt(v_hat) + ADAM_EPS) + ADAM_WD * w), mn, vn


def adamw(name, w, g, m, v):
    R, C = w.shape
    tr = _row_tile(R, C)

    def body(w_ref, g_ref, m_ref, v_ref, d_ref, mo_ref, vo_ref):
        d_ref[...], mo_ref[...], vo_ref[...] = _adamw_math(w_ref[...], g_ref[...], m_ref[...], v_ref[...])

    spec = _rb(tr, C)
    return _rows_call(name, body, R, tr, [(w, spec), (g, spec), (m, spec), (v, spec)], [((R, C), F32, spec)] * 3)


def adamw_halves(name, w, mine, other, m, v, c_idx):
    R, C = w.shape
    hr = R // 2
    tr = _row_tile(hr, C)

    def body(c_ref, w_ref, a_ref, b_ref, m_ref, v_ref, g_ref, d_ref, mo_ref, vo_ref):
        gv = jnp.where(pl.program_id(0) == c_ref[0], a_ref[...], b_ref[...])
        g_ref[...] = gv
        d_ref[...], mo_ref[...], vo_ref[...] = _adamw_math(w_ref[...], gv, m_ref[...], v_ref[...])

    full = pl.BlockSpec((None, tr, C), lambda hh, i, c_ref: (hh, i, 0))
    mine_spec = pl.BlockSpec((tr, C), lambda hh, i, c_ref: (jnp.where(hh == c_ref[0], i, 0), 0))
    other_spec = pl.BlockSpec((tr, C), lambda hh, i, c_ref: (jnp.where(hh == c_ref[0], 0, i), 0))
    outs = pl.pallas_call(
        body, name=name,
        grid_spec=pltpu.PrefetchScalarGridSpec(num_scalar_prefetch=1, grid=(2, hr // tr),
                                               in_specs=[full, mine_spec, other_spec, full, full], out_specs=[full] * 4),
        out_shape=[jax.ShapeDtypeStruct((2, hr, C), F32)] * 4,
        compiler_params=_params(('parallel', 'parallel')))(
            c_idx, w.reshape(2, hr, C), mine, other, m.reshape(2, hr, C), v.reshape(2, hr, C))
    return [o.reshape(R, C) for o in outs]


def _place():
    x, y, c = lax.axis_index('x'), lax.axis_index('y'), lax.axis_index('c')
    return x, y, c, [(1 - x, y), (x, 1 - y), (1 - x, 1 - y)]


def _rcopy(src, dst, ssem, rsem, dev):
    return pltpu.make_async_remote_copy(src_ref=src, dst_ref=dst, send_sem=ssem, recv_sem=rsem, device_id=dev,
                                        device_id_type=MESH)


HBM = pl.BlockSpec(memory_space=pltpu.HBM)
SEM = pl.BlockSpec(memory_space=pltpu.SEMAPHORE)
EFFECT = pltpu.SideEffectType.DATAFLOW_SIDE_EFFECTING


def _in_hbm(a):
    return pltpu.with_memory_space_constraint(a, pltpu.HBM)


def _rows_part(shape, whole, half):
    return pl.ds(0, shape[0]) if whole else pl.ds(half * (shape[0] // 2), shape[0] // 2)


def gather_start(name, shards, whole):
    nT = len(shards)

    def body(*refs):
        srcs, lands = refs[:nT], refs[nT:2 * nT]
        ssem, rsem, token = refs[2 * nT], refs[2 * nT + 1], refs[-1]
        x, y, c, chips = _place()
        for t in range(nT):
            rows = _rows_part(shards[t].shape, whole[t], c)
            for k, (px, py) in enumerate(chips):
                _rcopy(srcs[t].at[rows], lands[t].at[2 * x + y, rows], ssem.at[3 * t + k], rsem.at[3 * t + k],
                       (px, py, c)).start()
        token[...] = jnp.zeros_like(token)

    zones = [lax.empty((N_CHIPS,) + s.shape, s.dtype) for s in shards]
    outs = pl.pallas_call(
        body, name=name,
        out_shape=(pltpu.SemaphoreType.DMA((3 * nT,)), pltpu.SemaphoreType.DMA((3 * nT,)),
                   *[pltpu.HBM(s.shape, s.dtype) for s in shards], *[pltpu.HBM(z.shape, z.dtype) for z in zones],
                   jax.ShapeDtypeStruct((8, LANES), F32)),
        in_specs=[HBM] * (2 * nT), out_specs=(SEM, SEM, *[HBM] * (2 * nT), pl.BlockSpec(memory_space=pltpu.VMEM)),
        input_output_aliases={i: 2 + i for i in range(2 * nT)},
        compiler_params=pltpu.CompilerParams(has_side_effects=EFFECT))(*[_in_hbm(a) for a in list(shards) + zones])
    return outs[0], outs[1], outs[2:2 + nT], outs[2 + nT:2 + 2 * nT], outs[-1]


def gather_wait(name, t, shard, zone, ssem, rsem, after, whole):
    after = after if isinstance(after, (list, tuple)) else [after]

    def body(src_ref, land_ref, ssem_ref, rsem_ref, *rest):
        x, y, c, chips = _place()
        rows = _rows_part(shard.shape, whole, c)
        for k, (px, py) in enumerate(chips):
            cp = _rcopy(src_ref.at[rows], land_ref.at[2 * px + py, rows], ssem_ref.at[3 * t + k], rsem_ref.at[3 * t + k],
                        (px, py, c))
            cp.wait_send()
            cp.wait_recv()

    return pl.pallas_call(
        body, name=name, out_shape=(pltpu.HBM(shard.shape, shard.dtype), pltpu.HBM(zone.shape, zone.dtype)),
        in_specs=(HBM, HBM, SEM, SEM, *[ANY] * len(after)), out_specs=(HBM, HBM), input_output_aliases={0: 0, 1: 1},
        compiler_params=pltpu.CompilerParams(has_side_effects=EFFECT))(shard, zone, ssem, rsem, *after)


def pair_swap(name, zone):
    hr = zone.shape[1] // 2

    def body(z_in, z_ref, ssem, rsem):
        x, y, c, chips = _place()
        cps = []
        for k, (px, py) in enumerate(chips):
            blk = z_ref.at[2 * px + py, pl.ds(c * hr, hr)]
            cps.append(_rcopy(blk, blk, ssem.at[k], rsem.at[k], (x, y, 1 - c)))
            cps[-1].start()
        for k, (px, py) in enumerate(chips):
            blk = z_ref.at[2 * px + py, pl.ds((1 - c) * hr, hr)]
            _rcopy(blk, blk, ssem.at[k], rsem.at[k], (x, y, 1 - c)).wait_recv()
        for cp in cps:
            cp.wait_send()

    return pl.pallas_call(
        body, name=name, in_specs=[ANY], out_specs=ANY, out_shape=jax.ShapeDtypeStruct(zone.shape, zone.dtype),
        input_output_aliases={0: 0},
        scratch_shapes=[pltpu.SemaphoreType.DMA((3,)), pltpu.SemaphoreType.DMA((3,))],
        compiler_params=_params())(zone)


N_SENDERS = 7


def _scatter_copies(g_ref, l_ref, ssem, rsem):
    x, y, c, chips = _place()
    cps = []
    for k, (px, py) in enumerate(chips):
        for d in range(2):
            to = (c + d) % 2
            cps.append(_rcopy(g_ref.at[2 * px + py, to], l_ref.at[2 * k + d], ssem.at[2 * k + d], rsem.at[2 * k + d],
                              (px, py, to)))
    cps.append(_rcopy(g_ref.at[2 * x + y, 1 - c], l_ref.at[6], ssem.at[6], rsem.at[6], (x, y, 1 - c)))
    return cps


def scatter_start(name, g):
    def body(g_ref, l_ref, ssem, rsem, g_out, l_out, token):
        for cp in _scatter_copies(g_ref, l_ref, ssem, rsem):
            cp.start()
        token[...] = jnp.zeros_like(token)

    zone = lax.empty((N_SENDERS,) + g.shape[2:], g.dtype)
    return pl.pallas_call(
        body, name=name,
        out_shape=(pltpu.SemaphoreType.DMA((N_SENDERS,)), pltpu.SemaphoreType.DMA((N_SENDERS,)),
                   pltpu.HBM(g.shape, g.dtype), pltpu.HBM(zone.shape, zone.dtype), jax.ShapeDtypeStruct((8, LANES), F32)),
        in_specs=[HBM, HBM], out_specs=(SEM, SEM, HBM, HBM, pl.BlockSpec(memory_space=pltpu.VMEM)),
        input_output_aliases={0: 2, 1: 3},
        compiler_params=pltpu.CompilerParams(has_side_effects=EFFECT))(_in_hbm(g), _in_hbm(zone))


def scatter_wait(name, g, zone, ssem, rsem, after):
    def body(g_ref, l_ref, ssem_ref, rsem_ref, after_ref, g_out, l_out):
        for cp in _scatter_copies(g_ref, l_ref, ssem_ref, rsem_ref):
            cp.wait_send()
            cp.wait_recv()

    return pl.pallas_call(
        body, name=name, out_shape=(pltpu.HBM(g.shape, g.dtype), pltpu.HBM(zone.shape, zone.dtype)),
        in_specs=(HBM, HBM, SEM, SEM, ANY), out_specs=(HBM, HBM), input_output_aliases={0: 0, 1: 1},
        compiler_params=pltpu.CompilerParams(has_side_effects=EFFECT))(g, zone, ssem, rsem, after)


def sum_parts(name, g, landed, chip_idx, c_idx):
    hr, C = g.shape[2:]
    tr = _row_tile(hr, C, min_rows=16)

    def body(me_ref, c_ref, g_ref, l_ref, o_ref):
        acc = g_ref[...].astype(F32)
        for s in range(N_SENDERS):
            acc = acc + l_ref[s].astype(F32)
        o_ref[...] = acc

    return pl.pallas_call(
        body, name=name,
        grid_spec=pltpu.PrefetchScalarGridSpec(
            num_scalar_prefetch=2, grid=(hr // tr,),
            in_specs=[pl.BlockSpec((None, None, tr, C), lambda i, me_ref, c_ref: (me_ref[0], c_ref[0], i, 0)),
                      pl.BlockSpec((N_SENDERS, tr, C), lambda i, me_ref, c_ref: (0, i, 0))],
            out_specs=pl.BlockSpec((tr, C), lambda i, me_ref, c_ref: (i, 0))),
        out_shape=jax.ShapeDtypeStruct((hr, C), F32),
        compiler_params=_params(('parallel',)))(chip_idx, c_idx, g, landed)


def pair_join(name, halves):
    nT = len(halves)

    def body(*refs):
        ins, outs = refs[:nT], refs[nT:2 * nT]
        ssem, rsem = refs[2 * nT:]
        x, y, c, _ = _place()
        cps = [_rcopy(ins[t], outs[t], ssem.at[t], rsem.at[t], (x, y, 1 - c)) for t in range(nT)]
        for cp in cps:
            cp.start()
        for cp in cps:
            cp.wait()

    return pl.pallas_call(
        body, name=name, in_specs=[ANY] * nT, out_specs=[ANY] * nT,
        out_shape=[jax.ShapeDtypeStruct(h.shape, h.dtype) for h in halves],
        scratch_shapes=[pltpu.SemaphoreType.DMA((nT,)), pltpu.SemaphoreType.DMA((nT,))],
        compiler_params=_params())(*halves)


N_DEVICES = 8


def _spread_copies(b_ref, l_ref, ssem, rsem):
    x, y, c, chips = _place()
    me = 4 * x + 2 * y + c
    pairs = []
    for px, py, pc in [(px, py, pc) for px, py in chips for pc in (c, 1 - c)] + [(x, y, 1 - c)]:
        it = 4 * px + 2 * py + pc
        pairs.append((_rcopy(b_ref, l_ref.at[me], ssem.at[it], rsem.at[me], (px, py, pc)),
                      _rcopy(b_ref, l_ref.at[it], ssem.at[it], rsem.at[it], (px, py, pc))))
    return pairs


def spread_start(name, buf):
    def body(b_ref, l_ref, ssem, rsem, b_out, l_out, token):
        for mine, _ in _spread_copies(b_ref, l_ref, ssem, rsem):
            mine.start()
        token[...] = jnp.zeros_like(token)

    zone = lax.empty((N_DEVICES,) + buf.shape, buf.dtype)
    return pl.pallas_call(
        body, name=name,
        out_shape=(pltpu.SemaphoreType.DMA((N_DEVICES,)), pltpu.SemaphoreType.DMA((N_DEVICES,)),
                   pltpu.HBM(buf.shape, buf.dtype), pltpu.HBM(zone.shape, zone.dtype), jax.ShapeDtypeStruct((8, LANES), F32)),
        in_specs=[HBM, HBM], out_specs=(SEM, SEM, HBM, HBM, pl.BlockSpec(memory_space=pltpu.VMEM)),
        input_output_aliases={0: 2, 1: 3},
        compiler_params=pltpu.CompilerParams(has_side_effects=EFFECT))(_in_hbm(buf), _in_hbm(zone))


def spread_wait(name, buf, zone, ssem, rsem, after):
    def body(b_ref, l_ref, ssem_ref, rsem_ref, after_ref, b_out, l_out):
        for mine, theirs in _spread_copies(b_ref, l_ref, ssem_ref, rsem_ref):
            mine.wait_send()
            theirs.wait_recv()

    return pl.pallas_call(
        body, name=name, out_shape=(pltpu.HBM(buf.shape, buf.dtype), pltpu.HBM(zone.shape, zone.dtype)),
        in_specs=(HBM, HBM, SEM, SEM, ANY), out_specs=(HBM, HBM), input_output_aliases={0: 0, 1: 1},
        compiler_params=pltpu.CompilerParams(has_side_effects=EFFECT))(buf, zone, ssem, rsem, after)


def sum_devices(name, zone):
    _, R, C = zone.shape
    tr = _row_tile(R, C)

    def body(z_ref, o_ref):
        acc = z_ref[0]
        for d in range(1, N_DEVICES):
            acc = acc + z_ref[d]
        o_ref[...] = acc

    return pl.pallas_call(
        body, name=name, grid=(R // tr,), in_specs=[pl.BlockSpec((N_DEVICES, tr, C), lambda i: (0, i, 0))],
        out_specs=pl.BlockSpec((tr, C), lambda i: (i, 0)), out_shape=jax.ShapeDtypeStruct((R, C), F32),
        compiler_params=_params(('parallel',)))(zone)


class _InWindows:
    def __init__(self, FW, LW, H, C):
        gap = LANES - H
        padded = lambda o: o if o < 3 * FW + H else o + gap
        self.width = 3 * FW + LANES + 2 * LW
        self.first = [padded(C * j) // LANES for j in range(N_CHIPS)]
        self.blocks = max(padded(C * (j + 1) - 1) // LANES - self.first[j] + 1 for j in range(N_CHIPS))
        assert all((b + self.blocks) * LANES <= self.width for b in self.first)
        self.cols = self.blocks * LANES
        self.runs = []
        for j in range(N_CHIPS):
            cut = min(max(3 * FW + H - C * j, 0), C)
            spans = [(0, cut), (cut, C)]
            self.runs.append([(t0, t1, padded(C * j + t0) - LANES * self.first[j]) for t0, t1 in spans if t1 > t0])

    def to_window(self, shard, chip):
        def place(j, s):
            parts, pos = [], 0
            for t0, t1, w0 in self.runs[j]:
                parts += [jnp.zeros((s.shape[0], w0 - pos), s.dtype), s[:, t0:t1]]
                pos = w0 + t1 - t0
            parts.append(jnp.zeros((s.shape[0], self.cols - pos), s.dtype))
            return jnp.concatenate([p for p in parts if p.shape[1]], axis=1)
        return lax.switch(chip, [functools.partial(place, j) for j in range(N_CHIPS)], shard)

    def from_window(self, win, chip):
        def take(j, w):
            return jnp.concatenate([w[:, w0:w0 + t1 - t0] for t0, t1, w0 in self.runs[j]], axis=1)
        return lax.switch(chip, [functools.partial(take, j) for j in range(N_CHIPS)], win)

    def assemble(self, zone):
        total = None
        for j in range(N_CHIPS):
            lead = self.first[j] * LANES
            part = jnp.pad(zone[j], ((0, 0), (lead, self.width - lead - self.cols)))
            total = part if total is None else total + part
        return total

    def windows(self, padded_matrix):
        return jnp.stack([padded_matrix[:, b * LANES:b * LANES + self.cols] for b in self.first])


_PACK = 8 * LANES


PACK_ROWS = 256


def _pack(arrs):
    flat = []
    for a in arrs:
        v = a.reshape(-1).astype(F32)
        flat.append(jnp.pad(v, (0, (-v.shape[0]) % _PACK)))
    rows = sum(v.shape[0] for v in flat) // LANES
    flat.append(jnp.zeros(((-rows) % PACK_ROWS) * LANES, F32))
    return jnp.concatenate(flat).reshape(-1, LANES)


def _unpack(buf, shapes):
    out, off = [], 0
    flat = buf.reshape(-1)
    for sh in shapes:
        n = math.prod(sh)
        out.append(flat[off:off + n].reshape(sh))
        off += n + (-n) % _PACK
    return out


def kernel(x, mem, g_mix, w_in, b_f, g_q, g_k, conv_w, conv_b, w_ra, b_ra, w_ri, b_ri, lam, g_fox_out, g_lru_out, w_out, g_xattn, g_mem, w_cq, w_ckv, g_cq, g_ck, w_co, g_ffn, w_gate_up, w_down, loss_target, m_g_mix, m_w_in, m_b_f, m_g_q, m_g_k, m_conv_w, m_conv_b, m_w_ra, m_b_ra, m_w_ri, m_b_ri, m_lam, m_g_fox_out, m_g_lru_out, m_w_out, m_g_xattn, m_g_mem, m_w_cq, m_w_ckv, m_g_cq, m_g_ck, m_w_co, m_g_ffn, m_w_gate_up, m_w_down, v_g_mix, v_w_in, v_b_f, v_g_q, v_g_k, v_conv_w, v_conv_b, v_w_ra, v_b_ra, v_w_ri, v_b_ri, v_lam, v_g_fox_out, v_g_lru_out, v_w_out, v_g_xattn, v_g_mem, v_w_cq, v_w_ckv, v_g_cq, v_g_ck, v_w_co, v_g_ffn, v_w_gate_up, v_w_down):
    given = dict(locals())
    W = {n: given[n][0] for n in WEIGHTS}
    M1 = {n: given['m_' + n][0] for n in WEIGHTS}
    V1 = {n: given['v_' + n][0] for n in WEIGHTS}
    xs, ms, tgt = x[0], mem[0], loss_target[0]
    S, D = xs.shape
    H = W['b_f'].shape[0]
    FW = H * HEAD_DIM
    LW = W['lam'].shape[0]
    nb = W['w_ra'].shape[0]
    XW = W['w_cq'].shape[1]
    F = W['w_down'].shape[0] * N_CHIPS
    IN_W = W['w_in'].shape[1] * N_CHIPS
    assert FW == LW and LW == nb * LANES and IN_W == 3 * FW + H + 2 * LW and H <= 8
    T = _tile(S, (512, 256, 128))
    c_idx = lax.axis_index('c').astype(jnp.int32).reshape(1)
    chip = 2 * lax.axis_index('x') + lax.axis_index('y')
    chip_idx = chip.astype(jnp.int32).reshape(1)
    vec = lambda n: W[n].reshape(1, -1)

    wins = _InWindows(FW, LW, H, W['w_in'].shape[1])
    started = {}
    g_tok = jnp.zeros((1, 1), F32)
    for call, names in (('gather_start_first', ['conv_w', 'w_in']), ('gather_start_rest', BIG[1:])):
        own = [W[n].reshape(-1, LANES) if n == 'conv_w' else W[n].astype(BF16) + g_tok.astype(BF16) for n in names]
        own = [wins.to_window(o, chip) if n == 'w_in' else o for n, o in zip(names, own)]
        ssem, rsem, srcs, zones, tok = gather_start(call, own, [n == 'conv_w' for n in names])
        g_tok = tok[0:1, 0:1]
        started.update({n: (t, srcs[t], zones[t], ssem, rsem) for t, n in enumerate(names)})

    def fetch(n, after):
        t, g_src, g_zone, g_ssem, g_rsem = started[n]
        src, zone = gather_wait('gather_wait_' + n, t, g_src, g_zone, g_ssem, g_rsem, after, n == 'conv_w')
        if n != 'conv_w':
            zone = pair_swap('pair_swap_' + n, zone)
        return lax.dynamic_update_index_in_dim(zone, src, chip, 0)

    b_f_pad = jnp.pad(vec('b_f'), ((0, 0), (0, LANES - H)))
    u_off, g_off = 3 * FW // LANES, (3 * FW + LW) // LANES

    h1 = norm_fwd('norm_mix', xs, vec('g_mix') + g_tok[0:1, 0:1])
    conv_full = fetch('conv_w', h1).reshape(N_CHIPS, CONV_W, LW // N_CHIPS).transpose(1, 0, 2).reshape(CONV_W, LW)
    w_in_pad = wins.assemble(fetch('w_in', [h1, M1['w_in'], V1['w_in']]))
    w5 = jnp.concatenate([w_in_pad[:, :3 * FW], w_in_pad[:, 3 * FW + LANES:]], axis=1)
    wf = w_in_pad[:, 3 * FW:3 * FW + LANES]
    proj = _mm('proj_in', h1, w5, 'nn', F32)
    f_raw = _mm('proj_f', h1, wf, 'nn', F32)
    qn, kn, vb = qkv_fwd(proj, vec('g_q'), vec('g_k'), FW)
    cc = fgate_fwd(f_raw, b_f_pad)
    ct = cc[:, :8].T
    o_fox, lse = fox_fwd(qn, kn, vb, cc, ct, T)
    lru_w = (conv_full, vec('conv_b'), W['w_ra'], vec('b_ra'), W['w_ri'], vec('b_ri'), vec('lam'))
    y_lru = lru_fwd(proj, *lru_w, u_off, g_off)
    mixn = mix_fwd(o_fox, y_lru, vec('g_fox_out'), vec('g_lru_out'))
    w_out_f = fetch('w_out', mixn).reshape(2 * FW, D)
    x1 = _mm('proj_out', mixn, w_out_f, 'nn', F32, res=xs)

    hq = norm_fwd('norm_xq', x1, vec('g_xattn'))
    mn = norm_fwd('norm_mem', ms, vec('g_mem'))
    w_cq_f = fetch('w_cq', hq).reshape(D, XW)
    w_ckv_f = fetch('w_ckv', hq).reshape(D, 2 * XW)
    cq_raw = _mm('proj_cq', hq, w_cq_f, 'nn', F32)
    ckv = _mm('proj_ckv', mn, w_ckv_f, 'nn', F32)
    o_x = xattn_fwd(cq_raw, ckv, vec('g_cq'), vec('g_ck'))
    w_co_g = fetch('w_co', o_x)
    x2 = _mm_colsharded('proj_co', o_x, w_co_g, F32, res=x1)

    hf = norm_fwd('norm_ffn', x2, vec('g_ffn'))
    w_gu_g = fetch('w_gate_up', hf)
    gu, act = gate_up_fwd(hf, w_gu_g, F)
    w_down_f = fetch('w_down', act).reshape(F, D)
    dy, dyb, loss_blk = down_fwd_loss(act, w_down_f, x2, tgt)

    gw, pending = {}, []

    def reduce_begin(n, g):
        sp = g.reshape(N_CHIPS, 2, g.shape[1] // 2, g.shape[2])
        ssem, rsem, sp, zone, tok = scatter_start('scatter_start_' + n, sp)
        pending.append((n, sp, zone, ssem, rsem))
        return tok[0:1, 0:1]

    t_down = reduce_begin('w_down', _mm('bwd_down_w', act, dyb, 'tn', BF16).reshape(N_CHIPS, F // N_CHIPS, D))
    dgu = down_bwd_x(dyb, w_down_f, gu, t_down)
    dhf = _mm_colsharded_t('bwd_gate_up_x', dgu, w_gu_g, F32)
    t_gu = reduce_begin('w_gate_up', _mm_grad_colsharded('bwd_gate_up_w', hf, dgu, N_CHIPS, BF16))
    dx2, dx2b, gw['g_ffn'] = norm_bwd('norm_ffn_bwd', x2, vec('g_ffn') + t_down + t_gu, dhf, res=dy)

    do_x = _mm_colsharded_t('bwd_co_x', dx2b, w_co_g, BF16)
    t_co = reduce_begin('w_co', _mm_grad_colsharded('bwd_co_w', o_x, dx2b, N_CHIPS, BF16))
    dcq_raw, dckv, gw['g_cq'], gw['g_ck'] = xattn_bwd(cq_raw, ckv, vec('g_cq') + t_co, vec('g_ck'), do_x)
    dhq = _mm('bwd_cq_x', dcq_raw, w_cq_f, 'nt', F32)
    t_cq = reduce_begin('w_cq', _mm('bwd_cq_w', hq, dcq_raw, 'tn', BF16).reshape(N_CHIPS, D // N_CHIPS, XW))
    dmn = _mm('bwd_ckv_x', dckv, w_ckv_f, 'nt', F32)
    t_ckv = reduce_begin('w_ckv', _mm('bwd_ckv_w', mn, dckv, 'tn', BF16).reshape(N_CHIPS, D // N_CHIPS, 2 * XW))
    (gw['g_mem'],) = norm_bwd('norm_mem_bwd', ms, vec('g_mem'), dmn, want_dx=False)
    dx1, dx1b, gw['g_xattn'] = norm_bwd('norm_xq_bwd', x1, vec('g_xattn') + t_cq + t_ckv, dhq, res=dx2)

    dmix = _mm('bwd_out_x', dx1b, w_out_f, 'nt', F32)
    t_out = reduce_begin('w_out', _mm('bwd_out_w', mixn, dx1b, 'tn', BF16).reshape(N_CHIPS, 2 * FW // N_CHIPS, D))
    do_fox, delta, dy_lru, gw['g_fox_out'], gw['g_lru_out'] = mix_bwd(o_fox, y_lru, vec('g_fox_out') + t_out,
                                                                     vec('g_lru_out'), dmix)
    (du, dgate, gw['conv_w'], gw['conv_b'], gw['w_ra'], gw['b_ra'], gw['w_ri'], gw['b_ri'],
     gw['lam']) = lru_bwd(proj, dy_lru, *lru_w, u_off, g_off)
    early = [n for n in SMALL if n not in ('g_q', 'g_k', 'b_f', 'g_mix')]
    late = [n for n in SMALL if n not in early]
    e_ssem, e_rsem, e_buf, e_zone, e_tok = spread_start('spread_start_early', _pack([gw[n] for n in early]))
    dqn, delta2 = fox_bwd_q(qn, kn, vb, do_fox, cc, ct, lse, delta, T)
    dkn, dv, dct = fox_bwd_kv(qn, kn, vb, do_fox, cc, ct, lse, delta2, T)
    dq, dk, gw['g_q'], gw['g_k'] = qkv_bwd(proj, vec('g_q') + e_tok[0:1, 0:1], vec('g_k'), dqn, dkn, FW)
    dc = jnp.pad(dct.reshape(H, S).T, ((0, 0), (0, LANES - H)))
    df, db_f = fgate_bwd(f_raw, b_f_pad, dc, H)
    gw['b_f'] = db_f[:, :H]
    dproj = jnp.concatenate([dq, dk, dv, du, dgate], axis=1)
    dw5 = _mm('bwd_in_w', h1, dproj, 'tn', BF16)
    dwf = _mm('bwd_f_w', h1, df, 'tn', BF16)
    t_in = reduce_begin('w_in', wins.windows(jnp.concatenate([dw5[:, :3 * FW], dwf, dw5[:, 3 * FW:]], axis=1)))
    dh_a = _mm('bwd_f_x', df, wf, 'nt', F32)
    dh1 = _mm('bwd_in_x', dproj, w5, 'nt', F32, res=dh_a)
    grad_x, _, gw['g_mix'] = norm_bwd('norm_mix_bwd', xs, vec('g_mix') + t_in, dh1, res=dx1)
    l_ssem, l_rsem, l_buf, l_zone, _ = spread_start('spread_start_late',
                                                    _pack([gw[n] for n in late] + [loss_blk[0:1, 0:1]]))

    grads, delta_w, new_m, new_v = {}, {}, {}, {}
    done = grad_x
    for n, part, zone, ssem, rsem in pending:
        part, landed = scatter_wait('scatter_wait_' + n, part, zone, ssem, rsem, done)
        mine = sum_parts('sum_parts_' + n, part, landed, chip_idx, c_idx)
        (other,) = pair_join('pair_join_' + n, [mine])
        if n == 'w_in':
            mine, other = wins.from_window(mine, chip), wins.from_window(other, chip)
        grads[n], delta_w[n], new_m[n], new_v[n] = adamw_halves('adamw_' + n, W[n], mine, other, M1[n], V1[n], c_idx)
        done = delta_w[n]

    device = 4 * lax.axis_index('x') + 2 * lax.axis_index('y') + lax.axis_index('c')
    summed = {}
    for tag, names, buf, zone, ssem, rsem in (('early', early, e_buf, e_zone, e_ssem, e_rsem),
                                              ('late', late + ['loss'], l_buf, l_zone, l_ssem, l_rsem)):
        buf, zone = spread_wait('spread_wait_' + tag, buf, zone, ssem, rsem, done)
        total = sum_devices('sum_small_' + tag, lax.dynamic_update_index_in_dim(zone, buf, device, 0))
        summed.update(zip(names, _unpack(total, [gw[n].shape if n != 'loss' else (1, 1) for n in names])))
    loss = summed['loss'].reshape(())
    for n in SMALL:
        g = summed[n]
        grads[n] = g.reshape(W[n].shape) if n != 'conv_w' else lax.dynamic_slice_in_dim(
            g, chip * (LW // N_CHIPS), LW // N_CHIPS, axis=1)
    packs = [_pack([d[n] for n in SMALL]) for d in (W, grads, M1, V1)]
    shapes = [W[n].shape for n in SMALL]
    for d, res in zip((delta_w, new_m, new_v), adamw('adamw_small', *packs)):
        d.update(zip(SMALL, _unpack(res, shapes)))

    lead = lambda d: [d[n][None] for n in WEIGHTS]
    return (loss, grad_x[None], *lead(grads), *lead(delta_w), *lead(new_m), *lead(new_v))
```

```python
import functools
import math

import jax
import jax.numpy as jnp
from jax import lax
from jax.experimental import pallas as pl
from jax.experimental.pallas import tpu as pltpu

F32 = jnp.float32
BF16 = jnp.bfloat16
HEAD_DIM = 128
LANES = 128
LRU_C = 8.0
RMS_EPS = 1e-6
CONV_W = 4
ADAM_LR = 0.001
ADAM_B1 = 0.9
ADAM_B2 = 0.999
ADAM_EPS = 1e-08
ADAM_WD = 0.01
ADAM_STEP = 10
VMEM_LIMIT = 56 * 1024 * 1024
N_CHIPS = 4
MESH = pl.DeviceIdType.MESH
ANY = pl.BlockSpec(memory_space=pl.ANY)

WEIGHTS = ['g_mix', 'w_in', 'b_f', 'g_q', 'g_k', 'conv_w', 'conv_b', 'w_ra', 'b_ra', 'w_ri', 'b_ri', 'lam',
           'g_fox_out', 'g_lru_out', 'w_out', 'g_xattn', 'g_mem', 'w_cq', 'w_ckv', 'g_cq', 'g_ck', 'w_co', 'g_ffn',
           'w_gate_up', 'w_down']
BIG = ['w_in', 'w_out', 'w_cq', 'w_ckv', 'w_co', 'w_gate_up', 'w_down']
SMALL = [n for n in WEIGHTS if n not in BIG]


def _params(sem=None):
    if sem is None:
        return pltpu.CompilerParams(vmem_limit_bytes=VMEM_LIMIT)
    return pltpu.CompilerParams(dimension_semantics=sem, vmem_limit_bytes=VMEM_LIMIT)


def _tile(n, cands):
    for t in cands:
        if n % t == 0:
            return t
    return n


ROW_BLOCK_BYTES = 1 << 20


def _row_tile(n_rows, n_cols, min_rows=8):
    cands = [t for t in (512, 256, 128, 64, 32, 16, 8) if t >= min_rows and t * n_cols * 4 <= ROW_BLOCK_BYTES]
    return _tile(n_rows, cands or [min_rows])


def _sigmoid(z):
    return 1.0 / (1.0 + jnp.exp(-z))


def _softplus(z):
    return jnp.maximum(z, 0.0) + jnp.log(1.0 + jnp.exp(-jnp.abs(z)))


def _neg_expm1(z):
    series = -z * (1.0 + z * (0.5 + z * (1.0 / 6.0 + z * (1.0 / 24.0 + z * (1.0 / 120.0)))))
    return jnp.where(z > -0.25, series, 1.0 - jnp.exp(z))


_GELU_K = math.sqrt(2.0 / math.pi)


def _gelu_and_grad(z):
    inner = _GELU_K * (z + 0.044715 * z * z * z)
    t = jnp.tanh(inner)
    g = 0.5 * z * (1.0 + t)
    dg = 0.5 * (1.0 + t) + 0.5 * z * (1.0 - t * t) * _GELU_K * (1.0 + 3.0 * 0.044715 * z * z)
    return g, dg


def _rms(xv, g):
    r = lax.rsqrt(jnp.mean(xv * xv, axis=-1, keepdims=True) + RMS_EPS)
    return xv * r * g


def _rms_bwd(xv, g, dy):
    r = lax.rsqrt(jnp.mean(xv * xv, axis=-1, keepdims=True) + RMS_EPS)
    xh = xv * r
    dyg = dy * g
    dx = r * (dyg - xh * jnp.mean(dyg * xh, axis=-1, keepdims=True))
    return dx, jnp.sum(dy * xh, axis=0, keepdims=True)


def _heads(fn, n_heads, *arrs):
    outs = [fn(*[a[:, h * HEAD_DIM:(h + 1) * HEAD_DIM] for a in arrs]) for h in range(n_heads)]
    first = jnp.concatenate([o[0] for o in outs], axis=1) if n_heads > 1 else outs[0][0]
    rest = [functools.reduce(lambda p, q: p + q, [o[i] for o in outs]) for i in range(1, len(outs[0]))]
    return (first, *rest)


def _split3(v):
    hi = v.astype(BF16)
    r1 = v - hi.astype(F32)
    mid = r1.astype(BF16)
    lo = (r1 - mid.astype(F32)).astype(BF16)
    return hi, mid, lo


def _acc_out(ref, first, val):
    @pl.when(first)
    def _():
        ref[...] = val

    @pl.when(jnp.logical_not(first))
    def _():
        ref[...] += val


_DIMS = {'nn': (((1,), (0,)), ((), ())), 'nt': (((1,), (1,)), ((), ())), 'tn': (((0,), (0,)), ((), ()))}


MM_VMEM_BYTES = 36 * 1024 * 1024


MXU_FLOPS = 800e12
HBM_BYTES_S = 3.2e12
VMEM_ADD_BYTES_S = 8e12
STEP_S = 0.35e-6


def _k_tile(K, tm, tn, a, b, o_dtype, res):
    fixed = tm * tn * (2 * jnp.dtype(o_dtype).itemsize + 4 + (8 if res is not None else 0))
    per_k = 2 * (tm * a.dtype.itemsize + tn * b.dtype.itemsize)
    per_k += 2 * tm * (a.dtype.itemsize > 2) + 2 * tn * (b.dtype.itemsize > 2)
    units = K // LANES
    for d in sorted((d for d in range(1, units + 1) if units % d == 0), reverse=True):
        if fixed + d * LANES * per_k <= MM_VMEM_BYTES:
            return d * LANES
    return None


def _mm_tiles(M, N, K, k_span, a, b, o_dtype, res, tn_cands=(2048, 1024, 512, 256, 128)):
    best = None
    for tm in (2048, 1024, 512, 256, 128):
        for tn in tn_cands:
            if M % tm or N % tn:
                continue
            tk = _k_tile(k_span, tm, tn, a, b, o_dtype, res)
            if tk is None:
                continue
            nk = K // tk
            traffic = (M * K * a.dtype.itemsize * (N // tn) + K * N * b.dtype.itemsize * (M // tm)
                       + M * N * (jnp.dtype(o_dtype).itemsize + (4 if res is not None else 0)))
            work = 2.0 * M * N * K / MXU_FLOPS + (M * N * 4 * nk / VMEM_ADD_BYTES_S if nk > 1 else 0.0)
            t = max(work, traffic / HBM_BYTES_S) + (M // tm) * (N // tn) * nk * STEP_S
            if best is None or t < best[0]:
                best = (t, tm, tn, tk)
    assert best is not None, (M, N, K)
    return best[1:]


def _mm_call(name, a, b, mode, grid, a_spec, b_spec, o_spec, o_shape, o_dtype, acc_shape, res=None):
    nk = grid[2]
    dn = _DIMS[mode]

    def body(*refs):
        a_ref, b_ref = refs[:2]
        r_ref = refs[2] if res is not None else None
        o_ref = refs[3] if res is not None else refs[2]
        part = lax.dot_general(a_ref[...].astype(BF16), b_ref[...].astype(BF16), dn, preferred_element_type=F32)

        def finish(r):
            if r_ref is not None:
                r = r + r_ref[...]
            o_ref[...] = r.astype(o_dtype)

        if nk == 1:
            finish(part)
            return
        acc = refs[-1]
        k = pl.program_id(2)

        @pl.when(k == 0)
        def _():
            acc[...] = part

        @pl.when(k > 0)
        def _():
            acc[...] += part

        @pl.when(k == nk - 1)
        def _():
            finish(acc[...])

    ins = [a, b] + ([] if res is None else [res])
    specs = [a_spec, b_spec] + ([] if res is None else [o_spec])
    return pl.pallas_call(
        body, name=name, grid=grid, in_specs=specs, out_specs=o_spec,
        out_shape=jax.ShapeDtypeStruct(o_shape, o_dtype),
        scratch_shapes=[] if nk == 1 else [pltpu.VMEM(acc_shape, F32)],
        compiler_params=_params(('parallel', 'parallel', 'arbitrary')))(*ins)


def _mm(name, a, b, mode, o_dtype, res=None):
    if mode == 'tn':
        K, M = a.shape
    else:
        M, K = a.shape
    N = b.shape[0] if mode == 'nt' else b.shape[1]
    tm, tn, tk = _mm_tiles(M, N, K, K, a, b, o_dtype, res)
    a_spec = (pl.BlockSpec((tk, tm), lambda m, n, k: (k, m)) if mode == 'tn'
              else pl.BlockSpec((tm, tk), lambda m, n, k: (m, k)))
    b_spec = (pl.BlockSpec((tn, tk), lambda m, n, k: (n, k)) if mode == 'nt'
              else pl.BlockSpec((tk, tn), lambda m, n, k: (k, n)))
    o_spec = pl.BlockSpec((tm, tn), lambda m, n, k: (m, n))
    return _mm_call(name, a, b, mode, (M // tm, N // tn, K // tk), a_spec, b_spec, o_spec, (M, N), o_dtype,
                    (tm, tn), res)


def _mm_colsharded(name, a, w, o_dtype, res=None):
    M, K = a.shape
    J, _, Nj = w.shape
    tm, tn, tk = _mm_tiles(M, J * Nj, K, K, a, w, o_dtype, res,
                           tn_cands=[t for t in (2816, 1408, 1024, 512, 256, 128) if Nj % t == 0])
    per = Nj // tn
    return _mm_call(name, a, w, 'nn', (M // tm, J * per, K // tk),
                    pl.BlockSpec((tm, tk), lambda m, n, k: (m, k)),
                    pl.BlockSpec((None, tk, tn), lambda m, n, k: (n // per, k, n % per)),
                    pl.BlockSpec((tm, tn), lambda m, n, k: (m, n)), (M, J * Nj), o_dtype, (tm, tn), res)


def _planes_spec(arr, rows, cols, row_of, col_of):
    if arr.ndim == 2:
        return pl.BlockSpec((rows, cols), lambda m, n, k: (row_of(m, n, k), col_of(m, n, k)))
    per_plane = arr.shape[2] // cols
    return pl.BlockSpec((None, rows, cols),
                        lambda m, n, k: (col_of(m, n, k) // per_plane, row_of(m, n, k), col_of(m, n, k) % per_plane))


def _mm_colsharded_t(name, a, w, o_dtype):
    M = a.shape[-2]
    J, K, Nj = w.shape
    tm, tn, tk = _mm_tiles(M, K, J * Nj, Nj, a, w, o_dtype, None)
    per = Nj // tk
    return _mm_call(name, a, w, 'nt', (M // tm, K // tn, J * per),
                    _planes_spec(a, tm, tk, lambda m, n, k: m, lambda m, n, k: k),
                    pl.BlockSpec((None, tn, tk), lambda m, n, k: (k // per, n, k % per)),
                    pl.BlockSpec((tm, tn), lambda m, n, k: (m, n)), (M, K), o_dtype, (tm, tn))


def _mm_grad_colsharded(name, a, dy, J, o_dtype):
    S, M = a.shape
    Nj = dy.shape[-1] * (dy.shape[0] if dy.ndim == 3 else 1) // J
    tm, tn, tk = _mm_tiles(M, J * Nj, S, S, a, dy, o_dtype, None,
                           tn_cands=[t for t in (2816, 1408, 1024, 512, 256, 128) if Nj % t == 0])
    per = Nj // tn
    return _mm_call(name, a, dy, 'tn', (M // tm, J * per, S // tk),
                    pl.BlockSpec((tk, tm), lambda m, n, k: (k, m)),
                    _planes_spec(dy, tk, tn, lambda m, n, k: k, lambda m, n, k: n),
                    pl.BlockSpec((None, tm, tn), lambda m, n, k: (n // per, m, n % per)), (J, M, Nj), o_dtype, (tm, tn))


def _rows_call(name, body, n_rows, tr, ins, outs):
    return pl.pallas_call(
        body, name=name, grid=(n_rows // tr,), in_specs=[s for _, s in ins], out_specs=[s for _, _, s in outs],
        out_shape=[jax.ShapeDtypeStruct(sh, dt) for sh, dt, _ in outs],
        compiler_params=_params(('arbitrary',)))(*[a for a, _ in ins])


def _rb(tr, w, cb=0):
    return pl.BlockSpec((tr, w), lambda i: (i, cb))


def _fb(shape):
    nd = len(shape)
    return pl.BlockSpec(shape, lambda i: (0,) * nd)


def norm_fwd(name, xv, g):
    S, D = xv.shape
    tr = _tile(S, (256, 128))

    def body(x_ref, g_ref, o_ref):
        o_ref[...] = _rms(x_ref[...], g_ref[...]).astype(BF16)

    return _rows_call(name, body, S, tr, [(xv, _rb(tr, D)), (g, _fb((1, D)))], [((S, D), BF16, _rb(tr, D))])[0]


def norm_bwd(name, xv, g, dy, res=None, want_dx=True):
    S, D = xv.shape
    tr = _tile(S, (256, 128))

    def body(*refs):
        if res is None:
            x_ref, g_ref, dy_ref = refs[:3]
            outs = refs[3:]
            r_ref = None
        else:
            x_ref, g_ref, dy_ref, r_ref = refs[:4]
            outs = refs[4:]
        dx, dg = _rms_bwd(x_ref[...], g_ref[...], dy_ref[...])
        if r_ref is not None:
            dx = dx + r_ref[...]
        if want_dx:
            outs[0][...] = dx
            outs[1][...] = dx.astype(BF16)
        _acc_out(outs[-1], pl.program_id(0) == 0, dg)

    ins = [(xv, _rb(tr, D)), (g, _fb((1, D))), (dy, _rb(tr, D))] + ([] if res is None else [(res, _rb(tr, D))])
    outs = ([((S, D), F32, _rb(tr, D)), ((S, D), BF16, _rb(tr, D))] if want_dx else []) + [((1, D), F32, _fb((1, D)))]
    return _rows_call(name, body, S, tr, ins, outs)


def qkv_fwd(proj, g_q, g_k, FW):
    S = proj.shape[0]
    H = FW // HEAD_DIM
    tr = _tile(S, (256, 128))

    def body(q_ref, k_ref, v_ref, gq_ref, gk_ref, qo, ko, vo):
        qo[...] = _heads(lambda t: (_rms(t, gq_ref[...]),), H, q_ref[...])[0].astype(BF16)
        ko[...] = _heads(lambda t: (_rms(t, gk_ref[...]),), H, k_ref[...])[0].astype(BF16)
        vo[...] = v_ref[...].astype(BF16)

    o = ((S, FW), BF16, _rb(tr, FW))
    return _rows_call('qkv_fwd', body, S, tr,
                      [(proj, _rb(tr, FW, 0)), (proj, _rb(tr, FW, 1)), (proj, _rb(tr, FW, 2)),
                       (g_q, _fb((1, HEAD_DIM))), (g_k, _fb((1, HEAD_DIM)))], [o, o, o])


def qkv_bwd(proj, g_q, g_k, dqn, dkn, FW):
    S = proj.shape[0]
    H = FW // HEAD_DIM
    tr = _tile(S, (256, 128))

    def body(q_ref, k_ref, gq_ref, gk_ref, dq_ref, dk_ref, dqo, dko, dgq, dgk):
        dq, gq = _heads(lambda t, d: _rms_bwd(t, gq_ref[...], d), H, q_ref[...], dq_ref[...])
        dk, gk = _heads(lambda t, d: _rms_bwd(t, gk_ref[...], d), H, k_ref[...], dk_ref[...])
        dqo[...] = dq.astype(BF16)
        dko[...] = dk.astype(BF16)
        first = pl.program_id(0) == 0
        _acc_out(dgq, first, gq)
        _acc_out(dgk, first, gk)

    o = ((S, FW), BF16, _rb(tr, FW))
    og = ((1, HEAD_DIM), F32, _fb((1, HEAD_DIM)))
    return _rows_call('qkv_bwd', body, S, tr,
                      [(proj, _rb(tr, FW, 0)), (proj, _rb(tr, FW, 1)), (g_q, _fb((1, HEAD_DIM))),
                       (g_k, _fb((1, HEAD_DIM))), (dqn, _rb(tr, FW)), (dkn, _rb(tr, FW))], [o, o, og, og])


def _tri(n, upper):
    r = lax.broadcasted_iota(jnp.int32, (n, n), 0)
    c = lax.broadcasted_iota(jnp.int32, (n, n), 1)
    return jnp.where((c >= r) if upper else (c <= r), 1.0, 0.0).astype(BF16)


def _blocked_cumsum(val, S, blk, reverse):
    tri = _tri(blk, reverse)
    order = range(S // blk - 1, -1, -1) if reverse else range(S // blk)
    carry = jnp.zeros((1, LANES), F32)
    outs = {}
    for bi in order:
        part = val[bi * blk:(bi + 1) * blk]
        acc = carry
        for piece in _split3(part):
            acc = acc + jnp.dot(tri, piece, preferred_element_type=F32)
        outs[bi] = acc
        carry = carry + jnp.sum(part, axis=0, keepdims=True)
    return jnp.concatenate([outs[bi] for bi in range(S // blk)], axis=0)


def fgate_fwd(f_raw, b_f_pad):
    S = f_raw.shape[0]
    blk = _tile(S, (256, 128))

    def body(f_ref, b_ref, c_ref):
        z = f_ref[...] + b_ref[...]
        c_ref[...] = _blocked_cumsum(-_softplus(-z), S, blk, False)

    return pl.pallas_call(body, name='fgate_fwd', grid=(1,), in_specs=[_fb((S, LANES)), _fb((1, LANES))],
                          out_specs=_fb((S, LANES)), out_shape=jax.ShapeDtypeStruct((S, LANES), F32),
                          compiler_params=_params(('arbitrary',)))(f_raw, b_f_pad)


def fgate_bwd(f_raw, b_f_pad, dc, H):
    S = f_raw.shape[0]
    blk = _tile(S, (256, 128))

    def body(f_ref, b_ref, dc_ref, df_ref, db_ref):
        z = f_ref[...] + b_ref[...]
        dlogf = _blocked_cumsum(dc_ref[...], S, blk, True)
        lane = lax.broadcasted_iota(jnp.int32, (S, LANES), 1)
        df = jnp.where(lane < H, dlogf * _sigmoid(-z), 0.0)
        df_ref[...] = df.astype(BF16)
        db_ref[...] = jnp.sum(df, axis=0, keepdims=True)

    return pl.pallas_call(body, name='fgate_bwd', grid=(1,),
                          in_specs=[_fb((S, LANES)), _fb((1, LANES)), _fb((S, LANES))],
                          out_specs=[_fb((S, LANES)), _fb((1, LANES))],
                          out_shape=[jax.ShapeDtypeStruct((S, LANES), BF16), jax.ShapeDtypeStruct((1, LANES), F32)],
                          compiler_params=_params(('arbitrary',)))(f_raw, b_f_pad, dc)


def _fox_logits(q, k, c_blk, ct_blk, h, T, diagonal):
    s = lax.dot_general(q, k, _DIMS['nt'], preferred_element_type=F32) * (1.0 / math.sqrt(HEAD_DIM))
    lane = lax.broadcasted_iota(jnp.int32, c_blk.shape, 1)
    cq = jnp.sum(jnp.where(lane == h, c_blk, 0.0), axis=1, keepdims=True)
    sub = lax.broadcasted_iota(jnp.int32, ct_blk.shape, 0)
    ck = jnp.sum(jnp.where(sub == h, ct_blk, 0.0), axis=0, keepdims=True)
    s = s + cq - ck
    if not diagonal:
        return s
    rows = lax.broadcasted_iota(jnp.int32, (T, T), 0)
    cols = lax.broadcasted_iota(jnp.int32, (T, T), 1)
    return jnp.where(cols <= rows, s, -jnp.inf)


def _below_and_on_diagonal(q_blk, k_blk, step):
    @pl.when(k_blk < q_blk)
    def _():
        step(False)

    @pl.when(k_blk == q_blk)
    def _():
        step(True)


def fox_fwd(qn, kn, vb, c, ct, T):
    S, FW = qn.shape
    H = FW // HEAD_DIM
    Hp = ct.shape[0]
    n = S // T

    HB = _tile(H, (4, 2, 1))
    W2 = HB * HEAD_DIM

    def body(q_ref, k_ref, v_ref, c_ref, ct_ref, o_ref, lse_ref, m_s, l_s, acc_s):
        hb, i, j = pl.program_id(0), pl.program_id(1), pl.program_id(2)

        @pl.when(j == 0)
        def _():
            m_s[...] = jnp.full_like(m_s, -jnp.inf)
            l_s[...] = jnp.zeros_like(l_s)
            acc_s[...] = jnp.zeros_like(acc_s)

        def step(diagonal):
            for hh in range(HB):
                sl = slice(hh * HEAD_DIM, (hh + 1) * HEAD_DIM)
                s = _fox_logits(q_ref[:, sl], k_ref[:, sl], c_ref[...], ct_ref[...], hb * HB + hh, T, diagonal)
                m_old = m_s[hh]
                m_new = jnp.maximum(m_old, jnp.max(s, axis=1, keepdims=True))
                alpha = jnp.exp(m_old - m_new)
                p = jnp.exp(s - m_new)
                l_s[hh] = alpha * l_s[hh] + jnp.sum(p, axis=1, keepdims=True)
                acc_s[hh] = alpha * acc_s[hh] + jnp.dot(p.astype(BF16), v_ref[:, sl], preferred_element_type=F32)
                m_s[hh] = m_new

        _below_and_on_diagonal(i, j, step)

        @pl.when(j == i)
        def _():
            for hh in range(HB):
                o_ref[:, hh * HEAD_DIM:(hh + 1) * HEAD_DIM] = acc_s[hh] / l_s[hh]
                lse_ref[hh] = jnp.broadcast_to(m_s[hh] + jnp.log(l_s[hh]), (T, LANES))

    qs = pl.BlockSpec((T, W2), lambda h, i, j: (i, h))
    ks = pl.BlockSpec((T, W2), lambda h, i, j: (jnp.minimum(j, i), h))
    return pl.pallas_call(
        body, name='fox_fwd', grid=(H // HB, n, n),
        in_specs=[qs, ks, ks, pl.BlockSpec((T, LANES), lambda h, i, j: (i, 0)),
                  pl.BlockSpec((Hp, T), lambda h, i, j: (0, jnp.minimum(j, i)))],
        out_specs=[qs, pl.BlockSpec((HB, T, LANES), lambda h, i, j: (h, i, 0))],
        out_shape=[jax.ShapeDtypeStruct((S, FW), F32), jax.ShapeDtypeStruct((H, S, LANES), F32)],
        scratch_shapes=[pltpu.VMEM((HB, T, 1), F32), pltpu.VMEM((HB, T, 1), F32), pltpu.VMEM((HB, T, HEAD_DIM), F32)],
        compiler_params=_params(('parallel', 'parallel', 'arbitrary')))(qn, kn, vb, c, ct)


def _fox_p_ds(q_ref, k_ref, v_ref, do_ref, c_ref, ct_ref, lse_ref, dl_ref, h, T, diagonal):
    s = _fox_logits(q_ref[...], k_ref[...], c_ref[...], ct_ref[...], h, T, diagonal)
    p = jnp.exp(s - jnp.tile(lse_ref[...], (1, T // LANES)))
    dp = lax.dot_general(do_ref[...], v_ref[...], _DIMS['nt'], preferred_element_type=F32)
    ds = p * (dp - jnp.tile(dl_ref[...], (1, T // LANES)))
    return p, dp, ds


def fox_bwd_q(qn, kn, vb, do, c, ct, lse, dl, T):
    S, FW = qn.shape
    H = FW // HEAD_DIM
    Hp = ct.shape[0]
    n = S // T
    HB = _tile(H, (4, 2, 1))
    W2 = HB * HEAD_DIM

    def body(q_ref, k_ref, v_ref, do_ref, c_ref, ct_ref, lse_ref, dl_ref, dq_ref, dl2_ref, acc_s, rs_s):
        hb, i, j = pl.program_id(0), pl.program_id(1), pl.program_id(2)

        @pl.when(j == 0)
        def _():
            acc_s[...] = jnp.zeros_like(acc_s)
            rs_s[...] = jnp.zeros_like(rs_s)

        def step(diagonal):
            for hh in range(HB):
                sl = slice(hh * HEAD_DIM, (hh + 1) * HEAD_DIM)
                p, dp, ds = _fox_p_ds(q_ref.at[:, sl], k_ref.at[:, sl], v_ref.at[:, sl], do_ref.at[:, sl], c_ref, ct_ref,
                                      lse_ref.at[hh], dl_ref.at[hh], hb * HB + hh, T, diagonal)
                acc_s[hh] += jnp.dot(ds.astype(BF16), k_ref[:, sl], preferred_element_type=F32)
                rs_s[hh] += jnp.sum(p * dp, axis=1, keepdims=True)

        _below_and_on_diagonal(i, j, step)

        @pl.when(j == i)
        def _():
            for hh in range(HB):
                dq_ref[:, hh * HEAD_DIM:(hh + 1) * HEAD_DIM] = acc_s[hh] * (1.0 / math.sqrt(HEAD_DIM))
                dl2_ref[hh] = jnp.broadcast_to(rs_s[hh], (T, LANES))

    qs = pl.BlockSpec((T, W2), lambda h, i, j: (i, h))
    ks = pl.BlockSpec((T, W2), lambda h, i, j: (jnp.minimum(j, i), h))
    st = pl.BlockSpec((HB, T, LANES), lambda h, i, j: (h, i, 0))
    return pl.pallas_call(
        body, name='fox_bwd_q', grid=(H // HB, n, n),
        in_specs=[qs, ks, ks, qs, pl.BlockSpec((T, LANES), lambda h, i, j: (i, 0)),
                  pl.BlockSpec((Hp, T), lambda h, i, j: (0, jnp.minimum(j, i))), st, st],
        out_specs=[qs, st], out_shape=[jax.ShapeDtypeStruct((S, FW), F32), jax.ShapeDtypeStruct((H, S, LANES), F32)],
        scratch_shapes=[pltpu.VMEM((HB, T, HEAD_DIM), F32), pltpu.VMEM((HB, T, 1), F32)],
        compiler_params=_params(('parallel', 'parallel', 'arbitrary')))(qn, kn, vb, do, c, ct, lse, dl)


def fox_bwd_kv(qn, kn, vb, do, c, ct, lse, dl, T):
    S, FW = qn.shape
    H = FW // HEAD_DIM
    Hp = ct.shape[0]
    n = S // T

    HB = _tile(H, (4, 2, 1))
    W2 = HB * HEAD_DIM

    def body(q_ref, k_ref, v_ref, do_ref, c_ref, ct_ref, lse_ref, dl_ref, dk_ref, dv_ref, dc_ref, dk_s, dv_s, dc_s):
        hb, j, i = pl.program_id(0), pl.program_id(1), pl.program_id(2)

        @pl.when(i == 0)
        def _():
            dk_s[...] = jnp.zeros_like(dk_s)
            dv_s[...] = jnp.zeros_like(dv_s)
            dc_s[...] = jnp.zeros_like(dc_s)

        def step(diagonal):
            for hh in range(HB):
                sl = slice(hh * HEAD_DIM, (hh + 1) * HEAD_DIM)
                p, _, ds = _fox_p_ds(q_ref.at[:, sl], k_ref.at[:, sl], v_ref.at[:, sl], do_ref.at[:, sl], c_ref, ct_ref,
                                     lse_ref.at[hh], dl_ref.at[hh], hb * HB + hh, T, diagonal)
                dv_s[hh] += lax.dot_general(p.astype(BF16), do_ref[:, sl], _DIMS['tn'], preferred_element_type=F32)
                dk_s[hh] += lax.dot_general(ds.astype(BF16), q_ref[:, sl], _DIMS['tn'], preferred_element_type=F32)
                dc_s[hh] += jnp.sum(ds, axis=0, keepdims=True)

        _below_and_on_diagonal(i, j, step)

        @pl.when(i == n - 1)
        def _():
            for hh in range(HB):
                sl = slice(hh * HEAD_DIM, (hh + 1) * HEAD_DIM)
                dk_ref[:, sl] = dk_s[hh] * (1.0 / math.sqrt(HEAD_DIM))
                dv_ref[:, sl] = dv_s[hh].astype(BF16)
                dc_ref[hh] = -dc_s[hh]

    qs = pl.BlockSpec((T, W2), lambda h, j, i: (jnp.maximum(i, j), h))
    ks = pl.BlockSpec((T, W2), lambda h, j, i: (j, h))
    st = pl.BlockSpec((HB, T, LANES), lambda h, j, i: (h, jnp.maximum(i, j), 0))
    return pl.pallas_call(
        body, name='fox_bwd_kv', grid=(H // HB, n, n),
        in_specs=[qs, ks, ks, qs, pl.BlockSpec((T, LANES), lambda h, j, i: (jnp.maximum(i, j), 0)),
                  pl.BlockSpec((Hp, T), lambda h, j, i: (0, j)), st, st],
        out_specs=[ks, ks, pl.BlockSpec((HB, 1, T), lambda h, j, i: (h, 0, j))],
        out_shape=[jax.ShapeDtypeStruct((S, FW), F32), jax.ShapeDtypeStruct((S, FW), BF16),
                   jax.ShapeDtypeStruct((H, 1, S), F32)],
        scratch_shapes=[pltpu.VMEM((HB, T, HEAD_DIM), F32), pltpu.VMEM((HB, T, HEAD_DIM), F32),
                        pltpu.VMEM((HB, 1, T), F32)],
        compiler_params=_params(('parallel', 'parallel', 'arbitrary')))(qn, kn, vb, do, c, ct, lse, dl)


def _shift_down(v, d, rows, fill):
    return jnp.where(rows >= d, pltpu.roll(v, d, 0), fill)


def _shift_up(v, d, rows, S, fill):
    return jnp.where(rows < S - d, pltpu.roll(v, S - d, 0), fill)


SUBLANES = 8


def _scan_by_doubling(a, b, pos, span, reverse):
    n = a.shape[0]
    d = 1
    while d < span:
        if reverse:
            keep = pos < span - d
            a_s, b_s = jnp.where(keep, pltpu.roll(a, n - d, 0), 1.0), jnp.where(keep, pltpu.roll(b, n - d, 0), 0.0)
        else:
            keep = pos >= d
            a_s, b_s = jnp.where(keep, pltpu.roll(a, d, 0), 1.0), jnp.where(keep, pltpu.roll(b, d, 0), 0.0)
        b = a * b_s + b
        a = a * a_s
        d *= 2
    return a, b


def _scan(a, b, rows, S, reverse, scr):
    groups = S // SUBLANES
    a, b = _scan_by_doubling(a, b, jnp.bitwise_and(rows, SUBLANES - 1), SUBLANES, reverse)
    scr[0][...] = a
    scr[1][...] = b
    edge = 0 if reverse else SUBLANES - 1
    a_g = scr[0][pl.ds(edge, groups, stride=SUBLANES), :]
    b_g = scr[1][pl.ds(edge, groups, stride=SUBLANES), :]
    g_pos = lax.broadcasted_iota(jnp.int32, (groups, LANES), 0)
    _, h_g = _scan_by_doubling(a_g, b_g, g_pos, groups, reverse)
    if reverse:
        carry = jnp.where(g_pos < groups - 1, pltpu.roll(h_g, groups - 1, 0), 0.0)
    else:
        carry = jnp.where(g_pos >= 1, pltpu.roll(h_g, 1, 0), 0.0)
    for r in range(SUBLANES):
        scr[0][pl.ds(r, groups, stride=SUBLANES), :] = carry
    return b + a * scr[0][...]


def _lru_forward(u, cw, cb, wra, bra, wri, bri, lam, rows, scr):
    uc = cb + cw[CONV_W - 1] * u
    for d in range(1, CONV_W):
        uc = uc + cw[CONV_W - 1 - d] * _shift_down(u, d, rows, 0.0)
    ucb = uc.astype(BF16)
    r = _sigmoid(jnp.dot(ucb, wra.astype(BF16), preferred_element_type=F32) + bra)
    ig = _sigmoid(jnp.dot(ucb, wri.astype(BF16), preferred_element_type=F32) + bri)
    sp = _softplus(-lam)
    log_a = -LRU_C * r * sp
    a = jnp.exp(log_a)
    sq = jnp.sqrt(_neg_expm1(2.0 * log_a))
    iu = ig * uc
    hseq = _scan(a, sq * iu, rows, u.shape[0], False, scr)
    return uc, ucb, r, ig, sp, a, sq, iu, hseq


def _lru_specs(S, n_u, n_g):
    col = lambda off: pl.BlockSpec((S, LANES), lambda cbk: (0, off + cbk))
    vec = pl.BlockSpec((1, LANES), lambda cbk: (0, cbk))
    mat = pl.BlockSpec((None, LANES, LANES), lambda cbk: (cbk, 0, 0))
    cw = pl.BlockSpec((CONV_W, LANES), lambda cbk: (0, cbk))
    return col, vec, mat, cw


def lru_fwd(proj, conv_w, conv_b, w_ra, b_ra, w_ri, b_ri, lam, u_off, g_off):
    S = proj.shape[0]
    nb = w_ra.shape[0]
    col, vec, mat, cws = _lru_specs(S, u_off, g_off)

    def body(u_ref, g_ref, cw_ref, cb_ref, wra_ref, bra_ref, wri_ref, bri_ref, lam_ref, y_ref, scr0, scr1):
        rows = lax.broadcasted_iota(jnp.int32, (S, LANES), 0)
        cw = [cw_ref[t:t + 1, :] for t in range(CONV_W)]
        hseq = _lru_forward(u_ref[...], cw, cb_ref[...], wra_ref[...], bra_ref[...], wri_ref[...],
                            bri_ref[...], lam_ref[...], rows, (scr0, scr1))[-1]
        y_ref[...] = hseq * _gelu_and_grad(g_ref[...])[0]

    return pl.pallas_call(
        body, name='lru_fwd', grid=(nb,),
        in_specs=[col(u_off), col(g_off), cws, vec, mat, vec, mat, vec, vec], out_specs=col(0),
        out_shape=jax.ShapeDtypeStruct((S, nb * LANES), F32),
        scratch_shapes=[pltpu.VMEM((S, LANES), F32), pltpu.VMEM((S, LANES), F32)],
        compiler_params=_params(('parallel',)))(proj, proj, conv_w, conv_b, w_ra, b_ra, w_ri, b_ri, lam)


def lru_bwd(proj, dy, conv_w, conv_b, w_ra, b_ra, w_ri, b_ri, lam, u_off, g_off):
    S = proj.shape[0]
    nb = w_ra.shape[0]
    LW = nb * LANES
    col, vec, mat, cws = _lru_specs(S, u_off, g_off)

    def body(u_ref, g_ref, dy_ref, cw_ref, cb_ref, wra_ref, bra_ref, wri_ref, bri_ref, lam_ref,
             du_ref, dg_ref, dcw_ref, dcb_ref, dwra_ref, dbra_ref, dwri_ref, dbri_ref, dlam_ref, scr0, scr1):
        rows = lax.broadcasted_iota(jnp.int32, (S, LANES), 0)
        u, lam_v = u_ref[...], lam_ref[...]
        cw = [cw_ref[t:t + 1, :] for t in range(CONV_W)]
        wra, wri = wra_ref[...].astype(BF16), wri_ref[...].astype(BF16)
        uc, ucb, r, ig, sp, a, sq, iu, hseq = _lru_forward(u, cw, cb_ref[...], wra, bra_ref[...], wri, bri_ref[...],
                                                           lam_v, rows, (scr0, scr1))
        gl, dgl = _gelu_and_grad(g_ref[...])
        dy_v = dy_ref[...]
        dg_ref[...] = (dy_v * hseq * dgl).astype(BF16)
        G = _scan(_shift_up(a, 1, rows, S, 0.0), dy_v * gl, rows, S, True, (scr0, scr1))
        da = G * _shift_down(hseq, 1, rows, 0.0)
        diu = G * sq
        dsq = G * iu
        dlog_a = da * a - dsq * a * a / jnp.maximum(sq, 1e-30)
        dr = dlog_a * (-LRU_C * sp)
        dsp = jnp.sum(dlog_a * (-LRU_C * r), axis=0, keepdims=True)
        dlam_ref[...] = -dsp * _sigmoid(-lam_v)
        dzr = dr * r * (1.0 - r)
        dzi = diu * uc * ig * (1.0 - ig)
        dzrb, dzib = dzr.astype(BF16), dzi.astype(BF16)
        duc = (diu * ig + lax.dot_general(dzrb, wra, _DIMS['nt'], preferred_element_type=F32)
               + lax.dot_general(dzib, wri, _DIMS['nt'], preferred_element_type=F32))
        dwra_ref[...] = lax.dot_general(ucb, dzrb, _DIMS['tn'], preferred_element_type=F32)
        dwri_ref[...] = lax.dot_general(ucb, dzib, _DIMS['tn'], preferred_element_type=F32)
        dbra_ref[...] = jnp.sum(dzr, axis=0, keepdims=True)
        dbri_ref[...] = jnp.sum(dzi, axis=0, keepdims=True)
        dcb_ref[...] = jnp.sum(duc, axis=0, keepdims=True)
        du = cw[CONV_W - 1] * duc
        dcw_ref[CONV_W - 1:CONV_W, :] = jnp.sum(duc * u, axis=0, keepdims=True)
        for d in range(1, CONV_W):
            du = du + cw[CONV_W - 1 - d] * _shift_up(duc, d, rows, S, 0.0)
            dcw_ref[CONV_W - 1 - d:CONV_W - d, :] = jnp.sum(duc * _shift_down(u, d, rows, 0.0), axis=0, keepdims=True)
        du_ref[...] = du.astype(BF16)

    sd = jax.ShapeDtypeStruct
    return pl.pallas_call(
        body, name='lru_bwd', grid=(nb,),
        in_specs=[col(u_off), col(g_off), col(0), cws, vec, mat, vec, mat, vec, vec],
        out_specs=[col(0), col(0), cws, vec, mat, vec, mat, vec, vec],
        out_shape=[sd((S, LW), BF16), sd((S, LW), BF16), sd((CONV_W, LW), F32), sd((1, LW), F32),
                   sd((nb, LANES, LANES), F32), sd((1, LW), F32), sd((nb, LANES, LANES), F32), sd((1, LW), F32),
                   sd((1, LW), F32)],
        scratch_shapes=[pltpu.VMEM((S, LANES), F32), pltpu.VMEM((S, LANES), F32)],
        compiler_params=_params(('parallel',)))(proj, proj, dy, conv_w, conv_b, w_ra, b_ra, w_ri, b_ri, lam)


def mix_fwd(o_fox, y_lru, g_fox, g_lru):
    S, FW = o_fox.shape
    tr = _tile(S, (256, 128))

    def body(o_ref, y_ref, gf_ref, gl_ref, m_ref):
        m_ref[...] = jnp.concatenate([_rms(o_ref[...], gf_ref[...]), _rms(y_ref[...], gl_ref[...])],
                                     axis=1).astype(BF16)

    return _rows_call('mix_fwd', body, S, tr,
                      [(o_fox, _rb(tr, FW)), (y_lru, _rb(tr, FW)), (g_fox, _fb((1, FW))), (g_lru, _fb((1, FW)))],
                      [((S, 2 * FW), BF16, _rb(tr, 2 * FW))])[0]


def mix_bwd(o_fox, y_lru, g_fox, g_lru, dmix):
    S, FW = o_fox.shape
    H = FW // HEAD_DIM
    tr = _tile(S, (256, 128))

    def body(o_ref, y_ref, gf_ref, gl_ref, df_ref, dl_ref, do_ref, dlt_ref, dy_ref, dgf_ref, dgl_ref):
        o = o_ref[...]
        do, dgf = _rms_bwd(o, gf_ref[...], df_ref[...])
        dyl, dgl = _rms_bwd(y_ref[...], gl_ref[...], dl_ref[...])
        do_ref[...] = do.astype(BF16)
        dy_ref[...] = dyl
        prod = do * o
        for h in range(H):
            dlt_ref[h] = jnp.broadcast_to(
                jnp.sum(prod[:, h * HEAD_DIM:(h + 1) * HEAD_DIM], axis=1, keepdims=True), (tr, LANES))
        first = pl.program_id(0) == 0
        _acc_out(dgf_ref, first, dgf)
        _acc_out(dgl_ref, first, dgl)

    g = _fb((1, FW))
    return _rows_call('mix_bwd', body, S, tr,
                      [(o_fox, _rb(tr, FW)), (y_lru, _rb(tr, FW)), (g_fox, g), (g_lru, g), (dmix, _rb(tr, FW, 0)),
                       (dmix, _rb(tr, FW, 1))],
                      [((S, FW), BF16, _rb(tr, FW)), ((H, S, LANES), F32, pl.BlockSpec((H, tr, LANES), lambda i: (0, i, 0))),
                       ((S, FW), F32, _rb(tr, FW)), ((1, FW), F32, g), ((1, FW), F32, g)])


def _xattn_heads(cq_raw, ckv, g_cq, g_ck, XW):
    out = []
    for h in range(XW // HEAD_DIM):
        sl = slice(h * HEAD_DIM, (h + 1) * HEAD_DIM)
        out.append((cq_raw[:, sl], _rms(cq_raw[:, sl], g_cq), ckv[:, sl], _rms(ckv[:, sl], g_ck),
                    ckv[:, XW + h * HEAD_DIM:XW + (h + 1) * HEAD_DIM].astype(BF16)))
    return out


def xattn_fwd(cq_raw, ckv, g_cq, g_ck):
    S, XW = cq_raw.shape
    M = ckv.shape[0]
    tr = _tile(S, (512, 256, 128))

    def body(q_ref, kv_ref, gq_ref, gk_ref, o_ref):
        outs = []
        for _, qn, _, kn, v in _xattn_heads(q_ref[...], kv_ref[...], gq_ref[...], gk_ref[...], XW):
            s = lax.dot_general(qn.astype(BF16), kn.astype(BF16), _DIMS['nt'], preferred_element_type=F32)
            s = s / math.sqrt(HEAD_DIM)
            p = jnp.exp(s - jnp.max(s, axis=1, keepdims=True))
            p = p / jnp.sum(p, axis=1, keepdims=True)
            outs.append(jnp.dot(p.astype(BF16), v, preferred_element_type=F32))
        o_ref[...] = jnp.concatenate(outs, axis=1).astype(BF16)

    g = _fb((1, HEAD_DIM))
    return _rows_call('xattn_fwd', body, S, tr,
                      [(cq_raw, _rb(tr, XW)), (ckv, _fb((M, 2 * XW))), (g_cq, g), (g_ck, g)],
                      [((S, XW), BF16, _rb(tr, XW))])[0]


def xattn_bwd(cq_raw, ckv, g_cq, g_ck, do):
    S, XW = cq_raw.shape
    M = ckv.shape[0]
    tr = _tile(S, (512, 256, 128))
    n = S // tr

    def body(q_ref, kv_ref, gq_ref, gk_ref, do_ref, dq_ref, dkv_ref, dgq_ref, dgk_ref):
        i = pl.program_id(0)
        do_v = do_ref[...]
        dqs, dkn, dvs = [], [], []
        dgq = jnp.zeros((1, HEAD_DIM), F32)
        for h, (q_raw, qn, _, kn, v) in enumerate(_xattn_heads(q_ref[...], kv_ref[...], gq_ref[...], gk_ref[...], XW)):
            qb, kb = qn.astype(BF16), kn.astype(BF16)
            doh = do_v[:, h * HEAD_DIM:(h + 1) * HEAD_DIM]
            s = lax.dot_general(qb, kb, _DIMS['nt'], preferred_element_type=F32) / math.sqrt(HEAD_DIM)
            p = jnp.exp(s - jnp.max(s, axis=1, keepdims=True))
            p = p / jnp.sum(p, axis=1, keepdims=True)
            dp = lax.dot_general(doh, v, _DIMS['nt'], preferred_element_type=F32)
            ds = (p * (dp - jnp.sum(p * dp, axis=1, keepdims=True)) / math.sqrt(HEAD_DIM)).astype(BF16)
            dvs.append(lax.dot_general(p.astype(BF16), doh, _DIMS['tn'], preferred_element_type=F32))
            dkn.append(lax.dot_general(ds, qb, _DIMS['tn'], preferred_element_type=F32))
            dq, g1 = _rms_bwd(q_raw, gq_ref[...], jnp.dot(ds, kb, preferred_element_type=F32))
            dqs.append(dq)
            dgq = dgq + g1
        dq_ref[...] = jnp.concatenate(dqs, axis=1).astype(BF16)
        first = i == 0
        _acc_out(dgq_ref, first, dgq)
        _acc_out(dkv_ref, first, jnp.concatenate(dkn + dvs, axis=1))

        @pl.when(i == n - 1)
        def _():
            kv = kv_ref[...]
            acc = dkv_ref[...]
            dk, gk = _heads(lambda t, d: _rms_bwd(t, gk_ref[...], d), XW // HEAD_DIM, kv[:, :XW], acc[:, :XW])
            dkv_ref[:, :XW] = dk
            dgk_ref[...] = gk

    g = _fb((1, HEAD_DIM))
    return _rows_call('xattn_bwd', body, S, tr,
                      [(cq_raw, _rb(tr, XW)), (ckv, _fb((M, 2 * XW))), (g_cq, g), (g_ck, g), (do, _rb(tr, XW))],
                      [((S, XW), BF16, _rb(tr, XW)), ((M, 2 * XW), F32, _fb((M, 2 * XW))), ((1, HEAD_DIM), F32, g),
                       ((1, HEAD_DIM), F32, g)])


def gate_up_fwd(hf, w, F):
    S, D = hf.shape
    J, _, Nj = w.shape
    tm = _tile(S, (1024, 512, 256, 128))
    tn = _tile(Nj, (256, 128))
    per = Nj // tn
    half = J // 2 * per

    def body(a_ref, bg_ref, bu_ref, gu_ref, act_ref):
        a = a_ref[...]
        g = jnp.dot(a, bg_ref[...], preferred_element_type=F32)
        u = jnp.dot(a, bu_ref[...], preferred_element_type=F32)
        gu_ref[0] = g
        gu_ref[1] = u
        act_ref[...] = (g * _sigmoid(g) * u).astype(BF16)

    return pl.pallas_call(
        body, name='proj_gate_up', grid=(S // tm, half),
        in_specs=[pl.BlockSpec((tm, D), lambda m, n: (m, 0)),
                  pl.BlockSpec((None, D, tn), lambda m, n: (n // per, 0, n % per)),
                  pl.BlockSpec((None, D, tn), lambda m, n: ((n + half) // per, 0, n % per))],
        out_specs=[pl.BlockSpec((2, tm, tn), lambda m, n: (0, m, n)), pl.BlockSpec((tm, tn), lambda m, n: (m, n))],
        out_shape=[jax.ShapeDtypeStruct((2, S, F), F32), jax.ShapeDtypeStruct((S, F), BF16)],
        compiler_params=_params(('parallel', 'parallel')))(hf, w, w)


def down_bwd_x(dyb, w_down, gu, after):
    S, D = dyb.shape
    F = w_down.shape[0]
    tm = _tile(S, (1024, 512, 256, 128))
    tn = _tile(F, (512, 256, 128))

    def body(a_ref, b_ref, gu_ref, after_ref, o_ref):
        da = lax.dot_general(a_ref[...], b_ref[...], _DIMS['nt'], preferred_element_type=F32)
        g = gu_ref[0]
        sg = _sigmoid(g)
        o_ref[0] = (da * gu_ref[1] * sg * (1.0 + g * (1.0 - sg))).astype(BF16)
        o_ref[1] = (da * g * sg).astype(BF16)

    planes = pl.BlockSpec((2, tm, tn), lambda m, n: (0, m, n))
    return pl.pallas_call(
        body, name='bwd_down_x', grid=(S // tm, F // tn),
        in_specs=[pl.BlockSpec((tm, D), lambda m, n: (m, 0)), pl.BlockSpec((tn, D), lambda m, n: (n, 0)), planes, ANY],
        out_specs=planes, out_shape=jax.ShapeDtypeStruct((2, S, F), BF16),
        compiler_params=_params(('parallel', 'parallel')))(dyb, w_down, gu, after)


def down_fwd_loss(act, w_down, x2, target):
    S, F = act.shape
    D = w_down.shape[1]
    tm, tn, tk = _mm_tiles(S, D, F, F, act, w_down, F32, x2, tn_cands=(512, 256, 128))
    nk = F // tk

    def body(a_ref, b_ref, x_ref, t_ref, d_ref, db_ref, l_ref, acc):
        m, n, k = pl.program_id(0), pl.program_id(1), pl.program_id(2)
        part = jnp.dot(a_ref[...], b_ref[...], preferred_element_type=F32)

        @pl.when(k == 0)
        def _():
            acc[...] = part

        @pl.when(k > 0)
        def _():
            acc[...] += part

        @pl.when(k == nk - 1)
        def _():
            err = acc[...] + x_ref[...] - t_ref[...]
            d = err * (1.0 / D)
            d_ref[...] = d
            db_ref[...] = d.astype(BF16)
            tot = jnp.sum(jnp.sum(err * err, axis=1, keepdims=True), axis=0, keepdims=True) * (0.5 / D)
            _acc_out(l_ref, jnp.logical_and(m == 0, n == 0), jnp.broadcast_to(tot, (1, LANES)))

    tile = pl.BlockSpec((tm, tn), lambda m, n, k: (m, n))
    return pl.pallas_call(
        body, name='proj_down', grid=(S // tm, D // tn, nk),
        in_specs=[pl.BlockSpec((tm, tk), lambda m, n, k: (m, k)), pl.BlockSpec((tk, tn), lambda m, n, k: (k, n)), tile, tile],
        out_specs=[tile, tile, pl.BlockSpec((1, LANES), lambda m, n, k: (0, 0))],
        out_shape=[jax.ShapeDtypeStruct((S, D), F32), jax.ShapeDtypeStruct((S, D), BF16),
                   jax.ShapeDtypeStruct((1, LANES), F32)],
        scratch_shapes=[pltpu.VMEM((tm, tn), F32)],
        compiler_params=_params(('arbitrary', 'arbitrary', 'arbitrary')))(act, w_down, x2, target)


def swiglu_bwd(gu, dact, F, after):
    S = gu.shape[1]
    tr = _tile(S, (256, 128))
    tf = _tile(F, (1408, 1024, 512, 256, 128))
    nf = F // tf

    def body(gu_ref, da_ref, after_ref, o_ref):
        g, da = gu_ref[0], da_ref[...]
        sg = _sigmoid(g)
        o_ref[0] = (da * gu_ref[1] * sg * (1.0 + g * (1.0 - sg))).astype(BF16)
        o_ref[1] = (da * g * sg).astype(BF16)

    planes = pl.BlockSpec((2, tr, tf), lambda i, n: (0, i, n))
    return pl.pallas_call(
        body, name='swiglu_bwd', grid=(S // tr, nf),
        in_specs=[planes, pl.BlockSpec((tr, tf), lambda i, n: (i, n)), ANY],
        out_specs=planes, out_shape=jax.ShapeDtypeStruct((2, S, F), BF16),
        compiler_params=_params(('parallel', 'parallel')))(gu, dact, after)


def loss_head(y, target):
    S, D = y.shape
    tr = _tile(S, (256, 128))

    def body(y_ref, t_ref, d_ref, db_ref, l_ref):
        err = y_ref[...] - t_ref[...]
        d = err * (1.0 / D)
        d_ref[...] = d
        db_ref[...] = d.astype(BF16)
        part = jnp.sum(jnp.sum(err * err, axis=1, keepdims=True), axis=0, keepdims=True) * (0.5 / D)
        _acc_out(l_ref, pl.program_id(0) == 0, jnp.broadcast_to(part, (1, LANES)))

    return _rows_call('loss_head', body, S, tr, [(y, _rb(tr, D)), (target, _rb(tr, D))],
                      [((S, D), F32, _rb(tr, D)), ((S, D), BF16, _rb(tr, D)), ((1, LANES), F32, _fb((1, LANES)))])


def _adamw_math(w, gv, m, v):
    mn = ADAM_B1 * m + (1.0 - ADAM_B1) * gv
    vn = ADAM_B2 * v + (1.0 - ADAM_B2) * (gv * gv)
    m_hat = mn / (1.0 - ADAM_B1 ** ADAM_STEP)
    v_hat = vn / (1.0 - ADAM_B2 ** ADAM_STEP)
    return -ADAM_LR * (m_hat / (jnp.sqrt(v_hat) + ADAM_EPS) + ADAM_WD * w), mn, vn


def adamw(name, w, g, m, v):
    R, C = w.shape
    tr = _row_tile(R, C)

    def body(w_ref, g_ref, m_ref, v_ref, d_ref, mo_ref, vo_ref):
        d_ref[...], mo_ref[...], vo_ref[...] = _adamw_math(w_ref[...], g_ref[...], m_ref[...], v_ref[...])

    spec = _rb(tr, C)
    return _rows_call(name, body, R, tr, [(w, spec), (g, spec), (m, spec), (v, spec)], [((R, C), F32, spec)] * 3)


def adamw_halves(name, w, mine, other, m, v, c_idx):
    R, C = w.shape
    hr = R // 2
    tr = _row_tile(hr, C)

    def body(c_ref, w_ref, a_ref, b_ref, m_ref, v_ref, g_ref, d_ref, mo_ref, vo_ref):
        gv = jnp.where(pl.program_id(0) == c_ref[0], a_ref[...], b_ref[...])
        g_ref[...] = gv
        d_ref[...], mo_ref[...], vo_ref[...] = _adamw_math(w_ref[...], gv, m_ref[...], v_ref[...])

    full = pl.BlockSpec((None, tr, C), lambda hh, i, c_ref: (hh, i, 0))
    mine_spec = pl.BlockSpec((tr, C), lambda hh, i, c_ref: (jnp.where(hh == c_ref[0], i, 0), 0))
    other_spec = pl.BlockSpec((tr, C), lambda hh, i, c_ref: (jnp.where(hh == c_ref[0], 0, i), 0))
    outs = pl.pallas_call(
        body, name=name,
        grid_spec=pltpu.PrefetchScalarGridSpec(num_scalar_prefetch=1, grid=(2, hr // tr),
                                               in_specs=[full, mine_spec, other_spec, full, full], out_specs=[full] * 4),
        out_shape=[jax.ShapeDtypeStruct((2, hr, C), F32)] * 4,
        compiler_params=_params(('parallel', 'parallel')))(
            c_idx, w.reshape(2, hr, C), mine, other, m.reshape(2, hr, C), v.reshape(2, hr, C))
    return [o.reshape(R, C) for o in outs]


def _place():
    x, y, c = lax.axis_index('x'), lax.axis_index('y'), lax.axis_index('c')
    return x, y, c, [(1 - x, y), (x, 1 - y), (1 - x, 1 - y)]


def _rcopy(src, dst, ssem, rsem, dev):
    return pltpu.make_async_remote_copy(src_ref=src, dst_ref=dst, send_sem=ssem, recv_sem=rsem, device_id=dev,
                                        device_id_type=MESH)


HBM = pl.BlockSpec(memory_space=pltpu.HBM)
SEM = pl.BlockSpec(memory_space=pltpu.SEMAPHORE)
EFFECT = pltpu.SideEffectType.DATAFLOW_SIDE_EFFECTING


def _in_hbm(a):
    return pltpu.with_memory_space_constraint(a, pltpu.HBM)


def _rows_part(shape, whole, half):
    return pl.ds(0, shape[0]) if whole else pl.ds(half * (shape[0] // 2), shape[0] // 2)


def gather_start(name, shards, whole):
    nT = len(shards)

    def body(*refs):
        srcs, lands = refs[:nT], refs[nT:2 * nT]
        ssem, rsem, token = refs[2 * nT], refs[2 * nT + 1], refs[-1]
        x, y, c, chips = _place()
        for t in range(nT):
            rows = _rows_part(shards[t].shape, whole[t], c)
            for k, (px, py) in enumerate(chips):
                _rcopy(srcs[t].at[rows], lands[t].at[2 * x + y, rows], ssem.at[3 * t + k], rsem.at[3 * t + k],
                       (px, py, c)).start()
        token[...] = jnp.zeros_like(token)

    zones = [lax.empty((N_CHIPS,) + s.shape, s.dtype) for s in shards]
    outs = pl.pallas_call(
        body, name=name,
        out_shape=(pltpu.SemaphoreType.DMA((3 * nT,)), pltpu.SemaphoreType.DMA((3 * nT,)),
                   *[pltpu.HBM(s.shape, s.dtype) for s in shards], *[pltpu.HBM(z.shape, z.dtype) for z in zones],
                   jax.ShapeDtypeStruct((8, LANES), F32)),
        in_specs=[HBM] * (2 * nT), out_specs=(SEM, SEM, *[HBM] * (2 * nT), pl.BlockSpec(memory_space=pltpu.VMEM)),
        input_output_aliases={i: 2 + i for i in range(2 * nT)},
        compiler_params=pltpu.CompilerParams(has_side_effects=EFFECT))(*[_in_hbm(a) for a in list(shards) + zones])
    return outs[0], outs[1], outs[2:2 + nT], outs[2 + nT:2 + 2 * nT], outs[-1]


def gather_wait(name, t, shard, zone, ssem, rsem, after, whole):
    after = after if isinstance(after, (list, tuple)) else [after]

    def body(src_ref, land_ref, ssem_ref, rsem_ref, *rest):
        x, y, c, chips = _place()
        rows = _rows_part(shard.shape, whole, c)
        for k, (px, py) in enumerate(chips):
            cp = _rcopy(src_ref.at[rows], land_ref.at[2 * px + py, rows], ssem_ref.at[3 * t + k], rsem_ref.at[3 * t + k],
                        (px, py, c))
            cp.wait_send()
            cp.wait_recv()

    return pl.pallas_call(
        body, name=name, out_shape=(pltpu.HBM(shard.shape, shard.dtype), pltpu.HBM(zone.shape, zone.dtype)),
        in_specs=(HBM, HBM, SEM, SEM, *[ANY] * len(after)), out_specs=(HBM, HBM), input_output_aliases={0: 0, 1: 1},
        compiler_params=pltpu.CompilerParams(has_side_effects=EFFECT))(shard, zone, ssem, rsem, *after)


def pair_swap(name, zone):
    hr = zone.shape[1] // 2

    def body(z_in, z_ref, ssem, rsem):
        x, y, c, chips = _place()
        cps = []
        for k, (px, py) in enumerate(chips):
            blk = z_ref.at[2 * px + py, pl.ds(c * hr, hr)]
            cps.append(_rcopy(blk, blk, ssem.at[k], rsem.at[k], (x, y, 1 - c)))
            cps[-1].start()
        for k, (px, py) in enumerate(chips):
            blk = z_ref.at[2 * px + py, pl.ds((1 - c) * hr, hr)]
            _rcopy(blk, blk, ssem.at[k], rsem.at[k], (x, y, 1 - c)).wait_recv()
        for cp in cps:
            cp.wait_send()

    return pl.pallas_call(
        body, name=name, in_specs=[ANY], out_specs=ANY, out_shape=jax.ShapeDtypeStruct(zone.shape, zone.dtype),
        input_output_aliases={0: 0},
        scratch_shapes=[pltpu.SemaphoreType.DMA((3,)), pltpu.SemaphoreType.DMA((3,))],
        compiler_params=_params())(zone)


N_SENDERS = 7


def _scatter_copies(g_ref, l_ref, ssem, rsem):
    x, y, c, chips = _place()
    cps = []
    for k, (px, py) in enumerate(chips):
        for d in range(2):
            to = (c + d) % 2
            cps.append(_rcopy(g_ref.at[2 * px + py, to], l_ref.at[2 * k + d], ssem.at[2 * k + d], rsem.at[2 * k + d],
                              (px, py, to)))
    cps.append(_rcopy(g_ref.at[2 * x + y, 1 - c], l_ref.at[6], ssem.at[6], rsem.at[6], (x, y, 1 - c)))
    return cps


def scatter_start(name, g):
    def body(g_ref, l_ref, ssem, rsem, g_out, l_out, token):
        for cp in _scatter_copies(g_ref, l_ref, ssem, rsem):
            cp.start()
        token[...] = jnp.zeros_like(token)

    zone = lax.empty((N_SENDERS,) + g.shape[2:], g.dtype)
    return pl.pallas_call(
        body, name=name,
        out_shape=(pltpu.SemaphoreType.DMA((N_SENDERS,)), pltpu.SemaphoreType.DMA((N_SENDERS,)),
                   pltpu.HBM(g.shape, g.dtype), pltpu.HBM(zone.shape, zone.dtype), jax.ShapeDtypeStruct((8, LANES), F32)),
        in_specs=[HBM, HBM], out_specs=(SEM, SEM, HBM, HBM, pl.BlockSpec(memory_space=pltpu.VMEM)),
        input_output_aliases={0: 2, 1: 3},
        compiler_params=pltpu.CompilerParams(has_side_effects=EFFECT))(_in_hbm(g), _in_hbm(zone))


def scatter_wait(name, g, zone, ssem, rsem, after):
    def body(g_ref, l_ref, ssem_ref, rsem_ref, after_ref, g_out, l_out):
        for cp in _scatter_copies(g_ref, l_ref, ssem_ref, rsem_ref):
            cp.wait_send()
            cp.wait_recv()

    return pl.pallas_call(
        body, name=name, out_shape=(pltpu.HBM(g.shape, g.dtype), pltpu.HBM(zone.shape, zone.dtype)),
        in_specs=(HBM, HBM, SEM, SEM, ANY), out_specs=(HBM, HBM), input_output_aliases={0: 0, 1: 1},
        compiler_params=pltpu.CompilerParams(has_side_effects=EFFECT))(g, zone, ssem, rsem, after)


def sum_parts(name, g, landed, chip_idx, c_idx):
    hr, C = g.shape[2:]
    tr = _row_tile(hr, C, min_rows=16)

    def body(me_ref, c_ref, g_ref, l_ref, o_ref):
        acc = g_ref[...].astype(F32)
        for s in range(N_SENDERS):
            acc = acc + l_ref[s].astype(F32)
        o_ref[...] = acc

    return pl.pallas_call(
        body, name=name,
        grid_spec=pltpu.PrefetchScalarGridSpec(
            num_scalar_prefetch=2, grid=(hr // tr,),
            in_specs=[pl.BlockSpec((None, None, tr, C), lambda i, me_ref, c_ref: (me_ref[0], c_ref[0], i, 0)),
                      pl.BlockSpec((N_SENDERS, tr, C), lambda i, me_ref, c_ref: (0, i, 0))],
            out_specs=pl.BlockSpec((tr, C), lambda i, me_ref, c_ref: (i, 0))),
        out_shape=jax.ShapeDtypeStruct((hr, C), F32),
        compiler_params=_params(('parallel',)))(chip_idx, c_idx, g, landed)


def pair_join(name, halves):
    nT = len(halves)

    def body(*refs):
        ins, outs = refs[:nT], refs[nT:2 * nT]
        ssem, rsem = refs[2 * nT:]
        x, y, c, _ = _place()
        cps = [_rcopy(ins[t], outs[t], ssem.at[t], rsem.at[t], (x, y, 1 - c)) for t in range(nT)]
        for cp in cps:
            cp.start()
        for cp in cps:
            cp.wait()

    return pl.pallas_call(
        body, name=name, in_specs=[ANY] * nT, out_specs=[ANY] * nT,
        out_shape=[jax.ShapeDtypeStruct(h.shape, h.dtype) for h in halves],
        scratch_shapes=[pltpu.SemaphoreType.DMA((nT,)), pltpu.SemaphoreType.DMA((nT,))],
        compiler_params=_params())(*halves)


N_DEVICES = 8


def _spread_copies(b_ref, l_ref, ssem, rsem):
    x, y, c, chips = _place()
    me = 4 * x + 2 * y + c
    pairs = []
    for px, py, pc in [(px, py, pc) for px, py in chips for pc in (c, 1 - c)] + [(x, y, 1 - c)]:
        it = 4 * px + 2 * py + pc
        pairs.append((_rcopy(b_ref, l_ref.at[me], ssem.at[it], rsem.at[me], (px, py, pc)),
                      _rcopy(b_ref, l_ref.at[it], ssem.at[it], rsem.at[it], (px, py, pc))))
    return pairs


def spread_start(name, buf):
    def body(b_ref, l_ref, ssem, rsem, b_out, l_out, token):
        for mine, _ in _spread_copies(b_ref, l_ref, ssem, rsem):
            mine.start()
        token[...] = jnp.zeros_like(token)

    zone = lax.empty((N_DEVICES,) + buf.shape, buf.dtype)
    return pl.pallas_call(
        body, name=name,
        out_shape=(pltpu.SemaphoreType.DMA((N_DEVICES,)), pltpu.SemaphoreType.DMA((N_DEVICES,)),
                   pltpu.HBM(buf.shape, buf.dtype), pltpu.HBM(zone.shape, zone.dtype), jax.ShapeDtypeStruct((8, LANES), F32)),
        in_specs=[HBM, HBM], out_specs=(SEM, SEM, HBM, HBM, pl.BlockSpec(memory_space=pltpu.VMEM)),
        input_output_aliases={0: 2, 1: 3},
        compiler_params=pltpu.CompilerParams(has_side_effects=EFFECT))(_in_hbm(buf), _in_hbm(zone))


def spread_wait(name, buf, zone, ssem, rsem, after):
    def body(b_ref, l_ref, ssem_ref, rsem_ref, after_ref, b_out, l_out):
        for mine, theirs in _spread_copies(b_ref, l_ref, ssem_ref, rsem_ref):
            mine.wait_send()
            theirs.wait_recv()

    return pl.pallas_call(
        body, name=name, out_shape=(pltpu.HBM(buf.shape, buf.dtype), pltpu.HBM(zone.shape, zone.dtype)),
        in_specs=(HBM, HBM, SEM, SEM, ANY), out_specs=(HBM, HBM), input_output_aliases={0: 0, 1: 1},
        compiler_params=pltpu.CompilerParams(has_side_effects=EFFECT))(buf, zone, ssem, rsem, after)


def sum_devices(name, zone):
    _, R, C = zone.shape
    tr = _row_tile(R, C)

    def body(z_ref, o_ref):
        acc = z_ref[0]
        for d in range(1, N_DEVICES):
            acc = acc + z_ref[d]
        o_ref[...] = acc

    return pl.pallas_call(
        body, name=name, grid=(R // tr,), in_specs=[pl.BlockSpec((N_DEVICES, tr, C), lambda i: (0, i, 0))],
        out_specs=pl.BlockSpec((tr, C), lambda i: (i, 0)), out_shape=jax.ShapeDtypeStruct((R, C), F32),
        compiler_params=_params(('parallel',)))(zone)


class _InWindows:
    def __init__(self, FW, LW, H, C):
        gap = LANES - H
        padded = lambda o: o if o < 3 * FW + H else o + gap
        self.width = 3 * FW + LANES + 2 * LW
        self.f_block = 3 * FW // LANES
        self.first = [padded(C * j) // LANES for j in range(N_CHIPS)]
        self.blocks = max(padded(C * (j + 1) - 1) // LANES - self.first[j] + 1 for j in range(N_CHIPS))
        assert all((b + self.blocks) * LANES <= self.width for b in self.first)
        self.cols = self.blocks * LANES
        self.runs = []
        for j in range(N_CHIPS):
            cut = min(max(3 * FW + H - C * j, 0), C)
            spans = [(0, cut), (cut, C)]
            self.runs.append([(t0, t1, padded(C * j + t0) - LANES * self.first[j]) for t0, t1 in spans if t1 > t0])

    def to_window(self, shard, chip):
        def place(j, s):
            parts, pos = [], 0
            for t0, t1, w0 in self.runs[j]:
                parts += [jnp.zeros((s.shape[0], w0 - pos), s.dtype), s[:, t0:t1]]
                pos = w0 + t1 - t0
            parts.append(jnp.zeros((s.shape[0], self.cols - pos), s.dtype))
            return jnp.concatenate([p for p in parts if p.shape[1]], axis=1)
        return lax.switch(chip, [functools.partial(place, j) for j in range(N_CHIPS)], shard)

    def from_window(self, win, chip):
        def take(j, w):
            return jnp.concatenate([w[:, w0:w0 + t1 - t0] for t0, t1, w0 in self.runs[j]], axis=1)
        return lax.switch(chip, [functools.partial(take, j) for j in range(N_CHIPS)], win)

    def _spans(self, j):
        b0, b1 = self.first[j], self.first[j] + self.blocks
        return (b0, min(b1, self.f_block)), b0 <= self.f_block < b1, (max(b0, self.f_block + 1), b1)

    def assemble(self, zone):
        main, f_blk = None, None
        for j in range(N_CHIPS):
            (a0, a1), has_f, (c0, c1) = self._spans(j)
            for p0, p1, shift in ((a0, a1, 0), (c0, c1, 1)):
                if p1 > p0:
                    part = zone[j][:, (p0 - self.first[j]) * LANES:(p1 - self.first[j]) * LANES]
                    part = jnp.pad(part, ((0, 0), ((p0 - shift) * LANES, self.width - LANES - (p1 - shift) * LANES)))
                    main = part if main is None else main + part
            if has_f:
                part = zone[j][:, (self.f_block - self.first[j]) * LANES:(self.f_block - self.first[j] + 1) * LANES]
                f_blk = part if f_blk is None else f_blk + part
        return main, f_blk

    def windows(self, main, f_blk):
        out = []
        for j in range(N_CHIPS):
            (a0, a1), has_f, (c0, c1) = self._spans(j)
            parts = [main[:, a0 * LANES:a1 * LANES]] if a1 > a0 else []
            parts += [f_blk] if has_f else []
            parts += [main[:, (c0 - 1) * LANES:(c1 - 1) * LANES]] if c1 > c0 else []
            out.append(jnp.concatenate(parts, axis=1))
        return jnp.stack(out)


_PACK = 8 * LANES


PACK_ROWS = 256


def _pack(arrs):
    flat = []
    for a in arrs:
        v = a.reshape(-1).astype(F32)
        flat.append(jnp.pad(v, (0, (-v.shape[0]) % _PACK)))
    rows = sum(v.shape[0] for v in flat) // LANES
    flat.append(jnp.zeros(((-rows) % PACK_ROWS) * LANES, F32))
    return jnp.concatenate(flat).reshape(-1, LANES)


def _unpack(buf, shapes):
    out, off = [], 0
    flat = buf.reshape(-1)
    for sh in shapes:
        n = math.prod(sh)
        out.append(flat[off:off + n].reshape(sh))
        off += n + (-n) % _PACK
    return out


def kernel(x, mem, g_mix, w_in, b_f, g_q, g_k, conv_w, conv_b, w_ra, b_ra, w_ri, b_ri, lam, g_fox_out, g_lru_out, w_out, g_xattn, g_mem, w_cq, w_ckv, g_cq, g_ck, w_co, g_ffn, w_gate_up, w_down, loss_target, m_g_mix, m_w_in, m_b_f, m_g_q, m_g_k, m_conv_w, m_conv_b, m_w_ra, m_b_ra, m_w_ri, m_b_ri, m_lam, m_g_fox_out, m_g_lru_out, m_w_out, m_g_xattn, m_g_mem, m_w_cq, m_w_ckv, m_g_cq, m_g_ck, m_w_co, m_g_ffn, m_w_gate_up, m_w_down, v_g_mix, v_w_in, v_b_f, v_g_q, v_g_k, v_conv_w, v_conv_b, v_w_ra, v_b_ra, v_w_ri, v_b_ri, v_lam, v_g_fox_out, v_g_lru_out, v_w_out, v_g_xattn, v_g_mem, v_w_cq, v_w_ckv, v_g_cq, v_g_ck, v_w_co, v_g_ffn, v_w_gate_up, v_w_down):
    given = dict(locals())
    W = {n: given[n][0] for n in WEIGHTS}
    M1 = {n: given['m_' + n][0] for n in WEIGHTS}
    V1 = {n: given['v_' + n][0] for n in WEIGHTS}
    xs, ms, tgt = x[0], mem[0], loss_target[0]
    S, D = xs.shape
    H = W['b_f'].shape[0]
    FW = H * HEAD_DIM
    LW = W['lam'].shape[0]
    nb = W['w_ra'].shape[0]
    XW = W['w_cq'].shape[1]
    F = W['w_down'].shape[0] * N_CHIPS
    IN_W = W['w_in'].shape[1] * N_CHIPS
    assert FW == LW and LW == nb * LANES and IN_W == 3 * FW + H + 2 * LW and H <= 8
    T = _tile(S, (512, 256, 128))
    c_idx = lax.axis_index('c').astype(jnp.int32).reshape(1)
    chip = 2 * lax.axis_index('x') + lax.axis_index('y')
    chip_idx = chip.astype(jnp.int32).reshape(1)
    vec = lambda n: W[n].reshape(1, -1)

    wins = _InWindows(FW, LW, H, W['w_in'].shape[1])
    started = {}
    g_tok = jnp.zeros((1, 1), F32)
    for call, names in (('gather_start_first', ['conv_w', 'w_in']), ('gather_start_rest', BIG[1:])):
        own = [W[n].reshape(-1, LANES) if n == 'conv_w' else W[n].astype(BF16) + g_tok.astype(BF16) for n in names]
        own = [wins.to_window(o, chip) if n == 'w_in' else o for n, o in zip(names, own)]
        ssem, rsem, srcs, zones, tok = gather_start(call, own, [n == 'conv_w' for n in names])
        g_tok = tok[0:1, 0:1]
        started.update({n: (t, srcs[t], zones[t], ssem, rsem) for t, n in enumerate(names)})

    def fetch(n, after):
        t, g_src, g_zone, g_ssem, g_rsem = started[n]
        src, zone = gather_wait('gather_wait_' + n, t, g_src, g_zone, g_ssem, g_rsem, after, n == 'conv_w')
        if n != 'conv_w':
            zone = pair_swap('pair_swap_' + n, zone)
        return lax.dynamic_update_index_in_dim(zone, src, chip, 0)

    b_f_pad = jnp.pad(vec('b_f'), ((0, 0), (0, LANES - H)))
    u_off, g_off = 3 * FW // LANES, (3 * FW + LW) // LANES

    h1 = norm_fwd('norm_mix', xs, vec('g_mix') + g_tok[0:1, 0:1])
    conv_full = fetch('conv_w', h1).reshape(N_CHIPS, CONV_W, LW // N_CHIPS).transpose(1, 0, 2).reshape(CONV_W, LW)
    w5, wf = wins.assemble(fetch('w_in', [h1, M1['w_in'], V1['w_in']]))
    proj = _mm('proj_in', h1, w5, 'nn', F32)
    f_raw = _mm('proj_f', h1, wf, 'nn', F32)
    qn, kn, vb = qkv_fwd(proj, vec('g_q'), vec('g_k'), FW)
    cc = fgate_fwd(f_raw, b_f_pad)
    ct = cc[:, :8].T
    o_fox, lse = fox_fwd(qn, kn, vb, cc, ct, T)
    lru_w = (conv_full, vec('conv_b'), W['w_ra'], vec('b_ra'), W['w_ri'], vec('b_ri'), vec('lam'))
    y_lru = lru_fwd(proj, *lru_w, u_off, g_off)
    mixn = mix_fwd(o_fox, y_lru, vec('g_fox_out'), vec('g_lru_out'))
    w_out_f = fetch('w_out', mixn).reshape(2 * FW, D)
    x1 = _mm('proj_out', mixn, w_out_f, 'nn', F32, res=xs)

    hq = norm_fwd('norm_xq', x1, vec('g_xattn'))
    mn = norm_fwd('norm_mem', ms, vec('g_mem'))
    w_cq_f = fetch('w_cq', hq).reshape(D, XW)
    w_ckv_f = fetch('w_ckv', hq).reshape(D, 2 * XW)
    cq_raw = _mm('proj_cq', hq, w_cq_f, 'nn', F32)
    ckv = _mm('proj_ckv', mn, w_ckv_f, 'nn', F32)
    o_x = xattn_fwd(cq_raw, ckv, vec('g_cq'), vec('g_ck'))
    w_co_g = fetch('w_co', o_x)
    x2 = _mm_colsharded('proj_co', o_x, w_co_g, F32, res=x1)

    hf = norm_fwd('norm_ffn', x2, vec('g_ffn'))
    w_gu_g = fetch('w_gate_up', hf)
    gu, act = gate_up_fwd(hf, w_gu_g, F)
    w_down_f = fetch('w_down', act).reshape(F, D)
    dy, dyb, loss_blk = down_fwd_loss(act, w_down_f, x2, tgt)

    gw, pending = {}, []

    def reduce_begin(n, g):
        sp = g.reshape(N_CHIPS, 2, g.shape[1] // 2, g.shape[2])
        ssem, rsem, sp, zone, tok = scatter_start('scatter_start_' + n, sp)
        pending.append((n, sp, zone, ssem, rsem))
        return tok[0:1, 0:1]

    t_down = reduce_begin('w_down', _mm('bwd_down_w', act, dyb, 'tn', BF16).reshape(N_CHIPS, F // N_CHIPS, D))
    dgu = down_bwd_x(dyb, w_down_f, gu, t_down)
    dhf = _mm_colsharded_t('bwd_gate_up_x', dgu, w_gu_g, F32)
    t_gu = reduce_begin('w_gate_up', _mm_grad_colsharded('bwd_gate_up_w', hf, dgu, N_CHIPS, BF16))
    dx2, dx2b, gw['g_ffn'] = norm_bwd('norm_ffn_bwd', x2, vec('g_ffn') + t_down + t_gu, dhf, res=dy)

    do_x = _mm_colsharded_t('bwd_co_x', dx2b, w_co_g, BF16)
    t_co = reduce_begin('w_co', _mm_grad_colsharded('bwd_co_w', o_x, dx2b, N_CHIPS, BF16))
    dcq_raw, dckv, gw['g_cq'], gw['g_ck'] = xattn_bwd(cq_raw, ckv, vec('g_cq') + t_co, vec('g_ck'), do_x)
    dhq = _mm('bwd_cq_x', dcq_raw, w_cq_f, 'nt', F32)
    t_cq = reduce_begin('w_cq', _mm('bwd_cq_w', hq, dcq_raw, 'tn', BF16).reshape(N_CHIPS, D // N_CHIPS, XW))
    dmn = _mm('bwd_ckv_x', dckv, w_ckv_f, 'nt', F32)
    t_ckv = reduce_begin('w_ckv', _mm('bwd_ckv_w', mn, dckv, 'tn', BF16).reshape(N_CHIPS, D // N_CHIPS, 2 * XW))
    (gw['g_mem'],) = norm_bwd('norm_mem_bwd', ms, vec('g_mem'), dmn, want_dx=False)
    dx1, dx1b, gw['g_xattn'] = norm_bwd('norm_xq_bwd', x1, vec('g_xattn') + t_cq + t_ckv, dhq, res=dx2)

    dmix = _mm('bwd_out_x', dx1b, w_out_f, 'nt', F32)
    t_out = reduce_begin('w_out', _mm('bwd_out_w', mixn, dx1b, 'tn', BF16).reshape(N_CHIPS, 2 * FW // N_CHIPS, D))
    do_fox, delta, dy_lru, gw['g_fox_out'], gw['g_lru_out'] = mix_bwd(o_fox, y_lru, vec('g_fox_out') + t_out,
                                                                     vec('g_lru_out'), dmix)
    (du, dgate, gw['conv_w'], gw['conv_b'], gw['w_ra'], gw['b_ra'], gw['w_ri'], gw['b_ri'],
     gw['lam']) = lru_bwd(proj, dy_lru, *lru_w, u_off, g_off)
    early = [n for n in SMALL if n not in ('g_q', 'g_k', 'b_f', 'g_mix')]
    late = [n for n in SMALL if n not in early]
    e_ssem, e_rsem, e_buf, e_zone, e_tok = spread_start('spread_start_early', _pack([gw[n] for n in early]))
    dqn, delta2 = fox_bwd_q(qn, kn, vb, do_fox, cc, ct, lse, delta, T)
    dkn, dv, dct = fox_bwd_kv(qn, kn, vb, do_fox, cc, ct, lse, delta2, T)
    dq, dk, gw['g_q'], gw['g_k'] = qkv_bwd(proj, vec('g_q') + e_tok[0:1, 0:1], vec('g_k'), dqn, dkn, FW)
    dc = jnp.pad(dct.reshape(H, S).T, ((0, 0), (0, LANES - H)))
    df, db_f = fgate_bwd(f_raw, b_f_pad, dc, H)
    gw['b_f'] = db_f[:, :H]
    dproj = jnp.concatenate([dq, dk, dv, du, dgate], axis=1)
    dw5 = _mm('bwd_in_w', h1, dproj, 'tn', BF16)
    dwf = _mm('bwd_f_w', h1, df, 'tn', BF16)
    t_in = reduce_begin('w_in', wins.windows(dw5, dwf))
    dh_a = _mm('bwd_f_x', df, wf, 'nt', F32)
    dh1 = _mm('bwd_in_x', dproj, w5, 'nt', F32, res=dh_a)
    grad_x, _, gw['g_mix'] = norm_bwd('norm_mix_bwd', xs, vec('g_mix') + t_in, dh1, res=dx1)
    l_ssem, l_rsem, l_buf, l_zone, _ = spread_start('spread_start_late',
                                                    _pack([gw[n] for n in late] + [loss_blk[0:1, 0:1]]))

    grads, delta_w, new_m, new_v = {}, {}, {}, {}
    done = grad_x
    for n, part, zone, ssem, rsem in pending:
        part, landed = scatter_wait('scatter_wait_' + n, part, zone, ssem, rsem, done)
        mine = sum_parts('sum_parts_' + n, part, landed, chip_idx, c_idx)
        (other,) = pair_join('pair_join_' + n, [mine])
        if n == 'w_in':
            mine, other = wins.from_window(mine, chip), wins.from_window(other, chip)
        grads[n], delta_w[n], new_m[n], new_v[n] = adamw_halves('adamw_' + n, W[n], mine, other, M1[n], V1[n], c_idx)
        done = delta_w[n]

    device = 4 * lax.axis_index('x') + 2 * lax.axis_index('y') + lax.axis_index('c')
    summed = {}
    for tag, names, buf, zone, ssem, rsem in (('early', early, e_buf, e_zone, e_ssem, e_rsem),
                                              ('late', late + ['loss'], l_buf, l_zone, l_ssem, l_rsem)):
        buf, zone = spread_wait('spread_wait_' + tag, buf, zone, ssem, rsem, done)
        total = sum_devices('sum_small_' + tag, lax.dynamic_update_index_in_dim(zone, buf, device, 0))
        summed.update(zip(names, _unpack(total, [gw[n].shape if n != 'loss' else (1, 1) for n in names])))
    loss = summed['loss'].reshape(())
    for n in SMALL:
        g = summed[n]
        grads[n] = g.reshape(W[n].shape) if n != 'conv_w' else lax.dynamic_slice_in_dim(
            g, chip * (LW // N_CHIPS), LW // N_CHIPS, axis=1)
    packs = [_pack([d[n] for n in SMALL]) for d in (W, grads, M1, V1)]
    shapes = [W[n].shape for n in SMALL]
    for d, res in zip((delta_w, new_m, new_v), adamw('adamw_small', *packs)):
        d.update(zip(SMALL, _unpack(res, shapes)))

    lead = lambda d: [d[n][None] for n in WEIGHTS]
    return (loss, grad_x[None], *lead(grads), *lead(delta_w), *lead(new_m), *lead(new_v))
```

```python
import functools
import math

import jax
import jax.numpy as jnp
from jax import lax
from jax.experimental import pallas as pl
from jax.experimental.pallas import tpu as pltpu

F32 = jnp.float32
BF16 = jnp.bfloat16
HEAD_DIM = 128
LANES = 128
LRU_C = 8.0
RMS_EPS = 1e-6
CONV_W = 4
ADAM_LR = 0.001
ADAM_B1 = 0.9
ADAM_B2 = 0.999
ADAM_EPS = 1e-08
ADAM_WD = 0.01
ADAM_STEP = 10
VMEM_LIMIT = 56 * 1024 * 1024
N_CHIPS = 4
MESH = pl.DeviceIdType.MESH
ANY = pl.BlockSpec(memory_space=pl.ANY)

WEIGHTS = ['g_mix', 'w_in', 'b_f', 'g_q', 'g_k', 'conv_w', 'conv_b', 'w_ra', 'b_ra', 'w_ri', 'b_ri', 'lam',
           'g_fox_out', 'g_lru_out', 'w_out', 'g_xattn', 'g_mem', 'w_cq', 'w_ckv', 'g_cq', 'g_ck', 'w_co', 'g_ffn',
           'w_gate_up', 'w_down']
BIG = ['w_in', 'w_out', 'w_cq', 'w_ckv', 'w_co', 'w_gate_up', 'w_down']
SMALL = [n for n in WEIGHTS if n not in BIG]


def _params(sem=None):
    if sem is None:
        return pltpu.CompilerParams(vmem_limit_bytes=VMEM_LIMIT)
    return pltpu.CompilerParams(dimension_semantics=sem, vmem_limit_bytes=VMEM_LIMIT)


def _tile(n, cands):
    for t in cands:
        if n % t == 0:
            return t
    return n


ROW_BLOCK_BYTES = 1 << 20


def _row_tile(n_rows, n_cols, min_rows=8):
    cands = [t for t in (512, 256, 128, 64, 32, 16, 8) if t >= min_rows and t * n_cols * 4 <= ROW_BLOCK_BYTES]
    return _tile(n_rows, cands or [min_rows])


def _sigmoid(z):
    return 1.0 / (1.0 + jnp.exp(-z))


def _softplus(z):
    return jnp.maximum(z, 0.0) + jnp.log(1.0 + jnp.exp(-jnp.abs(z)))


def _neg_expm1(z):
    series = -z * (1.0 + z * (0.5 + z * (1.0 / 6.0 + z * (1.0 / 24.0 + z * (1.0 / 120.0)))))
    return jnp.where(z > -0.25, series, 1.0 - jnp.exp(z))


_GELU_K = math.sqrt(2.0 / math.pi)


def _gelu_and_grad(z):
    inner = _GELU_K * (z + 0.044715 * z * z * z)
    t = jnp.tanh(inner)
    g = 0.5 * z * (1.0 + t)
    dg = 0.5 * (1.0 + t) + 0.5 * z * (1.0 - t * t) * _GELU_K * (1.0 + 3.0 * 0.044715 * z * z)
    return g, dg


def _rms(xv, g):
    r = lax.rsqrt(jnp.mean(xv * xv, axis=-1, keepdims=True) + RMS_EPS)
    return xv * r * g


def _rms_bwd(xv, g, dy):
    r = lax.rsqrt(jnp.mean(xv * xv, axis=-1, keepdims=True) + RMS_EPS)
    xh = xv * r
    dyg = dy * g
    dx = r * (dyg - xh * jnp.mean(dyg * xh, axis=-1, keepdims=True))
    return dx, jnp.sum(dy * xh, axis=0, keepdims=True)


def _heads(fn, n_heads, *arrs):
    outs = [fn(*[a[:, h * HEAD_DIM:(h + 1) * HEAD_DIM] for a in arrs]) for h in range(n_heads)]
    first = jnp.concatenate([o[0] for o in outs], axis=1) if n_heads > 1 else outs[0][0]
    rest = [functools.reduce(lambda p, q: p + q, [o[i] for o in outs]) for i in range(1, len(outs[0]))]
    return (first, *rest)


def _split3(v):
    hi = v.astype(BF16)
    r1 = v - hi.astype(F32)
    mid = r1.astype(BF16)
    lo = (r1 - mid.astype(F32)).astype(BF16)
    return hi, mid, lo


def _acc_out(ref, first, val):
    @pl.when(first)
    def _():
        ref[...] = val

    @pl.when(jnp.logical_not(first))
    def _():
        ref[...] += val


_DIMS = {'nn': (((1,), (0,)), ((), ())), 'nt': (((1,), (1,)), ((), ())), 'tn': (((0,), (0,)), ((), ()))}


MM_VMEM_BYTES = 36 * 1024 * 1024


MXU_FLOPS = 800e12
HBM_BYTES_S = 3.2e12
VMEM_ADD_BYTES_S = 8e12
STEP_S = 0.35e-6


def _k_tile(K, tm, tn, a, b, o_dtype, res):
    fixed = tm * tn * (2 * jnp.dtype(o_dtype).itemsize + 4 + (8 if res is not None else 0))
    per_k = 2 * (tm * a.dtype.itemsize + tn * b.dtype.itemsize)
    per_k += 2 * tm * (a.dtype.itemsize > 2) + 2 * tn * (b.dtype.itemsize > 2)
    units = K // LANES
    for d in sorted((d for d in range(1, units + 1) if units % d == 0), reverse=True):
        if fixed + d * LANES * per_k <= MM_VMEM_BYTES:
            return d * LANES
    return None


def _mm_tiles(M, N, K, k_span, a, b, o_dtype, res, tn_cands=(2048, 1024, 512, 256, 128)):
    best = None
    for tm in (2048, 1024, 512, 256, 128):
        for tn in tn_cands:
            if M % tm or N % tn:
                continue
            tk = _k_tile(k_span, tm, tn, a, b, o_dtype, res)
            if tk is None:
                continue
            nk = K // tk
            traffic = (M * K * a.dtype.itemsize * (N // tn) + K * N * b.dtype.itemsize * (M // tm)
                       + M * N * (jnp.dtype(o_dtype).itemsize + (4 if res is not None else 0)))
            work = 2.0 * M * N * K / MXU_FLOPS + (M * N * 4 * nk / VMEM_ADD_BYTES_S if nk > 1 else 0.0)
            t = max(work, traffic / HBM_BYTES_S) + (M // tm) * (N // tn) * nk * STEP_S
            if best is None or t < best[0]:
                best = (t, tm, tn, tk)
    assert best is not None, (M, N, K)
    return best[1:]


def _mm_call(name, a, b, mode, grid, a_spec, b_spec, o_spec, o_shape, o_dtype, acc_shape, res=None):
    nk = grid[2]
    dn = _DIMS[mode]

    def body(*refs):
        a_ref, b_ref = refs[:2]
        r_ref = refs[2] if res is not None else None
        o_ref = refs[3] if res is not None else refs[2]
        part = lax.dot_general(a_ref[...].astype(BF16), b_ref[...].astype(BF16), dn, preferred_element_type=F32)

        def finish(r):
            if r_ref is not None:
                r = r + r_ref[...]
            o_ref[...] = r.astype(o_dtype)

        if nk == 1:
            finish(part)
            return
        acc = refs[-1]
        k = pl.program_id(2)

        @pl.when(k == 0)
        def _():
            acc[...] = part

        @pl.when(k > 0)
        def _():
            acc[...] += part

        @pl.when(k == nk - 1)
        def _():
            finish(acc[...])

    ins = [a, b] + ([] if res is None else [res])
    specs = [a_spec, b_spec] + ([] if res is None else [o_spec])
    return pl.pallas_call(
        body, name=name, grid=grid, in_specs=specs, out_specs=o_spec,
        out_shape=jax.ShapeDtypeStruct(o_shape, o_dtype),
        scratch_shapes=[] if nk == 1 else [pltpu.VMEM(acc_shape, F32)],
        compiler_params=_params(('parallel', 'parallel', 'arbitrary')))(*ins)


def _mm(name, a, b, mode, o_dtype, res=None):
    if mode == 'tn':
        K, M = a.shape
    else:
        M, K = a.shape
    N = b.shape[0] if mode == 'nt' else b.shape[1]
    tm, tn, tk = _mm_tiles(M, N, K, K, a, b, o_dtype, res)
    a_spec = (pl.BlockSpec((tk, tm), lambda m, n, k: (k, m)) if mode == 'tn'
              else pl.BlockSpec((tm, tk), lambda m, n, k: (m, k)))
    b_spec = (pl.BlockSpec((tn, tk), lambda m, n, k: (n, k)) if mode == 'nt'
              else pl.BlockSpec((tk, tn), lambda m, n, k: (k, n)))
    o_spec = pl.BlockSpec((tm, tn), lambda m, n, k: (m, n))
    return _mm_call(name, a, b, mode, (M // tm, N // tn, K // tk), a_spec, b_spec, o_spec, (M, N), o_dtype,
                    (tm, tn), res)


def _mm_colsharded(name, a, w, o_dtype, res=None):
    M, K = a.shape
    J, _, Nj = w.shape
    tm, tn, tk = _mm_tiles(M, J * Nj, K, K, a, w, o_dtype, res,
                           tn_cands=[t for t in (2816, 1408, 1024, 512, 256, 128) if Nj % t == 0])
    per = Nj // tn
    return _mm_call(name, a, w, 'nn', (M // tm, J * per, K // tk),
                    pl.BlockSpec((tm, tk), lambda m, n, k: (m, k)),
                    pl.BlockSpec((None, tk, tn), lambda m, n, k: (n // per, k, n % per)),
                    pl.BlockSpec((tm, tn), lambda m, n, k: (m, n)), (M, J * Nj), o_dtype, (tm, tn), res)


def _planes_spec(arr, rows, cols, row_of, col_of):
    if arr.ndim == 2:
        return pl.BlockSpec((rows, cols), lambda m, n, k: (row_of(m, n, k), col_of(m, n, k)))
    per_plane = arr.shape[2] // cols
    return pl.BlockSpec((None, rows, cols),
                        lambda m, n, k: (col_of(m, n, k) // per_plane, row_of(m, n, k), col_of(m, n, k) % per_plane))


def _mm_colsharded_t(name, a, w, o_dtype):
    M = a.shape[-2]
    J, K, Nj = w.shape
    tm, tn, tk = _mm_tiles(M, K, J * Nj, Nj, a, w, o_dtype, None)
    per = Nj // tk
    return _mm_call(name, a, w, 'nt', (M // tm, K // tn, J * per),
                    _planes_spec(a, tm, tk, lambda m, n, k: m, lambda m, n, k: k),
                    pl.BlockSpec((None, tn, tk), lambda m, n, k: (k // per, n, k % per)),
                    pl.BlockSpec((tm, tn), lambda m, n, k: (m, n)), (M, K), o_dtype, (tm, tn))


def _mm_grad_colsharded(name, a, dy, J, o_dtype):
    S, M = a.shape
    Nj = dy.shape[-1] * (dy.shape[0] if dy.ndim == 3 else 1) // J
    tm, tn, tk = _mm_tiles(M, J * Nj, S, S, a, dy, o_dtype, None,
                           tn_cands=[t for t in (2816, 1408, 1024, 512, 256, 128) if Nj % t == 0])
    per = Nj // tn
    return _mm_call(name, a, dy, 'tn', (M // tm, J * per, S // tk),
                    pl.BlockSpec((tk, tm), lambda m, n, k: (k, m)),
                    _planes_spec(dy, tk, tn, lambda m, n, k: k, lambda m, n, k: n),
                    pl.BlockSpec((None, tm, tn), lambda m, n, k: (n // per, m, n % per)), (J, M, Nj), o_dtype, (tm, tn))


def _rows_call(name, body, n_rows, tr, ins, outs):
    return pl.pallas_call(
        body, name=name, grid=(n_rows // tr,), in_specs=[s for _, s in ins], out_specs=[s for _, _, s in outs],
        out_shape=[jax.ShapeDtypeStruct(sh, dt) for sh, dt, _ in outs],
        compiler_params=_params(('arbitrary',)))(*[a for a, _ in ins])


def _rb(tr, w, cb=0):
    return pl.BlockSpec((tr, w), lambda i: (i, cb))


def _fb(shape):
    nd = len(shape)
    return pl.BlockSpec(shape, lambda i: (0,) * nd)


def norm_fwd(name, xv, g):
    S, D = xv.shape
    tr = _tile(S, (256, 128))

    def body(x_ref, g_ref, o_ref):
        o_ref[...] = _rms(x_ref[...], g_ref[...]).astype(BF16)

    return _rows_call(name, body, S, tr, [(xv, _rb(tr, D)), (g, _fb((1, D)))], [((S, D), BF16, _rb(tr, D))])[0]


def norm_bwd(name, xv, g, dy, res=None, want_dx=True):
    S, D = xv.shape
    tr = _tile(S, (256, 128))

    def body(*refs):
        if res is None:
            x_ref, g_ref, dy_ref = refs[:3]
            outs = refs[3:]
            r_ref = None
        else:
            x_ref, g_ref, dy_ref, r_ref = refs[:4]
            outs = refs[4:]
        dx, dg = _rms_bwd(x_ref[...], g_ref[...], dy_ref[...])
        if r_ref is not None:
            dx = dx + r_ref[...]
        if want_dx:
            outs[0][...] = dx
            outs[1][...] = dx.astype(BF16)
        _acc_out(outs[-1], pl.program_id(0) == 0, dg)

    ins = [(xv, _rb(tr, D)), (g, _fb((1, D))), (dy, _rb(tr, D))] + ([] if res is None else [(res, _rb(tr, D))])
    outs = ([((S, D), F32, _rb(tr, D)), ((S, D), BF16, _rb(tr, D))] if want_dx else []) + [((1, D), F32, _fb((1, D)))]
    return _rows_call(name, body, S, tr, ins, outs)


def qkv_fwd(proj, g_q, g_k, FW):
    S = proj.shape[0]
    H = FW // HEAD_DIM
    tr = _tile(S, (256, 128))

    def body(q_ref, k_ref, v_ref, gq_ref, gk_ref, qo, ko, vo):
        qo[...] = _heads(lambda t: (_rms(t, gq_ref[...]),), H, q_ref[...])[0].astype(BF16)
        ko[...] = _heads(lambda t: (_rms(t, gk_ref[...]),), H, k_ref[...])[0].astype(BF16)
        vo[...] = v_ref[...].astype(BF16)

    o = ((S, FW), BF16, _rb(tr, FW))
    return _rows_call('qkv_fwd', body, S, tr,
                      [(proj, _rb(tr, FW, 0)), (proj, _rb(tr, FW, 1)), (proj, _rb(tr, FW, 2)),
                       (g_q, _fb((1, HEAD_DIM))), (g_k, _fb((1, HEAD_DIM)))], [o, o, o])


def qkv_bwd(proj, g_q, g_k, dqn, dkn, FW):
    S = proj.shape[0]
    H = FW // HEAD_DIM
    tr = _tile(S, (256, 128))

    def body(q_ref, k_ref, gq_ref, gk_ref, dq_ref, dk_ref, dqo, dko, dgq, dgk):
        dq, gq = _heads(lambda t, d: _rms_bwd(t, gq_ref[...], d), H, q_ref[...], dq_ref[...])
        dk, gk = _heads(lambda t, d: _rms_bwd(t, gk_ref[...], d), H, k_ref[...], dk_ref[...])
        dqo[...] = dq.astype(BF16)
        dko[...] = dk.astype(BF16)
        first = pl.program_id(0) == 0
        _acc_out(dgq, first, gq)
        _acc_out(dgk, first, gk)

    o = ((S, FW), BF16, _rb(tr, FW))
    og = ((1, HEAD_DIM), F32, _fb((1, HEAD_DIM)))
    return _rows_call('qkv_bwd', body, S, tr,
                      [(proj, _rb(tr, FW, 0)), (proj, _rb(tr, FW, 1)), (g_q, _fb((1, HEAD_DIM))),
                       (g_k, _fb((1, HEAD_DIM))), (dqn, _rb(tr, FW)), (dkn, _rb(tr, FW))], [o, o, og, og])


def _tri(n, upper):
    r = lax.broadcasted_iota(jnp.int32, (n, n), 0)
    c = lax.broadcasted_iota(jnp.int32, (n, n), 1)
    return jnp.where((c >= r) if upper else (c <= r), 1.0, 0.0).astype(BF16)


def _blocked_cumsum(val, S, blk, reverse):
    tri = _tri(blk, reverse)
    order = range(S // blk - 1, -1, -1) if reverse else range(S // blk)
    carry = jnp.zeros((1, LANES), F32)
    outs = {}
    for bi in order:
        part = val[bi * blk:(bi + 1) * blk]
        acc = carry
        for piece in _split3(part):
            acc = acc + jnp.dot(tri, piece, preferred_element_type=F32)
        outs[bi] = acc
        carry = carry + jnp.sum(part, axis=0, keepdims=True)
    return jnp.concatenate([outs[bi] for bi in range(S // blk)], axis=0)


def fgate_fwd(f_raw, b_f_pad):
    S = f_raw.shape[0]
    blk = _tile(S, (256, 128))

    def body(f_ref, b_ref, c_ref):
        z = f_ref[...] + b_ref[...]
        c_ref[...] = _blocked_cumsum(-_softplus(-z), S, blk, False)

    return pl.pallas_call(body, name='fgate_fwd', grid=(1,), in_specs=[_fb((S, LANES)), _fb((1, LANES))],
                          out_specs=_fb((S, LANES)), out_shape=jax.ShapeDtypeStruct((S, LANES), F32),
                          compiler_params=_params(('arbitrary',)))(f_raw, b_f_pad)


def fgate_bwd(f_raw, b_f_pad, dc, H):
    S = f_raw.shape[0]
    blk = _tile(S, (256, 128))

    def body(f_ref, b_ref, dc_ref, df_ref, db_ref):
        z = f_ref[...] + b_ref[...]
        dlogf = _blocked_cumsum(dc_ref[...], S, blk, True)
        lane = lax.broadcasted_iota(jnp.int32, (S, LANES), 1)
        df = jnp.where(lane < H, dlogf * _sigmoid(-z), 0.0)
        df_ref[...] = df.astype(BF16)
        db_ref[...] = jnp.sum(df, axis=0, keepdims=True)

    return pl.pallas_call(body, name='fgate_bwd', grid=(1,),
                          in_specs=[_fb((S, LANES)), _fb((1, LANES)), _fb((S, LANES))],
                          out_specs=[_fb((S, LANES)), _fb((1, LANES))],
                          out_shape=[jax.ShapeDtypeStruct((S, LANES), BF16), jax.ShapeDtypeStruct((1, LANES), F32)],
                          compiler_params=_params(('arbitrary',)))(f_raw, b_f_pad, dc)


def _fox_logits(q, k, c_blk, ct_blk, h, T, diagonal):
    s = lax.dot_general(q, k, _DIMS['nt'], preferred_element_type=F32) * (1.0 / math.sqrt(HEAD_DIM))
    lane = lax.broadcasted_iota(jnp.int32, c_blk.shape, 1)
    cq = jnp.sum(jnp.where(lane == h, c_blk, 0.0), axis=1, keepdims=True)
    sub = lax.broadcasted_iota(jnp.int32, ct_blk.shape, 0)
    ck = jnp.sum(jnp.where(sub == h, ct_blk, 0.0), axis=0, keepdims=True)
    s = s + cq - ck
    if not diagonal:
        return s
    rows = lax.broadcasted_iota(jnp.int32, (T, T), 0)
    cols = lax.broadcasted_iota(jnp.int32, (T, T), 1)
    return jnp.where(cols <= rows, s, -jnp.inf)


def _below_and_on_diagonal(q_blk, k_blk, step):
    @pl.when(k_blk < q_blk)
    def _():
        step(False)

    @pl.when(k_blk == q_blk)
    def _():
        step(True)


def fox_fwd(qn, kn, vb, c, ct, T):
    S, FW = qn.shape
    H = FW // HEAD_DIM
    Hp = ct.shape[0]
    n = S // T

    HB = _tile(H, (8, 4, 2, 1))
    W2 = HB * HEAD_DIM

    def body(q_ref, k_ref, v_ref, c_ref, ct_ref, o_ref, lse_ref, m_s, l_s, acc_s):
        hb, i, j = pl.program_id(0), pl.program_id(1), pl.program_id(2)

        @pl.when(j == 0)
        def _():
            m_s[...] = jnp.full_like(m_s, -jnp.inf)
            l_s[...] = jnp.zeros_like(l_s)
            acc_s[...] = jnp.zeros_like(acc_s)

        def step(diagonal):
            for hh in range(HB):
                sl = slice(hh * HEAD_DIM, (hh + 1) * HEAD_DIM)
                s = _fox_logits(q_ref[:, sl], k_ref[:, sl], c_ref[...], ct_ref[...], hb * HB + hh, T, diagonal)
                m_old = m_s[hh]
                m_new = jnp.maximum(m_old, jnp.max(s, axis=1, keepdims=True))
                alpha = jnp.exp(m_old - m_new)
                p = jnp.exp(s - m_new)
                l_s[hh] = alpha * l_s[hh] + jnp.sum(p, axis=1, keepdims=True)
                acc_s[hh] = alpha * acc_s[hh] + jnp.dot(p.astype(BF16), v_ref[:, sl], preferred_element_type=F32)
                m_s[hh] = m_new

        _below_and_on_diagonal(i, j, step)

        @pl.when(j == i)
        def _():
            for hh in range(HB):
                o_ref[:, hh * HEAD_DIM:(hh + 1) * HEAD_DIM] = acc_s[hh] / l_s[hh]
                lse_ref[hh] = jnp.broadcast_to(m_s[hh] + jnp.log(l_s[hh]), (T, LANES))

    qs = pl.BlockSpec((T, W2), lambda h, i, j: (i, h))
    ks = pl.BlockSpec((T, W2), lambda h, i, j: (jnp.minimum(j, i), h))
    return pl.pallas_call(
        body, name='fox_fwd', grid=(H // HB, n, n),
        in_specs=[qs, ks, ks, pl.BlockSpec((T, LANES), lambda h, i, j: (i, 0)),
                  pl.BlockSpec((Hp, T), lambda h, i, j: (0, jnp.minimum(j, i)))],
        out_specs=[qs, pl.BlockSpec((HB, T, LANES), lambda h, i, j: (h, i, 0))],
        out_shape=[jax.ShapeDtypeStruct((S, FW), F32), jax.ShapeDtypeStruct((H, S, LANES), F32)],
        scratch_shapes=[pltpu.VMEM((HB, T, 1), F32), pltpu.VMEM((HB, T, 1), F32), pltpu.VMEM((HB, T, HEAD_DIM), F32)],
        compiler_params=_params(('parallel', 'parallel', 'arbitrary')))(qn, kn, vb, c, ct)


def _fox_p_ds(q_ref, k_ref, v_ref, do_ref, c_ref, ct_ref, lse_ref, dl_ref, h, T, diagonal):
    s = _fox_logits(q_ref[...], k_ref[...], c_ref[...], ct_ref[...], h, T, diagonal)
    p = jnp.exp(s - jnp.tile(lse_ref[...], (1, T // LANES)))
    dp = lax.dot_general(do_ref[...], v_ref[...], _DIMS['nt'], preferred_element_type=F32)
    ds = p * (dp - jnp.tile(dl_ref[...], (1, T // LANES)))
    return p, dp, ds


def fox_bwd_q(qn, kn, vb, do, c, ct, lse, dl, T):
    S, FW = qn.shape
    H = FW // HEAD_DIM
    Hp = ct.shape[0]
    n = S // T
    HB = _tile(H, (8, 4, 2, 1))
    W2 = HB * HEAD_DIM

    def body(q_ref, k_ref, v_ref, do_ref, c_ref, ct_ref, lse_ref, dl_ref, dq_ref, dl2_ref, acc_s, rs_s):
        hb, i, j = pl.program_id(0), pl.program_id(1), pl.program_id(2)

        @pl.when(j == 0)
        def _():
            acc_s[...] = jnp.zeros_like(acc_s)
            rs_s[...] = jnp.zeros_like(rs_s)

        def step(diagonal):
            for hh in range(HB):
                sl = slice(hh * HEAD_DIM, (hh + 1) * HEAD_DIM)
                p, dp, ds = _fox_p_ds(q_ref.at[:, sl], k_ref.at[:, sl], v_ref.at[:, sl], do_ref.at[:, sl], c_ref, ct_ref,
                                      lse_ref.at[hh], dl_ref.at[hh], hb * HB + hh, T, diagonal)
                acc_s[hh] += jnp.dot(ds.astype(BF16), k_ref[:, sl], preferred_element_type=F32)
                rs_s[hh] += jnp.sum(p * dp, axis=1, keepdims=True)

        _below_and_on_diagonal(i, j, step)

        @pl.when(j == i)
        def _():
            for hh in range(HB):
                dq_ref[:, hh * HEAD_DIM:(hh + 1) * HEAD_DIM] = acc_s[hh] * (1.0 / math.sqrt(HEAD_DIM))
                dl2_ref[hh] = jnp.broadcast_to(rs_s[hh], (T, LANES))

    qs = pl.BlockSpec((T, W2), lambda h, i, j: (i, h))
    ks = pl.BlockSpec((T, W2), lambda h, i, j: (jnp.minimum(j, i), h))
    st = pl.BlockSpec((HB, T, LANES), lambda h, i, j: (h, i, 0))
    return pl.pallas_call(
        body, name='fox_bwd_q', grid=(H // HB, n, n),
        in_specs=[qs, ks, ks, qs, pl.BlockSpec((T, LANES), lambda h, i, j: (i, 0)),
                  pl.BlockSpec((Hp, T), lambda h, i, j: (0, jnp.minimum(j, i))), st, st],
        out_specs=[qs, st], out_shape=[jax.ShapeDtypeStruct((S, FW), F32), jax.ShapeDtypeStruct((H, S, LANES), F32)],
        scratch_shapes=[pltpu.VMEM((HB, T, HEAD_DIM), F32), pltpu.VMEM((HB, T, 1), F32)],
        compiler_params=_params(('parallel', 'parallel', 'arbitrary')))(qn, kn, vb, do, c, ct, lse, dl)


def fox_bwd_kv(qn, kn, vb, do, c, ct, lse, dl, T):
    S, FW = qn.shape
    H = FW // HEAD_DIM
    Hp = ct.shape[0]
    n = S // T

    HB = _tile(H, (8, 4, 2, 1))
    W2 = HB * HEAD_DIM

    def body(q_ref, k_ref, v_ref, do_ref, c_ref, ct_ref, lse_ref, dl_ref, dk_ref, dv_ref, dc_ref, dk_s, dv_s, dc_s):
        hb, j, i = pl.program_id(0), pl.program_id(1), pl.program_id(2)

        @pl.when(i == 0)
        def _():
            dk_s[...] = jnp.zeros_like(dk_s)
            dv_s[...] = jnp.zeros_like(dv_s)
            dc_s[...] = jnp.zeros_like(dc_s)

        def step(diagonal):
            for hh in range(HB):
                sl = slice(hh * HEAD_DIM, (hh + 1) * HEAD_DIM)
                p, _, ds = _fox_p_ds(q_ref.at[:, sl], k_ref.at[:, sl], v_ref.at[:, sl], do_ref.at[:, sl], c_ref, ct_ref,
                                     lse_ref.at[hh], dl_ref.at[hh], hb * HB + hh, T, diagonal)
                dv_s[hh] += lax.dot_general(p.astype(BF16), do_ref[:, sl], _DIMS['tn'], preferred_element_type=F32)
                dk_s[hh] += lax.dot_general(ds.astype(BF16), q_ref[:, sl], _DIMS['tn'], preferred_element_type=F32)
                dc_s[hh] += jnp.sum(ds, axis=0, keepdims=True)

        _below_and_on_diagonal(i, j, step)

        @pl.when(i == n - 1)
        def _():
            for hh in range(HB):
                sl = slice(hh * HEAD_DIM, (hh + 1) * HEAD_DIM)
                dk_ref[:, sl] = dk_s[hh] * (1.0 / math.sqrt(HEAD_DIM))
                dv_ref[:, sl] = dv_s[hh].astype(BF16)
                dc_ref[hh] = -dc_s[hh]

    qs = pl.BlockSpec((T, W2), lambda h, j, i: (jnp.maximum(i, j), h))
    ks = pl.BlockSpec((T, W2), lambda h, j, i: (j, h))
    st = pl.BlockSpec((HB, T, LANES), lambda h, j, i: (h, jnp.maximum(i, j), 0))
    return pl.pallas_call(
        body, name='fox_bwd_kv', grid=(H // HB, n, n),
        in_specs=[qs, ks, ks, qs, pl.BlockSpec((T, LANES), lambda h, j, i: (jnp.maximum(i, j), 0)),
                  pl.BlockSpec((Hp, T), lambda h, j, i: (0, j)), st, st],
        out_specs=[ks, ks, pl.BlockSpec((HB, 1, T), lambda h, j, i: (h, 0, j))],
        out_shape=[jax.ShapeDtypeStruct((S, FW), F32), jax.ShapeDtypeStruct((S, FW), BF16),
                   jax.ShapeDtypeStruct((H, 1, S), F32)],
        scratch_shapes=[pltpu.VMEM((HB, T, HEAD_DIM), F32), pltpu.VMEM((HB, T, HEAD_DIM), F32),
                        pltpu.VMEM((HB, 1, T), F32)],
        compiler_params=_params(('parallel', 'parallel', 'arbitrary')))(qn, kn, vb, do, c, ct, lse, dl)


def _shift_down(v, d, rows, fill):
    return jnp.where(rows >= d, pltpu.roll(v, d, 0), fill)


def _shift_up(v, d, rows, S, fill):
    return jnp.where(rows < S - d, pltpu.roll(v, S - d, 0), fill)


SUBLANES = 8


def _scan_by_doubling(a, b, pos, span, reverse):
    n = a.shape[0]
    d = 1
    while d < span:
        if reverse:
            keep = pos < span - d
            a_s, b_s = jnp.where(keep, pltpu.roll(a, n - d, 0), 1.0), jnp.where(keep, pltpu.roll(b, n - d, 0), 0.0)
        else:
            keep = pos >= d
            a_s, b_s = jnp.where(keep, pltpu.roll(a, d, 0), 1.0), jnp.where(keep, pltpu.roll(b, d, 0), 0.0)
        b = a * b_s + b
        a = a * a_s
        d *= 2
    return a, b


def _scan(a, b, rows, S, reverse, scr):
    groups = S // SUBLANES
    a, b = _scan_by_doubling(a, b, jnp.bitwise_and(rows, SUBLANES - 1), SUBLANES, reverse)
    scr[0][...] = a
    scr[1][...] = b
    edge = 0 if reverse else SUBLANES - 1
    a_g = scr[0][pl.ds(edge, groups, stride=SUBLANES), :]
    b_g = scr[1][pl.ds(edge, groups, stride=SUBLANES), :]
    g_pos = lax.broadcasted_iota(jnp.int32, (groups, LANES), 0)
    _, h_g = _scan_by_doubling(a_g, b_g, g_pos, groups, reverse)
    if reverse:
        carry = jnp.where(g_pos < groups - 1, pltpu.roll(h_g, groups - 1, 0), 0.0)
    else:
        carry = jnp.where(g_pos >= 1, pltpu.roll(h_g, 1, 0), 0.0)
    for r in range(SUBLANES):
        scr[0][pl.ds(r, groups, stride=SUBLANES), :] = carry
    return b + a * scr[0][...]


def _lru_forward(u, cw, cb, wra, bra, wri, bri, lam, rows, scr):
    uc = cb + cw[CONV_W - 1] * u
    for d in range(1, CONV_W):
        uc = uc + cw[CONV_W - 1 - d] * _shift_down(u, d, rows, 0.0)
    ucb = uc.astype(BF16)
    r = _sigmoid(jnp.dot(ucb, wra.astype(BF16), preferred_element_type=F32) + bra)
    ig = _sigmoid(jnp.dot(ucb, wri.astype(BF16), preferred_element_type=F32) + bri)
    sp = _softplus(-lam)
    log_a = -LRU_C * r * sp
    a = jnp.exp(log_a)
    sq = jnp.sqrt(_neg_expm1(2.0 * log_a))
    iu = ig * uc
    hseq = _scan(a, sq * iu, rows, u.shape[0], False, scr)
    return uc, ucb, r, ig, sp, a, sq, iu, hseq


def _lru_specs(S, n_u, n_g):
    col = lambda off: pl.BlockSpec((S, LANES), lambda cbk: (0, off + cbk))
    vec = pl.BlockSpec((1, LANES), lambda cbk: (0, cbk))
    mat = pl.BlockSpec((None, LANES, LANES), lambda cbk: (cbk, 0, 0))
    cw = pl.BlockSpec((CONV_W, LANES), lambda cbk: (0, cbk))
    return col, vec, mat, cw


def lru_fwd(proj, conv_w, conv_b, w_ra, b_ra, w_ri, b_ri, lam, u_off, g_off):
    S = proj.shape[0]
    nb = w_ra.shape[0]
    col, vec, mat, cws = _lru_specs(S, u_off, g_off)

    def body(u_ref, g_ref, cw_ref, cb_ref, wra_ref, bra_ref, wri_ref, bri_ref, lam_ref, y_ref, scr0, scr1):
        rows = lax.broadcasted_iota(jnp.int32, (S, LANES), 0)
        cw = [cw_ref[t:t + 1, :] for t in range(CONV_W)]
        hseq = _lru_forward(u_ref[...], cw, cb_ref[...], wra_ref[...], bra_ref[...], wri_ref[...],
                            bri_ref[...], lam_ref[...], rows, (scr0, scr1))[-1]
        y_ref[...] = hseq * _gelu_and_grad(g_ref[...])[0]

    return pl.pallas_call(
        body, name='lru_fwd', grid=(nb,),
        in_specs=[col(u_off), col(g_off), cws, vec, mat, vec, mat, vec, vec], out_specs=col(0),
        out_shape=jax.ShapeDtypeStruct((S, nb * LANES), F32),
        scratch_shapes=[pltpu.VMEM((S, LANES), F32), pltpu.VMEM((S, LANES), F32)],
        compiler_params=_params(('parallel',)))(proj, proj, conv_w, conv_b, w_ra, b_ra, w_ri, b_ri, lam)


def lru_bwd(proj, dy, conv_w, conv_b, w_ra, b_ra, w_ri, b_ri, lam, u_off, g_off):
    S = proj.shape[0]
    nb = w_ra.shape[0]
    LW = nb * LANES
    col, vec, mat, cws = _lru_specs(S, u_off, g_off)

    def body(u_ref, g_ref, dy_ref, cw_ref, cb_ref, wra_ref, bra_ref, wri_ref, bri_ref, lam_ref,
             du_ref, dg_ref, dcw_ref, dcb_ref, dwra_ref, dbra_ref, dwri_ref, dbri_ref, dlam_ref, scr0, scr1):
        rows = lax.broadcasted_iota(jnp.int32, (S, LANES), 0)
        u, lam_v = u_ref[...], lam_ref[...]
        cw = [cw_ref[t:t + 1, :] for t in range(CONV_W)]
        wra, wri = wra_ref[...].astype(BF16), wri_ref[...].astype(BF16)
        uc, ucb, r, ig, sp, a, sq, iu, hseq = _lru_forward(u, cw, cb_ref[...], wra, bra_ref[...], wri, bri_ref[...],
                                                           lam_v, rows, (scr0, scr1))
        gl, dgl = _gelu_and_grad(g_ref[...])
        dy_v = dy_ref[...]
        dg_ref[...] = (dy_v * hseq * dgl).astype(BF16)
        G = _scan(_shift_up(a, 1, rows, S, 0.0), dy_v * gl, rows, S, True, (scr0, scr1))
        da = G * _shift_down(hseq, 1, rows, 0.0)
        diu = G * sq
        dsq = G * iu
        dlog_a = da * a - dsq * a * a / jnp.maximum(sq, 1e-30)
        dr = dlog_a * (-LRU_C * sp)
        dsp = jnp.sum(dlog_a * (-LRU_C * r), axis=0, keepdims=True)
        dlam_ref[...] = -dsp * _sigmoid(-lam_v)
        dzr = dr * r * (1.0 - r)
        dzi = diu * uc * ig * (1.0 - ig)
        dzrb, dzib = dzr.astype(BF16), dzi.astype(BF16)
        duc = (diu * ig + lax.dot_general(dzrb, wra, _DIMS['nt'], preferred_element_type=F32)
               + lax.dot_general(dzib, wri, _DIMS['nt'], preferred_element_type=F32))
        dwra_ref[...] = lax.dot_general(ucb, dzrb, _DIMS['tn'], preferred_element_type=F32)
        dwri_ref[...] = lax.dot_general(ucb, dzib, _DIMS['tn'], preferred_element_type=F32)
        dbra_ref[...] = jnp.sum(dzr, axis=0, keepdims=True)
        dbri_ref[...] = jnp.sum(dzi, axis=0, keepdims=True)
        dcb_ref[...] = jnp.sum(duc, axis=0, keepdims=True)
        du = cw[CONV_W - 1] * duc
        dcw_ref[CONV_W - 1:CONV_W, :] = jnp.sum(duc * u, axis=0, keepdims=True)
        for d in range(1, CONV_W):
            du = du + cw[CONV_W - 1 - d] * _shift_up(duc, d, rows, S, 0.0)
            dcw_ref[CONV_W - 1 - d:CONV_W - d, :] = jnp.sum(duc * _shift_down(u, d, rows, 0.0), axis=0, keepdims=True)
        du_ref[...] = du.astype(BF16)

    sd = jax.ShapeDtypeStruct
    return pl.pallas_call(
        body, name='lru_bwd', grid=(nb,),
        in_specs=[col(u_off), col(g_off), col(0), cws, vec, mat, vec, mat, vec, vec],
        out_specs=[col(0), col(0), cws, vec, mat, vec, mat, vec, vec],
        out_shape=[sd((S, LW), BF16), sd((S, LW), BF16), sd((CONV_W, LW), F32), sd((1, LW), F32),
                   sd((nb, LANES, LANES), F32), sd((1, LW), F32), sd((nb, LANES, LANES), F32), sd((1, LW), F32),
                   sd((1, LW), F32)],
        scratch_shapes=[pltpu.VMEM((S, LANES), F32), pltpu.VMEM((S, LANES), F32)],
        compiler_params=_params(('parallel',)))(proj, proj, dy, conv_w, conv_b, w_ra, b_ra, w_ri, b_ri, lam)


def mix_fwd(o_fox, y_lru, g_fox, g_lru):
    S, FW = o_fox.shape
    tr = _tile(S, (256, 128))

    def body(o_ref, y_ref, gf_ref, gl_ref, m_ref):
        m_ref[...] = jnp.concatenate([_rms(o_ref[...], gf_ref[...]), _rms(y_ref[...], gl_ref[...])],
                                     axis=1).astype(BF16)

    return _rows_call('mix_fwd', body, S, tr,
                      [(o_fox, _rb(tr, FW)), (y_lru, _rb(tr, FW)), (g_fox, _fb((1, FW))), (g_lru, _fb((1, FW)))],
                      [((S, 2 * FW), BF16, _rb(tr, 2 * FW))])[0]


def mix_bwd(o_fox, y_lru, g_fox, g_lru, dmix):
    S, FW = o_fox.shape
    H = FW // HEAD_DIM
    tr = _tile(S, (256, 128))

    def body(o_ref, y_ref, gf_ref, gl_ref, df_ref, dl_ref, do_ref, dlt_ref, dy_ref, dgf_ref, dgl_ref):
        o = o_ref[...]
        do, dgf = _rms_bwd(o, gf_ref[...], df_ref[...])
        dyl, dgl = _rms_bwd(y_ref[...], gl_ref[...], dl_ref[...])
        do_ref[...] = do.astype(BF16)
        dy_ref[...] = dyl
        prod = do * o
        for h in range(H):
            dlt_ref[h] = jnp.broadcast_to(
                jnp.sum(prod[:, h * HEAD_DIM:(h + 1) * HEAD_DIM], axis=1, keepdims=True), (tr, LANES))
        first = pl.program_id(0) == 0
        _acc_out(dgf_ref, first, dgf)
        _acc_out(dgl_ref, first, dgl)

    g = _fb((1, FW))
    return _rows_call('mix_bwd', body, S, tr,
                      [(o_fox, _rb(tr, FW)), (y_lru, _rb(tr, FW)), (g_fox, g), (g_lru, g), (dmix, _rb(tr, FW, 0)),
                       (dmix, _rb(tr, FW, 1))],
                      [((S, FW), BF16, _rb(tr, FW)), ((H, S, LANES), F32, pl.BlockSpec((H, tr, LANES), lambda i: (0, i, 0))),
                       ((S, FW), F32, _rb(tr, FW)), ((1, FW), F32, g), ((1, FW), F32, g)])


def _xattn_heads(cq_raw, ckv, g_cq, g_ck, XW):
    out = []
    for h in range(XW // HEAD_DIM):
        sl = slice(h * HEAD_DIM, (h + 1) * HEAD_DIM)
        out.append((cq_raw[:, sl], _rms(cq_raw[:, sl], g_cq), ckv[:, sl], _rms(ckv[:, sl], g_ck),
                    ckv[:, XW + h * HEAD_DIM:XW + (h + 1) * HEAD_DIM].astype(BF16)))
    return out


def xattn_fwd(cq_raw, ckv, g_cq, g_ck):
    S, XW = cq_raw.shape
    M = ckv.shape[0]
    tr = _tile(S, (512, 256, 128))

    def body(q_ref, kv_ref, gq_ref, gk_ref, o_ref):
        outs = []
        for _, qn, _, kn, v in _xattn_heads(q_ref[...], kv_ref[...], gq_ref[...], gk_ref[...], XW):
            s = lax.dot_general(qn.astype(BF16), kn.astype(BF16), _DIMS['nt'], preferred_element_type=F32)
            s = s / math.sqrt(HEAD_DIM)
            p = jnp.exp(s - jnp.max(s, axis=1, keepdims=True))
            p = p / jnp.sum(p, axis=1, keepdims=True)
            outs.append(jnp.dot(p.astype(BF16), v, preferred_element_type=F32))
        o_ref[...] = jnp.concatenate(outs, axis=1).astype(BF16)

    g = _fb((1, HEAD_DIM))
    return _rows_call('xattn_fwd', body, S, tr,
                      [(cq_raw, _rb(tr, XW)), (ckv, _fb((M, 2 * XW))), (g_cq, g), (g_ck, g)],
                      [((S, XW), BF16, _rb(tr, XW))])[0]


def xattn_bwd(cq_raw, ckv, g_cq, g_ck, do):
    S, XW = cq_raw.shape
    M = ckv.shape[0]
    tr = _tile(S, (512, 256, 128))
    n = S // tr

    def body(q_ref, kv_ref, gq_ref, gk_ref, do_ref, dq_ref, dkv_ref, dgq_ref, dgk_ref):
        i = pl.program_id(0)
        do_v = do_ref[...]
        dqs, dkn, dvs = [], [], []
        dgq = jnp.zeros((1, HEAD_DIM), F32)
        for h, (q_raw, qn, _, kn, v) in enumerate(_xattn_heads(q_ref[...], kv_ref[...], gq_ref[...], gk_ref[...], XW)):
            qb, kb = qn.astype(BF16), kn.astype(BF16)
            doh = do_v[:, h * HEAD_DIM:(h + 1) * HEAD_DIM]
            s = lax.dot_general(qb, kb, _DIMS['nt'], preferred_element_type=F32) / math.sqrt(HEAD_DIM)
            p = jnp.exp(s - jnp.max(s, axis=1, keepdims=True))
            p = p / jnp.sum(p, axis=1, keepdims=True)
            dp = lax.dot_general(doh, v, _DIMS['nt'], preferred_element_type=F32)
            ds = (p * (dp - jnp.sum(p * dp, axis=1, keepdims=True)) / math.sqrt(HEAD_DIM)).astype(BF16)
            dvs.append(lax.dot_general(p.astype(BF16), doh, _DIMS['tn'], preferred_element_type=F32))
            dkn.append(lax.dot_general(ds, qb, _DIMS['tn'], preferred_element_type=F32))
            dq, g1 = _rms_bwd(q_raw, gq_ref[...], jnp.dot(ds, kb, preferred_element_type=F32))
            dqs.append(dq)
            dgq = dgq + g1
        dq_ref[...] = jnp.concatenate(dqs, axis=1).astype(BF16)
        first = i == 0
        _acc_out(dgq_ref, first, dgq)
        _acc_out(dkv_ref, first, jnp.concatenate(dkn + dvs, axis=1))

        @pl.when(i == n - 1)
        def _():
            kv = kv_ref[...]
            acc = dkv_ref[...]
            dk, gk = _heads(lambda t, d: _rms_bwd(t, gk_ref[...], d), XW // HEAD_DIM, kv[:, :XW], acc[:, :XW])
            dkv_ref[:, :XW] = dk
            dgk_ref[...] = gk

    g = _fb((1, HEAD_DIM))
    return _rows_call('xattn_bwd', body, S, tr,
                      [(cq_raw, _rb(tr, XW)), (ckv, _fb((M, 2 * XW))), (g_cq, g), (g_ck, g), (do, _rb(tr, XW))],
                      [((S, XW), BF16, _rb(tr, XW)), ((M, 2 * XW), F32, _fb((M, 2 * XW))), ((1, HEAD_DIM), F32, g),
                       ((1, HEAD_DIM), F32, g)])


def gate_up_fwd(hf, w, F):
    S, D = hf.shape
    J, _, Nj = w.shape
    tm = _tile(S, (1024, 512, 256, 128))
    tn = _tile(Nj, (256, 128))
    per = Nj // tn
    half = J // 2 * per

    def body(a_ref, bg_ref, bu_ref, gu_ref, act_ref):
        a = a_ref[...]
        g = jnp.dot(a, bg_ref[...], preferred_element_type=F32)
        u = jnp.dot(a, bu_ref[...], preferred_element_type=F32)
        gu_ref[0] = g
        gu_ref[1] = u
        act_ref[...] = (g * _sigmoid(g) * u).astype(BF16)

    return pl.pallas_call(
        body, name='proj_gate_up', grid=(S // tm, half),
        in_specs=[pl.BlockSpec((tm, D), lambda m, n: (m, 0)),
                  pl.BlockSpec((None, D, tn), lambda m, n: (n // per, 0, n % per)),
                  pl.BlockSpec((None, D, tn), lambda m, n: ((n + half) // per, 0, n % per))],
        out_specs=[pl.BlockSpec((2, tm, tn), lambda m, n: (0, m, n)), pl.BlockSpec((tm, tn), lambda m, n: (m, n))],
        out_shape=[jax.ShapeDtypeStruct((2, S, F), F32), jax.ShapeDtypeStruct((S, F), BF16)],
        compiler_params=_params(('parallel', 'parallel')))(hf, w, w)


def down_bwd_x(dyb, w_down, gu, after):
    S, D = dyb.shape
    F = w_down.shape[0]
    tm = _tile(S, (1024, 512, 256, 128))
    tn = _tile(F, (512, 256, 128))

    def body(a_ref, b_ref, gu_ref, after_ref, o_ref):
        da = lax.dot_general(a_ref[...], b_ref[...], _DIMS['nt'], preferred_element_type=F32)
        g = gu_ref[0]
        sg = _sigmoid(g)
        o_ref[0] = (da * gu_ref[1] * sg * (1.0 + g * (1.0 - sg))).astype(BF16)
        o_ref[1] = (da * g * sg).astype(BF16)

    planes = pl.BlockSpec((2, tm, tn), lambda m, n: (0, m, n))
    return pl.pallas_call(
        body, name='bwd_down_x', grid=(S // tm, F // tn),
        in_specs=[pl.BlockSpec((tm, D), lambda m, n: (m, 0)), pl.BlockSpec((tn, D), lambda m, n: (n, 0)), planes, ANY],
        out_specs=planes, out_shape=jax.ShapeDtypeStruct((2, S, F), BF16),
        compiler_params=_params(('parallel', 'parallel')))(dyb, w_down, gu, after)


def down_fwd_loss(act, w_down, x2, target):
    S, F = act.shape
    D = w_down.shape[1]
    tm, tn, tk = _mm_tiles(S, D, F, F, act, w_down, F32, x2, tn_cands=(512, 256, 128))
    nk = F // tk

    def body(a_ref, b_ref, x_ref, t_ref, d_ref, db_ref, l_ref, acc):
        m, n, k = pl.program_id(0), pl.program_id(1), pl.program_id(2)
        part = jnp.dot(a_ref[...], b_ref[...], preferred_element_type=F32)

        @pl.when(k == 0)
        def _():
            acc[...] = part

        @pl.when(k > 0)
        def _():
            acc[...] += part

        @pl.when(k == nk - 1)
        def _():
            err = acc[...] + x_ref[...] - t_ref[...]
            d = err * (1.0 / D)
            d_ref[...] = d
            db_ref[...] = d.astype(BF16)
            tot = jnp.sum(jnp.sum(err * err, axis=1, keepdims=True), axis=0, keepdims=True) * (0.5 / D)
            _acc_out(l_ref, jnp.logical_and(m == 0, n == 0), jnp.broadcast_to(tot, (1, LANES)))

    tile = pl.BlockSpec((tm, tn), lambda m, n, k: (m, n))
    return pl.pallas_call(
        body, name='proj_down', grid=(S // tm, D // tn, nk),
        in_specs=[pl.BlockSpec((tm, tk), lambda m, n, k: (m, k)), pl.BlockSpec((tk, tn), lambda m, n, k: (k, n)), tile, tile],
        out_specs=[tile, tile, pl.BlockSpec((1, LANES), lambda m, n, k: (0, 0))],
        out_shape=[jax.ShapeDtypeStruct((S, D), F32), jax.ShapeDtypeStruct((S, D), BF16),
                   jax.ShapeDtypeStruct((1, LANES), F32)],
        scratch_shapes=[pltpu.VMEM((tm, tn), F32)],
        compiler_params=_params(('arbitrary', 'arbitrary', 'arbitrary')))(act, w_down, x2, target)


def swiglu_bwd(gu, dact, F, after):
    S = gu.shape[1]
    tr = _tile(S, (256, 128))
    tf = _tile(F, (1408, 1024, 512, 256, 128))
    nf = F // tf

    def body(gu_ref, da_ref, after_ref, o_ref):
        g, da = gu_ref[0], da_ref[...]
        sg = _sigmoid(g)
        o_ref[0] = (da * gu_ref[1] * sg * (1.0 + g * (1.0 - sg))).astype(BF16)
        o_ref[1] = (da * g * sg).astype(BF16)

    planes = pl.BlockSpec((2, tr, tf), lambda i, n: (0, i, n))
    return pl.pallas_call(
        body, name='swiglu_bwd', grid=(S // tr, nf),
        in_specs=[planes, pl.BlockSpec((tr, tf), lambda i, n: (i, n)), ANY],
        out_specs=planes, out_shape=jax.ShapeDtypeStruct((2, S, F), BF16),
        compiler_params=_params(('parallel', 'parallel')))(gu, dact, after)


def loss_head(y, target):
    S, D = y.shape
    tr = _tile(S, (256, 128))

    def body(y_ref, t_ref, d_ref, db_ref, l_ref):
        err = y_ref[...] - t_ref[...]
        d = err * (1.0 / D)
        d_ref[...] = d
        db_ref[...] = d.astype(BF16)
        part = jnp.sum(jnp.sum(err * err, axis=1, keepdims=True), axis=0, keepdims=True) * (0.5 / D)
        _acc_out(l_ref, pl.program_id(0) == 0, jnp.broadcast_to(part, (1, LANES)))

    return _rows_call('loss_head', body, S, tr, [(y, _rb(tr, D)), (target, _rb(tr, D))],
                      [((S, D), F32, _rb(tr, D)), ((S, D), BF16, _rb(tr, D)), ((1, LANES), F32, _fb((1, LANES)))])


def _adamw_math(w, gv, m, v):
    mn = ADAM_B1 * m + (1.0 - ADAM_B1) * gv
    vn = ADAM_B2 * v + (1.0 - ADAM_B2) * (gv * gv)
    m_hat = mn / (1.0 - ADAM_B1 ** ADAM_STEP)
    v_hat = vn / (1.0 - ADAM_B2 ** ADAM_STEP)
    return -ADAM_LR * (m_hat / (jnp.sqrt(v_hat) + ADAM_EPS) + ADAM_WD * w), mn, vn


def adamw(name, w, g, m, v):
    R, C = w.shape
    tr = _row_tile(R, C)

    def body(w_ref, g_ref, m_ref, v_ref, d_ref, mo_ref, vo_ref):
        d_ref[...], mo_ref[...], vo_ref[...] = _adamw_math(w_ref[...], g_ref[...], m_ref[...], v_ref[...])

    spec = _rb(tr, C)
    return _rows_call(name, body, R, tr, [(w, spec), (g, spec), (m, spec), (v, spec)], [((R, C), F32, spec)] * 3)


def adamw_halves(name, w, mine, other, m, v, c_idx):
    R, C = w.shape
    hr = R // 2
    tr = _row_tile(hr, C)

    def body(c_ref, w_ref, a_ref, b_ref, m_ref, v_ref, g_ref, d_ref, mo_ref, vo_ref):
        gv = jnp.where(pl.program_id(0) == c_ref[0], a_ref[...], b_ref[...])
        g_ref[...] = gv
        d_ref[...], mo_ref[...], vo_ref[...] = _adamw_math(w_ref[...], gv, m_ref[...], v_ref[...])

    full = pl.BlockSpec((None, tr, C), lambda hh, i, c_ref: (hh, i, 0))
    mine_spec = pl.BlockSpec((tr, C), lambda hh, i, c_ref: (jnp.where(hh == c_ref[0], i, 0), 0))
    other_spec = pl.BlockSpec((tr, C), lambda hh, i, c_ref: (jnp.where(hh == c_ref[0], 0, i), 0))
    outs = pl.pallas_call(
        body, name=name,
        grid_spec=pltpu.PrefetchScalarGridSpec(num_scalar_prefetch=1, grid=(2, hr // tr),
                                               in_specs=[full, mine_spec, other_spec, full, full], out_specs=[full] * 4),
        out_shape=[jax.ShapeDtypeStruct((2, hr, C), F32)] * 4,
        compiler_params=_params(('parallel', 'parallel')))(
            c_idx, w.reshape(2, hr, C), mine, other, m.reshape(2, hr, C), v.reshape(2, hr, C))
    return [o.reshape(R, C) for o in outs]


def _place():
    x, y, c = lax.axis_index('x'), lax.axis_index('y'), lax.axis_index('c')
    return x, y, c, [(1 - x, y), (x, 1 - y), (1 - x, 1 - y)]


def _rcopy(src, dst, ssem, rsem, dev):
    return pltpu.make_async_remote_copy(src_ref=src, dst_ref=dst, send_sem=ssem, recv_sem=rsem, device_id=dev,
                                        device_id_type=MESH)


HBM = pl.BlockSpec(memory_space=pltpu.HBM)
SEM = pl.BlockSpec(memory_space=pltpu.SEMAPHORE)
EFFECT = pltpu.SideEffectType.DATAFLOW_SIDE_EFFECTING


def _in_hbm(a):
    return pltpu.with_memory_space_constraint(a, pltpu.HBM)


def _rows_part(shape, whole, half):
    return pl.ds(0, shape[0]) if whole else pl.ds(half * (shape[0] // 2), shape[0] // 2)


def gather_start(name, shards, whole):
    nT = len(shards)

    def body(*refs):
        srcs, lands = refs[:nT], refs[nT:2 * nT]
        ssem, rsem, token = refs[2 * nT], refs[2 * nT + 1], refs[-1]
        x, y, c, chips = _place()
        for t in range(nT):
            rows = _rows_part(shards[t].shape, whole[t], c)
            for k, (px, py) in enumerate(chips):
                _rcopy(srcs[t].at[rows], lands[t].at[2 * x + y, rows], ssem.at[3 * t + k], rsem.at[3 * t + k],
                       (px, py, c)).start()
        token[...] = jnp.zeros_like(token)

    zones = [lax.empty((N_CHIPS,) + s.shape, s.dtype) for s in shards]
    outs = pl.pallas_call(
        body, name=name,
        out_shape=(pltpu.SemaphoreType.DMA((3 * nT,)), pltpu.SemaphoreType.DMA((3 * nT,)),
                   *[pltpu.HBM(s.shape, s.dtype) for s in shards], *[pltpu.HBM(z.shape, z.dtype) for z in zones],
                   jax.ShapeDtypeStruct((8, LANES), F32)),
        in_specs=[HBM] * (2 * nT), out_specs=(SEM, SEM, *[HBM] * (2 * nT), pl.BlockSpec(memory_space=pltpu.VMEM)),
        input_output_aliases={i: 2 + i for i in range(2 * nT)},
        compiler_params=pltpu.CompilerParams(has_side_effects=EFFECT))(*[_in_hbm(a) for a in list(shards) + zones])
    return outs[0], outs[1], outs[2:2 + nT], outs[2 + nT:2 + 2 * nT], outs[-1]


def gather_wait(name, t, shard, zone, ssem, rsem, after, whole):
    after = after if isinstance(after, (list, tuple)) else [after]

    def body(src_ref, land_ref, ssem_ref, rsem_ref, *rest):
        x, y, c, chips = _place()
        rows = _rows_part(shard.shape, whole, c)
        for k, (px, py) in enumerate(chips):
            cp = _rcopy(src_ref.at[rows], land_ref.at[2 * px + py, rows], ssem_ref.at[3 * t + k], rsem_ref.at[3 * t + k],
                        (px, py, c))
            cp.wait_send()
            cp.wait_recv()

    return pl.pallas_call(
        body, name=name, out_shape=(pltpu.HBM(shard.shape, shard.dtype), pltpu.HBM(zone.shape, zone.dtype)),
        in_specs=(HBM, HBM, SEM, SEM, *[ANY] * len(after)), out_specs=(HBM, HBM), input_output_aliases={0: 0, 1: 1},
        compiler_params=pltpu.CompilerParams(has_side_effects=EFFECT))(shard, zone, ssem, rsem, *after)


def pair_swap(name, zone):
    hr = zone.shape[1] // 2

    def body(z_in, z_ref, ssem, rsem):
        x, y, c, chips = _place()
        cps = []
        for k, (px, py) in enumerate(chips):
            blk = z_ref.at[2 * px + py, pl.ds(c * hr, hr)]
            cps.append(_rcopy(blk, blk, ssem.at[k], rsem.at[k], (x, y, 1 - c)))
            cps[-1].start()
        for k, (px, py) in enumerate(chips):
            blk = z_ref.at[2 * px + py, pl.ds((1 - c) * hr, hr)]
            _rcopy(blk, blk, ssem.at[k], rsem.at[k], (x, y, 1 - c)).wait_recv()
        for cp in cps:
            cp.wait_send()

    return pl.pallas_call(
        body, name=name, in_specs=[ANY], out_specs=ANY, out_shape=jax.ShapeDtypeStruct(zone.shape, zone.dtype),
        input_output_aliases={0: 0},
        scratch_shapes=[pltpu.SemaphoreType.DMA((3,)), pltpu.SemaphoreType.DMA((3,))],
        compiler_params=_params())(zone)


def _swap_copies(z_ref, ssem, rsem):
    hr = z_ref.shape[1] // 2
    x, y, c, chips = _place()
    pairs = []
    for k, (px, py) in enumerate(chips):
        mine = z_ref.at[2 * px + py, pl.ds(c * hr, hr)]
        theirs = z_ref.at[2 * px + py, pl.ds((1 - c) * hr, hr)]
        pairs.append((_rcopy(mine, mine, ssem.at[k], rsem.at[k], (x, y, 1 - c)),
                      _rcopy(theirs, theirs, ssem.at[k], rsem.at[k], (x, y, 1 - c))))
    return pairs


def swap_start(name, zone):
    def body(z_ref, ssem, rsem, z_out, token):
        for mine, _ in _swap_copies(z_ref, ssem, rsem):
            mine.start()
        token[...] = jnp.zeros_like(token)

    return pl.pallas_call(
        body, name=name,
        out_shape=(pltpu.SemaphoreType.DMA((3,)), pltpu.SemaphoreType.DMA((3,)), pltpu.HBM(zone.shape, zone.dtype),
                   jax.ShapeDtypeStruct((8, LANES), F32)),
        in_specs=[HBM], out_specs=(SEM, SEM, HBM, pl.BlockSpec(memory_space=pltpu.VMEM)), input_output_aliases={0: 2},
        compiler_params=pltpu.CompilerParams(has_side_effects=EFFECT))(_in_hbm(zone))


def swap_wait(name, zone, ssem, rsem, after):
    def body(z_ref, ssem_ref, rsem_ref, after_ref, z_out):
        for mine, theirs in _swap_copies(z_ref, ssem_ref, rsem_ref):
            mine.wait_send()
            theirs.wait_recv()

    return pl.pallas_call(
        body, name=name, out_shape=(pltpu.HBM(zone.shape, zone.dtype),),
        in_specs=(HBM, SEM, SEM, ANY), out_specs=(HBM,), input_output_aliases={0: 0},
        compiler_params=pltpu.CompilerParams(has_side_effects=EFFECT))(zone, ssem, rsem, after)[0]


N_SENDERS = 7


def _scatter_copies(g_ref, l_ref, ssem, rsem):
    x, y, c, chips = _place()
    cps = []
    for k, (px, py) in enumerate(chips):
        for d in range(2):
            to = (c + d) % 2
            cps.append(_rcopy(g_ref.at[2 * px + py, to], l_ref.at[2 * k + d], ssem.at[2 * k + d], rsem.at[2 * k + d],
                              (px, py, to)))
    cps.append(_rcopy(g_ref.at[2 * x + y, 1 - c], l_ref.at[6], ssem.at[6], rsem.at[6], (x, y, 1 - c)))
    return cps


def scatter_start(name, g):
    def body(g_ref, l_ref, ssem, rsem, g_out, l_out, token):
        for cp in _scatter_copies(g_ref, l_ref, ssem, rsem):
            cp.start()
        token[...] = jnp.zeros_like(token)

    zone = lax.empty((N_SENDERS,) + g.shape[2:], g.dtype)
    return pl.pallas_call(
        body, name=name,
        out_shape=(pltpu.SemaphoreType.DMA((N_SENDERS,)), pltpu.SemaphoreType.DMA((N_SENDERS,)),
                   pltpu.HBM(g.shape, g.dtype), pltpu.HBM(zone.shape, zone.dtype), jax.ShapeDtypeStruct((8, LANES), F32)),
        in_specs=[HBM, HBM], out_specs=(SEM, SEM, HBM, HBM, pl.BlockSpec(memory_space=pltpu.VMEM)),
        input_output_aliases={0: 2, 1: 3},
        compiler_params=pltpu.CompilerParams(has_side_effects=EFFECT))(_in_hbm(g), _in_hbm(zone))


def scatter_wait(name, g, zone, ssem, rsem, after):
    def body(g_ref, l_ref, ssem_ref, rsem_ref, after_ref, g_out, l_out):
        for cp in _scatter_copies(g_ref, l_ref, ssem_ref, rsem_ref):
            cp.wait_send()
            cp.wait_recv()

    return pl.pallas_call(
        body, name=name, out_shape=(pltpu.HBM(g.shape, g.dtype), pltpu.HBM(zone.shape, zone.dtype)),
        in_specs=(HBM, HBM, SEM, SEM, ANY), out_specs=(HBM, HBM), input_output_aliases={0: 0, 1: 1},
        compiler_params=pltpu.CompilerParams(has_side_effects=EFFECT))(g, zone, ssem, rsem, after)


def sum_parts(name, g, landed, chip_idx, c_idx):
    hr, C = g.shape[2:]
    tr = _row_tile(hr, C, min_rows=16)

    def body(me_ref, c_ref, g_ref, l_ref, o_ref):
        acc = g_ref[...].astype(F32)
        for s in range(N_SENDERS):
            acc = acc + l_ref[s].astype(F32)
        o_ref[...] = acc

    return pl.pallas_call(
        body, name=name,
        grid_spec=pltpu.PrefetchScalarGridSpec(
            num_scalar_prefetch=2, grid=(hr // tr,),
            in_specs=[pl.BlockSpec((None, None, tr, C), lambda i, me_ref, c_ref: (me_ref[0], c_ref[0], i, 0)),
                      pl.BlockSpec((N_SENDERS, tr, C), lambda i, me_ref, c_ref: (0, i, 0))],
            out_specs=pl.BlockSpec((tr, C), lambda i, me_ref, c_ref: (i, 0))),
        out_shape=jax.ShapeDtypeStruct((hr, C), F32),
        compiler_params=_params(('parallel',)))(chip_idx, c_idx, g, landed)


def pair_join(name, halves):
    nT = len(halves)

    def body(*refs):
        ins, outs = refs[:nT], refs[nT:2 * nT]
        ssem, rsem = refs[2 * nT:]
        x, y, c, _ = _place()
        cps = [_rcopy(ins[t], outs[t], ssem.at[t], rsem.at[t], (x, y, 1 - c)) for t in range(nT)]
        for cp in cps:
            cp.start()
        for cp in cps:
            cp.wait()

    return pl.pallas_call(
        body, name=name, in_specs=[ANY] * nT, out_specs=[ANY] * nT,
        out_shape=[jax.ShapeDtypeStruct(h.shape, h.dtype) for h in halves],
        scratch_shapes=[pltpu.SemaphoreType.DMA((nT,)), pltpu.SemaphoreType.DMA((nT,))],
        compiler_params=_params())(*halves)


N_DEVICES = 8


def _spread_copies(b_ref, l_ref, ssem, rsem):
    x, y, c, chips = _place()
    me = 4 * x + 2 * y + c
    pairs = []
    for px, py, pc in [(px, py, pc) for px, py in chips for pc in (c, 1 - c)] + [(x, y, 1 - c)]:
        it = 4 * px + 2 * py + pc
        pairs.append((_rcopy(b_ref, l_ref.at[me], ssem.at[it], rsem.at[me], (px, py, pc)),
                      _rcopy(b_ref, l_ref.at[it], ssem.at[it], rsem.at[it], (px, py, pc))))
    return pairs


def spread_start(name, buf):
    def body(b_ref, l_ref, ssem, rsem, b_out, l_out, token):
        for mine, _ in _spread_copies(b_ref, l_ref, ssem, rsem):
            mine.start()
        token[...] = jnp.zeros_like(token)

    zone = lax.empty((N_DEVICES,) + buf.shape, buf.dtype)
    return pl.pallas_call(
        body, name=name,
        out_shape=(pltpu.SemaphoreType.DMA((N_DEVICES,)), pltpu.SemaphoreType.DMA((N_DEVICES,)),
                   pltpu.HBM(buf.shape, buf.dtype), pltpu.HBM(zone.shape, zone.dtype), jax.ShapeDtypeStruct((8, LANES), F32)),
        in_specs=[HBM, HBM], out_specs=(SEM, SEM, HBM, HBM, pl.BlockSpec(memory_space=pltpu.VMEM)),
        input_output_aliases={0: 2, 1: 3},
        compiler_params=pltpu.CompilerParams(has_side_effects=EFFECT))(_in_hbm(buf), _in_hbm(zone))


def spread_wait(name, buf, zone, ssem, rsem, after):
    def body(b_ref, l_ref, ssem_ref, rsem_ref, after_ref, b_out, l_out):
        for mine, theirs in _spread_copies(b_ref, l_ref, ssem_ref, rsem_ref):
            mine.wait_send()
            theirs.wait_recv()

    return pl.pallas_call(
        body, name=name, out_shape=(pltpu.HBM(buf.shape, buf.dtype), pltpu.HBM(zone.shape, zone.dtype)),
        in_specs=(HBM, HBM, SEM, SEM, ANY), out_specs=(HBM, HBM), input_output_aliases={0: 0, 1: 1},
        compiler_params=pltpu.CompilerParams(has_side_effects=EFFECT))(buf, zone, ssem, rsem, after)


def sum_devices(name, zone):
    _, R, C = zone.shape
    tr = _row_tile(R, C)

    def body(z_ref, o_ref):
        acc = z_ref[0]
        for d in range(1, N_DEVICES):
            acc = acc + z_ref[d]
        o_ref[...] = acc

    return pl.pallas_call(
        body, name=name, grid=(R // tr,), in_specs=[pl.BlockSpec((N_DEVICES, tr, C), lambda i: (0, i, 0))],
        out_specs=pl.BlockSpec((tr, C), lambda i: (i, 0)), out_shape=jax.ShapeDtypeStruct((R, C), F32),
        compiler_params=_params(('parallel',)))(zone)


class _InWindows:
    def __init__(self, FW, LW, H, C):
        gap = LANES - H
        padded = lambda o: o if o < 3 * FW + H else o + gap
        self.width = 3 * FW + LANES + 2 * LW
        self.f_block = 3 * FW // LANES
        self.first = [padded(C * j) // LANES for j in range(N_CHIPS)]
        self.blocks = max(padded(C * (j + 1) - 1) // LANES - self.first[j] + 1 for j in range(N_CHIPS))
        assert all((b + self.blocks) * LANES <= self.width for b in self.first)
        self.cols = self.blocks * LANES
        self.runs = []
        for j in range(N_CHIPS):
            cut = min(max(3 * FW + H - C * j, 0), C)
            spans = [(0, cut), (cut, C)]
            self.runs.append([(t0, t1, padded(C * j + t0) - LANES * self.first[j]) for t0, t1 in spans if t1 > t0])

    def to_window(self, shard, chip):
        def place(j, s):
            parts, pos = [], 0
            for t0, t1, w0 in self.runs[j]:
                parts += [jnp.zeros((s.shape[0], w0 - pos), s.dtype), s[:, t0:t1]]
                pos = w0 + t1 - t0
            parts.append(jnp.zeros((s.shape[0], self.cols - pos), s.dtype))
            return jnp.concatenate([p for p in parts if p.shape[1]], axis=1)
        return lax.switch(chip, [functools.partial(place, j) for j in range(N_CHIPS)], shard)

    def from_window(self, win, chip):
        def take(j, w):
            return jnp.concatenate([w[:, w0:w0 + t1 - t0] for t0, t1, w0 in self.runs[j]], axis=1)
        return lax.switch(chip, [functools.partial(take, j) for j in range(N_CHIPS)], win)

    def _spans(self, j):
        b0, b1 = self.first[j], self.first[j] + self.blocks
        return (b0, min(b1, self.f_block)), b0 <= self.f_block < b1, (max(b0, self.f_block + 1), b1)

    def assemble(self, zone):
        main, f_blk = None, None
        for j in range(N_CHIPS):
            (a0, a1), has_f, (c0, c1) = self._spans(j)
            for p0, p1, shift in ((a0, a1, 0), (c0, c1, 1)):
                if p1 > p0:
                    part = zone[j][:, (p0 - self.first[j]) * LANES:(p1 - self.first[j]) * LANES]
                    part = jnp.pad(part, ((0, 0), ((p0 - shift) * LANES, self.width - LANES - (p1 - shift) * LANES)))
                    main = part if main is None else main + part
            if has_f:
                part = zone[j][:, (self.f_block - self.first[j]) * LANES:(self.f_block - self.first[j] + 1) * LANES]
                f_blk = part if f_blk is None else f_blk + part
        return main, f_blk

    def windows(self, main, f_blk):
        out = []
        for j in range(N_CHIPS):
            (a0, a1), has_f, (c0, c1) = self._spans(j)
            parts = [main[:, a0 * LANES:a1 * LANES]] if a1 > a0 else []
            parts += [f_blk] if has_f else []
            parts += [main[:, (c0 - 1) * LANES:(c1 - 1) * LANES]] if c1 > c0 else []
            out.append(jnp.concatenate(parts, axis=1))
        return jnp.stack(out)


_PACK = 8 * LANES


PACK_ROWS = 256


def _pack(arrs):
    flat = []
    for a in arrs:
        v = a.reshape(-1).astype(F32)
        flat.append(jnp.pad(v, (0, (-v.shape[0]) % _PACK)))
    rows = sum(v.shape[0] for v in flat) // LANES
    flat.append(jnp.zeros(((-rows) % PACK_ROWS) * LANES, F32))
    return jnp.concatenate(flat).reshape(-1, LANES)


def _unpack(buf, shapes):
    out, off = [], 0
    flat = buf.reshape(-1)
    for sh in shapes:
        n = math.prod(sh)
        out.append(flat[off:off + n].reshape(sh))
        off += n + (-n) % _PACK
    return out


def kernel(x, mem, g_mix, w_in, b_f, g_q, g_k, conv_w, conv_b, w_ra, b_ra, w_ri, b_ri, lam, g_fox_out, g_lru_out, w_out, g_xattn, g_mem, w_cq, w_ckv, g_cq, g_ck, w_co, g_ffn, w_gate_up, w_down, loss_target, m_g_mix, m_w_in, m_b_f, m_g_q, m_g_k, m_conv_w, m_conv_b, m_w_ra, m_b_ra, m_w_ri, m_b_ri, m_lam, m_g_fox_out, m_g_lru_out, m_w_out, m_g_xattn, m_g_mem, m_w_cq, m_w_ckv, m_g_cq, m_g_ck, m_w_co, m_g_ffn, m_w_gate_up, m_w_down, v_g_mix, v_w_in, v_b_f, v_g_q, v_g_k, v_conv_w, v_conv_b, v_w_ra, v_b_ra, v_w_ri, v_b_ri, v_lam, v_g_fox_out, v_g_lru_out, v_w_out, v_g_xattn, v_g_mem, v_w_cq, v_w_ckv, v_g_cq, v_g_ck, v_w_co, v_g_ffn, v_w_gate_up, v_w_down):
    given = dict(locals())
    W = {n: given[n][0] for n in WEIGHTS}
    M1 = {n: given['m_' + n][0] for n in WEIGHTS}
    V1 = {n: given['v_' + n][0] for n in WEIGHTS}
    xs, ms, tgt = x[0], mem[0], loss_target[0]
    S, D = xs.shape
    H = W['b_f'].shape[0]
    FW = H * HEAD_DIM
    LW = W['lam'].shape[0]
    nb = W['w_ra'].shape[0]
    XW = W['w_cq'].shape[1]
    F = W['w_down'].shape[0] * N_CHIPS
    IN_W = W['w_in'].shape[1] * N_CHIPS
    assert FW == LW and LW == nb * LANES and IN_W == 3 * FW + H + 2 * LW and H <= 8
    T = _tile(S, (512, 256, 128))
    c_idx = lax.axis_index('c').astype(jnp.int32).reshape(1)
    chip = 2 * lax.axis_index('x') + lax.axis_index('y')
    chip_idx = chip.astype(jnp.int32).reshape(1)
    vec = lambda n: W[n].reshape(1, -1)

    wins = _InWindows(FW, LW, H, W['w_in'].shape[1])
    started = {}
    g_tok = jnp.zeros((1, 1), F32)
    for call, names in (('gather_start_first', ['conv_w', 'w_in']), ('gather_start_rest', BIG[1:])):
        own = [W[n].reshape(-1, LANES) if n == 'conv_w' else W[n].astype(BF16) + g_tok.astype(BF16) for n in names]
        own = [wins.to_window(o, chip) if n == 'w_in' else o for n, o in zip(names, own)]
        ssem, rsem, srcs, zones, tok = gather_start(call, own, [n == 'conv_w' for n in names])
        g_tok = tok[0:1, 0:1]
        started.update({n: (t, srcs[t], zones[t], ssem, rsem) for t, n in enumerate(names)})

    def fetch(n, after):
        t, g_src, g_zone, g_ssem, g_rsem = started[n]
        src, zone = gather_wait('gather_wait_' + n, t, g_src, g_zone, g_ssem, g_rsem, after, n == 'conv_w')
        if n != 'conv_w':
            zone = pair_swap('pair_swap_' + n, zone)
        return lax.dynamic_update_index_in_dim(zone, src, chip, 0)

    def fetch_begin(n, after):
        t, g_src, g_zone, g_ssem, g_rsem = started[n]
        src, zone = gather_wait('gather_wait_' + n, t, g_src, g_zone, g_ssem, g_rsem, after, False)
        ssem, rsem, zone, _ = swap_start('swap_start_' + n, zone)
        return src, zone, ssem, rsem

    def fetch_end(n, begun, after):
        src, zone, ssem, rsem = begun
        return lax.dynamic_update_index_in_dim(swap_wait('swap_wait_' + n, zone, ssem, rsem, after), src, chip, 0)

    b_f_pad = jnp.pad(vec('b_f'), ((0, 0), (0, LANES - H)))
    u_off, g_off = 3 * FW // LANES, (3 * FW + LW) // LANES

    h1 = norm_fwd('norm_mix', xs, vec('g_mix') + g_tok[0:1, 0:1])
    conv_full = fetch('conv_w', h1).reshape(N_CHIPS, CONV_W, LW // N_CHIPS).transpose(1, 0, 2).reshape(CONV_W, LW)
    w5, wf = wins.assemble(fetch('w_in', [h1, M1['w_in'], V1['w_in']]))
    proj = _mm('proj_in', h1, w5, 'nn', F32)
    f_raw = _mm('proj_f', h1, wf, 'nn', F32)
    qn, kn, vb = qkv_fwd(proj, vec('g_q'), vec('g_k'), FW)
    cc = fgate_fwd(f_raw, b_f_pad)
    ct = cc[:, :8].T
    o_fox, lse = fox_fwd(qn, kn, vb, cc, ct, T)
    lru_w = (conv_full, vec('conv_b'), W['w_ra'], vec('b_ra'), W['w_ri'], vec('b_ri'), vec('lam'))
    y_lru = lru_fwd(proj, *lru_w, u_off, g_off)
    mixn = mix_fwd(o_fox, y_lru, vec('g_fox_out'), vec('g_lru_out'))
    w_out_f = fetch('w_out', mixn).reshape(2 * FW, D)
    begun = {n: fetch_begin(n, mixn) for n in ('w_cq', 'w_ckv', 'w_co')}
    x1 = _mm('proj_out', mixn, w_out_f, 'nn', F32, res=xs)

    hq = norm_fwd('norm_xq', x1, vec('g_xattn'))
    mn = norm_fwd('norm_mem', ms, vec('g_mem'))
    w_cq_f = fetch_end('w_cq', begun['w_cq'], hq).reshape(D, XW)
    w_ckv_f = fetch_end('w_ckv', begun['w_ckv'], hq).reshape(D, 2 * XW)
    cq_raw = _mm('proj_cq', hq, w_cq_f, 'nn', F32)
    ckv = _mm('proj_ckv', mn, w_ckv_f, 'nn', F32)
    o_x = xattn_fwd(cq_raw, ckv, vec('g_cq'), vec('g_ck'))
    begun['w_gate_up'] = fetch_begin('w_gate_up', o_x)
    w_co_g = fetch_end('w_co', begun['w_co'], o_x)
    x2 = _mm_colsharded('proj_co', o_x, w_co_g, F32, res=x1)

    hf = norm_fwd('norm_ffn', x2, vec('g_ffn'))
    begun['w_down'] = fetch_begin('w_down', hf)
    w_gu_g = fetch_end('w_gate_up', begun['w_gate_up'], hf)
    gu, act = gate_up_fwd(hf, w_gu_g, F)
    w_down_f = fetch_end('w_down', begun['w_down'], act).reshape(F, D)
    dy, dyb, loss_blk = down_fwd_loss(act, w_down_f, x2, tgt)

    gw, pending = {}, []

    def reduce_begin(n, g):
        sp = g.reshape(N_CHIPS, 2, g.shape[1] // 2, g.shape[2])
        ssem, rsem, sp, zone, tok = scatter_start('scatter_start_' + n, sp)
        pending.append((n, sp, zone, ssem, rsem))
        return tok[0:1, 0:1]

    t_down = reduce_begin('w_down', _mm('bwd_down_w', act, dyb, 'tn', BF16).reshape(N_CHIPS, F // N_CHIPS, D))
    dgu = down_bwd_x(dyb, w_down_f, gu, t_down)
    dhf = _mm_colsharded_t('bwd_gate_up_x', dgu, w_gu_g, F32)
    t_gu = reduce_begin('w_gate_up', _mm_grad_colsharded('bwd_gate_up_w', hf, dgu, N_CHIPS, BF16))
    dx2, dx2b, gw['g_ffn'] = norm_bwd('norm_ffn_bwd', x2, vec('g_ffn') + t_down + t_gu, dhf, res=dy)

    do_x = _mm_colsharded_t('bwd_co_x', dx2b, w_co_g, BF16)
    t_co = reduce_begin('w_co', _mm_grad_colsharded('bwd_co_w', o_x, dx2b, N_CHIPS, BF16))
    dcq_raw, dckv, gw['g_cq'], gw['g_ck'] = xattn_bwd(cq_raw, ckv, vec('g_cq') + t_co, vec('g_ck'), do_x)
    dhq = _mm('bwd_cq_x', dcq_raw, w_cq_f, 'nt', F32)
    t_cq = reduce_begin('w_cq', _mm('bwd_cq_w', hq, dcq_raw, 'tn', BF16).reshape(N_CHIPS, D // N_CHIPS, XW))
    dmn = _mm('bwd_ckv_x', dckv, w_ckv_f, 'nt', F32)
    t_ckv = reduce_begin('w_ckv', _mm('bwd_ckv_w', mn, dckv, 'tn', BF16).reshape(N_CHIPS, D // N_CHIPS, 2 * XW))
    (gw['g_mem'],) = norm_bwd('norm_mem_bwd', ms, vec('g_mem'), dmn, want_dx=False)
    dx1, dx1b, gw['g_xattn'] = norm_bwd('norm_xq_bwd', x1, vec('g_xattn') + t_cq + t_ckv, dhq, res=dx2)

    dmix = _mm('bwd_out_x', dx1b, w_out_f, 'nt', F32)
    t_out = reduce_begin('w_out', _mm('bwd_out_w', mixn, dx1b, 'tn', BF16).reshape(N_CHIPS, 2 * FW // N_CHIPS, D))
    do_fox, delta, dy_lru, gw['g_fox_out'], gw['g_lru_out'] = mix_bwd(o_fox, y_lru, vec('g_fox_out') + t_out,
                                                                     vec('g_lru_out'), dmix)
    (du, dgate, gw['conv_w'], gw['conv_b'], gw['w_ra'], gw['b_ra'], gw['w_ri'], gw['b_ri'],
     gw['lam']) = lru_bwd(proj, dy_lru, *lru_w, u_off, g_off)
    early = [n for n in SMALL if n not in ('g_q', 'g_k', 'b_f', 'g_mix')]
    late = [n for n in SMALL if n not in early]
    e_ssem, e_rsem, e_buf, e_zone, e_tok = spread_start('spread_start_early', _pack([gw[n] for n in early]))
    dqn, delta2 = fox_bwd_q(qn, kn, vb, do_fox, cc, ct, lse, delta, T)
    dkn, dv, dct = fox_bwd_kv(qn, kn, vb, do_fox, cc, ct, lse, delta2, T)
    dq, dk, gw['g_q'], gw['g_k'] = qkv_bwd(proj, vec('g_q') + e_tok[0:1, 0:1], vec('g_k'), dqn, dkn, FW)
    dc = jnp.pad(dct.reshape(H, S).T, ((0, 0), (0, LANES - H)))
    df, db_f = fgate_bwd(f_raw, b_f_pad, dc, H)
    gw['b_f'] = db_f[:, :H]
    dproj = jnp.concatenate([dq, dk, dv, du, dgate], axis=1)
    dw5 = _mm('bwd_in_w', h1, dproj, 'tn', BF16)
    dwf = _mm('bwd_f_w', h1, df, 'tn', BF16)
    t_in = reduce_begin('w_in', wins.windows(dw5, dwf))
    dh_a = _mm('bwd_f_x', df, wf, 'nt', F32)
    dh1 = _mm('bwd_in_x', dproj, w5, 'nt', F32, res=dh_a)
    grad_x, _, gw['g_mix'] = norm_bwd('norm_mix_bwd', xs, vec('g_mix') + t_in, dh1, res=dx1)
    l_ssem, l_rsem, l_buf, l_zone, _ = spread_start('spread_start_late',
                                                    _pack([gw[n] for n in late] + [loss_blk[0:1, 0:1]]))

    grads, delta_w, new_m, new_v = {}, {}, {}, {}
    done = grad_x
    for n, part, zone, ssem, rsem in pending:
        part, landed = scatter_wait('scatter_wait_' + n, part, zone, ssem, rsem, done)
        mine = sum_parts('sum_parts_' + n, part, landed, chip_idx, c_idx)
        (other,) = pair_join('pair_join_' + n, [mine])
        if n == 'w_in':
            mine, other = wins.from_window(mine, chip), wins.from_window(other, chip)
        grads[n], delta_w[n], new_m[n], new_v[n] = adamw_halves('adamw_' + n, W[n], mine, other, M1[n], V1[n], c_idx)
        done = delta_w[n]

    device = 4 * lax.axis_index('x') + 2 * lax.axis_index('y') + lax.axis_index('c')
    summed = {}
    for tag, names, buf, zone, ssem, rsem in (('early', early, e_buf, e_zone, e_ssem, e_rsem),
                                              ('late', late + ['loss'], l_buf, l_zone, l_ssem, l_rsem)):
        buf, zone = spread_wait('spread_wait_' + tag, buf, zone, ssem, rsem, done)
        total = sum_devices('sum_small_' + tag, lax.dynamic_update_index_in_dim(zone, buf, device, 0))
        summed.update(zip(names, _unpack(total, [gw[n].shape if n != 'loss' else (1, 1) for n in names])))
    loss = summed['loss'].reshape(())
    for n in SMALL:
        g = summed[n]
        grads[n] = g.reshape(W[n].shape) if n != 'conv_w' else lax.dynamic_slice_in_dim(
            g, chip * (LW // N_CHIPS), LW // N_CHIPS, axis=1)
    packs = [_pack([d[n] for n in SMALL]) for d in (W, grads, M1, V1)]
    shapes = [W[n].shape for n in SMALL]
    for d, res in zip((delta_w, new_m, new_v), adamw('adamw_small', *packs)):
        d.update(zip(SMALL, _unpack(res, shapes)))

    lead = lambda d: [d[n][None] for n in WEIGHTS]
    return (loss, grad_x[None], *lead(grads), *lead(delta_w), *lead(new_m), *lead(new_v))
```

```python
import functools
import math

import jax
import jax.numpy as jnp
from jax import lax
from jax.experimental import pallas as pl
from jax.experimental.pallas import tpu as pltpu

F32 = jnp.float32
BF16 = jnp.bfloat16
HEAD_DIM = 128
LANES = 128
LRU_C = 8.0
RMS_EPS = 1e-6
CONV_W = 4
ADAM_LR = 0.001
ADAM_B1 = 0.9
ADAM_B2 = 0.999
ADAM_EPS = 1e-08
ADAM_WD = 0.01
ADAM_STEP = 10
VMEM_LIMIT = 56 * 1024 * 1024
N_CHIPS = 4
MESH = pl.DeviceIdType.MESH
ANY = pl.BlockSpec(memory_space=pl.ANY)

WEIGHTS = ['g_mix', 'w_in', 'b_f', 'g_q', 'g_k', 'conv_w', 'conv_b', 'w_ra', 'b_ra', 'w_ri', 'b_ri', 'lam',
           'g_fox_out', 'g_lru_out', 'w_out', 'g_xattn', 'g_mem', 'w_cq', 'w_ckv', 'g_cq', 'g_ck', 'w_co', 'g_ffn',
           'w_gate_up', 'w_down']
BIG = ['w_in', 'w_out', 'w_cq', 'w_ckv', 'w_co', 'w_gate_up', 'w_down']
SMALL = [n for n in WEIGHTS if n not in BIG]


def _params(sem=None):
    if sem is None:
        return pltpu.CompilerParams(vmem_limit_bytes=VMEM_LIMIT)
    return pltpu.CompilerParams(dimension_semantics=sem, vmem_limit_bytes=VMEM_LIMIT)


def _tile(n, cands):
    for t in cands:
        if n % t == 0:
            return t
    return n


ROW_BLOCK_BYTES = 1 << 20


def _row_tile(n_rows, n_cols, min_rows=8):
    cands = [t for t in (512, 256, 128, 64, 32, 16, 8) if t >= min_rows and t * n_cols * 4 <= ROW_BLOCK_BYTES]
    return _tile(n_rows, cands or [min_rows])


def _sigmoid(z):
    return 1.0 / (1.0 + jnp.exp(-z))


def _softplus(z):
    return jnp.maximum(z, 0.0) + jnp.log(1.0 + jnp.exp(-jnp.abs(z)))


def _neg_expm1(z):
    series = -z * (1.0 + z * (0.5 + z * (1.0 / 6.0 + z * (1.0 / 24.0 + z * (1.0 / 120.0)))))
    return jnp.where(z > -0.25, series, 1.0 - jnp.exp(z))


_GELU_K = math.sqrt(2.0 / math.pi)


def _gelu_and_grad(z):
    inner = _GELU_K * (z + 0.044715 * z * z * z)
    t = jnp.tanh(inner)
    g = 0.5 * z * (1.0 + t)
    dg = 0.5 * (1.0 + t) + 0.5 * z * (1.0 - t * t) * _GELU_K * (1.0 + 3.0 * 0.044715 * z * z)
    return g, dg


def _rms(xv, g):
    r = lax.rsqrt(jnp.mean(xv * xv, axis=-1, keepdims=True) + RMS_EPS)
    return xv * r * g


def _rms_bwd(xv, g, dy):
    r = lax.rsqrt(jnp.mean(xv * xv, axis=-1, keepdims=True) + RMS_EPS)
    xh = xv * r
    dyg = dy * g
    dx = r * (dyg - xh * jnp.mean(dyg * xh, axis=-1, keepdims=True))
    return dx, jnp.sum(dy * xh, axis=0, keepdims=True)


def _heads(fn, n_heads, *arrs):
    outs = [fn(*[a[:, h * HEAD_DIM:(h + 1) * HEAD_DIM] for a in arrs]) for h in range(n_heads)]
    first = jnp.concatenate([o[0] for o in outs], axis=1) if n_heads > 1 else outs[0][0]
    rest = [functools.reduce(lambda p, q: p + q, [o[i] for o in outs]) for i in range(1, len(outs[0]))]
    return (first, *rest)


def _split3(v):
    hi = v.astype(BF16)
    r1 = v - hi.astype(F32)
    mid = r1.astype(BF16)
    lo = (r1 - mid.astype(F32)).astype(BF16)
    return hi, mid, lo


def _acc_out(ref, first, val):
    @pl.when(first)
    def _():
        ref[...] = val

    @pl.when(jnp.logical_not(first))
    def _():
        ref[...] += val


_DIMS = {'nn': (((1,), (0,)), ((), ())), 'nt': (((1,), (1,)), ((), ())), 'tn': (((0,), (0,)), ((), ()))}


MM_VMEM_BYTES = 36 * 1024 * 1024


MXU_FLOPS = 800e12
HBM_BYTES_S = 3.2e12
VMEM_ADD_BYTES_S = 8e12
STEP_S = 0.35e-6


def _k_tile(K, tm, tn, a, b, o_dtype, res):
    fixed = tm * tn * (2 * jnp.dtype(o_dtype).itemsize + 4 + (8 if res is not None else 0))
    per_k = 2 * (tm * a.dtype.itemsize + tn * b.dtype.itemsize)
    per_k += 2 * tm * (a.dtype.itemsize > 2) + 2 * tn * (b.dtype.itemsize > 2)
    units = K // LANES
    for d in sorted((d for d in range(1, units + 1) if units % d == 0), reverse=True):
        if fixed + d * LANES * per_k <= MM_VMEM_BYTES:
            return d * LANES
    return None


def _mm_tiles(M, N, K, k_span, a, b, o_dtype, res, tn_cands=(2048, 1024, 512, 256, 128)):
    best = None
    for tm in (2048, 1024, 512, 256, 128):
        for tn in tn_cands:
            if M % tm or N % tn:
                continue
            tk = _k_tile(k_span, tm, tn, a, b, o_dtype, res)
            if tk is None:
                continue
            nk = K // tk
            traffic = (M * K * a.dtype.itemsize * (N // tn) + K * N * b.dtype.itemsize * (M // tm)
                       + M * N * (jnp.dtype(o_dtype).itemsize + (4 if res is not None else 0)))
            work = 2.0 * M * N * K / MXU_FLOPS + (M * N * 4 * nk / VMEM_ADD_BYTES_S if nk > 1 else 0.0)
            t = max(work, traffic / HBM_BYTES_S) + (M // tm) * (N // tn) * nk * STEP_S
            if best is None or t < best[0]:
                best = (t, tm, tn, tk)
    assert best is not None, (M, N, K)
    return best[1:]


def _mm_call(name, a, b, mode, grid, a_spec, b_spec, o_spec, o_shape, o_dtype, acc_shape, res=None):
    nk = grid[2]
    dn = _DIMS[mode]

    def body(*refs):
        a_ref, b_ref = refs[:2]
        r_ref = refs[2] if res is not None else None
        o_ref = refs[3] if res is not None else refs[2]
        part = lax.dot_general(a_ref[...].astype(BF16), b_ref[...].astype(BF16), dn, preferred_element_type=F32)

        def finish(r):
            if r_ref is not None:
                r = r + r_ref[...]
            o_ref[...] = r.astype(o_dtype)

        if nk == 1:
            finish(part)
            return
        acc = refs[-1]
        k = pl.program_id(2)

        @pl.when(k == 0)
        def _():
            acc[...] = part

        @pl.when(k > 0)
        def _():
            acc[...] += part

        @pl.when(k == nk - 1)
        def _():
            finish(acc[...])

    ins = [a, b] + ([] if res is None else [res])
    specs = [a_spec, b_spec] + ([] if res is None else [o_spec])
    return pl.pallas_call(
        body, name=name, grid=grid, in_specs=specs, out_specs=o_spec,
        out_shape=jax.ShapeDtypeStruct(o_shape, o_dtype),
        scratch_shapes=[] if nk == 1 else [pltpu.VMEM(acc_shape, F32)],
        compiler_params=_params(('parallel', 'parallel', 'arbitrary')))(*ins)


def _mm(name, a, b, mode, o_dtype, res=None):
    if mode == 'tn':
        K, M = a.shape
    else:
        M, K = a.shape
    N = b.shape[0] if mode == 'nt' else b.shape[1]
    tm, tn, tk = _mm_tiles(M, N, K, K, a, b, o_dtype, res)
    a_spec = (pl.BlockSpec((tk, tm), lambda m, n, k: (k, m)) if mode == 'tn'
              else pl.BlockSpec((tm, tk), lambda m, n, k: (m, k)))
    b_spec = (pl.BlockSpec((tn, tk), lambda m, n, k: (n, k)) if mode == 'nt'
              else pl.BlockSpec((tk, tn), lambda m, n, k: (k, n)))
    o_spec = pl.BlockSpec((tm, tn), lambda m, n, k: (m, n))
    return _mm_call(name, a, b, mode, (M // tm, N // tn, K // tk), a_spec, b_spec, o_spec, (M, N), o_dtype,
                    (tm, tn), res)


def _mm_colsharded(name, a, w, o_dtype, res=None):
    M, K = a.shape
    J, _, Nj = w.shape
    tm, tn, tk = _mm_tiles(M, J * Nj, K, K, a, w, o_dtype, res,
                           tn_cands=[t for t in (2816, 1408, 1024, 512, 256, 128) if Nj % t == 0])
    per = Nj // tn
    return _mm_call(name, a, w, 'nn', (M // tm, J * per, K // tk),
                    pl.BlockSpec((tm, tk), lambda m, n, k: (m, k)),
                    pl.BlockSpec((None, tk, tn), lambda m, n, k: (n // per, k, n % per)),
                    pl.BlockSpec((tm, tn), lambda m, n, k: (m, n)), (M, J * Nj), o_dtype, (tm, tn), res)


def _planes_spec(arr, rows, cols, row_of, col_of):
    if arr.ndim == 2:
        return pl.BlockSpec((rows, cols), lambda m, n, k: (row_of(m, n, k), col_of(m, n, k)))
    per_plane = arr.shape[2] // cols
    return pl.BlockSpec((None, rows, cols),
                        lambda m, n, k: (col_of(m, n, k) // per_plane, row_of(m, n, k), col_of(m, n, k) % per_plane))


def _mm_colsharded_t(name, a, w, o_dtype):
    M = a.shape[-2]
    J, K, Nj = w.shape
    tm, tn, tk = _mm_tiles(M, K, J * Nj, Nj, a, w, o_dtype, None)
    per = Nj // tk
    return _mm_call(name, a, w, 'nt', (M // tm, K // tn, J * per),
                    _planes_spec(a, tm, tk, lambda m, n, k: m, lambda m, n, k: k),
                    pl.BlockSpec((None, tn, tk), lambda m, n, k: (k // per, n, k % per)),
                    pl.BlockSpec((tm, tn), lambda m, n, k: (m, n)), (M, K), o_dtype, (tm, tn))


def _mm_grad_colsharded(name, a, dy, J, o_dtype):
    S, M = a.shape
    Nj = dy.shape[-1] * (dy.shape[0] if dy.ndim == 3 else 1) // J
    tm, tn, tk = _mm_tiles(M, J * Nj, S, S, a, dy, o_dtype, None,
                           tn_cands=[t for t in (2816, 1408, 1024, 512, 256, 128) if Nj % t == 0])
    per = Nj // tn
    return _mm_call(name, a, dy, 'tn', (M // tm, J * per, S // tk),
                    pl.BlockSpec((tk, tm), lambda m, n, k: (k, m)),
                    _planes_spec(dy, tk, tn, lambda m, n, k: k, lambda m, n, k: n),
                    pl.BlockSpec((None, tm, tn), lambda m, n, k: (n // per, m, n % per)), (J, M, Nj), o_dtype, (tm, tn))


def _rows_call(name, body, n_rows, tr, ins, outs):
    return pl.pallas_call(
        body, name=name, grid=(n_rows // tr,), in_specs=[s for _, s in ins], out_specs=[s for _, _, s in outs],
        out_shape=[jax.ShapeDtypeStruct(sh, dt) for sh, dt, _ in outs],
        compiler_params=_params(('arbitrary',)))(*[a for a, _ in ins])


def _rb(tr, w, cb=0):
    return pl.BlockSpec((tr, w), lambda i: (i, cb))


def _fb(shape):
    nd = len(shape)
    return pl.BlockSpec(shape, lambda i: (0,) * nd)


def norm_fwd(name, xv, g):
    S, D = xv.shape
    tr = _tile(S, (256, 128))

    def body(x_ref, g_ref, o_ref):
        o_ref[...] = _rms(x_ref[...], g_ref[...]).astype(BF16)

    return _rows_call(name, body, S, tr, [(xv, _rb(tr, D)), (g, _fb((1, D)))], [((S, D), BF16, _rb(tr, D))])[0]


def norm_bwd(name, xv, g, dy, res=None, want_dx=True):
    S, D = xv.shape
    tr = _tile(S, (256, 128))

    def body(*refs):
        if res is None:
            x_ref, g_ref, dy_ref = refs[:3]
            outs = refs[3:]
            r_ref = None
        else:
            x_ref, g_ref, dy_ref, r_ref = refs[:4]
            outs = refs[4:]
        dx, dg = _rms_bwd(x_ref[...], g_ref[...], dy_ref[...])
        if r_ref is not None:
            dx = dx + r_ref[...]
        if want_dx:
            outs[0][...] = dx
            outs[1][...] = dx.astype(BF16)
        _acc_out(outs[-1], pl.program_id(0) == 0, dg)

    ins = [(xv, _rb(tr, D)), (g, _fb((1, D))), (dy, _rb(tr, D))] + ([] if res is None else [(res, _rb(tr, D))])
    outs = ([((S, D), F32, _rb(tr, D)), ((S, D), BF16, _rb(tr, D))] if want_dx else []) + [((1, D), F32, _fb((1, D)))]
    return _rows_call(name, body, S, tr, ins, outs)


def qkv_fwd(proj, g_q, g_k, FW):
    S = proj.shape[0]
    H = FW // HEAD_DIM
    tr = _tile(S, (256, 128))

    def body(q_ref, k_ref, v_ref, gq_ref, gk_ref, qo, ko, vo):
        qo[...] = _heads(lambda t: (_rms(t, gq_ref[...]),), H, q_ref[...])[0].astype(BF16)
        ko[...] = _heads(lambda t: (_rms(t, gk_ref[...]),), H, k_ref[...])[0].astype(BF16)
        vo[...] = v_ref[...].astype(BF16)

    o = ((S, FW), BF16, _rb(tr, FW))
    return _rows_call('qkv_fwd', body, S, tr,
                      [(proj, _rb(tr, FW, 0)), (proj, _rb(tr, FW, 1)), (proj, _rb(tr, FW, 2)),
                       (g_q, _fb((1, HEAD_DIM))), (g_k, _fb((1, HEAD_DIM)))], [o, o, o])


def qkv_bwd(proj, g_q, g_k, dqn, dkn, FW):
    S = proj.shape[0]
    H = FW // HEAD_DIM
    tr = _tile(S, (256, 128))

    def body(q_ref, k_ref, gq_ref, gk_ref, dq_ref, dk_ref, dqo, dko, dgq, dgk):
        dq, gq = _heads(lambda t, d: _rms_bwd(t, gq_ref[...], d), H, q_ref[...], dq_ref[...])
        dk, gk = _heads(lambda t, d: _rms_bwd(t, gk_ref[...], d), H, k_ref[...], dk_ref[...])
        dqo[...] = dq.astype(BF16)
        dko[...] = dk.astype(BF16)
        first = pl.program_id(0) == 0
        _acc_out(dgq, first, gq)
        _acc_out(dgk, first, gk)

    o = ((S, FW), BF16, _rb(tr, FW))
    og = ((1, HEAD_DIM), F32, _fb((1, HEAD_DIM)))
    return _rows_call('qkv_bwd', body, S, tr,
                      [(proj, _rb(tr, FW, 0)), (proj, _rb(tr, FW, 1)), (g_q, _fb((1, HEAD_DIM))),
                       (g_k, _fb((1, HEAD_DIM))), (dqn, _rb(tr, FW)), (dkn, _rb(tr, FW))], [o, o, og, og])


def _tri(n, upper):
    r = lax.broadcasted_iota(jnp.int32, (n, n), 0)
    c = lax.broadcasted_iota(jnp.int32, (n, n), 1)
    return jnp.where((c >= r) if upper else (c <= r), 1.0, 0.0).astype(BF16)


def _blocked_cumsum(val, S, blk, reverse):
    tri = _tri(blk, reverse)
    order = range(S // blk - 1, -1, -1) if reverse else range(S // blk)
    carry = jnp.zeros((1, LANES), F32)
    outs = {}
    for bi in order:
        part = val[bi * blk:(bi + 1) * blk]
        acc = carry
        for piece in _split3(part):
            acc = acc + jnp.dot(tri, piece, preferred_element_type=F32)
        outs[bi] = acc
        carry = carry + jnp.sum(part, axis=0, keepdims=True)
    return jnp.concatenate([outs[bi] for bi in range(S // blk)], axis=0)


def fgate_fwd(f_raw, b_f_pad):
    S = f_raw.shape[0]
    blk = _tile(S, (256, 128))

    def body(f_ref, b_ref, c_ref):
        z = f_ref[...] + b_ref[...]
        c_ref[...] = _blocked_cumsum(-_softplus(-z), S, blk, False)

    return pl.pallas_call(body, name='fgate_fwd', grid=(1,), in_specs=[_fb((S, LANES)), _fb((1, LANES))],
                          out_specs=_fb((S, LANES)), out_shape=jax.ShapeDtypeStruct((S, LANES), F32),
                          compiler_params=_params(('arbitrary',)))(f_raw, b_f_pad)


def fgate_bwd(f_raw, b_f_pad, dc, H):
    S = f_raw.shape[0]
    blk = _tile(S, (256, 128))

    def body(f_ref, b_ref, dc_ref, df_ref, db_ref):
        z = f_ref[...] + b_ref[...]
        dlogf = _blocked_cumsum(dc_ref[...], S, blk, True)
        lane = lax.broadcasted_iota(jnp.int32, (S, LANES), 1)
        df = jnp.where(lane < H, dlogf * _sigmoid(-z), 0.0)
        df_ref[...] = df.astype(BF16)
        db_ref[...] = jnp.sum(df, axis=0, keepdims=True)

    return pl.pallas_call(body, name='fgate_bwd', grid=(1,),
                          in_specs=[_fb((S, LANES)), _fb((1, LANES)), _fb((S, LANES))],
                          out_specs=[_fb((S, LANES)), _fb((1, LANES))],
                          out_shape=[jax.ShapeDtypeStruct((S, LANES), BF16), jax.ShapeDtypeStruct((1, LANES), F32)],
                          compiler_params=_params(('arbitrary',)))(f_raw, b_f_pad, dc)


def _fox_logits(q, k, c_blk, ct_blk, h, T, diagonal):
    s = lax.dot_general(q, k, _DIMS['nt'], preferred_element_type=F32) * (1.0 / math.sqrt(HEAD_DIM))
    lane = lax.broadcasted_iota(jnp.int32, c_blk.shape, 1)
    cq = jnp.sum(jnp.where(lane == h, c_blk, 0.0), axis=1, keepdims=True)
    sub = lax.broadcasted_iota(jnp.int32, ct_blk.shape, 0)
    ck = jnp.sum(jnp.where(sub == h, ct_blk, 0.0), axis=0, keepdims=True)
    s = s + cq - ck
    if not diagonal:
        return s
    rows = lax.broadcasted_iota(jnp.int32, (T, T), 0)
    cols = lax.broadcasted_iota(jnp.int32, (T, T), 1)
    return jnp.where(cols <= rows, s, -jnp.inf)


def _below_and_on_diagonal(q_blk, k_blk, step):
    @pl.when(k_blk < q_blk)
    def _():
        step(False)

    @pl.when(k_blk == q_blk)
    def _():
        step(True)


def fox_fwd(qn, kn, vb, c, ct, T):
    S, FW = qn.shape
    H = FW // HEAD_DIM
    Hp = ct.shape[0]
    n = S // T

    HB = _tile(H, (8, 4, 2, 1))
    W2 = HB * HEAD_DIM

    def body(q_ref, k_ref, v_ref, c_ref, ct_ref, o_ref, lse_ref, m_s, l_s, acc_s):
        hb, i, j = pl.program_id(0), pl.program_id(1), pl.program_id(2)

        @pl.when(j == 0)
        def _():
            m_s[...] = jnp.full_like(m_s, -jnp.inf)
            l_s[...] = jnp.zeros_like(l_s)
            acc_s[...] = jnp.zeros_like(acc_s)

        def step(diagonal):
            for hh in range(HB):
                sl = slice(hh * HEAD_DIM, (hh + 1) * HEAD_DIM)
                s = _fox_logits(q_ref[:, sl], k_ref[:, sl], c_ref[...], ct_ref[...], hb * HB + hh, T, diagonal)
                m_old = m_s[hh]
                m_new = jnp.maximum(m_old, jnp.max(s, axis=1, keepdims=True))
                alpha = jnp.exp(m_old - m_new)
                p = jnp.exp(s - m_new)
                l_s[hh] = alpha * l_s[hh] + jnp.sum(p, axis=1, keepdims=True)
                acc_s[hh] = alpha * acc_s[hh] + jnp.dot(p.astype(BF16), v_ref[:, sl], preferred_element_type=F32)
                m_s[hh] = m_new

        _below_and_on_diagonal(i, j, step)

        @pl.when(j == i)
        def _():
            for hh in range(HB):
                o_ref[:, hh * HEAD_DIM:(hh + 1) * HEAD_DIM] = acc_s[hh] / l_s[hh]
                lse_ref[hh] = jnp.broadcast_to(m_s[hh] + jnp.log(l_s[hh]), (T, LANES))

    qs = pl.BlockSpec((T, W2), lambda h, i, j: (i, h))
    ks = pl.BlockSpec((T, W2), lambda h, i, j: (jnp.minimum(j, i), h))
    return pl.pallas_call(
        body, name='fox_fwd', grid=(H // HB, n, n),
        in_specs=[qs, ks, ks, pl.BlockSpec((T, LANES), lambda h, i, j: (i, 0)),
                  pl.BlockSpec((Hp, T), lambda h, i, j: (0, jnp.minimum(j, i)))],
        out_specs=[qs, pl.BlockSpec((HB, T, LANES), lambda h, i, j: (h, i, 0))],
        out_shape=[jax.ShapeDtypeStruct((S, FW), F32), jax.ShapeDtypeStruct((H, S, LANES), F32)],
        scratch_shapes=[pltpu.VMEM((HB, T, 1), F32), pltpu.VMEM((HB, T, 1), F32), pltpu.VMEM((HB, T, HEAD_DIM), F32)],
        compiler_params=_params(('parallel', 'parallel', 'arbitrary')))(qn, kn, vb, c, ct)


def _fox_p_ds(q_ref, k_ref, v_ref, do_ref, c_ref, ct_ref, lse_ref, dl_ref, h, T, diagonal):
    s = _fox_logits(q_ref[...], k_ref[...], c_ref[...], ct_ref[...], h, T, diagonal)
    p = jnp.exp(s - jnp.tile(lse_ref[...], (1, T // LANES)))
    dp = lax.dot_general(do_ref[...], v_ref[...], _DIMS['nt'], preferred_element_type=F32)
    ds = p * (dp - jnp.tile(dl_ref[...], (1, T // LANES)))
    return p, dp, ds


def fox_bwd_q(qn, kn, vb, do, c, ct, lse, dl, T):
    S, FW = qn.shape
    H = FW // HEAD_DIM
    Hp = ct.shape[0]
    n = S // T
    HB = _tile(H, (8, 4, 2, 1))
    W2 = HB * HEAD_DIM

    def body(q_ref, k_ref, v_ref, do_ref, c_ref, ct_ref, lse_ref, dl_ref, dq_ref, dl2_ref, acc_s, rs_s):
        hb, i, j = pl.program_id(0), pl.program_id(1), pl.program_id(2)

        @pl.when(j == 0)
        def _():
            acc_s[...] = jnp.zeros_like(acc_s)
            rs_s[...] = jnp.zeros_like(rs_s)

        def step(diagonal):
            for hh in range(HB):
                sl = slice(hh * HEAD_DIM, (hh + 1) * HEAD_DIM)
                p, dp, ds = _fox_p_ds(q_ref.at[:, sl], k_ref.at[:, sl], v_ref.at[:, sl], do_ref.at[:, sl], c_ref, ct_ref,
                                      lse_ref.at[hh], dl_ref.at[hh], hb * HB + hh, T, diagonal)
                acc_s[hh] += jnp.dot(ds.astype(BF16), k_ref[:, sl], preferred_element_type=F32)
                rs_s[hh] += jnp.sum(p * dp, axis=1, keepdims=True)

        _below_and_on_diagonal(i, j, step)

        @pl.when(j == i)
        def _():
            for hh in range(HB):
                dq_ref[:, hh * HEAD_DIM:(hh + 1) * HEAD_DIM] = acc_s[hh] * (1.0 / math.sqrt(HEAD_DIM))
                dl2_ref[hh] = jnp.broadcast_to(rs_s[hh], (T, LANES))

    qs = pl.BlockSpec((T, W2), lambda h, i, j: (i, h))
    ks = pl.BlockSpec((T, W2), lambda h, i, j: (jnp.minimum(j, i), h))
    st = pl.BlockSpec((HB, T, LANES), lambda h, i, j: (h, i, 0))
    return pl.pallas_call(
        body, name='fox_bwd_q', grid=(H // HB, n, n),
        in_specs=[qs, ks, ks, qs, pl.BlockSpec((T, LANES), lambda h, i, j: (i, 0)),
                  pl.BlockSpec((Hp, T), lambda h, i, j: (0, jnp.minimum(j, i))), st, st],
        out_specs=[qs, st], out_shape=[jax.ShapeDtypeStruct((S, FW), F32), jax.ShapeDtypeStruct((H, S, LANES), F32)],
        scratch_shapes=[pltpu.VMEM((HB, T, HEAD_DIM), F32), pltpu.VMEM((HB, T, 1), F32)],
        compiler_params=_params(('parallel', 'parallel', 'arbitrary')))(qn, kn, vb, do, c, ct, lse, dl)


def fox_bwd_kv(qn, kn, vb, do, c, ct, lse, dl, T):
    S, FW = qn.shape
    H = FW // HEAD_DIM
    Hp = ct.shape[0]
    n = S // T

    HB = _tile(H, (8, 4, 2, 1))
    W2 = HB * HEAD_DIM

    def body(q_ref, k_ref, v_ref, do_ref, c_ref, ct_ref, lse_ref, dl_ref, dk_ref, dv_ref, dc_ref, dk_s, dv_s, dc_s):
        hb, j, i = pl.program_id(0), pl.program_id(1), pl.program_id(2)

        @pl.when(i == 0)
        def _():
            dk_s[...] = jnp.zeros_like(dk_s)
            dv_s[...] = jnp.zeros_like(dv_s)
            dc_s[...] = jnp.zeros_like(dc_s)

        def step(diagonal):
            for hh in range(HB):
                sl = slice(hh * HEAD_DIM, (hh + 1) * HEAD_DIM)
                p, _, ds = _fox_p_ds(q_ref.at[:, sl], k_ref.at[:, sl], v_ref.at[:, sl], do_ref.at[:, sl], c_ref, ct_ref,
                                     lse_ref.at[hh], dl_ref.at[hh], hb * HB + hh, T, diagonal)
                dv_s[hh] += lax.dot_general(p.astype(BF16), do_ref[:, sl], _DIMS['tn'], preferred_element_type=F32)
                dk_s[hh] += lax.dot_general(ds.astype(BF16), q_ref[:, sl], _DIMS['tn'], preferred_element_type=F32)
                dc_s[hh] += jnp.sum(ds, axis=0, keepdims=True)

        _below_and_on_diagonal(i, j, step)

        @pl.when(i == n - 1)
        def _():
            for hh in range(HB):
                sl = slice(hh * HEAD_DIM, (hh + 1) * HEAD_DIM)
                dk_ref[:, sl] = dk_s[hh] * (1.0 / math.sqrt(HEAD_DIM))
                dv_ref[:, sl] = dv_s[hh].astype(BF16)
                dc_ref[hh] = -dc_s[hh]

    qs = pl.BlockSpec((T, W2), lambda h, j, i: (jnp.maximum(i, j), h))
    ks = pl.BlockSpec((T, W2), lambda h, j, i: (j, h))
    st = pl.BlockSpec((HB, T, LANES), lambda h, j, i: (h, jnp.maximum(i, j), 0))
    return pl.pallas_call(
        body, name='fox_bwd_kv', grid=(H // HB, n, n),
        in_specs=[qs, ks, ks, qs, pl.BlockSpec((T, LANES), lambda h, j, i: (jnp.maximum(i, j), 0)),
                  pl.BlockSpec((Hp, T), lambda h, j, i: (0, j)), st, st],
        out_specs=[ks, ks, pl.BlockSpec((HB, 1, T), lambda h, j, i: (h, 0, j))],
        out_shape=[jax.ShapeDtypeStruct((S, FW), F32), jax.ShapeDtypeStruct((S, FW), BF16),
                   jax.ShapeDtypeStruct((H, 1, S), F32)],
        scratch_shapes=[pltpu.VMEM((HB, T, HEAD_DIM), F32), pltpu.VMEM((HB, T, HEAD_DIM), F32),
                        pltpu.VMEM((HB, 1, T), F32)],
        compiler_params=_params(('parallel', 'parallel', 'arbitrary')))(qn, kn, vb, do, c, ct, lse, dl)


def _shift_down(v, d, rows, fill):
    return jnp.where(rows >= d, pltpu.roll(v, d, 0), fill)


def _shift_up(v, d, rows, S, fill):
    return jnp.where(rows < S - d, pltpu.roll(v, S - d, 0), fill)


SUBLANES = 8


def _scan_by_doubling(a, b, pos, span, reverse):
    n = a.shape[0]
    d = 1
    while d < span:
        if reverse:
            keep = pos < span - d
            a_s, b_s = jnp.where(keep, pltpu.roll(a, n - d, 0), 1.0), jnp.where(keep, pltpu.roll(b, n - d, 0), 0.0)
        else:
            keep = pos >= d
            a_s, b_s = jnp.where(keep, pltpu.roll(a, d, 0), 1.0), jnp.where(keep, pltpu.roll(b, d, 0), 0.0)
        b = a * b_s + b
        a = a * a_s
        d *= 2
    return a, b


def _scan(a, b, rows, S, reverse, scr):
    groups = S // SUBLANES
    a, b = _scan_by_doubling(a, b, jnp.bitwise_and(rows, SUBLANES - 1), SUBLANES, reverse)
    scr[0][...] = a
    scr[1][...] = b
    edge = 0 if reverse else SUBLANES - 1
    a_g = scr[0][pl.ds(edge, groups, stride=SUBLANES), :]
    b_g = scr[1][pl.ds(edge, groups, stride=SUBLANES), :]
    g_pos = lax.broadcasted_iota(jnp.int32, (groups, LANES), 0)
    _, h_g = _scan_by_doubling(a_g, b_g, g_pos, groups, reverse)
    if reverse:
        carry = jnp.where(g_pos < groups - 1, pltpu.roll(h_g, groups - 1, 0), 0.0)
    else:
        carry = jnp.where(g_pos >= 1, pltpu.roll(h_g, 1, 0), 0.0)
    for r in range(SUBLANES):
        scr[0][pl.ds(r, groups, stride=SUBLANES), :] = carry
    return b + a * scr[0][...]


def _lru_forward(u, cw, cb, wra, bra, wri, bri, lam, rows, scr):
    uc = cb + cw[CONV_W - 1] * u
    for d in range(1, CONV_W):
        uc = uc + cw[CONV_W - 1 - d] * _shift_down(u, d, rows, 0.0)
    ucb = uc.astype(BF16)
    r = _sigmoid(jnp.dot(ucb, wra.astype(BF16), preferred_element_type=F32) + bra)
    ig = _sigmoid(jnp.dot(ucb, wri.astype(BF16), preferred_element_type=F32) + bri)
    sp = _softplus(-lam)
    log_a = -LRU_C * r * sp
    a = jnp.exp(log_a)
    sq = jnp.sqrt(_neg_expm1(2.0 * log_a))
    iu = ig * uc
    hseq = _scan(a, sq * iu, rows, u.shape[0], False, scr)
    return uc, ucb, r, ig, sp, a, sq, iu, hseq


def _lru_specs(S, n_u, n_g):
    col = lambda off: pl.BlockSpec((S, LANES), lambda cbk: (0, off + cbk))
    vec = pl.BlockSpec((1, LANES), lambda cbk: (0, cbk))
    mat = pl.BlockSpec((None, LANES, LANES), lambda cbk: (cbk, 0, 0))
    cw = pl.BlockSpec((CONV_W, LANES), lambda cbk: (0, cbk))
    return col, vec, mat, cw


def lru_fwd(proj, conv_w, conv_b, w_ra, b_ra, w_ri, b_ri, lam, u_off, g_off):
    S = proj.shape[0]
    nb = w_ra.shape[0]
    col, vec, mat, cws = _lru_specs(S, u_off, g_off)

    def body(u_ref, g_ref, cw_ref, cb_ref, wra_ref, bra_ref, wri_ref, bri_ref, lam_ref, y_ref, scr0, scr1):
        rows = lax.broadcasted_iota(jnp.int32, (S, LANES), 0)
        cw = [cw_ref[t:t + 1, :] for t in range(CONV_W)]
        hseq = _lru_forward(u_ref[...], cw, cb_ref[...], wra_ref[...], bra_ref[...], wri_ref[...],
                            bri_ref[...], lam_ref[...], rows, (scr0, scr1))[-1]
        y_ref[...] = hseq * _gelu_and_grad(g_ref[...])[0]

    return pl.pallas_call(
        body, name='lru_fwd', grid=(nb,),
        in_specs=[col(u_off), col(g_off), cws, vec, mat, vec, mat, vec, vec], out_specs=col(0),
        out_shape=jax.ShapeDtypeStruct((S, nb * LANES), F32),
        scratch_shapes=[pltpu.VMEM((S, LANES), F32), pltpu.VMEM((S, LANES), F32)],
        compiler_params=_params(('parallel',)))(proj, proj, conv_w, conv_b, w_ra, b_ra, w_ri, b_ri, lam)


def lru_bwd(proj, dy, conv_w, conv_b, w_ra, b_ra, w_ri, b_ri, lam, u_off, g_off):
    S = proj.shape[0]
    nb = w_ra.shape[0]
    LW = nb * LANES
    col, vec, mat, cws = _lru_specs(S, u_off, g_off)

    def body(u_ref, g_ref, dy_ref, cw_ref, cb_ref, wra_ref, bra_ref, wri_ref, bri_ref, lam_ref,
             du_ref, dg_ref, dcw_ref, dcb_ref, dwra_ref, dbra_ref, dwri_ref, dbri_ref, dlam_ref, scr0, scr1):
        rows = lax.broadcasted_iota(jnp.int32, (S, LANES), 0)
        u, lam_v = u_ref[...], lam_ref[...]
        cw = [cw_ref[t:t + 1, :] for t in range(CONV_W)]
        wra, wri = wra_ref[...].astype(BF16), wri_ref[...].astype(BF16)
        uc, ucb, r, ig, sp, a, sq, iu, hseq = _lru_forward(u, cw, cb_ref[...], wra, bra_ref[...], wri, bri_ref[...],
                                                           lam_v, rows, (scr0, scr1))
        gl, dgl = _gelu_and_grad(g_ref[...])
        dy_v = dy_ref[...]
        dg_ref[...] = (dy_v * hseq * dgl).astype(BF16)
        G = _scan(_shift_up(a, 1, rows, S, 0.0), dy_v * gl, rows, S, True, (scr0, scr1))
        da = G * _shift_down(hseq, 1, rows, 0.0)
        diu = G * sq
        dsq = G * iu
        dlog_a = da * a - dsq * a * a / jnp.maximum(sq, 1e-30)
        dr = dlog_a * (-LRU_C * sp)
        dsp = jnp.sum(dlog_a * (-LRU_C * r), axis=0, keepdims=True)
        dlam_ref[...] = -dsp * _sigmoid(-lam_v)
        dzr = dr * r * (1.0 - r)
        dzi = diu * uc * ig * (1.0 - ig)
        dzrb, dzib = dzr.astype(BF16), dzi.astype(BF16)
        duc = (diu * ig + lax.dot_general(dzrb, wra, _DIMS['nt'], preferred_element_type=F32)
               + lax.dot_general(dzib, wri, _DIMS['nt'], preferred_element_type=F32))
        dwra_ref[...] = lax.dot_general(ucb, dzrb, _DIMS['tn'], preferred_element_type=F32)
        dwri_ref[...] = lax.dot_general(ucb, dzib, _DIMS['tn'], preferred_element_type=F32)
        dbra_ref[...] = jnp.sum(dzr, axis=0, keepdims=True)
        dbri_ref[...] = jnp.sum(dzi, axis=0, keepdims=True)
        dcb_ref[...] = jnp.sum(duc, axis=0, keepdims=True)
        du = cw[CONV_W - 1] * duc
        dcw_ref[CONV_W - 1:CONV_W, :] = jnp.sum(duc * u, axis=0, keepdims=True)
        for d in range(1, CONV_W):
            du = du + cw[CONV_W - 1 - d] * _shift_up(duc, d, rows, S, 0.0)
            dcw_ref[CONV_W - 1 - d:CONV_W - d, :] = jnp.sum(duc * _shift_down(u, d, rows, 0.0), axis=0, keepdims=True)
        du_ref[...] = du.astype(BF16)

    sd = jax.ShapeDtypeStruct
    return pl.pallas_call(
        body, name='lru_bwd', grid=(nb,),
        in_specs=[col(u_off), col(g_off), col(0), cws, vec, mat, vec, mat, vec, vec],
        out_specs=[col(0), col(0), cws, vec, mat, vec, mat, vec, vec],
        out_shape=[sd((S, LW), BF16), sd((S, LW), BF16), sd((CONV_W, LW), F32), sd((1, LW), F32),
                   sd((nb, LANES, LANES), F32), sd((1, LW), F32), sd((nb, LANES, LANES), F32), sd((1, LW), F32),
                   sd((1, LW), F32)],
        scratch_shapes=[pltpu.VMEM((S, LANES), F32), pltpu.VMEM((S, LANES), F32)],
        compiler_params=_params(('parallel',)))(proj, proj, dy, conv_w, conv_b, w_ra, b_ra, w_ri, b_ri, lam)


def mix_fwd(o_fox, y_lru, g_fox, g_lru):
    S, FW = o_fox.shape
    tr = _tile(S, (256, 128))

    def body(o_ref, y_ref, gf_ref, gl_ref, m_ref):
        m_ref[...] = jnp.concatenate([_rms(o_ref[...], gf_ref[...]), _rms(y_ref[...], gl_ref[...])],
                                     axis=1).astype(BF16)

    return _rows_call('mix_fwd', body, S, tr,
                      [(o_fox, _rb(tr, FW)), (y_lru, _rb(tr, FW)), (g_fox, _fb((1, FW))), (g_lru, _fb((1, FW)))],
                      [((S, 2 * FW), BF16, _rb(tr, 2 * FW))])[0]


def mix_bwd(o_fox, y_lru, g_fox, g_lru, dmix):
    S, FW = o_fox.shape
    H = FW // HEAD_DIM
    tr = _tile(S, (256, 128))

    def body(o_ref, y_ref, gf_ref, gl_ref, df_ref, dl_ref, do_ref, dlt_ref, dy_ref, dgf_ref, dgl_ref):
        o = o_ref[...]
        do, dgf = _rms_bwd(o, gf_ref[...], df_ref[...])
        dyl, dgl = _rms_bwd(y_ref[...], gl_ref[...], dl_ref[...])
        do_ref[...] = do.astype(BF16)
        dy_ref[...] = dyl
        prod = do * o
        for h in range(H):
            dlt_ref[h] = jnp.broadcast_to(
                jnp.sum(prod[:, h * HEAD_DIM:(h + 1) * HEAD_DIM], axis=1, keepdims=True), (tr, LANES))
        first = pl.program_id(0) == 0
        _acc_out(dgf_ref, first, dgf)
        _acc_out(dgl_ref, first, dgl)

    g = _fb((1, FW))
    return _rows_call('mix_bwd', body, S, tr,
                      [(o_fox, _rb(tr, FW)), (y_lru, _rb(tr, FW)), (g_fox, g), (g_lru, g), (dmix, _rb(tr, FW, 0)),
                       (dmix, _rb(tr, FW, 1))],
                      [((S, FW), BF16, _rb(tr, FW)), ((H, S, LANES), F32, pl.BlockSpec((H, tr, LANES), lambda i: (0, i, 0))),
                       ((S, FW), F32, _rb(tr, FW)), ((1, FW), F32, g), ((1, FW), F32, g)])


def _xattn_heads(cq_raw, ckv, g_cq, g_ck, XW):
    out = []
    for h in range(XW // HEAD_DIM):
        sl = slice(h * HEAD_DIM, (h + 1) * HEAD_DIM)
        out.append((cq_raw[:, sl], _rms(cq_raw[:, sl], g_cq), ckv[:, sl], _rms(ckv[:, sl], g_ck),
                    ckv[:, XW + h * HEAD_DIM:XW + (h + 1) * HEAD_DIM].astype(BF16)))
    return out


def xattn_fwd(cq_raw, ckv, g_cq, g_ck):
    S, XW = cq_raw.shape
    M = ckv.shape[0]
    tr = _tile(S, (512, 256, 128))

    def body(q_ref, kv_ref, gq_ref, gk_ref, o_ref):
        outs = []
        for _, qn, _, kn, v in _xattn_heads(q_ref[...], kv_ref[...], gq_ref[...], gk_ref[...], XW):
            s = lax.dot_general(qn.astype(BF16), kn.astype(BF16), _DIMS['nt'], preferred_element_type=F32)
            s = s / math.sqrt(HEAD_DIM)
            p = jnp.exp(s - jnp.max(s, axis=1, keepdims=True))
            p = p / jnp.sum(p, axis=1, keepdims=True)
            outs.append(jnp.dot(p.astype(BF16), v, preferred_element_type=F32))
        o_ref[...] = jnp.concatenate(outs, axis=1).astype(BF16)

    g = _fb((1, HEAD_DIM))
    return _rows_call('xattn_fwd', body, S, tr,
                      [(cq_raw, _rb(tr, XW)), (ckv, _fb((M, 2 * XW))), (g_cq, g), (g_ck, g)],
                      [((S, XW), BF16, _rb(tr, XW))])[0]


def xattn_bwd(cq_raw, ckv, g_cq, g_ck, do):
    S, XW = cq_raw.shape
    M = ckv.shape[0]
    tr = _tile(S, (512, 256, 128))
    n = S // tr

    def body(q_ref, kv_ref, gq_ref, gk_ref, do_ref, dq_ref, dkv_ref, dgq_ref, dgk_ref):
        i = pl.program_id(0)
        do_v = do_ref[...]
        dqs, dkn, dvs = [], [], []
        dgq = jnp.zeros((1, HEAD_DIM), F32)
        for h, (q_raw, qn, _, kn, v) in enumerate(_xattn_heads(q_ref[...], kv_ref[...], gq_ref[...], gk_ref[...], XW)):
            qb, kb = qn.astype(BF16), kn.astype(BF16)
            doh = do_v[:, h * HEAD_DIM:(h + 1) * HEAD_DIM]
            s = lax.dot_general(qb, kb, _DIMS['nt'], preferred_element_type=F32) / math.sqrt(HEAD_DIM)
            p = jnp.exp(s - jnp.max(s, axis=1, keepdims=True))
            p = p / jnp.sum(p, axis=1, keepdims=True)
            dp = lax.dot_general(doh, v, _DIMS['nt'], preferred_element_type=F32)
            ds = (p * (dp - jnp.sum(p * dp, axis=1, keepdims=True)) / math.sqrt(HEAD_DIM)).astype(BF16)
            dvs.append(lax.dot_general(p.astype(BF16), doh, _DIMS['tn'], preferred_element_type=F32))
            dkn.append(lax.dot_general(ds, qb, _DIMS['tn'], preferred_element_type=F32))
            dq, g1 = _rms_bwd(q_raw, gq_ref[...], jnp.dot(ds, kb, preferred_element_type=F32))
            dqs.append(dq)
            dgq = dgq + g1
        dq_ref[...] = jnp.concatenate(dqs, axis=1).astype(BF16)
        first = i == 0
        _acc_out(dgq_ref, first, dgq)
        _acc_out(dkv_ref, first, jnp.concatenate(dkn + dvs, axis=1))

        @pl.when(i == n - 1)
        def _():
            kv = kv_ref[...]
            acc = dkv_ref[...]
            dk, gk = _heads(lambda t, d: _rms_bwd(t, gk_ref[...], d), XW // HEAD_DIM, kv[:, :XW], acc[:, :XW])
            dkv_ref[:, :XW] = dk
            dgk_ref[...] = gk

    g = _fb((1, HEAD_DIM))
    return _rows_call('xattn_bwd', body, S, tr,
                      [(cq_raw, _rb(tr, XW)), (ckv, _fb((M, 2 * XW))), (g_cq, g), (g_ck, g), (do, _rb(tr, XW))],
                      [((S, XW), BF16, _rb(tr, XW)), ((M, 2 * XW), F32, _fb((M, 2 * XW))), ((1, HEAD_DIM), F32, g),
                       ((1, HEAD_DIM), F32, g)])


def gate_up_fwd(hf, w, F):
    S, D = hf.shape
    J, _, Nj = w.shape
    tm = _tile(S, (1024, 512, 256, 128))
    tn = _tile(Nj, (256, 128))
    per = Nj // tn
    half = J // 2 * per

    def body(a_ref, bg_ref, bu_ref, gu_ref, act_ref):
        a = a_ref[...]
        g = jnp.dot(a, bg_ref[...], preferred_element_type=F32)
        u = jnp.dot(a, bu_ref[...], preferred_element_type=F32)
        gu_ref[0] = g
        gu_ref[1] = u
        act_ref[...] = (g * _sigmoid(g) * u).astype(BF16)

    return pl.pallas_call(
        body, name='proj_gate_up', grid=(S // tm, half),
        in_specs=[pl.BlockSpec((tm, D), lambda m, n: (m, 0)),
                  pl.BlockSpec((None, D, tn), lambda m, n: (n // per, 0, n % per)),
                  pl.BlockSpec((None, D, tn), lambda m, n: ((n + half) // per, 0, n % per))],
        out_specs=[pl.BlockSpec((2, tm, tn), lambda m, n: (0, m, n)), pl.BlockSpec((tm, tn), lambda m, n: (m, n))],
        out_shape=[jax.ShapeDtypeStruct((2, S, F), F32), jax.ShapeDtypeStruct((S, F), BF16)],
        compiler_params=_params(('parallel', 'parallel')))(hf, w, w)


def down_bwd_x(dyb, w_down, gu, after):
    S, D = dyb.shape
    F = w_down.shape[0]
    tm = _tile(S, (1024, 512, 256, 128))
    tn = _tile(F, (512, 256, 128))

    def body(a_ref, b_ref, gu_ref, after_ref, o_ref):
        da = lax.dot_general(a_ref[...], b_ref[...], _DIMS['nt'], preferred_element_type=F32)
        g = gu_ref[0]
        sg = _sigmoid(g)
        o_ref[0] = (da * gu_ref[1] * sg * (1.0 + g * (1.0 - sg))).astype(BF16)
        o_ref[1] = (da * g * sg).astype(BF16)

    planes = pl.BlockSpec((2, tm, tn), lambda m, n: (0, m, n))
    return pl.pallas_call(
        body, name='bwd_down_x', grid=(S // tm, F // tn),
        in_specs=[pl.BlockSpec((tm, D), lambda m, n: (m, 0)), pl.BlockSpec((tn, D), lambda m, n: (n, 0)), planes, ANY],
        out_specs=planes, out_shape=jax.ShapeDtypeStruct((2, S, F), BF16),
        compiler_params=_params(('parallel', 'parallel')))(dyb, w_down, gu, after)


def down_fwd_loss(act, w_down, x2, target):
    S, F = act.shape
    D = w_down.shape[1]
    tm, tn, tk = _mm_tiles(S, D, F, F, act, w_down, F32, x2, tn_cands=(512, 256, 128))
    nk = F // tk

    def body(a_ref, b_ref, x_ref, t_ref, d_ref, db_ref, l_ref, acc):
        m, n, k = pl.program_id(0), pl.program_id(1), pl.program_id(2)
        part = jnp.dot(a_ref[...], b_ref[...], preferred_element_type=F32)

        @pl.when(k == 0)
        def _():
            acc[...] = part

        @pl.when(k > 0)
        def _():
            acc[...] += part

        @pl.when(k == nk - 1)
        def _():
            err = acc[...] + x_ref[...] - t_ref[...]
            d = err * (1.0 / D)
            d_ref[...] = d
            db_ref[...] = d.astype(BF16)
            tot = jnp.sum(jnp.sum(err * err, axis=1, keepdims=True), axis=0, keepdims=True) * (0.5 / D)
            _acc_out(l_ref, jnp.logical_and(m == 0, n == 0), jnp.broadcast_to(tot, (1, LANES)))

    tile = pl.BlockSpec((tm, tn), lambda m, n, k: (m, n))
    return pl.pallas_call(
        body, name='proj_down', grid=(S // tm, D // tn, nk),
        in_specs=[pl.BlockSpec((tm, tk), lambda m, n, k: (m, k)), pl.BlockSpec((tk, tn), lambda m, n, k: (k, n)), tile, tile],
        out_specs=[tile, tile, pl.BlockSpec((1, LANES), lambda m, n, k: (0, 0))],
        out_shape=[jax.ShapeDtypeStruct((S, D), F32), jax.ShapeDtypeStruct((S, D), BF16),
                   jax.ShapeDtypeStruct((1, LANES), F32)],
        scratch_shapes=[pltpu.VMEM((tm, tn), F32)],
        compiler_params=_params(('arbitrary', 'arbitrary', 'arbitrary')))(act, w_down, x2, target)


def swiglu_bwd(gu, dact, F, after):
    S = gu.shape[1]
    tr = _tile(S, (256, 128))
    tf = _tile(F, (1408, 1024, 512, 256, 128))
    nf = F // tf

    def body(gu_ref, da_ref, after_ref, o_ref):
        g, da = gu_ref[0], da_ref[...]
        sg = _sigmoid(g)
        o_ref[0] = (da * gu_ref[1] * sg * (1.0 + g * (1.0 - sg))).astype(BF16)
        o_ref[1] = (da * g * sg).astype(BF16)

    planes = pl.BlockSpec((2, tr, tf), lambda i, n: (0, i, n))
    return pl.pallas_call(
        body, name='swiglu_bwd', grid=(S // tr, nf),
        in_specs=[planes, pl.BlockSpec((tr, tf), lambda i, n: (i, n)), ANY],
        out_specs=planes, out_shape=jax.ShapeDtypeStruct((2, S, F), BF16),
        compiler_params=_params(('parallel', 'parallel')))(gu, dact, after)


def loss_head(y, target):
    S, D = y.shape
    tr = _tile(S, (256, 128))

    def body(y_ref, t_ref, d_ref, db_ref, l_ref):
        err = y_ref[...] - t_ref[...]
        d = err * (1.0 / D)
        d_ref[...] = d
        db_ref[...] = d.astype(BF16)
        part = jnp.sum(jnp.sum(err * err, axis=1, keepdims=True), axis=0, keepdims=True) * (0.5 / D)
        _acc_out(l_ref, pl.program_id(0) == 0, jnp.broadcast_to(part, (1, LANES)))

    return _rows_call('loss_head', body, S, tr, [(y, _rb(tr, D)), (target, _rb(tr, D))],
                      [((S, D), F32, _rb(tr, D)), ((S, D), BF16, _rb(tr, D)), ((1, LANES), F32, _fb((1, LANES)))])


def _adamw_math(w, gv, m, v):
    mn = ADAM_B1 * m + (1.0 - ADAM_B1) * gv
    vn = ADAM_B2 * v + (1.0 - ADAM_B2) * (gv * gv)
    m_hat = mn / (1.0 - ADAM_B1 ** ADAM_STEP)
    v_hat = vn / (1.0 - ADAM_B2 ** ADAM_STEP)
    return -ADAM_LR * (m_hat / (jnp.sqrt(v_hat) + ADAM_EPS) + ADAM_WD * w), mn, vn


def adamw(name, w, g, m, v):
    R, C = w.shape
    tr = _row_tile(R, C)

    def body(w_ref, g_ref, m_ref, v_ref, d_ref, mo_ref, vo_ref):
        d_ref[...], mo_ref[...], vo_ref[...] = _adamw_math(w_ref[...], g_ref[...], m_ref[...], v_ref[...])

    spec = _rb(tr, C)
    return _rows_call(name, body, R, tr, [(w, spec), (g, spec), (m, spec), (v, spec)], [((R, C), F32, spec)] * 3)


def adamw_halves(name, w, mine, other, m, v, c_idx):
    R, C = w.shape
    hr = R // 2
    tr = _row_tile(hr, C)

    def body(c_ref, w_ref, a_ref, b_ref, m_ref, v_ref, g_ref, d_ref, mo_ref, vo_ref):
        gv = jnp.where(pl.program_id(0) == c_ref[0], a_ref[...], b_ref[...])
        g_ref[...] = gv
        d_ref[...], mo_ref[...], vo_ref[...] = _adamw_math(w_ref[...], gv, m_ref[...], v_ref[...])

    full = pl.BlockSpec((None, tr, C), lambda hh, i, c_ref: (hh, i, 0))
    mine_spec = pl.BlockSpec((tr, C), lambda hh, i, c_ref: (jnp.where(hh == c_ref[0], i, 0), 0))
    other_spec = pl.BlockSpec((tr, C), lambda hh, i, c_ref: (jnp.where(hh == c_ref[0], 0, i), 0))
    outs = pl.pallas_call(
        body, name=name,
        grid_spec=pltpu.PrefetchScalarGridSpec(num_scalar_prefetch=1, grid=(2, hr // tr),
                                               in_specs=[full, mine_spec, other_spec, full, full], out_specs=[full] * 4),
        out_shape=[jax.ShapeDtypeStruct((2, hr, C), F32)] * 4,
        compiler_params=_params(('parallel', 'parallel')))(
            c_idx, w.reshape(2, hr, C), mine, other, m.reshape(2, hr, C), v.reshape(2, hr, C))
    return [o.reshape(R, C) for o in outs]


def _place():
    x, y, c = lax.axis_index('x'), lax.axis_index('y'), lax.axis_index('c')
    return x, y, c, [(1 - x, y), (x, 1 - y), (1 - x, 1 - y)]


def _rcopy(src, dst, ssem, rsem, dev):
    return pltpu.make_async_remote_copy(src_ref=src, dst_ref=dst, send_sem=ssem, recv_sem=rsem, device_id=dev,
                                        device_id_type=MESH)


HBM = pl.BlockSpec(memory_space=pltpu.HBM)
SEM = pl.BlockSpec(memory_space=pltpu.SEMAPHORE)
EFFECT = pltpu.SideEffectType.DATAFLOW_SIDE_EFFECTING


def _in_hbm(a):
    return pltpu.with_memory_space_constraint(a, pltpu.HBM)


def _rows_part(shape, whole, half):
    return pl.ds(0, shape[0]) if whole else pl.ds(half * (shape[0] // 2), shape[0] // 2)


def gather_start(name, shards, whole):
    nT = len(shards)

    def body(*refs):
        srcs, lands = refs[:nT], refs[nT:2 * nT]
        ssem, rsem, token = refs[2 * nT], refs[2 * nT + 1], refs[-1]
        x, y, c, chips = _place()
        for t in range(nT):
            rows = _rows_part(shards[t].shape, whole[t], c)
            for k, (px, py) in enumerate(chips):
                _rcopy(srcs[t].at[rows], lands[t].at[2 * x + y, rows], ssem.at[3 * t + k], rsem.at[3 * t + k],
                       (px, py, c)).start()
        token[...] = jnp.zeros_like(token)

    zones = [lax.empty((N_CHIPS,) + s.shape, s.dtype) for s in shards]
    outs = pl.pallas_call(
        body, name=name,
        out_shape=(pltpu.SemaphoreType.DMA((3 * nT,)), pltpu.SemaphoreType.DMA((3 * nT,)),
                   *[pltpu.HBM(s.shape, s.dtype) for s in shards], *[pltpu.HBM(z.shape, z.dtype) for z in zones],
                   jax.ShapeDtypeStruct((8, LANES), F32)),
        in_specs=[HBM] * (2 * nT), out_specs=(SEM, SEM, *[HBM] * (2 * nT), pl.BlockSpec(memory_space=pltpu.VMEM)),
        input_output_aliases={i: 2 + i for i in range(2 * nT)},
        compiler_params=pltpu.CompilerParams(has_side_effects=EFFECT))(*[_in_hbm(a) for a in list(shards) + zones])
    return outs[0], outs[1], outs[2:2 + nT], outs[2 + nT:2 + 2 * nT], outs[-1]


def gather_wait(name, t, shard, zone, ssem, rsem, after, whole):
    after = after if isinstance(after, (list, tuple)) else [after]

    def body(src_ref, land_ref, ssem_ref, rsem_ref, *rest):
        x, y, c, chips = _place()
        rows = _rows_part(shard.shape, whole, c)
        for k, (px, py) in enumerate(chips):
            cp = _rcopy(src_ref.at[rows], land_ref.at[2 * px + py, rows], ssem_ref.at[3 * t + k], rsem_ref.at[3 * t + k],
                        (px, py, c))
            cp.wait_send()
            cp.wait_recv()

    return pl.pallas_call(
        body, name=name, out_shape=(pltpu.HBM(shard.shape, shard.dtype), pltpu.HBM(zone.shape, zone.dtype)),
        in_specs=(HBM, HBM, SEM, SEM, *[ANY] * len(after)), out_specs=(HBM, HBM), input_output_aliases={0: 0, 1: 1},
        compiler_params=pltpu.CompilerParams(has_side_effects=EFFECT))(shard, zone, ssem, rsem, *after)


def pair_swap(name, zone):
    hr = zone.shape[1] // 2

    def body(z_in, z_ref, ssem, rsem):
        x, y, c, chips = _place()
        cps = []
        for k, (px, py) in enumerate(chips):
            blk = z_ref.at[2 * px + py, pl.ds(c * hr, hr)]
            cps.append(_rcopy(blk, blk, ssem.at[k], rsem.at[k], (x, y, 1 - c)))
            cps[-1].start()
        for k, (px, py) in enumerate(chips):
            blk = z_ref.at[2 * px + py, pl.ds((1 - c) * hr, hr)]
            _rcopy(blk, blk, ssem.at[k], rsem.at[k], (x, y, 1 - c)).wait_recv()
        for cp in cps:
            cp.wait_send()

    return pl.pallas_call(
        body, name=name, in_specs=[ANY], out_specs=ANY, out_shape=jax.ShapeDtypeStruct(zone.shape, zone.dtype),
        input_output_aliases={0: 0},
        scratch_shapes=[pltpu.SemaphoreType.DMA((3,)), pltpu.SemaphoreType.DMA((3,))],
        compiler_params=_params())(zone)


def _swap_copies(z_ref, ssem, rsem):
    hr = z_ref.shape[1] // 2
    x, y, c, chips = _place()
    pairs = []
    for k, (px, py) in enumerate(chips):
        mine = z_ref.at[2 * px + py, pl.ds(c * hr, hr)]
        theirs = z_ref.at[2 * px + py, pl.ds((1 - c) * hr, hr)]
        pairs.append((_rcopy(mine, mine, ssem.at[k], rsem.at[k], (x, y, 1 - c)),
                      _rcopy(theirs, theirs, ssem.at[k], rsem.at[k], (x, y, 1 - c))))
    return pairs


def swap_start(name, zone):
    def body(z_ref, ssem, rsem, z_out, token):
        for mine, _ in _swap_copies(z_ref, ssem, rsem):
            mine.start()
        token[...] = jnp.zeros_like(token)

    return pl.pallas_call(
        body, name=name,
        out_shape=(pltpu.SemaphoreType.DMA((3,)), pltpu.SemaphoreType.DMA((3,)), pltpu.HBM(zone.shape, zone.dtype),
                   jax.ShapeDtypeStruct((8, LANES), F32)),
        in_specs=[HBM], out_specs=(SEM, SEM, HBM, pl.BlockSpec(memory_space=pltpu.VMEM)), input_output_aliases={0: 2},
        compiler_params=pltpu.CompilerParams(has_side_effects=EFFECT))(_in_hbm(zone))


def swap_wait(name, zone, ssem, rsem, after):
    def body(z_ref, ssem_ref, rsem_ref, after_ref, z_out):
        for mine, theirs in _swap_copies(z_ref, ssem_ref, rsem_ref):
            mine.wait_send()
            theirs.wait_recv()

    return pl.pallas_call(
        body, name=name, out_shape=(pltpu.HBM(zone.shape, zone.dtype),),
        in_specs=(HBM, SEM, SEM, ANY), out_specs=(HBM,), input_output_aliases={0: 0},
        compiler_params=pltpu.CompilerParams(has_side_effects=EFFECT))(zone, ssem, rsem, after)[0]


N_SENDERS = 7


def _scatter_copies(g_ref, l_ref, ssem, rsem):
    x, y, c, chips = _place()
    cps = []
    for k, (px, py) in enumerate(chips):
        for d in range(2):
            to = (c + d) % 2
            cps.append(_rcopy(g_ref.at[2 * px + py, to], l_ref.at[2 * k + d], ssem.at[2 * k + d], rsem.at[2 * k + d],
                              (px, py, to)))
    cps.append(_rcopy(g_ref.at[2 * x + y, 1 - c], l_ref.at[6], ssem.at[6], rsem.at[6], (x, y, 1 - c)))
    return cps


def scatter_start(name, g):
    def body(g_ref, l_ref, ssem, rsem, g_out, l_out, token):
        for cp in _scatter_copies(g_ref, l_ref, ssem, rsem):
            cp.start()
        token[...] = jnp.zeros_like(token)

    zone = lax.empty((N_SENDERS,) + g.shape[2:], g.dtype)
    return pl.pallas_call(
        body, name=name,
        out_shape=(pltpu.SemaphoreType.DMA((N_SENDERS,)), pltpu.SemaphoreType.DMA((N_SENDERS,)),
                   pltpu.HBM(g.shape, g.dtype), pltpu.HBM(zone.shape, zone.dtype), jax.ShapeDtypeStruct((8, LANES), F32)),
        in_specs=[HBM, HBM], out_specs=(SEM, SEM, HBM, HBM, pl.BlockSpec(memory_space=pltpu.VMEM)),
        input_output_aliases={0: 2, 1: 3},
        compiler_params=pltpu.CompilerParams(has_side_effects=EFFECT))(_in_hbm(g), _in_hbm(zone))


def scatter_wait(name, g, zone, ssem, rsem, after):
    def body(g_ref, l_ref, ssem_ref, rsem_ref, after_ref, g_out, l_out):
        for cp in _scatter_copies(g_ref, l_ref, ssem_ref, rsem_ref):
            cp.wait_send()
            cp.wait_recv()

    return pl.pallas_call(
        body, name=name, out_shape=(pltpu.HBM(g.shape, g.dtype), pltpu.HBM(zone.shape, zone.dtype)),
        in_specs=(HBM, HBM, SEM, SEM, ANY), out_specs=(HBM, HBM), input_output_aliases={0: 0, 1: 1},
        compiler_params=pltpu.CompilerParams(has_side_effects=EFFECT))(g, zone, ssem, rsem, after)


def sum_parts(name, g, landed, chip_idx, c_idx):
    hr, C = g.shape[2:]
    tr = _row_tile(hr, C, min_rows=16)

    def body(me_ref, c_ref, g_ref, l_ref, o_ref):
        acc = g_ref[...].astype(F32)
        for s in range(N_SENDERS):
            acc = acc + l_ref[s].astype(F32)
        o_ref[...] = acc

    return pl.pallas_call(
        body, name=name,
        grid_spec=pltpu.PrefetchScalarGridSpec(
            num_scalar_prefetch=2, grid=(hr // tr,),
            in_specs=[pl.BlockSpec((None, None, tr, C), lambda i, me_ref, c_ref: (me_ref[0], c_ref[0], i, 0)),
                      pl.BlockSpec((N_SENDERS, tr, C), lambda i, me_ref, c_ref: (0, i, 0))],
            out_specs=pl.BlockSpec((tr, C), lambda i, me_ref, c_ref: (i, 0))),
        out_shape=jax.ShapeDtypeStruct((hr, C), F32),
        compiler_params=_params(('parallel',)))(chip_idx, c_idx, g, landed)


def pair_join(name, halves):
    nT = len(halves)

    def body(*refs):
        ins, outs = refs[:nT], refs[nT:2 * nT]
        ssem, rsem = refs[2 * nT:]
        x, y, c, _ = _place()
        cps = [_rcopy(ins[t], outs[t], ssem.at[t], rsem.at[t], (x, y, 1 - c)) for t in range(nT)]
        for cp in cps:
            cp.start()
        for cp in cps:
            cp.wait()

    return pl.pallas_call(
        body, name=name, in_specs=[ANY] * nT, out_specs=[ANY] * nT,
        out_shape=[jax.ShapeDtypeStruct(h.shape, h.dtype) for h in halves],
        scratch_shapes=[pltpu.SemaphoreType.DMA((nT,)), pltpu.SemaphoreType.DMA((nT,))],
        compiler_params=_params())(*halves)


N_DEVICES = 8


def _spread_copies(b_ref, l_ref, ssem, rsem):
    x, y, c, chips = _place()
    me = 4 * x + 2 * y + c
    pairs = []
    for px, py, pc in [(px, py, pc) for px, py in chips for pc in (c, 1 - c)] + [(x, y, 1 - c)]:
        it = 4 * px + 2 * py + pc
        pairs.append((_rcopy(b_ref, l_ref.at[me], ssem.at[it], rsem.at[me], (px, py, pc)),
                      _rcopy(b_ref, l_ref.at[it], ssem.at[it], rsem.at[it], (px, py, pc))))
    return pairs


def spread_start(name, buf):
    def body(b_ref, l_ref, ssem, rsem, b_out, l_out, token):
        for mine, _ in _spread_copies(b_ref, l_ref, ssem, rsem):
            mine.start()
        token[...] = jnp.zeros_like(token)

    zone = lax.empty((N_DEVICES,) + buf.shape, buf.dtype)
    return pl.pallas_call(
        body, name=name,
        out_shape=(pltpu.SemaphoreType.DMA((N_DEVICES,)), pltpu.SemaphoreType.DMA((N_DEVICES,)),
                   pltpu.HBM(buf.shape, buf.dtype), pltpu.HBM(zone.shape, zone.dtype), jax.ShapeDtypeStruct((8, LANES), F32)),
        in_specs=[HBM, HBM], out_specs=(SEM, SEM, HBM, HBM, pl.BlockSpec(memory_space=pltpu.VMEM)),
        input_output_aliases={0: 2, 1: 3},
        compiler_params=pltpu.CompilerParams(has_side_effects=EFFECT))(_in_hbm(buf), _in_hbm(zone))


def spread_wait(name, buf, zone, ssem, rsem, after):
    def body(b_ref, l_ref, ssem_ref, rsem_ref, after_ref, b_out, l_out):
        for mine, theirs in _spread_copies(b_ref, l_ref, ssem_ref, rsem_ref):
            mine.wait_send()
            theirs.wait_recv()

    return pl.pallas_call(
        body, name=name, out_shape=(pltpu.HBM(buf.shape, buf.dtype), pltpu.HBM(zone.shape, zone.dtype)),
        in_specs=(HBM, HBM, SEM, SEM, ANY), out_specs=(HBM, HBM), input_output_aliases={0: 0, 1: 1},
        compiler_params=pltpu.CompilerParams(has_side_effects=EFFECT))(buf, zone, ssem, rsem, after)


def sum_devices(name, zone):
    _, R, C = zone.shape
    tr = _row_tile(R, C)

    def body(z_ref, o_ref):
        acc = z_ref[0]
        for d in range(1, N_DEVICES):
            acc = acc + z_ref[d]
        o_ref[...] = acc

    return pl.pallas_call(
        body, name=name, grid=(R // tr,), in_specs=[pl.BlockSpec((N_DEVICES, tr, C), lambda i: (0, i, 0))],
        out_specs=pl.BlockSpec((tr, C), lambda i: (i, 0)), out_shape=jax.ShapeDtypeStruct((R, C), F32),
        compiler_params=_params(('parallel',)))(zone)


class _InWindows:
    def __init__(self, FW, LW, H, C):
        gap = LANES - H
        padded = lambda o: o if o < 3 * FW + H else o + gap
        self.width = 3 * FW + LANES + 2 * LW
        self.f_block = 3 * FW // LANES
        self.first = [padded(C * j) // LANES for j in range(N_CHIPS)]
        self.blocks = max(padded(C * (j + 1) - 1) // LANES - self.first[j] + 1 for j in range(N_CHIPS))
        assert all((b + self.blocks) * LANES <= self.width for b in self.first)
        self.cols = self.blocks * LANES
        self.runs = []
        for j in range(N_CHIPS):
            cut = min(max(3 * FW + H - C * j, 0), C)
            spans = [(0, cut), (cut, C)]
            self.runs.append([(t0, t1, padded(C * j + t0) - LANES * self.first[j]) for t0, t1 in spans if t1 > t0])

    def to_window(self, shard, chip):
        def place(j, s):
            parts, pos = [], 0
            for t0, t1, w0 in self.runs[j]:
                parts += [jnp.zeros((s.shape[0], w0 - pos), s.dtype), s[:, t0:t1]]
                pos = w0 + t1 - t0
            parts.append(jnp.zeros((s.shape[0], self.cols - pos), s.dtype))
            return jnp.concatenate([p for p in parts if p.shape[1]], axis=1)
        return lax.switch(chip, [functools.partial(place, j) for j in range(N_CHIPS)], shard)

    def from_window(self, win, chip):
        def take(j, w):
            return jnp.concatenate([w[:, w0:w0 + t1 - t0] for t0, t1, w0 in self.runs[j]], axis=1)
        return lax.switch(chip, [functools.partial(take, j) for j in range(N_CHIPS)], win)

    def _spans(self, j):
        b0, b1 = self.first[j], self.first[j] + self.blocks
        return (b0, min(b1, self.f_block)), b0 <= self.f_block < b1, (max(b0, self.f_block + 1), b1)

    def assemble(self, zone):
        main, f_blk = None, None
        for j in range(N_CHIPS):
            (a0, a1), has_f, (c0, c1) = self._spans(j)
            for p0, p1, shift in ((a0, a1, 0), (c0, c1, 1)):
                if p1 > p0:
                    part = zone[j][:, (p0 - self.first[j]) * LANES:(p1 - self.first[j]) * LANES]
                    part = jnp.pad(part, ((0, 0), ((p0 - shift) * LANES, self.width - LANES - (p1 - shift) * LANES)))
                    main = part if main is None else main + part
            if has_f:
                part = zone[j][:, (self.f_block - self.first[j]) * LANES:(self.f_block - self.first[j] + 1) * LANES]
                f_blk = part if f_blk is None else f_blk + part
        return main, f_blk

    def windows(self, main, f_blk):
        out = []
        for j in range(N_CHIPS):
            (a0, a1), has_f, (c0, c1) = self._spans(j)
            parts = [main[:, a0 * LANES:a1 * LANES]] if a1 > a0 else []
            parts += [f_blk] if has_f else []
            parts += [main[:, (c0 - 1) * LANES:(c1 - 1) * LANES]] if c1 > c0 else []
            out.append(jnp.concatenate(parts, axis=1))
        return jnp.stack(out)


_PACK = 8 * LANES


PACK_ROWS = 256


def _pack(arrs):
    flat = []
    for a in arrs:
        v = a.reshape(-1).astype(F32)
        flat.append(jnp.pad(v, (0, (-v.shape[0]) % _PACK)))
    rows = sum(v.shape[0] for v in flat) // LANES
    flat.append(jnp.zeros(((-rows) % PACK_ROWS) * LANES, F32))
    return jnp.concatenate(flat).reshape(-1, LANES)


def _unpack(buf, shapes):
    out, off = [], 0
    flat = buf.reshape(-1)
    for sh in shapes:
        n = math.prod(sh)
        out.append(flat[off:off + n].reshape(sh))
        off += n + (-n) % _PACK
    return out


def kernel(x, mem, g_mix, w_in, b_f, g_q, g_k, conv_w, conv_b, w_ra, b_ra, w_ri, b_ri, lam, g_fox_out, g_lru_out, w_out, g_xattn, g_mem, w_cq, w_ckv, g_cq, g_ck, w_co, g_ffn, w_gate_up, w_down, loss_target, m_g_mix, m_w_in, m_b_f, m_g_q, m_g_k, m_conv_w, m_conv_b, m_w_ra, m_b_ra, m_w_ri, m_b_ri, m_lam, m_g_fox_out, m_g_lru_out, m_w_out, m_g_xattn, m_g_mem, m_w_cq, m_w_ckv, m_g_cq, m_g_ck, m_w_co, m_g_ffn, m_w_gate_up, m_w_down, v_g_mix, v_w_in, v_b_f, v_g_q, v_g_k, v_conv_w, v_conv_b, v_w_ra, v_b_ra, v_w_ri, v_b_ri, v_lam, v_g_fox_out, v_g_lru_out, v_w_out, v_g_xattn, v_g_mem, v_w_cq, v_w_ckv, v_g_cq, v_g_ck, v_w_co, v_g_ffn, v_w_gate_up, v_w_down):
    given = dict(locals())
    W = {n: given[n][0] for n in WEIGHTS}
    M1 = {n: given['m_' + n][0] for n in WEIGHTS}
    V1 = {n: given['v_' + n][0] for n in WEIGHTS}
    xs, ms, tgt = x[0], mem[0], loss_target[0]
    S, D = xs.shape
    H = W['b_f'].shape[0]
    FW = H * HEAD_DIM
    LW = W['lam'].shape[0]
    nb = W['w_ra'].shape[0]
    XW = W['w_cq'].shape[1]
    F = W['w_down'].shape[0] * N_CHIPS
    IN_W = W['w_in'].shape[1] * N_CHIPS
    assert FW == LW and LW == nb * LANES and IN_W == 3 * FW + H + 2 * LW and H <= 8
    T = _tile(S, (512, 256, 128))
    c_idx = lax.axis_index('c').astype(jnp.int32).reshape(1)
    chip = 2 * lax.axis_index('x') + lax.axis_index('y')
    chip_idx = chip.astype(jnp.int32).reshape(1)
    vec = lambda n: W[n].reshape(1, -1)

    wins = _InWindows(FW, LW, H, W['w_in'].shape[1])
    started = {}
    g_tok = jnp.zeros((1, 1), F32)
    rest = ['w_gate_up', 'w_out', 'w_cq', 'w_ckv', 'w_co', 'w_down']
    for call, names in (('gather_start_first', ['conv_w', 'w_in']), ('gather_start_rest', rest)):
        own = [W[n].reshape(-1, LANES) if n == 'conv_w' else W[n].astype(BF16) + g_tok.astype(BF16) for n in names]
        own = [wins.to_window(o, chip) if n == 'w_in' else o for n, o in zip(names, own)]
        ssem, rsem, srcs, zones, tok = gather_start(call, own, [n == 'conv_w' for n in names])
        g_tok = tok[0:1, 0:1]
        started.update({n: (t, srcs[t], zones[t], ssem, rsem) for t, n in enumerate(names)})

    def fetch(n, after):
        t, g_src, g_zone, g_ssem, g_rsem = started[n]
        src, zone = gather_wait('gather_wait_' + n, t, g_src, g_zone, g_ssem, g_rsem, after, n == 'conv_w')
        if n != 'conv_w':
            zone = pair_swap('pair_swap_' + n, zone)
        return lax.dynamic_update_index_in_dim(zone, src, chip, 0)

    def fetch_begin(n, after):
        t, g_src, g_zone, g_ssem, g_rsem = started[n]
        src, zone = gather_wait('gather_wait_' + n, t, g_src, g_zone, g_ssem, g_rsem, after, False)
        ssem, rsem, zone, _ = swap_start('swap_start_' + n, zone)
        return src, zone, ssem, rsem

    def fetch_end(n, begun, after):
        src, zone, ssem, rsem = begun
        return lax.dynamic_update_index_in_dim(swap_wait('swap_wait_' + n, zone, ssem, rsem, after), src, chip, 0)

    b_f_pad = jnp.pad(vec('b_f'), ((0, 0), (0, LANES - H)))
    u_off, g_off = 3 * FW // LANES, (3 * FW + LW) // LANES

    h1 = norm_fwd('norm_mix', xs, vec('g_mix') + g_tok[0:1, 0:1])
    conv_full = fetch('conv_w', h1).reshape(N_CHIPS, CONV_W, LW // N_CHIPS).transpose(1, 0, 2).reshape(CONV_W, LW)
    w5, wf = wins.assemble(fetch('w_in', [h1, M1['w_in'], V1['w_in']]))
    proj = _mm('proj_in', h1, w5, 'nn', F32)
    f_raw = _mm('proj_f', h1, wf, 'nn', F32)
    qn, kn, vb = qkv_fwd(proj, vec('g_q'), vec('g_k'), FW)
    cc = fgate_fwd(f_raw, b_f_pad)
    ct = cc[:, :8].T
    o_fox, lse = fox_fwd(qn, kn, vb, cc, ct, T)
    lru_w = (conv_full, vec('conv_b'), W['w_ra'], vec('b_ra'), W['w_ri'], vec('b_ri'), vec('lam'))
    y_lru = lru_fwd(proj, *lru_w, u_off, g_off)
    mixn = mix_fwd(o_fox, y_lru, vec('g_fox_out'), vec('g_lru_out'))
    w_out_f = fetch('w_out', mixn).reshape(2 * FW, D)
    begun = {n: fetch_begin(n, mixn) for n in ('w_cq', 'w_ckv', 'w_co', 'w_gate_up')}
    x1 = _mm('proj_out', mixn, w_out_f, 'nn', F32, res=xs)

    hq = norm_fwd('norm_xq', x1, vec('g_xattn'))
    mn = norm_fwd('norm_mem', ms, vec('g_mem'))
    w_cq_f = fetch_end('w_cq', begun['w_cq'], hq).reshape(D, XW)
    w_ckv_f = fetch_end('w_ckv', begun['w_ckv'], hq).reshape(D, 2 * XW)
    cq_raw = _mm('proj_cq', hq, w_cq_f, 'nn', F32)
    ckv = _mm('proj_ckv', mn, w_ckv_f, 'nn', F32)
    o_x = xattn_fwd(cq_raw, ckv, vec('g_cq'), vec('g_ck'))
    begun['w_down'] = fetch_begin('w_down', o_x)
    w_co_g = fetch_end('w_co', begun['w_co'], o_x)
    x2 = _mm_colsharded('proj_co', o_x, w_co_g, F32, res=x1)

    hf = norm_fwd('norm_ffn', x2, vec('g_ffn'))
    w_gu_g = fetch_end('w_gate_up', begun['w_gate_up'], hf)
    gu, act = gate_up_fwd(hf, w_gu_g, F)
    w_down_f = fetch_end('w_down', begun['w_down'], act).reshape(F, D)
    dy, dyb, loss_blk = down_fwd_loss(act, w_down_f, x2, tgt)

    gw, pending = {}, []

    def reduce_begin(n, g):
        sp = g.reshape(N_CHIPS, 2, g.shape[1] // 2, g.shape[2])
        ssem, rsem, sp, zone, tok = scatter_start('scatter_start_' + n, sp)
        pending.append((n, sp, zone, ssem, rsem))
        return tok[0:1, 0:1]

    t_down = reduce_begin('w_down', _mm('bwd_down_w', act, dyb, 'tn', BF16).reshape(N_CHIPS, F // N_CHIPS, D))
    dgu = down_bwd_x(dyb, w_down_f, gu, t_down)
    dhf = _mm_colsharded_t('bwd_gate_up_x', dgu, w_gu_g, F32)
    t_gu = reduce_begin('w_gate_up', _mm_grad_colsharded('bwd_gate_up_w', hf, dgu, N_CHIPS, BF16))
    dx2, dx2b, gw['g_ffn'] = norm_bwd('norm_ffn_bwd', x2, vec('g_ffn') + t_down + t_gu, dhf, res=dy)

    do_x = _mm_colsharded_t('bwd_co_x', dx2b, w_co_g, BF16)
    t_co = reduce_begin('w_co', _mm_grad_colsharded('bwd_co_w', o_x, dx2b, N_CHIPS, BF16))
    dcq_raw, dckv, gw['g_cq'], gw['g_ck'] = xattn_bwd(cq_raw, ckv, vec('g_cq') + t_co, vec('g_ck'), do_x)
    dhq = _mm('bwd_cq_x', dcq_raw, w_cq_f, 'nt', F32)
    t_cq = reduce_begin('w_cq', _mm('bwd_cq_w', hq, dcq_raw, 'tn', BF16).reshape(N_CHIPS, D // N_CHIPS, XW))
    dmn = _mm('bwd_ckv_x', dckv, w_ckv_f, 'nt', F32)
    t_ckv = reduce_begin('w_ckv', _mm('bwd_ckv_w', mn, dckv, 'tn', BF16).reshape(N_CHIPS, D // N_CHIPS, 2 * XW))
    (gw['g_mem'],) = norm_bwd('norm_mem_bwd', ms, vec('g_mem'), dmn, want_dx=False)
    dx1, dx1b, gw['g_xattn'] = norm_bwd('norm_xq_bwd', x1, vec('g_xattn') + t_cq + t_ckv, dhq, res=dx2)

    dmix = _mm('bwd_out_x', dx1b, w_out_f, 'nt', F32)
    t_out = reduce_begin('w_out', _mm('bwd_out_w', mixn, dx1b, 'tn', BF16).reshape(N_CHIPS, 2 * FW // N_CHIPS, D))
    do_fox, delta, dy_lru, gw['g_fox_out'], gw['g_lru_out'] = mix_bwd(o_fox, y_lru, vec('g_fox_out') + t_out,
                                                                     vec('g_lru_out'), dmix)
    (du, dgate, gw['conv_w'], gw['conv_b'], gw['w_ra'], gw['b_ra'], gw['w_ri'], gw['b_ri'],
     gw['lam']) = lru_bwd(proj, dy_lru, *lru_w, u_off, g_off)
    early = [n for n in SMALL if n not in ('g_q', 'g_k', 'b_f', 'g_mix')]
    late = [n for n in SMALL if n not in early]
    e_ssem, e_rsem, e_buf, e_zone, e_tok = spread_start('spread_start_early', _pack([gw[n] for n in early]))
    dqn, delta2 = fox_bwd_q(qn, kn, vb, do_fox, cc, ct, lse, delta, T)
    dkn, dv, dct = fox_bwd_kv(qn, kn, vb, do_fox, cc, ct, lse, delta2, T)
    dq, dk, gw['g_q'], gw['g_k'] = qkv_bwd(proj, vec('g_q') + e_tok[0:1, 0:1], vec('g_k'), dqn, dkn, FW)
    dc = jnp.pad(dct.reshape(H, S).T, ((0, 0), (0, LANES - H)))
    df, db_f = fgate_bwd(f_raw, b_f_pad, dc, H)
    gw['b_f'] = db_f[:, :H]
    dproj = jnp.concatenate([dq, dk, dv, du, dgate], axis=1)
    dw5 = _mm('bwd_in_w', h1, dproj, 'tn', BF16)
    dwf = _mm('bwd_f_w', h1, df, 'tn', BF16)
    t_in = reduce_begin('w_in', wins.windows(dw5, dwf))
    dh_a = _mm('bwd_f_x', df, wf, 'nt', F32)
    dh1 = _mm('bwd_in_x', dproj, w5, 'nt', F32, res=dh_a)
    grad_x, _, gw['g_mix'] = norm_bwd('norm_mix_bwd', xs, vec('g_mix') + t_in, dh1, res=dx1)
    l_ssem, l_rsem, l_buf, l_zone, _ = spread_start('spread_start_late',
                                                    _pack([gw[n] for n in late] + [loss_blk[0:1, 0:1]]))

    grads, delta_w, new_m, new_v = {}, {}, {}, {}
    done = grad_x
    for n, part, zone, ssem, rsem in pending:
        part, landed = scatter_wait('scatter_wait_' + n, part, zone, ssem, rsem, done)
        mine = sum_parts('sum_parts_' + n, part, landed, chip_idx, c_idx)
        (other,) = pair_join('pair_join_' + n, [mine])
        if n == 'w_in':
            mine, other = wins.from_window(mine, chip), wins.from_window(other, chip)
        grads[n], delta_w[n], new_m[n], new_v[n] = adamw_halves('adamw_' + n, W[n], mine, other, M1[n], V1[n], c_idx)
        done = delta_w[n]

    device = 4 * lax.axis_index('x') + 2 * lax.axis_index('y') + lax.axis_index('c')
    summed = {}
    for tag, names, buf, zone, ssem, rsem in (('early', early, e_buf, e_zone, e_ssem, e_rsem),
                                              ('late', late + ['loss'], l_buf, l_zone, l_ssem, l_rsem)):
        buf, zone = spread_wait('spread_wait_' + tag, buf, zone, ssem, rsem, done)
        total = sum_devices('sum_small_' + tag, lax.dynamic_update_index_in_dim(zone, buf, device, 0))
        summed.update(zip(names, _unpack(total, [gw[n].shape if n != 'loss' else (1, 1) for n in names])))
    loss = summed['loss'].reshape(())
    for n in SMALL:
        g = summed[n]
        grads[n] = g.reshape(W[n].shape) if n != 'conv_w' else lax.dynamic_slice_in_dim(
            g, chip * (LW // N_CHIPS), LW // N_CHIPS, axis=1)
    packs = [_pack([d[n] for n in SMALL]) for d in (W, grads, M1, V1)]
    shapes = [W[n].shape for n in SMALL]
    for d, res in zip((delta_w, new_m, new_v), adamw('adamw_small', *packs)):
        d.update(zip(SMALL, _unpack(res, shapes)))

    lead = lambda d: [d[n][None] for n in WEIGHTS]
    return (loss, grad_x[None], *lead(grads), *lead(delta_w), *lead(new_m), *lead(new_v))
```

```python
import functools
import math

import jax
import jax.numpy as jnp
from jax import lax
from jax.experimental import pallas as pl
from jax.experimental.pallas import tpu as pltpu

F32 = jnp.float32
BF16 = jnp.bfloat16
HEAD_DIM = 128
LANES = 128
LRU_C = 8.0
RMS_EPS = 1e-6
CONV_W = 4
ADAM_LR = 0.001
ADAM_B1 = 0.9
ADAM_B2 = 0.999
ADAM_EPS = 1e-08
ADAM_WD = 0.01
ADAM_STEP = 10
VMEM_LIMIT = 56 * 1024 * 1024
N_CHIPS = 4
MESH = pl.DeviceIdType.MESH
ANY = pl.BlockSpec(memory_space=pl.ANY)

WEIGHTS = ['g_mix', 'w_in', 'b_f', 'g_q', 'g_k', 'conv_w', 'conv_b', 'w_ra', 'b_ra', 'w_ri', 'b_ri', 'lam',
           'g_fox_out', 'g_lru_out', 'w_out', 'g_xattn', 'g_mem', 'w_cq', 'w_ckv', 'g_cq', 'g_ck', 'w_co', 'g_ffn',
           'w_gate_up', 'w_down']
BIG = ['w_in', 'w_out', 'w_cq', 'w_ckv', 'w_co', 'w_gate_up', 'w_down']
SMALL = [n for n in WEIGHTS if n not in BIG]


def _params(sem=None):
    if sem is None:
        return pltpu.CompilerParams(vmem_limit_bytes=VMEM_LIMIT)
    return pltpu.CompilerParams(dimension_semantics=sem, vmem_limit_bytes=VMEM_LIMIT)


def _tile(n, cands):
    for t in cands:
        if n % t == 0:
            return t
    return n


ROW_BLOCK_BYTES = 1 << 20


def _row_tile(n_rows, n_cols, min_rows=8):
    cands = [t for t in (512, 256, 128, 64, 32, 16, 8) if t >= min_rows and t * n_cols * 4 <= ROW_BLOCK_BYTES]
    return _tile(n_rows, cands or [min_rows])


def _sigmoid(z):
    return 1.0 / (1.0 + jnp.exp(-z))


def _softplus(z):
    return jnp.maximum(z, 0.0) + jnp.log(1.0 + jnp.exp(-jnp.abs(z)))


def _neg_expm1(z):
    series = -z * (1.0 + z * (0.5 + z * (1.0 / 6.0 + z * (1.0 / 24.0 + z * (1.0 / 120.0)))))
    return jnp.where(z > -0.25, series, 1.0 - jnp.exp(z))


_GELU_K = math.sqrt(2.0 / math.pi)


def _gelu_and_grad(z):
    inner = _GELU_K * (z + 0.044715 * z * z * z)
    t = jnp.tanh(inner)
    g = 0.5 * z * (1.0 + t)
    dg = 0.5 * (1.0 + t) + 0.5 * z * (1.0 - t * t) * _GELU_K * (1.0 + 3.0 * 0.044715 * z * z)
    return g, dg


def _rms(xv, g):
    r = lax.rsqrt(jnp.mean(xv * xv, axis=-1, keepdims=True) + RMS_EPS)
    return xv * r * g


def _rms_bwd(xv, g, dy):
    r = lax.rsqrt(jnp.mean(xv * xv, axis=-1, keepdims=True) + RMS_EPS)
    xh = xv * r
    dyg = dy * g
    dx = r * (dyg - xh * jnp.mean(dyg * xh, axis=-1, keepdims=True))
    return dx, jnp.sum(dy * xh, axis=0, keepdims=True)


def _heads(fn, n_heads, *arrs):
    outs = [fn(*[a[:, h * HEAD_DIM:(h + 1) * HEAD_DIM] for a in arrs]) for h in range(n_heads)]
    first = jnp.concatenate([o[0] for o in outs], axis=1) if n_heads > 1 else outs[0][0]
    rest = [functools.reduce(lambda p, q: p + q, [o[i] for o in outs]) for i in range(1, len(outs[0]))]
    return (first, *rest)


def _split3(v):
    hi = v.astype(BF16)
    r1 = v - hi.astype(F32)
    mid = r1.astype(BF16)
    lo = (r1 - mid.astype(F32)).astype(BF16)
    return hi, mid, lo


def _acc_out(ref, first, val):
    @pl.when(first)
    def _():
        ref[...] = val

    @pl.when(jnp.logical_not(first))
    def _():
        ref[...] += val


_DIMS = {'nn': (((1,), (0,)), ((), ())), 'nt': (((1,), (1,)), ((), ())), 'tn': (((0,), (0,)), ((), ()))}


MM_VMEM_BYTES = 36 * 1024 * 1024


MXU_FLOPS = 800e12
HBM_BYTES_S = 3.2e12
VMEM_ADD_BYTES_S = 8e12
STEP_S = 0.35e-6


def _k_tile(K, tm, tn, a, b, o_dtype, res):
    fixed = tm * tn * (2 * jnp.dtype(o_dtype).itemsize + 4 + (8 if res is not None else 0))
    per_k = 2 * (tm * a.dtype.itemsize + tn * b.dtype.itemsize)
    per_k += 2 * tm * (a.dtype.itemsize > 2) + 2 * tn * (b.dtype.itemsize > 2)
    units = K // LANES
    for d in sorted((d for d in range(1, units + 1) if units % d == 0), reverse=True):
        if fixed + d * LANES * per_k <= MM_VMEM_BYTES:
            return d * LANES
    return None


def _mm_tiles(M, N, K, k_span, a, b, o_dtype, res, tn_cands=(2048, 1024, 512, 256, 128)):
    best = None
    for tm in (2048, 1024, 512, 256, 128):
        for tn in tn_cands:
            if M % tm or N % tn:
                continue
            tk = _k_tile(k_span, tm, tn, a, b, o_dtype, res)
            if tk is None:
                continue
            nk = K // tk
            traffic = (M * K * a.dtype.itemsize * (N // tn) + K * N * b.dtype.itemsize * (M // tm)
                       + M * N * (jnp.dtype(o_dtype).itemsize + (4 if res is not None else 0)))
            work = 2.0 * M * N * K / MXU_FLOPS + (M * N * 4 * nk / VMEM_ADD_BYTES_S if nk > 1 else 0.0)
            t = max(work, traffic / HBM_BYTES_S) + (M // tm) * (N // tn) * nk * STEP_S
            if best is None or t < best[0]:
                best = (t, tm, tn, tk)
    assert best is not None, (M, N, K)
    return best[1:]


def _mm_call(name, a, b, mode, grid, a_spec, b_spec, o_spec, o_shape, o_dtype, acc_shape, res=None):
    nk = grid[2]
    dn = _DIMS[mode]

    def body(*refs):
        a_ref, b_ref = refs[:2]
        r_ref = refs[2] if res is not None else None
        o_ref = refs[3] if res is not None else refs[2]
        part = lax.dot_general(a_ref[...].astype(BF16), b_ref[...].astype(BF16), dn, preferred_element_type=F32)

        def finish(r):
            if r_ref is not None:
                r = r + r_ref[...]
            o_ref[...] = r.astype(o_dtype)

        if nk == 1:
            finish(part)
            return
        acc = refs[-1]
        k = pl.program_id(2)

        @pl.when(k == 0)
        def _():
            acc[...] = part

        @pl.when(k > 0)
        def _():
            acc[...] += part

        @pl.when(k == nk - 1)
        def _():
            finish(acc[...])

    ins = [a, b] + ([] if res is None else [res])
    specs = [a_spec, b_spec] + ([] if res is None else [o_spec])
    return pl.pallas_call(
        body, name=name, grid=grid, in_specs=specs, out_specs=o_spec,
        out_shape=jax.ShapeDtypeStruct(o_shape, o_dtype),
        scratch_shapes=[] if nk == 1 else [pltpu.VMEM(acc_shape, F32)],
        compiler_params=_params(('parallel', 'parallel', 'arbitrary')))(*ins)


def _mm(name, a, b, mode, o_dtype, res=None):
    if mode == 'tn':
        K, M = a.shape
    else:
        M, K = a.shape
    N = b.shape[0] if mode == 'nt' else b.shape[1]
    tm, tn, tk = _mm_tiles(M, N, K, K, a, b, o_dtype, res)
    a_spec = (pl.BlockSpec((tk, tm), lambda m, n, k: (k, m)) if mode == 'tn'
              else pl.BlockSpec((tm, tk), lambda m, n, k: (m, k)))
    b_spec = (pl.BlockSpec((tn, tk), lambda m, n, k: (n, k)) if mode == 'nt'
              else pl.BlockSpec((tk, tn), lambda m, n, k: (k, n)))
    o_spec = pl.BlockSpec((tm, tn), lambda m, n, k: (m, n))
    return _mm_call(name, a, b, mode, (M // tm, N // tn, K // tk), a_spec, b_spec, o_spec, (M, N), o_dtype,
                    (tm, tn), res)


def _mm_colsharded(name, a, w, o_dtype, res=None):
    M, K = a.shape
    J, _, Nj = w.shape
    tm, tn, tk = _mm_tiles(M, J * Nj, K, K, a, w, o_dtype, res,
                           tn_cands=[t for t in (2816, 1408, 1024, 512, 256, 128) if Nj % t == 0])
    per = Nj // tn
    return _mm_call(name, a, w, 'nn', (M // tm, J * per, K // tk),
                    pl.BlockSpec((tm, tk), lambda m, n, k: (m, k)),
                    pl.BlockSpec((None, tk, tn), lambda m, n, k: (n // per, k, n % per)),
                    pl.BlockSpec((tm, tn), lambda m, n, k: (m, n)), (M, J * Nj), o_dtype, (tm, tn), res)


def _planes_spec(arr, rows, cols, row_of, col_of):
    if arr.ndim == 2:
        return pl.BlockSpec((rows, cols), lambda m, n, k: (row_of(m, n, k), col_of(m, n, k)))
    per_plane = arr.shape[2] // cols
    return pl.BlockSpec((None, rows, cols),
                        lambda m, n, k: (col_of(m, n, k) // per_plane, row_of(m, n, k), col_of(m, n, k) % per_plane))


def _mm_colsharded_t(name, a, w, o_dtype):
    M = a.shape[-2]
    J, K, Nj = w.shape
    tm, tn, tk = _mm_tiles(M, K, J * Nj, Nj, a, w, o_dtype, None)
    per = Nj // tk
    return _mm_call(name, a, w, 'nt', (M // tm, K // tn, J * per),
                    _planes_spec(a, tm, tk, lambda m, n, k: m, lambda m, n, k: k),
                    pl.BlockSpec((None, tn, tk), lambda m, n, k: (k // per, n, k % per)),
                    pl.BlockSpec((tm, tn), lambda m, n, k: (m, n)), (M, K), o_dtype, (tm, tn))


def _mm_grad_colsharded(name, a, dy, J, o_dtype):
    S, M = a.shape
    Nj = dy.shape[-1] * (dy.shape[0] if dy.ndim == 3 else 1) // J
    tm, tn, tk = _mm_tiles(M, J * Nj, S, S, a, dy, o_dtype, None,
                           tn_cands=[t for t in (2816, 1408, 1024, 512, 256, 128) if Nj % t == 0])
    per = Nj // tn
    return _mm_call(name, a, dy, 'tn', (M // tm, J * per, S // tk),
                    pl.BlockSpec((tk, tm), lambda m, n, k: (k, m)),
                    _planes_spec(dy, tk, tn, lambda m, n, k: k, lambda m, n, k: n),
                    pl.BlockSpec((None, tm, tn), lambda m, n, k: (n // per, m, n % per)), (J, M, Nj), o_dtype, (tm, tn))


def _rows_call(name, body, n_rows, tr, ins, outs):
    return pl.pallas_call(
        body, name=name, grid=(n_rows // tr,), in_specs=[s for _, s in ins], out_specs=[s for _, _, s in outs],
        out_shape=[jax.ShapeDtypeStruct(sh, dt) for sh, dt, _ in outs],
        compiler_params=_params(('arbitrary',)))(*[a for a, _ in ins])


def _rb(tr, w, cb=0):
    return pl.BlockSpec((tr, w), lambda i: (i, cb))


def _fb(shape):
    nd = len(shape)
    return pl.BlockSpec(shape, lambda i: (0,) * nd)


def norm_fwd(name, xv, g):
    S, D = xv.shape
    tr = _tile(S, (256, 128))

    def body(x_ref, g_ref, o_ref):
        o_ref[...] = _rms(x_ref[...], g_ref[...]).astype(BF16)

    return _rows_call(name, body, S, tr, [(xv, _rb(tr, D)), (g, _fb((1, D)))], [((S, D), BF16, _rb(tr, D))])[0]


def norm_bwd(name, xv, g, dy, res=None, want_dx=True):
    S, D = xv.shape
    tr = _tile(S, (256, 128))

    def body(*refs):
        if res is None:
            x_ref, g_ref, dy_ref = refs[:3]
            outs = refs[3:]
            r_ref = None
        else:
            x_ref, g_ref, dy_ref, r_ref = refs[:4]
            outs = refs[4:]
        dx, dg = _rms_bwd(x_ref[...], g_ref[...], dy_ref[...])
        if r_ref is not None:
            dx = dx + r_ref[...]
        if want_dx:
            outs[0][...] = dx
            outs[1][...] = dx.astype(BF16)
        _acc_out(outs[-1], pl.program_id(0) == 0, dg)

    ins = [(xv, _rb(tr, D)), (g, _fb((1, D))), (dy, _rb(tr, D))] + ([] if res is None else [(res, _rb(tr, D))])
    outs = ([((S, D), F32, _rb(tr, D)), ((S, D), BF16, _rb(tr, D))] if want_dx else []) + [((1, D), F32, _fb((1, D)))]
    return _rows_call(name, body, S, tr, ins, outs)


def qkv_fwd(proj, g_q, g_k, FW):
    S = proj.shape[0]
    H = FW // HEAD_DIM
    tr = _tile(S, (256, 128))

    def body(q_ref, k_ref, v_ref, gq_ref, gk_ref, qo, ko, vo):
        qo[...] = _heads(lambda t: (_rms(t, gq_ref[...]),), H, q_ref[...])[0].astype(BF16)
        ko[...] = _heads(lambda t: (_rms(t, gk_ref[...]),), H, k_ref[...])[0].astype(BF16)
        vo[...] = v_ref[...].astype(BF16)

    o = ((S, FW), BF16, _rb(tr, FW))
    return _rows_call('qkv_fwd', body, S, tr,
                      [(proj, _rb(tr, FW, 0)), (proj, _rb(tr, FW, 1)), (proj, _rb(tr, FW, 2)),
                       (g_q, _fb((1, HEAD_DIM))), (g_k, _fb((1, HEAD_DIM)))], [o, o, o])


def qkv_bwd(proj, g_q, g_k, dqn, dkn, FW):
    S = proj.shape[0]
    H = FW // HEAD_DIM
    tr = _tile(S, (256, 128))

    def body(q_ref, k_ref, gq_ref, gk_ref, dq_ref, dk_ref, dqo, dko, dgq, dgk):
        dq, gq = _heads(lambda t, d: _rms_bwd(t, gq_ref[...], d), H, q_ref[...], dq_ref[...])
        dk, gk = _heads(lambda t, d: _rms_bwd(t, gk_ref[...], d), H, k_ref[...], dk_ref[...])
        dqo[...] = dq.astype(BF16)
        dko[...] = dk.astype(BF16)
        first = pl.program_id(0) == 0
        _acc_out(dgq, first, gq)
        _acc_out(dgk, first, gk)

    o = ((S, FW), BF16, _rb(tr, FW))
    og = ((1, HEAD_DIM), F32, _fb((1, HEAD_DIM)))
    return _rows_call('qkv_bwd', body, S, tr,
                      [(proj, _rb(tr, FW, 0)), (proj, _rb(tr, FW, 1)), (g_q, _fb((1, HEAD_DIM))),
                       (g_k, _fb((1, HEAD_DIM))), (dqn, _rb(tr, FW)), (dkn, _rb(tr, FW))], [o, o, og, og])


def _tri(n, upper):
    r = lax.broadcasted_iota(jnp.int32, (n, n), 0)
    c = lax.broadcasted_iota(jnp.int32, (n, n), 1)
    return jnp.where((c >= r) if upper else (c <= r), 1.0, 0.0).astype(BF16)


def _blocked_cumsum(val, S, blk, reverse):
    tri = _tri(blk, reverse)
    order = range(S // blk - 1, -1, -1) if reverse else range(S // blk)
    carry = jnp.zeros((1, LANES), F32)
    outs = {}
    for bi in order:
        part = val[bi * blk:(bi + 1) * blk]
        acc = carry
        for piece in _split3(part):
            acc = acc + jnp.dot(tri, piece, preferred_element_type=F32)
        outs[bi] = acc
        carry = carry + jnp.sum(part, axis=0, keepdims=True)
    return jnp.concatenate([outs[bi] for bi in range(S // blk)], axis=0)


def fgate_fwd(f_raw, b_f_pad):
    S = f_raw.shape[0]
    blk = _tile(S, (256, 128))

    def body(f_ref, b_ref, c_ref):
        z = f_ref[...] + b_ref[...]
        c_ref[...] = _blocked_cumsum(-_softplus(-z), S, blk, False)

    return pl.pallas_call(body, name='fgate_fwd', grid=(1,), in_specs=[_fb((S, LANES)), _fb((1, LANES))],
                          out_specs=_fb((S, LANES)), out_shape=jax.ShapeDtypeStruct((S, LANES), F32),
                          compiler_params=_params(('arbitrary',)))(f_raw, b_f_pad)


def fgate_bwd(f_raw, b_f_pad, dc, H):
    S = f_raw.shape[0]
    blk = _tile(S, (256, 128))

    def body(f_ref, b_ref, dc_ref, df_ref, db_ref):
        z = f_ref[...] + b_ref[...]
        dlogf = _blocked_cumsum(dc_ref[...], S, blk, True)
        lane = lax.broadcasted_iota(jnp.int32, (S, LANES), 1)
        df = jnp.where(lane < H, dlogf * _sigmoid(-z), 0.0)
        df_ref[...] = df.astype(BF16)
        db_ref[...] = jnp.sum(df, axis=0, keepdims=True)

    return pl.pallas_call(body, name='fgate_bwd', grid=(1,),
                          in_specs=[_fb((S, LANES)), _fb((1, LANES)), _fb((S, LANES))],
                          out_specs=[_fb((S, LANES)), _fb((1, LANES))],
                          out_shape=[jax.ShapeDtypeStruct((S, LANES), BF16), jax.ShapeDtypeStruct((1, LANES), F32)],
                          compiler_params=_params(('arbitrary',)))(f_raw, b_f_pad, dc)


def _fox_logits(q, k, c_blk, ct_blk, h, T, diagonal):
    s = lax.dot_general(q, k, _DIMS['nt'], preferred_element_type=F32) * (1.0 / math.sqrt(HEAD_DIM))
    lane = lax.broadcasted_iota(jnp.int32, c_blk.shape, 1)
    cq = jnp.sum(jnp.where(lane == h, c_blk, 0.0), axis=1, keepdims=True)
    sub = lax.broadcasted_iota(jnp.int32, ct_blk.shape, 0)
    ck = jnp.sum(jnp.where(sub == h, ct_blk, 0.0), axis=0, keepdims=True)
    s = s + cq - ck
    if not diagonal:
        return s
    rows = lax.broadcasted_iota(jnp.int32, (T, T), 0)
    cols = lax.broadcasted_iota(jnp.int32, (T, T), 1)
    return jnp.where(cols <= rows, s, -jnp.inf)


def _below_and_on_diagonal(q_blk, k_blk, step):
    @pl.when(k_blk < q_blk)
    def _():
        step(False)

    @pl.when(k_blk == q_blk)
    def _():
        step(True)


def fox_fwd(qn, kn, vb, c, ct, T):
    S, FW = qn.shape
    H = FW // HEAD_DIM
    Hp = ct.shape[0]
    n = S // T

    HB = _tile(H, (8, 4, 2, 1))
    W2 = HB * HEAD_DIM

    def body(q_ref, k_ref, v_ref, c_ref, ct_ref, o_ref, lse_ref, m_s, l_s, acc_s):
        hb, i, j = pl.program_id(0), pl.program_id(1), pl.program_id(2)

        @pl.when(j == 0)
        def _():
            m_s[...] = jnp.full_like(m_s, -jnp.inf)
            l_s[...] = jnp.zeros_like(l_s)
            acc_s[...] = jnp.zeros_like(acc_s)

        def step(diagonal):
            for hh in range(HB):
                sl = slice(hh * HEAD_DIM, (hh + 1) * HEAD_DIM)
                s = _fox_logits(q_ref[:, sl], k_ref[:, sl], c_ref[...], ct_ref[...], hb * HB + hh, T, diagonal)
                m_old = m_s[hh]
                m_new = jnp.maximum(m_old, jnp.max(s, axis=1, keepdims=True))
                alpha = jnp.exp(m_old - m_new)
                p = jnp.exp(s - m_new)
                l_s[hh] = alpha * l_s[hh] + jnp.sum(p, axis=1, keepdims=True)
                acc_s[hh] = alpha * acc_s[hh] + jnp.dot(p.astype(BF16), v_ref[:, sl], preferred_element_type=F32)
                m_s[hh] = m_new

        _below_and_on_diagonal(i, j, step)

        @pl.when(j == i)
        def _():
            for hh in range(HB):
                o_ref[:, hh * HEAD_DIM:(hh + 1) * HEAD_DIM] = acc_s[hh] / l_s[hh]
                lse_ref[hh] = jnp.broadcast_to(m_s[hh] + jnp.log(l_s[hh]), (T, LANES))

    qs = pl.BlockSpec((T, W2), lambda h, i, j: (i, h))
    ks = pl.BlockSpec((T, W2), lambda h, i, j: (jnp.minimum(j, i), h))
    return pl.pallas_call(
        body, name='fox_fwd', grid=(H // HB, n, n),
        in_specs=[qs, ks, ks, pl.BlockSpec((T, LANES), lambda h, i, j: (i, 0)),
                  pl.BlockSpec((Hp, T), lambda h, i, j: (0, jnp.minimum(j, i)))],
        out_specs=[qs, pl.BlockSpec((HB, T, LANES), lambda h, i, j: (h, i, 0))],
        out_shape=[jax.ShapeDtypeStruct((S, FW), F32), jax.ShapeDtypeStruct((H, S, LANES), F32)],
        scratch_shapes=[pltpu.VMEM((HB, T, 1), F32), pltpu.VMEM((HB, T, 1), F32), pltpu.VMEM((HB, T, HEAD_DIM), F32)],
        compiler_params=_params(('parallel', 'parallel', 'arbitrary')))(qn, kn, vb, c, ct)


def _fox_p_ds(q_ref, k_ref, v_ref, do_ref, c_ref, ct_ref, lse_ref, dl_ref, h, T, diagonal):
    s = _fox_logits(q_ref[...], k_ref[...], c_ref[...], ct_ref[...], h, T, diagonal)
    p = jnp.exp(s - jnp.tile(lse_ref[...], (1, T // LANES)))
    dp = lax.dot_general(do_ref[...], v_ref[...], _DIMS['nt'], preferred_element_type=F32)
    ds = p * (dp - jnp.tile(dl_ref[...], (1, T // LANES)))
    return p, dp, ds


def fox_bwd_q(qn, kn, vb, do, c, ct, lse, dl, T):
    S, FW = qn.shape
    H = FW // HEAD_DIM
    Hp = ct.shape[0]
    n = S // T
    HB = _tile(H, (8, 4, 2, 1))
    W2 = HB * HEAD_DIM

    def body(q_ref, k_ref, v_ref, do_ref, c_ref, ct_ref, lse_ref, dl_ref, dq_ref, dl2_ref, acc_s, rs_s):
        hb, i, j = pl.program_id(0), pl.program_id(1), pl.program_id(2)

        @pl.when(j == 0)
        def _():
            acc_s[...] = jnp.zeros_like(acc_s)
            rs_s[...] = jnp.zeros_like(rs_s)

        def step(diagonal):
            for hh in range(HB):
                sl = slice(hh * HEAD_DIM, (hh + 1) * HEAD_DIM)
                p, dp, ds = _fox_p_ds(q_ref.at[:, sl], k_ref.at[:, sl], v_ref.at[:, sl], do_ref.at[:, sl], c_ref, ct_ref,
                                      lse_ref.at[hh], dl_ref.at[hh], hb * HB + hh, T, diagonal)
                acc_s[hh] += jnp.dot(ds.astype(BF16), k_ref[:, sl], preferred_element_type=F32)
                rs_s[hh] += jnp.sum(p * dp, axis=1, keepdims=True)

        _below_and_on_diagonal(i, j, step)

        @pl.when(j == i)
        def _():
            for hh in range(HB):
                dq_ref[:, hh * HEAD_DIM:(hh + 1) * HEAD_DIM] = acc_s[hh] * (1.0 / math.sqrt(HEAD_DIM))
                dl2_ref[hh] = jnp.broadcast_to(rs_s[hh], (T, LANES))

    qs = pl.BlockSpec((T, W2), lambda h, i, j: (i, h))
    ks = pl.BlockSpec((T, W2), lambda h, i, j: (jnp.minimum(j, i), h))
    st = pl.BlockSpec((HB, T, LANES), lambda h, i, j: (h, i, 0))
    return pl.pallas_call(
        body, name='fox_bwd_q', grid=(H // HB, n, n),
        in_specs=[qs, ks, ks, qs, pl.BlockSpec((T, LANES), lambda h, i, j: (i, 0)),
                  pl.BlockSpec((Hp, T), lambda h, i, j: (0, jnp.minimum(j, i))), st, st],
        out_specs=[qs, st], out_shape=[jax.ShapeDtypeStruct((S, FW), F32), jax.ShapeDtypeStruct((H, S, LANES), F32)],
        scratch_shapes=[pltpu.VMEM((HB, T, HEAD_DIM), F32), pltpu.VMEM((HB, T, 1), F32)],
        compiler_params=_params(('parallel', 'parallel', 'arbitrary')))(qn, kn, vb, do, c, ct, lse, dl)


def fox_bwd_kv(qn, kn, vb, do, c, ct, lse, dl, T):
    S, FW = qn.shape
    H = FW // HEAD_DIM
    Hp = ct.shape[0]
    n = S // T

    HB = _tile(H, (8, 4, 2, 1))
    W2 = HB * HEAD_DIM

    def body(q_ref, k_ref, v_ref, do_ref, c_ref, ct_ref, lse_ref, dl_ref, dk_ref, dv_ref, dc_ref, dk_s, dv_s, dc_s):
        hb, j, i = pl.program_id(0), pl.program_id(1), pl.program_id(2)

        @pl.when(i == 0)
        def _():
            dk_s[...] = jnp.zeros_like(dk_s)
            dv_s[...] = jnp.zeros_like(dv_s)
            dc_s[...] = jnp.zeros_like(dc_s)

        def step(diagonal):
            for hh in range(HB):
                sl = slice(hh * HEAD_DIM, (hh + 1) * HEAD_DIM)
                p, _, ds = _fox_p_ds(q_ref.at[:, sl], k_ref.at[:, sl], v_ref.at[:, sl], do_ref.at[:, sl], c_ref, ct_ref,
                                     lse_ref.at[hh], dl_ref.at[hh], hb * HB + hh, T, diagonal)
                dv_s[hh] += lax.dot_general(p.astype(BF16), do_ref[:, sl], _DIMS['tn'], preferred_element_type=F32)
                dk_s[hh] += lax.dot_general(ds.astype(BF16), q_ref[:, sl], _DIMS['tn'], preferred_element_type=F32)
                dc_s[hh] += jnp.sum(ds, axis=0, keepdims=True)

        _below_and_on_diagonal(i, j, step)

        @pl.when(i == n - 1)
        def _():
            for hh in range(HB):
                sl = slice(hh * HEAD_DIM, (hh + 1) * HEAD_DIM)
                dk_ref[:, sl] = dk_s[hh] * (1.0 / math.sqrt(HEAD_DIM))
                dv_ref[:, sl] = dv_s[hh].astype(BF16)
                dc_ref[hh] = -dc_s[hh]

    qs = pl.BlockSpec((T, W2), lambda h, j, i: (jnp.maximum(i, j), h))
    ks = pl.BlockSpec((T, W2), lambda h, j, i: (j, h))
    st = pl.BlockSpec((HB, T, LANES), lambda h, j, i: (h, jnp.maximum(i, j), 0))
    return pl.pallas_call(
        body, name='fox_bwd_kv', grid=(H // HB, n, n),
        in_specs=[qs, ks, ks, qs, pl.BlockSpec((T, LANES), lambda h, j, i: (jnp.maximum(i, j), 0)),
                  pl.BlockSpec((Hp, T), lambda h, j, i: (0, j)), st, st],
        out_specs=[ks, ks, pl.BlockSpec((HB, 1, T), lambda h, j, i: (h, 0, j))],
        out_shape=[jax.ShapeDtypeStruct((S, FW), F32), jax.ShapeDtypeStruct((S, FW), BF16),
                   jax.ShapeDtypeStruct((H, 1, S), F32)],
        scratch_shapes=[pltpu.VMEM((HB, T, HEAD_DIM), F32), pltpu.VMEM((HB, T, HEAD_DIM), F32),
                        pltpu.VMEM((HB, 1, T), F32)],
        compiler_params=_params(('parallel', 'parallel', 'arbitrary')))(qn, kn, vb, do, c, ct, lse, dl)


def _shift_down(v, d, rows, fill):
    return jnp.where(rows >= d, pltpu.roll(v, d, 0), fill)


def _shift_up(v, d, rows, S, fill):
    return jnp.where(rows < S - d, pltpu.roll(v, S - d, 0), fill)


SUBLANES = 8


def _scan_by_doubling(a, b, pos, span, reverse):
    n = a.shape[0]
    d = 1
    while d < span:
        if reverse:
            keep = pos < span - d
            a_s, b_s = jnp.where(keep, pltpu.roll(a, n - d, 0), 1.0), jnp.where(keep, pltpu.roll(b, n - d, 0), 0.0)
        else:
            keep = pos >= d
            a_s, b_s = jnp.where(keep, pltpu.roll(a, d, 0), 1.0), jnp.where(keep, pltpu.roll(b, d, 0), 0.0)
        b = a * b_s + b
        a = a * a_s
        d *= 2
    return a, b


def _scan(a, b, rows, S, reverse, scr):
    groups = S // SUBLANES
    a, b = _scan_by_doubling(a, b, jnp.bitwise_and(rows, SUBLANES - 1), SUBLANES, reverse)
    scr[0][...] = a
    scr[1][...] = b
    edge = 0 if reverse else SUBLANES - 1
    a_g = scr[0][pl.ds(edge, groups, stride=SUBLANES), :]
    b_g = scr[1][pl.ds(edge, groups, stride=SUBLANES), :]
    g_pos = lax.broadcasted_iota(jnp.int32, (groups, LANES), 0)
    _, h_g = _scan_by_doubling(a_g, b_g, g_pos, groups, reverse)
    if reverse:
        carry = jnp.where(g_pos < groups - 1, pltpu.roll(h_g, groups - 1, 0), 0.0)
    else:
        carry = jnp.where(g_pos >= 1, pltpu.roll(h_g, 1, 0), 0.0)
    for r in range(SUBLANES):
        scr[0][pl.ds(r, groups, stride=SUBLANES), :] = carry
    return b + a * scr[0][...]


def _lru_forward(u, cw, cb, wra, bra, wri, bri, lam, rows, scr):
    uc = cb + cw[CONV_W - 1] * u
    for d in range(1, CONV_W):
        uc = uc + cw[CONV_W - 1 - d] * _shift_down(u, d, rows, 0.0)
    ucb = uc.astype(BF16)
    r = _sigmoid(jnp.dot(ucb, wra.astype(BF16), preferred_element_type=F32) + bra)
    ig = _sigmoid(jnp.dot(ucb, wri.astype(BF16), preferred_element_type=F32) + bri)
    sp = _softplus(-lam)
    log_a = -LRU_C * r * sp
    a = jnp.exp(log_a)
    sq = jnp.sqrt(_neg_expm1(2.0 * log_a))
    iu = ig * uc
    hseq = _scan(a, sq * iu, rows, u.shape[0], False, scr)
    return uc, ucb, r, ig, sp, a, sq, iu, hseq


def _lru_specs(S, n_u, n_g):
    col = lambda off: pl.BlockSpec((S, LANES), lambda cbk: (0, off + cbk))
    vec = pl.BlockSpec((1, LANES), lambda cbk: (0, cbk))
    mat = pl.BlockSpec((None, LANES, LANES), lambda cbk: (cbk, 0, 0))
    cw = pl.BlockSpec((CONV_W, LANES), lambda cbk: (0, cbk))
    return col, vec, mat, cw


def lru_fwd(proj, conv_w, conv_b, w_ra, b_ra, w_ri, b_ri, lam, u_off, g_off):
    S = proj.shape[0]
    nb = w_ra.shape[0]
    col, vec, mat, cws = _lru_specs(S, u_off, g_off)

    def body(u_ref, g_ref, cw_ref, cb_ref, wra_ref, bra_ref, wri_ref, bri_ref, lam_ref, y_ref, scr0, scr1):
        rows = lax.broadcasted_iota(jnp.int32, (S, LANES), 0)
        cw = [cw_ref[t:t + 1, :] for t in range(CONV_W)]
        hseq = _lru_forward(u_ref[...], cw, cb_ref[...], wra_ref[...], bra_ref[...], wri_ref[...],
                            bri_ref[...], lam_ref[...], rows, (scr0, scr1))[-1]
        y_ref[...] = hseq * _gelu_and_grad(g_ref[...])[0]

    return pl.pallas_call(
        body, name='lru_fwd', grid=(nb,),
        in_specs=[col(u_off), col(g_off), cws, vec, mat, vec, mat, vec, vec], out_specs=col(0),
        out_shape=jax.ShapeDtypeStruct((S, nb * LANES), F32),
        scratch_shapes=[pltpu.VMEM((S, LANES), F32), pltpu.VMEM((S, LANES), F32)],
        compiler_params=_params(('parallel',)))(proj, proj, conv_w, conv_b, w_ra, b_ra, w_ri, b_ri, lam)


def lru_bwd(proj, dy, conv_w, conv_b, w_ra, b_ra, w_ri, b_ri, lam, u_off, g_off):
    S = proj.shape[0]
    nb = w_ra.shape[0]
    LW = nb * LANES
    col, vec, mat, cws = _lru_specs(S, u_off, g_off)

    def body(u_ref, g_ref, dy_ref, cw_ref, cb_ref, wra_ref, bra_ref, wri_ref, bri_ref, lam_ref,
             du_ref, dg_ref, dcw_ref, dcb_ref, dwra_ref, dbra_ref, dwri_ref, dbri_ref, dlam_ref, scr0, scr1):
        rows = lax.broadcasted_iota(jnp.int32, (S, LANES), 0)
        u, lam_v = u_ref[...], lam_ref[...]
        cw = [cw_ref[t:t + 1, :] for t in range(CONV_W)]
        wra, wri = wra_ref[...].astype(BF16), wri_ref[...].astype(BF16)
        uc, ucb, r, ig, sp, a, sq, iu, hseq = _lru_forward(u, cw, cb_ref[...], wra, bra_ref[...], wri, bri_ref[...],
                                                           lam_v, rows, (scr0, scr1))
        gl, dgl = _gelu_and_grad(g_ref[...])
        dy_v = dy_ref[...]
        dg_ref[...] = (dy_v * hseq * dgl).astype(BF16)
        G = _scan(_shift_up(a, 1, rows, S, 0.0), dy_v * gl, rows, S, True, (scr0, scr1))
        da = G * _shift_down(hseq, 1, rows, 0.0)
        diu = G * sq
        dsq = G * iu
        dlog_a = da * a - dsq * a * a / jnp.maximum(sq, 1e-30)
        dr = dlog_a * (-LRU_C * sp)
        dsp = jnp.sum(dlog_a * (-LRU_C * r), axis=0, keepdims=True)
        dlam_ref[...] = -dsp * _sigmoid(-lam_v)
        dzr = dr * r * (1.0 - r)
        dzi = diu * uc * ig * (1.0 - ig)
        dzrb, dzib = dzr.astype(BF16), dzi.astype(BF16)
        duc = (diu * ig + lax.dot_general(dzrb, wra, _DIMS['nt'], preferred_element_type=F32)
               + lax.dot_general(dzib, wri, _DIMS['nt'], preferred_element_type=F32))
        dwra_ref[...] = lax.dot_general(ucb, dzrb, _DIMS['tn'], preferred_element_type=F32)
        dwri_ref[...] = lax.dot_general(ucb, dzib, _DIMS['tn'], preferred_element_type=F32)
        dbra_ref[...] = jnp.sum(dzr, axis=0, keepdims=True)
        dbri_ref[...] = jnp.sum(dzi, axis=0, keepdims=True)
        dcb_ref[...] = jnp.sum(duc, axis=0, keepdims=True)
        du = cw[CONV_W - 1] * duc
        dcw_ref[CONV_W - 1:CONV_W, :] = jnp.sum(duc * u, axis=0, keepdims=True)
        for d in range(1, CONV_W):
            du = du + cw[CONV_W - 1 - d] * _shift_up(duc, d, rows, S, 0.0)
            dcw_ref[CONV_W - 1 - d:CONV_W - d, :] = jnp.sum(duc * _shift_down(u, d, rows, 0.0), axis=0, keepdims=True)
        du_ref[...] = du.astype(BF16)

    sd = jax.ShapeDtypeStruct
    return pl.pallas_call(
        body, name='lru_bwd', grid=(nb,),
        in_specs=[col(u_off), col(g_off), col(0), cws, vec, mat, vec, mat, vec, vec],
        out_specs=[col(0), col(0), cws, vec, mat, vec, mat, vec, vec],
        out_shape=[sd((S, LW), BF16), sd((S, LW), BF16), sd((CONV_W, LW), F32), sd((1, LW), F32),
                   sd((nb, LANES, LANES), F32), sd((1, LW), F32), sd((nb, LANES, LANES), F32), sd((1, LW), F32),
                   sd((1, LW), F32)],
        scratch_shapes=[pltpu.VMEM((S, LANES), F32), pltpu.VMEM((S, LANES), F32)],
        compiler_params=_params(('parallel',)))(proj, proj, dy, conv_w, conv_b, w_ra, b_ra, w_ri, b_ri, lam)


def mix_fwd(o_fox, y_lru, g_fox, g_lru):
    S, FW = o_fox.shape
    tr = _tile(S, (256, 128))

    def body(o_ref, y_ref, gf_ref, gl_ref, m_ref):
        m_ref[...] = jnp.concatenate([_rms(o_ref[...], gf_ref[...]), _rms(y_ref[...], gl_ref[...])],
                                     axis=1).astype(BF16)

    return _rows_call('mix_fwd', body, S, tr,
                      [(o_fox, _rb(tr, FW)), (y_lru, _rb(tr, FW)), (g_fox, _fb((1, FW))), (g_lru, _fb((1, FW)))],
                      [((S, 2 * FW), BF16, _rb(tr, 2 * FW))])[0]


def mix_bwd(o_fox, y_lru, g_fox, g_lru, dmix):
    S, FW = o_fox.shape
    H = FW // HEAD_DIM
    tr = _tile(S, (256, 128))

    def body(o_ref, y_ref, gf_ref, gl_ref, df_ref, dl_ref, do_ref, dlt_ref, dy_ref, dgf_ref, dgl_ref):
        o = o_ref[...]
        do, dgf = _rms_bwd(o, gf_ref[...], df_ref[...])
        dyl, dgl = _rms_bwd(y_ref[...], gl_ref[...], dl_ref[...])
        do_ref[...] = do.astype(BF16)
        dy_ref[...] = dyl
        prod = do * o
        for h in range(H):
            dlt_ref[h] = jnp.broadcast_to(
                jnp.sum(prod[:, h * HEAD_DIM:(h + 1) * HEAD_DIM], axis=1, keepdims=True), (tr, LANES))
        first = pl.program_id(0) == 0
        _acc_out(dgf_ref, first, dgf)
        _acc_out(dgl_ref, first, dgl)

    g = _fb((1, FW))
    return _rows_call('mix_bwd', body, S, tr,
                      [(o_fox, _rb(tr, FW)), (y_lru, _rb(tr, FW)), (g_fox, g), (g_lru, g), (dmix, _rb(tr, FW, 0)),
                       (dmix, _rb(tr, FW, 1))],
                      [((S, FW), BF16, _rb(tr, FW)), ((H, S, LANES), F32, pl.BlockSpec((H, tr, LANES), lambda i: (0, i, 0))),
                       ((S, FW), F32, _rb(tr, FW)), ((1, FW), F32, g), ((1, FW), F32, g)])


def _xattn_heads(cq_raw, ckv, g_cq, g_ck, XW):
    out = []
    for h in range(XW // HEAD_DIM):
        sl = slice(h * HEAD_DIM, (h + 1) * HEAD_DIM)
        out.append((cq_raw[:, sl], _rms(cq_raw[:, sl], g_cq), ckv[:, sl], _rms(ckv[:, sl], g_ck),
                    ckv[:, XW + h * HEAD_DIM:XW + (h + 1) * HEAD_DIM].astype(BF16)))
    return out


def xattn_fwd(cq_raw, ckv, g_cq, g_ck):
    S, XW = cq_raw.shape
    M = ckv.shape[0]
    tr = _tile(S, (512, 256, 128))

    def body(q_ref, kv_ref, gq_ref, gk_ref, o_ref):
        outs = []
        for _, qn, _, kn, v in _xattn_heads(q_ref[...], kv_ref[...], gq_ref[...], gk_ref[...], XW):
            s = lax.dot_general(qn.astype(BF16), kn.astype(BF16), _DIMS['nt'], preferred_element_type=F32)
            s = s / math.sqrt(HEAD_DIM)
            p = jnp.exp(s - jnp.max(s, axis=1, keepdims=True))
            p = p / jnp.sum(p, axis=1, keepdims=True)
            outs.append(jnp.dot(p.astype(BF16), v, preferred_element_type=F32))
        o_ref[...] = jnp.concatenate(outs, axis=1).astype(BF16)

    g = _fb((1, HEAD_DIM))
    return _rows_call('xattn_fwd', body, S, tr,
                      [(cq_raw, _rb(tr, XW)), (ckv, _fb((M, 2 * XW))), (g_cq, g), (g_ck, g)],
                      [((S, XW), BF16, _rb(tr, XW))])[0]


def xattn_bwd(cq_raw, ckv, g_cq, g_ck, do):
    S, XW = cq_raw.shape
    M = ckv.shape[0]
    tr = _tile(S, (512, 256, 128))
    n = S // tr

    def body(q_ref, kv_ref, gq_ref, gk_ref, do_ref, dq_ref, dkv_ref, dgq_ref, dgk_ref):
        i = pl.program_id(0)
        do_v = do_ref[...]
        dqs, dkn, dvs = [], [], []
        dgq = jnp.zeros((1, HEAD_DIM), F32)
        for h, (q_raw, qn, _, kn, v) in enumerate(_xattn_heads(q_ref[...], kv_ref[...], gq_ref[...], gk_ref[...], XW)):
            qb, kb = qn.astype(BF16), kn.astype(BF16)
            doh = do_v[:, h * HEAD_DIM:(h + 1) * HEAD_DIM]
            s = lax.dot_general(qb, kb, _DIMS['nt'], preferred_element_type=F32) / math.sqrt(HEAD_DIM)
            p = jnp.exp(s - jnp.max(s, axis=1, keepdims=True))
            p = p / jnp.sum(p, axis=1, keepdims=True)
            dp = lax.dot_general(doh, v, _DIMS['nt'], preferred_element_type=F32)
            ds = (p * (dp - jnp.sum(p * dp, axis=1, keepdims=True)) / math.sqrt(HEAD_DIM)).astype(BF16)
            dvs.append(lax.dot_general(p.astype(BF16), doh, _DIMS['tn'], preferred_element_type=F32))
            dkn.append(lax.dot_general(ds, qb, _DIMS['tn'], preferred_element_type=F32))
            dq, g1 = _rms_bwd(q_raw, gq_ref[...], jnp.dot(ds, kb, preferred_element_type=F32))
            dqs.append(dq)
            dgq = dgq + g1
        dq_ref[...] = jnp.concatenate(dqs, axis=1).astype(BF16)
        first = i == 0
        _acc_out(dgq_ref, first, dgq)
        _acc_out(dkv_ref, first, jnp.concatenate(dkn + dvs, axis=1))

        @pl.when(i == n - 1)
        def _():
            kv = kv_ref[...]
            acc = dkv_ref[...]
            dk, gk = _heads(lambda t, d: _rms_bwd(t, gk_ref[...], d), XW // HEAD_DIM, kv[:, :XW], acc[:, :XW])
            dkv_ref[:, :XW] = dk
            dgk_ref[...] = gk

    g = _fb((1, HEAD_DIM))
    return _rows_call('xattn_bwd', body, S, tr,
                      [(cq_raw, _rb(tr, XW)), (ckv, _fb((M, 2 * XW))), (g_cq, g), (g_ck, g), (do, _rb(tr, XW))],
                      [((S, XW), BF16, _rb(tr, XW)), ((M, 2 * XW), F32, _fb((M, 2 * XW))), ((1, HEAD_DIM), F32, g),
                       ((1, HEAD_DIM), F32, g)])


def gate_up_fwd(hf, w, F):
    S, D = hf.shape
    J, _, Nj = w.shape
    tm = _tile(S, (1024, 512, 256, 128))
    tn = _tile(Nj, (256, 128))
    per = Nj // tn
    half = J // 2 * per

    def body(a_ref, bg_ref, bu_ref, gu_ref, act_ref):
        a = a_ref[...]
        g = jnp.dot(a, bg_ref[...], preferred_element_type=F32)
        u = jnp.dot(a, bu_ref[...], preferred_element_type=F32)
        gu_ref[0] = g
        gu_ref[1] = u
        act_ref[...] = (g * _sigmoid(g) * u).astype(BF16)

    return pl.pallas_call(
        body, name='proj_gate_up', grid=(S // tm, half),
        in_specs=[pl.BlockSpec((tm, D), lambda m, n: (m, 0)),
                  pl.BlockSpec((None, D, tn), lambda m, n: (n // per, 0, n % per)),
                  pl.BlockSpec((None, D, tn), lambda m, n: ((n + half) // per, 0, n % per))],
        out_specs=[pl.BlockSpec((2, tm, tn), lambda m, n: (0, m, n)), pl.BlockSpec((tm, tn), lambda m, n: (m, n))],
        out_shape=[jax.ShapeDtypeStruct((2, S, F), F32), jax.ShapeDtypeStruct((S, F), BF16)],
        compiler_params=_params(('parallel', 'parallel')))(hf, w, w)


def down_bwd_x(dyb, w_down, gu, after):
    S, D = dyb.shape
    F = w_down.shape[0]
    tm = _tile(S, (1024, 512, 256, 128))
    tn = _tile(F, (512, 256, 128))

    def body(a_ref, b_ref, gu_ref, after_ref, o_ref):
        da = lax.dot_general(a_ref[...], b_ref[...], _DIMS['nt'], preferred_element_type=F32)
        g = gu_ref[0]
        sg = _sigmoid(g)
        o_ref[0] = (da * gu_ref[1] * sg * (1.0 + g * (1.0 - sg))).astype(BF16)
        o_ref[1] = (da * g * sg).astype(BF16)

    planes = pl.BlockSpec((2, tm, tn), lambda m, n: (0, m, n))
    return pl.pallas_call(
        body, name='bwd_down_x', grid=(S // tm, F // tn),
        in_specs=[pl.BlockSpec((tm, D), lambda m, n: (m, 0)), pl.BlockSpec((tn, D), lambda m, n: (n, 0)), planes, ANY],
        out_specs=planes, out_shape=jax.ShapeDtypeStruct((2, S, F), BF16),
        compiler_params=_params(('parallel', 'parallel')))(dyb, w_down, gu, after)


def down_fwd_loss(act, w_down, x2, target):
    S, F = act.shape
    D = w_down.shape[1]
    tm, tn, tk = _mm_tiles(S, D, F, F, act, w_down, F32, x2, tn_cands=(512, 256, 128))
    nk = F // tk

    def body(a_ref, b_ref, x_ref, t_ref, d_ref, db_ref, l_ref, acc):
        m, n, k = pl.program_id(0), pl.program_id(1), pl.program_id(2)
        part = jnp.dot(a_ref[...], b_ref[...], preferred_element_type=F32)

        @pl.when(k == 0)
        def _():
            acc[...] = part

        @pl.when(k > 0)
        def _():
            acc[...] += part

        @pl.when(k == nk - 1)
        def _():
            err = acc[...] + x_ref[...] - t_ref[...]
            d = err * (1.0 / D)
            d_ref[...] = d
            db_ref[...] = d.astype(BF16)
            tot = jnp.sum(jnp.sum(err * err, axis=1, keepdims=True), axis=0, keepdims=True) * (0.5 / D)
            _acc_out(l_ref, jnp.logical_and(m == 0, n == 0), jnp.broadcast_to(tot, (1, LANES)))

    tile = pl.BlockSpec((tm, tn), lambda m, n, k: (m, n))
    return pl.pallas_call(
        body, name='proj_down', grid=(S // tm, D // tn, nk),
        in_specs=[pl.BlockSpec((tm, tk), lambda m, n, k: (m, k)), pl.BlockSpec((tk, tn), lambda m, n, k: (k, n)), tile, tile],
        out_specs=[tile, tile, pl.BlockSpec((1, LANES), lambda m, n, k: (0, 0))],
        out_shape=[jax.ShapeDtypeStruct((S, D), F32), jax.ShapeDtypeStruct((S, D), BF16),
                   jax.ShapeDtypeStruct((1, LANES), F32)],
        scratch_shapes=[pltpu.VMEM((tm, tn), F32)],
        compiler_params=_params(('arbitrary', 'arbitrary', 'arbitrary')))(act, w_down, x2, target)


def _adamw_math(w, gv, m, v):
    mn = ADAM_B1 * m + (1.0 - ADAM_B1) * gv
    vn = ADAM_B2 * v + (1.0 - ADAM_B2) * (gv * gv)
    m_hat = mn / (1.0 - ADAM_B1 ** ADAM_STEP)
    v_hat = vn / (1.0 - ADAM_B2 ** ADAM_STEP)
    return -ADAM_LR * (m_hat / (jnp.sqrt(v_hat) + ADAM_EPS) + ADAM_WD * w), mn, vn


def adamw(name, w, g, m, v):
    R, C = w.shape
    tr = _row_tile(R, C)

    def body(w_ref, g_ref, m_ref, v_ref, d_ref, mo_ref, vo_ref):
        d_ref[...], mo_ref[...], vo_ref[...] = _adamw_math(w_ref[...], g_ref[...], m_ref[...], v_ref[...])

    spec = _rb(tr, C)
    return _rows_call(name, body, R, tr, [(w, spec), (g, spec), (m, spec), (v, spec)], [((R, C), F32, spec)] * 3)


def adamw_halves(name, w, mine, other, m, v, c_idx):
    R, C = w.shape
    hr = R // 2
    tr = _row_tile(hr, C)

    def body(c_ref, w_ref, a_ref, b_ref, m_ref, v_ref, g_ref, d_ref, mo_ref, vo_ref):
        gv = jnp.where(pl.program_id(0) == c_ref[0], a_ref[...], b_ref[...])
        g_ref[...] = gv
        d_ref[...], mo_ref[...], vo_ref[...] = _adamw_math(w_ref[...], gv, m_ref[...], v_ref[...])

    full = pl.BlockSpec((None, tr, C), lambda hh, i, c_ref: (hh, i, 0))
    mine_spec = pl.BlockSpec((tr, C), lambda hh, i, c_ref: (jnp.where(hh == c_ref[0], i, 0), 0))
    other_spec = pl.BlockSpec((tr, C), lambda hh, i, c_ref: (jnp.where(hh == c_ref[0], 0, i), 0))
    outs = pl.pallas_call(
        body, name=name,
        grid_spec=pltpu.PrefetchScalarGridSpec(num_scalar_prefetch=1, grid=(2, hr // tr),
                                               in_specs=[full, mine_spec, other_spec, full, full], out_specs=[full] * 4),
        out_shape=[jax.ShapeDtypeStruct((2, hr, C), F32)] * 4,
        compiler_params=_params(('parallel', 'parallel')))(
            c_idx, w.reshape(2, hr, C), mine, other, m.reshape(2, hr, C), v.reshape(2, hr, C))
    return [o.reshape(R, C) for o in outs]


def _place():
    x, y, c = lax.axis_index('x'), lax.axis_index('y'), lax.axis_index('c')
    return x, y, c, [(1 - x, y), (x, 1 - y), (1 - x, 1 - y)]


def _rcopy(src, dst, ssem, rsem, dev):
    return pltpu.make_async_remote_copy(src_ref=src, dst_ref=dst, send_sem=ssem, recv_sem=rsem, device_id=dev,
                                        device_id_type=MESH)


HBM = pl.BlockSpec(memory_space=pltpu.HBM)
SEM = pl.BlockSpec(memory_space=pltpu.SEMAPHORE)
EFFECT = pltpu.SideEffectType.DATAFLOW_SIDE_EFFECTING


def _in_hbm(a):
    return pltpu.with_memory_space_constraint(a, pltpu.HBM)


def _rows_part(shape, whole, half):
    return pl.ds(0, shape[0]) if whole else pl.ds(half * (shape[0] // 2), shape[0] // 2)


def gather_start(name, shards, whole):
    nT = len(shards)

    def body(*refs):
        srcs, lands = refs[:nT], refs[nT:2 * nT]
        ssem, rsem, token = refs[2 * nT], refs[2 * nT + 1], refs[-1]
        x, y, c, chips = _place()
        for t in range(nT):
            rows = _rows_part(shards[t].shape, whole[t], c)
            for k, (px, py) in enumerate(chips):
                _rcopy(srcs[t].at[rows], lands[t].at[2 * x + y, rows], ssem.at[3 * t + k], rsem.at[3 * t + k],
                       (px, py, c)).start()
        token[...] = jnp.zeros_like(token)

    zones = [lax.empty((N_CHIPS,) + s.shape, s.dtype) for s in shards]
    outs = pl.pallas_call(
        body, name=name,
        out_shape=(pltpu.SemaphoreType.DMA((3 * nT,)), pltpu.SemaphoreType.DMA((3 * nT,)),
                   *[pltpu.HBM(s.shape, s.dtype) for s in shards], *[pltpu.HBM(z.shape, z.dtype) for z in zones],
                   jax.ShapeDtypeStruct((8, LANES), F32)),
        in_specs=[HBM] * (2 * nT), out_specs=(SEM, SEM, *[HBM] * (2 * nT), pl.BlockSpec(memory_space=pltpu.VMEM)),
        input_output_aliases={i: 2 + i for i in range(2 * nT)},
        compiler_params=pltpu.CompilerParams(has_side_effects=EFFECT))(*[_in_hbm(a) for a in list(shards) + zones])
    return outs[0], outs[1], outs[2:2 + nT], outs[2 + nT:2 + 2 * nT], outs[-1]


def gather_wait(name, t, shard, zone, ssem, rsem, after, whole):
    after = after if isinstance(after, (list, tuple)) else [after]

    def body(src_ref, land_ref, ssem_ref, rsem_ref, *rest):
        x, y, c, chips = _place()
        rows = _rows_part(shard.shape, whole, c)
        for k, (px, py) in enumerate(chips):
            cp = _rcopy(src_ref.at[rows], land_ref.at[2 * px + py, rows], ssem_ref.at[3 * t + k], rsem_ref.at[3 * t + k],
                        (px, py, c))
            cp.wait_send()
            cp.wait_recv()

    return pl.pallas_call(
        body, name=name, out_shape=(pltpu.HBM(shard.shape, shard.dtype), pltpu.HBM(zone.shape, zone.dtype)),
        in_specs=(HBM, HBM, SEM, SEM, *[ANY] * len(after)), out_specs=(HBM, HBM), input_output_aliases={0: 0, 1: 1},
        compiler_params=pltpu.CompilerParams(has_side_effects=EFFECT))(shard, zone, ssem, rsem, *after)


def pair_swap(name, zone):
    hr = zone.shape[1] // 2

    def body(z_in, z_ref, ssem, rsem):
        x, y, c, chips = _place()
        cps = []
        for k, (px, py) in enumerate(chips):
            blk = z_ref.at[2 * px + py, pl.ds(c * hr, hr)]
            cps.append(_rcopy(blk, blk, ssem.at[k], rsem.at[k], (x, y, 1 - c)))
            cps[-1].start()
        for k, (px, py) in enumerate(chips):
            blk = z_ref.at[2 * px + py, pl.ds((1 - c) * hr, hr)]
            _rcopy(blk, blk, ssem.at[k], rsem.at[k], (x, y, 1 - c)).wait_recv()
        for cp in cps:
            cp.wait_send()

    return pl.pallas_call(
        body, name=name, in_specs=[ANY], out_specs=ANY, out_shape=jax.ShapeDtypeStruct(zone.shape, zone.dtype),
        input_output_aliases={0: 0},
        scratch_shapes=[pltpu.SemaphoreType.DMA((3,)), pltpu.SemaphoreType.DMA((3,))],
        compiler_params=_params())(zone)


def _swap_copies(z_ref, ssem, rsem):
    hr = z_ref.shape[1] // 2
    x, y, c, chips = _place()
    pairs = []
    for k, (px, py) in enumerate(chips):
        mine = z_ref.at[2 * px + py, pl.ds(c * hr, hr)]
        theirs = z_ref.at[2 * px + py, pl.ds((1 - c) * hr, hr)]
        pairs.append((_rcopy(mine, mine, ssem.at[k], rsem.at[k], (x, y, 1 - c)),
                      _rcopy(theirs, theirs, ssem.at[k], rsem.at[k], (x, y, 1 - c))))
    return pairs


def swap_start(name, zone):
    def body(z_ref, ssem, rsem, z_out, token):
        for mine, _ in _swap_copies(z_ref, ssem, rsem):
            mine.start()
        token[...] = jnp.zeros_like(token)

    return pl.pallas_call(
        body, name=name,
        out_shape=(pltpu.SemaphoreType.DMA((3,)), pltpu.SemaphoreType.DMA((3,)), pltpu.HBM(zone.shape, zone.dtype),
                   jax.ShapeDtypeStruct((8, LANES), F32)),
        in_specs=[HBM], out_specs=(SEM, SEM, HBM, pl.BlockSpec(memory_space=pltpu.VMEM)), input_output_aliases={0: 2},
        compiler_params=pltpu.CompilerParams(has_side_effects=EFFECT))(_in_hbm(zone))


def swap_wait(name, zone, ssem, rsem, after):
    def body(z_ref, ssem_ref, rsem_ref, after_ref, z_out):
        for mine, theirs in _swap_copies(z_ref, ssem_ref, rsem_ref):
            mine.wait_send()
            theirs.wait_recv()

    return pl.pallas_call(
        body, name=name, out_shape=(pltpu.HBM(zone.shape, zone.dtype),),
        in_specs=(HBM, SEM, SEM, ANY), out_specs=(HBM,), input_output_aliases={0: 0},
        compiler_params=pltpu.CompilerParams(has_side_effects=EFFECT))(zone, ssem, rsem, after)[0]


N_SENDERS = 7


def _scatter_copies(g_ref, l_ref, ssem, rsem):
    x, y, c, chips = _place()
    cps = []
    for k, (px, py) in enumerate(chips):
        for d in range(2):
            to = (c + d) % 2
            cps.append(_rcopy(g_ref.at[2 * px + py, to], l_ref.at[2 * k + d], ssem.at[2 * k + d], rsem.at[2 * k + d],
                              (px, py, to)))
    cps.append(_rcopy(g_ref.at[2 * x + y, 1 - c], l_ref.at[6], ssem.at[6], rsem.at[6], (x, y, 1 - c)))
    return cps


def scatter_start(name, g):
    def body(g_ref, l_ref, ssem, rsem, g_out, l_out, token):
        for cp in _scatter_copies(g_ref, l_ref, ssem, rsem):
            cp.start()
        token[...] = jnp.zeros_like(token)

    zone = lax.empty((N_SENDERS,) + g.shape[2:], g.dtype)
    return pl.pallas_call(
        body, name=name,
        out_shape=(pltpu.SemaphoreType.DMA((N_SENDERS,)), pltpu.SemaphoreType.DMA((N_SENDERS,)),
                   pltpu.HBM(g.shape, g.dtype), pltpu.HBM(zone.shape, zone.dtype), jax.ShapeDtypeStruct((8, LANES), F32)),
        in_specs=[HBM, HBM], out_specs=(SEM, SEM, HBM, HBM, pl.BlockSpec(memory_space=pltpu.VMEM)),
        input_output_aliases={0: 2, 1: 3},
        compiler_params=pltpu.CompilerParams(has_side_effects=EFFECT))(_in_hbm(g), _in_hbm(zone))


def scatter_wait(name, g, zone, ssem, rsem, after):
    def body(g_ref, l_ref, ssem_ref, rsem_ref, after_ref, g_out, l_out):
        for cp in _scatter_copies(g_ref, l_ref, ssem_ref, rsem_ref):
            cp.wait_send()
            cp.wait_recv()

    return pl.pallas_call(
        body, name=name, out_shape=(pltpu.HBM(g.shape, g.dtype), pltpu.HBM(zone.shape, zone.dtype)),
        in_specs=(HBM, HBM, SEM, SEM, ANY), out_specs=(HBM, HBM), input_output_aliases={0: 0, 1: 1},
        compiler_params=pltpu.CompilerParams(has_side_effects=EFFECT))(g, zone, ssem, rsem, after)


def sum_parts(name, g, landed, chip_idx, c_idx):
    hr, C = g.shape[2:]
    tr = _row_tile(hr, C, min_rows=16)

    def body(me_ref, c_ref, g_ref, l_ref, o_ref):
        acc = g_ref[...].astype(F32)
        for s in range(N_SENDERS):
            acc = acc + l_ref[s].astype(F32)
        o_ref[...] = acc

    return pl.pallas_call(
        body, name=name,
        grid_spec=pltpu.PrefetchScalarGridSpec(
            num_scalar_prefetch=2, grid=(hr // tr,),
            in_specs=[pl.BlockSpec((None, None, tr, C), lambda i, me_ref, c_ref: (me_ref[0], c_ref[0], i, 0)),
                      pl.BlockSpec((N_SENDERS, tr, C), lambda i, me_ref, c_ref: (0, i, 0))],
            out_specs=pl.BlockSpec((tr, C), lambda i, me_ref, c_ref: (i, 0))),
        out_shape=jax.ShapeDtypeStruct((hr, C), F32),
        compiler_params=_params(('parallel',)))(chip_idx, c_idx, g, landed)


def pair_join(name, halves):
    nT = len(halves)

    def body(*refs):
        ins, outs = refs[:nT], refs[nT:2 * nT]
        ssem, rsem = refs[2 * nT:]
        x, y, c, _ = _place()
        cps = [_rcopy(ins[t], outs[t], ssem.at[t], rsem.at[t], (x, y, 1 - c)) for t in range(nT)]
        for cp in cps:
            cp.start()
        for cp in cps:
            cp.wait()

    return pl.pallas_call(
        body, name=name, in_specs=[ANY] * nT, out_specs=[ANY] * nT,
        out_shape=[jax.ShapeDtypeStruct(h.shape, h.dtype) for h in halves],
        scratch_shapes=[pltpu.SemaphoreType.DMA((nT,)), pltpu.SemaphoreType.DMA((nT,))],
        compiler_params=_params())(*halves)


N_DEVICES = 8


def _spread_copies(b_ref, l_ref, ssem, rsem):
    x, y, c, chips = _place()
    me = 4 * x + 2 * y + c
    pairs = []
    for px, py, pc in [(px, py, pc) for px, py in chips for pc in (c, 1 - c)] + [(x, y, 1 - c)]:
        it = 4 * px + 2 * py + pc
        pairs.append((_rcopy(b_ref, l_ref.at[me], ssem.at[it], rsem.at[me], (px, py, pc)),
                      _rcopy(b_ref, l_ref.at[it], ssem.at[it], rsem.at[it], (px, py, pc))))
    return pairs


def spread_start(name, buf):
    def body(b_ref, l_ref, ssem, rsem, b_out, l_out, token):
        for mine, _ in _spread_copies(b_ref, l_ref, ssem, rsem):
            mine.start()
        token[...] = jnp.zeros_like(token)

    zone = lax.empty((N_DEVICES,) + buf.shape, buf.dtype)
    return pl.pallas_call(
        body, name=name,
        out_shape=(pltpu.SemaphoreType.DMA((N_DEVICES,)), pltpu.SemaphoreType.DMA((N_DEVICES,)),
                   pltpu.HBM(buf.shape, buf.dtype), pltpu.HBM(zone.shape, zone.dtype), jax.ShapeDtypeStruct((8, LANES), F32)),
        in_specs=[HBM, HBM], out_specs=(SEM, SEM, HBM, HBM, pl.BlockSpec(memory_space=pltpu.VMEM)),
        input_output_aliases={0: 2, 1: 3},
        compiler_params=pltpu.CompilerParams(has_side_effects=EFFECT))(_in_hbm(buf), _in_hbm(zone))


def spread_wait(name, buf, zone, ssem, rsem, after):
    def body(b_ref, l_ref, ssem_ref, rsem_ref, after_ref, b_out, l_out):
        for mine, theirs in _spread_copies(b_ref, l_ref, ssem_ref, rsem_ref):
            mine.wait_send()
            theirs.wait_recv()

    return pl.pallas_call(
        body, name=name, out_shape=(pltpu.HBM(buf.shape, buf.dtype), pltpu.HBM(zone.shape, zone.dtype)),
        in_specs=(HBM, HBM, SEM, SEM, ANY), out_specs=(HBM, HBM), input_output_aliases={0: 0, 1: 1},
        compiler_params=pltpu.CompilerParams(has_side_effects=EFFECT))(buf, zone, ssem, rsem, after)


def sum_devices(name, zone):
    _, R, C = zone.shape
    tr = _row_tile(R, C)

    def body(z_ref, o_ref):
        acc = z_ref[0]
        for d in range(1, N_DEVICES):
            acc = acc + z_ref[d]
        o_ref[...] = acc

    return pl.pallas_call(
        body, name=name, grid=(R // tr,), in_specs=[pl.BlockSpec((N_DEVICES, tr, C), lambda i: (0, i, 0))],
        out_specs=pl.BlockSpec((tr, C), lambda i: (i, 0)), out_shape=jax.ShapeDtypeStruct((R, C), F32),
        compiler_params=_params(('parallel',)))(zone)


class _InWindows:
    def __init__(self, FW, LW, H, C):
        gap = LANES - H
        padded = lambda o: o if o < 3 * FW + H else o + gap
        self.width = 3 * FW + LANES + 2 * LW
        self.f_block = 3 * FW // LANES
        self.first = [padded(C * j) // LANES for j in range(N_CHIPS)]
        self.blocks = max(padded(C * (j + 1) - 1) // LANES - self.first[j] + 1 for j in range(N_CHIPS))
        assert all((b + self.blocks) * LANES <= self.width for b in self.first)
        self.cols = self.blocks * LANES
        self.runs = []
        for j in range(N_CHIPS):
            cut = min(max(3 * FW + H - C * j, 0), C)
            spans = [(0, cut), (cut, C)]
            self.runs.append([(t0, t1, padded(C * j + t0) - LANES * self.first[j]) for t0, t1 in spans if t1 > t0])

    def to_window(self, shard, chip):
        def place(j, s):
            parts, pos = [], 0
            for t0, t1, w0 in self.runs[j]:
                parts += [jnp.zeros((s.shape[0], w0 - pos), s.dtype), s[:, t0:t1]]
                pos = w0 + t1 - t0
            parts.append(jnp.zeros((s.shape[0], self.cols - pos), s.dtype))
            return jnp.concatenate([p for p in parts if p.shape[1]], axis=1)
        return lax.switch(chip, [functools.partial(place, j) for j in range(N_CHIPS)], shard)

    def from_window(self, win, chip):
        def take(j, w):
            return jnp.concatenate([w[:, w0:w0 + t1 - t0] for t0, t1, w0 in self.runs[j]], axis=1)
        return lax.switch(chip, [functools.partial(take, j) for j in range(N_CHIPS)], win)

    def _spans(self, j):
        b0, b1 = self.first[j], self.first[j] + self.blocks
        return (b0, min(b1, self.f_block)), b0 <= self.f_block < b1, (max(b0, self.f_block + 1), b1)

    def assemble(self, zone):
        main, f_blk = None, None
        for j in range(N_CHIPS):
            (a0, a1), has_f, (c0, c1) = self._spans(j)
            for p0, p1, shift in ((a0, a1, 0), (c0, c1, 1)):
                if p1 > p0:
                    part = zone[j][:, (p0 - self.first[j]) * LANES:(p1 - self.first[j]) * LANES]
                    part = jnp.pad(part, ((0, 0), ((p0 - shift) * LANES, self.width - LANES - (p1 - shift) * LANES)))
                    main = part if main is None else main + part
            if has_f:
                part = zone[j][:, (self.f_block - self.first[j]) * LANES:(self.f_block - self.first[j] + 1) * LANES]
                f_blk = part if f_blk is None else f_blk + part
        return main, f_blk

    def windows(self, main, f_blk):
        out = []
        for j in range(N_CHIPS):
            (a0, a1), has_f, (c0, c1) = self._spans(j)
            parts = [main[:, a0 * LANES:a1 * LANES]] if a1 > a0 else []
            parts += [f_blk] if has_f else []
            parts += [main[:, (c0 - 1) * LANES:(c1 - 1) * LANES]] if c1 > c0 else []
            out.append(jnp.concatenate(parts, axis=1))
        return jnp.stack(out)


_PACK = 8 * LANES


PACK_ROWS = 256


def _pack(arrs):
    flat = []
    for a in arrs:
        v = a.reshape(-1).astype(F32)
        flat.append(jnp.pad(v, (0, (-v.shape[0]) % _PACK)))
    rows = sum(v.shape[0] for v in flat) // LANES
    flat.append(jnp.zeros(((-rows) % PACK_ROWS) * LANES, F32))
    return jnp.concatenate(flat).reshape(-1, LANES)


def _unpack(buf, shapes):
    out, off = [], 0
    flat = buf.reshape(-1)
    for sh in shapes:
        n = math.prod(sh)
        out.append(flat[off:off + n].reshape(sh))
        off += n + (-n) % _PACK
    return out


def kernel(x, mem, g_mix, w_in, b_f, g_q, g_k, conv_w, conv_b, w_ra, b_ra, w_ri, b_ri, lam, g_fox_out, g_lru_out, w_out, g_xattn, g_mem, w_cq, w_ckv, g_cq, g_ck, w_co, g_ffn, w_gate_up, w_down, loss_target, m_g_mix, m_w_in, m_b_f, m_g_q, m_g_k, m_conv_w, m_conv_b, m_w_ra, m_b_ra, m_w_ri, m_b_ri, m_lam, m_g_fox_out, m_g_lru_out, m_w_out, m_g_xattn, m_g_mem, m_w_cq, m_w_ckv, m_g_cq, m_g_ck, m_w_co, m_g_ffn, m_w_gate_up, m_w_down, v_g_mix, v_w_in, v_b_f, v_g_q, v_g_k, v_conv_w, v_conv_b, v_w_ra, v_b_ra, v_w_ri, v_b_ri, v_lam, v_g_fox_out, v_g_lru_out, v_w_out, v_g_xattn, v_g_mem, v_w_cq, v_w_ckv, v_g_cq, v_g_ck, v_w_co, v_g_ffn, v_w_gate_up, v_w_down):
    given = dict(locals())
    W = {n: given[n][0] for n in WEIGHTS}
    M1 = {n: given['m_' + n][0] for n in WEIGHTS}
    V1 = {n: given['v_' + n][0] for n in WEIGHTS}
    xs, ms, tgt = x[0], mem[0], loss_target[0]
    S, D = xs.shape
    H = W['b_f'].shape[0]
    FW = H * HEAD_DIM
    LW = W['lam'].shape[0]
    nb = W['w_ra'].shape[0]
    XW = W['w_cq'].shape[1]
    F = W['w_down'].shape[0] * N_CHIPS
    IN_W = W['w_in'].shape[1] * N_CHIPS
    assert FW == LW and LW == nb * LANES and IN_W == 3 * FW + H + 2 * LW and H <= 8
    T = _tile(S, (512, 256, 128))
    c_idx = lax.axis_index('c').astype(jnp.int32).reshape(1)
    chip = 2 * lax.axis_index('x') + lax.axis_index('y')
    chip_idx = chip.astype(jnp.int32).reshape(1)
    vec = lambda n: W[n].reshape(1, -1)

    wins = _InWindows(FW, LW, H, W['w_in'].shape[1])
    started = {}
    g_tok = jnp.zeros((1, 1), F32)
    for call, names in (('gather_start_first', ['conv_w', 'w_in']), ('gather_start_rest', BIG[1:])):
        own = [W[n].reshape(-1, LANES) if n == 'conv_w' else W[n].astype(BF16) + g_tok.astype(BF16) for n in names]
        own = [wins.to_window(o, chip) if n == 'w_in' else o for n, o in zip(names, own)]
        ssem, rsem, srcs, zones, tok = gather_start(call, own, [n == 'conv_w' for n in names])
        g_tok = tok[0:1, 0:1]
        started.update({n: (t, srcs[t], zones[t], ssem, rsem) for t, n in enumerate(names)})

    def fetch(n, after):
        t, g_src, g_zone, g_ssem, g_rsem = started[n]
        src, zone = gather_wait('gather_wait_' + n, t, g_src, g_zone, g_ssem, g_rsem, after, n == 'conv_w')
        if n != 'conv_w':
            zone = pair_swap('pair_swap_' + n, zone)
        return lax.dynamic_update_index_in_dim(zone, src, chip, 0)

    def fetch_begin(n, after):
        t, g_src, g_zone, g_ssem, g_rsem = started[n]
        src, zone = gather_wait('gather_wait_' + n, t, g_src, g_zone, g_ssem, g_rsem, after, False)
        ssem, rsem, zone, _ = swap_start('swap_start_' + n, zone)
        return src, zone, ssem, rsem

    def fetch_end(n, begun, after):
        src, zone, ssem, rsem = begun
        return lax.dynamic_update_index_in_dim(swap_wait('swap_wait_' + n, zone, ssem, rsem, after), src, chip, 0)

    b_f_pad = jnp.pad(vec('b_f'), ((0, 0), (0, LANES - H)))
    u_off, g_off = 3 * FW // LANES, (3 * FW + LW) // LANES

    h1 = norm_fwd('norm_mix', xs, vec('g_mix') + g_tok[0:1, 0:1])
    conv_full = fetch('conv_w', h1).reshape(N_CHIPS, CONV_W, LW // N_CHIPS).transpose(1, 0, 2).reshape(CONV_W, LW)
    w5, wf = wins.assemble(fetch('w_in', [h1, M1['w_in'], V1['w_in']]))
    proj = _mm('proj_in', h1, w5, 'nn', F32)
    f_raw = _mm('proj_f', h1, wf, 'nn', F32)
    qn, kn, vb = qkv_fwd(proj, vec('g_q'), vec('g_k'), FW)
    cc = fgate_fwd(f_raw, b_f_pad)
    ct = cc[:, :8].T
    o_fox, lse = fox_fwd(qn, kn, vb, cc, ct, T)
    lru_w = (conv_full, vec('conv_b'), W['w_ra'], vec('b_ra'), W['w_ri'], vec('b_ri'), vec('lam'))
    y_lru = lru_fwd(proj, *lru_w, u_off, g_off)
    mixn = mix_fwd(o_fox, y_lru, vec('g_fox_out'), vec('g_lru_out'))
    w_out_f = fetch('w_out', mixn).reshape(2 * FW, D)
    begun = {n: fetch_begin(n, mixn) for n in ('w_cq', 'w_ckv', 'w_co')}
    x1 = _mm('proj_out', mixn, w_out_f, 'nn', F32, res=xs)

    hq = norm_fwd('norm_xq', x1, vec('g_xattn'))
    mn = norm_fwd('norm_mem', ms, vec('g_mem'))
    w_cq_f = fetch_end('w_cq', begun['w_cq'], hq).reshape(D, XW)
    w_ckv_f = fetch_end('w_ckv', begun['w_ckv'], hq).reshape(D, 2 * XW)
    begun['w_gate_up'] = fetch_begin('w_gate_up', hq)
    cq_raw = _mm('proj_cq', hq, w_cq_f, 'nn', F32)
    ckv = _mm('proj_ckv', mn, w_ckv_f, 'nn', F32)
    o_x = xattn_fwd(cq_raw, ckv, vec('g_cq'), vec('g_ck'))
    w_co_g = fetch_end('w_co', begun['w_co'], o_x)
    x2 = _mm_colsharded('proj_co', o_x, w_co_g, F32, res=x1)

    hf = norm_fwd('norm_ffn', x2, vec('g_ffn'))
    begun['w_down'] = fetch_begin('w_down', hf)
    w_gu_g = fetch_end('w_gate_up', begun['w_gate_up'], hf)
    gu, act = gate_up_fwd(hf, w_gu_g, F)
    w_down_f = fetch_end('w_down', begun['w_down'], act).reshape(F, D)
    dy, dyb, loss_blk = down_fwd_loss(act, w_down_f, x2, tgt)

    gw, pending = {}, []

    def reduce_begin(n, g):
        sp = g.reshape(N_CHIPS, 2, g.shape[1] // 2, g.shape[2])
        ssem, rsem, sp, zone, tok = scatter_start('scatter_start_' + n, sp)
        pending.append((n, sp, zone, ssem, rsem))
        return tok[0:1, 0:1]

    t_down = reduce_begin('w_down', _mm('bwd_down_w', act, dyb, 'tn', BF16).reshape(N_CHIPS, F // N_CHIPS, D))
    dgu = down_bwd_x(dyb, w_down_f, gu, t_down)
    dhf = _mm_colsharded_t('bwd_gate_up_x', dgu, w_gu_g, F32)
    t_gu = reduce_begin('w_gate_up', _mm_grad_colsharded('bwd_gate_up_w', hf, dgu, N_CHIPS, BF16))
    dx2, dx2b, gw['g_ffn'] = norm_bwd('norm_ffn_bwd', x2, vec('g_ffn') + t_down + t_gu, dhf, res=dy)

    do_x = _mm_colsharded_t('bwd_co_x', dx2b, w_co_g, BF16)
    t_co = reduce_begin('w_co', _mm_grad_colsharded('bwd_co_w', o_x, dx2b, N_CHIPS, BF16))
    dcq_raw, dckv, gw['g_cq'], gw['g_ck'] = xattn_bwd(cq_raw, ckv, vec('g_cq') + t_co, vec('g_ck'), do_x)
    dhq = _mm('bwd_cq_x', dcq_raw, w_cq_f, 'nt', F32)
    t_cq = reduce_begin('w_cq', _mm('bwd_cq_w', hq, dcq_raw, 'tn', BF16).reshape(N_CHIPS, D // N_CHIPS, XW))
    dmn = _mm('bwd_ckv_x', dckv, w_ckv_f, 'nt', F32)
    t_ckv = reduce_begin('w_ckv', _mm('bwd_ckv_w', mn, dckv, 'tn', BF16).reshape(N_CHIPS, D // N_CHIPS, 2 * XW))
    (gw['g_mem'],) = norm_bwd('norm_mem_bwd', ms, vec('g_mem'), dmn, want_dx=False)
    dx1, dx1b, gw['g_xattn'] = norm_bwd('norm_xq_bwd', x1, vec('g_xattn') + t_cq + t_ckv, dhq, res=dx2)

    dmix = _mm('bwd_out_x', dx1b, w_out_f, 'nt', F32)
    t_out = reduce_begin('w_out', _mm('bwd_out_w', mixn, dx1b, 'tn', BF16).reshape(N_CHIPS, 2 * FW // N_CHIPS, D))
    do_fox, delta, dy_lru, gw['g_fox_out'], gw['g_lru_out'] = mix_bwd(o_fox, y_lru, vec('g_fox_out') + t_out,
                                                                     vec('g_lru_out'), dmix)
    (du, dgate, gw['conv_w'], gw['conv_b'], gw['w_ra'], gw['b_ra'], gw['w_ri'], gw['b_ri'],
     gw['lam']) = lru_bwd(proj, dy_lru, *lru_w, u_off, g_off)
    early = [n for n in SMALL if n not in ('g_q', 'g_k', 'b_f', 'g_mix')]
    late = [n for n in SMALL if n not in early]
    e_ssem, e_rsem, e_buf, e_zone, e_tok = spread_start('spread_start_early', _pack([gw[n] for n in early]))
    dqn, delta2 = fox_bwd_q(qn, kn, vb, do_fox, cc, ct, lse, delta, T)
    dkn, dv, dct = fox_bwd_kv(qn, kn, vb, do_fox, cc, ct, lse, delta2, T)
    dq, dk, gw['g_q'], gw['g_k'] = qkv_bwd(proj, vec('g_q') + e_tok[0:1, 0:1], vec('g_k'), dqn, dkn, FW)
    dc = jnp.pad(dct.reshape(H, S).T, ((0, 0), (0, LANES - H)))
    df, db_f = fgate_bwd(f_raw, b_f_pad, dc, H)
    gw['b_f'] = db_f[:, :H]
    dproj = jnp.concatenate([dq, dk, dv, du, dgate], axis=1)
    dw5 = _mm('bwd_in_w', h1, dproj, 'tn', BF16)
    dwf = _mm('bwd_f_w', h1, df, 'tn', BF16)
    t_in = reduce_begin('w_in', wins.windows(dw5, dwf))
    dh_a = _mm('bwd_f_x', df, wf, 'nt', F32)
    dh1 = _mm('bwd_in_x', dproj, w5, 'nt', F32, res=dh_a)
    grad_x, _, gw['g_mix'] = norm_bwd('norm_mix_bwd', xs, vec('g_mix') + t_in, dh1, res=dx1)
    l_ssem, l_rsem, l_buf, l_zone, _ = spread_start('spread_start_late',
                                                    _pack([gw[n] for n in late] + [loss_blk[0:1, 0:1]]))

    grads, delta_w, new_m, new_v = {}, {}, {}, {}
    done = grad_x
    for n, part, zone, ssem, rsem in pending:
        part, landed = scatter_wait('scatter_wait_' + n, part, zone, ssem, rsem, done)
        mine = sum_parts('sum_parts_' + n, part, landed, chip_idx, c_idx)
        (other,) = pair_join('pair_join_' + n, [mine])
        if n == 'w_in':
            mine, other = wins.from_window(mine, chip), wins.from_window(other, chip)
        grads[n], delta_w[n], new_m[n], new_v[n] = adamw_halves('adamw_' + n, W[n], mine, other, M1[n], V1[n], c_idx)
        done = delta_w[n]

    device = 4 * lax.axis_index('x') + 2 * lax.axis_index('y') + lax.axis_index('c')
    summed = {}
    for tag, names, buf, zone, ssem, rsem in (('early', early, e_buf, e_zone, e_ssem, e_rsem),
                                              ('late', late + ['loss'], l_buf, l_zone, l_ssem, l_rsem)):
        buf, zone = spread_wait('spread_wait_' + tag, buf, zone, ssem, rsem, done)
        total = sum_devices('sum_small_' + tag, lax.dynamic_update_index_in_dim(zone, buf, device, 0))
        summed.update(zip(names, _unpack(total, [gw[n].shape if n != 'loss' else (1, 1) for n in names])))
    loss = summed['loss'].reshape(())
    for n in SMALL:
        g = summed[n]
        grads[n] = g.reshape(W[n].shape) if n != 'conv_w' else lax.dynamic_slice_in_dim(
            g, chip * (LW // N_CHIPS), LW // N_CHIPS, axis=1)
    packs = [_pack([d[n] for n in SMALL]) for d in (W, grads, M1, V1)]
    shapes = [W[n].shape for n in SMALL]
    for d, res in zip((delta_w, new_m, new_v), adamw('adamw_small', *packs)):
        d.update(zip(SMALL, _unpack(res, shapes)))

    lead = lambda d: [d[n][None] for n in WEIGHTS]
    return (loss, grad_x[None], *lead(grads), *lead(delta_w), *lead(new_m), *lead(new_v))
```

```python
import functools
import math

import jax
import jax.numpy as jnp
from jax import lax
from jax.experimental import pallas as pl
from jax.experimental.pallas import tpu as pltpu

F32 = jnp.float32
BF16 = jnp.bfloat16
HEAD_DIM = 128
LANES = 128
LRU_C = 8.0
RMS_EPS = 1e-6
CONV_W = 4
ADAM_LR = 0.001
ADAM_B1 = 0.9
ADAM_B2 = 0.999
ADAM_EPS = 1e-08
ADAM_WD = 0.01
ADAM_STEP = 10
VMEM_LIMIT = 56 * 1024 * 1024
N_CHIPS = 4
MESH = pl.DeviceIdType.MESH
ANY = pl.BlockSpec(memory_space=pl.ANY)

WEIGHTS = ['g_mix', 'w_in', 'b_f', 'g_q', 'g_k', 'conv_w', 'conv_b', 'w_ra', 'b_ra', 'w_ri', 'b_ri', 'lam',
           'g_fox_out', 'g_lru_out', 'w_out', 'g_xattn', 'g_mem', 'w_cq', 'w_ckv', 'g_cq', 'g_ck', 'w_co', 'g_ffn',
           'w_gate_up', 'w_down']
BIG = ['w_in', 'w_out', 'w_cq', 'w_ckv', 'w_co', 'w_gate_up', 'w_down']
SMALL = [n for n in WEIGHTS if n not in BIG]


def _params(sem=None):
    if sem is None:
        return pltpu.CompilerParams(vmem_limit_bytes=VMEM_LIMIT)
    return pltpu.CompilerParams(dimension_semantics=sem, vmem_limit_bytes=VMEM_LIMIT)


def _tile(n, cands):
    for t in cands:
        if n % t == 0:
            return t
    return n


ROW_BLOCK_BYTES = 1 << 20


def _row_tile(n_rows, n_cols, min_rows=8):
    cands = [t for t in (512, 256, 128, 64, 32, 16, 8) if t >= min_rows and t * n_cols * 4 <= ROW_BLOCK_BYTES]
    return _tile(n_rows, cands or [min_rows])


def _sigmoid(z):
    return 1.0 / (1.0 + jnp.exp(-z))


def _softplus(z):
    return jnp.maximum(z, 0.0) + jnp.log(1.0 + jnp.exp(-jnp.abs(z)))


def _neg_expm1(z):
    series = -z * (1.0 + z * (0.5 + z * (1.0 / 6.0 + z * (1.0 / 24.0 + z * (1.0 / 120.0)))))
    return jnp.where(z > -0.25, series, 1.0 - jnp.exp(z))


_GELU_K = math.sqrt(2.0 / math.pi)


def _gelu_and_grad(z):
    inner = _GELU_K * (z + 0.044715 * z * z * z)
    t = jnp.tanh(inner)
    g = 0.5 * z * (1.0 + t)
    dg = 0.5 * (1.0 + t) + 0.5 * z * (1.0 - t * t) * _GELU_K * (1.0 + 3.0 * 0.044715 * z * z)
    return g, dg


def _rms(xv, g):
    r = lax.rsqrt(jnp.mean(xv * xv, axis=-1, keepdims=True) + RMS_EPS)
    return xv * r * g


def _rms_bwd(xv, g, dy):
    r = lax.rsqrt(jnp.mean(xv * xv, axis=-1, keepdims=True) + RMS_EPS)
    xh = xv * r
    dyg = dy * g
    dx = r * (dyg - xh * jnp.mean(dyg * xh, axis=-1, keepdims=True))
    return dx, jnp.sum(dy * xh, axis=0, keepdims=True)


def _heads(fn, n_heads, *arrs):
    outs = [fn(*[a[:, h * HEAD_DIM:(h + 1) * HEAD_DIM] for a in arrs]) for h in range(n_heads)]
    first = jnp.concatenate([o[0] for o in outs], axis=1) if n_heads > 1 else outs[0][0]
    rest = [functools.reduce(lambda p, q: p + q, [o[i] for o in outs]) for i in range(1, len(outs[0]))]
    return (first, *rest)


def _split3(v):
    hi = v.astype(BF16)
    r1 = v - hi.astype(F32)
    mid = r1.astype(BF16)
    lo = (r1 - mid.astype(F32)).astype(BF16)
    return hi, mid, lo


def _acc_out(ref, first, val):
    @pl.when(first)
    def _():
        ref[...] = val

    @pl.when(jnp.logical_not(first))
    def _():
        ref[...] += val


_DIMS = {'nn': (((1,), (0,)), ((), ())), 'nt': (((1,), (1,)), ((), ())), 'tn': (((0,), (0,)), ((), ()))}


MM_VMEM_BYTES = 36 * 1024 * 1024


MXU_FLOPS = 800e12
HBM_BYTES_S = 3.2e12
VMEM_ADD_BYTES_S = 8e12
STEP_S = 0.35e-6


def _k_tile(K, tm, tn, a, b, o_dtype, res):
    fixed = tm * tn * (2 * jnp.dtype(o_dtype).itemsize + 4 + (8 if res is not None else 0))
    per_k = 2 * (tm * a.dtype.itemsize + tn * b.dtype.itemsize)
    per_k += 2 * tm * (a.dtype.itemsize > 2) + 2 * tn * (b.dtype.itemsize > 2)
    units = K // LANES
    for d in sorted((d for d in range(1, units + 1) if units % d == 0), reverse=True):
        if fixed + d * LANES * per_k <= MM_VMEM_BYTES:
            return d * LANES
    return None


def _mm_tiles(M, N, K, k_span, a, b, o_dtype, res, tn_cands=(2048, 1024, 512, 256, 128)):
    best = None
    for tm in (2048, 1024, 512, 256, 128):
        for tn in tn_cands:
            if M % tm or N % tn:
                continue
            tk = _k_tile(k_span, tm, tn, a, b, o_dtype, res)
            if tk is None:
                continue
            nk = K // tk
            traffic = (M * K * a.dtype.itemsize * (N // tn) + K * N * b.dtype.itemsize * (M // tm)
                       + M * N * (jnp.dtype(o_dtype).itemsize + (4 if res is not None else 0)))
            work = 2.0 * M * N * K / MXU_FLOPS + (M * N * 4 * nk / VMEM_ADD_BYTES_S if nk > 1 else 0.0)
            t = max(work, traffic / HBM_BYTES_S) + (M // tm) * (N // tn) * nk * STEP_S
            if best is None or t < best[0]:
                best = (t, tm, tn, tk)
    assert best is not None, (M, N, K)
    return best[1:]


def _mm_call(name, a, b, mode, grid, a_spec, b_spec, o_spec, o_shape, o_dtype, acc_shape, res=None):
    nk = grid[2]
    dn = _DIMS[mode]

    def body(*refs):
        a_ref, b_ref = refs[:2]
        r_ref = refs[2] if res is not None else None
        o_ref = refs[3] if res is not None else refs[2]
        part = lax.dot_general(a_ref[...].astype(BF16), b_ref[...].astype(BF16), dn, preferred_element_type=F32)

        def finish(r):
            if r_ref is not None:
                r = r + r_ref[...]
            o_ref[...] = r.astype(o_dtype)

        if nk == 1:
            finish(part)
            return
        acc = refs[-1]
        k = pl.program_id(2)

        @pl.when(k == 0)
        def _():
            acc[...] = part

        @pl.when(k > 0)
        def _():
            acc[...] += part

        @pl.when(k == nk - 1)
        def _():
            finish(acc[...])

    ins = [a, b] + ([] if res is None else [res])
    specs = [a_spec, b_spec] + ([] if res is None else [o_spec])
    return pl.pallas_call(
        body, name=name, grid=grid, in_specs=specs, out_specs=o_spec,
        out_shape=jax.ShapeDtypeStruct(o_shape, o_dtype),
        scratch_shapes=[] if nk == 1 else [pltpu.VMEM(acc_shape, F32)],
        compiler_params=_params(('parallel', 'parallel', 'arbitrary')))(*ins)


def _mm(name, a, b, mode, o_dtype, res=None):
    if mode == 'tn':
        K, M = a.shape
    else:
        M, K = a.shape
    N = b.shape[0] if mode == 'nt' else b.shape[1]
    tm, tn, tk = _mm_tiles(M, N, K, K, a, b, o_dtype, res)
    a_spec = (pl.BlockSpec((tk, tm), lambda m, n, k: (k, m)) if mode == 'tn'
              else pl.BlockSpec((tm, tk), lambda m, n, k: (m, k)))
    b_spec = (pl.BlockSpec((tn, tk), lambda m, n, k: (n, k)) if mode == 'nt'
              else pl.BlockSpec((tk, tn), lambda m, n, k: (k, n)))
    o_spec = pl.BlockSpec((tm, tn), lambda m, n, k: (m, n))
    return _mm_call(name, a, b, mode, (M // tm, N // tn, K // tk), a_spec, b_spec, o_spec, (M, N), o_dtype,
                    (tm, tn), res)


def _mm_colsharded(name, a, w, o_dtype, res=None):
    M, K = a.shape
    J, _, Nj = w.shape
    tm, tn, tk = _mm_tiles(M, J * Nj, K, K, a, w, o_dtype, res,
                           tn_cands=[t for t in (2816, 1408, 1024, 512, 256, 128) if Nj % t == 0])
    per = Nj // tn
    return _mm_call(name, a, w, 'nn', (M // tm, J * per, K // tk),
                    pl.BlockSpec((tm, tk), lambda m, n, k: (m, k)),
                    pl.BlockSpec((None, tk, tn), lambda m, n, k: (n // per, k, n % per)),
                    pl.BlockSpec((tm, tn), lambda m, n, k: (m, n)), (M, J * Nj), o_dtype, (tm, tn), res)


def _planes_spec(arr, rows, cols, row_of, col_of):
    if arr.ndim == 2:
        return pl.BlockSpec((rows, cols), lambda m, n, k: (row_of(m, n, k), col_of(m, n, k)))
    per_plane = arr.shape[2] // cols
    return pl.BlockSpec((None, rows, cols),
                        lambda m, n, k: (col_of(m, n, k) // per_plane, row_of(m, n, k), col_of(m, n, k) % per_plane))


def _mm_colsharded_t(name, a, w, o_dtype):
    M = a.shape[-2]
    J, K, Nj = w.shape
    tm, tn, tk = _mm_tiles(M, K, J * Nj, Nj, a, w, o_dtype, None)
    per = Nj // tk
    return _mm_call(name, a, w, 'nt', (M // tm, K // tn, J * per),
                    _planes_spec(a, tm, tk, lambda m, n, k: m, lambda m, n, k: k),
                    pl.BlockSpec((None, tn, tk), lambda m, n, k: (k // per, n, k % per)),
                    pl.BlockSpec((tm, tn), lambda m, n, k: (m, n)), (M, K), o_dtype, (tm, tn))


def _mm_grad_colsharded(name, a, dy, J, o_dtype):
    S, M = a.shape
    Nj = dy.shape[-1] * (dy.shape[0] if dy.ndim == 3 else 1) // J
    tm, tn, tk = _mm_tiles(M, J * Nj, S, S, a, dy, o_dtype, None,
                           tn_cands=[t for t in (2816, 1408, 1024, 512, 256, 128) if Nj % t == 0])
    per = Nj // tn
    return _mm_call(name, a, dy, 'tn', (M // tm, J * per, S // tk),
                    pl.BlockSpec((tk, tm), lambda m, n, k: (k, m)),
                    _planes_spec(dy, tk, tn, lambda m, n, k: k, lambda m, n, k: n),
                    pl.BlockSpec((None, tm, tn), lambda m, n, k: (n // per, m, n % per)), (J, M, Nj), o_dtype, (tm, tn))


def _rows_call(name, body, n_rows, tr, ins, outs):
    return pl.pallas_call(
        body, name=name, grid=(n_rows // tr,), in_specs=[s for _, s in ins], out_specs=[s for _, _, s in outs],
        out_shape=[jax.ShapeDtypeStruct(sh, dt) for sh, dt, _ in outs],
        compiler_params=_params(('arbitrary',)))(*[a for a, _ in ins])


def _rb(tr, w, cb=0):
    return pl.BlockSpec((tr, w), lambda i: (i, cb))


def _fb(shape):
    nd = len(shape)
    return pl.BlockSpec(shape, lambda i: (0,) * nd)


def norm_fwd(name, xv, g):
    S, D = xv.shape
    tr = _tile(S, (256, 128))

    def body(x_ref, g_ref, o_ref):
        o_ref[...] = _rms(x_ref[...], g_ref[...]).astype(BF16)

    return _rows_call(name, body, S, tr, [(xv, _rb(tr, D)), (g, _fb((1, D)))], [((S, D), BF16, _rb(tr, D))])[0]


def norm_bwd(name, xv, g, dy, res=None, want_dx=True):
    S, D = xv.shape
    tr = _tile(S, (256, 128))

    def body(*refs):
        if res is None:
            x_ref, g_ref, dy_ref = refs[:3]
            outs = refs[3:]
            r_ref = None
        else:
            x_ref, g_ref, dy_ref, r_ref = refs[:4]
            outs = refs[4:]
        dx, dg = _rms_bwd(x_ref[...], g_ref[...], dy_ref[...])
        if r_ref is not None:
            dx = dx + r_ref[...]
        if want_dx:
            outs[0][...] = dx
            outs[1][...] = dx.astype(BF16)
        _acc_out(outs[-1], pl.program_id(0) == 0, dg)

    ins = [(xv, _rb(tr, D)), (g, _fb((1, D))), (dy, _rb(tr, D))] + ([] if res is None else [(res, _rb(tr, D))])
    outs = ([((S, D), F32, _rb(tr, D)), ((S, D), BF16, _rb(tr, D))] if want_dx else []) + [((1, D), F32, _fb((1, D)))]
    return _rows_call(name, body, S, tr, ins, outs)


def qkv_fwd(proj, g_q, g_k, FW):
    S = proj.shape[0]
    H = FW // HEAD_DIM
    tr = _tile(S, (256, 128))

    def body(q_ref, k_ref, v_ref, gq_ref, gk_ref, qo, ko, vo):
        qo[...] = _heads(lambda t: (_rms(t, gq_ref[...]),), H, q_ref[...])[0].astype(BF16)
        ko[...] = _heads(lambda t: (_rms(t, gk_ref[...]),), H, k_ref[...])[0].astype(BF16)
        vo[...] = v_ref[...].astype(BF16)

    o = ((S, FW), BF16, _rb(tr, FW))
    return _rows_call('qkv_fwd', body, S, tr,
                      [(proj, _rb(tr, FW, 0)), (proj, _rb(tr, FW, 1)), (proj, _rb(tr, FW, 2)),
                       (g_q, _fb((1, HEAD_DIM))), (g_k, _fb((1, HEAD_DIM)))], [o, o, o])


def qkv_bwd(proj, g_q, g_k, dqn, dkn, FW):
    S = proj.shape[0]
    H = FW // HEAD_DIM
    tr = _tile(S, (256, 128))

    def body(q_ref, k_ref, gq_ref, gk_ref, dq_ref, dk_ref, dqo, dko, dgq, dgk):
        dq, gq = _heads(lambda t, d: _rms_bwd(t, gq_ref[...], d), H, q_ref[...], dq_ref[...])
        dk, gk = _heads(lambda t, d: _rms_bwd(t, gk_ref[...], d), H, k_ref[...], dk_ref[...])
        dqo[...] = dq.astype(BF16)
        dko[...] = dk.astype(BF16)
        first = pl.program_id(0) == 0
        _acc_out(dgq, first, gq)
        _acc_out(dgk, first, gk)

    o = ((S, FW), BF16, _rb(tr, FW))
    og = ((1, HEAD_DIM), F32, _fb((1, HEAD_DIM)))
    return _rows_call('qkv_bwd', body, S, tr,
                      [(proj, _rb(tr, FW, 0)), (proj, _rb(tr, FW, 1)), (g_q, _fb((1, HEAD_DIM))),
                       (g_k, _fb((1, HEAD_DIM))), (dqn, _rb(tr, FW)), (dkn, _rb(tr, FW))], [o, o, og, og])


def _tri(n, upper):
    r = lax.broadcasted_iota(jnp.int32, (n, n), 0)
    c = lax.broadcasted_iota(jnp.int32, (n, n), 1)
    return jnp.where((c >= r) if upper else (c <= r), 1.0, 0.0).astype(BF16)


def _blocked_cumsum(val, S, blk, reverse):
    tri = _tri(blk, reverse)
    order = range(S // blk - 1, -1, -1) if reverse else range(S // blk)
    carry = jnp.zeros((1, LANES), F32)
    outs = {}
    for bi in order:
        part = val[bi * blk:(bi + 1) * blk]
        acc = carry
        for piece in _split3(part):
            acc = acc + jnp.dot(tri, piece, preferred_element_type=F32)
        outs[bi] = acc
        carry = carry + jnp.sum(part, axis=0, keepdims=True)
    return jnp.concatenate([outs[bi] for bi in range(S // blk)], axis=0)


def fgate_fwd(f_raw, b_f_pad):
    S = f_raw.shape[0]
    blk = _tile(S, (256, 128))

    def body(f_ref, b_ref, c_ref):
        z = f_ref[...] + b_ref[...]
        c_ref[...] = _blocked_cumsum(-_softplus(-z), S, blk, False)

    return pl.pallas_call(body, name='fgate_fwd', grid=(1,), in_specs=[_fb((S, LANES)), _fb((1, LANES))],
                          out_specs=_fb((S, LANES)), out_shape=jax.ShapeDtypeStruct((S, LANES), F32),
                          compiler_params=_params(('arbitrary',)))(f_raw, b_f_pad)


def fgate_bwd(f_raw, b_f_pad, dc, H):
    S = f_raw.shape[0]
    blk = _tile(S, (256, 128))

    def body(f_ref, b_ref, dc_ref, df_ref, db_ref):
        z = f_ref[...] + b_ref[...]
        dlogf = _blocked_cumsum(dc_ref[...], S, blk, True)
        lane = lax.broadcasted_iota(jnp.int32, (S, LANES), 1)
        df = jnp.where(lane < H, dlogf * _sigmoid(-z), 0.0)
        df_ref[...] = df.astype(BF16)
        db_ref[...] = jnp.sum(df, axis=0, keepdims=True)

    return pl.pallas_call(body, name='fgate_bwd', grid=(1,),
                          in_specs=[_fb((S, LANES)), _fb((1, LANES)), _fb((S, LANES))],
                          out_specs=[_fb((S, LANES)), _fb((1, LANES))],
                          out_shape=[jax.ShapeDtypeStruct((S, LANES), BF16), jax.ShapeDtypeStruct((1, LANES), F32)],
                          compiler_params=_params(('arbitrary',)))(f_raw, b_f_pad, dc)


def _fox_logits(q, k, c_blk, ct_blk, h, T, diagonal):
    s = lax.dot_general(q, k, _DIMS['nt'], preferred_element_type=F32) * (1.0 / math.sqrt(HEAD_DIM))
    lane = lax.broadcasted_iota(jnp.int32, c_blk.shape, 1)
    cq = jnp.sum(jnp.where(lane == h, c_blk, 0.0), axis=1, keepdims=True)
    sub = lax.broadcasted_iota(jnp.int32, ct_blk.shape, 0)
    ck = jnp.sum(jnp.where(sub == h, ct_blk, 0.0), axis=0, keepdims=True)
    s = s + cq - ck
    if not diagonal:
        return s
    rows = lax.broadcasted_iota(jnp.int32, (T, T), 0)
    cols = lax.broadcasted_iota(jnp.int32, (T, T), 1)
    return jnp.where(cols <= rows, s, -jnp.inf)


def _below_and_on_diagonal(q_blk, k_blk, step):
    @pl.when(k_blk < q_blk)
    def _():
        step(False)

    @pl.when(k_blk == q_blk)
    def _():
        step(True)


def fox_fwd(qn, kn, vb, c, ct, T):
    S, FW = qn.shape
    H = FW // HEAD_DIM
    Hp = ct.shape[0]
    n = S // T

    HB = _tile(H, (8, 4, 2, 1))
    W2 = HB * HEAD_DIM

    def body(q_ref, k_ref, v_ref, c_ref, ct_ref, o_ref, lse_ref, m_s, l_s, acc_s):
        hb, i, j = pl.program_id(0), pl.program_id(1), pl.program_id(2)

        @pl.when(j == 0)
        def _():
            m_s[...] = jnp.full_like(m_s, -jnp.inf)
            l_s[...] = jnp.zeros_like(l_s)
            acc_s[...] = jnp.zeros_like(acc_s)

        def step(diagonal):
            for hh in range(HB):
                sl = slice(hh * HEAD_DIM, (hh + 1) * HEAD_DIM)
                s = _fox_logits(q_ref[:, sl], k_ref[:, sl], c_ref[...], ct_ref[...], hb * HB + hh, T, diagonal)
                m_old = m_s[hh]
                m_new = jnp.maximum(m_old, jnp.max(s, axis=1, keepdims=True))
                alpha = jnp.exp(m_old - m_new)
                p = jnp.exp(s - m_new)
                l_s[hh] = alpha * l_s[hh] + jnp.sum(p, axis=1, keepdims=True)
                acc_s[hh] = alpha * acc_s[hh] + jnp.dot(p.astype(BF16), v_ref[:, sl], preferred_element_type=F32)
                m_s[hh] = m_new

        _below_and_on_diagonal(i, j, step)

        @pl.when(j == i)
        def _():
            for hh in range(HB):
                o_ref[:, hh * HEAD_DIM:(hh + 1) * HEAD_DIM] = acc_s[hh] / l_s[hh]
                lse_ref[hh] = jnp.broadcast_to(m_s[hh] + jnp.log(l_s[hh]), (T, LANES))

    qs = pl.BlockSpec((T, W2), lambda h, i, j: (i, h))
    ks = pl.BlockSpec((T, W2), lambda h, i, j: (jnp.minimum(j, i), h))
    return pl.pallas_call(
        body, name='fox_fwd', grid=(H // HB, n, n),
        in_specs=[qs, ks, ks, pl.BlockSpec((T, LANES), lambda h, i, j: (i, 0)),
                  pl.BlockSpec((Hp, T), lambda h, i, j: (0, jnp.minimum(j, i)))],
        out_specs=[qs, pl.BlockSpec((HB, T, LANES), lambda h, i, j: (h, i, 0))],
        out_shape=[jax.ShapeDtypeStruct((S, FW), F32), jax.ShapeDtypeStruct((H, S, LANES), F32)],
        scratch_shapes=[pltpu.VMEM((HB, T, 1), F32), pltpu.VMEM((HB, T, 1), F32), pltpu.VMEM((HB, T, HEAD_DIM), F32)],
        compiler_params=_params(('parallel', 'parallel', 'arbitrary')))(qn, kn, vb, c, ct)


def _fox_p_ds(q_ref, k_ref, v_ref, do_ref, c_ref, ct_ref, lse_ref, dl_ref, h, T, diagonal):
    s = _fox_logits(q_ref[...], k_ref[...], c_ref[...], ct_ref[...], h, T, diagonal)
    p = jnp.exp(s - jnp.tile(lse_ref[...], (1, T // LANES)))
    dp = lax.dot_general(do_ref[...], v_ref[...], _DIMS['nt'], preferred_element_type=F32)
    ds = p * (dp - jnp.tile(dl_ref[...], (1, T // LANES)))
    return p, dp, ds


def fox_bwd_q(qn, kn, vb, do, c, ct, lse, dl, T):
    S, FW = qn.shape
    H = FW // HEAD_DIM
    Hp = ct.shape[0]
    n = S // T
    HB = _tile(H, (8, 4, 2, 1))
    W2 = HB * HEAD_DIM

    def body(q_ref, k_ref, v_ref, do_ref, c_ref, ct_ref, lse_ref, dl_ref, dq_ref, dl2_ref, acc_s, rs_s):
        hb, i, j = pl.program_id(0), pl.program_id(1), pl.program_id(2)

        @pl.when(j == 0)
        def _():
            acc_s[...] = jnp.zeros_like(acc_s)
            rs_s[...] = jnp.zeros_like(rs_s)

        def step(diagonal):
            for hh in range(HB):
                sl = slice(hh * HEAD_DIM, (hh + 1) * HEAD_DIM)
                p, dp, ds = _fox_p_ds(q_ref.at[:, sl], k_ref.at[:, sl], v_ref.at[:, sl], do_ref.at[:, sl], c_ref, ct_ref,
                                      lse_ref.at[hh], dl_ref.at[hh], hb * HB + hh, T, diagonal)
                acc_s[hh] += jnp.dot(ds.astype(BF16), k_ref[:, sl], preferred_element_type=F32)
                rs_s[hh] += jnp.sum(p * dp, axis=1, keepdims=True)

        _below_and_on_diagonal(i, j, step)

        @pl.when(j == i)
        def _():
            for hh in range(HB):
                dq_ref[:, hh * HEAD_DIM:(hh + 1) * HEAD_DIM] = acc_s[hh] * (1.0 / math.sqrt(HEAD_DIM))
                dl2_ref[hh] = jnp.broadcast_to(rs_s[hh], (T, LANES))

    qs = pl.BlockSpec((T, W2), lambda h, i, j: (i, h))
    ks = pl.BlockSpec((T, W2), lambda h, i, j: (jnp.minimum(j, i), h))
    st = pl.BlockSpec((HB, T, LANES), lambda h, i, j: (h, i, 0))
    return pl.pallas_call(
        body, name='fox_bwd_q', grid=(H // HB, n, n),
        in_specs=[qs, ks, ks, qs, pl.BlockSpec((T, LANES), lambda h, i, j: (i, 0)),
                  pl.BlockSpec((Hp, T), lambda h, i, j: (0, jnp.minimum(j, i))), st, st],
        out_specs=[qs, st], out_shape=[jax.ShapeDtypeStruct((S, FW), F32), jax.ShapeDtypeStruct((H, S, LANES), F32)],
        scratch_shapes=[pltpu.VMEM((HB, T, HEAD_DIM), F32), pltpu.VMEM((HB, T, 1), F32)],
        compiler_params=_params(('parallel', 'parallel', 'arbitrary')))(qn, kn, vb, do, c, ct, lse, dl)


def fox_bwd_kv(qn, kn, vb, do, c, ct, lse, dl, T):
    S, FW = qn.shape
    H = FW // HEAD_DIM
    Hp = ct.shape[0]
    n = S // T

    HB = _tile(H, (8, 4, 2, 1))
    W2 = HB * HEAD_DIM

    def body(q_ref, k_ref, v_ref, do_ref, c_ref, ct_ref, lse_ref, dl_ref, dk_ref, dv_ref, dc_ref, dk_s, dv_s, dc_s):
        hb, j, i = pl.program_id(0), pl.program_id(1), pl.program_id(2)

        @pl.when(i == 0)
        def _():
            dk_s[...] = jnp.zeros_like(dk_s)
            dv_s[...] = jnp.zeros_like(dv_s)
            dc_s[...] = jnp.zeros_like(dc_s)

        def step(diagonal):
            for hh in range(HB):
                sl = slice(hh * HEAD_DIM, (hh + 1) * HEAD_DIM)
                p, _, ds = _fox_p_ds(q_ref.at[:, sl], k_ref.at[:, sl], v_ref.at[:, sl], do_ref.at[:, sl], c_ref, ct_ref,
                                     lse_ref.at[hh], dl_ref.at[hh], hb * HB + hh, T, diagonal)
                dv_s[hh] += lax.dot_general(p.astype(BF16), do_ref[:, sl], _DIMS['tn'], preferred_element_type=F32)
                dk_s[hh] += lax.dot_general(ds.astype(BF16), q_ref[:, sl], _DIMS['tn'], preferred_element_type=F32)
                dc_s[hh] += jnp.sum(ds, axis=0, keepdims=True)

        _below_and_on_diagonal(i, j, step)

        @pl.when(i == n - 1)
        def _():
            for hh in range(HB):
                sl = slice(hh * HEAD_DIM, (hh + 1) * HEAD_DIM)
                dk_ref[:, sl] = dk_s[hh] * (1.0 / math.sqrt(HEAD_DIM))
                dv_ref[:, sl] = dv_s[hh].astype(BF16)
                dc_ref[hh] = -dc_s[hh]

    qs = pl.BlockSpec((T, W2), lambda h, j, i: (jnp.maximum(i, j), h))
    ks = pl.BlockSpec((T, W2), lambda h, j, i: (j, h))
    st = pl.BlockSpec((HB, T, LANES), lambda h, j, i: (h, jnp.maximum(i, j), 0))
    return pl.pallas_call(
        body, name='fox_bwd_kv', grid=(H // HB, n, n),
        in_specs=[qs, ks, ks, qs, pl.BlockSpec((T, LANES), lambda h, j, i: (jnp.maximum(i, j), 0)),
                  pl.BlockSpec((Hp, T), lambda h, j, i: (0, j)), st, st],
        out_specs=[ks, ks, pl.BlockSpec((HB, 1, T), lambda h, j, i: (h, 0, j))],
        out_shape=[jax.ShapeDtypeStruct((S, FW), F32), jax.ShapeDtypeStruct((S, FW), BF16),
                   jax.ShapeDtypeStruct((H, 1, S), F32)],
        scratch_shapes=[pltpu.VMEM((HB, T, HEAD_DIM), F32), pltpu.VMEM((HB, T, HEAD_DIM), F32),
                        pltpu.VMEM((HB, 1, T), F32)],
        compiler_params=_params(('parallel', 'parallel', 'arbitrary')))(qn, kn, vb, do, c, ct, lse, dl)


def _shift_down(v, d, rows, fill):
    return jnp.where(rows >= d, pltpu.roll(v, d, 0), fill)


def _shift_up(v, d, rows, S, fill):
    return jnp.where(rows < S - d, pltpu.roll(v, S - d, 0), fill)


SUBLANES = 8


def _scan_by_doubling(a, b, pos, span, reverse):
    n = a.shape[0]
    d = 1
    while d < span:
        if reverse:
            keep = pos < span - d
            a_s, b_s = jnp.where(keep, pltpu.roll(a, n - d, 0), 1.0), jnp.where(keep, pltpu.roll(b, n - d, 0), 0.0)
        else:
            keep = pos >= d
            a_s, b_s = jnp.where(keep, pltpu.roll(a, d, 0), 1.0), jnp.where(keep, pltpu.roll(b, d, 0), 0.0)
        b = a * b_s + b
        a = a * a_s
        d *= 2
    return a, b


def _scan(a, b, rows, S, reverse, scr):
    groups = S // SUBLANES
    a, b = _scan_by_doubling(a, b, jnp.bitwise_and(rows, SUBLANES - 1), SUBLANES, reverse)
    scr[0][...] = a
    scr[1][...] = b
    edge = 0 if reverse else SUBLANES - 1
    a_g = scr[0][pl.ds(edge, groups, stride=SUBLANES), :]
    b_g = scr[1][pl.ds(edge, groups, stride=SUBLANES), :]
    g_pos = lax.broadcasted_iota(jnp.int32, (groups, LANES), 0)
    _, h_g = _scan_by_doubling(a_g, b_g, g_pos, groups, reverse)
    if reverse:
        carry = jnp.where(g_pos < groups - 1, pltpu.roll(h_g, groups - 1, 0), 0.0)
    else:
        carry = jnp.where(g_pos >= 1, pltpu.roll(h_g, 1, 0), 0.0)
    for r in range(SUBLANES):
        scr[0][pl.ds(r, groups, stride=SUBLANES), :] = carry
    return b + a * scr[0][...]


def _lru_forward(u, cw, cb, wra, bra, wri, bri, lam, rows, scr):
    uc = cb + cw[CONV_W - 1] * u
    for d in range(1, CONV_W):
        uc = uc + cw[CONV_W - 1 - d] * _shift_down(u, d, rows, 0.0)
    ucb = uc.astype(BF16)
    r = _sigmoid(jnp.dot(ucb, wra.astype(BF16), preferred_element_type=F32) + bra)
    ig = _sigmoid(jnp.dot(ucb, wri.astype(BF16), preferred_element_type=F32) + bri)
    sp = _softplus(-lam)
    log_a = -LRU_C * r * sp
    a = jnp.exp(log_a)
    sq = jnp.sqrt(_neg_expm1(2.0 * log_a))
    iu = ig * uc
    hseq = _scan(a, sq * iu, rows, u.shape[0], False, scr)
    return uc, ucb, r, ig, sp, a, sq, iu, hseq


def _lru_specs(S, n_u, n_g):
    col = lambda off: pl.BlockSpec((S, LANES), lambda cbk: (0, off + cbk))
    vec = pl.BlockSpec((1, LANES), lambda cbk: (0, cbk))
    mat = pl.BlockSpec((None, LANES, LANES), lambda cbk: (cbk, 0, 0))
    cw = pl.BlockSpec((CONV_W, LANES), lambda cbk: (0, cbk))
    return col, vec, mat, cw


def lru_fwd(proj, conv_w, conv_b, w_ra, b_ra, w_ri, b_ri, lam, u_off, g_off):
    S = proj.shape[0]
    nb = w_ra.shape[0]
    col, vec, mat, cws = _lru_specs(S, u_off, g_off)

    def body(u_ref, g_ref, cw_ref, cb_ref, wra_ref, bra_ref, wri_ref, bri_ref, lam_ref, y_ref, scr0, scr1):
        rows = lax.broadcasted_iota(jnp.int32, (S, LANES), 0)
        cw = [cw_ref[t:t + 1, :] for t in range(CONV_W)]
        hseq = _lru_forward(u_ref[...], cw, cb_ref[...], wra_ref[...], bra_ref[...], wri_ref[...],
                            bri_ref[...], lam_ref[...], rows, (scr0, scr1))[-1]
        y_ref[...] = hseq * _gelu_and_grad(g_ref[...])[0]

    return pl.pallas_call(
        body, name='lru_fwd', grid=(nb,),
        in_specs=[col(u_off), col(g_off), cws, vec, mat, vec, mat, vec, vec], out_specs=col(0),
        out_shape=jax.ShapeDtypeStruct((S, nb * LANES), F32),
        scratch_shapes=[pltpu.VMEM((S, LANES), F32), pltpu.VMEM((S, LANES), F32)],
        compiler_params=_params(('parallel',)))(proj, proj, conv_w, conv_b, w_ra, b_ra, w_ri, b_ri, lam)


def lru_bwd(proj, dy, conv_w, conv_b, w_ra, b_ra, w_ri, b_ri, lam, u_off, g_off):
    S = proj.shape[0]
    nb = w_ra.shape[0]
    LW = nb * LANES
    col, vec, mat, cws = _lru_specs(S, u_off, g_off)

    def body(u_ref, g_ref, dy_ref, cw_ref, cb_ref, wra_ref, bra_ref, wri_ref, bri_ref, lam_ref,
             du_ref, dg_ref, dcw_ref, dcb_ref, dwra_ref, dbra_ref, dwri_ref, dbri_ref, dlam_ref, scr0, scr1):
        rows = lax.broadcasted_iota(jnp.int32, (S, LANES), 0)
        u, lam_v = u_ref[...], lam_ref[...]
        cw = [cw_ref[t:t + 1, :] for t in range(CONV_W)]
        wra, wri = wra_ref[...].astype(BF16), wri_ref[...].astype(BF16)
        uc, ucb, r, ig, sp, a, sq, iu, hseq = _lru_forward(u, cw, cb_ref[...], wra, bra_ref[...], wri, bri_ref[...],
                                                           lam_v, rows, (scr0, scr1))
        gl, dgl = _gelu_and_grad(g_ref[...])
        dy_v = dy_ref[...]
        dg_ref[...] = (dy_v * hseq * dgl).astype(BF16)
        G = _scan(_shift_up(a, 1, rows, S, 0.0), dy_v * gl, rows, S, True, (scr0, scr1))
        da = G * _shift_down(hseq, 1, rows, 0.0)
        diu = G * sq
        dsq = G * iu
        dlog_a = da * a - dsq * a * a / jnp.maximum(sq, 1e-30)
        dr = dlog_a * (-LRU_C * sp)
        dsp = jnp.sum(dlog_a * (-LRU_C * r), axis=0, keepdims=True)
        dlam_ref[...] = -dsp * _sigmoid(-lam_v)
        dzr = dr * r * (1.0 - r)
        dzi = diu * uc * ig * (1.0 - ig)
        dzrb, dzib = dzr.astype(BF16), dzi.astype(BF16)
        duc = (diu * ig + lax.dot_general(dzrb, wra, _DIMS['nt'], preferred_element_type=F32)
               + lax.dot_general(dzib, wri, _DIMS['nt'], preferred_element_type=F32))
        dwra_ref[...] = lax.dot_general(ucb, dzrb, _DIMS['tn'], preferred_element_type=F32)
        dwri_ref[...] = lax.dot_general(ucb, dzib, _DIMS['tn'], preferred_element_type=F32)
        dbra_ref[...] = jnp.sum(dzr, axis=0, keepdims=True)
        dbri_ref[...] = jnp.sum(dzi, axis=0, keepdims=True)
        dcb_ref[...] = jnp.sum(duc, axis=0, keepdims=True)
        du = cw[CONV_W - 1] * duc
        dcw_ref[CONV_W - 1:CONV_W, :] = jnp.sum(duc * u, axis=0, keepdims=True)
        for d in range(1, CONV_W):
            du = du + cw[CONV_W - 1 - d] * _shift_up(duc, d, rows, S, 0.0)
            dcw_ref[CONV_W - 1 - d:CONV_W - d, :] = jnp.sum(duc * _shift_down(u, d, rows, 0.0), axis=0, keepdims=True)
        du_ref[...] = du.astype(BF16)

    sd = jax.ShapeDtypeStruct
    return pl.pallas_call(
        body, name='lru_bwd', grid=(nb,),
        in_specs=[col(u_off), col(g_off), col(0), cws, vec, mat, vec, mat, vec, vec],
        out_specs=[col(0), col(0), cws, vec, mat, vec, mat, vec, vec],
        out_shape=[sd((S, LW), BF16), sd((S, LW), BF16), sd((CONV_W, LW), F32), sd((1, LW), F32),
                   sd((nb, LANES, LANES), F32), sd((1, LW), F32), sd((nb, LANES, LANES), F32), sd((1, LW), F32),
                   sd((1, LW), F32)],
        scratch_shapes=[pltpu.VMEM((S, LANES), F32), pltpu.VMEM((S, LANES), F32)],
        compiler_params=_params(('parallel',)))(proj, proj, dy, conv_w, conv_b, w_ra, b_ra, w_ri, b_ri, lam)


def mix_fwd(o_fox, y_lru, g_fox, g_lru):
    S, FW = o_fox.shape
    tr = _tile(S, (256, 128))

    def body(o_ref, y_ref, gf_ref, gl_ref, m_ref):
        m_ref[...] = jnp.concatenate([_rms(o_ref[...], gf_ref[...]), _rms(y_ref[...], gl_ref[...])],
                                     axis=1).astype(BF16)

    return _rows_call('mix_fwd', body, S, tr,
                      [(o_fox, _rb(tr, FW)), (y_lru, _rb(tr, FW)), (g_fox, _fb((1, FW))), (g_lru, _fb((1, FW)))],
                      [((S, 2 * FW), BF16, _rb(tr, 2 * FW))])[0]


def mix_bwd(o_fox, y_lru, g_fox, g_lru, dmix):
    S, FW = o_fox.shape
    H = FW // HEAD_DIM
    tr = _tile(S, (256, 128))

    def body(o_ref, y_ref, gf_ref, gl_ref, df_ref, dl_ref, do_ref, dlt_ref, dy_ref, dgf_ref, dgl_ref):
        o = o_ref[...]
        do, dgf = _rms_bwd(o, gf_ref[...], df_ref[...])
        dyl, dgl = _rms_bwd(y_ref[...], gl_ref[...], dl_ref[...])
        do_ref[...] = do.astype(BF16)
        dy_ref[...] = dyl
        prod = do * o
        for h in range(H):
            dlt_ref[h] = jnp.broadcast_to(
                jnp.sum(prod[:, h * HEAD_DIM:(h + 1) * HEAD_DIM], axis=1, keepdims=True), (tr, LANES))
        first = pl.program_id(0) == 0
        _acc_out(dgf_ref, first, dgf)
        _acc_out(dgl_ref, first, dgl)

    g = _fb((1, FW))
    return _rows_call('mix_bwd', body, S, tr,
                      [(o_fox, _rb(tr, FW)), (y_lru, _rb(tr, FW)), (g_fox, g), (g_lru, g), (dmix, _rb(tr, FW, 0)),
                       (dmix, _rb(tr, FW, 1))],
                      [((S, FW), BF16, _rb(tr, FW)), ((H, S, LANES), F32, pl.BlockSpec((H, tr, LANES), lambda i: (0, i, 0))),
                       ((S, FW), F32, _rb(tr, FW)), ((1, FW), F32, g), ((1, FW), F32, g)])


def _xattn_heads(cq_raw, ckv, g_cq, g_ck, XW):
    out = []
    for h in range(XW // HEAD_DIM):
        sl = slice(h * HEAD_DIM, (h + 1) * HEAD_DIM)
        out.append((cq_raw[:, sl], _rms(cq_raw[:, sl], g_cq), ckv[:, sl], _rms(ckv[:, sl], g_ck),
                    ckv[:, XW + h * HEAD_DIM:XW + (h + 1) * HEAD_DIM].astype(BF16)))
    return out


def xattn_fwd(cq_raw, ckv, g_cq, g_ck):
    S, XW = cq_raw.shape
    M = ckv.shape[0]
    tr = _tile(S, (512, 256, 128))

    def body(q_ref, kv_ref, gq_ref, gk_ref, o_ref):
        outs = []
        for _, qn, _, kn, v in _xattn_heads(q_ref[...], kv_ref[...], gq_ref[...], gk_ref[...], XW):
            s = lax.dot_general(qn.astype(BF16), kn.astype(BF16), _DIMS['nt'], preferred_element_type=F32)
            s = s / math.sqrt(HEAD_DIM)
            p = jnp.exp(s - jnp.max(s, axis=1, keepdims=True))
            p = p / jnp.sum(p, axis=1, keepdims=True)
            outs.append(jnp.dot(p.astype(BF16), v, preferred_element_type=F32))
        o_ref[...] = jnp.concatenate(outs, axis=1).astype(BF16)

    g = _fb((1, HEAD_DIM))
    return _rows_call('xattn_fwd', body, S, tr,
                      [(cq_raw, _rb(tr, XW)), (ckv, _fb((M, 2 * XW))), (g_cq, g), (g_ck, g)],
                      [((S, XW), BF16, _rb(tr, XW))])[0]


def xattn_bwd(cq_raw, ckv, g_cq, g_ck, do):
    S, XW = cq_raw.shape
    M = ckv.shape[0]
    tr = _tile(S, (512, 256, 128))
    n = S // tr

    def body(q_ref, kv_ref, gq_ref, gk_ref, do_ref, dq_ref, dkv_ref, dgq_ref, dgk_ref):
        i = pl.program_id(0)
        do_v = do_ref[...]
        dqs, dkn, dvs = [], [], []
        dgq = jnp.zeros((1, HEAD_DIM), F32)
        for h, (q_raw, qn, _, kn, v) in enumerate(_xattn_heads(q_ref[...], kv_ref[...], gq_ref[...], gk_ref[...], XW)):
            qb, kb = qn.astype(BF16), kn.astype(BF16)
            doh = do_v[:, h * HEAD_DIM:(h + 1) * HEAD_DIM]
            s = lax.dot_general(qb, kb, _DIMS['nt'], preferred_element_type=F32) / math.sqrt(HEAD_DIM)
            p = jnp.exp(s - jnp.max(s, axis=1, keepdims=True))
            p = p / jnp.sum(p, axis=1, keepdims=True)
            dp = lax.dot_general(doh, v, _DIMS['nt'], preferred_element_type=F32)
            ds = (p * (dp - jnp.sum(p * dp, axis=1, keepdims=True)) / math.sqrt(HEAD_DIM)).astype(BF16)
            dvs.append(lax.dot_general(p.astype(BF16), doh, _DIMS['tn'], preferred_element_type=F32))
            dkn.append(lax.dot_general(ds, qb, _DIMS['tn'], preferred_element_type=F32))
            dq, g1 = _rms_bwd(q_raw, gq_ref[...], jnp.dot(ds, kb, preferred_element_type=F32))
            dqs.append(dq)
            dgq = dgq + g1
        dq_ref[...] = jnp.concatenate(dqs, axis=1).astype(BF16)
        first = i == 0
        _acc_out(dgq_ref, first, dgq)
        _acc_out(dkv_ref, first, jnp.concatenate(dkn + dvs, axis=1))

        @pl.when(i == n - 1)
        def _():
            kv = kv_ref[...]
            acc = dkv_ref[...]
            dk, gk = _heads(lambda t, d: _rms_bwd(t, gk_ref[...], d), XW // HEAD_DIM, kv[:, :XW], acc[:, :XW])
            dkv_ref[:, :XW] = dk
            dgk_ref[...] = gk

    g = _fb((1, HEAD_DIM))
    return _rows_call('xattn_bwd', body, S, tr,
                      [(cq_raw, _rb(tr, XW)), (ckv, _fb((M, 2 * XW))), (g_cq, g), (g_ck, g), (do, _rb(tr, XW))],
                      [((S, XW), BF16, _rb(tr, XW)), ((M, 2 * XW), F32, _fb((M, 2 * XW))), ((1, HEAD_DIM), F32, g),
                       ((1, HEAD_DIM), F32, g)])


def gate_up_fwd(hf, w, F):
    S, D = hf.shape
    J, _, Nj = w.shape
    tm = _tile(S, (1024, 512, 256, 128))
    tn = _tile(Nj, (256, 128))
    per = Nj // tn
    half = J // 2 * per

    def body(a_ref, bg_ref, bu_ref, gu_ref, act_ref):
        a = a_ref[...]
        g = jnp.dot(a, bg_ref[...], preferred_element_type=F32)
        u = jnp.dot(a, bu_ref[...], preferred_element_type=F32)
        gu_ref[0] = g
        gu_ref[1] = u
        act_ref[...] = (g * _sigmoid(g) * u).astype(BF16)

    return pl.pallas_call(
        body, name='proj_gate_up', grid=(S // tm, half),
        in_specs=[pl.BlockSpec((tm, D), lambda m, n: (m, 0)),
                  pl.BlockSpec((None, D, tn), lambda m, n: (n // per, 0, n % per)),
                  pl.BlockSpec((None, D, tn), lambda m, n: ((n + half) // per, 0, n % per))],
        out_specs=[pl.BlockSpec((2, tm, tn), lambda m, n: (0, m, n)), pl.BlockSpec((tm, tn), lambda m, n: (m, n))],
        out_shape=[jax.ShapeDtypeStruct((2, S, F), F32), jax.ShapeDtypeStruct((S, F), BF16)],
        compiler_params=_params(('parallel', 'parallel')))(hf, w, w)


def down_bwd_x(dyb, w_down, gu, after):
    S, D = dyb.shape
    F = w_down.shape[0]
    tm = _tile(S, (1024, 512, 256, 128))
    tn = _tile(F, (512, 256, 128))

    def body(a_ref, b_ref, gu_ref, after_ref, o_ref):
        da = lax.dot_general(a_ref[...], b_ref[...], _DIMS['nt'], preferred_element_type=F32)
        g = gu_ref[0]
        sg = _sigmoid(g)
        o_ref[0] = (da * gu_ref[1] * sg * (1.0 + g * (1.0 - sg))).astype(BF16)
        o_ref[1] = (da * g * sg).astype(BF16)

    planes = pl.BlockSpec((2, tm, tn), lambda m, n: (0, m, n))
    return pl.pallas_call(
        body, name='bwd_down_x', grid=(S // tm, F // tn),
        in_specs=[pl.BlockSpec((tm, D), lambda m, n: (m, 0)), pl.BlockSpec((tn, D), lambda m, n: (n, 0)), planes, ANY],
        out_specs=planes, out_shape=jax.ShapeDtypeStruct((2, S, F), BF16),
        compiler_params=_params(('parallel', 'parallel')))(dyb, w_down, gu, after)


def down_fwd_loss(act, w_down, x2, target):
    S, F = act.shape
    D = w_down.shape[1]
    tm, tn, tk = _mm_tiles(S, D, F, F, act, w_down, F32, x2, tn_cands=(512, 256, 128))
    nk = F // tk

    def body(a_ref, b_ref, x_ref, t_ref, d_ref, db_ref, l_ref, acc):
        m, n, k = pl.program_id(0), pl.program_id(1), pl.program_id(2)
        part = jnp.dot(a_ref[...], b_ref[...], preferred_element_type=F32)

        @pl.when(k == 0)
        def _():
            acc[...] = part

        @pl.when(k > 0)
        def _():
            acc[...] += part

        @pl.when(k == nk - 1)
        def _():
            err = acc[...] + x_ref[...] - t_ref[...]
            d = err * (1.0 / D)
            d_ref[...] = d
            db_ref[...] = d.astype(BF16)
            tot = jnp.sum(jnp.sum(err * err, axis=1, keepdims=True), axis=0, keepdims=True) * (0.5 / D)
            _acc_out(l_ref, jnp.logical_and(m == 0, n == 0), jnp.broadcast_to(tot, (1, LANES)))

    tile = pl.BlockSpec((tm, tn), lambda m, n, k: (m, n))
    return pl.pallas_call(
        body, name='proj_down', grid=(S // tm, D // tn, nk),
        in_specs=[pl.BlockSpec((tm, tk), lambda m, n, k: (m, k)), pl.BlockSpec((tk, tn), lambda m, n, k: (k, n)), tile, tile],
        out_specs=[tile, tile, pl.BlockSpec((1, LANES), lambda m, n, k: (0, 0))],
        out_shape=[jax.ShapeDtypeStruct((S, D), F32), jax.ShapeDtypeStruct((S, D), BF16),
                   jax.ShapeDtypeStruct((1, LANES), F32)],
        scratch_shapes=[pltpu.VMEM((tm, tn), F32)],
        compiler_params=_params(('arbitrary', 'arbitrary', 'arbitrary')))(act, w_down, x2, target)


def _adamw_math(w, gv, m, v):
    mn = ADAM_B1 * m + (1.0 - ADAM_B1) * gv
    vn = ADAM_B2 * v + (1.0 - ADAM_B2) * (gv * gv)
    m_hat = mn / (1.0 - ADAM_B1 ** ADAM_STEP)
    v_hat = vn / (1.0 - ADAM_B2 ** ADAM_STEP)
    return -ADAM_LR * (m_hat / (jnp.sqrt(v_hat) + ADAM_EPS) + ADAM_WD * w), mn, vn


def adamw(name, w, g, m, v):
    R, C = w.shape
    tr = _row_tile(R, C)

    def body(w_ref, g_ref, m_ref, v_ref, d_ref, mo_ref, vo_ref):
        d_ref[...], mo_ref[...], vo_ref[...] = _adamw_math(w_ref[...], g_ref[...], m_ref[...], v_ref[...])

    spec = _rb(tr, C)
    return _rows_call(name, body, R, tr, [(w, spec), (g, spec), (m, spec), (v, spec)], [((R, C), F32, spec)] * 3)


def adamw_halves(name, w, mine, other, m, v, c_idx):
    R, C = w.shape
    hr = R // 2
    tr = _row_tile(hr, C)

    def body(c_ref, w_ref, a_ref, b_ref, m_ref, v_ref, g_ref, d_ref, mo_ref, vo_ref):
        gv = jnp.where(pl.program_id(0) == c_ref[0], a_ref[...], b_ref[...])
        g_ref[...] = gv
        d_ref[...], mo_ref[...], vo_ref[...] = _adamw_math(w_ref[...], gv, m_ref[...], v_ref[...])

    full = pl.BlockSpec((None, tr, C), lambda hh, i, c_ref: (hh, i, 0))
    mine_spec = pl.BlockSpec((tr, C), lambda hh, i, c_ref: (jnp.where(hh == c_ref[0], i, 0), 0))
    other_spec = pl.BlockSpec((tr, C), lambda hh, i, c_ref: (jnp.where(hh == c_ref[0], 0, i), 0))
    outs = pl.pallas_call(
        body, name=name,
        grid_spec=pltpu.PrefetchScalarGridSpec(num_scalar_prefetch=1, grid=(2, hr // tr),
                                               in_specs=[full, mine_spec, other_spec, full, full], out_specs=[full] * 4),
        out_shape=[jax.ShapeDtypeStruct((2, hr, C), F32)] * 4,
        compiler_params=_params(('parallel', 'parallel')))(
            c_idx, w.reshape(2, hr, C), mine, other, m.reshape(2, hr, C), v.reshape(2, hr, C))
    return [o.reshape(R, C) for o in outs]


def _place():
    x, y, c = lax.axis_index('x'), lax.axis_index('y'), lax.axis_index('c')
    return x, y, c, [(1 - x, y), (x, 1 - y), (1 - x, 1 - y)]


def _rcopy(src, dst, ssem, rsem, dev):
    return pltpu.make_async_remote_copy(src_ref=src, dst_ref=dst, send_sem=ssem, recv_sem=rsem, device_id=dev,
                                        device_id_type=MESH)


HBM = pl.BlockSpec(memory_space=pltpu.HBM)
SEM = pl.BlockSpec(memory_space=pltpu.SEMAPHORE)
EFFECT = pltpu.SideEffectType.DATAFLOW_SIDE_EFFECTING


def _in_hbm(a):
    return pltpu.with_memory_space_constraint(a, pltpu.HBM)


def _rows_part(shape, whole, half):
    return pl.ds(0, shape[0]) if whole else pl.ds(half * (shape[0] // 2), shape[0] // 2)


def gather_start(name, shards, whole):
    nT = len(shards)

    def body(*refs):
        srcs, lands = refs[:nT], refs[nT:2 * nT]
        ssem, rsem, token = refs[2 * nT], refs[2 * nT + 1], refs[-1]
        x, y, c, chips = _place()
        for t in range(nT):
            rows = _rows_part(shards[t].shape, whole[t], c)
            for k, (px, py) in enumerate(chips):
                _rcopy(srcs[t].at[rows], lands[t].at[2 * x + y, rows], ssem.at[3 * t + k], rsem.at[3 * t + k],
                       (px, py, c)).start()
        token[...] = jnp.zeros_like(token)

    zones = [lax.empty((N_CHIPS,) + s.shape, s.dtype) for s in shards]
    outs = pl.pallas_call(
        body, name=name,
        out_shape=(pltpu.SemaphoreType.DMA((3 * nT,)), pltpu.SemaphoreType.DMA((3 * nT,)),
                   *[pltpu.HBM(s.shape, s.dtype) for s in shards], *[pltpu.HBM(z.shape, z.dtype) for z in zones],
                   jax.ShapeDtypeStruct((8, LANES), F32)),
        in_specs=[HBM] * (2 * nT), out_specs=(SEM, SEM, *[HBM] * (2 * nT), pl.BlockSpec(memory_space=pltpu.VMEM)),
        input_output_aliases={i: 2 + i for i in range(2 * nT)},
        compiler_params=pltpu.CompilerParams(has_side_effects=EFFECT))(*[_in_hbm(a) for a in list(shards) + zones])
    return outs[0], outs[1], outs[2:2 + nT], outs[2 + nT:2 + 2 * nT], outs[-1]


def gather_wait(name, t, shard, zone, ssem, rsem, after, whole):
    after = after if isinstance(after, (list, tuple)) else [after]

    def body(src_ref, land_ref, ssem_ref, rsem_ref, *rest):
        x, y, c, chips = _place()
        rows = _rows_part(shard.shape, whole, c)
        for k, (px, py) in enumerate(chips):
            cp = _rcopy(src_ref.at[rows], land_ref.at[2 * px + py, rows], ssem_ref.at[3 * t + k], rsem_ref.at[3 * t + k],
                        (px, py, c))
            cp.wait_send()
            cp.wait_recv()

    return pl.pallas_call(
        body, name=name, out_shape=(pltpu.HBM(shard.shape, shard.dtype), pltpu.HBM(zone.shape, zone.dtype)),
        in_specs=(HBM, HBM, SEM, SEM, *[ANY] * len(after)), out_specs=(HBM, HBM), input_output_aliases={0: 0, 1: 1},
        compiler_params=pltpu.CompilerParams(has_side_effects=EFFECT))(shard, zone, ssem, rsem, *after)


def pair_swap(name, zone):
    hr = zone.shape[1] // 2

    def body(z_in, z_ref, ssem, rsem):
        x, y, c, chips = _place()
        cps = []
        for k, (px, py) in enumerate(chips):
            blk = z_ref.at[2 * px + py, pl.ds(c * hr, hr)]
            cps.append(_rcopy(blk, blk, ssem.at[k], rsem.at[k], (x, y, 1 - c)))
            cps[-1].start()
        for k, (px, py) in enumerate(chips):
            blk = z_ref.at[2 * px + py, pl.ds((1 - c) * hr, hr)]
            _rcopy(blk, blk, ssem.at[k], rsem.at[k], (x, y, 1 - c)).wait_recv()
        for cp in cps:
            cp.wait_send()

    return pl.pallas_call(
        body, name=name, in_specs=[ANY], out_specs=ANY, out_shape=jax.ShapeDtypeStruct(zone.shape, zone.dtype),
        input_output_aliases={0: 0},
        scratch_shapes=[pltpu.SemaphoreType.DMA((3,)), pltpu.SemaphoreType.DMA((3,))],
        compiler_params=_params())(zone)


def _swap_copies(z_ref, ssem, rsem):
    hr = z_ref.shape[1] // 2
    x, y, c, chips = _place()
    pairs = []
    for k, (px, py) in enumerate(chips):
        mine = z_ref.at[2 * px + py, pl.ds(c * hr, hr)]
        theirs = z_ref.at[2 * px + py, pl.ds((1 - c) * hr, hr)]
        pairs.append((_rcopy(mine, mine, ssem.at[k], rsem.at[k], (x, y, 1 - c)),
                      _rcopy(theirs, theirs, ssem.at[k], rsem.at[k], (x, y, 1 - c))))
    return pairs


def swap_start(name, zone):
    def body(z_ref, ssem, rsem, z_out, token):
        for mine, _ in _swap_copies(z_ref, ssem, rsem):
            mine.start()
        token[...] = jnp.zeros_like(token)

    return pl.pallas_call(
        body, name=name,
        out_shape=(pltpu.SemaphoreType.DMA((3,)), pltpu.SemaphoreType.DMA((3,)), pltpu.HBM(zone.shape, zone.dtype),
                   jax.ShapeDtypeStruct((8, LANES), F32)),
        in_specs=[HBM], out_specs=(SEM, SEM, HBM, pl.BlockSpec(memory_space=pltpu.VMEM)), input_output_aliases={0: 2},
        compiler_params=pltpu.CompilerParams(has_side_effects=EFFECT))(_in_hbm(zone))


def swap_wait(name, zone, ssem, rsem, after):
    def body(z_ref, ssem_ref, rsem_ref, after_ref, z_out):
        for mine, theirs in _swap_copies(z_ref, ssem_ref, rsem_ref):
            mine.wait_send()
            theirs.wait_recv()

    return pl.pallas_call(
        body, name=name, out_shape=(pltpu.HBM(zone.shape, zone.dtype),),
        in_specs=(HBM, SEM, SEM, ANY), out_specs=(HBM,), input_output_aliases={0: 0},
        compiler_params=pltpu.CompilerParams(has_side_effects=EFFECT))(zone, ssem, rsem, after)[0]


N_SENDERS = 7


def _scatter_copies(g_ref, l_ref, ssem, rsem):
    x, y, c, chips = _place()
    cps = []
    for k, (px, py) in enumerate(chips):
        for d in range(2):
            to = (c + d) % 2
            cps.append(_rcopy(g_ref.at[2 * px + py, to], l_ref.at[2 * k + d], ssem.at[2 * k + d], rsem.at[2 * k + d],
                              (px, py, to)))
    cps.append(_rcopy(g_ref.at[2 * x + y, 1 - c], l_ref.at[6], ssem.at[6], rsem.at[6], (x, y, 1 - c)))
    return cps


def scatter_start(name, g):
    def body(g_ref, l_ref, ssem, rsem, g_out, l_out, token):
        for cp in _scatter_copies(g_ref, l_ref, ssem, rsem):
            cp.start()
        token[...] = jnp.zeros_like(token)

    zone = lax.empty((N_SENDERS,) + g.shape[2:], g.dtype)
    return pl.pallas_call(
        body, name=name,
        out_shape=(pltpu.SemaphoreType.DMA((N_SENDERS,)), pltpu.SemaphoreType.DMA((N_SENDERS,)),
                   pltpu.HBM(g.shape, g.dtype), pltpu.HBM(zone.shape, zone.dtype), jax.ShapeDtypeStruct((8, LANES), F32)),
        in_specs=[HBM, HBM], out_specs=(SEM, SEM, HBM, HBM, pl.BlockSpec(memory_space=pltpu.VMEM)),
        input_output_aliases={0: 2, 1: 3},
        compiler_params=pltpu.CompilerParams(has_side_effects=EFFECT))(_in_hbm(g), _in_hbm(zone))


def scatter_wait(name, g, zone, ssem, rsem, after):
    def body(g_ref, l_ref, ssem_ref, rsem_ref, after_ref, g_out, l_out):
        for cp in _scatter_copies(g_ref, l_ref, ssem_ref, rsem_ref):
            cp.wait_send()
            cp.wait_recv()

    return pl.pallas_call(
        body, name=name, out_shape=(pltpu.HBM(g.shape, g.dtype), pltpu.HBM(zone.shape, zone.dtype)),
        in_specs=(HBM, HBM, SEM, SEM, ANY), out_specs=(HBM, HBM), input_output_aliases={0: 0, 1: 1},
        compiler_params=pltpu.CompilerParams(has_side_effects=EFFECT))(g, zone, ssem, rsem, after)


def sum_parts(name, g, landed, chip_idx, c_idx):
    hr, C = g.shape[2:]
    tr = _row_tile(hr, C, min_rows=16)

    def body(me_ref, c_ref, g_ref, l_ref, o_ref):
        acc = g_ref[...].astype(F32)
        for s in range(N_SENDERS):
            acc = acc + l_ref[s].astype(F32)
        o_ref[...] = acc

    return pl.pallas_call(
        body, name=name,
        grid_spec=pltpu.PrefetchScalarGridSpec(
            num_scalar_prefetch=2, grid=(hr // tr,),
            in_specs=[pl.BlockSpec((None, None, tr, C), lambda i, me_ref, c_ref: (me_ref[0], c_ref[0], i, 0)),
                      pl.BlockSpec((N_SENDERS, tr, C), lambda i, me_ref, c_ref: (0, i, 0))],
            out_specs=pl.BlockSpec((tr, C), lambda i, me_ref, c_ref: (i, 0))),
        out_shape=jax.ShapeDtypeStruct((hr, C), F32),
        compiler_params=_params(('parallel',)))(chip_idx, c_idx, g, landed)


def pair_join(name, halves):
    nT = len(halves)

    def body(*refs):
        ins, outs = refs[:nT], refs[nT:2 * nT]
        ssem, rsem = refs[2 * nT:]
        x, y, c, _ = _place()
        cps = [_rcopy(ins[t], outs[t], ssem.at[t], rsem.at[t], (x, y, 1 - c)) for t in range(nT)]
        for cp in cps:
            cp.start()
        for cp in cps:
            cp.wait()

    return pl.pallas_call(
        body, name=name, in_specs=[ANY] * nT, out_specs=[ANY] * nT,
        out_shape=[jax.ShapeDtypeStruct(h.shape, h.dtype) for h in halves],
        scratch_shapes=[pltpu.SemaphoreType.DMA((nT,)), pltpu.SemaphoreType.DMA((nT,))],
        compiler_params=_params())(*halves)


N_DEVICES = 8


def _spread_copies(b_ref, l_ref, ssem, rsem):
    x, y, c, chips = _place()
    me = 4 * x + 2 * y + c
    pairs = []
    for px, py, pc in [(px, py, pc) for px, py in chips for pc in (c, 1 - c)] + [(x, y, 1 - c)]:
        it = 4 * px + 2 * py + pc
        pairs.append((_rcopy(b_ref, l_ref.at[me], ssem.at[it], rsem.at[me], (px, py, pc)),
                      _rcopy(b_ref, l_ref.at[it], ssem.at[it], rsem.at[it], (px, py, pc))))
    return pairs


def spread_start(name, buf):
    def body(b_ref, l_ref, ssem, rsem, b_out, l_out, token):
        for mine, _ in _spread_copies(b_ref, l_ref, ssem, rsem):
            mine.start()
        token[...] = jnp.zeros_like(token)

    zone = lax.empty((N_DEVICES,) + buf.shape, buf.dtype)
    return pl.pallas_call(
        body, name=name,
        out_shape=(pltpu.SemaphoreType.DMA((N_DEVICES,)), pltpu.SemaphoreType.DMA((N_DEVICES,)),
                   pltpu.HBM(buf.shape, buf.dtype), pltpu.HBM(zone.shape, zone.dtype), jax.ShapeDtypeStruct((8, LANES), F32)),
        in_specs=[HBM, HBM], out_specs=(SEM, SEM, HBM, HBM, pl.BlockSpec(memory_space=pltpu.VMEM)),
        input_output_aliases={0: 2, 1: 3},
        compiler_params=pltpu.CompilerParams(has_side_effects=EFFECT))(_in_hbm(buf), _in_hbm(zone))


def spread_wait(name, buf, zone, ssem, rsem, after):
    def body(b_ref, l_ref, ssem_ref, rsem_ref, after_ref, b_out, l_out):
        for mine, theirs in _spread_copies(b_ref, l_ref, ssem_ref, rsem_ref):
            mine.wait_send()
            theirs.wait_recv()

    return pl.pallas_call(
        body, name=name, out_shape=(pltpu.HBM(buf.shape, buf.dtype), pltpu.HBM(zone.shape, zone.dtype)),
        in_specs=(HBM, HBM, SEM, SEM, ANY), out_specs=(HBM, HBM), input_output_aliases={0: 0, 1: 1},
        compiler_params=pltpu.CompilerParams(has_side_effects=EFFECT))(buf, zone, ssem, rsem, after)


def sum_devices(name, zone):
    _, R, C = zone.shape
    tr = _row_tile(R, C)

    def body(z_ref, o_ref):
        acc = z_ref[0]
        for d in range(1, N_DEVICES):
            acc = acc + z_ref[d]
        o_ref[...] = acc

    return pl.pallas_call(
        body, name=name, grid=(R // tr,), in_specs=[pl.BlockSpec((N_DEVICES, tr, C), lambda i: (0, i, 0))],
        out_specs=pl.BlockSpec((tr, C), lambda i: (i, 0)), out_shape=jax.ShapeDtypeStruct((R, C), F32),
        compiler_params=_params(('parallel',)))(zone)


class _InWindows:
    def __init__(self, FW, LW, H, C):
        gap = LANES - H
        padded = lambda o: o if o < 3 * FW + H else o + gap
        self.width = 3 * FW + LANES + 2 * LW
        self.f_block = 3 * FW // LANES
        self.first = [padded(C * j) // LANES for j in range(N_CHIPS)]
        self.blocks = max(padded(C * (j + 1) - 1) // LANES - self.first[j] + 1 for j in range(N_CHIPS))
        assert all((b + self.blocks) * LANES <= self.width for b in self.first)
        self.cols = self.blocks * LANES
        self.runs = []
        for j in range(N_CHIPS):
            cut = min(max(3 * FW + H - C * j, 0), C)
            spans = [(0, cut), (cut, C)]
            self.runs.append([(t0, t1, padded(C * j + t0) - LANES * self.first[j]) for t0, t1 in spans if t1 > t0])

    def to_window(self, shard, chip):
        def place(j, s):
            parts, pos = [], 0
            for t0, t1, w0 in self.runs[j]:
                parts += [jnp.zeros((s.shape[0], w0 - pos), s.dtype), s[:, t0:t1]]
                pos = w0 + t1 - t0
            parts.append(jnp.zeros((s.shape[0], self.cols - pos), s.dtype))
            return jnp.concatenate([p for p in parts if p.shape[1]], axis=1)
        return lax.switch(chip, [functools.partial(place, j) for j in range(N_CHIPS)], shard)

    def from_window(self, win, chip):
        def take(j, w):
            return jnp.concatenate([w[:, w0:w0 + t1 - t0] for t0, t1, w0 in self.runs[j]], axis=1)
        return lax.switch(chip, [functools.partial(take, j) for j in range(N_CHIPS)], win)

    def _spans(self, j):
        b0, b1 = self.first[j], self.first[j] + self.blocks
        return (b0, min(b1, self.f_block)), b0 <= self.f_block < b1, (max(b0, self.f_block + 1), b1)

    def assemble(self, zone):
        main, f_blk = None, None
        for j in range(N_CHIPS):
            (a0, a1), has_f, (c0, c1) = self._spans(j)
            for p0, p1, shift in ((a0, a1, 0), (c0, c1, 1)):
                if p1 > p0:
                    part = zone[j][:, (p0 - self.first[j]) * LANES:(p1 - self.first[j]) * LANES]
                    part = jnp.pad(part, ((0, 0), ((p0 - shift) * LANES, self.width - LANES - (p1 - shift) * LANES)))
                    main = part if main is None else main + part
            if has_f:
                part = zone[j][:, (self.f_block - self.first[j]) * LANES:(self.f_block - self.first[j] + 1) * LANES]
                f_blk = part if f_blk is None else f_blk + part
        return main, f_blk

    def windows(self, main, f_blk):
        out = []
        for j in range(N_CHIPS):
            (a0, a1), has_f, (c0, c1) = self._spans(j)
            parts = [main[:, a0 * LANES:a1 * LANES]] if a1 > a0 else []
            parts += [f_blk] if has_f else []
            parts += [main[:, (c0 - 1) * LANES:(c1 - 1) * LANES]] if c1 > c0 else []
            out.append(jnp.concatenate(parts, axis=1))
        return jnp.stack(out)


_PACK = 8 * LANES


PACK_ROWS = 256


def _pack(arrs):
    flat = []
    for a in arrs:
        v = a.reshape(-1).astype(F32)
        flat.append(jnp.pad(v, (0, (-v.shape[0]) % _PACK)))
    rows = sum(v.shape[0] for v in flat) // LANES
    flat.append(jnp.zeros(((-rows) % PACK_ROWS) * LANES, F32))
    return jnp.concatenate(flat).reshape(-1, LANES)


def _unpack(buf, shapes):
    out, off = [], 0
    flat = buf.reshape(-1)
    for sh in shapes:
        n = math.prod(sh)
        out.append(flat[off:off + n].reshape(sh))
        off += n + (-n) % _PACK
    return out


def kernel(x, mem, g_mix, w_in, b_f, g_q, g_k, conv_w, conv_b, w_ra, b_ra, w_ri, b_ri, lam, g_fox_out, g_lru_out, w_out, g_xattn, g_mem, w_cq, w_ckv, g_cq, g_ck, w_co, g_ffn, w_gate_up, w_down, loss_target, m_g_mix, m_w_in, m_b_f, m_g_q, m_g_k, m_conv_w, m_conv_b, m_w_ra, m_b_ra, m_w_ri, m_b_ri, m_lam, m_g_fox_out, m_g_lru_out, m_w_out, m_g_xattn, m_g_mem, m_w_cq, m_w_ckv, m_g_cq, m_g_ck, m_w_co, m_g_ffn, m_w_gate_up, m_w_down, v_g_mix, v_w_in, v_b_f, v_g_q, v_g_k, v_conv_w, v_conv_b, v_w_ra, v_b_ra, v_w_ri, v_b_ri, v_lam, v_g_fox_out, v_g_lru_out, v_w_out, v_g_xattn, v_g_mem, v_w_cq, v_w_ckv, v_g_cq, v_g_ck, v_w_co, v_g_ffn, v_w_gate_up, v_w_down):
    given = dict(locals())
    W = {n: given[n][0] for n in WEIGHTS}
    M1 = {n: given['m_' + n][0] for n in WEIGHTS}
    V1 = {n: given['v_' + n][0] for n in WEIGHTS}
    xs, ms, tgt = x[0], mem[0], loss_target[0]
    S, D = xs.shape
    H = W['b_f'].shape[0]
    FW = H * HEAD_DIM
    LW = W['lam'].shape[0]
    nb = W['w_ra'].shape[0]
    XW = W['w_cq'].shape[1]
    F = W['w_down'].shape[0] * N_CHIPS
    IN_W = W['w_in'].shape[1] * N_CHIPS
    assert FW == LW and LW == nb * LANES and IN_W == 3 * FW + H + 2 * LW and H <= 8
    T = _tile(S, (512, 256, 128))
    c_idx = lax.axis_index('c').astype(jnp.int32).reshape(1)
    chip = 2 * lax.axis_index('x') + lax.axis_index('y')
    chip_idx = chip.astype(jnp.int32).reshape(1)
    vec = lambda n: W[n].reshape(1, -1)

    wins = _InWindows(FW, LW, H, W['w_in'].shape[1])
    started = {}
    g_tok = jnp.zeros((1, 1), F32)
    for call, names in (('gather_start_first', ['conv_w', 'w_in']), ('gather_start_rest', BIG[1:])):
        own = [W[n].reshape(-1, LANES) if n == 'conv_w' else W[n].astype(BF16) + g_tok.astype(BF16) for n in names]
        own = [wins.to_window(o, chip) if n == 'w_in' else o for n, o in zip(names, own)]
        ssem, rsem, srcs, zones, tok = gather_start(call, own, [n == 'conv_w' for n in names])
        g_tok = tok[0:1, 0:1]
        started.update({n: (t, srcs[t], zones[t], ssem, rsem) for t, n in enumerate(names)})

    def fetch(n, after):
        t, g_src, g_zone, g_ssem, g_rsem = started[n]
        src, zone = gather_wait('gather_wait_' + n, t, g_src, g_zone, g_ssem, g_rsem, after, n == 'conv_w')
        if n != 'conv_w':
            zone = pair_swap('pair_swap_' + n, zone)
        return lax.dynamic_update_index_in_dim(zone, src, chip, 0)

    def fetch_begin(n, after):
        t, g_src, g_zone, g_ssem, g_rsem = started[n]
        src, zone = gather_wait('gather_wait_' + n, t, g_src, g_zone, g_ssem, g_rsem, after, False)
        ssem, rsem, zone, tok = swap_start('swap_start_' + n, zone)
        return src, zone, ssem, rsem, tok[0:1, 0:1]

    def fetch_end(n, begun, after):
        src, zone, ssem, rsem, _ = begun
        return lax.dynamic_update_index_in_dim(swap_wait('swap_wait_' + n, zone, ssem, rsem, after), src, chip, 0)

    b_f_pad = jnp.pad(vec('b_f'), ((0, 0), (0, LANES - H)))
    u_off, g_off = 3 * FW // LANES, (3 * FW + LW) // LANES

    h1 = norm_fwd('norm_mix', xs, vec('g_mix') + g_tok[0:1, 0:1])
    conv_full = fetch('conv_w', h1).reshape(N_CHIPS, CONV_W, LW // N_CHIPS).transpose(1, 0, 2).reshape(CONV_W, LW)
    w5, wf = wins.assemble(fetch('w_in', [h1, M1['w_in'], V1['w_in']]))
    proj = _mm('proj_in', h1, w5, 'nn', F32)
    f_raw = _mm('proj_f', h1, wf, 'nn', F32)
    qn, kn, vb = qkv_fwd(proj, vec('g_q'), vec('g_k'), FW)
    cc = fgate_fwd(f_raw, b_f_pad)
    ct = cc[:, :8].T
    o_fox, lse = fox_fwd(qn, kn, vb, cc, ct, T)
    lru_w = (conv_full, vec('conv_b'), W['w_ra'], vec('b_ra'), W['w_ri'], vec('b_ri'), vec('lam'))
    y_lru = lru_fwd(proj, *lru_w, u_off, g_off)
    mixn = mix_fwd(o_fox, y_lru, vec('g_fox_out'), vec('g_lru_out'))
    w_out_f = fetch('w_out', mixn).reshape(2 * FW, D)
    begun = {n: fetch_begin(n, mixn) for n in ('w_cq', 'w_ckv', 'w_co')}
    x1 = _mm('proj_out', mixn, w_out_f, 'nn', F32, res=xs)

    hq = norm_fwd('norm_xq', x1, vec('g_xattn'))
    mn = norm_fwd('norm_mem', ms, vec('g_mem'))
    w_cq_f = fetch_end('w_cq', begun['w_cq'], hq).reshape(D, XW)
    w_ckv_f = fetch_end('w_ckv', begun['w_ckv'], hq).reshape(D, 2 * XW)
    begun['w_gate_up'] = fetch_begin('w_gate_up', hq)
    cq_raw = _mm('proj_cq', hq, w_cq_f, 'nn', F32)
    ckv = _mm('proj_ckv', mn, w_ckv_f, 'nn', F32)
    o_x = xattn_fwd(cq_raw, ckv, vec('g_cq') + begun['w_gate_up'][-1], vec('g_ck'))
    w_co_g = fetch_end('w_co', begun['w_co'], o_x)
    x2 = _mm_colsharded('proj_co', o_x, w_co_g, F32, res=x1)

    hf = norm_fwd('norm_ffn', x2, vec('g_ffn'))
    begun['w_down'] = fetch_begin('w_down', hf)
    w_gu_g = fetch_end('w_gate_up', begun['w_gate_up'], hf)
    gu, act = gate_up_fwd(hf, w_gu_g, F)
    w_down_f = fetch_end('w_down', begun['w_down'], act).reshape(F, D)
    dy, dyb, loss_blk = down_fwd_loss(act, w_down_f, x2, tgt)

    gw, pending = {}, []

    def reduce_begin(n, g):
        sp = g.reshape(N_CHIPS, 2, g.shape[1] // 2, g.shape[2])
        ssem, rsem, sp, zone, tok = scatter_start('scatter_start_' + n, sp)
        pending.append((n, sp, zone, ssem, rsem))
        return tok[0:1, 0:1]

    t_down = reduce_begin('w_down', _mm('bwd_down_w', act, dyb, 'tn', BF16).reshape(N_CHIPS, F // N_CHIPS, D))
    dgu = down_bwd_x(dyb, w_down_f, gu, t_down)
    dhf = _mm_colsharded_t('bwd_gate_up_x', dgu, w_gu_g, F32)
    t_gu = reduce_begin('w_gate_up', _mm_grad_colsharded('bwd_gate_up_w', hf, dgu, N_CHIPS, BF16))
    dx2, dx2b, gw['g_ffn'] = norm_bwd('norm_ffn_bwd', x2, vec('g_ffn') + t_down + t_gu, dhf, res=dy)

    do_x = _mm_colsharded_t('bwd_co_x', dx2b, w_co_g, BF16)
    t_co = reduce_begin('w_co', _mm_grad_colsharded('bwd_co_w', o_x, dx2b, N_CHIPS, BF16))
    dcq_raw, dckv, gw['g_cq'], gw['g_ck'] = xattn_bwd(cq_raw, ckv, vec('g_cq') + t_co, vec('g_ck'), do_x)
    dhq = _mm('bwd_cq_x', dcq_raw, w_cq_f, 'nt', F32)
    t_cq = reduce_begin('w_cq', _mm('bwd_cq_w', hq, dcq_raw, 'tn', BF16).reshape(N_CHIPS, D // N_CHIPS, XW))
    dmn = _mm('bwd_ckv_x', dckv, w_ckv_f, 'nt', F32)
    t_ckv = reduce_begin('w_ckv', _mm('bwd_ckv_w', mn, dckv, 'tn', BF16).reshape(N_CHIPS, D // N_CHIPS, 2 * XW))
    (gw['g_mem'],) = norm_bwd('norm_mem_bwd', ms, vec('g_mem'), dmn, want_dx=False)
    dx1, dx1b, gw['g_xattn'] = norm_bwd('norm_xq_bwd', x1, vec('g_xattn') + t_cq + t_ckv, dhq, res=dx2)

    dmix = _mm('bwd_out_x', dx1b, w_out_f, 'nt', F32)
    t_out = reduce_begin('w_out', _mm('bwd_out_w', mixn, dx1b, 'tn', BF16).reshape(N_CHIPS, 2 * FW // N_CHIPS, D))
    do_fox, delta, dy_lru, gw['g_fox_out'], gw['g_lru_out'] = mix_bwd(o_fox, y_lru, vec('g_fox_out') + t_out,
                                                                     vec('g_lru_out'), dmix)
    (du, dgate, gw['conv_w'], gw['conv_b'], gw['w_ra'], gw['b_ra'], gw['w_ri'], gw['b_ri'],
     gw['lam']) = lru_bwd(proj, dy_lru, *lru_w, u_off, g_off)
    early = [n for n in SMALL if n not in ('g_q', 'g_k', 'b_f', 'g_mix')]
    late = [n for n in SMALL if n not in early]
    e_ssem, e_rsem, e_buf, e_zone, e_tok = spread_start('spread_start_early', _pack([gw[n] for n in early]))
    dqn, delta2 = fox_bwd_q(qn, kn, vb, do_fox, cc, ct, lse, delta, T)
    dkn, dv, dct = fox_bwd_kv(qn, kn, vb, do_fox, cc, ct, lse, delta2, T)
    dq, dk, gw['g_q'], gw['g_k'] = qkv_bwd(proj, vec('g_q') + e_tok[0:1, 0:1], vec('g_k'), dqn, dkn, FW)
    dc = jnp.pad(dct.reshape(H, S).T, ((0, 0), (0, LANES - H)))
    df, db_f = fgate_bwd(f_raw, b_f_pad, dc, H)
    gw['b_f'] = db_f[:, :H]
    dproj = jnp.concatenate([dq, dk, dv, du, dgate], axis=1)
    dw5 = _mm('bwd_in_w', h1, dproj, 'tn', BF16)
    dwf = _mm('bwd_f_w', h1, df, 'tn', BF16)
    t_in = reduce_begin('w_in', wins.windows(dw5, dwf))
    dh_a = _mm('bwd_f_x', df, wf, 'nt', F32)
    dh1 = _mm('bwd_in_x', dproj, w5, 'nt', F32, res=dh_a)
    grad_x, _, gw['g_mix'] = norm_bwd('norm_mix_bwd', xs, vec('g_mix') + t_in, dh1, res=dx1)
    l_ssem, l_rsem, l_buf, l_zone, _ = spread_start('spread_start_late',
                                                    _pack([gw[n] for n in late] + [loss_blk[0:1, 0:1]]))

    grads, delta_w, new_m, new_v = {}, {}, {}, {}

    def land(n, part, zone, ssem, rsem, done):
        part, landed = scatter_wait('scatter_wait_' + n, part, zone, ssem, rsem, done)
        mine = sum_parts('sum_parts_' + n, part, landed, chip_idx, c_idx)
        (other,) = pair_join('pair_join_' + n, [mine])
        if n == 'w_in':
            mine, other = wins.from_window(mine, chip), wins.from_window(other, chip)
        grads[n], delta_w[n], new_m[n], new_v[n] = adamw_halves('adamw_' + n, W[n], mine, other, M1[n], V1[n], c_idx)
        return delta_w[n]

    done = grad_x
    for entry in pending[:-1]:
        done = land(*entry, done)

    device = 4 * lax.axis_index('x') + 2 * lax.axis_index('y') + lax.axis_index('c')
    summed = {}
    for tag, names, buf, zone, ssem, rsem in (('early', early, e_buf, e_zone, e_ssem, e_rsem),
                                              ('late', late + ['loss'], l_buf, l_zone, l_ssem, l_rsem)):
        buf, zone = spread_wait('spread_wait_' + tag, buf, zone, ssem, rsem, done)
        total = sum_devices('sum_small_' + tag, lax.dynamic_update_index_in_dim(zone, buf, device, 0))
        summed.update(zip(names, _unpack(total, [gw[n].shape if n != 'loss' else (1, 1) for n in names])))
    loss = summed['loss'].reshape(())
    for n in SMALL:
        g = summed[n]
        grads[n] = g.reshape(W[n].shape) if n != 'conv_w' else lax.dynamic_slice_in_dim(
            g, chip * (LW // N_CHIPS), LW // N_CHIPS, axis=1)
    packs = [_pack([d[n] for n in SMALL]) for d in (W, grads, M1, V1)]
    shapes = [W[n].shape for n in SMALL]
    small_out = adamw('adamw_small', *packs)
    for d, res in zip((delta_w, new_m, new_v), small_out):
        d.update(zip(SMALL, _unpack(res, shapes)))
    land(*pending[-1], small_out[0])

    lead = lambda d: [d[n][None] for n in WEIGHTS]
    return (loss, grad_x[None], *lead(grads), *lead(delta_w), *lead(new_m), *lead(new_v))
```

```python
import functools
import math

import jax
import jax.numpy as jnp
from jax import lax
from jax.experimental import pallas as pl
from jax.experimental.pallas import tpu as pltpu

F32 = jnp.float32
BF16 = jnp.bfloat16
HEAD_DIM = 128
LANES = 128
LRU_C = 8.0
RMS_EPS = 1e-6
CONV_W = 4
ADAM_LR = 0.001
ADAM_B1 = 0.9
ADAM_B2 = 0.999
ADAM_EPS = 1e-08
ADAM_WD = 0.01
ADAM_STEP = 10
VMEM_LIMIT = 56 * 1024 * 1024
N_CHIPS = 4
MESH = pl.DeviceIdType.MESH
ANY = pl.BlockSpec(memory_space=pl.ANY)

WEIGHTS = ['g_mix', 'w_in', 'b_f', 'g_q', 'g_k', 'conv_w', 'conv_b', 'w_ra', 'b_ra', 'w_ri', 'b_ri', 'lam',
           'g_fox_out', 'g_lru_out', 'w_out', 'g_xattn', 'g_mem', 'w_cq', 'w_ckv', 'g_cq', 'g_ck', 'w_co', 'g_ffn',
           'w_gate_up', 'w_down']
BIG = ['w_in', 'w_out', 'w_cq', 'w_ckv', 'w_co', 'w_gate_up', 'w_down']
SMALL = [n for n in WEIGHTS if n not in BIG]


def _params(sem=None):
    if sem is None:
        return pltpu.CompilerParams(vmem_limit_bytes=VMEM_LIMIT)
    return pltpu.CompilerParams(dimension_semantics=sem, vmem_limit_bytes=VMEM_LIMIT)


def _tile(n, cands):
    for t in cands:
        if n % t == 0:
            return t
    return n


ROW_BLOCK_BYTES = 1 << 20


def _row_tile(n_rows, n_cols, min_rows=8):
    cands = [t for t in (512, 256, 128, 64, 32, 16, 8) if t >= min_rows and t * n_cols * 4 <= ROW_BLOCK_BYTES]
    return _tile(n_rows, cands or [min_rows])


def _sigmoid(z):
    return 1.0 / (1.0 + jnp.exp(-z))


def _softplus(z):
    return jnp.maximum(z, 0.0) + jnp.log(1.0 + jnp.exp(-jnp.abs(z)))


def _neg_expm1(z):
    series = -z * (1.0 + z * (0.5 + z * (1.0 / 6.0 + z * (1.0 / 24.0 + z * (1.0 / 120.0)))))
    return jnp.where(z > -0.25, series, 1.0 - jnp.exp(z))


_GELU_K = math.sqrt(2.0 / math.pi)


def _gelu_and_grad(z):
    inner = _GELU_K * (z + 0.044715 * z * z * z)
    t = jnp.tanh(inner)
    g = 0.5 * z * (1.0 + t)
    dg = 0.5 * (1.0 + t) + 0.5 * z * (1.0 - t * t) * _GELU_K * (1.0 + 3.0 * 0.044715 * z * z)
    return g, dg


def _rms(xv, g):
    r = lax.rsqrt(jnp.mean(xv * xv, axis=-1, keepdims=True) + RMS_EPS)
    return xv * r * g


def _rms_bwd(xv, g, dy):
    r = lax.rsqrt(jnp.mean(xv * xv, axis=-1, keepdims=True) + RMS_EPS)
    xh = xv * r
    dyg = dy * g
    dx = r * (dyg - xh * jnp.mean(dyg * xh, axis=-1, keepdims=True))
    return dx, jnp.sum(dy * xh, axis=0, keepdims=True)


def _heads(fn, n_heads, *arrs):
    outs = [fn(*[a[:, h * HEAD_DIM:(h + 1) * HEAD_DIM] for a in arrs]) for h in range(n_heads)]
    first = jnp.concatenate([o[0] for o in outs], axis=1) if n_heads > 1 else outs[0][0]
    rest = [functools.reduce(lambda p, q: p + q, [o[i] for o in outs]) for i in range(1, len(outs[0]))]
    return (first, *rest)


def _split3(v):
    hi = v.astype(BF16)
    r1 = v - hi.astype(F32)
    mid = r1.astype(BF16)
    lo = (r1 - mid.astype(F32)).astype(BF16)
    return hi, mid, lo


def _acc_out(ref, first, val):
    @pl.when(first)
    def _():
        ref[...] = val

    @pl.when(jnp.logical_not(first))
    def _():
        ref[...] += val


_DIMS = {'nn': (((1,), (0,)), ((), ())), 'nt': (((1,), (1,)), ((), ())), 'tn': (((0,), (0,)), ((), ()))}


MM_VMEM_BYTES = 36 * 1024 * 1024


MXU_FLOPS = 800e12
HBM_BYTES_S = 3.2e12
VMEM_ADD_BYTES_S = 8e12
STEP_S = 0.35e-6


def _k_tile(K, tm, tn, a, b, o_dtype, res):
    fixed = tm * tn * (2 * jnp.dtype(o_dtype).itemsize + 4 + (8 if res is not None else 0))
    per_k = 2 * (tm * a.dtype.itemsize + tn * b.dtype.itemsize)
    per_k += 2 * tm * (a.dtype.itemsize > 2) + 2 * tn * (b.dtype.itemsize > 2)
    units = K // LANES
    for d in sorted((d for d in range(1, units + 1) if units % d == 0), reverse=True):
        if fixed + d * LANES * per_k <= MM_VMEM_BYTES:
            return d * LANES
    return None


def _mm_tiles(M, N, K, k_span, a, b, o_dtype, res, tn_cands=(2048, 1024, 512, 256, 128)):
    best = None
    for tm in (2048, 1024, 512, 256, 128):
        for tn in tn_cands:
            if M % tm or N % tn:
                continue
            tk = _k_tile(k_span, tm, tn, a, b, o_dtype, res)
            if tk is None:
                continue
            nk = K // tk
            traffic = (M * K * a.dtype.itemsize * (N // tn) + K * N * b.dtype.itemsize * (M // tm)
                       + M * N * (jnp.dtype(o_dtype).itemsize + (4 if res is not None else 0)))
            work = 2.0 * M * N * K / MXU_FLOPS + (M * N * 4 * nk / VMEM_ADD_BYTES_S if nk > 1 else 0.0)
            t = max(work, traffic / HBM_BYTES_S) + (M // tm) * (N // tn) * nk * STEP_S
            if best is None or t < best[0]:
                best = (t, tm, tn, tk)
    assert best is not None, (M, N, K)
    return best[1:]


def _mm_call(name, a, b, mode, grid, a_spec, b_spec, o_spec, o_shape, o_dtype, acc_shape, res=None):
    nk = grid[2]
    dn = _DIMS[mode]

    def body(*refs):
        a_ref, b_ref = refs[:2]
        r_ref = refs[2] if res is not None else None
        o_ref = refs[3] if res is not None else refs[2]
        part = lax.dot_general(a_ref[...].astype(BF16), b_ref[...].astype(BF16), dn, preferred_element_type=F32)

        def finish(r):
            if r_ref is not None:
                r = r + r_ref[...]
            o_ref[...] = r.astype(o_dtype)

        if nk == 1:
            finish(part)
            return
        acc = refs[-1]
        k = pl.program_id(2)

        @pl.when(k == 0)
        def _():
            acc[...] = part

        @pl.when(k > 0)
        def _():
            acc[...] += part

        @pl.when(k == nk - 1)
        def _():
            finish(acc[...])

    ins = [a, b] + ([] if res is None else [res])
    specs = [a_spec, b_spec] + ([] if res is None else [o_spec])
    return pl.pallas_call(
        body, name=name, grid=grid, in_specs=specs, out_specs=o_spec,
        out_shape=jax.ShapeDtypeStruct(o_shape, o_dtype),
        scratch_shapes=[] if nk == 1 else [pltpu.VMEM(acc_shape, F32)],
        compiler_params=_params(('parallel', 'parallel', 'arbitrary')))(*ins)


def _mm(name, a, b, mode, o_dtype, res=None):
    if mode == 'tn':
        K, M = a.shape
    else:
        M, K = a.shape
    N = b.shape[0] if mode == 'nt' else b.shape[1]
    tm, tn, tk = _mm_tiles(M, N, K, K, a, b, o_dtype, res)
    a_spec = (pl.BlockSpec((tk, tm), lambda m, n, k: (k, m)) if mode == 'tn'
              else pl.BlockSpec((tm, tk), lambda m, n, k: (m, k)))
    b_spec = (pl.BlockSpec((tn, tk), lambda m, n, k: (n, k)) if mode == 'nt'
              else pl.BlockSpec((tk, tn), lambda m, n, k: (k, n)))
    o_spec = pl.BlockSpec((tm, tn), lambda m, n, k: (m, n))
    return _mm_call(name, a, b, mode, (M // tm, N // tn, K // tk), a_spec, b_spec, o_spec, (M, N), o_dtype,
                    (tm, tn), res)


def _mm_colsharded(name, a, w, o_dtype, res=None):
    M, K = a.shape
    J, _, Nj = w.shape
    tm, tn, tk = _mm_tiles(M, J * Nj, K, K, a, w, o_dtype, res,
                           tn_cands=[t for t in (2816, 1408, 1024, 512, 256, 128) if Nj % t == 0])
    per = Nj // tn
    return _mm_call(name, a, w, 'nn', (M // tm, J * per, K // tk),
                    pl.BlockSpec((tm, tk), lambda m, n, k: (m, k)),
                    pl.BlockSpec((None, tk, tn), lambda m, n, k: (n // per, k, n % per)),
                    pl.BlockSpec((tm, tn), lambda m, n, k: (m, n)), (M, J * Nj), o_dtype, (tm, tn), res)


def _planes_spec(arr, rows, cols, row_of, col_of):
    if arr.ndim == 2:
        return pl.BlockSpec((rows, cols), lambda m, n, k: (row_of(m, n, k), col_of(m, n, k)))
    per_plane = arr.shape[2] // cols
    return pl.BlockSpec((None, rows, cols),
                        lambda m, n, k: (col_of(m, n, k) // per_plane, row_of(m, n, k), col_of(m, n, k) % per_plane))


def _mm_colsharded_t(name, a, w, o_dtype):
    M = a.shape[-2]
    J, K, Nj = w.shape
    tm, tn, tk = _mm_tiles(M, K, J * Nj, Nj, a, w, o_dtype, None)
    per = Nj // tk
    return _mm_call(name, a, w, 'nt', (M // tm, K // tn, J * per),
                    _planes_spec(a, tm, tk, lambda m, n, k: m, lambda m, n, k: k),
                    pl.BlockSpec((None, tn, tk), lambda m, n, k: (k // per, n, k % per)),
                    pl.BlockSpec((tm, tn), lambda m, n, k: (m, n)), (M, K), o_dtype, (tm, tn))


def _mm_grad_colsharded(name, a, dy, J, o_dtype):
    S, M = a.shape
    Nj = dy.shape[-1] * (dy.shape[0] if dy.ndim == 3 else 1) // J
    tm, tn, tk = _mm_tiles(M, J * Nj, S, S, a, dy, o_dtype, None,
                           tn_cands=[t for t in (2816, 1408, 1024, 512, 256, 128) if Nj % t == 0])
    per = Nj // tn
    return _mm_call(name, a, dy, 'tn', (M // tm, J * per, S // tk),
                    pl.BlockSpec((tk, tm), lambda m, n, k: (k, m)),
                    _planes_spec(dy, tk, tn, lambda m, n, k: k, lambda m, n, k: n),
                    pl.BlockSpec((None, tm, tn), lambda m, n, k: (n // per, m, n % per)), (J, M, Nj), o_dtype, (tm, tn))


def _rows_call(name, body, n_rows, tr, ins, outs):
    return pl.pallas_call(
        body, name=name, grid=(n_rows // tr,), in_specs=[s for _, s in ins], out_specs=[s for _, _, s in outs],
        out_shape=[jax.ShapeDtypeStruct(sh, dt) for sh, dt, _ in outs],
        compiler_params=_params(('arbitrary',)))(*[a for a, _ in ins])


def _rb(tr, w, cb=0):
    return pl.BlockSpec((tr, w), lambda i: (i, cb))


def _fb(shape):
    nd = len(shape)
    return pl.BlockSpec(shape, lambda i: (0,) * nd)


def norm_fwd(name, xv, g):
    S, D = xv.shape
    tr = _tile(S, (256, 128))

    def body(x_ref, g_ref, o_ref):
        o_ref[...] = _rms(x_ref[...], g_ref[...]).astype(BF16)

    return _rows_call(name, body, S, tr, [(xv, _rb(tr, D)), (g, _fb((1, D)))], [((S, D), BF16, _rb(tr, D))])[0]


def norm_bwd(name, xv, g, dy, res=None, want_dx=True):
    S, D = xv.shape
    tr = _tile(S, (256, 128))

    def body(*refs):
        if res is None:
            x_ref, g_ref, dy_ref = refs[:3]
            outs = refs[3:]
            r_ref = None
        else:
            x_ref, g_ref, dy_ref, r_ref = refs[:4]
            outs = refs[4:]
        dx, dg = _rms_bwd(x_ref[...], g_ref[...], dy_ref[...])
        if r_ref is not None:
            dx = dx + r_ref[...]
        if want_dx:
            outs[0][...] = dx
            outs[1][...] = dx.astype(BF16)
        _acc_out(outs[-1], pl.program_id(0) == 0, dg)

    ins = [(xv, _rb(tr, D)), (g, _fb((1, D))), (dy, _rb(tr, D))] + ([] if res is None else [(res, _rb(tr, D))])
    outs = ([((S, D), F32, _rb(tr, D)), ((S, D), BF16, _rb(tr, D))] if want_dx else []) + [((1, D), F32, _fb((1, D)))]
    return _rows_call(name, body, S, tr, ins, outs)


def qkv_fwd(proj, g_q, g_k, FW):
    S = proj.shape[0]
    H = FW // HEAD_DIM
    tr = _tile(S, (256, 128))

    def body(q_ref, k_ref, v_ref, gq_ref, gk_ref, qo, ko, vo):
        qo[...] = _heads(lambda t: (_rms(t, gq_ref[...]),), H, q_ref[...])[0].astype(BF16)
        ko[...] = _heads(lambda t: (_rms(t, gk_ref[...]),), H, k_ref[...])[0].astype(BF16)
        vo[...] = v_ref[...].astype(BF16)

    o = ((S, FW), BF16, _rb(tr, FW))
    return _rows_call('qkv_fwd', body, S, tr,
                      [(proj, _rb(tr, FW, 0)), (proj, _rb(tr, FW, 1)), (proj, _rb(tr, FW, 2)),
                       (g_q, _fb((1, HEAD_DIM))), (g_k, _fb((1, HEAD_DIM)))], [o, o, o])


def qkv_bwd(proj, g_q, g_k, dqn, dkn, FW):
    S = proj.shape[0]
    H = FW // HEAD_DIM
    tr = _tile(S, (256, 128))

    def body(q_ref, k_ref, gq_ref, gk_ref, dq_ref, dk_ref, dqo, dko, dgq, dgk):
        dq, gq = _heads(lambda t, d: _rms_bwd(t, gq_ref[...], d), H, q_ref[...], dq_ref[...])
        dk, gk = _heads(lambda t, d: _rms_bwd(t, gk_ref[...], d), H, k_ref[...], dk_ref[...])
        dqo[...] = dq.astype(BF16)
        dko[...] = dk.astype(BF16)
        first = pl.program_id(0) == 0
        _acc_out(dgq, first, gq)
        _acc_out(dgk, first, gk)

    o = ((S, FW), BF16, _rb(tr, FW))
    og = ((1, HEAD_DIM), F32, _fb((1, HEAD_DIM)))
    return _rows_call('qkv_bwd', body, S, tr,
                      [(proj, _rb(tr, FW, 0)), (proj, _rb(tr, FW, 1)), (g_q, _fb((1, HEAD_DIM))),
                       (g_k, _fb((1, HEAD_DIM))), (dqn, _rb(tr, FW)), (dkn, _rb(tr, FW))], [o, o, og, og])


def _tri(n, upper):
    r = lax.broadcasted_iota(jnp.int32, (n, n), 0)
    c = lax.broadcasted_iota(jnp.int32, (n, n), 1)
    return jnp.where((c >= r) if upper else (c <= r), 1.0, 0.0).astype(BF16)


def _blocked_cumsum(val, S, blk, reverse):
    tri = _tri(blk, reverse)
    order = range(S // blk - 1, -1, -1) if reverse else range(S // blk)
    carry = jnp.zeros((1, LANES), F32)
    outs = {}
    for bi in order:
        part = val[bi * blk:(bi + 1) * blk]
        acc = carry
        for piece in _split3(part):
            acc = acc + jnp.dot(tri, piece, preferred_element_type=F32)
        outs[bi] = acc
        carry = carry + jnp.sum(part, axis=0, keepdims=True)
    return jnp.concatenate([outs[bi] for bi in range(S // blk)], axis=0)


def fgate_fwd(f_raw, b_f_pad):
    S = f_raw.shape[0]
    blk = _tile(S, (256, 128))

    def body(f_ref, b_ref, c_ref):
        z = f_ref[...] + b_ref[...]
        c_ref[...] = _blocked_cumsum(-_softplus(-z), S, blk, False)

    return pl.pallas_call(body, name='fgate_fwd', grid=(1,), in_specs=[_fb((S, LANES)), _fb((1, LANES))],
                          out_specs=_fb((S, LANES)), out_shape=jax.ShapeDtypeStruct((S, LANES), F32),
                          compiler_params=_params(('arbitrary',)))(f_raw, b_f_pad)


def fgate_bwd(f_raw, b_f_pad, dc, H):
    S = f_raw.shape[0]
    blk = _tile(S, (256, 128))

    def body(f_ref, b_ref, dc_ref, df_ref, db_ref):
        z = f_ref[...] + b_ref[...]
        dlogf = _blocked_cumsum(dc_ref[...], S, blk, True)
        lane = lax.broadcasted_iota(jnp.int32, (S, LANES), 1)
        df = jnp.where(lane < H, dlogf * _sigmoid(-z), 0.0)
        df_ref[...] = df.astype(BF16)
        db_ref[...] = jnp.sum(df, axis=0, keepdims=True)

    return pl.pallas_call(body, name='fgate_bwd', grid=(1,),
                          in_specs=[_fb((S, LANES)), _fb((1, LANES)), _fb((S, LANES))],
                          out_specs=[_fb((S, LANES)), _fb((1, LANES))],
                          out_shape=[jax.ShapeDtypeStruct((S, LANES), BF16), jax.ShapeDtypeStruct((1, LANES), F32)],
                          compiler_params=_params(('arbitrary',)))(f_raw, b_f_pad, dc)


def _fox_logits(q, k, c_blk, ct_blk, h, T, diagonal):
    s = lax.dot_general(q, k, _DIMS['nt'], preferred_element_type=F32) * (1.0 / math.sqrt(HEAD_DIM))
    lane = lax.broadcasted_iota(jnp.int32, c_blk.shape, 1)
    cq = jnp.sum(jnp.where(lane == h, c_blk, 0.0), axis=1, keepdims=True)
    sub = lax.broadcasted_iota(jnp.int32, ct_blk.shape, 0)
    ck = jnp.sum(jnp.where(sub == h, ct_blk, 0.0), axis=0, keepdims=True)
    s = s + cq - ck
    if not diagonal:
        return s
    rows = lax.broadcasted_iota(jnp.int32, (T, T), 0)
    cols = lax.broadcasted_iota(jnp.int32, (T, T), 1)
    return jnp.where(cols <= rows, s, -jnp.inf)


def _below_and_on_diagonal(q_blk, k_blk, step):
    @pl.when(k_blk < q_blk)
    def _():
        step(False)

    @pl.when(k_blk == q_blk)
    def _():
        step(True)


def fox_fwd(qn, kn, vb, c, ct, T):
    S, FW = qn.shape
    H = FW // HEAD_DIM
    Hp = ct.shape[0]
    n = S // T

    HB = _tile(H, (8, 4, 2, 1))
    W2 = HB * HEAD_DIM

    def body(q_ref, k_ref, v_ref, c_ref, ct_ref, o_ref, lse_ref, m_s, l_s, acc_s):
        hb, i, j = pl.program_id(0), pl.program_id(1), pl.program_id(2)

        @pl.when(j == 0)
        def _():
            m_s[...] = jnp.full_like(m_s, -jnp.inf)
            l_s[...] = jnp.zeros_like(l_s)
            acc_s[...] = jnp.zeros_like(acc_s)

        def step(diagonal):
            for hh in range(HB):
                sl = slice(hh * HEAD_DIM, (hh + 1) * HEAD_DIM)
                s = _fox_logits(q_ref[:, sl], k_ref[:, sl], c_ref[...], ct_ref[...], hb * HB + hh, T, diagonal)
                m_old = m_s[hh]
                m_new = jnp.maximum(m_old, jnp.max(s, axis=1, keepdims=True))
                alpha = jnp.exp(m_old - m_new)
                p = jnp.exp(s - m_new)
                l_s[hh] = alpha * l_s[hh] + jnp.sum(p, axis=1, keepdims=True)
                acc_s[hh] = alpha * acc_s[hh] + jnp.dot(p.astype(BF16), v_ref[:, sl], preferred_element_type=F32)
                m_s[hh] = m_new

        _below_and_on_diagonal(i, j, step)

        @pl.when(j == i)
        def _():
            for hh in range(HB):
                o_ref[:, hh * HEAD_DIM:(hh + 1) * HEAD_DIM] = acc_s[hh] / l_s[hh]
                lse_ref[hh] = jnp.broadcast_to(m_s[hh] + jnp.log(l_s[hh]), (T, LANES))

    qs = pl.BlockSpec((T, W2), lambda h, i, j: (i, h))
    ks = pl.BlockSpec((T, W2), lambda h, i, j: (jnp.minimum(j, i), h))
    return pl.pallas_call(
        body, name='fox_fwd', grid=(H // HB, n, n),
        in_specs=[qs, ks, ks, pl.BlockSpec((T, LANES), lambda h, i, j: (i, 0)),
                  pl.BlockSpec((Hp, T), lambda h, i, j: (0, jnp.minimum(j, i)))],
        out_specs=[qs, pl.BlockSpec((HB, T, LANES), lambda h, i, j: (h, i, 0))],
        out_shape=[jax.ShapeDtypeStruct((S, FW), F32), jax.ShapeDtypeStruct((H, S, LANES), F32)],
        scratch_shapes=[pltpu.VMEM((HB, T, 1), F32), pltpu.VMEM((HB, T, 1), F32), pltpu.VMEM((HB, T, HEAD_DIM), F32)],
        compiler_params=_params(('parallel', 'parallel', 'arbitrary')))(qn, kn, vb, c, ct)


def _fox_p_ds(q_ref, k_ref, v_ref, do_ref, c_ref, ct_ref, lse_ref, dl_ref, h, T, diagonal):
    s = _fox_logits(q_ref[...], k_ref[...], c_ref[...], ct_ref[...], h, T, diagonal)
    p = jnp.exp(s - jnp.tile(lse_ref[...], (1, T // LANES)))
    dp = lax.dot_general(do_ref[...], v_ref[...], _DIMS['nt'], preferred_element_type=F32)
    ds = p * (dp - jnp.tile(dl_ref[...], (1, T // LANES)))
    return p, dp, ds


def fox_bwd_q(qn, kn, vb, do, c, ct, lse, dl, T):
    S, FW = qn.shape
    H = FW // HEAD_DIM
    Hp = ct.shape[0]
    n = S // T
    HB = _tile(H, (8, 4, 2, 1))
    W2 = HB * HEAD_DIM

    def body(q_ref, k_ref, v_ref, do_ref, c_ref, ct_ref, lse_ref, dl_ref, dq_ref, dl2_ref, acc_s, rs_s):
        hb, i, j = pl.program_id(0), pl.program_id(1), pl.program_id(2)

        @pl.when(j == 0)
        def _():
            acc_s[...] = jnp.zeros_like(acc_s)
            rs_s[...] = jnp.zeros_like(rs_s)

        def step(diagonal):
            for hh in range(HB):
                sl = slice(hh * HEAD_DIM, (hh + 1) * HEAD_DIM)
                p, dp, ds = _fox_p_ds(q_ref.at[:, sl], k_ref.at[:, sl], v_ref.at[:, sl], do_ref.at[:, sl], c_ref, ct_ref,
                                      lse_ref.at[hh], dl_ref.at[hh], hb * HB + hh, T, diagonal)
                acc_s[hh] += jnp.dot(ds.astype(BF16), k_ref[:, sl], preferred_element_type=F32)
                rs_s[hh] += jnp.sum(p * dp, axis=1, keepdims=True)

        _below_and_on_diagonal(i, j, step)

        @pl.when(j == i)
        def _():
            for hh in range(HB):
                dq_ref[:, hh * HEAD_DIM:(hh + 1) * HEAD_DIM] = acc_s[hh] * (1.0 / math.sqrt(HEAD_DIM))
                dl2_ref[hh] = jnp.broadcast_to(rs_s[hh], (T, LANES))

    qs = pl.BlockSpec((T, W2), lambda h, i, j: (i, h))
    ks = pl.BlockSpec((T, W2), lambda h, i, j: (jnp.minimum(j, i), h))
    st = pl.BlockSpec((HB, T, LANES), lambda h, i, j: (h, i, 0))
    return pl.pallas_call(
        body, name='fox_bwd_q', grid=(H // HB, n, n),
        in_specs=[qs, ks, ks, qs, pl.BlockSpec((T, LANES), lambda h, i, j: (i, 0)),
                  pl.BlockSpec((Hp, T), lambda h, i, j: (0, jnp.minimum(j, i))), st, st],
        out_specs=[qs, st], out_shape=[jax.ShapeDtypeStruct((S, FW), F32), jax.ShapeDtypeStruct((H, S, LANES), F32)],
        scratch_shapes=[pltpu.VMEM((HB, T, HEAD_DIM), F32), pltpu.VMEM((HB, T, 1), F32)],
        compiler_params=_params(('parallel', 'parallel', 'arbitrary')))(qn, kn, vb, do, c, ct, lse, dl)


def fox_bwd_kv(qn, kn, vb, do, c, ct, lse, dl, T):
    S, FW = qn.shape
    H = FW // HEAD_DIM
    Hp = ct.shape[0]
    n = S // T

    HB = _tile(H, (8, 4, 2, 1))
    W2 = HB * HEAD_DIM

    def body(q_ref, k_ref, v_ref, do_ref, c_ref, ct_ref, lse_ref, dl_ref, dk_ref, dv_ref, dc_ref, dk_s, dv_s, dc_s):
        hb, j, i = pl.program_id(0), pl.program_id(1), pl.program_id(2)

        @pl.when(i == 0)
        def _():
            dk_s[...] = jnp.zeros_like(dk_s)
            dv_s[...] = jnp.zeros_like(dv_s)
            dc_s[...] = jnp.zeros_like(dc_s)

        def step(diagonal):
            for hh in range(HB):
                sl = slice(hh * HEAD_DIM, (hh + 1) * HEAD_DIM)
                p, _, ds = _fox_p_ds(q_ref.at[:, sl], k_ref.at[:, sl], v_ref.at[:, sl], do_ref.at[:, sl], c_ref, ct_ref,
                                     lse_ref.at[hh], dl_ref.at[hh], hb * HB + hh, T, diagonal)
                dv_s[hh] += lax.dot_general(p.astype(BF16), do_ref[:, sl], _DIMS['tn'], preferred_element_type=F32)
                dk_s[hh] += lax.dot_general(ds.astype(BF16), q_ref[:, sl], _DIMS['tn'], preferred_element_type=F32)
                dc_s[hh] += jnp.sum(ds, axis=0, keepdims=True)

        _below_and_on_diagonal(i, j, step)

        @pl.when(i == n - 1)
        def _():
            for hh in range(HB):
                sl = slice(hh * HEAD_DIM, (hh + 1) * HEAD_DIM)
                dk_ref[:, sl] = dk_s[hh] * (1.0 / math.sqrt(HEAD_DIM))
                dv_ref[:, sl] = dv_s[hh].astype(BF16)
                dc_ref[hh] = -dc_s[hh]

    qs = pl.BlockSpec((T, W2), lambda h, j, i: (jnp.maximum(i, j), h))
    ks = pl.BlockSpec((T, W2), lambda h, j, i: (j, h))
    st = pl.BlockSpec((HB, T, LANES), lambda h, j, i: (h, jnp.maximum(i, j), 0))
    return pl.pallas_call(
        body, name='fox_bwd_kv', grid=(H // HB, n, n),
        in_specs=[qs, ks, ks, qs, pl.BlockSpec((T, LANES), lambda h, j, i: (jnp.maximum(i, j), 0)),
                  pl.BlockSpec((Hp, T), lambda h, j, i: (0, j)), st, st],
        out_specs=[ks, ks, pl.BlockSpec((HB, 1, T), lambda h, j, i: (h, 0, j))],
        out_shape=[jax.ShapeDtypeStruct((S, FW), F32), jax.ShapeDtypeStruct((S, FW), BF16),
                   jax.ShapeDtypeStruct((H, 1, S), F32)],
        scratch_shapes=[pltpu.VMEM((HB, T, HEAD_DIM), F32), pltpu.VMEM((HB, T, HEAD_DIM), F32),
                        pltpu.VMEM((HB, 1, T), F32)],
        compiler_params=_params(('parallel', 'parallel', 'arbitrary')))(qn, kn, vb, do, c, ct, lse, dl)


def _shift_down(v, d, rows, fill):
    return jnp.where(rows >= d, pltpu.roll(v, d, 0), fill)


def _shift_up(v, d, rows, S, fill):
    return jnp.where(rows < S - d, pltpu.roll(v, S - d, 0), fill)


SUBLANES = 8


def _scan_by_doubling(a, b, pos, span, reverse):
    n = a.shape[0]
    d = 1
    while d < span:
        if reverse:
            keep = pos < span - d
            a_s, b_s = jnp.where(keep, pltpu.roll(a, n - d, 0), 1.0), jnp.where(keep, pltpu.roll(b, n - d, 0), 0.0)
        else:
            keep = pos >= d
            a_s, b_s = jnp.where(keep, pltpu.roll(a, d, 0), 1.0), jnp.where(keep, pltpu.roll(b, d, 0), 0.0)
        b = a * b_s + b
        a = a * a_s
        d *= 2
    return a, b


def _scan(a, b, rows, S, reverse, scr):
    groups = S // SUBLANES
    a, b = _scan_by_doubling(a, b, jnp.bitwise_and(rows, SUBLANES - 1), SUBLANES, reverse)
    scr[0][...] = a
    scr[1][...] = b
    edge = 0 if reverse else SUBLANES - 1
    a_g = scr[0][pl.ds(edge, groups, stride=SUBLANES), :]
    b_g = scr[1][pl.ds(edge, groups, stride=SUBLANES), :]
    g_pos = lax.broadcasted_iota(jnp.int32, (groups, LANES), 0)
    _, h_g = _scan_by_doubling(a_g, b_g, g_pos, groups, reverse)
    if reverse:
        carry = jnp.where(g_pos < groups - 1, pltpu.roll(h_g, groups - 1, 0), 0.0)
    else:
        carry = jnp.where(g_pos >= 1, pltpu.roll(h_g, 1, 0), 0.0)
    for r in range(SUBLANES):
        scr[0][pl.ds(r, groups, stride=SUBLANES), :] = carry
    return b + a * scr[0][...]


def _lru_forward(u, cw, cb, wra, bra, wri, bri, lam, rows, scr):
    uc = cb + cw[CONV_W - 1] * u
    for d in range(1, CONV_W):
        uc = uc + cw[CONV_W - 1 - d] * _shift_down(u, d, rows, 0.0)
    ucb = uc.astype(BF16)
    r = _sigmoid(jnp.dot(ucb, wra.astype(BF16), preferred_element_type=F32) + bra)
    ig = _sigmoid(jnp.dot(ucb, wri.astype(BF16), preferred_element_type=F32) + bri)
    sp = _softplus(-lam)
    log_a = -LRU_C * r * sp
    a = jnp.exp(log_a)
    sq = jnp.sqrt(_neg_expm1(2.0 * log_a))
    iu = ig * uc
    hseq = _scan(a, sq * iu, rows, u.shape[0], False, scr)
    return uc, ucb, r, ig, sp, a, sq, iu, hseq


def _lru_specs(S, n_u, n_g):
    col = lambda off: pl.BlockSpec((S, LANES), lambda cbk: (0, off + cbk))
    vec = pl.BlockSpec((1, LANES), lambda cbk: (0, cbk))
    mat = pl.BlockSpec((None, LANES, LANES), lambda cbk: (cbk, 0, 0))
    cw = pl.BlockSpec((CONV_W, LANES), lambda cbk: (0, cbk))
    return col, vec, mat, cw


def lru_fwd(proj, conv_w, conv_b, w_ra, b_ra, w_ri, b_ri, lam, u_off, g_off):
    S = proj.shape[0]
    nb = w_ra.shape[0]
    col, vec, mat, cws = _lru_specs(S, u_off, g_off)

    def body(u_ref, g_ref, cw_ref, cb_ref, wra_ref, bra_ref, wri_ref, bri_ref, lam_ref, y_ref, scr0, scr1):
        rows = lax.broadcasted_iota(jnp.int32, (S, LANES), 0)
        cw = [cw_ref[t:t + 1, :] for t in range(CONV_W)]
        hseq = _lru_forward(u_ref[...], cw, cb_ref[...], wra_ref[...], bra_ref[...], wri_ref[...],
                            bri_ref[...], lam_ref[...], rows, (scr0, scr1))[-1]
        y_ref[...] = hseq * _gelu_and_grad(g_ref[...])[0]

    return pl.pallas_call(
        body, name='lru_fwd', grid=(nb,),
        in_specs=[col(u_off), col(g_off), cws, vec, mat, vec, mat, vec, vec], out_specs=col(0),
        out_shape=jax.ShapeDtypeStruct((S, nb * LANES), F32),
        scratch_shapes=[pltpu.VMEM((S, LANES), F32), pltpu.VMEM((S, LANES), F32)],
        compiler_params=_params(('parallel',)))(proj, proj, conv_w, conv_b, w_ra, b_ra, w_ri, b_ri, lam)


def lru_bwd(proj, dy, conv_w, conv_b, w_ra, b_ra, w_ri, b_ri, lam, u_off, g_off):
    S = proj.shape[0]
    nb = w_ra.shape[0]
    LW = nb * LANES
    col, vec, mat, cws = _lru_specs(S, u_off, g_off)

    def body(u_ref, g_ref, dy_ref, cw_ref, cb_ref, wra_ref, bra_ref, wri_ref, bri_ref, lam_ref,
             du_ref, dg_ref, dcw_ref, dcb_ref, dwra_ref, dbra_ref, dwri_ref, dbri_ref, dlam_ref, scr0, scr1):
        rows = lax.broadcasted_iota(jnp.int32, (S, LANES), 0)
        u, lam_v = u_ref[...], lam_ref[...]
        cw = [cw_ref[t:t + 1, :] for t in range(CONV_W)]
        wra, wri = wra_ref[...].astype(BF16), wri_ref[...].astype(BF16)
        uc, ucb, r, ig, sp, a, sq, iu, hseq = _lru_forward(u, cw, cb_ref[...], wra, bra_ref[...], wri, bri_ref[...],
                                                           lam_v, rows, (scr0, scr1))
        gl, dgl = _gelu_and_grad(g_ref[...])
        dy_v = dy_ref[...]
        dg_ref[...] = (dy_v * hseq * dgl).astype(BF16)
        G = _scan(_shift_up(a, 1, rows, S, 0.0), dy_v * gl, rows, S, True, (scr0, scr1))
        da = G * _shift_down(hseq, 1, rows, 0.0)
        diu = G * sq
        dsq = G * iu
        dlog_a = da * a - dsq * a * a / jnp.maximum(sq, 1e-30)
        dr = dlog_a * (-LRU_C * sp)
        dsp = jnp.sum(dlog_a * (-LRU_C * r), axis=0, keepdims=True)
        dlam_ref[...] = -dsp * _sigmoid(-lam_v)
        dzr = dr * r * (1.0 - r)
        dzi = diu * uc * ig * (1.0 - ig)
        dzrb, dzib = dzr.astype(BF16), dzi.astype(BF16)
        duc = (diu * ig + lax.dot_general(dzrb, wra, _DIMS['nt'], preferred_element_type=F32)
               + lax.dot_general(dzib, wri, _DIMS['nt'], preferred_element_type=F32))
        dwra_ref[...] = lax.dot_general(ucb, dzrb, _DIMS['tn'], preferred_element_type=F32)
        dwri_ref[...] = lax.dot_general(ucb, dzib, _DIMS['tn'], preferred_element_type=F32)
        dbra_ref[...] = jnp.sum(dzr, axis=0, keepdims=True)
        dbri_ref[...] = jnp.sum(dzi, axis=0, keepdims=True)
        dcb_ref[...] = jnp.sum(duc, axis=0, keepdims=True)
        du = cw[CONV_W - 1] * duc
        dcw_ref[CONV_W - 1:CONV_W, :] = jnp.sum(duc * u, axis=0, keepdims=True)
        for d in range(1, CONV_W):
            du = du + cw[CONV_W - 1 - d] * _shift_up(duc, d, rows, S, 0.0)
            dcw_ref[CONV_W - 1 - d:CONV_W - d, :] = jnp.sum(duc * _shift_down(u, d, rows, 0.0), axis=0, keepdims=True)
        du_ref[...] = du.astype(BF16)

    sd = jax.ShapeDtypeStruct
    return pl.pallas_call(
        body, name='lru_bwd', grid=(nb,),
        in_specs=[col(u_off), col(g_off), col(0), cws, vec, mat, vec, mat, vec, vec],
        out_specs=[col(0), col(0), cws, vec, mat, vec, mat, vec, vec],
        out_shape=[sd((S, LW), BF16), sd((S, LW), BF16), sd((CONV_W, LW), F32), sd((1, LW), F32),
                   sd((nb, LANES, LANES), F32), sd((1, LW), F32), sd((nb, LANES, LANES), F32), sd((1, LW), F32),
                   sd((1, LW), F32)],
        scratch_shapes=[pltpu.VMEM((S, LANES), F32), pltpu.VMEM((S, LANES), F32)],
        compiler_params=_params(('parallel',)))(proj, proj, dy, conv_w, conv_b, w_ra, b_ra, w_ri, b_ri, lam)


def mix_fwd(o_fox, y_lru, g_fox, g_lru):
    S, FW = o_fox.shape
    tr = _tile(S, (256, 128))

    def body(o_ref, y_ref, gf_ref, gl_ref, m_ref):
        m_ref[...] = jnp.concatenate([_rms(o_ref[...], gf_ref[...]), _rms(y_ref[...], gl_ref[...])],
                                     axis=1).astype(BF16)

    return _rows_call('mix_fwd', body, S, tr,
                      [(o_fox, _rb(tr, FW)), (y_lru, _rb(tr, FW)), (g_fox, _fb((1, FW))), (g_lru, _fb((1, FW)))],
                      [((S, 2 * FW), BF16, _rb(tr, 2 * FW))])[0]


def mix_bwd(o_fox, y_lru, g_fox, g_lru, dmix):
    S, FW = o_fox.shape
    H = FW // HEAD_DIM
    tr = _tile(S, (256, 128))

    def body(o_ref, y_ref, gf_ref, gl_ref, df_ref, dl_ref, do_ref, dlt_ref, dy_ref, dgf_ref, dgl_ref):
        o = o_ref[...]
        do, dgf = _rms_bwd(o, gf_ref[...], df_ref[...])
        dyl, dgl = _rms_bwd(y_ref[...], gl_ref[...], dl_ref[...])
        do_ref[...] = do.astype(BF16)
        dy_ref[...] = dyl
        prod = do * o
        for h in range(H):
            dlt_ref[h] = jnp.broadcast_to(
                jnp.sum(prod[:, h * HEAD_DIM:(h + 1) * HEAD_DIM], axis=1, keepdims=True), (tr, LANES))
        first = pl.program_id(0) == 0
        _acc_out(dgf_ref, first, dgf)
        _acc_out(dgl_ref, first, dgl)

    g = _fb((1, FW))
    return _rows_call('mix_bwd', body, S, tr,
                      [(o_fox, _rb(tr, FW)), (y_lru, _rb(tr, FW)), (g_fox, g), (g_lru, g), (dmix, _rb(tr, FW, 0)),
                       (dmix, _rb(tr, FW, 1))],
                      [((S, FW), BF16, _rb(tr, FW)), ((H, S, LANES), F32, pl.BlockSpec((H, tr, LANES), lambda i: (0, i, 0))),
                       ((S, FW), F32, _rb(tr, FW)), ((1, FW), F32, g), ((1, FW), F32, g)])


def _xattn_heads(cq_raw, ckv, g_cq, g_ck, XW):
    out = []
    for h in range(XW // HEAD_DIM):
        sl = slice(h * HEAD_DIM, (h + 1) * HEAD_DIM)
        out.append((cq_raw[:, sl], _rms(cq_raw[:, sl], g_cq), ckv[:, sl], _rms(ckv[:, sl], g_ck),
                    ckv[:, XW + h * HEAD_DIM:XW + (h + 1) * HEAD_DIM].astype(BF16)))
    return out


def xattn_fwd(cq_raw, ckv, g_cq, g_ck):
    S, XW = cq_raw.shape
    M = ckv.shape[0]
    tr = _tile(S, (512, 256, 128))

    def body(q_ref, kv_ref, gq_ref, gk_ref, o_ref):
        outs = []
        for _, qn, _, kn, v in _xattn_heads(q_ref[...], kv_ref[...], gq_ref[...], gk_ref[...], XW):
            s = lax.dot_general(qn.astype(BF16), kn.astype(BF16), _DIMS['nt'], preferred_element_type=F32)
            s = s / math.sqrt(HEAD_DIM)
            p = jnp.exp(s - jnp.max(s, axis=1, keepdims=True))
            p = p / jnp.sum(p, axis=1, keepdims=True)
            outs.append(jnp.dot(p.astype(BF16), v, preferred_element_type=F32))
        o_ref[...] = jnp.concatenate(outs, axis=1).astype(BF16)

    g = _fb((1, HEAD_DIM))
    return _rows_call('xattn_fwd', body, S, tr,
                      [(cq_raw, _rb(tr, XW)), (ckv, _fb((M, 2 * XW))), (g_cq, g), (g_ck, g)],
                      [((S, XW), BF16, _rb(tr, XW))])[0]


def xattn_bwd(cq_raw, ckv, g_cq, g_ck, do):
    S, XW = cq_raw.shape
    M = ckv.shape[0]
    tr = _tile(S, (512, 256, 128))
    n = S // tr

    def body(q_ref, kv_ref, gq_ref, gk_ref, do_ref, dq_ref, dkv_ref, dgq_ref, dgk_ref):
        i = pl.program_id(0)
        do_v = do_ref[...]
        dqs, dkn, dvs = [], [], []
        dgq = jnp.zeros((1, HEAD_DIM), F32)
        for h, (q_raw, qn, _, kn, v) in enumerate(_xattn_heads(q_ref[...], kv_ref[...], gq_ref[...], gk_ref[...], XW)):
            qb, kb = qn.astype(BF16), kn.astype(BF16)
            doh = do_v[:, h * HEAD_DIM:(h + 1) * HEAD_DIM]
            s = lax.dot_general(qb, kb, _DIMS['nt'], preferred_element_type=F32) / math.sqrt(HEAD_DIM)
            p = jnp.exp(s - jnp.max(s, axis=1, keepdims=True))
            p = p / jnp.sum(p, axis=1, keepdims=True)
            dp = lax.dot_general(doh, v, _DIMS['nt'], preferred_element_type=F32)
            ds = (p * (dp - jnp.sum(p * dp, axis=1, keepdims=True)) / math.sqrt(HEAD_DIM)).astype(BF16)
            dvs.append(lax.dot_general(p.astype(BF16), doh, _DIMS['tn'], preferred_element_type=F32))
            dkn.append(lax.dot_general(ds, qb, _DIMS['tn'], preferred_element_type=F32))
            dq, g1 = _rms_bwd(q_raw, gq_ref[...], jnp.dot(ds, kb, preferred_element_type=F32))
            dqs.append(dq)
            dgq = dgq + g1
        dq_ref[...] = jnp.concatenate(dqs, axis=1).astype(BF16)
        first = i == 0
        _acc_out(dgq_ref, first, dgq)
        _acc_out(dkv_ref, first, jnp.concatenate(dkn + dvs, axis=1))

        @pl.when(i == n - 1)
        def _():
            kv = kv_ref[...]
            acc = dkv_ref[...]
            dk, gk = _heads(lambda t, d: _rms_bwd(t, gk_ref[...], d), XW // HEAD_DIM, kv[:, :XW], acc[:, :XW])
            dkv_ref[:, :XW] = dk
            dgk_ref[...] = gk

    g = _fb((1, HEAD_DIM))
    return _rows_call('xattn_bwd', body, S, tr,
                      [(cq_raw, _rb(tr, XW)), (ckv, _fb((M, 2 * XW))), (g_cq, g), (g_ck, g), (do, _rb(tr, XW))],
                      [((S, XW), BF16, _rb(tr, XW)), ((M, 2 * XW), F32, _fb((M, 2 * XW))), ((1, HEAD_DIM), F32, g),
                       ((1, HEAD_DIM), F32, g)])


def gate_up_fwd(hf, w, F):
    S, D = hf.shape
    J, _, Nj = w.shape
    tm = _tile(S, (1024, 512, 256, 128))
    tn = _tile(Nj, (256, 128))
    per = Nj // tn
    half = J // 2 * per

    def body(a_ref, bg_ref, bu_ref, gu_ref, act_ref):
        a = a_ref[...]
        g = jnp.dot(a, bg_ref[...], preferred_element_type=F32)
        u = jnp.dot(a, bu_ref[...], preferred_element_type=F32)
        gu_ref[0] = g
        gu_ref[1] = u
        act_ref[...] = (g * _sigmoid(g) * u).astype(BF16)

    return pl.pallas_call(
        body, name='proj_gate_up', grid=(S // tm, half),
        in_specs=[pl.BlockSpec((tm, D), lambda m, n: (m, 0)),
                  pl.BlockSpec((None, D, tn), lambda m, n: (n // per, 0, n % per)),
                  pl.BlockSpec((None, D, tn), lambda m, n: ((n + half) // per, 0, n % per))],
        out_specs=[pl.BlockSpec((2, tm, tn), lambda m, n: (0, m, n)), pl.BlockSpec((tm, tn), lambda m, n: (m, n))],
        out_shape=[jax.ShapeDtypeStruct((2, S, F), F32), jax.ShapeDtypeStruct((S, F), BF16)],
        compiler_params=_params(('parallel', 'parallel')))(hf, w, w)


def down_bwd_x(dyb, w_down, gu, after):
    S, D = dyb.shape
    F = w_down.shape[0]
    tm = _tile(S, (1024, 512, 256, 128))
    tn = _tile(F, (512, 256, 128))

    def body(a_ref, b_ref, gu_ref, after_ref, o_ref):
        da = lax.dot_general(a_ref[...], b_ref[...], _DIMS['nt'], preferred_element_type=F32)
        g = gu_ref[0]
        sg = _sigmoid(g)
        o_ref[0] = (da * gu_ref[1] * sg * (1.0 + g * (1.0 - sg))).astype(BF16)
        o_ref[1] = (da * g * sg).astype(BF16)

    planes = pl.BlockSpec((2, tm, tn), lambda m, n: (0, m, n))
    return pl.pallas_call(
        body, name='bwd_down_x', grid=(S // tm, F // tn),
        in_specs=[pl.BlockSpec((tm, D), lambda m, n: (m, 0)), pl.BlockSpec((tn, D), lambda m, n: (n, 0)), planes, ANY],
        out_specs=planes, out_shape=jax.ShapeDtypeStruct((2, S, F), BF16),
        compiler_params=_params(('parallel', 'parallel')))(dyb, w_down, gu, after)


def down_fwd_loss(act, w_down, x2, target):
    S, F = act.shape
    D = w_down.shape[1]
    tm, tn, tk = _mm_tiles(S, D, F, F, act, w_down, F32, x2, tn_cands=(512, 256, 128))
    nk = F // tk

    def body(a_ref, b_ref, x_ref, t_ref, d_ref, db_ref, l_ref, acc):
        m, n, k = pl.program_id(0), pl.program_id(1), pl.program_id(2)
        part = jnp.dot(a_ref[...], b_ref[...], preferred_element_type=F32)

        @pl.when(k == 0)
        def _():
            acc[...] = part

        @pl.when(k > 0)
        def _():
            acc[...] += part

        @pl.when(k == nk - 1)
        def _():
            err = acc[...] + x_ref[...] - t_ref[...]
            d = err * (1.0 / D)
            d_ref[...] = d
            db_ref[...] = d.astype(BF16)
            tot = jnp.sum(jnp.sum(err * err, axis=1, keepdims=True), axis=0, keepdims=True) * (0.5 / D)
            _acc_out(l_ref, jnp.logical_and(m == 0, n == 0), jnp.broadcast_to(tot, (1, LANES)))

    tile = pl.BlockSpec((tm, tn), lambda m, n, k: (m, n))
    return pl.pallas_call(
        body, name='proj_down', grid=(S // tm, D // tn, nk),
        in_specs=[pl.BlockSpec((tm, tk), lambda m, n, k: (m, k)), pl.BlockSpec((tk, tn), lambda m, n, k: (k, n)), tile, tile],
        out_specs=[tile, tile, pl.BlockSpec((1, LANES), lambda m, n, k: (0, 0))],
        out_shape=[jax.ShapeDtypeStruct((S, D), F32), jax.ShapeDtypeStruct((S, D), BF16),
                   jax.ShapeDtypeStruct((1, LANES), F32)],
        scratch_shapes=[pltpu.VMEM((tm, tn), F32)],
        compiler_params=_params(('arbitrary', 'arbitrary', 'arbitrary')))(act, w_down, x2, target)


def _adamw_math(w, gv, m, v):
    mn = ADAM_B1 * m + (1.0 - ADAM_B1) * gv
    vn = ADAM_B2 * v + (1.0 - ADAM_B2) * (gv * gv)
    m_hat = mn / (1.0 - ADAM_B1 ** ADAM_STEP)
    v_hat = vn / (1.0 - ADAM_B2 ** ADAM_STEP)
    return -ADAM_LR * (m_hat / (jnp.sqrt(v_hat) + ADAM_EPS) + ADAM_WD * w), mn, vn


def adamw(name, w, g, m, v):
    R, C = w.shape
    tr = _row_tile(R, C)

    def body(w_ref, g_ref, m_ref, v_ref, d_ref, mo_ref, vo_ref):
        d_ref[...], mo_ref[...], vo_ref[...] = _adamw_math(w_ref[...], g_ref[...], m_ref[...], v_ref[...])

    spec = _rb(tr, C)
    return _rows_call(name, body, R, tr, [(w, spec), (g, spec), (m, spec), (v, spec)], [((R, C), F32, spec)] * 3)


def adamw_halves(name, w, mine, other, m, v, c_idx):
    R, C = w.shape
    hr = R // 2
    tr = _row_tile(hr, C)

    def body(c_ref, w_ref, a_ref, b_ref, m_ref, v_ref, g_ref, d_ref, mo_ref, vo_ref):
        gv = jnp.where(pl.program_id(0) == c_ref[0], a_ref[...], b_ref[...])
        g_ref[...] = gv
        d_ref[...], mo_ref[...], vo_ref[...] = _adamw_math(w_ref[...], gv, m_ref[...], v_ref[...])

    full = pl.BlockSpec((None, tr, C), lambda hh, i, c_ref: (hh, i, 0))
    mine_spec = pl.BlockSpec((tr, C), lambda hh, i, c_ref: (jnp.where(hh == c_ref[0], i, 0), 0))
    other_spec = pl.BlockSpec((tr, C), lambda hh, i, c_ref: (jnp.where(hh == c_ref[0], 0, i), 0))
    outs = pl.pallas_call(
        body, name=name,
        grid_spec=pltpu.PrefetchScalarGridSpec(num_scalar_prefetch=1, grid=(2, hr // tr),
                                               in_specs=[full, mine_spec, other_spec, full, full], out_specs=[full] * 4),
        out_shape=[jax.ShapeDtypeStruct((2, hr, C), F32)] * 4,
        compiler_params=_params(('parallel', 'parallel')))(
            c_idx, w.reshape(2, hr, C), mine, other, m.reshape(2, hr, C), v.reshape(2, hr, C))
    return [o.reshape(R, C) for o in outs]


def _place():
    x, y, c = lax.axis_index('x'), lax.axis_index('y'), lax.axis_index('c')
    return x, y, c, [(1 - x, y), (x, 1 - y), (1 - x, 1 - y)]


def _rcopy(src, dst, ssem, rsem, dev):
    return pltpu.make_async_remote_copy(src_ref=src, dst_ref=dst, send_sem=ssem, recv_sem=rsem, device_id=dev,
                                        device_id_type=MESH)


HBM = pl.BlockSpec(memory_space=pltpu.HBM)
SEM = pl.BlockSpec(memory_space=pltpu.SEMAPHORE)
EFFECT = pltpu.SideEffectType.DATAFLOW_SIDE_EFFECTING


def _in_hbm(a):
    return pltpu.with_memory_space_constraint(a, pltpu.HBM)


def _rows_part(shape, whole, half):
    return pl.ds(0, shape[0]) if whole else pl.ds(half * (shape[0] // 2), shape[0] // 2)


def gather_start(name, shards, whole):
    nT = len(shards)

    def body(*refs):
        srcs, lands = refs[:nT], refs[nT:2 * nT]
        ssem, rsem, token = refs[2 * nT], refs[2 * nT + 1], refs[-1]
        x, y, c, chips = _place()
        for t in range(nT):
            rows = _rows_part(shards[t].shape, whole[t], c)
            for k, (px, py) in enumerate(chips):
                _rcopy(srcs[t].at[rows], lands[t].at[2 * x + y, rows], ssem.at[3 * t + k], rsem.at[3 * t + k],
                       (px, py, c)).start()
        token[...] = jnp.zeros_like(token)

    zones = [lax.empty((N_CHIPS,) + s.shape, s.dtype) for s in shards]
    outs = pl.pallas_call(
        body, name=name,
        out_shape=(pltpu.SemaphoreType.DMA((3 * nT,)), pltpu.SemaphoreType.DMA((3 * nT,)),
                   *[pltpu.HBM(s.shape, s.dtype) for s in shards], *[pltpu.HBM(z.shape, z.dtype) for z in zones],
                   jax.ShapeDtypeStruct((8, LANES), F32)),
        in_specs=[HBM] * (2 * nT), out_specs=(SEM, SEM, *[HBM] * (2 * nT), pl.BlockSpec(memory_space=pltpu.VMEM)),
        input_output_aliases={i: 2 + i for i in range(2 * nT)},
        compiler_params=pltpu.CompilerParams(has_side_effects=EFFECT))(*[_in_hbm(a) for a in list(shards) + zones])
    return outs[0], outs[1], outs[2:2 + nT], outs[2 + nT:2 + 2 * nT], outs[-1]


def gather_wait(name, t, shard, zone, ssem, rsem, after, whole):
    after = after if isinstance(after, (list, tuple)) else [after]

    def body(src_ref, land_ref, ssem_ref, rsem_ref, *rest):
        x, y, c, chips = _place()
        rows = _rows_part(shard.shape, whole, c)
        for k, (px, py) in enumerate(chips):
            cp = _rcopy(src_ref.at[rows], land_ref.at[2 * px + py, rows], ssem_ref.at[3 * t + k], rsem_ref.at[3 * t + k],
                        (px, py, c))
            cp.wait_send()
            cp.wait_recv()

    return pl.pallas_call(
        body, name=name, out_shape=(pltpu.HBM(shard.shape, shard.dtype), pltpu.HBM(zone.shape, zone.dtype)),
        in_specs=(HBM, HBM, SEM, SEM, *[ANY] * len(after)), out_specs=(HBM, HBM), input_output_aliases={0: 0, 1: 1},
        compiler_params=pltpu.CompilerParams(has_side_effects=EFFECT))(shard, zone, ssem, rsem, *after)


def pair_swap(name, zone):
    hr = zone.shape[1] // 2

    def body(z_in, z_ref, ssem, rsem):
        x, y, c, chips = _place()
        cps = []
        for k, (px, py) in enumerate(chips):
            blk = z_ref.at[2 * px + py, pl.ds(c * hr, hr)]
            cps.append(_rcopy(blk, blk, ssem.at[k], rsem.at[k], (x, y, 1 - c)))
            cps[-1].start()
        for k, (px, py) in enumerate(chips):
            blk = z_ref.at[2 * px + py, pl.ds((1 - c) * hr, hr)]
            _rcopy(blk, blk, ssem.at[k], rsem.at[k], (x, y, 1 - c)).wait_recv()
        for cp in cps:
            cp.wait_send()

    return pl.pallas_call(
        body, name=name, in_specs=[ANY], out_specs=ANY, out_shape=jax.ShapeDtypeStruct(zone.shape, zone.dtype),
        input_output_aliases={0: 0},
        scratch_shapes=[pltpu.SemaphoreType.DMA((3,)), pltpu.SemaphoreType.DMA((3,))],
        compiler_params=_params())(zone)


N_SENDERS = 7


def _scatter_copies(g_ref, l_ref, ssem, rsem):
    x, y, c, chips = _place()
    cps = []
    for k, (px, py) in enumerate(chips):
        for d in range(2):
            to = (c + d) % 2
            cps.append(_rcopy(g_ref.at[2 * px + py, to], l_ref.at[2 * k + d], ssem.at[2 * k + d], rsem.at[2 * k + d],
                              (px, py, to)))
    cps.append(_rcopy(g_ref.at[2 * x + y, 1 - c], l_ref.at[6], ssem.at[6], rsem.at[6], (x, y, 1 - c)))
    return cps


def scatter_start(name, g):
    def body(g_ref, l_ref, ssem, rsem, g_out, l_out, token):
        for cp in _scatter_copies(g_ref, l_ref, ssem, rsem):
            cp.start()
        token[...] = jnp.zeros_like(token)

    zone = lax.empty((N_SENDERS,) + g.shape[2:], g.dtype)
    return pl.pallas_call(
        body, name=name,
        out_shape=(pltpu.SemaphoreType.DMA((N_SENDERS,)), pltpu.SemaphoreType.DMA((N_SENDERS,)),
                   pltpu.HBM(g.shape, g.dtype), pltpu.HBM(zone.shape, zone.dtype), jax.ShapeDtypeStruct((8, LANES), F32)),
        in_specs=[HBM, HBM], out_specs=(SEM, SEM, HBM, HBM, pl.BlockSpec(memory_space=pltpu.VMEM)),
        input_output_aliases={0: 2, 1: 3},
        compiler_params=pltpu.CompilerParams(has_side_effects=EFFECT))(_in_hbm(g), _in_hbm(zone))


def scatter_wait(name, g, zone, ssem, rsem, after):
    def body(g_ref, l_ref, ssem_ref, rsem_ref, after_ref, g_out, l_out):
        for cp in _scatter_copies(g_ref, l_ref, ssem_ref, rsem_ref):
            cp.wait_send()
            cp.wait_recv()

    return pl.pallas_call(
        body, name=name, out_shape=(pltpu.HBM(g.shape, g.dtype), pltpu.HBM(zone.shape, zone.dtype)),
        in_specs=(HBM, HBM, SEM, SEM, ANY), out_specs=(HBM, HBM), input_output_aliases={0: 0, 1: 1},
        compiler_params=pltpu.CompilerParams(has_side_effects=EFFECT))(g, zone, ssem, rsem, after)


def sum_parts(name, g, landed, chip_idx, c_idx):
    hr, C = g.shape[2:]
    tr = _row_tile(hr, C, min_rows=16)

    def body(me_ref, c_ref, g_ref, l_ref, o_ref):
        acc = g_ref[...].astype(F32)
        for s in range(N_SENDERS):
            acc = acc + l_ref[s].astype(F32)
        o_ref[...] = acc

    return pl.pallas_call(
        body, name=name,
        grid_spec=pltpu.PrefetchScalarGridSpec(
            num_scalar_prefetch=2, grid=(hr // tr,),
            in_specs=[pl.BlockSpec((None, None, tr, C), lambda i, me_ref, c_ref: (me_ref[0], c_ref[0], i, 0)),
                      pl.BlockSpec((N_SENDERS, tr, C), lambda i, me_ref, c_ref: (0, i, 0))],
            out_specs=pl.BlockSpec((tr, C), lambda i, me_ref, c_ref: (i, 0))),
        out_shape=jax.ShapeDtypeStruct((hr, C), F32),
        compiler_params=_params(('parallel',)))(chip_idx, c_idx, g, landed)


def pair_join(name, halves):
    nT = len(halves)

    def body(*refs):
        ins, outs = refs[:nT], refs[nT:2 * nT]
        ssem, rsem = refs[2 * nT:]
        x, y, c, _ = _place()
        cps = [_rcopy(ins[t], outs[t], ssem.at[t], rsem.at[t], (x, y, 1 - c)) for t in range(nT)]
        for cp in cps:
            cp.start()
        for cp in cps:
            cp.wait()

    return pl.pallas_call(
        body, name=name, in_specs=[ANY] * nT, out_specs=[ANY] * nT,
        out_shape=[jax.ShapeDtypeStruct(h.shape, h.dtype) for h in halves],
        scratch_shapes=[pltpu.SemaphoreType.DMA((nT,)), pltpu.SemaphoreType.DMA((nT,))],
        compiler_params=_params())(*halves)


N_DEVICES = 8


def _spread_copies(b_ref, l_ref, ssem, rsem):
    x, y, c, chips = _place()
    me = 4 * x + 2 * y + c
    pairs = []
    for px, py, pc in [(px, py, pc) for px, py in chips for pc in (c, 1 - c)] + [(x, y, 1 - c)]:
        it = 4 * px + 2 * py + pc
        pairs.append((_rcopy(b_ref, l_ref.at[me], ssem.at[it], rsem.at[me], (px, py, pc)),
                      _rcopy(b_ref, l_ref.at[it], ssem.at[it], rsem.at[it], (px, py, pc))))
    return pairs


def spread_start(name, buf):
    def body(b_ref, l_ref, ssem, rsem, b_out, l_out, token):
        for mine, _ in _spread_copies(b_ref, l_ref, ssem, rsem):
            mine.start()
        token[...] = jnp.zeros_like(token)

    zone = lax.empty((N_DEVICES,) + buf.shape, buf.dtype)
    return pl.pallas_call(
        body, name=name,
        out_shape=(pltpu.SemaphoreType.DMA((N_DEVICES,)), pltpu.SemaphoreType.DMA((N_DEVICES,)),
                   pltpu.HBM(buf.shape, buf.dtype), pltpu.HBM(zone.shape, zone.dtype), jax.ShapeDtypeStruct((8, LANES), F32)),
        in_specs=[HBM, HBM], out_specs=(SEM, SEM, HBM, HBM, pl.BlockSpec(memory_space=pltpu.VMEM)),
        input_output_aliases={0: 2, 1: 3},
        compiler_params=pltpu.CompilerParams(has_side_effects=EFFECT))(_in_hbm(buf), _in_hbm(zone))


def spread_wait(name, buf, zone, ssem, rsem, after):
    def body(b_ref, l_ref, ssem_ref, rsem_ref, after_ref, b_out, l_out):
        for mine, theirs in _spread_copies(b_ref, l_ref, ssem_ref, rsem_ref):
            mine.wait_send()
            theirs.wait_recv()

    return pl.pallas_call(
        body, name=name, out_shape=(pltpu.HBM(buf.shape, buf.dtype), pltpu.HBM(zone.shape, zone.dtype)),
        in_specs=(HBM, HBM, SEM, SEM, ANY), out_specs=(HBM, HBM), input_output_aliases={0: 0, 1: 1},
        compiler_params=pltpu.CompilerParams(has_side_effects=EFFECT))(buf, zone, ssem, rsem, after)


def sum_devices(name, zone):
    _, R, C = zone.shape
    tr = _row_tile(R, C)

    def body(z_ref, o_ref):
        acc = z_ref[0]
        for d in range(1, N_DEVICES):
            acc = acc + z_ref[d]
        o_ref[...] = acc

    return pl.pallas_call(
        body, name=name, grid=(R // tr,), in_specs=[pl.BlockSpec((N_DEVICES, tr, C), lambda i: (0, i, 0))],
        out_specs=pl.BlockSpec((tr, C), lambda i: (i, 0)), out_shape=jax.ShapeDtypeStruct((R, C), F32),
        compiler_params=_params(('parallel',)))(zone)


class _InWindows:
    def __init__(self, FW, LW, H, C):
        gap = LANES - H
        padded = lambda o: o if o < 3 * FW + H else o + gap
        self.width = 3 * FW + LANES + 2 * LW
        self.f_block = 3 * FW // LANES
        self.first = [padded(C * j) // LANES for j in range(N_CHIPS)]
        self.blocks = max(padded(C * (j + 1) - 1) // LANES - self.first[j] + 1 for j in range(N_CHIPS))
        assert all((b + self.blocks) * LANES <= self.width for b in self.first)
        self.cols = self.blocks * LANES
        self.runs = []
        for j in range(N_CHIPS):
            cut = min(max(3 * FW + H - C * j, 0), C)
            spans = [(0, cut), (cut, C)]
            self.runs.append([(t0, t1, padded(C * j + t0) - LANES * self.first[j]) for t0, t1 in spans if t1 > t0])

    def to_window(self, shard, chip):
        def place(j, s):
            parts, pos = [], 0
            for t0, t1, w0 in self.runs[j]:
                parts += [jnp.zeros((s.shape[0], w0 - pos), s.dtype), s[:, t0:t1]]
                pos = w0 + t1 - t0
            parts.append(jnp.zeros((s.shape[0], self.cols - pos), s.dtype))
            return jnp.concatenate([p for p in parts if p.shape[1]], axis=1)
        return lax.switch(chip, [functools.partial(place, j) for j in range(N_CHIPS)], shard)

    def from_window(self, win, chip):
        def take(j, w):
            return jnp.concatenate([w[:, w0:w0 + t1 - t0] for t0, t1, w0 in self.runs[j]], axis=1)
        return lax.switch(chip, [functools.partial(take, j) for j in range(N_CHIPS)], win)

    def _spans(self, j):
        b0, b1 = self.first[j], self.first[j] + self.blocks
        return (b0, min(b1, self.f_block)), b0 <= self.f_block < b1, (max(b0, self.f_block + 1), b1)

    def assemble(self, zone):
        main, f_blk = None, None
        for j in range(N_CHIPS):
            (a0, a1), has_f, (c0, c1) = self._spans(j)
            for p0, p1, shift in ((a0, a1, 0), (c0, c1, 1)):
                if p1 > p0:
                    part = zone[j][:, (p0 - self.first[j]) * LANES:(p1 - self.first[j]) * LANES]
                    part = jnp.pad(part, ((0, 0), ((p0 - shift) * LANES, self.width - LANES - (p1 - shift) * LANES)))
                    main = part if main is None else main + part
            if has_f:
                part = zone[j][:, (self.f_block - self.first[j]) * LANES:(self.f_block - self.first[j] + 1) * LANES]
                f_blk = part if f_blk is None else f_blk + part
        return main, f_blk

    def windows(self, main, f_blk):
        out = []
        for j in range(N_CHIPS):
            (a0, a1), has_f, (c0, c1) = self._spans(j)
            parts = [main[:, a0 * LANES:a1 * LANES]] if a1 > a0 else []
            parts += [f_blk] if has_f else []
            parts += [main[:, (c0 - 1) * LANES:(c1 - 1) * LANES]] if c1 > c0 else []
            out.append(jnp.concatenate(parts, axis=1))
        return jnp.stack(out)


_PACK = 8 * LANES


PACK_ROWS = 256


def _pack(arrs):
    flat = []
    for a in arrs:
        v = a.reshape(-1).astype(F32)
        flat.append(jnp.pad(v, (0, (-v.shape[0]) % _PACK)))
    rows = sum(v.shape[0] for v in flat) // LANES
    flat.append(jnp.zeros(((-rows) % PACK_ROWS) * LANES, F32))
    return jnp.concatenate(flat).reshape(-1, LANES)


def _unpack(buf, shapes):
    out, off = [], 0
    flat = buf.reshape(-1)
    for sh in shapes:
        n = math.prod(sh)
        out.append(flat[off:off + n].reshape(sh))
        off += n + (-n) % _PACK
    return out


def kernel(x, mem, g_mix, w_in, b_f, g_q, g_k, conv_w, conv_b, w_ra, b_ra, w_ri, b_ri, lam, g_fox_out, g_lru_out, w_out, g_xattn, g_mem, w_cq, w_ckv, g_cq, g_ck, w_co, g_ffn, w_gate_up, w_down, loss_target, m_g_mix, m_w_in, m_b_f, m_g_q, m_g_k, m_conv_w, m_conv_b, m_w_ra, m_b_ra, m_w_ri, m_b_ri, m_lam, m_g_fox_out, m_g_lru_out, m_w_out, m_g_xattn, m_g_mem, m_w_cq, m_w_ckv, m_g_cq, m_g_ck, m_w_co, m_g_ffn, m_w_gate_up, m_w_down, v_g_mix, v_w_in, v_b_f, v_g_q, v_g_k, v_conv_w, v_conv_b, v_w_ra, v_b_ra, v_w_ri, v_b_ri, v_lam, v_g_fox_out, v_g_lru_out, v_w_out, v_g_xattn, v_g_mem, v_w_cq, v_w_ckv, v_g_cq, v_g_ck, v_w_co, v_g_ffn, v_w_gate_up, v_w_down):
    given = dict(locals())
    W = {n: given[n][0] for n in WEIGHTS}
    M1 = {n: given['m_' + n][0] for n in WEIGHTS}
    V1 = {n: given['v_' + n][0] for n in WEIGHTS}
    xs, ms, tgt = x[0], mem[0], loss_target[0]
    S, D = xs.shape
    H = W['b_f'].shape[0]
    FW = H * HEAD_DIM
    LW = W['lam'].shape[0]
    nb = W['w_ra'].shape[0]
    XW = W['w_cq'].shape[1]
    F = W['w_down'].shape[0] * N_CHIPS
    IN_W = W['w_in'].shape[1] * N_CHIPS
    assert FW == LW and LW == nb * LANES and IN_W == 3 * FW + H + 2 * LW and H <= 8
    T = _tile(S, (512, 256, 128))
    c_idx = lax.axis_index('c').astype(jnp.int32).reshape(1)
    chip = 2 * lax.axis_index('x') + lax.axis_index('y')
    chip_idx = chip.astype(jnp.int32).reshape(1)
    vec = lambda n: W[n].reshape(1, -1)

    wins = _InWindows(FW, LW, H, W['w_in'].shape[1])
    started = {}
    g_tok = jnp.zeros((1, 1), F32)
    for call, names in (('gather_start_first', ['conv_w', 'w_in']), ('gather_start_rest', BIG[1:])):
        own = [W[n].reshape(-1, LANES) if n == 'conv_w' else W[n].astype(BF16) + g_tok.astype(BF16) for n in names]
        own = [wins.to_window(o, chip) if n == 'w_in' else o for n, o in zip(names, own)]
        ssem, rsem, srcs, zones, tok = gather_start(call, own, [n == 'conv_w' for n in names])
        g_tok = tok[0:1, 0:1]
        started.update({n: (t, srcs[t], zones[t], ssem, rsem) for t, n in enumerate(names)})

    def fetch(n, after):
        t, g_src, g_zone, g_ssem, g_rsem = started[n]
        src, zone = gather_wait('gather_wait_' + n, t, g_src, g_zone, g_ssem, g_rsem, after, n == 'conv_w')
        if n != 'conv_w':
            zone = pair_swap('pair_swap_' + n, zone)
        return lax.dynamic_update_index_in_dim(zone, src, chip, 0)

    b_f_pad = jnp.pad(vec('b_f'), ((0, 0), (0, LANES - H)))
    u_off, g_off = 3 * FW // LANES, (3 * FW + LW) // LANES

    h1 = norm_fwd('norm_mix', xs, vec('g_mix') + g_tok[0:1, 0:1])
    conv_full = fetch('conv_w', h1).reshape(N_CHIPS, CONV_W, LW // N_CHIPS).transpose(1, 0, 2).reshape(CONV_W, LW)
    w5, wf = wins.assemble(fetch('w_in', [h1, M1['w_in'], V1['w_in']]))
    proj = _mm('proj_in', h1, w5, 'nn', F32)
    f_raw = _mm('proj_f', h1, wf, 'nn', F32)
    qn, kn, vb = qkv_fwd(proj, vec('g_q'), vec('g_k'), FW)
    cc = fgate_fwd(f_raw, b_f_pad)
    ct = cc[:, :8].T
    o_fox, lse = fox_fwd(qn, kn, vb, cc, ct, T)
    lru_w = (conv_full, vec('conv_b'), W['w_ra'], vec('b_ra'), W['w_ri'], vec('b_ri'), vec('lam'))
    y_lru = lru_fwd(proj, *lru_w, u_off, g_off)
    mixn = mix_fwd(o_fox, y_lru, vec('g_fox_out'), vec('g_lru_out'))
    w_out_f = fetch('w_out', mixn).reshape(2 * FW, D)
    x1 = _mm('proj_out', mixn, w_out_f, 'nn', F32, res=xs)

    hq = norm_fwd('norm_xq', x1, vec('g_xattn'))
    mn = norm_fwd('norm_mem', ms, vec('g_mem'))
    w_cq_f = fetch('w_cq', hq).reshape(D, XW)
    w_ckv_f = fetch('w_ckv', hq).reshape(D, 2 * XW)
    cq_raw = _mm('proj_cq', hq, w_cq_f, 'nn', F32)
    ckv = _mm('proj_ckv', mn, w_ckv_f, 'nn', F32)
    o_x = xattn_fwd(cq_raw, ckv, vec('g_cq'), vec('g_ck'))
    w_co_g = fetch('w_co', o_x)
    x2 = _mm_colsharded('proj_co', o_x, w_co_g, F32, res=x1)

    hf = norm_fwd('norm_ffn', x2, vec('g_ffn'))
    w_gu_g = fetch('w_gate_up', hf)
    gu, act = gate_up_fwd(hf, w_gu_g, F)
    w_down_f = fetch('w_down', act).reshape(F, D)
    dy, dyb, loss_blk = down_fwd_loss(act, w_down_f, x2, tgt)

    gw, pending = {}, []

    def reduce_begin(n, g):
        sp = g.reshape(N_CHIPS, 2, g.shape[1] // 2, g.shape[2])
        ssem, rsem, sp, zone, tok = scatter_start('scatter_start_' + n, sp)
        pending.append((n, sp, zone, ssem, rsem))
        return tok[0:1, 0:1]

    t_down = reduce_begin('w_down', _mm('bwd_down_w', act, dyb, 'tn', BF16).reshape(N_CHIPS, F // N_CHIPS, D))
    dgu = down_bwd_x(dyb, w_down_f, gu, t_down)
    dhf = _mm_colsharded_t('bwd_gate_up_x', dgu, w_gu_g, F32)
    t_gu = reduce_begin('w_gate_up', _mm_grad_colsharded('bwd_gate_up_w', hf, dgu, N_CHIPS, BF16))
    dx2, dx2b, gw['g_ffn'] = norm_bwd('norm_ffn_bwd', x2, vec('g_ffn') + t_down + t_gu, dhf, res=dy)

    do_x = _mm_colsharded_t('bwd_co_x', dx2b, w_co_g, BF16)
    t_co = reduce_begin('w_co', _mm_grad_colsharded('bwd_co_w', o_x, dx2b, N_CHIPS, BF16))
    dcq_raw, dckv, gw['g_cq'], gw['g_ck'] = xattn_bwd(cq_raw, ckv, vec('g_cq') + t_co, vec('g_ck'), do_x)
    dhq = _mm('bwd_cq_x', dcq_raw, w_cq_f, 'nt', F32)
    t_cq = reduce_begin('w_cq', _mm('bwd_cq_w', hq, dcq_raw, 'tn', BF16).reshape(N_CHIPS, D // N_CHIPS, XW))
    dmn = _mm('bwd_ckv_x', dckv, w_ckv_f, 'nt', F32)
    t_ckv = reduce_begin('w_ckv', _mm('bwd_ckv_w', mn, dckv, 'tn', BF16).reshape(N_CHIPS, D // N_CHIPS, 2 * XW))
    (gw['g_mem'],) = norm_bwd('norm_mem_bwd', ms, vec('g_mem'), dmn, want_dx=False)
    dx1, dx1b, gw['g_xattn'] = norm_bwd('norm_xq_bwd', x1, vec('g_xattn') + t_cq + t_ckv, dhq, res=dx2)

    dmix = _mm('bwd_out_x', dx1b, w_out_f, 'nt', F32)
    t_out = reduce_begin('w_out', _mm('bwd_out_w', mixn, dx1b, 'tn', BF16).reshape(N_CHIPS, 2 * FW // N_CHIPS, D))
    do_fox, delta, dy_lru, gw['g_fox_out'], gw['g_lru_out'] = mix_bwd(o_fox, y_lru, vec('g_fox_out') + t_out,
                                                                     vec('g_lru_out'), dmix)
    (du, dgate, gw['conv_w'], gw['conv_b'], gw['w_ra'], gw['b_ra'], gw['w_ri'], gw['b_ri'],
     gw['lam']) = lru_bwd(proj, dy_lru, *lru_w, u_off, g_off)
    early = [n for n in SMALL if n not in ('g_q', 'g_k', 'b_f', 'g_mix')]
    late = [n for n in SMALL if n not in early]
    e_ssem, e_rsem, e_buf, e_zone, e_tok = spread_start('spread_start_early', _pack([gw[n] for n in early]))
    dqn, delta2 = fox_bwd_q(qn, kn, vb, do_fox, cc, ct, lse, delta, T)
    dkn, dv, dct = fox_bwd_kv(qn, kn, vb, do_fox, cc, ct, lse, delta2, T)
    dq, dk, gw['g_q'], gw['g_k'] = qkv_bwd(proj, vec('g_q') + e_tok[0:1, 0:1], vec('g_k'), dqn, dkn, FW)
    dc = jnp.pad(dct.reshape(H, S).T, ((0, 0), (0, LANES - H)))
    df, db_f = fgate_bwd(f_raw, b_f_pad, dc, H)
    gw['b_f'] = db_f[:, :H]
    dproj = jnp.concatenate([dq, dk, dv, du, dgate], axis=1)
    dw5 = _mm('bwd_in_w', h1, dproj, 'tn', BF16)
    dwf = _mm('bwd_f_w', h1, df, 'tn', BF16)
    t_in = reduce_begin('w_in', wins.windows(dw5, dwf))
    dh_a = _mm('bwd_f_x', df, wf, 'nt', F32)
    dh1 = _mm('bwd_in_x', dproj, w5, 'nt', F32, res=dh_a)
    grad_x, _, gw['g_mix'] = norm_bwd('norm_mix_bwd', xs, vec('g_mix') + t_in, dh1, res=dx1)
    l_ssem, l_rsem, l_buf, l_zone, _ = spread_start('spread_start_late',
                                                    _pack([gw[n] for n in late] + [loss_blk[0:1, 0:1]]))

    grads, delta_w, new_m, new_v = {}, {}, {}, {}
    done = grad_x
    for n, part, zone, ssem, rsem in pending:
        part, landed = scatter_wait('scatter_wait_' + n, part, zone, ssem, rsem, done)
        mine = sum_parts('sum_parts_' + n, part, landed, chip_idx, c_idx)
        (other,) = pair_join('pair_join_' + n, [mine])
        if n == 'w_in':
            mine, other = wins.from_window(mine, chip), wins.from_window(other, chip)
        grads[n], delta_w[n], new_m[n], new_v[n] = adamw_halves('adamw_' + n, W[n], mine, other, M1[n], V1[n], c_idx)
        done = delta_w[n]

    device = 4 * lax.axis_index('x') + 2 * lax.axis_index('y') + lax.axis_index('c')
    summed = {}
    for tag, names, buf, zone, ssem, rsem in (('early', early, e_buf, e_zone, e_ssem, e_rsem),
                                              ('late', late + ['loss'], l_buf, l_zone, l_ssem, l_rsem)):
        buf, zone = spread_wait('spread_wait_' + tag, buf, zone, ssem, rsem, done)
        total = sum_devices('sum_small_' + tag, lax.dynamic_update_index_in_dim(zone, buf, device, 0))
        summed.update(zip(names, _unpack(total, [gw[n].shape if n != 'loss' else (1, 1) for n in names])))
    loss = summed['loss'].reshape(())
    for n in SMALL:
        g = summed[n]
        grads[n] = g.reshape(W[n].shape) if n != 'conv_w' else lax.dynamic_slice_in_dim(
            g, chip * (LW // N_CHIPS), LW // N_CHIPS, axis=1)
    packs = [_pack([d[n] for n in SMALL]) for d in (W, grads, M1, V1)]
    shapes = [W[n].shape for n in SMALL]
    for d, res in zip((delta_w, new_m, new_v), adamw('adamw_small', *packs)):
        d.update(zip(SMALL, _unpack(res, shapes)))

    lead = lambda d: [d[n][None] for n in WEIGHTS]
    return (loss, grad_x[None], *lead(grads), *lead(delta_w), *lead(new_m), *lead(new_v))
```

```python
import functools
import math

import jax
import jax.numpy as jnp
from jax import lax
from jax.experimental import pallas as pl
from jax.experimental.pallas import tpu as pltpu

F32 = jnp.float32
BF16 = jnp.bfloat16
HEAD_DIM = 128
LANES = 128
LRU_C = 8.0
RMS_EPS = 1e-6
CONV_W = 4
ADAM_LR = 0.001
ADAM_B1 = 0.9
ADAM_B2 = 0.999
ADAM_EPS = 1e-08
ADAM_WD = 0.01
ADAM_STEP = 10
VMEM_LIMIT = 56 * 1024 * 1024
N_CHIPS = 4
MESH = pl.DeviceIdType.MESH
ANY = pl.BlockSpec(memory_space=pl.ANY)

WEIGHTS = ['g_mix', 'w_in', 'b_f', 'g_q', 'g_k', 'conv_w', 'conv_b', 'w_ra', 'b_ra', 'w_ri', 'b_ri', 'lam',
           'g_fox_out', 'g_lru_out', 'w_out', 'g_xattn', 'g_mem', 'w_cq', 'w_ckv', 'g_cq', 'g_ck', 'w_co', 'g_ffn',
           'w_gate_up', 'w_down']
BIG = ['w_in', 'w_out', 'w_cq', 'w_ckv', 'w_co', 'w_gate_up', 'w_down']
SMALL = [n for n in WEIGHTS if n not in BIG]


def _params(sem=None):
    if sem is None:
        return pltpu.CompilerParams(vmem_limit_bytes=VMEM_LIMIT)
    return pltpu.CompilerParams(dimension_semantics=sem, vmem_limit_bytes=VMEM_LIMIT)


def _tile(n, cands):
    for t in cands:
        if n % t == 0:
            return t
    return n


ROW_BLOCK_BYTES = 1 << 20


def _row_tile(n_rows, n_cols, min_rows=8):
    cands = [t for t in (512, 256, 128, 64, 32, 16, 8) if t >= min_rows and t * n_cols * 4 <= ROW_BLOCK_BYTES]
    return _tile(n_rows, cands or [min_rows])


def _sigmoid(z):
    return 1.0 / (1.0 + jnp.exp(-z))


def _softplus(z):
    return jnp.maximum(z, 0.0) + jnp.log(1.0 + jnp.exp(-jnp.abs(z)))


def _neg_expm1(z):
    series = -z * (1.0 + z * (0.5 + z * (1.0 / 6.0 + z * (1.0 / 24.0 + z * (1.0 / 120.0)))))
    return jnp.where(z > -0.25, series, 1.0 - jnp.exp(z))


_GELU_K = math.sqrt(2.0 / math.pi)


def _gelu_and_grad(z):
    inner = _GELU_K * (z + 0.044715 * z * z * z)
    t = jnp.tanh(inner)
    g = 0.5 * z * (1.0 + t)
    dg = 0.5 * (1.0 + t) + 0.5 * z * (1.0 - t * t) * _GELU_K * (1.0 + 3.0 * 0.044715 * z * z)
    return g, dg


def _rms(xv, g):
    r = lax.rsqrt(jnp.mean(xv * xv, axis=-1, keepdims=True) + RMS_EPS)
    return xv * r * g


def _rms_bwd(xv, g, dy):
    r = lax.rsqrt(jnp.mean(xv * xv, axis=-1, keepdims=True) + RMS_EPS)
    xh = xv * r
    dyg = dy * g
    dx = r * (dyg - xh * jnp.mean(dyg * xh, axis=-1, keepdims=True))
    return dx, jnp.sum(dy * xh, axis=0, keepdims=True)


def _heads(fn, n_heads, *arrs):
    outs = [fn(*[a[:, h * HEAD_DIM:(h + 1) * HEAD_DIM] for a in arrs]) for h in range(n_heads)]
    first = jnp.concatenate([o[0] for o in outs], axis=1) if n_heads > 1 else outs[0][0]
    rest = [functools.reduce(lambda p, q: p + q, [o[i] for o in outs]) for i in range(1, len(outs[0]))]
    return (first, *rest)


def _split3(v):
    hi = v.astype(BF16)
    r1 = v - hi.astype(F32)
    mid = r1.astype(BF16)
    lo = (r1 - mid.astype(F32)).astype(BF16)
    return hi, mid, lo


def _acc_out(ref, first, val):
    @pl.when(first)
    def _():
        ref[...] = val

    @pl.when(jnp.logical_not(first))
    def _():
        ref[...] += val


_DIMS = {'nn': (((1,), (0,)), ((), ())), 'nt': (((1,), (1,)), ((), ())), 'tn': (((0,), (0,)), ((), ()))}


MM_VMEM_BYTES = 36 * 1024 * 1024


MXU_FLOPS = 800e12
HBM_BYTES_S = 3.2e12
VMEM_ADD_BYTES_S = 8e12
STEP_S = 0.35e-6


def _k_tile(K, tm, tn, a, b, o_dtype, res):
    fixed = tm * tn * (2 * jnp.dtype(o_dtype).itemsize + 4 + (8 if res is not None else 0))
    per_k = 2 * (tm * a.dtype.itemsize + tn * b.dtype.itemsize)
    per_k += 2 * tm * (a.dtype.itemsize > 2) + 2 * tn * (b.dtype.itemsize > 2)
    units = K // LANES
    for d in sorted((d for d in range(1, units + 1) if units % d == 0), reverse=True):
        if fixed + d * LANES * per_k <= MM_VMEM_BYTES:
            return d * LANES
    return None


def _mm_tiles(M, N, K, k_span, a, b, o_dtype, res, tn_cands=(2048, 1024, 512, 256, 128)):
    best = None
    for tm in (2048, 1024, 512, 256, 128):
        for tn in tn_cands:
            if M % tm or N % tn:
                continue
            tk = _k_tile(k_span, tm, tn, a, b, o_dtype, res)
            if tk is None:
                continue
            nk = K // tk
            traffic = (M * K * a.dtype.itemsize * (N // tn) + K * N * b.dtype.itemsize * (M // tm)
                       + M * N * (jnp.dtype(o_dtype).itemsize + (4 if res is not None else 0)))
            work = 2.0 * M * N * K / MXU_FLOPS + (M * N * 4 * nk / VMEM_ADD_BYTES_S if nk > 1 else 0.0)
            t = max(work, traffic / HBM_BYTES_S) + (M // tm) * (N // tn) * nk * STEP_S
            if best is None or t < best[0]:
                best = (t, tm, tn, tk)
    assert best is not None, (M, N, K)
    return best[1:]


def _mm_call(name, a, b, mode, grid, a_spec, b_spec, o_spec, o_shape, o_dtype, acc_shape, res=None):
    nk = grid[2]
    dn = _DIMS[mode]

    def body(*refs):
        a_ref, b_ref = refs[:2]
        r_ref = refs[2] if res is not None else None
        o_ref = refs[3] if res is not None else refs[2]
        part = lax.dot_general(a_ref[...].astype(BF16), b_ref[...].astype(BF16), dn, preferred_element_type=F32)

        def finish(r):
            if r_ref is not None:
                r = r + r_ref[...]
            o_ref[...] = r.astype(o_dtype)

        if nk == 1:
            finish(part)
            return
        acc = refs[-1]
        k = pl.program_id(2)

        @pl.when(k == 0)
        def _():
            acc[...] = part

        @pl.when(k > 0)
        def _():
            acc[...] += part

        @pl.when(k == nk - 1)
        def _():
            finish(acc[...])

    ins = [a, b] + ([] if res is None else [res])
    specs = [a_spec, b_spec] + ([] if res is None else [o_spec])
    return pl.pallas_call(
        body, name=name, grid=grid, in_specs=specs, out_specs=o_spec,
        out_shape=jax.ShapeDtypeStruct(o_shape, o_dtype),
        scratch_shapes=[] if nk == 1 else [pltpu.VMEM(acc_shape, F32)],
        compiler_params=_params(('parallel', 'parallel', 'arbitrary')))(*ins)


def _mm(name, a, b, mode, o_dtype, res=None):
    if mode == 'tn':
        K, M = a.shape
    else:
        M, K = a.shape
    N = b.shape[0] if mode == 'nt' else b.shape[1]
    tm, tn, tk = _mm_tiles(M, N, K, K, a, b, o_dtype, res)
    a_spec = (pl.BlockSpec((tk, tm), lambda m, n, k: (k, m)) if mode == 'tn'
              else pl.BlockSpec((tm, tk), lambda m, n, k: (m, k)))
    b_spec = (pl.BlockSpec((tn, tk), lambda m, n, k: (n, k)) if mode == 'nt'
              else pl.BlockSpec((tk, tn), lambda m, n, k: (k, n)))
    o_spec = pl.BlockSpec((tm, tn), lambda m, n, k: (m, n))
    return _mm_call(name, a, b, mode, (M // tm, N // tn, K // tk), a_spec, b_spec, o_spec, (M, N), o_dtype,
                    (tm, tn), res)


def _mm_colsharded(name, a, w, o_dtype, res=None):
    M, K = a.shape
    J, _, Nj = w.shape
    tm, tn, tk = _mm_tiles(M, J * Nj, K, K, a, w, o_dtype, res,
                           tn_cands=[t for t in (2816, 1408, 1024, 512, 256, 128) if Nj % t == 0])
    per = Nj // tn
    return _mm_call(name, a, w, 'nn', (M // tm, J * per, K // tk),
                    pl.BlockSpec((tm, tk), lambda m, n, k: (m, k)),
                    pl.BlockSpec((None, tk, tn), lambda m, n, k: (n // per, k, n % per)),
                    pl.BlockSpec((tm, tn), lambda m, n, k: (m, n)), (M, J * Nj), o_dtype, (tm, tn), res)


def _planes_spec(arr, rows, cols, row_of, col_of):
    if arr.ndim == 2:
        return pl.BlockSpec((rows, cols), lambda m, n, k: (row_of(m, n, k), col_of(m, n, k)))
    per_plane = arr.shape[2] // cols
    return pl.BlockSpec((None, rows, cols),
                        lambda m, n, k: (col_of(m, n, k) // per_plane, row_of(m, n, k), col_of(m, n, k) % per_plane))


def _mm_colsharded_t(name, a, w, o_dtype):
    M = a.shape[-2]
    J, K, Nj = w.shape
    tm, tn, tk = _mm_tiles(M, K, J * Nj, Nj, a, w, o_dtype, None)
    per = Nj // tk
    return _mm_call(name, a, w, 'nt', (M // tm, K // tn, J * per),
                    _planes_spec(a, tm, tk, lambda m, n, k: m, lambda m, n, k: k),
                    pl.BlockSpec((None, tn, tk), lambda m, n, k: (k // per, n, k % per)),
                    pl.BlockSpec((tm, tn), lambda m, n, k: (m, n)), (M, K), o_dtype, (tm, tn))


def _mm_grad_colsharded(name, a, dy, J, o_dtype):
    S, M = a.shape
    Nj = dy.shape[-1] * (dy.shape[0] if dy.ndim == 3 else 1) // J
    tm, tn, tk = _mm_tiles(M, J * Nj, S, S, a, dy, o_dtype, None,
                           tn_cands=[t for t in (2816, 1408, 1024, 512, 256, 128) if Nj % t == 0])
    per = Nj // tn
    return _mm_call(name, a, dy, 'tn', (M // tm, J * per, S // tk),
                    pl.BlockSpec((tk, tm), lambda m, n, k: (k, m)),
                    _planes_spec(dy, tk, tn, lambda m, n, k: k, lambda m, n, k: n),
                    pl.BlockSpec((None, tm, tn), lambda m, n, k: (n // per, m, n % per)), (J, M, Nj), o_dtype, (tm, tn))


def _rows_call(name, body, n_rows, tr, ins, outs):
    return pl.pallas_call(
        body, name=name, grid=(n_rows // tr,), in_specs=[s for _, s in ins], out_specs=[s for _, _, s in outs],
        out_shape=[jax.ShapeDtypeStruct(sh, dt) for sh, dt, _ in outs],
        compiler_params=_params(('arbitrary',)))(*[a for a, _ in ins])


def _rb(tr, w, cb=0):
    return pl.BlockSpec((tr, w), lambda i: (i, cb))


def _fb(shape):
    nd = len(shape)
    return pl.BlockSpec(shape, lambda i: (0,) * nd)


def norm_fwd(name, xv, g):
    S, D = xv.shape
    tr = _tile(S, (256, 128))

    def body(x_ref, g_ref, o_ref):
        o_ref[...] = _rms(x_ref[...], g_ref[...]).astype(BF16)

    return _rows_call(name, body, S, tr, [(xv, _rb(tr, D)), (g, _fb((1, D)))], [((S, D), BF16, _rb(tr, D))])[0]


def norm_bwd(name, xv, g, dy, res=None, want_dx=True):
    S, D = xv.shape
    tr = _tile(S, (256, 128))

    def body(*refs):
        if res is None:
            x_ref, g_ref, dy_ref = refs[:3]
            outs = refs[3:]
            r_ref = None
        else:
            x_ref, g_ref, dy_ref, r_ref = refs[:4]
            outs = refs[4:]
        dx, dg = _rms_bwd(x_ref[...], g_ref[...], dy_ref[...])
        if r_ref is not None:
            dx = dx + r_ref[...]
        if want_dx:
            outs[0][...] = dx
            outs[1][...] = dx.astype(BF16)
        _acc_out(outs[-1], pl.program_id(0) == 0, dg)

    ins = [(xv, _rb(tr, D)), (g, _fb((1, D))), (dy, _rb(tr, D))] + ([] if res is None else [(res, _rb(tr, D))])
    outs = ([((S, D), F32, _rb(tr, D)), ((S, D), BF16, _rb(tr, D))] if want_dx else []) + [((1, D), F32, _fb((1, D)))]
    return _rows_call(name, body, S, tr, ins, outs)


def qkv_fwd(proj, g_q, g_k, FW):
    S = proj.shape[0]
    H = FW // HEAD_DIM
    tr = _tile(S, (256, 128))

    def body(q_ref, k_ref, v_ref, gq_ref, gk_ref, qo, ko, vo):
        qo[...] = _heads(lambda t: (_rms(t, gq_ref[...]),), H, q_ref[...])[0].astype(BF16)
        ko[...] = _heads(lambda t: (_rms(t, gk_ref[...]),), H, k_ref[...])[0].astype(BF16)
        vo[...] = v_ref[...].astype(BF16)

    o = ((S, FW), BF16, _rb(tr, FW))
    return _rows_call('qkv_fwd', body, S, tr,
                      [(proj, _rb(tr, FW, 0)), (proj, _rb(tr, FW, 1)), (proj, _rb(tr, FW, 2)),
                       (g_q, _fb((1, HEAD_DIM))), (g_k, _fb((1, HEAD_DIM)))], [o, o, o])


def qkv_bwd(proj, g_q, g_k, dqn, dkn, FW):
    S = proj.shape[0]
    H = FW // HEAD_DIM
    tr = _tile(S, (256, 128))

    def body(q_ref, k_ref, gq_ref, gk_ref, dq_ref, dk_ref, dqo, dko, dgq, dgk):
        dq, gq = _heads(lambda t, d: _rms_bwd(t, gq_ref[...], d), H, q_ref[...], dq_ref[...])
        dk, gk = _heads(lambda t, d: _rms_bwd(t, gk_ref[...], d), H, k_ref[...], dk_ref[...])
        dqo[...] = dq.astype(BF16)
        dko[...] = dk.astype(BF16)
        first = pl.program_id(0) == 0
        _acc_out(dgq, first, gq)
        _acc_out(dgk, first, gk)

    o = ((S, FW), BF16, _rb(tr, FW))
    og = ((1, HEAD_DIM), F32, _fb((1, HEAD_DIM)))
    return _rows_call('qkv_bwd', body, S, tr,
                      [(proj, _rb(tr, FW, 0)), (proj, _rb(tr, FW, 1)), (g_q, _fb((1, HEAD_DIM))),
                       (g_k, _fb((1, HEAD_DIM))), (dqn, _rb(tr, FW)), (dkn, _rb(tr, FW))], [o, o, og, og])


def _tri(n, upper):
    r = lax.broadcasted_iota(jnp.int32, (n, n), 0)
    c = lax.broadcasted_iota(jnp.int32, (n, n), 1)
    return jnp.where((c >= r) if upper else (c <= r), 1.0, 0.0).astype(BF16)


def _blocked_cumsum(val, S, blk, reverse):
    tri = _tri(blk, reverse)
    order = range(S // blk - 1, -1, -1) if reverse else range(S // blk)
    carry = jnp.zeros((1, LANES), F32)
    outs = {}
    for bi in order:
        part = val[bi * blk:(bi + 1) * blk]
        acc = carry
        for piece in _split3(part):
            acc = acc + jnp.dot(tri, piece, preferred_element_type=F32)
        outs[bi] = acc
        carry = carry + jnp.sum(part, axis=0, keepdims=True)
    return jnp.concatenate([outs[bi] for bi in range(S // blk)], axis=0)


def fgate_fwd(f_raw, b_f_pad):
    S = f_raw.shape[0]
    blk = _tile(S, (256, 128))

    def body(f_ref, b_ref, c_ref):
        z = f_ref[...] + b_ref[...]
        c_ref[...] = _blocked_cumsum(-_softplus(-z), S, blk, False)

    return pl.pallas_call(body, name='fgate_fwd', grid=(1,), in_specs=[_fb((S, LANES)), _fb((1, LANES))],
                          out_specs=_fb((S, LANES)), out_shape=jax.ShapeDtypeStruct((S, LANES), F32),
                          compiler_params=_params(('arbitrary',)))(f_raw, b_f_pad)


def fgate_bwd(f_raw, b_f_pad, dc, H):
    S = f_raw.shape[0]
    blk = _tile(S, (256, 128))

    def body(f_ref, b_ref, dc_ref, df_ref, db_ref):
        z = f_ref[...] + b_ref[...]
        dlogf = _blocked_cumsum(dc_ref[...], S, blk, True)
        lane = lax.broadcasted_iota(jnp.int32, (S, LANES), 1)
        df = jnp.where(lane < H, dlogf * _sigmoid(-z), 0.0)
        df_ref[...] = df.astype(BF16)
        db_ref[...] = jnp.sum(df, axis=0, keepdims=True)

    return pl.pallas_call(body, name='fgate_bwd', grid=(1,),
                          in_specs=[_fb((S, LANES)), _fb((1, LANES)), _fb((S, LANES))],
                          out_specs=[_fb((S, LANES)), _fb((1, LANES))],
                          out_shape=[jax.ShapeDtypeStruct((S, LANES), BF16), jax.ShapeDtypeStruct((1, LANES), F32)],
                          compiler_params=_params(('arbitrary',)))(f_raw, b_f_pad, dc)


def _fox_logits(q, k, c_blk, ct_blk, h, T, diagonal):
    s = lax.dot_general(q, k, _DIMS['nt'], preferred_element_type=F32) * (1.0 / math.sqrt(HEAD_DIM))
    lane = lax.broadcasted_iota(jnp.int32, c_blk.shape, 1)
    cq = jnp.sum(jnp.where(lane == h, c_blk, 0.0), axis=1, keepdims=True)
    sub = lax.broadcasted_iota(jnp.int32, ct_blk.shape, 0)
    ck = jnp.sum(jnp.where(sub == h, ct_blk, 0.0), axis=0, keepdims=True)
    s = s + cq - ck
    if not diagonal:
        return s
    rows = lax.broadcasted_iota(jnp.int32, (T, T), 0)
    cols = lax.broadcasted_iota(jnp.int32, (T, T), 1)
    return jnp.where(cols <= rows, s, -jnp.inf)


def _below_and_on_diagonal(q_blk, k_blk, step):
    @pl.when(k_blk < q_blk)
    def _():
        step(False)

    @pl.when(k_blk == q_blk)
    def _():
        step(True)


def fox_fwd(qn, kn, vb, c, ct, T):
    S, FW = qn.shape
    H = FW // HEAD_DIM
    Hp = ct.shape[0]
    n = S // T

    HB = _tile(H, (8, 4, 2, 1))
    W2 = HB * HEAD_DIM

    def body(q_ref, k_ref, v_ref, c_ref, ct_ref, o_ref, lse_ref, m_s, l_s, acc_s):
        hb, i, j = pl.program_id(0), pl.program_id(1), pl.program_id(2)

        @pl.when(j == 0)
        def _():
            m_s[...] = jnp.full_like(m_s, -jnp.inf)
            l_s[...] = jnp.zeros_like(l_s)
            acc_s[...] = jnp.zeros_like(acc_s)

        def step(diagonal):
            for hh in range(HB):
                sl = slice(hh * HEAD_DIM, (hh + 1) * HEAD_DIM)
                s = _fox_logits(q_ref[:, sl], k_ref[:, sl], c_ref[...], ct_ref[...], hb * HB + hh, T, diagonal)
                m_old = m_s[hh]
                m_new = jnp.maximum(m_old, jnp.max(s, axis=1, keepdims=True))
                alpha = jnp.exp(m_old - m_new)
                p = jnp.exp(s - m_new)
                l_s[hh] = alpha * l_s[hh] + jnp.sum(p, axis=1, keepdims=True)
                acc_s[hh] = alpha * acc_s[hh] + jnp.dot(p.astype(BF16), v_ref[:, sl], preferred_element_type=F32)
                m_s[hh] = m_new

        _below_and_on_diagonal(i, j, step)

        @pl.when(j == i)
        def _():
            for hh in range(HB):
                o_ref[:, hh * HEAD_DIM:(hh + 1) * HEAD_DIM] = acc_s[hh] / l_s[hh]
                lse_ref[hh] = jnp.broadcast_to(m_s[hh] + jnp.log(l_s[hh]), (T, LANES))

    qs = pl.BlockSpec((T, W2), lambda h, i, j: (i, h))
    ks = pl.BlockSpec((T, W2), lambda h, i, j: (jnp.minimum(j, i), h))
    return pl.pallas_call(
        body, name='fox_fwd', grid=(H // HB, n, n),
        in_specs=[qs, ks, ks, pl.BlockSpec((T, LANES), lambda h, i, j: (i, 0)),
                  pl.BlockSpec((Hp, T), lambda h, i, j: (0, jnp.minimum(j, i)))],
        out_specs=[qs, pl.BlockSpec((HB, T, LANES), lambda h, i, j: (h, i, 0))],
        out_shape=[jax.ShapeDtypeStruct((S, FW), F32), jax.ShapeDtypeStruct((H, S, LANES), F32)],
        scratch_shapes=[pltpu.VMEM((HB, T, 1), F32), pltpu.VMEM((HB, T, 1), F32), pltpu.VMEM((HB, T, HEAD_DIM), F32)],
        compiler_params=_params(('parallel', 'parallel', 'arbitrary')))(qn, kn, vb, c, ct)


def _fox_p_ds(q_ref, k_ref, v_ref, do_ref, c_ref, ct_ref, lse_ref, dl_ref, h, T, diagonal):
    s = _fox_logits(q_ref[...], k_ref[...], c_ref[...], ct_ref[...], h, T, diagonal)
    p = jnp.exp(s - jnp.tile(lse_ref[...], (1, T // LANES)))
    dp = lax.dot_general(do_ref[...], v_ref[...], _DIMS['nt'], preferred_element_type=F32)
    ds = p * (dp - jnp.tile(dl_ref[...], (1, T // LANES)))
    return p, dp, ds


def fox_bwd_q(qn, kn, vb, do, c, ct, lse, dl, T):
    S, FW = qn.shape
    H = FW // HEAD_DIM
    Hp = ct.shape[0]
    n = S // T
    HB = _tile(H, (8, 4, 2, 1))
    W2 = HB * HEAD_DIM

    def body(q_ref, k_ref, v_ref, do_ref, c_ref, ct_ref, lse_ref, dl_ref, dq_ref, dl2_ref, acc_s, rs_s):
        hb, i, j = pl.program_id(0), pl.program_id(1), pl.program_id(2)

        @pl.when(j == 0)
        def _():
            acc_s[...] = jnp.zeros_like(acc_s)
            rs_s[...] = jnp.zeros_like(rs_s)

        def step(diagonal):
            for hh in range(HB):
                sl = slice(hh * HEAD_DIM, (hh + 1) * HEAD_DIM)
                p, dp, ds = _fox_p_ds(q_ref.at[:, sl], k_ref.at[:, sl], v_ref.at[:, sl], do_ref.at[:, sl], c_ref, ct_ref,
                                      lse_ref.at[hh], dl_ref.at[hh], hb * HB + hh, T, diagonal)
                acc_s[hh] += jnp.dot(ds.astype(BF16), k_ref[:, sl], preferred_element_type=F32)
                rs_s[hh] += jnp.sum(p * dp, axis=1, keepdims=True)

        _below_and_on_diagonal(i, j, step)

        @pl.when(j == i)
        def _():
            for hh in range(HB):
                dq_ref[:, hh * HEAD_DIM:(hh + 1) * HEAD_DIM] = acc_s[hh] * (1.0 / math.sqrt(HEAD_DIM))
                dl2_ref[hh] = jnp.broadcast_to(rs_s[hh], (T, LANES))

    qs = pl.BlockSpec((T, W2), lambda h, i, j: (i, h))
    ks = pl.BlockSpec((T, W2), lambda h, i, j: (jnp.minimum(j, i), h))
    st = pl.BlockSpec((HB, T, LANES), lambda h, i, j: (h, i, 0))
    return pl.pallas_call(
        body, name='fox_bwd_q', grid=(H // HB, n, n),
        in_specs=[qs, ks, ks, qs, pl.BlockSpec((T, LANES), lambda h, i, j: (i, 0)),
                  pl.BlockSpec((Hp, T), lambda h, i, j: (0, jnp.minimum(j, i))), st, st],
        out_specs=[qs, st], out_shape=[jax.ShapeDtypeStruct((S, FW), F32), jax.ShapeDtypeStruct((H, S, LANES), F32)],
        scratch_shapes=[pltpu.VMEM((HB, T, HEAD_DIM), F32), pltpu.VMEM((HB, T, 1), F32)],
        compiler_params=_params(('parallel', 'parallel', 'arbitrary')))(qn, kn, vb, do, c, ct, lse, dl)


def fox_bwd_kv(qn, kn, vb, do, c, ct, lse, dl, T):
    S, FW = qn.shape
    H = FW // HEAD_DIM
    Hp = ct.shape[0]
    n = S // T

    HB = _tile(H, (8, 4, 2, 1))
    W2 = HB * HEAD_DIM

    def body(q_ref, k_ref, v_ref, do_ref, c_ref, ct_ref, lse_ref, dl_ref, dk_ref, dv_ref, dc_ref, dk_s, dv_s, dc_s):
        hb, j, i = pl.program_id(0), pl.program_id(1), pl.program_id(2)

        @pl.when(i == 0)
        def _():
            dk_s[...] = jnp.zeros_like(dk_s)
            dv_s[...] = jnp.zeros_like(dv_s)
            dc_s[...] = jnp.zeros_like(dc_s)

        def step(diagonal):
            for hh in range(HB):
                sl = slice(hh * HEAD_DIM, (hh + 1) * HEAD_DIM)
                p, _, ds = _fox_p_ds(q_ref.at[:, sl], k_ref.at[:, sl], v_ref.at[:, sl], do_ref.at[:, sl], c_ref, ct_ref,
                                     lse_ref.at[hh], dl_ref.at[hh], hb * HB + hh, T, diagonal)
                dv_s[hh] += lax.dot_general(p.astype(BF16), do_ref[:, sl], _DIMS['tn'], preferred_element_type=F32)
                dk_s[hh] += lax.dot_general(ds.astype(BF16), q_ref[:, sl], _DIMS['tn'], preferred_element_type=F32)
                dc_s[hh] += jnp.sum(ds, axis=0, keepdims=True)

        _below_and_on_diagonal(i, j, step)

        @pl.when(i == n - 1)
        def _():
            for hh in range(HB):
                sl = slice(hh * HEAD_DIM, (hh + 1) * HEAD_DIM)
                dk_ref[:, sl] = dk_s[hh] * (1.0 / math.sqrt(HEAD_DIM))
                dv_ref[:, sl] = dv_s[hh].astype(BF16)
                dc_ref[hh] = -dc_s[hh]

    qs = pl.BlockSpec((T, W2), lambda h, j, i: (jnp.maximum(i, j), h))
    ks = pl.BlockSpec((T, W2), lambda h, j, i: (j, h))
    st = pl.BlockSpec((HB, T, LANES), lambda h, j, i: (h, jnp.maximum(i, j), 0))
    return pl.pallas_call(
        body, name='fox_bwd_kv', grid=(H // HB, n, n),
        in_specs=[qs, ks, ks, qs, pl.BlockSpec((T, LANES), lambda h, j, i: (jnp.maximum(i, j), 0)),
                  pl.BlockSpec((Hp, T), lambda h, j, i: (0, j)), st, st],
        out_specs=[ks, ks, pl.BlockSpec((HB, 1, T), lambda h, j, i: (h, 0, j))],
        out_shape=[jax.ShapeDtypeStruct((S, FW), F32), jax.ShapeDtypeStruct((S, FW), BF16),
                   jax.ShapeDtypeStruct((H, 1, S), F32)],
        scratch_shapes=[pltpu.VMEM((HB, T, HEAD_DIM), F32), pltpu.VMEM((HB, T, HEAD_DIM), F32),
                        pltpu.VMEM((HB, 1, T), F32)],
        compiler_params=_params(('parallel', 'parallel', 'arbitrary')))(qn, kn, vb, do, c, ct, lse, dl)


def _shift_down(v, d, rows, fill):
    return jnp.where(rows >= d, pltpu.roll(v, d, 0), fill)


def _shift_up(v, d, rows, S, fill):
    return jnp.where(rows < S - d, pltpu.roll(v, S - d, 0), fill)


SUBLANES = 8


def _scan_by_doubling(a, b, pos, span, reverse):
    n = a.shape[0]
    d = 1
    while d < span:
        if reverse:
            keep = pos < span - d
            a_s, b_s = jnp.where(keep, pltpu.roll(a, n - d, 0), 1.0), jnp.where(keep, pltpu.roll(b, n - d, 0), 0.0)
        else:
            keep = pos >= d
            a_s, b_s = jnp.where(keep, pltpu.roll(a, d, 0), 1.0), jnp.where(keep, pltpu.roll(b, d, 0), 0.0)
        b = a * b_s + b
        a = a * a_s
        d *= 2
    return a, b


def _scan(a, b, rows, S, reverse, scr):
    groups = S // SUBLANES
    a, b = _scan_by_doubling(a, b, jnp.bitwise_and(rows, SUBLANES - 1), SUBLANES, reverse)
    scr[0][...] = a
    scr[1][...] = b
    edge = 0 if reverse else SUBLANES - 1
    a_g = scr[0][pl.ds(edge, groups, stride=SUBLANES), :]
    b_g = scr[1][pl.ds(edge, groups, stride=SUBLANES), :]
    g_pos = lax.broadcasted_iota(jnp.int32, (groups, LANES), 0)
    _, h_g = _scan_by_doubling(a_g, b_g, g_pos, groups, reverse)
    if reverse:
        carry = jnp.where(g_pos < groups - 1, pltpu.roll(h_g, groups - 1, 0), 0.0)
    else:
        carry = jnp.where(g_pos >= 1, pltpu.roll(h_g, 1, 0), 0.0)
    for r in range(SUBLANES):
        scr[0][pl.ds(r, groups, stride=SUBLANES), :] = carry
    return b + a * scr[0][...]


def _lru_forward(u, cw, cb, wra, bra, wri, bri, lam, rows, scr):
    uc = cb + cw[CONV_W - 1] * u
    for d in range(1, CONV_W):
        uc = uc + cw[CONV_W - 1 - d] * _shift_down(u, d, rows, 0.0)
    ucb = uc.astype(BF16)
    r = _sigmoid(jnp.dot(ucb, wra.astype(BF16), preferred_element_type=F32) + bra)
    ig = _sigmoid(jnp.dot(ucb, wri.astype(BF16), preferred_element_type=F32) + bri)
    sp = _softplus(-lam)
    log_a = -LRU_C * r * sp
    a = jnp.exp(log_a)
    sq = jnp.sqrt(_neg_expm1(2.0 * log_a))
    iu = ig * uc
    hseq = _scan(a, sq * iu, rows, u.shape[0], False, scr)
    return uc, ucb, r, ig, sp, a, sq, iu, hseq


def _lru_specs(S, n_u, n_g):
    col = lambda off: pl.BlockSpec((S, LANES), lambda cbk: (0, off + cbk))
    vec = pl.BlockSpec((1, LANES), lambda cbk: (0, cbk))
    mat = pl.BlockSpec((None, LANES, LANES), lambda cbk: (cbk, 0, 0))
    cw = pl.BlockSpec((CONV_W, LANES), lambda cbk: (0, cbk))
    return col, vec, mat, cw


def lru_fwd(proj, conv_w, conv_b, w_ra, b_ra, w_ri, b_ri, lam, u_off, g_off):
    S = proj.shape[0]
    nb = w_ra.shape[0]
    col, vec, mat, cws = _lru_specs(S, u_off, g_off)

    def body(u_ref, g_ref, cw_ref, cb_ref, wra_ref, bra_ref, wri_ref, bri_ref, lam_ref, y_ref, scr0, scr1):
        rows = lax.broadcasted_iota(jnp.int32, (S, LANES), 0)
        cw = [cw_ref[t:t + 1, :] for t in range(CONV_W)]
        hseq = _lru_forward(u_ref[...], cw, cb_ref[...], wra_ref[...], bra_ref[...], wri_ref[...],
                            bri_ref[...], lam_ref[...], rows, (scr0, scr1))[-1]
        y_ref[...] = hseq * _gelu_and_grad(g_ref[...])[0]

    return pl.pallas_call(
        body, name='lru_fwd', grid=(nb,),
        in_specs=[col(u_off), col(g_off), cws, vec, mat, vec, mat, vec, vec], out_specs=col(0),
        out_shape=jax.ShapeDtypeStruct((S, nb * LANES), F32),
        scratch_shapes=[pltpu.VMEM((S, LANES), F32), pltpu.VMEM((S, LANES), F32)],
        compiler_params=_params(('parallel',)))(proj, proj, conv_w, conv_b, w_ra, b_ra, w_ri, b_ri, lam)


def lru_bwd(proj, dy, conv_w, conv_b, w_ra, b_ra, w_ri, b_ri, lam, u_off, g_off):
    S = proj.shape[0]
    nb = w_ra.shape[0]
    LW = nb * LANES
    col, vec, mat, cws = _lru_specs(S, u_off, g_off)

    def body(u_ref, g_ref, dy_ref, cw_ref, cb_ref, wra_ref, bra_ref, wri_ref, bri_ref, lam_ref,
             du_ref, dg_ref, dcw_ref, dcb_ref, dwra_ref, dbra_ref, dwri_ref, dbri_ref, dlam_ref, scr0, scr1):
        rows = lax.broadcasted_iota(jnp.int32, (S, LANES), 0)
        u, lam_v = u_ref[...], lam_ref[...]
        cw = [cw_ref[t:t + 1, :] for t in range(CONV_W)]
        wra, wri = wra_ref[...].astype(BF16), wri_ref[...].astype(BF16)
        uc, ucb, r, ig, sp, a, sq, iu, hseq = _lru_forward(u, cw, cb_ref[...], wra, bra_ref[...], wri, bri_ref[...],
                                                           lam_v, rows, (scr0, scr1))
        gl, dgl = _gelu_and_grad(g_ref[...])
        dy_v = dy_ref[...]
        dg_ref[...] = (dy_v * hseq * dgl).astype(BF16)
        G = _scan(_shift_up(a, 1, rows, S, 0.0), dy_v * gl, rows, S, True, (scr0, scr1))
        da = G * _shift_down(hseq, 1, rows, 0.0)
        diu = G * sq
        dsq = G * iu
        dlog_a = da * a - dsq * a * a / jnp.maximum(sq, 1e-30)
        dr = dlog_a * (-LRU_C * sp)
        dsp = jnp.sum(dlog_a * (-LRU_C * r), axis=0, keepdims=True)
        dlam_ref[...] = -dsp * _sigmoid(-lam_v)
        dzr = dr * r * (1.0 - r)
        dzi = diu * uc * ig * (1.0 - ig)
        dzrb, dzib = dzr.astype(BF16), dzi.astype(BF16)
        duc = (diu * ig + lax.dot_general(dzrb, wra, _DIMS['nt'], preferred_element_type=F32)
               + lax.dot_general(dzib, wri, _DIMS['nt'], preferred_element_type=F32))
        dwra_ref[...] = lax.dot_general(ucb, dzrb, _DIMS['tn'], preferred_element_type=F32)
        dwri_ref[...] = lax.dot_general(ucb, dzib, _DIMS['tn'], preferred_element_type=F32)
        dbra_ref[...] = jnp.sum(dzr, axis=0, keepdims=True)
        dbri_ref[...] = jnp.sum(dzi, axis=0, keepdims=True)
        dcb_ref[...] = jnp.sum(duc, axis=0, keepdims=True)
        du = cw[CONV_W - 1] * duc
        dcw_ref[CONV_W - 1:CONV_W, :] = jnp.sum(duc * u, axis=0, keepdims=True)
        for d in range(1, CONV_W):
            du = du + cw[CONV_W - 1 - d] * _shift_up(duc, d, rows, S, 0.0)
            dcw_ref[CONV_W - 1 - d:CONV_W - d, :] = jnp.sum(duc * _shift_down(u, d, rows, 0.0), axis=0, keepdims=True)
        du_ref[...] = du.astype(BF16)

    sd = jax.ShapeDtypeStruct
    return pl.pallas_call(
        body, name='lru_bwd', grid=(nb,),
        in_specs=[col(u_off), col(g_off), col(0), cws, vec, mat, vec, mat, vec, vec],
        out_specs=[col(0), col(0), cws, vec, mat, vec, mat, vec, vec],
        out_shape=[sd((S, LW), BF16), sd((S, LW), BF16), sd((CONV_W, LW), F32), sd((1, LW), F32),
                   sd((nb, LANES, LANES), F32), sd((1, LW), F32), sd((nb, LANES, LANES), F32), sd((1, LW), F32),
                   sd((1, LW), F32)],
        scratch_shapes=[pltpu.VMEM((S, LANES), F32), pltpu.VMEM((S, LANES), F32)],
        compiler_params=_params(('parallel',)))(proj, proj, dy, conv_w, conv_b, w_ra, b_ra, w_ri, b_ri, lam)


def mix_fwd(o_fox, y_lru, g_fox, g_lru):
    S, FW = o_fox.shape
    tr = _tile(S, (256, 128))

    def body(o_ref, y_ref, gf_ref, gl_ref, m_ref):
        m_ref[...] = jnp.concatenate([_rms(o_ref[...], gf_ref[...]), _rms(y_ref[...], gl_ref[...])],
                                     axis=1).astype(BF16)

    return _rows_call('mix_fwd', body, S, tr,
                      [(o_fox, _rb(tr, FW)), (y_lru, _rb(tr, FW)), (g_fox, _fb((1, FW))), (g_lru, _fb((1, FW)))],
                      [((S, 2 * FW), BF16, _rb(tr, 2 * FW))])[0]


def mix_bwd(o_fox, y_lru, g_fox, g_lru, dmix):
    S, FW = o_fox.shape
    H = FW // HEAD_DIM
    tr = _tile(S, (256, 128))

    def body(o_ref, y_ref, gf_ref, gl_ref, df_ref, dl_ref, do_ref, dlt_ref, dy_ref, dgf_ref, dgl_ref):
        o = o_ref[...]
        do, dgf = _rms_bwd(o, gf_ref[...], df_ref[...])
        dyl, dgl = _rms_bwd(y_ref[...], gl_ref[...], dl_ref[...])
        do_ref[...] = do.astype(BF16)
        dy_ref[...] = dyl
        prod = do * o
        for h in range(H):
            dlt_ref[h] = jnp.broadcast_to(
                jnp.sum(prod[:, h * HEAD_DIM:(h + 1) * HEAD_DIM], axis=1, keepdims=True), (tr, LANES))
        first = pl.program_id(0) == 0
        _acc_out(dgf_ref, first, dgf)
        _acc_out(dgl_ref, first, dgl)

    g = _fb((1, FW))
    return _rows_call('mix_bwd', body, S, tr,
                      [(o_fox, _rb(tr, FW)), (y_lru, _rb(tr, FW)), (g_fox, g), (g_lru, g), (dmix, _rb(tr, FW, 0)),
                       (dmix, _rb(tr, FW, 1))],
                      [((S, FW), BF16, _rb(tr, FW)), ((H, S, LANES), F32, pl.BlockSpec((H, tr, LANES), lambda i: (0, i, 0))),
                       ((S, FW), F32, _rb(tr, FW)), ((1, FW), F32, g), ((1, FW), F32, g)])


def _xattn_heads(cq_raw, ckv, g_cq, g_ck, XW):
    out = []
    for h in range(XW // HEAD_DIM):
        sl = slice(h * HEAD_DIM, (h + 1) * HEAD_DIM)
        out.append((cq_raw[:, sl], _rms(cq_raw[:, sl], g_cq), ckv[:, sl], _rms(ckv[:, sl], g_ck),
                    ckv[:, XW + h * HEAD_DIM:XW + (h + 1) * HEAD_DIM].astype(BF16)))
    return out


def xattn_fwd(cq_raw, ckv, g_cq, g_ck):
    S, XW = cq_raw.shape
    M = ckv.shape[0]
    tr = _tile(S, (512, 256, 128))

    def body(q_ref, kv_ref, gq_ref, gk_ref, o_ref):
        outs = []
        for _, qn, _, kn, v in _xattn_heads(q_ref[...], kv_ref[...], gq_ref[...], gk_ref[...], XW):
            s = lax.dot_general(qn.astype(BF16), kn.astype(BF16), _DIMS['nt'], preferred_element_type=F32)
            s = s / math.sqrt(HEAD_DIM)
            p = jnp.exp(s - jnp.max(s, axis=1, keepdims=True))
            p = p / jnp.sum(p, axis=1, keepdims=True)
            outs.append(jnp.dot(p.astype(BF16), v, preferred_element_type=F32))
        o_ref[...] = jnp.concatenate(outs, axis=1).astype(BF16)

    g = _fb((1, HEAD_DIM))
    return _rows_call('xattn_fwd', body, S, tr,
                      [(cq_raw, _rb(tr, XW)), (ckv, _fb((M, 2 * XW))), (g_cq, g), (g_ck, g)],
                      [((S, XW), BF16, _rb(tr, XW))])[0]


def xattn_bwd(cq_raw, ckv, g_cq, g_ck, do):
    S, XW = cq_raw.shape
    M = ckv.shape[0]
    tr = _tile(S, (512, 256, 128))
    n = S // tr

    def body(q_ref, kv_ref, gq_ref, gk_ref, do_ref, dq_ref, dkv_ref, dgq_ref, dgk_ref):
        i = pl.program_id(0)
        do_v = do_ref[...]
        dqs, dkn, dvs = [], [], []
        dgq = jnp.zeros((1, HEAD_DIM), F32)
        for h, (q_raw, qn, _, kn, v) in enumerate(_xattn_heads(q_ref[...], kv_ref[...], gq_ref[...], gk_ref[...], XW)):
            qb, kb = qn.astype(BF16), kn.astype(BF16)
            doh = do_v[:, h * HEAD_DIM:(h + 1) * HEAD_DIM]
            s = lax.dot_general(qb, kb, _DIMS['nt'], preferred_element_type=F32) / math.sqrt(HEAD_DIM)
            p = jnp.exp(s - jnp.max(s, axis=1, keepdims=True))
            p = p / jnp.sum(p, axis=1, keepdims=True)
            dp = lax.dot_general(doh, v, _DIMS['nt'], preferred_element_type=F32)
            ds = (p * (dp - jnp.sum(p * dp, axis=1, keepdims=True)) / math.sqrt(HEAD_DIM)).astype(BF16)
            dvs.append(lax.dot_general(p.astype(BF16), doh, _DIMS['tn'], preferred_element_type=F32))
            dkn.append(lax.dot_general(ds, qb, _DIMS['tn'], preferred_element_type=F32))
            dq, g1 = _rms_bwd(q_raw, gq_ref[...], jnp.dot(ds, kb, preferred_element_type=F32))
            dqs.append(dq)
            dgq = dgq + g1
        dq_ref[...] = jnp.concatenate(dqs, axis=1).astype(BF16)
        first = i == 0
        _acc_out(dgq_ref, first, dgq)
        _acc_out(dkv_ref, first, jnp.concatenate(dkn + dvs, axis=1))

        @pl.when(i == n - 1)
        def _():
            kv = kv_ref[...]
            acc = dkv_ref[...]
            dk, gk = _heads(lambda t, d: _rms_bwd(t, gk_ref[...], d), XW // HEAD_DIM, kv[:, :XW], acc[:, :XW])
            dkv_ref[:, :XW] = dk
            dgk_ref[...] = gk

    g = _fb((1, HEAD_DIM))
    return _rows_call('xattn_bwd', body, S, tr,
                      [(cq_raw, _rb(tr, XW)), (ckv, _fb((M, 2 * XW))), (g_cq, g), (g_ck, g), (do, _rb(tr, XW))],
                      [((S, XW), BF16, _rb(tr, XW)), ((M, 2 * XW), F32, _fb((M, 2 * XW))), ((1, HEAD_DIM), F32, g),
                       ((1, HEAD_DIM), F32, g)])


def gate_up_fwd(hf, w, F):
    S, D = hf.shape
    J, _, Nj = w.shape
    tm = _tile(S, (2048, 1024, 512, 256, 128))
    tn = _tile(Nj, (256, 128))
    per = Nj // tn
    half = J // 2 * per

    def body(a_ref, bg_ref, bu_ref, gu_ref, act_ref):
        a = a_ref[...]
        g = jnp.dot(a, bg_ref[...], preferred_element_type=F32)
        u = jnp.dot(a, bu_ref[...], preferred_element_type=F32)
        gu_ref[0] = g
        gu_ref[1] = u
        act_ref[...] = (g * _sigmoid(g) * u).astype(BF16)

    return pl.pallas_call(
        body, name='proj_gate_up', grid=(S // tm, half),
        in_specs=[pl.BlockSpec((tm, D), lambda m, n: (m, 0)),
                  pl.BlockSpec((None, D, tn), lambda m, n: (n // per, 0, n % per)),
                  pl.BlockSpec((None, D, tn), lambda m, n: ((n + half) // per, 0, n % per))],
        out_specs=[pl.BlockSpec((2, tm, tn), lambda m, n: (0, m, n)), pl.BlockSpec((tm, tn), lambda m, n: (m, n))],
        out_shape=[jax.ShapeDtypeStruct((2, S, F), F32), jax.ShapeDtypeStruct((S, F), BF16)],
        compiler_params=_params(('parallel', 'parallel')))(hf, w, w)


def down_bwd_x(dyb, w_down, gu, after):
    S, D = dyb.shape
    F = w_down.shape[0]
    tm = _tile(S, (1024, 512, 256, 128))
    tn = _tile(F, (512, 256, 128))

    def body(a_ref, b_ref, gu_ref, after_ref, o_ref):
        da = lax.dot_general(a_ref[...], b_ref[...], _DIMS['nt'], preferred_element_type=F32)
        g = gu_ref[0]
        sg = _sigmoid(g)
        o_ref[0] = (da * gu_ref[1] * sg * (1.0 + g * (1.0 - sg))).astype(BF16)
        o_ref[1] = (da * g * sg).astype(BF16)

    planes = pl.BlockSpec((2, tm, tn), lambda m, n: (0, m, n))
    return pl.pallas_call(
        body, name='bwd_down_x', grid=(S // tm, F // tn),
        in_specs=[pl.BlockSpec((tm, D), lambda m, n: (m, 0)), pl.BlockSpec((tn, D), lambda m, n: (n, 0)), planes, ANY],
        out_specs=planes, out_shape=jax.ShapeDtypeStruct((2, S, F), BF16),
        compiler_params=_params(('parallel', 'parallel')))(dyb, w_down, gu, after)


def down_fwd_loss(act, w_down, x2, target):
    S, F = act.shape
    D = w_down.shape[1]
    tm, tn, tk = _mm_tiles(S, D, F, F, act, w_down, F32, x2, tn_cands=(512, 256, 128))
    nk = F // tk

    def body(a_ref, b_ref, x_ref, t_ref, d_ref, db_ref, l_ref, acc):
        m, n, k = pl.program_id(0), pl.program_id(1), pl.program_id(2)
        part = jnp.dot(a_ref[...], b_ref[...], preferred_element_type=F32)

        @pl.when(k == 0)
        def _():
            acc[...] = part

        @pl.when(k > 0)
        def _():
            acc[...] += part

        @pl.when(k == nk - 1)
        def _():
            err = acc[...] + x_ref[...] - t_ref[...]
            d = err * (1.0 / D)
            d_ref[...] = d
            db_ref[...] = d.astype(BF16)
            tot = jnp.sum(jnp.sum(err * err, axis=1, keepdims=True), axis=0, keepdims=True) * (0.5 / D)
            _acc_out(l_ref, jnp.logical_and(m == 0, n == 0), jnp.broadcast_to(tot, (1, LANES)))

    tile = pl.BlockSpec((tm, tn), lambda m, n, k: (m, n))
    return pl.pallas_call(
        body, name='proj_down', grid=(S // tm, D // tn, nk),
        in_specs=[pl.BlockSpec((tm, tk), lambda m, n, k: (m, k)), pl.BlockSpec((tk, tn), lambda m, n, k: (k, n)), tile, tile],
        out_specs=[tile, tile, pl.BlockSpec((1, LANES), lambda m, n, k: (0, 0))],
        out_shape=[jax.ShapeDtypeStruct((S, D), F32), jax.ShapeDtypeStruct((S, D), BF16),
                   jax.ShapeDtypeStruct((1, LANES), F32)],
        scratch_shapes=[pltpu.VMEM((tm, tn), F32)],
        compiler_params=_params(('arbitrary', 'arbitrary', 'arbitrary')))(act, w_down, x2, target)


def _adamw_math(w, gv, m, v):
    mn = ADAM_B1 * m + (1.0 - ADAM_B1) * gv
    vn = ADAM_B2 * v + (1.0 - ADAM_B2) * (gv * gv)
    m_hat = mn / (1.0 - ADAM_B1 ** ADAM_STEP)
    v_hat = vn / (1.0 - ADAM_B2 ** ADAM_STEP)
    return -ADAM_LR * (m_hat / (jnp.sqrt(v_hat) + ADAM_EPS) + ADAM_WD * w), mn, vn


def adamw(name, w, g, m, v):
    R, C = w.shape
    tr = _row_tile(R, C)

    def body(w_ref, g_ref, m_ref, v_ref, d_ref, mo_ref, vo_ref):
        d_ref[...], mo_ref[...], vo_ref[...] = _adamw_math(w_ref[...], g_ref[...], m_ref[...], v_ref[...])

    spec = _rb(tr, C)
    return _rows_call(name, body, R, tr, [(w, spec), (g, spec), (m, spec), (v, spec)], [((R, C), F32, spec)] * 3)


def adamw_halves(name, w, mine, other, m, v, c_idx):
    R, C = w.shape
    hr = R // 2
    tr = _row_tile(hr, C)

    def body(c_ref, w_ref, a_ref, b_ref, m_ref, v_ref, g_ref, d_ref, mo_ref, vo_ref):
        gv = jnp.where(pl.program_id(0) == c_ref[0], a_ref[...], b_ref[...])
        g_ref[...] = gv
        d_ref[...], mo_ref[...], vo_ref[...] = _adamw_math(w_ref[...], gv, m_ref[...], v_ref[...])

    full = pl.BlockSpec((None, tr, C), lambda hh, i, c_ref: (hh, i, 0))
    mine_spec = pl.BlockSpec((tr, C), lambda hh, i, c_ref: (jnp.where(hh == c_ref[0], i, 0), 0))
    other_spec = pl.BlockSpec((tr, C), lambda hh, i, c_ref: (jnp.where(hh == c_ref[0], 0, i), 0))
    outs = pl.pallas_call(
        body, name=name,
        grid_spec=pltpu.PrefetchScalarGridSpec(num_scalar_prefetch=1, grid=(2, hr // tr),
                                               in_specs=[full, mine_spec, other_spec, full, full], out_specs=[full] * 4),
        out_shape=[jax.ShapeDtypeStruct((2, hr, C), F32)] * 4,
        compiler_params=_params(('parallel', 'parallel')))(
            c_idx, w.reshape(2, hr, C), mine, other, m.reshape(2, hr, C), v.reshape(2, hr, C))
    return [o.reshape(R, C) for o in outs]


def _place():
    x, y, c = lax.axis_index('x'), lax.axis_index('y'), lax.axis_index('c')
    return x, y, c, [(1 - x, y), (x, 1 - y), (1 - x, 1 - y)]


def _rcopy(src, dst, ssem, rsem, dev):
    return pltpu.make_async_remote_copy(src_ref=src, dst_ref=dst, send_sem=ssem, recv_sem=rsem, device_id=dev,
                                        device_id_type=MESH)


HBM = pl.BlockSpec(memory_space=pltpu.HBM)
SEM = pl.BlockSpec(memory_space=pltpu.SEMAPHORE)
EFFECT = pltpu.SideEffectType.DATAFLOW_SIDE_EFFECTING


def _in_hbm(a):
    return pltpu.with_memory_space_constraint(a, pltpu.HBM)


def _rows_part(shape, whole, half):
    return pl.ds(0, shape[0]) if whole else pl.ds(half * (shape[0] // 2), shape[0] // 2)


def gather_start(name, shards, whole):
    nT = len(shards)

    def body(*refs):
        srcs, lands = refs[:nT], refs[nT:2 * nT]
        ssem, rsem, token = refs[2 * nT], refs[2 * nT + 1], refs[-1]
        x, y, c, chips = _place()
        for t in range(nT):
            rows = _rows_part(shards[t].shape, whole[t], c)
            for k, (px, py) in enumerate(chips):
                _rcopy(srcs[t].at[rows], lands[t].at[2 * x + y, rows], ssem.at[3 * t + k], rsem.at[3 * t + k],
                       (px, py, c)).start()
        token[...] = jnp.zeros_like(token)

    zones = [lax.empty((N_CHIPS,) + s.shape, s.dtype) for s in shards]
    outs = pl.pallas_call(
        body, name=name,
        out_shape=(pltpu.SemaphoreType.DMA((3 * nT,)), pltpu.SemaphoreType.DMA((3 * nT,)),
                   *[pltpu.HBM(s.shape, s.dtype) for s in shards], *[pltpu.HBM(z.shape, z.dtype) for z in zones],
                   jax.ShapeDtypeStruct((8, LANES), F32)),
        in_specs=[HBM] * (2 * nT), out_specs=(SEM, SEM, *[HBM] * (2 * nT), pl.BlockSpec(memory_space=pltpu.VMEM)),
        input_output_aliases={i: 2 + i for i in range(2 * nT)},
        compiler_params=pltpu.CompilerParams(has_side_effects=EFFECT))(*[_in_hbm(a) for a in list(shards) + zones])
    return outs[0], outs[1], outs[2:2 + nT], outs[2 + nT:2 + 2 * nT], outs[-1]


def gather_wait(name, t, shard, zone, ssem, rsem, after, whole):
    after = after if isinstance(after, (list, tuple)) else [after]

    def body(src_ref, land_ref, ssem_ref, rsem_ref, *rest):
        x, y, c, chips = _place()
        rows = _rows_part(shard.shape, whole, c)
        for k, (px, py) in enumerate(chips):
            cp = _rcopy(src_ref.at[rows], land_ref.at[2 * px + py, rows], ssem_ref.at[3 * t + k], rsem_ref.at[3 * t + k],
                        (px, py, c))
            cp.wait_send()
            cp.wait_recv()

    return pl.pallas_call(
        body, name=name, out_shape=(pltpu.HBM(shard.shape, shard.dtype), pltpu.HBM(zone.shape, zone.dtype)),
        in_specs=(HBM, HBM, SEM, SEM, *[ANY] * len(after)), out_specs=(HBM, HBM), input_output_aliases={0: 0, 1: 1},
        compiler_params=pltpu.CompilerParams(has_side_effects=EFFECT))(shard, zone, ssem, rsem, *after)


def pair_swap(name, zone):
    hr = zone.shape[1] // 2

    def body(z_in, z_ref, ssem, rsem):
        x, y, c, chips = _place()
        cps = []
        for k, (px, py) in enumerate(chips):
            blk = z_ref.at[2 * px + py, pl.ds(c * hr, hr)]
            cps.append(_rcopy(blk, blk, ssem.at[k], rsem.at[k], (x, y, 1 - c)))
            cps[-1].start()
        for k, (px, py) in enumerate(chips):
            blk = z_ref.at[2 * px + py, pl.ds((1 - c) * hr, hr)]
            _rcopy(blk, blk, ssem.at[k], rsem.at[k], (x, y, 1 - c)).wait_recv()
        for cp in cps:
            cp.wait_send()

    return pl.pallas_call(
        body, name=name, in_specs=[ANY], out_specs=ANY, out_shape=jax.ShapeDtypeStruct(zone.shape, zone.dtype),
        input_output_aliases={0: 0},
        scratch_shapes=[pltpu.SemaphoreType.DMA((3,)), pltpu.SemaphoreType.DMA((3,))],
        compiler_params=_params())(zone)


N_SENDERS = 7


def _scatter_copies(g_ref, l_ref, ssem, rsem):
    x, y, c, chips = _place()
    cps = []
    for k, (px, py) in enumerate(chips):
        for d in range(2):
            to = (c + d) % 2
            cps.append(_rcopy(g_ref.at[2 * px + py, to], l_ref.at[2 * k + d], ssem.at[2 * k + d], rsem.at[2 * k + d],
                              (px, py, to)))
    cps.append(_rcopy(g_ref.at[2 * x + y, 1 - c], l_ref.at[6], ssem.at[6], rsem.at[6], (x, y, 1 - c)))
    return cps


def scatter_start(name, g):
    def body(g_ref, l_ref, ssem, rsem, g_out, l_out, token):
        for cp in _scatter_copies(g_ref, l_ref, ssem, rsem):
            cp.start()
        token[...] = jnp.zeros_like(token)

    zone = lax.empty((N_SENDERS,) + g.shape[2:], g.dtype)
    return pl.pallas_call(
        body, name=name,
        out_shape=(pltpu.SemaphoreType.DMA((N_SENDERS,)), pltpu.SemaphoreType.DMA((N_SENDERS,)),
                   pltpu.HBM(g.shape, g.dtype), pltpu.HBM(zone.shape, zone.dtype), jax.ShapeDtypeStruct((8, LANES), F32)),
        in_specs=[HBM, HBM], out_specs=(SEM, SEM, HBM, HBM, pl.BlockSpec(memory_space=pltpu.VMEM)),
        input_output_aliases={0: 2, 1: 3},
        compiler_params=pltpu.CompilerParams(has_side_effects=EFFECT))(_in_hbm(g), _in_hbm(zone))


def scatter_wait(name, g, zone, ssem, rsem, after):
    def body(g_ref, l_ref, ssem_ref, rsem_ref, after_ref, g_out, l_out):
        for cp in _scatter_copies(g_ref, l_ref, ssem_ref, rsem_ref):
            cp.wait_send()
            cp.wait_recv()

    return pl.pallas_call(
        body, name=name, out_shape=(pltpu.HBM(g.shape, g.dtype), pltpu.HBM(zone.shape, zone.dtype)),
        in_specs=(HBM, HBM, SEM, SEM, ANY), out_specs=(HBM, HBM), input_output_aliases={0: 0, 1: 1},
        compiler_params=pltpu.CompilerParams(has_side_effects=EFFECT))(g, zone, ssem, rsem, after)


def sum_parts(name, g, landed, chip_idx, c_idx):
    hr, C = g.shape[2:]
    tr = _row_tile(hr, C, min_rows=16)

    def body(me_ref, c_ref, g_ref, l_ref, o_ref):
        acc = g_ref[...].astype(F32)
        for s in range(N_SENDERS):
            acc = acc + l_ref[s].astype(F32)
        o_ref[...] = acc

    return pl.pallas_call(
        body, name=name,
        grid_spec=pltpu.PrefetchScalarGridSpec(
            num_scalar_prefetch=2, grid=(hr // tr,),
            in_specs=[pl.BlockSpec((None, None, tr, C), lambda i, me_ref, c_ref: (me_ref[0], c_ref[0], i, 0)),
                      pl.BlockSpec((N_SENDERS, tr, C), lambda i, me_ref, c_ref: (0, i, 0))],
            out_specs=pl.BlockSpec((tr, C), lambda i, me_ref, c_ref: (i, 0))),
        out_shape=jax.ShapeDtypeStruct((hr, C), F32),
        compiler_params=_params(('parallel',)))(chip_idx, c_idx, g, landed)


def pair_join(name, halves):
    nT = len(halves)

    def body(*refs):
        ins, outs = refs[:nT], refs[nT:2 * nT]
        ssem, rsem = refs[2 * nT:]
        x, y, c, _ = _place()
        cps = [_rcopy(ins[t], outs[t], ssem.at[t], rsem.at[t], (x, y, 1 - c)) for t in range(nT)]
        for cp in cps:
            cp.start()
        for cp in cps:
            cp.wait()

    return pl.pallas_call(
        body, name=name, in_specs=[ANY] * nT, out_specs=[ANY] * nT,
        out_shape=[jax.ShapeDtypeStruct(h.shape, h.dtype) for h in halves],
        scratch_shapes=[pltpu.SemaphoreType.DMA((nT,)), pltpu.SemaphoreType.DMA((nT,))],
        compiler_params=_params())(*halves)


N_DEVICES = 8


def _spread_copies(b_ref, l_ref, ssem, rsem):
    x, y, c, chips = _place()
    me = 4 * x + 2 * y + c
    pairs = []
    for px, py, pc in [(px, py, pc) for px, py in chips for pc in (c, 1 - c)] + [(x, y, 1 - c)]:
        it = 4 * px + 2 * py + pc
        pairs.append((_rcopy(b_ref, l_ref.at[me], ssem.at[it], rsem.at[me], (px, py, pc)),
                      _rcopy(b_ref, l_ref.at[it], ssem.at[it], rsem.at[it], (px, py, pc))))
    return pairs


def spread_start(name, buf):
    def body(b_ref, l_ref, ssem, rsem, b_out, l_out, token):
        for mine, _ in _spread_copies(b_ref, l_ref, ssem, rsem):
            mine.start()
        token[...] = jnp.zeros_like(token)

    zone = lax.empty((N_DEVICES,) + buf.shape, buf.dtype)
    return pl.pallas_call(
        body, name=name,
        out_shape=(pltpu.SemaphoreType.DMA((N_DEVICES,)), pltpu.SemaphoreType.DMA((N_DEVICES,)),
                   pltpu.HBM(buf.shape, buf.dtype), pltpu.HBM(zone.shape, zone.dtype), jax.ShapeDtypeStruct((8, LANES), F32)),
        in_specs=[HBM, HBM], out_specs=(SEM, SEM, HBM, HBM, pl.BlockSpec(memory_space=pltpu.VMEM)),
        input_output_aliases={0: 2, 1: 3},
        compiler_params=pltpu.CompilerParams(has_side_effects=EFFECT))(_in_hbm(buf), _in_hbm(zone))


def spread_wait(name, buf, zone, ssem, rsem, after):
    def body(b_ref, l_ref, ssem_ref, rsem_ref, after_ref, b_out, l_out):
        for mine, theirs in _spread_copies(b_ref, l_ref, ssem_ref, rsem_ref):
            mine.wait_send()
            theirs.wait_recv()

    return pl.pallas_call(
        body, name=name, out_shape=(pltpu.HBM(buf.shape, buf.dtype), pltpu.HBM(zone.shape, zone.dtype)),
        in_specs=(HBM, HBM, SEM, SEM, ANY), out_specs=(HBM, HBM), input_output_aliases={0: 0, 1: 1},
        compiler_params=pltpu.CompilerParams(has_side_effects=EFFECT))(buf, zone, ssem, rsem, after)


def sum_devices(name, zone):
    _, R, C = zone.shape
    tr = _row_tile(R, C)

    def body(z_ref, o_ref):
        acc = z_ref[0]
        for d in range(1, N_DEVICES):
            acc = acc + z_ref[d]
        o_ref[...] = acc

    return pl.pallas_call(
        body, name=name, grid=(R // tr,), in_specs=[pl.BlockSpec((N_DEVICES, tr, C), lambda i: (0, i, 0))],
        out_specs=pl.BlockSpec((tr, C), lambda i: (i, 0)), out_shape=jax.ShapeDtypeStruct((R, C), F32),
        compiler_params=_params(('parallel',)))(zone)


class _InWindows:
    def __init__(self, FW, LW, H, C):
        gap = LANES - H
        padded = lambda o: o if o < 3 * FW + H else o + gap
        self.width = 3 * FW + LANES + 2 * LW
        self.f_block = 3 * FW // LANES
        self.first = [padded(C * j) // LANES for j in range(N_CHIPS)]
        self.blocks = max(padded(C * (j + 1) - 1) // LANES - self.first[j] + 1 for j in range(N_CHIPS))
        assert all((b + self.blocks) * LANES <= self.width for b in self.first)
        self.cols = self.blocks * LANES
        self.runs = []
        for j in range(N_CHIPS):
            cut = min(max(3 * FW + H - C * j, 0), C)
            spans = [(0, cut), (cut, C)]
            self.runs.append([(t0, t1, padded(C * j + t0) - LANES * self.first[j]) for t0, t1 in spans if t1 > t0])

    def to_window(self, shard, chip):
        def place(j, s):
            parts, pos = [], 0
            for t0, t1, w0 in self.runs[j]:
                parts += [jnp.zeros((s.shape[0], w0 - pos), s.dtype), s[:, t0:t1]]
                pos = w0 + t1 - t0
            parts.append(jnp.zeros((s.shape[0], self.cols - pos), s.dtype))
            return jnp.concatenate([p for p in parts if p.shape[1]], axis=1)
        return lax.switch(chip, [functools.partial(place, j) for j in range(N_CHIPS)], shard)

    def from_window(self, win, chip):
        def take(j, w):
            return jnp.concatenate([w[:, w0:w0 + t1 - t0] for t0, t1, w0 in self.runs[j]], axis=1)
        return lax.switch(chip, [functools.partial(take, j) for j in range(N_CHIPS)], win)

    def _spans(self, j):
        b0, b1 = self.first[j], self.first[j] + self.blocks
        return (b0, min(b1, self.f_block)), b0 <= self.f_block < b1, (max(b0, self.f_block + 1), b1)

    def assemble(self, zone):
        main, f_blk = None, None
        for j in range(N_CHIPS):
            (a0, a1), has_f, (c0, c1) = self._spans(j)
            for p0, p1, shift in ((a0, a1, 0), (c0, c1, 1)):
                if p1 > p0:
                    part = zone[j][:, (p0 - self.first[j]) * LANES:(p1 - self.first[j]) * LANES]
                    part = jnp.pad(part, ((0, 0), ((p0 - shift) * LANES, self.width - LANES - (p1 - shift) * LANES)))
                    main = part if main is None else main + part
            if has_f:
                part = zone[j][:, (self.f_block - self.first[j]) * LANES:(self.f_block - self.first[j] + 1) * LANES]
                f_blk = part if f_blk is None else f_blk + part
        return main, f_blk

    def windows(self, main, f_blk):
        out = []
        for j in range(N_CHIPS):
            (a0, a1), has_f, (c0, c1) = self._spans(j)
            parts = [main[:, a0 * LANES:a1 * LANES]] if a1 > a0 else []
            parts += [f_blk] if has_f else []
            parts += [main[:, (c0 - 1) * LANES:(c1 - 1) * LANES]] if c1 > c0 else []
            out.append(jnp.concatenate(parts, axis=1))
        return jnp.stack(out)


_PACK = 8 * LANES


PACK_ROWS = 256


def _pack(arrs):
    flat = []
    for a in arrs:
        v = a.reshape(-1).astype(F32)
        flat.append(jnp.pad(v, (0, (-v.shape[0]) % _PACK)))
    rows = sum(v.shape[0] for v in flat) // LANES
    flat.append(jnp.zeros(((-rows) % PACK_ROWS) * LANES, F32))
    return jnp.concatenate(flat).reshape(-1, LANES)


def _unpack(buf, shapes):
    out, off = [], 0
    flat = buf.reshape(-1)
    for sh in shapes:
        n = math.prod(sh)
        out.append(flat[off:off + n].reshape(sh))
        off += n + (-n) % _PACK
    return out


def kernel(x, mem, g_mix, w_in, b_f, g_q, g_k, conv_w, conv_b, w_ra, b_ra, w_ri, b_ri, lam, g_fox_out, g_lru_out, w_out, g_xattn, g_mem, w_cq, w_ckv, g_cq, g_ck, w_co, g_ffn, w_gate_up, w_down, loss_target, m_g_mix, m_w_in, m_b_f, m_g_q, m_g_k, m_conv_w, m_conv_b, m_w_ra, m_b_ra, m_w_ri, m_b_ri, m_lam, m_g_fox_out, m_g_lru_out, m_w_out, m_g_xattn, m_g_mem, m_w_cq, m_w_ckv, m_g_cq, m_g_ck, m_w_co, m_g_ffn, m_w_gate_up, m_w_down, v_g_mix, v_w_in, v_b_f, v_g_q, v_g_k, v_conv_w, v_conv_b, v_w_ra, v_b_ra, v_w_ri, v_b_ri, v_lam, v_g_fox_out, v_g_lru_out, v_w_out, v_g_xattn, v_g_mem, v_w_cq, v_w_ckv, v_g_cq, v_g_ck, v_w_co, v_g_ffn, v_w_gate_up, v_w_down):
    given = dict(locals())
    W = {n: given[n][0] for n in WEIGHTS}
    M1 = {n: given['m_' + n][0] for n in WEIGHTS}
    V1 = {n: given['v_' + n][0] for n in WEIGHTS}
    xs, ms, tgt = x[0], mem[0], loss_target[0]
    S, D = xs.shape
    H = W['b_f'].shape[0]
    FW = H * HEAD_DIM
    LW = W['lam'].shape[0]
    nb = W['w_ra'].shape[0]
    XW = W['w_cq'].shape[1]
    F = W['w_down'].shape[0] * N_CHIPS
    IN_W = W['w_in'].shape[1] * N_CHIPS
    assert FW == LW and LW == nb * LANES and IN_W == 3 * FW + H + 2 * LW and H <= 8
    T = _tile(S, (512, 256, 128))
    c_idx = lax.axis_index('c').astype(jnp.int32).reshape(1)
    chip = 2 * lax.axis_index('x') + lax.axis_index('y')
    chip_idx = chip.astype(jnp.int32).reshape(1)
    vec = lambda n: W[n].reshape(1, -1)

    wins = _InWindows(FW, LW, H, W['w_in'].shape[1])
    started = {}
    g_tok = jnp.zeros((1, 1), F32)
    for call, names in (('gather_start_first', ['conv_w', 'w_in']), ('gather_start_rest', BIG[1:])):
        own = [W[n].reshape(-1, LANES) if n == 'conv_w' else W[n].astype(BF16) + g_tok.astype(BF16) for n in names]
        own = [wins.to_window(o, chip) if n == 'w_in' else o for n, o in zip(names, own)]
        ssem, rsem, srcs, zones, tok = gather_start(call, own, [n == 'conv_w' for n in names])
        g_tok = tok[0:1, 0:1]
        started.update({n: (t, srcs[t], zones[t], ssem, rsem) for t, n in enumerate(names)})

    def fetch(n, after):
        t, g_src, g_zone, g_ssem, g_rsem = started[n]
        src, zone = gather_wait('gather_wait_' + n, t, g_src, g_zone, g_ssem, g_rsem, after, n == 'conv_w')
        if n != 'conv_w':
            zone = pair_swap('pair_swap_' + n, zone)
        return lax.dynamic_update_index_in_dim(zone, src, chip, 0)

    b_f_pad = jnp.pad(vec('b_f'), ((0, 0), (0, LANES - H)))
    u_off, g_off = 3 * FW // LANES, (3 * FW + LW) // LANES

    h1 = norm_fwd('norm_mix', xs, vec('g_mix') + g_tok[0:1, 0:1])
    conv_full = fetch('conv_w', h1).reshape(N_CHIPS, CONV_W, LW // N_CHIPS).transpose(1, 0, 2).reshape(CONV_W, LW)
    w5, wf = wins.assemble(fetch('w_in', [h1, M1['w_in'], V1['w_in']]))
    proj = _mm('proj_in', h1, w5, 'nn', F32)
    f_raw = _mm('proj_f', h1, wf, 'nn', F32)
    qn, kn, vb = qkv_fwd(proj, vec('g_q'), vec('g_k'), FW)
    cc = fgate_fwd(f_raw, b_f_pad)
    ct = cc[:, :8].T
    o_fox, lse = fox_fwd(qn, kn, vb, cc, ct, T)
    lru_w = (conv_full, vec('conv_b'), W['w_ra'], vec('b_ra'), W['w_ri'], vec('b_ri'), vec('lam'))
    y_lru = lru_fwd(proj, *lru_w, u_off, g_off)
    mixn = mix_fwd(o_fox, y_lru, vec('g_fox_out'), vec('g_lru_out'))
    w_out_f = fetch('w_out', mixn).reshape(2 * FW, D)
    x1 = _mm('proj_out', mixn, w_out_f, 'nn', F32, res=xs)

    hq = norm_fwd('norm_xq', x1, vec('g_xattn'))
    mn = norm_fwd('norm_mem', ms, vec('g_mem'))
    w_cq_f = fetch('w_cq', hq).reshape(D, XW)
    w_ckv_f = fetch('w_ckv', hq).reshape(D, 2 * XW)
    cq_raw = _mm('proj_cq', hq, w_cq_f, 'nn', F32)
    ckv = _mm('proj_ckv', mn, w_ckv_f, 'nn', F32)
    o_x = xattn_fwd(cq_raw, ckv, vec('g_cq'), vec('g_ck'))
    w_co_g = fetch('w_co', o_x)
    x2 = _mm_colsharded('proj_co', o_x, w_co_g, F32, res=x1)

    hf = norm_fwd('norm_ffn', x2, vec('g_ffn'))
    w_gu_g = fetch('w_gate_up', hf)
    gu, act = gate_up_fwd(hf, w_gu_g, F)
    w_down_f = fetch('w_down', act).reshape(F, D)
    dy, dyb, loss_blk = down_fwd_loss(act, w_down_f, x2, tgt)

    gw, pending = {}, []

    def reduce_begin(n, g):
        sp = g.reshape(N_CHIPS, 2, g.shape[1] // 2, g.shape[2])
        ssem, rsem, sp, zone, tok = scatter_start('scatter_start_' + n, sp)
        pending.append((n, sp, zone, ssem, rsem))
        return tok[0:1, 0:1]

    t_down = reduce_begin('w_down', _mm('bwd_down_w', act, dyb, 'tn', BF16).reshape(N_CHIPS, F // N_CHIPS, D))
    dgu = down_bwd_x(dyb, w_down_f, gu, t_down)
    dhf = _mm_colsharded_t('bwd_gate_up_x', dgu, w_gu_g, F32)
    t_gu = reduce_begin('w_gate_up', _mm_grad_colsharded('bwd_gate_up_w', hf, dgu, N_CHIPS, BF16))
    dx2, dx2b, gw['g_ffn'] = norm_bwd('norm_ffn_bwd', x2, vec('g_ffn') + t_down + t_gu, dhf, res=dy)

    do_x = _mm_colsharded_t('bwd_co_x', dx2b, w_co_g, BF16)
    t_co = reduce_begin('w_co', _mm_grad_colsharded('bwd_co_w', o_x, dx2b, N_CHIPS, BF16))
    dcq_raw, dckv, gw['g_cq'], gw['g_ck'] = xattn_bwd(cq_raw, ckv, vec('g_cq') + t_co, vec('g_ck'), do_x)
    dhq = _mm('bwd_cq_x', dcq_raw, w_cq_f, 'nt', F32)
    t_cq = reduce_begin('w_cq', _mm('bwd_cq_w', hq, dcq_raw, 'tn', BF16).reshape(N_CHIPS, D // N_CHIPS, XW))
    dmn = _mm('bwd_ckv_x', dckv, w_ckv_f, 'nt', F32)
    t_ckv = reduce_begin('w_ckv', _mm('bwd_ckv_w', mn, dckv, 'tn', BF16).reshape(N_CHIPS, D // N_CHIPS, 2 * XW))
    (gw['g_mem'],) = norm_bwd('norm_mem_bwd', ms, vec('g_mem'), dmn, want_dx=False)
    dx1, dx1b, gw['g_xattn'] = norm_bwd('norm_xq_bwd', x1, vec('g_xattn') + t_cq + t_ckv, dhq, res=dx2)

    dmix = _mm('bwd_out_x', dx1b, w_out_f, 'nt', F32)
    t_out = reduce_begin('w_out', _mm('bwd_out_w', mixn, dx1b, 'tn', BF16).reshape(N_CHIPS, 2 * FW // N_CHIPS, D))
    do_fox, delta, dy_lru, gw['g_fox_out'], gw['g_lru_out'] = mix_bwd(o_fox, y_lru, vec('g_fox_out') + t_out,
                                                                     vec('g_lru_out'), dmix)
    (du, dgate, gw['conv_w'], gw['conv_b'], gw['w_ra'], gw['b_ra'], gw['w_ri'], gw['b_ri'],
     gw['lam']) = lru_bwd(proj, dy_lru, *lru_w, u_off, g_off)
    early = [n for n in SMALL if n not in ('g_q', 'g_k', 'b_f', 'g_mix')]
    late = [n for n in SMALL if n not in early]
    e_ssem, e_rsem, e_buf, e_zone, e_tok = spread_start('spread_start_early', _pack([gw[n] for n in early]))
    dqn, delta2 = fox_bwd_q(qn, kn, vb, do_fox, cc, ct, lse, delta, T)
    dkn, dv, dct = fox_bwd_kv(qn, kn, vb, do_fox, cc, ct, lse, delta2, T)
    dq, dk, gw['g_q'], gw['g_k'] = qkv_bwd(proj, vec('g_q') + e_tok[0:1, 0:1], vec('g_k'), dqn, dkn, FW)
    dc = jnp.pad(dct.reshape(H, S).T, ((0, 0), (0, LANES - H)))
    df, db_f = fgate_bwd(f_raw, b_f_pad, dc, H)
    gw['b_f'] = db_f[:, :H]
    dproj = jnp.concatenate([dq, dk, dv, du, dgate], axis=1)
    dw5 = _mm('bwd_in_w', h1, dproj, 'tn', BF16)
    dwf = _mm('bwd_f_w', h1, df, 'tn', BF16)
    t_in = reduce_begin('w_in', wins.windows(dw5, dwf))
    dh_a = _mm('bwd_f_x', df, wf, 'nt', F32)
    dh1 = _mm('bwd_in_x', dproj, w5, 'nt', F32, res=dh_a)
    grad_x, _, gw['g_mix'] = norm_bwd('norm_mix_bwd', xs, vec('g_mix') + t_in, dh1, res=dx1)
    l_ssem, l_rsem, l_buf, l_zone, _ = spread_start('spread_start_late',
                                                    _pack([gw[n] for n in late] + [loss_blk[0:1, 0:1]]))

    grads, delta_w, new_m, new_v = {}, {}, {}, {}
    done = grad_x
    for n, part, zone, ssem, rsem in pending:
        part, landed = scatter_wait('scatter_wait_' + n, part, zone, ssem, rsem, done)
        mine = sum_parts('sum_parts_' + n, part, landed, chip_idx, c_idx)
        (other,) = pair_join('pair_join_' + n, [mine])
        if n == 'w_in':
            mine, other = wins.from_window(mine, chip), wins.from_window(other, chip)
        grads[n], delta_w[n], new_m[n], new_v[n] = adamw_halves('adamw_' + n, W[n], mine, other, M1[n], V1[n], c_idx)
        done = delta_w[n]

    device = 4 * lax.axis_index('x') + 2 * lax.axis_index('y') + lax.axis_index('c')
    summed = {}
    for tag, names, buf, zone, ssem, rsem in (('early', early, e_buf, e_zone, e_ssem, e_rsem),
                                              ('late', late + ['loss'], l_buf, l_zone, l_ssem, l_rsem)):
        buf, zone = spread_wait('spread_wait_' + tag, buf, zone, ssem, rsem, done)
        total = sum_devices('sum_small_' + tag, lax.dynamic_update_index_in_dim(zone, buf, device, 0))
        summed.update(zip(names, _unpack(total, [gw[n].shape if n != 'loss' else (1, 1) for n in names])))
    loss = summed['loss'].reshape(())
    for n in SMALL:
        g = summed[n]
        grads[n] = g.reshape(W[n].shape) if n != 'conv_w' else lax.dynamic_slice_in_dim(
            g, chip * (LW // N_CHIPS), LW // N_CHIPS, axis=1)
    packs = [_pack([d[n] for n in SMALL]) for d in (W, grads, M1, V1)]
    shapes = [W[n].shape for n in SMALL]
    for d, res in zip((delta_w, new_m, new_v), adamw('adamw_small', *packs)):
        d.update(zip(SMALL, _unpack(res, shapes)))

    lead = lambda d: [d[n][None] for n in WEIGHTS]
    return (loss, grad_x[None], *lead(grads), *lead(delta_w), *lead(new_m), *lead(new_v))
```

```python
import functools
import math

import jax
import jax.numpy as jnp
from jax import lax
from jax.experimental import pallas as pl
from jax.experimental.pallas import tpu as pltpu

F32 = jnp.float32
BF16 = jnp.bfloat16
HEAD_DIM = 128
LANES = 128
LRU_C = 8.0
RMS_EPS = 1e-6
CONV_W = 4
ADAM_LR = 0.001
ADAM_B1 = 0.9
ADAM_B2 = 0.999
ADAM_EPS = 1e-08
ADAM_WD = 0.01
ADAM_STEP = 10
VMEM_LIMIT = 56 * 1024 * 1024
N_CHIPS = 4
MESH = pl.DeviceIdType.MESH
ANY = pl.BlockSpec(memory_space=pl.ANY)

WEIGHTS = ['g_mix', 'w_in', 'b_f', 'g_q', 'g_k', 'conv_w', 'conv_b', 'w_ra', 'b_ra', 'w_ri', 'b_ri', 'lam',
           'g_fox_out', 'g_lru_out', 'w_out', 'g_xattn', 'g_mem', 'w_cq', 'w_ckv', 'g_cq', 'g_ck', 'w_co', 'g_ffn',
           'w_gate_up', 'w_down']
BIG = ['w_in', 'w_out', 'w_cq', 'w_ckv', 'w_co', 'w_gate_up', 'w_down']
SMALL = [n for n in WEIGHTS if n not in BIG]


def _params(sem=None):
    if sem is None:
        return pltpu.CompilerParams(vmem_limit_bytes=VMEM_LIMIT)
    return pltpu.CompilerParams(dimension_semantics=sem, vmem_limit_bytes=VMEM_LIMIT)


def _tile(n, cands):
    for t in cands:
        if n % t == 0:
            return t
    return n


ROW_BLOCK_BYTES = 1 << 20


def _row_tile(n_rows, n_cols, min_rows=8):
    cands = [t for t in (512, 256, 128, 64, 32, 16, 8) if t >= min_rows and t * n_cols * 4 <= ROW_BLOCK_BYTES]
    return _tile(n_rows, cands or [min_rows])


def _sigmoid(z):
    return 1.0 / (1.0 + jnp.exp(-z))


def _softplus(z):
    return jnp.maximum(z, 0.0) + jnp.log(1.0 + jnp.exp(-jnp.abs(z)))


def _neg_expm1(z):
    series = -z * (1.0 + z * (0.5 + z * (1.0 / 6.0 + z * (1.0 / 24.0 + z * (1.0 / 120.0)))))
    return jnp.where(z > -0.25, series, 1.0 - jnp.exp(z))


_GELU_K = math.sqrt(2.0 / math.pi)


def _gelu_and_grad(z):
    inner = _GELU_K * (z + 0.044715 * z * z * z)
    t = jnp.tanh(inner)
    g = 0.5 * z * (1.0 + t)
    dg = 0.5 * (1.0 + t) + 0.5 * z * (1.0 - t * t) * _GELU_K * (1.0 + 3.0 * 0.044715 * z * z)
    return g, dg


def _rms(xv, g):
    r = lax.rsqrt(jnp.mean(xv * xv, axis=-1, keepdims=True) + RMS_EPS)
    return xv * r * g


def _rms_bwd(xv, g, dy):
    r = lax.rsqrt(jnp.mean(xv * xv, axis=-1, keepdims=True) + RMS_EPS)
    xh = xv * r
    dyg = dy * g
    dx = r * (dyg - xh * jnp.mean(dyg * xh, axis=-1, keepdims=True))
    return dx, jnp.sum(dy * xh, axis=0, keepdims=True)


def _heads(fn, n_heads, *arrs):
    outs = [fn(*[a[:, h * HEAD_DIM:(h + 1) * HEAD_DIM] for a in arrs]) for h in range(n_heads)]
    first = jnp.concatenate([o[0] for o in outs], axis=1) if n_heads > 1 else outs[0][0]
    rest = [functools.reduce(lambda p, q: p + q, [o[i] for o in outs]) for i in range(1, len(outs[0]))]
    return (first, *rest)


def _split3(v):
    hi = v.astype(BF16)
    r1 = v - hi.astype(F32)
    mid = r1.astype(BF16)
    lo = (r1 - mid.astype(F32)).astype(BF16)
    return hi, mid, lo


def _acc_out(ref, first, val):
    @pl.when(first)
    def _():
        ref[...] = val

    @pl.when(jnp.logical_not(first))
    def _():
        ref[...] += val


_DIMS = {'nn': (((1,), (0,)), ((), ())), 'nt': (((1,), (1,)), ((), ())), 'tn': (((0,), (0,)), ((), ()))}


MM_VMEM_BYTES = 36 * 1024 * 1024


MXU_FLOPS = 800e12
HBM_BYTES_S = 3.2e12
VMEM_ADD_BYTES_S = 8e12
STEP_S = 0.35e-6


def _k_tile(K, tm, tn, a, b, o_dtype, res):
    fixed = tm * tn * (2 * jnp.dtype(o_dtype).itemsize + 4 + (8 if res is not None else 0))
    per_k = 2 * (tm * a.dtype.itemsize + tn * b.dtype.itemsize)
    per_k += 2 * tm * (a.dtype.itemsize > 2) + 2 * tn * (b.dtype.itemsize > 2)
    units = K // LANES
    for d in sorted((d for d in range(1, units + 1) if units % d == 0), reverse=True):
        if fixed + d * LANES * per_k <= MM_VMEM_BYTES:
            return d * LANES
    return None


def _mm_tiles(M, N, K, k_span, a, b, o_dtype, res, tn_cands=(2048, 1024, 512, 256, 128)):
    best = None
    for tm in (2048, 1024, 512, 256, 128):
        for tn in tn_cands:
            if M % tm or N % tn:
                continue
            tk = _k_tile(k_span, tm, tn, a, b, o_dtype, res)
            if tk is None:
                continue
            nk = K // tk
            traffic = (M * K * a.dtype.itemsize * (N // tn) + K * N * b.dtype.itemsize * (M // tm)
                       + M * N * (jnp.dtype(o_dtype).itemsize + (4 if res is not None else 0)))
            work = 2.0 * M * N * K / MXU_FLOPS + (M * N * 4 * nk / VMEM_ADD_BYTES_S if nk > 1 else 0.0)
            t = max(work, traffic / HBM_BYTES_S) + (M // tm) * (N // tn) * nk * STEP_S
            if best is None or t < best[0]:
                best = (t, tm, tn, tk)
    assert best is not None, (M, N, K)
    return best[1:]


def _mm_call(name, a, b, mode, grid, a_spec, b_spec, o_spec, o_shape, o_dtype, acc_shape, res=None):
    nk = grid[2]
    dn = _DIMS[mode]

    def body(*refs):
        a_ref, b_ref = refs[:2]
        r_ref = refs[2] if res is not None else None
        o_ref = refs[3] if res is not None else refs[2]
        part = lax.dot_general(a_ref[...].astype(BF16), b_ref[...].astype(BF16), dn, preferred_element_type=F32)

        def finish(r):
            if r_ref is not None:
                r = r + r_ref[...]
            o_ref[...] = r.astype(o_dtype)

        if nk == 1:
            finish(part)
            return
        acc = refs[-1]
        k = pl.program_id(2)

        @pl.when(k == 0)
        def _():
            acc[...] = part

        @pl.when(k > 0)
        def _():
            acc[...] += part

        @pl.when(k == nk - 1)
        def _():
            finish(acc[...])

    ins = [a, b] + ([] if res is None else [res])
    specs = [a_spec, b_spec] + ([] if res is None else [o_spec])
    return pl.pallas_call(
        body, name=name, grid=grid, in_specs=specs, out_specs=o_spec,
        out_shape=jax.ShapeDtypeStruct(o_shape, o_dtype),
        scratch_shapes=[] if nk == 1 else [pltpu.VMEM(acc_shape, F32)],
        compiler_params=_params(('parallel', 'parallel', 'arbitrary')))(*ins)


def _mm(name, a, b, mode, o_dtype, res=None):
    if mode == 'tn':
        K, M = a.shape
    else:
        M, K = a.shape
    N = b.shape[0] if mode == 'nt' else b.shape[1]
    tm, tn, tk = _mm_tiles(M, N, K, K, a, b, o_dtype, res)
    a_spec = (pl.BlockSpec((tk, tm), lambda m, n, k: (k, m)) if mode == 'tn'
              else pl.BlockSpec((tm, tk), lambda m, n, k: (m, k)))
    b_spec = (pl.BlockSpec((tn, tk), lambda m, n, k: (n, k)) if mode == 'nt'
              else pl.BlockSpec((tk, tn), lambda m, n, k: (k, n)))
    o_spec = pl.BlockSpec((tm, tn), lambda m, n, k: (m, n))
    return _mm_call(name, a, b, mode, (M // tm, N // tn, K // tk), a_spec, b_spec, o_spec, (M, N), o_dtype,
                    (tm, tn), res)


def _mm_colsharded(name, a, w, o_dtype, res=None):
    M, K = a.shape
    J, _, Nj = w.shape
    tm, tn, tk = _mm_tiles(M, J * Nj, K, K, a, w, o_dtype, res,
                           tn_cands=[t for t in (2816, 1408, 1024, 512, 256, 128) if Nj % t == 0])
    per = Nj // tn
    return _mm_call(name, a, w, 'nn', (M // tm, J * per, K // tk),
                    pl.BlockSpec((tm, tk), lambda m, n, k: (m, k)),
                    pl.BlockSpec((None, tk, tn), lambda m, n, k: (n // per, k, n % per)),
                    pl.BlockSpec((tm, tn), lambda m, n, k: (m, n)), (M, J * Nj), o_dtype, (tm, tn), res)


def _planes_spec(arr, rows, cols, row_of, col_of):
    if arr.ndim == 2:
        return pl.BlockSpec((rows, cols), lambda m, n, k: (row_of(m, n, k), col_of(m, n, k)))
    per_plane = arr.shape[2] // cols
    return pl.BlockSpec((None, rows, cols),
                        lambda m, n, k: (col_of(m, n, k) // per_plane, row_of(m, n, k), col_of(m, n, k) % per_plane))


def _mm_colsharded_t(name, a, w, o_dtype):
    M = a.shape[-2]
    J, K, Nj = w.shape
    tm, tn, tk = _mm_tiles(M, K, J * Nj, Nj, a, w, o_dtype, None)
    per = Nj // tk
    return _mm_call(name, a, w, 'nt', (M // tm, K // tn, J * per),
                    _planes_spec(a, tm, tk, lambda m, n, k: m, lambda m, n, k: k),
                    pl.BlockSpec((None, tn, tk), lambda m, n, k: (k // per, n, k % per)),
                    pl.BlockSpec((tm, tn), lambda m, n, k: (m, n)), (M, K), o_dtype, (tm, tn))


def _mm_grad_colsharded(name, a, dy, J, o_dtype):
    S, M = a.shape
    Nj = dy.shape[-1] * (dy.shape[0] if dy.ndim == 3 else 1) // J
    tm, tn, tk = _mm_tiles(M, J * Nj, S, S, a, dy, o_dtype, None,
                           tn_cands=[t for t in (2816, 1408, 1024, 512, 256, 128) if Nj % t == 0])
    per = Nj // tn
    return _mm_call(name, a, dy, 'tn', (M // tm, J * per, S // tk),
                    pl.BlockSpec((tk, tm), lambda m, n, k: (k, m)),
                    _planes_spec(dy, tk, tn, lambda m, n, k: k, lambda m, n, k: n),
                    pl.BlockSpec((None, tm, tn), lambda m, n, k: (n // per, m, n % per)), (J, M, Nj), o_dtype, (tm, tn))


def _rows_call(name, body, n_rows, tr, ins, outs):
    return pl.pallas_call(
        body, name=name, grid=(n_rows // tr,), in_specs=[s for _, s in ins], out_specs=[s for _, _, s in outs],
        out_shape=[jax.ShapeDtypeStruct(sh, dt) for sh, dt, _ in outs],
        compiler_params=_params(('arbitrary',)))(*[a for a, _ in ins])


def _rb(tr, w, cb=0):
    return pl.BlockSpec((tr, w), lambda i: (i, cb))


def _fb(shape):
    nd = len(shape)
    return pl.BlockSpec(shape, lambda i: (0,) * nd)


def norm_fwd(name, xv, g):
    S, D = xv.shape
    tr = _tile(S, (256, 128))

    def body(x_ref, g_ref, o_ref):
        o_ref[...] = _rms(x_ref[...], g_ref[...]).astype(BF16)

    return _rows_call(name, body, S, tr, [(xv, _rb(tr, D)), (g, _fb((1, D)))], [((S, D), BF16, _rb(tr, D))])[0]


def norm_bwd(name, xv, g, dy, res=None, want_dx=True):
    S, D = xv.shape
    tr = _tile(S, (256, 128))

    def body(*refs):
        if res is None:
            x_ref, g_ref, dy_ref = refs[:3]
            outs = refs[3:]
            r_ref = None
        else:
            x_ref, g_ref, dy_ref, r_ref = refs[:4]
            outs = refs[4:]
        dx, dg = _rms_bwd(x_ref[...], g_ref[...], dy_ref[...])
        if r_ref is not None:
            dx = dx + r_ref[...]
        if want_dx:
            outs[0][...] = dx
            outs[1][...] = dx.astype(BF16)
        _acc_out(outs[-1], pl.program_id(0) == 0, dg)

    ins = [(xv, _rb(tr, D)), (g, _fb((1, D))), (dy, _rb(tr, D))] + ([] if res is None else [(res, _rb(tr, D))])
    outs = ([((S, D), F32, _rb(tr, D)), ((S, D), BF16, _rb(tr, D))] if want_dx else []) + [((1, D), F32, _fb((1, D)))]
    return _rows_call(name, body, S, tr, ins, outs)


def qkv_fwd(proj, g_q, g_k, FW):
    S = proj.shape[0]
    H = FW // HEAD_DIM
    tr = _tile(S, (256, 128))

    def body(q_ref, k_ref, v_ref, gq_ref, gk_ref, qo, ko, vo):
        qo[...] = _heads(lambda t: (_rms(t, gq_ref[...]),), H, q_ref[...])[0].astype(BF16)
        ko[...] = _heads(lambda t: (_rms(t, gk_ref[...]),), H, k_ref[...])[0].astype(BF16)
        vo[...] = v_ref[...].astype(BF16)

    o = ((S, FW), BF16, _rb(tr, FW))
    return _rows_call('qkv_fwd', body, S, tr,
                      [(proj, _rb(tr, FW, 0)), (proj, _rb(tr, FW, 1)), (proj, _rb(tr, FW, 2)),
                       (g_q, _fb((1, HEAD_DIM))), (g_k, _fb((1, HEAD_DIM)))], [o, o, o])


def qkv_bwd(proj, g_q, g_k, dqn, dkn, FW):
    S = proj.shape[0]
    H = FW // HEAD_DIM
    tr = _tile(S, (256, 128))

    def body(q_ref, k_ref, gq_ref, gk_ref, dq_ref, dk_ref, dqo, dko, dgq, dgk):
        dq, gq = _heads(lambda t, d: _rms_bwd(t, gq_ref[...], d), H, q_ref[...], dq_ref[...])
        dk, gk = _heads(lambda t, d: _rms_bwd(t, gk_ref[...], d), H, k_ref[...], dk_ref[...])
        dqo[...] = dq.astype(BF16)
        dko[...] = dk.astype(BF16)
        first = pl.program_id(0) == 0
        _acc_out(dgq, first, gq)
        _acc_out(dgk, first, gk)

    o = ((S, FW), BF16, _rb(tr, FW))
    og = ((1, HEAD_DIM), F32, _fb((1, HEAD_DIM)))
    return _rows_call('qkv_bwd', body, S, tr,
                      [(proj, _rb(tr, FW, 0)), (proj, _rb(tr, FW, 1)), (g_q, _fb((1, HEAD_DIM))),
                       (g_k, _fb((1, HEAD_DIM))), (dqn, _rb(tr, FW)), (dkn, _rb(tr, FW))], [o, o, og, og])


def _tri(n, upper):
    r = lax.broadcasted_iota(jnp.int32, (n, n), 0)
    c = lax.broadcasted_iota(jnp.int32, (n, n), 1)
    return jnp.where((c >= r) if upper else (c <= r), 1.0, 0.0).astype(BF16)


def _blocked_cumsum(val, S, blk, reverse):
    tri = _tri(blk, reverse)
    order = range(S // blk - 1, -1, -1) if reverse else range(S // blk)
    carry = jnp.zeros((1, LANES), F32)
    outs = {}
    for bi in order:
        part = val[bi * blk:(bi + 1) * blk]
        acc = carry
        for piece in _split3(part):
            acc = acc + jnp.dot(tri, piece, preferred_element_type=F32)
        outs[bi] = acc
        carry = carry + jnp.sum(part, axis=0, keepdims=True)
    return jnp.concatenate([outs[bi] for bi in range(S // blk)], axis=0)


def fgate_fwd(f_raw, b_f_pad):
    S = f_raw.shape[0]
    blk = _tile(S, (256, 128))

    def body(f_ref, b_ref, c_ref):
        z = f_ref[...] + b_ref[...]
        c_ref[...] = _blocked_cumsum(-_softplus(-z), S, blk, False)

    return pl.pallas_call(body, name='fgate_fwd', grid=(1,), in_specs=[_fb((S, LANES)), _fb((1, LANES))],
                          out_specs=_fb((S, LANES)), out_shape=jax.ShapeDtypeStruct((S, LANES), F32),
                          compiler_params=_params(('arbitrary',)))(f_raw, b_f_pad)


def fgate_bwd(f_raw, b_f_pad, dc, H):
    S = f_raw.shape[0]
    blk = _tile(S, (256, 128))

    def body(f_ref, b_ref, dc_ref, df_ref, db_ref):
        z = f_ref[...] + b_ref[...]
        dlogf = _blocked_cumsum(dc_ref[...], S, blk, True)
        lane = lax.broadcasted_iota(jnp.int32, (S, LANES), 1)
        df = jnp.where(lane < H, dlogf * _sigmoid(-z), 0.0)
        df_ref[...] = df.astype(BF16)
        db_ref[...] = jnp.sum(df, axis=0, keepdims=True)

    return pl.pallas_call(body, name='fgate_bwd', grid=(1,),
                          in_specs=[_fb((S, LANES)), _fb((1, LANES)), _fb((S, LANES))],
                          out_specs=[_fb((S, LANES)), _fb((1, LANES))],
                          out_shape=[jax.ShapeDtypeStruct((S, LANES), BF16), jax.ShapeDtypeStruct((1, LANES), F32)],
                          compiler_params=_params(('arbitrary',)))(f_raw, b_f_pad, dc)


def _fox_logits(q, k, c_blk, ct_blk, h, T, diagonal):
    s = lax.dot_general(q, k, _DIMS['nt'], preferred_element_type=F32) * (1.0 / math.sqrt(HEAD_DIM))
    lane = lax.broadcasted_iota(jnp.int32, c_blk.shape, 1)
    cq = jnp.sum(jnp.where(lane == h, c_blk, 0.0), axis=1, keepdims=True)
    sub = lax.broadcasted_iota(jnp.int32, ct_blk.shape, 0)
    ck = jnp.sum(jnp.where(sub == h, ct_blk, 0.0), axis=0, keepdims=True)
    s = s + cq - ck
    if not diagonal:
        return s
    rows = lax.broadcasted_iota(jnp.int32, (T, T), 0)
    cols = lax.broadcasted_iota(jnp.int32, (T, T), 1)
    return jnp.where(cols <= rows, s, -jnp.inf)


def _below_and_on_diagonal(q_blk, k_blk, step):
    @pl.when(k_blk < q_blk)
    def _():
        step(False)

    @pl.when(k_blk == q_blk)
    def _():
        step(True)


def fox_fwd(qn, kn, vb, c, ct, T):
    S, FW = qn.shape
    H = FW // HEAD_DIM
    Hp = ct.shape[0]
    n = S // T

    HB = _tile(H, (8, 4, 2, 1))
    W2 = HB * HEAD_DIM

    def body(q_ref, k_ref, v_ref, c_ref, ct_ref, o_ref, lse_ref, m_s, l_s, acc_s):
        hb, i, j = pl.program_id(0), pl.program_id(1), pl.program_id(2)

        @pl.when(j == 0)
        def _():
            m_s[...] = jnp.full_like(m_s, -jnp.inf)
            l_s[...] = jnp.zeros_like(l_s)
            acc_s[...] = jnp.zeros_like(acc_s)

        def step(diagonal):
            for hh in range(HB):
                sl = slice(hh * HEAD_DIM, (hh + 1) * HEAD_DIM)
                s = _fox_logits(q_ref[:, sl], k_ref[:, sl], c_ref[...], ct_ref[...], hb * HB + hh, T, diagonal)
                m_old = m_s[hh]
                m_new = jnp.maximum(m_old, jnp.max(s, axis=1, keepdims=True))
                alpha = jnp.exp(m_old - m_new)
                p = jnp.exp(s - m_new)
                l_s[hh] = alpha * l_s[hh] + jnp.sum(p, axis=1, keepdims=True)
                acc_s[hh] = alpha * acc_s[hh] + jnp.dot(p.astype(BF16), v_ref[:, sl], preferred_element_type=F32)
                m_s[hh] = m_new

        _below_and_on_diagonal(i, j, step)

        @pl.when(j == i)
        def _():
            for hh in range(HB):
                o_ref[:, hh * HEAD_DIM:(hh + 1) * HEAD_DIM] = acc_s[hh] / l_s[hh]
                lse_ref[hh] = jnp.broadcast_to(m_s[hh] + jnp.log(l_s[hh]), (T, LANES))

    qs = pl.BlockSpec((T, W2), lambda h, i, j: (i, h))
    ks = pl.BlockSpec((T, W2), lambda h, i, j: (jnp.minimum(j, i), h))
    return pl.pallas_call(
        body, name='fox_fwd', grid=(H // HB, n, n),
        in_specs=[qs, ks, ks, pl.BlockSpec((T, LANES), lambda h, i, j: (i, 0)),
                  pl.BlockSpec((Hp, T), lambda h, i, j: (0, jnp.minimum(j, i)))],
        out_specs=[qs, pl.BlockSpec((HB, T, LANES), lambda h, i, j: (h, i, 0))],
        out_shape=[jax.ShapeDtypeStruct((S, FW), F32), jax.ShapeDtypeStruct((H, S, LANES), F32)],
        scratch_shapes=[pltpu.VMEM((HB, T, 1), F32), pltpu.VMEM((HB, T, 1), F32), pltpu.VMEM((HB, T, HEAD_DIM), F32)],
        compiler_params=_params(('parallel', 'parallel', 'arbitrary')))(qn, kn, vb, c, ct)


def _fox_p_ds(q_ref, k_ref, v_ref, do_ref, c_ref, ct_ref, lse_ref, dl_ref, h, T, diagonal):
    s = _fox_logits(q_ref[...], k_ref[...], c_ref[...], ct_ref[...], h, T, diagonal)
    p = jnp.exp(s - jnp.tile(lse_ref[...], (1, T // LANES)))
    dp = lax.dot_general(do_ref[...], v_ref[...], _DIMS['nt'], preferred_element_type=F32)
    ds = p * (dp - jnp.tile(dl_ref[...], (1, T // LANES)))
    return p, dp, ds


def fox_bwd_q(qn, kn, vb, do, c, ct, lse, dl, T):
    S, FW = qn.shape
    H = FW // HEAD_DIM
    Hp = ct.shape[0]
    n = S // T
    HB = _tile(H, (8, 4, 2, 1))
    W2 = HB * HEAD_DIM

    def body(q_ref, k_ref, v_ref, do_ref, c_ref, ct_ref, lse_ref, dl_ref, dq_ref, dl2_ref, acc_s, rs_s):
        hb, i, j = pl.program_id(0), pl.program_id(1), pl.program_id(2)

        @pl.when(j == 0)
        def _():
            acc_s[...] = jnp.zeros_like(acc_s)
            rs_s[...] = jnp.zeros_like(rs_s)

        def step(diagonal):
            for hh in range(HB):
                sl = slice(hh * HEAD_DIM, (hh + 1) * HEAD_DIM)
                p, dp, ds = _fox_p_ds(q_ref.at[:, sl], k_ref.at[:, sl], v_ref.at[:, sl], do_ref.at[:, sl], c_ref, ct_ref,
                                      lse_ref.at[hh], dl_ref.at[hh], hb * HB + hh, T, diagonal)
                acc_s[hh] += jnp.dot(ds.astype(BF16), k_ref[:, sl], preferred_element_type=F32)
                rs_s[hh] += jnp.sum(p * dp, axis=1, keepdims=True)

        _below_and_on_diagonal(i, j, step)

        @pl.when(j == i)
        def _():
            for hh in range(HB):
                dq_ref[:, hh * HEAD_DIM:(hh + 1) * HEAD_DIM] = acc_s[hh] * (1.0 / math.sqrt(HEAD_DIM))
                dl2_ref[hh] = jnp.broadcast_to(rs_s[hh], (T, LANES))

    qs = pl.BlockSpec((T, W2), lambda h, i, j: (i, h))
    ks = pl.BlockSpec((T, W2), lambda h, i, j: (jnp.minimum(j, i), h))
    st = pl.BlockSpec((HB, T, LANES), lambda h, i, j: (h, i, 0))
    return pl.pallas_call(
        body, name='fox_bwd_q', grid=(H // HB, n, n),
        in_specs=[qs, ks, ks, qs, pl.BlockSpec((T, LANES), lambda h, i, j: (i, 0)),
                  pl.BlockSpec((Hp, T), lambda h, i, j: (0, jnp.minimum(j, i))), st, st],
        out_specs=[qs, st], out_shape=[jax.ShapeDtypeStruct((S, FW), F32), jax.ShapeDtypeStruct((H, S, LANES), F32)],
        scratch_shapes=[pltpu.VMEM((HB, T, HEAD_DIM), F32), pltpu.VMEM((HB, T, 1), F32)],
        compiler_params=_params(('parallel', 'parallel', 'arbitrary')))(qn, kn, vb, do, c, ct, lse, dl)


def fox_bwd_kv(qn, kn, vb, do, c, ct, lse, dl, T):
    S, FW = qn.shape
    H = FW // HEAD_DIM
    Hp = ct.shape[0]
    n = S // T

    HB = _tile(H, (8, 4, 2, 1))
    W2 = HB * HEAD_DIM

    def body(q_ref, k_ref, v_ref, do_ref, c_ref, ct_ref, lse_ref, dl_ref, dk_ref, dv_ref, dc_ref, dk_s, dv_s, dc_s):
        hb, j, i = pl.program_id(0), pl.program_id(1), pl.program_id(2)

        @pl.when(i == 0)
        def _():
            dk_s[...] = jnp.zeros_like(dk_s)
            dv_s[...] = jnp.zeros_like(dv_s)
            dc_s[...] = jnp.zeros_like(dc_s)

        def step(diagonal):
            for hh in range(HB):
                sl = slice(hh * HEAD_DIM, (hh + 1) * HEAD_DIM)
                p, _, ds = _fox_p_ds(q_ref.at[:, sl], k_ref.at[:, sl], v_ref.at[:, sl], do_ref.at[:, sl], c_ref, ct_ref,
                                     lse_ref.at[hh], dl_ref.at[hh], hb * HB + hh, T, diagonal)
                dv_s[hh] += lax.dot_general(p.astype(BF16), do_ref[:, sl], _DIMS['tn'], preferred_element_type=F32)
                dk_s[hh] += lax.dot_general(ds.astype(BF16), q_ref[:, sl], _DIMS['tn'], preferred_element_type=F32)
                dc_s[hh] += jnp.sum(ds, axis=0, keepdims=True)

        _below_and_on_diagonal(i, j, step)

        @pl.when(i == n - 1)
        def _():
            for hh in range(HB):
                sl = slice(hh * HEAD_DIM, (hh + 1) * HEAD_DIM)
                dk_ref[:, sl] = dk_s[hh] * (1.0 / math.sqrt(HEAD_DIM))
                dv_ref[:, sl] = dv_s[hh].astype(BF16)
                dc_ref[hh] = -dc_s[hh]

    qs = pl.BlockSpec((T, W2), lambda h, j, i: (jnp.maximum(i, j), h))
    ks = pl.BlockSpec((T, W2), lambda h, j, i: (j, h))
    st = pl.BlockSpec((HB, T, LANES), lambda h, j, i: (h, jnp.maximum(i, j), 0))
    return pl.pallas_call(
        body, name='fox_bwd_kv', grid=(H // HB, n, n),
        in_specs=[qs, ks, ks, qs, pl.BlockSpec((T, LANES), lambda h, j, i: (jnp.maximum(i, j), 0)),
                  pl.BlockSpec((Hp, T), lambda h, j, i: (0, j)), st, st],
        out_specs=[ks, ks, pl.BlockSpec((HB, 1, T), lambda h, j, i: (h, 0, j))],
        out_shape=[jax.ShapeDtypeStruct((S, FW), F32), jax.ShapeDtypeStruct((S, FW), BF16),
                   jax.ShapeDtypeStruct((H, 1, S), F32)],
        scratch_shapes=[pltpu.VMEM((HB, T, HEAD_DIM), F32), pltpu.VMEM((HB, T, HEAD_DIM), F32),
                        pltpu.VMEM((HB, 1, T), F32)],
        compiler_params=_params(('parallel', 'parallel', 'arbitrary')))(qn, kn, vb, do, c, ct, lse, dl)


def _shift_down(v, d, rows, fill):
    return jnp.where(rows >= d, pltpu.roll(v, d, 0), fill)


def _shift_up(v, d, rows, S, fill):
    return jnp.where(rows < S - d, pltpu.roll(v, S - d, 0), fill)


SUBLANES = 8


def _scan_by_doubling(a, b, pos, span, reverse):
    n = a.shape[0]
    d = 1
    while d < span:
        if reverse:
            keep = pos < span - d
            a_s, b_s = jnp.where(keep, pltpu.roll(a, n - d, 0), 1.0), jnp.where(keep, pltpu.roll(b, n - d, 0), 0.0)
        else:
            keep = pos >= d
            a_s, b_s = jnp.where(keep, pltpu.roll(a, d, 0), 1.0), jnp.where(keep, pltpu.roll(b, d, 0), 0.0)
        b = a * b_s + b
        a = a * a_s
        d *= 2
    return a, b


def _scan(a, b, rows, S, reverse, scr):
    groups = S // SUBLANES
    a, b = _scan_by_doubling(a, b, jnp.bitwise_and(rows, SUBLANES - 1), SUBLANES, reverse)
    scr[0][...] = a
    scr[1][...] = b
    edge = 0 if reverse else SUBLANES - 1
    a_g = scr[0][pl.ds(edge, groups, stride=SUBLANES), :]
    b_g = scr[1][pl.ds(edge, groups, stride=SUBLANES), :]
    g_pos = lax.broadcasted_iota(jnp.int32, (groups, LANES), 0)
    _, h_g = _scan_by_doubling(a_g, b_g, g_pos, groups, reverse)
    if reverse:
        carry = jnp.where(g_pos < groups - 1, pltpu.roll(h_g, groups - 1, 0), 0.0)
    else:
        carry = jnp.where(g_pos >= 1, pltpu.roll(h_g, 1, 0), 0.0)
    for r in range(SUBLANES):
        scr[0][pl.ds(r, groups, stride=SUBLANES), :] = carry
    return b + a * scr[0][...]


def _lru_forward(u, cw, cb, wra, bra, wri, bri, lam, rows, scr):
    uc = cb + cw[CONV_W - 1] * u
    for d in range(1, CONV_W):
        uc = uc + cw[CONV_W - 1 - d] * _shift_down(u, d, rows, 0.0)
    ucb = uc.astype(BF16)
    r = _sigmoid(jnp.dot(ucb, wra.astype(BF16), preferred_element_type=F32) + bra)
    ig = _sigmoid(jnp.dot(ucb, wri.astype(BF16), preferred_element_type=F32) + bri)
    sp = _softplus(-lam)
    log_a = -LRU_C * r * sp
    a = jnp.exp(log_a)
    sq = jnp.sqrt(_neg_expm1(2.0 * log_a))
    iu = ig * uc
    hseq = _scan(a, sq * iu, rows, u.shape[0], False, scr)
    return uc, ucb, r, ig, sp, a, sq, iu, hseq


def _lru_specs(S, n_u, n_g):
    col = lambda off: pl.BlockSpec((S, LANES), lambda cbk: (0, off + cbk))
    vec = pl.BlockSpec((1, LANES), lambda cbk: (0, cbk))
    mat = pl.BlockSpec((None, LANES, LANES), lambda cbk: (cbk, 0, 0))
    cw = pl.BlockSpec((CONV_W, LANES), lambda cbk: (0, cbk))
    return col, vec, mat, cw


def lru_fwd(proj, conv_w, conv_b, w_ra, b_ra, w_ri, b_ri, lam, u_off, g_off):
    S = proj.shape[0]
    nb = w_ra.shape[0]
    col, vec, mat, cws = _lru_specs(S, u_off, g_off)

    def body(u_ref, g_ref, cw_ref, cb_ref, wra_ref, bra_ref, wri_ref, bri_ref, lam_ref, y_ref, scr0, scr1):
        rows = lax.broadcasted_iota(jnp.int32, (S, LANES), 0)
        cw = [cw_ref[t:t + 1, :] for t in range(CONV_W)]
        hseq = _lru_forward(u_ref[...], cw, cb_ref[...], wra_ref[...], bra_ref[...], wri_ref[...],
                            bri_ref[...], lam_ref[...], rows, (scr0, scr1))[-1]
        y_ref[...] = hseq * _gelu_and_grad(g_ref[...])[0]

    return pl.pallas_call(
        body, name='lru_fwd', grid=(nb,),
        in_specs=[col(u_off), col(g_off), cws, vec, mat, vec, mat, vec, vec], out_specs=col(0),
        out_shape=jax.ShapeDtypeStruct((S, nb * LANES), F32),
        scratch_shapes=[pltpu.VMEM((S, LANES), F32), pltpu.VMEM((S, LANES), F32)],
        compiler_params=_params(('parallel',)))(proj, proj, conv_w, conv_b, w_ra, b_ra, w_ri, b_ri, lam)


def lru_bwd(proj, dy, conv_w, conv_b, w_ra, b_ra, w_ri, b_ri, lam, u_off, g_off):
    S = proj.shape[0]
    nb = w_ra.shape[0]
    LW = nb * LANES
    col, vec, mat, cws = _lru_specs(S, u_off, g_off)

    def body(u_ref, g_ref, dy_ref, cw_ref, cb_ref, wra_ref, bra_ref, wri_ref, bri_ref, lam_ref,
             du_ref, dg_ref, dcw_ref, dcb_ref, dwra_ref, dbra_ref, dwri_ref, dbri_ref, dlam_ref, scr0, scr1):
        rows = lax.broadcasted_iota(jnp.int32, (S, LANES), 0)
        u, lam_v = u_ref[...], lam_ref[...]
        cw = [cw_ref[t:t + 1, :] for t in range(CONV_W)]
        wra, wri = wra_ref[...].astype(BF16), wri_ref[...].astype(BF16)
        uc, ucb, r, ig, sp, a, sq, iu, hseq = _lru_forward(u, cw, cb_ref[...], wra, bra_ref[...], wri, bri_ref[...],
                                                           lam_v, rows, (scr0, scr1))
        gl, dgl = _gelu_and_grad(g_ref[...])
        dy_v = dy_ref[...]
        dg_ref[...] = (dy_v * hseq * dgl).astype(BF16)
        G = _scan(_shift_up(a, 1, rows, S, 0.0), dy_v * gl, rows, S, True, (scr0, scr1))
        da = G * _shift_down(hseq, 1, rows, 0.0)
        diu = G * sq
        dsq = G * iu
        dlog_a = da * a - dsq * a * a / jnp.maximum(sq, 1e-30)
        dr = dlog_a * (-LRU_C * sp)
        dsp = jnp.sum(dlog_a * (-LRU_C * r), axis=0, keepdims=True)
        dlam_ref[...] = -dsp * _sigmoid(-lam_v)
        dzr = dr * r * (1.0 - r)
        dzi = diu * uc * ig * (1.0 - ig)
        dzrb, dzib = dzr.astype(BF16), dzi.astype(BF16)
        duc = (diu * ig + lax.dot_general(dzrb, wra, _DIMS['nt'], preferred_element_type=F32)
               + lax.dot_general(dzib, wri, _DIMS['nt'], preferred_element_type=F32))
        dwra_ref[...] = lax.dot_general(ucb, dzrb, _DIMS['tn'], preferred_element_type=F32)
        dwri_ref[...] = lax.dot_general(ucb, dzib, _DIMS['tn'], preferred_element_type=F32)
        dbra_ref[...] = jnp.sum(dzr, axis=0, keepdims=True)
        dbri_ref[...] = jnp.sum(dzi, axis=0, keepdims=True)
        dcb_ref[...] = jnp.sum(duc, axis=0, keepdims=True)
        du = cw[CONV_W - 1] * duc
        dcw_ref[CONV_W - 1:CONV_W, :] = jnp.sum(duc * u, axis=0, keepdims=True)
        for d in range(1, CONV_W):
            du = du + cw[CONV_W - 1 - d] * _shift_up(duc, d, rows, S, 0.0)
            dcw_ref[CONV_W - 1 - d:CONV_W - d, :] = jnp.sum(duc * _shift_down(u, d, rows, 0.0), axis=0, keepdims=True)
        du_ref[...] = du.astype(BF16)

    sd = jax.ShapeDtypeStruct
    return pl.pallas_call(
        body, name='lru_bwd', grid=(nb,),
        in_specs=[col(u_off), col(g_off), col(0), cws, vec, mat, vec, mat, vec, vec],
        out_specs=[col(0), col(0), cws, vec, mat, vec, mat, vec, vec],
        out_shape=[sd((S, LW), BF16), sd((S, LW), BF16), sd((CONV_W, LW), F32), sd((1, LW), F32),
                   sd((nb, LANES, LANES), F32), sd((1, LW), F32), sd((nb, LANES, LANES), F32), sd((1, LW), F32),
                   sd((1, LW), F32)],
        scratch_shapes=[pltpu.VMEM((S, LANES), F32), pltpu.VMEM((S, LANES), F32)],
        compiler_params=_params(('parallel',)))(proj, proj, dy, conv_w, conv_b, w_ra, b_ra, w_ri, b_ri, lam)


def mix_fwd(o_fox, y_lru, g_fox, g_lru):
    S, FW = o_fox.shape
    tr = _tile(S, (256, 128))

    def body(o_ref, y_ref, gf_ref, gl_ref, m_ref):
        m_ref[...] = jnp.concatenate([_rms(o_ref[...], gf_ref[...]), _rms(y_ref[...], gl_ref[...])],
                                     axis=1).astype(BF16)

    return _rows_call('mix_fwd', body, S, tr,
                      [(o_fox, _rb(tr, FW)), (y_lru, _rb(tr, FW)), (g_fox, _fb((1, FW))), (g_lru, _fb((1, FW)))],
                      [((S, 2 * FW), BF16, _rb(tr, 2 * FW))])[0]


def mix_bwd(o_fox, y_lru, g_fox, g_lru, dmix):
    S, FW = o_fox.shape
    H = FW // HEAD_DIM
    tr = _tile(S, (256, 128))

    def body(o_ref, y_ref, gf_ref, gl_ref, df_ref, dl_ref, do_ref, dlt_ref, dy_ref, dgf_ref, dgl_ref):
        o = o_ref[...]
        do, dgf = _rms_bwd(o, gf_ref[...], df_ref[...])
        dyl, dgl = _rms_bwd(y_ref[...], gl_ref[...], dl_ref[...])
        do_ref[...] = do.astype(BF16)
        dy_ref[...] = dyl
        prod = do * o
        for h in range(H):
            dlt_ref[h] = jnp.broadcast_to(
                jnp.sum(prod[:, h * HEAD_DIM:(h + 1) * HEAD_DIM], axis=1, keepdims=True), (tr, LANES))
        first = pl.program_id(0) == 0
        _acc_out(dgf_ref, first, dgf)
        _acc_out(dgl_ref, first, dgl)

    g = _fb((1, FW))
    return _rows_call('mix_bwd', body, S, tr,
                      [(o_fox, _rb(tr, FW)), (y_lru, _rb(tr, FW)), (g_fox, g), (g_lru, g), (dmix, _rb(tr, FW, 0)),
                       (dmix, _rb(tr, FW, 1))],
                      [((S, FW), BF16, _rb(tr, FW)), ((H, S, LANES), F32, pl.BlockSpec((H, tr, LANES), lambda i: (0, i, 0))),
                       ((S, FW), F32, _rb(tr, FW)), ((1, FW), F32, g), ((1, FW), F32, g)])


def _xattn_heads(cq_raw, ckv, g_cq, g_ck, XW):
    out = []
    for h in range(XW // HEAD_DIM):
        sl = slice(h * HEAD_DIM, (h + 1) * HEAD_DIM)
        out.append((cq_raw[:, sl], _rms(cq_raw[:, sl], g_cq), ckv[:, sl], _rms(ckv[:, sl], g_ck),
                    ckv[:, XW + h * HEAD_DIM:XW + (h + 1) * HEAD_DIM].astype(BF16)))
    return out


def xattn_fwd(cq_raw, ckv, g_cq, g_ck):
    S, XW = cq_raw.shape
    M = ckv.shape[0]
    tr = _tile(S, (512, 256, 128))

    def body(q_ref, kv_ref, gq_ref, gk_ref, o_ref):
        outs = []
        for _, qn, _, kn, v in _xattn_heads(q_ref[...], kv_ref[...], gq_ref[...], gk_ref[...], XW):
            s = lax.dot_general(qn.astype(BF16), kn.astype(BF16), _DIMS['nt'], preferred_element_type=F32)
            s = s / math.sqrt(HEAD_DIM)
            p = jnp.exp(s - jnp.max(s, axis=1, keepdims=True))
            p = p / jnp.sum(p, axis=1, keepdims=True)
            outs.append(jnp.dot(p.astype(BF16), v, preferred_element_type=F32))
        o_ref[...] = jnp.concatenate(outs, axis=1).astype(BF16)

    g = _fb((1, HEAD_DIM))
    return _rows_call('xattn_fwd', body, S, tr,
                      [(cq_raw, _rb(tr, XW)), (ckv, _fb((M, 2 * XW))), (g_cq, g), (g_ck, g)],
                      [((S, XW), BF16, _rb(tr, XW))])[0]


def xattn_bwd(cq_raw, ckv, g_cq, g_ck, do):
    S, XW = cq_raw.shape
    M = ckv.shape[0]
    tr = _tile(S, (512, 256, 128))
    n = S // tr

    def body(q_ref, kv_ref, gq_ref, gk_ref, do_ref, dq_ref, dkv_ref, dgq_ref, dgk_ref):
        i = pl.program_id(0)
        do_v = do_ref[...]
        dqs, dkn, dvs = [], [], []
        dgq = jnp.zeros((1, HEAD_DIM), F32)
        for h, (q_raw, qn, _, kn, v) in enumerate(_xattn_heads(q_ref[...], kv_ref[...], gq_ref[...], gk_ref[...], XW)):
            qb, kb = qn.astype(BF16), kn.astype(BF16)
            doh = do_v[:, h * HEAD_DIM:(h + 1) * HEAD_DIM]
            s = lax.dot_general(qb, kb, _DIMS['nt'], preferred_element_type=F32) / math.sqrt(HEAD_DIM)
            p = jnp.exp(s - jnp.max(s, axis=1, keepdims=True))
            p = p / jnp.sum(p, axis=1, keepdims=True)
            dp = lax.dot_general(doh, v, _DIMS['nt'], preferred_element_type=F32)
            ds = (p * (dp - jnp.sum(p * dp, axis=1, keepdims=True)) / math.sqrt(HEAD_DIM)).astype(BF16)
            dvs.append(lax.dot_general(p.astype(BF16), doh, _DIMS['tn'], preferred_element_type=F32))
            dkn.append(lax.dot_general(ds, qb, _DIMS['tn'], preferred_element_type=F32))
            dq, g1 = _rms_bwd(q_raw, gq_ref[...], jnp.dot(ds, kb, preferred_element_type=F32))
            dqs.append(dq)
            dgq = dgq + g1
        dq_ref[...] = jnp.concatenate(dqs, axis=1).astype(BF16)
        first = i == 0
        _acc_out(dgq_ref, first, dgq)
        _acc_out(dkv_ref, first, jnp.concatenate(dkn + dvs, axis=1))

        @pl.when(i == n - 1)
        def _():
            kv = kv_ref[...]
            acc = dkv_ref[...]
            dk, gk = _heads(lambda t, d: _rms_bwd(t, gk_ref[...], d), XW // HEAD_DIM, kv[:, :XW], acc[:, :XW])
            dkv_ref[:, :XW] = dk
            dgk_ref[...] = gk

    g = _fb((1, HEAD_DIM))
    return _rows_call('xattn_bwd', body, S, tr,
                      [(cq_raw, _rb(tr, XW)), (ckv, _fb((M, 2 * XW))), (g_cq, g), (g_ck, g), (do, _rb(tr, XW))],
                      [((S, XW), BF16, _rb(tr, XW)), ((M, 2 * XW), F32, _fb((M, 2 * XW))), ((1, HEAD_DIM), F32, g),
                       ((1, HEAD_DIM), F32, g)])


def gate_up_fwd(hf, w, F):
    S, D = hf.shape
    J, _, Nj = w.shape
    tm = _tile(S, (1024, 512, 256, 128))
    tn = _tile(Nj, (256, 128))
    per = Nj // tn
    half = J // 2 * per

    def body(a_ref, bg_ref, bu_ref, gu_ref, act_ref):
        a = a_ref[...]
        g = jnp.dot(a, bg_ref[...], preferred_element_type=F32)
        u = jnp.dot(a, bu_ref[...], preferred_element_type=F32)
        gu_ref[0] = g.astype(BF16)
        gu_ref[1] = u.astype(BF16)
        act_ref[...] = (g * _sigmoid(g) * u).astype(BF16)

    return pl.pallas_call(
        body, name='proj_gate_up', grid=(S // tm, half),
        in_specs=[pl.BlockSpec((tm, D), lambda m, n: (m, 0)),
                  pl.BlockSpec((None, D, tn), lambda m, n: (n // per, 0, n % per)),
                  pl.BlockSpec((None, D, tn), lambda m, n: ((n + half) // per, 0, n % per))],
        out_specs=[pl.BlockSpec((2, tm, tn), lambda m, n: (0, m, n)), pl.BlockSpec((tm, tn), lambda m, n: (m, n))],
        out_shape=[jax.ShapeDtypeStruct((2, S, F), BF16), jax.ShapeDtypeStruct((S, F), BF16)],
        compiler_params=_params(('parallel', 'parallel')))(hf, w, w)


def down_bwd_x(dyb, w_down, gu, after):
    S, D = dyb.shape
    F = w_down.shape[0]
    tm = _tile(S, (1024, 512, 256, 128))
    tn = _tile(F, (512, 256, 128))

    def body(a_ref, b_ref, gu_ref, after_ref, o_ref):
        da = lax.dot_general(a_ref[...], b_ref[...], _DIMS['nt'], preferred_element_type=F32)
        g = gu_ref[0].astype(F32)
        sg = _sigmoid(g)
        o_ref[0] = (da * gu_ref[1].astype(F32) * sg * (1.0 + g * (1.0 - sg))).astype(BF16)
        o_ref[1] = (da * g * sg).astype(BF16)

    planes = pl.BlockSpec((2, tm, tn), lambda m, n: (0, m, n))
    return pl.pallas_call(
        body, name='bwd_down_x', grid=(S // tm, F // tn),
        in_specs=[pl.BlockSpec((tm, D), lambda m, n: (m, 0)), pl.BlockSpec((tn, D), lambda m, n: (n, 0)), planes, ANY],
        out_specs=planes, out_shape=jax.ShapeDtypeStruct((2, S, F), BF16),
        compiler_params=_params(('parallel', 'parallel')))(dyb, w_down, gu, after)


def down_fwd_loss(act, w_down, x2, target):
    S, F = act.shape
    D = w_down.shape[1]
    tm, tn, tk = _mm_tiles(S, D, F, F, act, w_down, F32, x2, tn_cands=(512, 256, 128))
    nk = F // tk

    def body(a_ref, b_ref, x_ref, t_ref, d_ref, db_ref, l_ref, acc):
        m, n, k = pl.program_id(0), pl.program_id(1), pl.program_id(2)
        part = jnp.dot(a_ref[...], b_ref[...], preferred_element_type=F32)

        @pl.when(k == 0)
        def _():
            acc[...] = part

        @pl.when(k > 0)
        def _():
            acc[...] += part

        @pl.when(k == nk - 1)
        def _():
            err = acc[...] + x_ref[...] - t_ref[...]
            d = err * (1.0 / D)
            d_ref[...] = d
            db_ref[...] = d.astype(BF16)
            tot = jnp.sum(jnp.sum(err * err, axis=1, keepdims=True), axis=0, keepdims=True) * (0.5 / D)
            _acc_out(l_ref, jnp.logical_and(m == 0, n == 0), jnp.broadcast_to(tot, (1, LANES)))

    tile = pl.BlockSpec((tm, tn), lambda m, n, k: (m, n))
    return pl.pallas_call(
        body, name='proj_down', grid=(S // tm, D // tn, nk),
        in_specs=[pl.BlockSpec((tm, tk), lambda m, n, k: (m, k)), pl.BlockSpec((tk, tn), lambda m, n, k: (k, n)), tile, tile],
        out_specs=[tile, tile, pl.BlockSpec((1, LANES), lambda m, n, k: (0, 0))],
        out_shape=[jax.ShapeDtypeStruct((S, D), F32), jax.ShapeDtypeStruct((S, D), BF16),
                   jax.ShapeDtypeStruct((1, LANES), F32)],
        scratch_shapes=[pltpu.VMEM((tm, tn), F32)],
        compiler_params=_params(('arbitrary', 'arbitrary', 'arbitrary')))(act, w_down, x2, target)


def _adamw_math(w, gv, m, v):
    mn = ADAM_B1 * m + (1.0 - ADAM_B1) * gv
    vn = ADAM_B2 * v + (1.0 - ADAM_B2) * (gv * gv)
    m_hat = mn / (1.0 - ADAM_B1 ** ADAM_STEP)
    v_hat = vn / (1.0 - ADAM_B2 ** ADAM_STEP)
    return -ADAM_LR * (m_hat / (jnp.sqrt(v_hat) + ADAM_EPS) + ADAM_WD * w), mn, vn


def adamw(name, w, g, m, v):
    R, C = w.shape
    tr = _row_tile(R, C)

    def body(w_ref, g_ref, m_ref, v_ref, d_ref, mo_ref, vo_ref):
        d_ref[...], mo_ref[...], vo_ref[...] = _adamw_math(w_ref[...], g_ref[...], m_ref[...], v_ref[...])

    spec = _rb(tr, C)
    return _rows_call(name, body, R, tr, [(w, spec), (g, spec), (m, spec), (v, spec)], [((R, C), F32, spec)] * 3)


def adamw_halves(name, w, mine, other, m, v, c_idx):
    R, C = w.shape
    hr = R // 2
    tr = _row_tile(hr, C)

    def body(c_ref, w_ref, a_ref, b_ref, m_ref, v_ref, g_ref, d_ref, mo_ref, vo_ref):
        gv = jnp.where(pl.program_id(0) == c_ref[0], a_ref[...], b_ref[...])
        g_ref[...] = gv
        d_ref[...], mo_ref[...], vo_ref[...] = _adamw_math(w_ref[...], gv, m_ref[...], v_ref[...])

    full = pl.BlockSpec((None, tr, C), lambda hh, i, c_ref: (hh, i, 0))
    mine_spec = pl.BlockSpec((tr, C), lambda hh, i, c_ref: (jnp.where(hh == c_ref[0], i, 0), 0))
    other_spec = pl.BlockSpec((tr, C), lambda hh, i, c_ref: (jnp.where(hh == c_ref[0], 0, i), 0))
    outs = pl.pallas_call(
        body, name=name,
        grid_spec=pltpu.PrefetchScalarGridSpec(num_scalar_prefetch=1, grid=(2, hr // tr),
                                               in_specs=[full, mine_spec, other_spec, full, full], out_specs=[full] * 4),
        out_shape=[jax.ShapeDtypeStruct((2, hr, C), F32)] * 4,
        compiler_params=_params(('parallel', 'parallel')))(
            c_idx, w.reshape(2, hr, C), mine, other, m.reshape(2, hr, C), v.reshape(2, hr, C))
    return [o.reshape(R, C) for o in outs]


def _place():
    x, y, c = lax.axis_index('x'), lax.axis_index('y'), lax.axis_index('c')
    return x, y, c, [(1 - x, y), (x, 1 - y), (1 - x, 1 - y)]


def _rcopy(src, dst, ssem, rsem, dev):
    return pltpu.make_async_remote_copy(src_ref=src, dst_ref=dst, send_sem=ssem, recv_sem=rsem, device_id=dev,
                                        device_id_type=MESH)


HBM = pl.BlockSpec(memory_space=pltpu.HBM)
SEM = pl.BlockSpec(memory_space=pltpu.SEMAPHORE)
EFFECT = pltpu.SideEffectType.DATAFLOW_SIDE_EFFECTING


def _in_hbm(a):
    return pltpu.with_memory_space_constraint(a, pltpu.HBM)


def _rows_part(shape, whole, half):
    return pl.ds(0, shape[0]) if whole else pl.ds(half * (shape[0] // 2), shape[0] // 2)


def gather_start(name, shards, whole):
    nT = len(shards)

    def body(*refs):
        srcs, lands = refs[:nT], refs[nT:2 * nT]
        ssem, rsem, token = refs[2 * nT], refs[2 * nT + 1], refs[-1]
        x, y, c, chips = _place()
        for t in range(nT):
            rows = _rows_part(shards[t].shape, whole[t], c)
            for k, (px, py) in enumerate(chips):
                _rcopy(srcs[t].at[rows], lands[t].at[2 * x + y, rows], ssem.at[3 * t + k], rsem.at[3 * t + k],
                       (px, py, c)).start()
        token[...] = jnp.zeros_like(token)

    zones = [lax.empty((N_CHIPS,) + s.shape, s.dtype) for s in shards]
    outs = pl.pallas_call(
        body, name=name,
        out_shape=(pltpu.SemaphoreType.DMA((3 * nT,)), pltpu.SemaphoreType.DMA((3 * nT,)),
                   *[pltpu.HBM(s.shape, s.dtype) for s in shards], *[pltpu.HBM(z.shape, z.dtype) for z in zones],
                   jax.ShapeDtypeStruct((8, LANES), F32)),
        in_specs=[HBM] * (2 * nT), out_specs=(SEM, SEM, *[HBM] * (2 * nT), pl.BlockSpec(memory_space=pltpu.VMEM)),
        input_output_aliases={i: 2 + i for i in range(2 * nT)},
        compiler_params=pltpu.CompilerParams(has_side_effects=EFFECT))(*[_in_hbm(a) for a in list(shards) + zones])
    return outs[0], outs[1], outs[2:2 + nT], outs[2 + nT:2 + 2 * nT], outs[-1]


def gather_wait(name, t, shard, zone, ssem, rsem, after, whole):
    after = after if isinstance(after, (list, tuple)) else [after]

    def body(src_ref, land_ref, ssem_ref, rsem_ref, *rest):
        x, y, c, chips = _place()
        rows = _rows_part(shard.shape, whole, c)
        for k, (px, py) in enumerate(chips):
            cp = _rcopy(src_ref.at[rows], land_ref.at[2 * px + py, rows], ssem_ref.at[3 * t + k], rsem_ref.at[3 * t + k],
                        (px, py, c))
            cp.wait_send()
            cp.wait_recv()

    return pl.pallas_call(
        body, name=name, out_shape=(pltpu.HBM(shard.shape, shard.dtype), pltpu.HBM(zone.shape, zone.dtype)),
        in_specs=(HBM, HBM, SEM, SEM, *[ANY] * len(after)), out_specs=(HBM, HBM), input_output_aliases={0: 0, 1: 1},
        compiler_params=pltpu.CompilerParams(has_side_effects=EFFECT))(shard, zone, ssem, rsem, *after)


def pair_swap(name, zone):
    hr = zone.shape[1] // 2

    def body(z_in, z_ref, ssem, rsem):
        x, y, c, chips = _place()
        cps = []
        for k, (px, py) in enumerate(chips):
            blk = z_ref.at[2 * px + py, pl.ds(c * hr, hr)]
            cps.append(_rcopy(blk, blk, ssem.at[k], rsem.at[k], (x, y, 1 - c)))
            cps[-1].start()
        for k, (px, py) in enumerate(chips):
            blk = z_ref.at[2 * px + py, pl.ds((1 - c) * hr, hr)]
            _rcopy(blk, blk, ssem.at[k], rsem.at[k], (x, y, 1 - c)).wait_recv()
        for cp in cps:
            cp.wait_send()

    return pl.pallas_call(
        body, name=name, in_specs=[ANY], out_specs=ANY, out_shape=jax.ShapeDtypeStruct(zone.shape, zone.dtype),
        input_output_aliases={0: 0},
        scratch_shapes=[pltpu.SemaphoreType.DMA((3,)), pltpu.SemaphoreType.DMA((3,))],
        compiler_params=_params())(zone)


N_SENDERS = 7


def _scatter_copies(g_ref, l_ref, ssem, rsem):
    x, y, c, chips = _place()
    cps = []
    for k, (px, py) in enumerate(chips):
        for d in range(2):
            to = (c + d) % 2
            cps.append(_rcopy(g_ref.at[2 * px + py, to], l_ref.at[2 * k + d], ssem.at[2 * k + d], rsem.at[2 * k + d],
                              (px, py, to)))
    cps.append(_rcopy(g_ref.at[2 * x + y, 1 - c], l_ref.at[6], ssem.at[6], rsem.at[6], (x, y, 1 - c)))
    return cps


def scatter_start(name, g):
    def body(g_ref, l_ref, ssem, rsem, g_out, l_out, token):
        for cp in _scatter_copies(g_ref, l_ref, ssem, rsem):
            cp.start()
        token[...] = jnp.zeros_like(token)

    zone = lax.empty((N_SENDERS,) + g.shape[2:], g.dtype)
    return pl.pallas_call(
        body, name=name,
        out_shape=(pltpu.SemaphoreType.DMA((N_SENDERS,)), pltpu.SemaphoreType.DMA((N_SENDERS,)),
                   pltpu.HBM(g.shape, g.dtype), pltpu.HBM(zone.shape, zone.dtype), jax.ShapeDtypeStruct((8, LANES), F32)),
        in_specs=[HBM, HBM], out_specs=(SEM, SEM, HBM, HBM, pl.BlockSpec(memory_space=pltpu.VMEM)),
        input_output_aliases={0: 2, 1: 3},
        compiler_params=pltpu.CompilerParams(has_side_effects=EFFECT))(_in_hbm(g), _in_hbm(zone))


def scatter_wait(name, g, zone, ssem, rsem, after):
    def body(g_ref, l_ref, ssem_ref, rsem_ref, after_ref, g_out, l_out):
        for cp in _scatter_copies(g_ref, l_ref, ssem_ref, rsem_ref):
            cp.wait_send()
            cp.wait_recv()

    return pl.pallas_call(
        body, name=name, out_shape=(pltpu.HBM(g.shape, g.dtype), pltpu.HBM(zone.shape, zone.dtype)),
        in_specs=(HBM, HBM, SEM, SEM, ANY), out_specs=(HBM, HBM), input_output_aliases={0: 0, 1: 1},
        compiler_params=pltpu.CompilerParams(has_side_effects=EFFECT))(g, zone, ssem, rsem, after)


def sum_parts(name, g, landed, chip_idx, c_idx):
    hr, C = g.shape[2:]
    tr = _row_tile(hr, C, min_rows=16)

    def body(me_ref, c_ref, g_ref, l_ref, o_ref):
        acc = g_ref[...].astype(F32)
        for s in range(N_SENDERS):
            acc = acc + l_ref[s].astype(F32)
        o_ref[...] = acc

    return pl.pallas_call(
        body, name=name,
        grid_spec=pltpu.PrefetchScalarGridSpec(
            num_scalar_prefetch=2, grid=(hr // tr,),
            in_specs=[pl.BlockSpec((None, None, tr, C), lambda i, me_ref, c_ref: (me_ref[0], c_ref[0], i, 0)),
                      pl.BlockSpec((N_SENDERS, tr, C), lambda i, me_ref, c_ref: (0, i, 0))],
            out_specs=pl.BlockSpec((tr, C), lambda i, me_ref, c_ref: (i, 0))),
        out_shape=jax.ShapeDtypeStruct((hr, C), F32),
        compiler_params=_params(('parallel',)))(chip_idx, c_idx, g, landed)


def pair_join(name, halves):
    nT = len(halves)

    def body(*refs):
        ins, outs = refs[:nT], refs[nT:2 * nT]
        ssem, rsem = refs[2 * nT:]
        x, y, c, _ = _place()
        cps = [_rcopy(ins[t], outs[t], ssem.at[t], rsem.at[t], (x, y, 1 - c)) for t in range(nT)]
        for cp in cps:
            cp.start()
        for cp in cps:
            cp.wait()

    return pl.pallas_call(
        body, name=name, in_specs=[ANY] * nT, out_specs=[ANY] * nT,
        out_shape=[jax.ShapeDtypeStruct(h.shape, h.dtype) for h in halves],
        scratch_shapes=[pltpu.SemaphoreType.DMA((nT,)), pltpu.SemaphoreType.DMA((nT,))],
        compiler_params=_params())(*halves)


N_DEVICES = 8


def _spread_copies(b_ref, l_ref, ssem, rsem):
    x, y, c, chips = _place()
    me = 4 * x + 2 * y + c
    pairs = []
    for px, py, pc in [(px, py, pc) for px, py in chips for pc in (c, 1 - c)] + [(x, y, 1 - c)]:
        it = 4 * px + 2 * py + pc
        pairs.append((_rcopy(b_ref, l_ref.at[me], ssem.at[it], rsem.at[me], (px, py, pc)),
                      _rcopy(b_ref, l_ref.at[it], ssem.at[it], rsem.at[it], (px, py, pc))))
    return pairs


def spread_start(name, buf):
    def body(b_ref, l_ref, ssem, rsem, b_out, l_out, token):
        for mine, _ in _spread_copies(b_ref, l_ref, ssem, rsem):
            mine.start()
        token[...] = jnp.zeros_like(token)

    zone = lax.empty((N_DEVICES,) + buf.shape, buf.dtype)
    return pl.pallas_call(
        body, name=name,
        out_shape=(pltpu.SemaphoreType.DMA((N_DEVICES,)), pltpu.SemaphoreType.DMA((N_DEVICES,)),
                   pltpu.HBM(buf.shape, buf.dtype), pltpu.HBM(zone.shape, zone.dtype), jax.ShapeDtypeStruct((8, LANES), F32)),
        in_specs=[HBM, HBM], out_specs=(SEM, SEM, HBM, HBM, pl.BlockSpec(memory_space=pltpu.VMEM)),
        input_output_aliases={0: 2, 1: 3},
        compiler_params=pltpu.CompilerParams(has_side_effects=EFFECT))(_in_hbm(buf), _in_hbm(zone))


def spread_wait(name, buf, zone, ssem, rsem, after):
    def body(b_ref, l_ref, ssem_ref, rsem_ref, after_ref, b_out, l_out):
        for mine, theirs in _spread_copies(b_ref, l_ref, ssem_ref, rsem_ref):
            mine.wait_send()
            theirs.wait_recv()

    return pl.pallas_call(
        body, name=name, out_shape=(pltpu.HBM(buf.shape, buf.dtype), pltpu.HBM(zone.shape, zone.dtype)),
        in_specs=(HBM, HBM, SEM, SEM, ANY), out_specs=(HBM, HBM), input_output_aliases={0: 0, 1: 1},
        compiler_params=pltpu.CompilerParams(has_side_effects=EFFECT))(buf, zone, ssem, rsem, after)


def sum_devices(name, zone):
    _, R, C = zone.shape
    tr = _row_tile(R, C)

    def body(z_ref, o_ref):
        acc = z_ref[0]
        for d in range(1, N_DEVICES):
            acc = acc + z_ref[d]
        o_ref[...] = acc

    return pl.pallas_call(
        body, name=name, grid=(R // tr,), in_specs=[pl.BlockSpec((N_DEVICES, tr, C), lambda i: (0, i, 0))],
        out_specs=pl.BlockSpec((tr, C), lambda i: (i, 0)), out_shape=jax.ShapeDtypeStruct((R, C), F32),
        compiler_params=_params(('parallel',)))(zone)


class _InWindows:
    def __init__(self, FW, LW, H, C):
        gap = LANES - H
        padded = lambda o: o if o < 3 * FW + H else o + gap
        self.width = 3 * FW + LANES + 2 * LW
        self.f_block = 3 * FW // LANES
        self.first = [padded(C * j) // LANES for j in range(N_CHIPS)]
        self.blocks = max(padded(C * (j + 1) - 1) // LANES - self.first[j] + 1 for j in range(N_CHIPS))
        assert all((b + self.blocks) * LANES <= self.width for b in self.first)
        self.cols = self.blocks * LANES
        self.runs = []
        for j in range(N_CHIPS):
            cut = min(max(3 * FW + H - C * j, 0), C)
            spans = [(0, cut), (cut, C)]
            self.runs.append([(t0, t1, padded(C * j + t0) - LANES * self.first[j]) for t0, t1 in spans if t1 > t0])

    def to_window(self, shard, chip):
        def place(j, s):
            parts, pos = [], 0
            for t0, t1, w0 in self.runs[j]:
                parts += [jnp.zeros((s.shape[0], w0 - pos), s.dtype), s[:, t0:t1]]
                pos = w0 + t1 - t0
            parts.append(jnp.zeros((s.shape[0], self.cols - pos), s.dtype))
            return jnp.concatenate([p for p in parts if p.shape[1]], axis=1)
        return lax.switch(chip, [functools.partial(place, j) for j in range(N_CHIPS)], shard)

    def from_window(self, win, chip):
        def take(j, w):
            return jnp.concatenate([w[:, w0:w0 + t1 - t0] for t0, t1, w0 in self.runs[j]], axis=1)
        return lax.switch(chip, [functools.partial(take, j) for j in range(N_CHIPS)], win)

    def _spans(self, j):
        b0, b1 = self.first[j], self.first[j] + self.blocks
        return (b0, min(b1, self.f_block)), b0 <= self.f_block < b1, (max(b0, self.f_block + 1), b1)

    def assemble(self, zone):
        main, f_blk = None, None
        for j in range(N_CHIPS):
            (a0, a1), has_f, (c0, c1) = self._spans(j)
            for p0, p1, shift in ((a0, a1, 0), (c0, c1, 1)):
                if p1 > p0:
                    part = zone[j][:, (p0 - self.first[j]) * LANES:(p1 - self.first[j]) * LANES]
                    part = jnp.pad(part, ((0, 0), ((p0 - shift) * LANES, self.width - LANES - (p1 - shift) * LANES)))
                    main = part if main is None else main + part
            if has_f:
                part = zone[j][:, (self.f_block - self.first[j]) * LANES:(self.f_block - self.first[j] + 1) * LANES]
                f_blk = part if f_blk is None else f_blk + part
        return main, f_blk

    def windows(self, main, f_blk):
        out = []
        for j in range(N_CHIPS):
            (a0, a1), has_f, (c0, c1) = self._spans(j)
            parts = [main[:, a0 * LANES:a1 * LANES]] if a1 > a0 else []
            parts += [f_blk] if has_f else []
            parts += [main[:, (c0 - 1) * LANES:(c1 - 1) * LANES]] if c1 > c0 else []
            out.append(jnp.concatenate(parts, axis=1))
        return jnp.stack(out)


_PACK = 8 * LANES


PACK_ROWS = 256


def _pack(arrs):
    flat = []
    for a in arrs:
        v = a.reshape(-1).astype(F32)
        flat.append(jnp.pad(v, (0, (-v.shape[0]) % _PACK)))
    rows = sum(v.shape[0] for v in flat) // LANES
    flat.append(jnp.zeros(((-rows) % PACK_ROWS) * LANES, F32))
    return jnp.concatenate(flat).reshape(-1, LANES)


def _unpack(buf, shapes):
    out, off = [], 0
    flat = buf.reshape(-1)
    for sh in shapes:
        n = math.prod(sh)
        out.append(flat[off:off + n].reshape(sh))
        off += n + (-n) % _PACK
    return out


def kernel(x, mem, g_mix, w_in, b_f, g_q, g_k, conv_w, conv_b, w_ra, b_ra, w_ri, b_ri, lam, g_fox_out, g_lru_out, w_out, g_xattn, g_mem, w_cq, w_ckv, g_cq, g_ck, w_co, g_ffn, w_gate_up, w_down, loss_target, m_g_mix, m_w_in, m_b_f, m_g_q, m_g_k, m_conv_w, m_conv_b, m_w_ra, m_b_ra, m_w_ri, m_b_ri, m_lam, m_g_fox_out, m_g_lru_out, m_w_out, m_g_xattn, m_g_mem, m_w_cq, m_w_ckv, m_g_cq, m_g_ck, m_w_co, m_g_ffn, m_w_gate_up, m_w_down, v_g_mix, v_w_in, v_b_f, v_g_q, v_g_k, v_conv_w, v_conv_b, v_w_ra, v_b_ra, v_w_ri, v_b_ri, v_lam, v_g_fox_out, v_g_lru_out, v_w_out, v_g_xattn, v_g_mem, v_w_cq, v_w_ckv, v_g_cq, v_g_ck, v_w_co, v_g_ffn, v_w_gate_up, v_w_down):
    given = dict(locals())
    W = {n: given[n][0] for n in WEIGHTS}
    M1 = {n: given['m_' + n][0] for n in WEIGHTS}
    V1 = {n: given['v_' + n][0] for n in WEIGHTS}
    xs, ms, tgt = x[0], mem[0], loss_target[0]
    S, D = xs.shape
    H = W['b_f'].shape[0]
    FW = H * HEAD_DIM
    LW = W['lam'].shape[0]
    nb = W['w_ra'].shape[0]
    XW = W['w_cq'].shape[1]
    F = W['w_down'].shape[0] * N_CHIPS
    IN_W = W['w_in'].shape[1] * N_CHIPS
    assert FW == LW and LW == nb * LANES and IN_W == 3 * FW + H + 2 * LW and H <= 8
    T = _tile(S, (512, 256, 128))
    c_idx = lax.axis_index('c').astype(jnp.int32).reshape(1)
    chip = 2 * lax.axis_index('x') + lax.axis_index('y')
    chip_idx = chip.astype(jnp.int32).reshape(1)
    vec = lambda n: W[n].reshape(1, -1)

    wins = _InWindows(FW, LW, H, W['w_in'].shape[1])
    started = {}
    g_tok = jnp.zeros((1, 1), F32)
    for call, names in (('gather_start_first', ['conv_w', 'w_in']), ('gather_start_rest', BIG[1:])):
        own = [W[n].reshape(-1, LANES) if n == 'conv_w' else W[n].astype(BF16) + g_tok.astype(BF16) for n in names]
        own = [wins.to_window(o, chip) if n == 'w_in' else o for n, o in zip(names, own)]
        ssem, rsem, srcs, zones, tok = gather_start(call, own, [n == 'conv_w' for n in names])
        g_tok = tok[0:1, 0:1]
        started.update({n: (t, srcs[t], zones[t], ssem, rsem) for t, n in enumerate(names)})

    def fetch(n, after):
        t, g_src, g_zone, g_ssem, g_rsem = started[n]
        src, zone = gather_wait('gather_wait_' + n, t, g_src, g_zone, g_ssem, g_rsem, after, n == 'conv_w')
        if n != 'conv_w':
            zone = pair_swap('pair_swap_' + n, zone)
        return lax.dynamic_update_index_in_dim(zone, src, chip, 0)

    b_f_pad = jnp.pad(vec('b_f'), ((0, 0), (0, LANES - H)))
    u_off, g_off = 3 * FW // LANES, (3 * FW + LW) // LANES

    h1 = norm_fwd('norm_mix', xs, vec('g_mix') + g_tok[0:1, 0:1])
    conv_full = fetch('conv_w', h1).reshape(N_CHIPS, CONV_W, LW // N_CHIPS).transpose(1, 0, 2).reshape(CONV_W, LW)
    w5, wf = wins.assemble(fetch('w_in', [h1, M1['w_in'], V1['w_in']]))
    proj = _mm('proj_in', h1, w5, 'nn', F32)
    f_raw = _mm('proj_f', h1, wf, 'nn', F32)
    qn, kn, vb = qkv_fwd(proj, vec('g_q'), vec('g_k'), FW)
    cc = fgate_fwd(f_raw, b_f_pad)
    ct = cc[:, :8].T
    o_fox, lse = fox_fwd(qn, kn, vb, cc, ct, T)
    lru_w = (conv_full, vec('conv_b'), W['w_ra'], vec('b_ra'), W['w_ri'], vec('b_ri'), vec('lam'))
    y_lru = lru_fwd(proj, *lru_w, u_off, g_off)
    mixn = mix_fwd(o_fox, y_lru, vec('g_fox_out'), vec('g_lru_out'))
    w_out_f = fetch('w_out', mixn).reshape(2 * FW, D)
    x1 = _mm('proj_out', mixn, w_out_f, 'nn', F32, res=xs)

    hq = norm_fwd('norm_xq', x1, vec('g_xattn'))
    mn = norm_fwd('norm_mem', ms, vec('g_mem'))
    w_cq_f = fetch('w_cq', hq).reshape(D, XW)
    w_ckv_f = fetch('w_ckv', hq).reshape(D, 2 * XW)
    cq_raw = _mm('proj_cq', hq, w_cq_f, 'nn', F32)
    ckv = _mm('proj_ckv', mn, w_ckv_f, 'nn', F32)
    o_x = xattn_fwd(cq_raw, ckv, vec('g_cq'), vec('g_ck'))
    w_co_g = fetch('w_co', o_x)
    x2 = _mm_colsharded('proj_co', o_x, w_co_g, F32, res=x1)

    hf = norm_fwd('norm_ffn', x2, vec('g_ffn'))
    w_gu_g = fetch('w_gate_up', hf)
    gu, act = gate_up_fwd(hf, w_gu_g, F)
    w_down_f = fetch('w_down', act).reshape(F, D)
    dy, dyb, loss_blk = down_fwd_loss(act, w_down_f, x2, tgt)

    gw, pending = {}, []

    def reduce_begin(n, g):
        sp = g.reshape(N_CHIPS, 2, g.shape[1] // 2, g.shape[2])
        ssem, rsem, sp, zone, tok = scatter_start('scatter_start_' + n, sp)
        pending.append((n, sp, zone, ssem, rsem))
        return tok[0:1, 0:1]

    t_down = reduce_begin('w_down', _mm('bwd_down_w', act, dyb, 'tn', BF16).reshape(N_CHIPS, F // N_CHIPS, D))
    dgu = down_bwd_x(dyb, w_down_f, gu, t_down)
    dhf = _mm_colsharded_t('bwd_gate_up_x', dgu, w_gu_g, F32)
    t_gu = reduce_begin('w_gate_up', _mm_grad_colsharded('bwd_gate_up_w', hf, dgu, N_CHIPS, BF16))
    dx2, dx2b, gw['g_ffn'] = norm_bwd('norm_ffn_bwd', x2, vec('g_ffn') + t_down + t_gu, dhf, res=dy)

    do_x = _mm_colsharded_t('bwd_co_x', dx2b, w_co_g, BF16)
    t_co = reduce_begin('w_co', _mm_grad_colsharded('bwd_co_w', o_x, dx2b, N_CHIPS, BF16))
    dcq_raw, dckv, gw['g_cq'], gw['g_ck'] = xattn_bwd(cq_raw, ckv, vec('g_cq') + t_co, vec('g_ck'), do_x)
    dhq = _mm('bwd_cq_x', dcq_raw, w_cq_f, 'nt', F32)
    t_cq = reduce_begin('w_cq', _mm('bwd_cq_w', hq, dcq_raw, 'tn', BF16).reshape(N_CHIPS, D // N_CHIPS, XW))
    dmn = _mm('bwd_ckv_x', dckv, w_ckv_f, 'nt', F32)
    t_ckv = reduce_begin('w_ckv', _mm('bwd_ckv_w', mn, dckv, 'tn', BF16).reshape(N_CHIPS, D // N_CHIPS, 2 * XW))
    (gw['g_mem'],) = norm_bwd('norm_mem_bwd', ms, vec('g_mem'), dmn, want_dx=False)
    dx1, dx1b, gw['g_xattn'] = norm_bwd('norm_xq_bwd', x1, vec('g_xattn') + t_cq + t_ckv, dhq, res=dx2)

    dmix = _mm('bwd_out_x', dx1b, w_out_f, 'nt', F32)
    t_out = reduce_begin('w_out', _mm('bwd_out_w', mixn, dx1b, 'tn', BF16).reshape(N_CHIPS, 2 * FW // N_CHIPS, D))
    do_fox, delta, dy_lru, gw['g_fox_out'], gw['g_lru_out'] = mix_bwd(o_fox, y_lru, vec('g_fox_out') + t_out,
                                                                     vec('g_lru_out'), dmix)
    (du, dgate, gw['conv_w'], gw['conv_b'], gw['w_ra'], gw['b_ra'], gw['w_ri'], gw['b_ri'],
     gw['lam']) = lru_bwd(proj, dy_lru, *lru_w, u_off, g_off)
    early = [n for n in SMALL if n not in ('g_q', 'g_k', 'b_f', 'g_mix')]
    late = [n for n in SMALL if n not in early]
    e_ssem, e_rsem, e_buf, e_zone, e_tok = spread_start('spread_start_early', _pack([gw[n] for n in early]))
    dqn, delta2 = fox_bwd_q(qn, kn, vb, do_fox, cc, ct, lse, delta, T)
    dkn, dv, dct = fox_bwd_kv(qn, kn, vb, do_fox, cc, ct, lse, delta2, T)
    dq, dk, gw['g_q'], gw['g_k'] = qkv_bwd(proj, vec('g_q') + e_tok[0:1, 0:1], vec('g_k'), dqn, dkn, FW)
    dc = jnp.pad(dct.reshape(H, S).T, ((0, 0), (0, LANES - H)))
    df, db_f = fgate_bwd(f_raw, b_f_pad, dc, H)
    gw['b_f'] = db_f[:, :H]
    dproj = jnp.concatenate([dq, dk, dv, du, dgate], axis=1)
    dw5 = _mm('bwd_in_w', h1, dproj, 'tn', BF16)
    dwf = _mm('bwd_f_w', h1, df, 'tn', BF16)
    t_in = reduce_begin('w_in', wins.windows(dw5, dwf))
    dh_a = _mm('bwd_f_x', df, wf, 'nt', F32)
    dh1 = _mm('bwd_in_x', dproj, w5, 'nt', F32, res=dh_a)
    grad_x, _, gw['g_mix'] = norm_bwd('norm_mix_bwd', xs, vec('g_mix') + t_in, dh1, res=dx1)
    l_ssem, l_rsem, l_buf, l_zone, _ = spread_start('spread_start_late',
                                                    _pack([gw[n] for n in late] + [loss_blk[0:1, 0:1]]))

    grads, delta_w, new_m, new_v = {}, {}, {}, {}
    done = grad_x
    for n, part, zone, ssem, rsem in pending:
        part, landed = scatter_wait('scatter_wait_' + n, part, zone, ssem, rsem, done)
        mine = sum_parts('sum_parts_' + n, part, landed, chip_idx, c_idx)
        (other,) = pair_join('pair_join_' + n, [mine])
        if n == 'w_in':
            mine, other = wins.from_window(mine, chip), wins.from_window(other, chip)
        grads[n], delta_w[n], new_m[n], new_v[n] = adamw_halves('adamw_' + n, W[n], mine, other, M1[n], V1[n], c_idx)
        done = delta_w[n]

    device = 4 * lax.axis_index('x') + 2 * lax.axis_index('y') + lax.axis_index('c')
    summed = {}
    for tag, names, buf, zone, ssem, rsem in (('early', early, e_buf, e_zone, e_ssem, e_rsem),
                                              ('late', late + ['loss'], l_buf, l_zone, l_ssem, l_rsem)):
        buf, zone = spread_wait('spread_wait_' + tag, buf, zone, ssem, rsem, done)
        total = sum_devices('sum_small_' + tag, lax.dynamic_update_index_in_dim(zone, buf, device, 0))
        summed.update(zip(names, _unpack(total, [gw[n].shape if n != 'loss' else (1, 1) for n in names])))
    loss = summed['loss'].reshape(())
    for n in SMALL:
        g = summed[n]
        grads[n] = g.reshape(W[n].shape) if n != 'conv_w' else lax.dynamic_slice_in_dim(
            g, chip * (LW // N_CHIPS), LW // N_CHIPS, axis=1)
    packs = [_pack([d[n] for n in SMALL]) for d in (W, grads, M1, V1)]
    shapes = [W[n].shape for n in SMALL]
    for d, res in zip((delta_w, new_m, new_v), adamw('adamw_small', *packs)):
        d.update(zip(SMALL, _unpack(res, shapes)))

    lead = lambda d: [d[n][None] for n in WEIGHTS]
    return (loss, grad_x[None], *lead(grads), *lead(delta_w), *lead(new_m), *lead(new_v))
```
